```python
import jax, jax.numpy as jnp
from jax import lax
import numpy as np

D_MODEL = 1024
BATCH = 8
SEQ = 4096
DEPTH = 2

N_MIXERS = 2
N_POOL_GROUPS = 4
POOL_GROUP = D_MODEL // N_POOL_GROUPS
POOL_WINDOWS = (2, 4, 8, 16)
N_HEADS = 16
QK_NOPE = 64
QK_ROPE = 32
V_HEAD = 64
Q_LORA = D_MODEL // 4
KV_LORA = D_MODEL // 8
ROPE_THETA = 10000.0
D_FF = 11 * D_MODEL // 4
Q_BLOCK = 128
EPS = 1e-6
N_MOD = 9
N_POOL_LAYERS = (DEPTH + 1) // 2
N_MLA_LAYERS = DEPTH // 2
ATTN_SCALE = (QK_NOPE + QK_ROPE) ** -0.5

kernel_name = "hybrid_pool_mla_macaron_encoder"


def rmsnorm(x, g):
    xf = x.astype(jnp.float32)
    y = xf * lax.rsqrt(jnp.mean(xf * xf, axis=-1, keepdims=True) + EPS)
    return (y * g.astype(jnp.float32)).astype(x.dtype)


def swiglu(h, w_in, w_out):
    gate, up = jnp.split(h @ w_in, 2, axis=-1)
    return (jax.nn.silu(gate) * up) @ w_out


def centred_mean(x, window):
    s = x.shape[1]
    cs = lax.cumsum(x.astype(jnp.float32), axis=1)
    cs = jnp.pad(cs, ((0, 0), (1, 0), (0, 0)))
    t = jnp.arange(s)
    hi = jnp.clip(t + window // 2, 0, s)
    lo = jnp.clip(t - window // 2, 0, s)
    tot = jnp.take(cs, hi, axis=1) - jnp.take(cs, lo, axis=1)
    cnt = (hi - lo).astype(jnp.float32)[None, :, None]
    return (tot / cnt).astype(x.dtype)


def pool_mixer(h, w, b, scale):
    B, S, _ = h.shape
    hg = h.reshape(B, S, N_POOL_GROUPS, POOL_GROUP)
    pooled = jnp.stack([centred_mean(hg[:, :, g], POOL_WINDOWS[g]) for g in range(N_POOL_GROUPS)], axis=2)
    y = jnp.einsum('bsgc,gcd->bsgd', pooled - hg, w) + b
    return y.reshape(B, S, D_MODEL) * scale


def rope_tables(s, dtype):
    inv = 1.0 / (ROPE_THETA ** (jnp.arange(0, QK_ROPE, 2, dtype=jnp.float32) / QK_ROPE))
    ang = jnp.arange(s, dtype=jnp.float32)[:, None] * inv[None, :]
    return jnp.cos(ang).astype(dtype), jnp.sin(ang).astype(dtype)


def apply_rope(x, cos, sin):
    x1, x2 = jnp.split(x, 2, axis=-1)
    return jnp.concatenate([x1 * cos - x2 * sin, x2 * cos + x1 * sin], axis=-1)


def mla_mixer(h, w_in, q_norm, kv_norm, w_uq, w_uk, w_uv, w_o, cos, sin):
    B, S, _ = h.shape
    lat = h @ w_in
    c_q, c_kv, k_r = jnp.split(lat, [Q_LORA, Q_LORA + KV_LORA], axis=-1)
    c_q = rmsnorm(c_q, q_norm)
    c_kv = rmsnorm(c_kv, kv_norm)
    q = jnp.einsum('bsc,chd->bshd', c_q, w_uq)
    q_nope, q_rope = q[..., :QK_NOPE], q[..., QK_NOPE:]
    q_rope = apply_rope(q_rope, cos[:, None, :], sin[:, None, :]) * ATTN_SCALE
    k_rope = apply_rope(k_r, cos, sin)
    q_lat = jnp.einsum('bshn,chn->bshc', q_nope, w_uk) * ATTN_SCALE
    nb = S // Q_BLOCK
    qlb = q_lat.reshape(B, nb, Q_BLOCK, N_HEADS, KV_LORA).transpose(1, 0, 2, 3, 4)
    qrb = q_rope.reshape(B, nb, Q_BLOCK, N_HEADS, QK_ROPE).transpose(1, 0, 2, 3, 4)

    def block(args):
        ql, qr = args
        s = (jnp.einsum('bqhc,bkc->bhqk', ql, c_kv)
             + jnp.einsum('bqhr,bkr->bhqk', qr, k_rope))
        p = jax.nn.softmax(s.astype(jnp.float32), axis=-1).astype(c_kv.dtype)
        return jnp.einsum('bhqk,bkc->bqhc', p, c_kv)

    o_lat = lax.map(block, (qlb, qrb))
    o_lat = o_lat.transpose(1, 0, 2, 3, 4).reshape(B, S, N_HEADS, KV_LORA)
    o = jnp.einsum('bshc,chv->bshv', o_lat, w_uv)
    return o.reshape(B, S, N_HEADS * V_HEAD) @ w_o


def modulated_sublayer(x, mod, g_pre, g_post, fn, weight):
    shift, scale, gate = mod[:, 0], mod[:, 1], mod[:, 2]
    h = rmsnorm(x, g_pre) * (1.0 + scale) + shift
    y = rmsnorm(fn(h), g_post)
    return x + weight * (1.0 + gate) * y


def _fwd_setup_inputs(seed: int = 0) -> dict:
    key = jax.random.key(seed)
    ks = jax.random.split(key, 20)
    n = jax.random.normal
    f32 = jnp.float32
    return {
        "x": n(ks[0], (BATCH, SEQ, D_MODEL), f32),
        "c": n(ks[1], (BATCH, D_MODEL), f32),
        "ada_w": n(ks[2], (DEPTH, D_MODEL, N_MOD * D_MODEL), f32) * (0.5 * D_MODEL ** -0.5),
        "ada_b": n(ks[3], (DEPTH, N_MOD * D_MODEL), f32) * 0.01,
        "norm_g": 1.0 + 0.05 * n(ks[4], (DEPTH, 6, D_MODEL), f32),
        "ffn_w_in": n(ks[5], (DEPTH, 2, D_MODEL, 2 * D_FF), f32) * D_MODEL ** -0.5,
        "ffn_w_out": n(ks[6], (DEPTH, 2, D_FF, D_MODEL), f32) * D_FF ** -0.5,
        "pool_w": n(ks[7], (N_POOL_LAYERS, N_POOL_GROUPS, POOL_GROUP, POOL_GROUP), f32) * POOL_GROUP ** -0.5,
        "pool_b": n(ks[8], (N_POOL_LAYERS, N_POOL_GROUPS, POOL_GROUP), f32) * 0.01,
        "pool_scale": 1.0 + 0.05 * n(ks[9], (N_POOL_LAYERS, D_MODEL), f32),
        "mla_w_in": n(ks[10], (N_MLA_LAYERS, D_MODEL, Q_LORA + KV_LORA + QK_ROPE), f32) * D_MODEL ** -0.5,
        "mla_q_norm": 1.0 + 0.05 * n(ks[11], (N_MLA_LAYERS, Q_LORA), f32),
        "mla_kv_norm": 1.0 + 0.05 * n(ks[12], (N_MLA_LAYERS, KV_LORA), f32),
        "mla_w_uq": n(ks[13], (N_MLA_LAYERS, Q_LORA, N_HEADS, QK_NOPE + QK_ROPE), f32) * Q_LORA ** -0.5,
        "mla_w_uk": n(ks[14], (N_MLA_LAYERS, KV_LORA, N_HEADS, QK_NOPE), f32) * KV_LORA ** -0.5,
        "mla_w_uv": n(ks[15], (N_MLA_LAYERS, KV_LORA, N_HEADS, V_HEAD), f32) * KV_LORA ** -0.5,
        "mla_w_o": n(ks[16], (N_MLA_LAYERS, N_HEADS * V_HEAD, D_MODEL), f32) * (N_HEADS * V_HEAD) ** -0.5,
    }


def _fwd_reference(x, c, ada_w, ada_b, norm_g, ffn_w_in, ffn_w_out, pool_w, pool_b, pool_scale,
              mla_w_in, mla_q_norm, mla_kv_norm, mla_w_uq, mla_w_uk, mla_w_uv, mla_w_o):
    B = x.shape[0]
    cos, sin = rope_tables(x.shape[1], x.dtype)
    sc = jax.nn.silu(c)
    for i in range(DEPTH):
        mod = (sc @ ada_w[i] + ada_b[i]).reshape(B, N_MOD, D_MODEL)[:, :, None, :]
        g = norm_g[i]
        x = modulated_sublayer(x, mod[:, 0:3], g[0], g[1],
                               lambda h: swiglu(h, ffn_w_in[i, 0], ffn_w_out[i, 0]), 0.5)
        if i % N_MIXERS == 0:
            li = i // N_MIXERS
            mixer = lambda h: pool_mixer(h, pool_w[li], pool_b[li], pool_scale[li])
        else:
            li = i // N_MIXERS
            mixer = lambda h: mla_mixer(h, mla_w_in[li], mla_q_norm[li], mla_kv_norm[li],
                                        mla_w_uq[li], mla_w_uk[li], mla_w_uv[li], mla_w_o[li],
                                        cos, sin)
        x = modulated_sublayer(x, mod[:, 3:6], g[2], g[3], mixer, 1.0)
        x = modulated_sublayer(x, mod[:, 6:9], g[4], g[5],
                               lambda h: swiglu(h, ffn_w_in[i, 1], ffn_w_out[i, 1]), 0.5)
    return x


import jax as _jax
import jax.numpy as _jnp

TWIN_FORMAT = 'train_step'
FWD_PARAMS = ['x', 'c', 'ada_w', 'ada_b', 'norm_g', 'ffn_w_in', 'ffn_w_out', 'pool_w', 'pool_b', 'pool_scale', 'mla_w_in', 'mla_q_norm', 'mla_kv_norm', 'mla_w_uq', 'mla_w_uk', 'mla_w_uv', 'mla_w_o']
TWIN_WEIGHTS = ['ada_w', 'ada_b', 'norm_g', 'ffn_w_in', 'ffn_w_out', 'pool_w', 'pool_b', 'pool_scale', 'mla_w_in', 'mla_q_norm', 'mla_kv_norm', 'mla_w_uq', 'mla_w_uk', 'mla_w_uv', 'mla_w_o']
TWIN_DIFF_INPUT = 'x'
TWIN_INPUTS = ['x', 'c', 'ada_w', 'ada_b', 'norm_g', 'ffn_w_in', 'ffn_w_out', 'pool_w', 'pool_b', 'pool_scale', 'mla_w_in', 'mla_q_norm', 'mla_kv_norm', 'mla_w_uq', 'mla_w_uk', 'mla_w_uv', 'mla_w_o', 'loss_target', 'm_ada_w', 'm_ada_b', 'm_norm_g', 'm_ffn_w_in', 'm_ffn_w_out', 'm_pool_w', 'm_pool_b', 'm_pool_scale', 'm_mla_w_in', 'm_mla_q_norm', 'm_mla_kv_norm', 'm_mla_w_uq', 'm_mla_w_uk', 'm_mla_w_uv', 'm_mla_w_o', 'v_ada_w', 'v_ada_b', 'v_norm_g', 'v_ffn_w_in', 'v_ffn_w_out', 'v_pool_w', 'v_pool_b', 'v_pool_scale', 'v_mla_w_in', 'v_mla_q_norm', 'v_mla_kv_norm', 'v_mla_w_uq', 'v_mla_w_uk', 'v_mla_w_uv', 'v_mla_w_o']
TWIN_OUTPUTS = ['loss', 'grad_x', 'grad_ada_w', 'grad_ada_b', 'grad_norm_g', 'grad_ffn_w_in', 'grad_ffn_w_out', 'grad_pool_w', 'grad_pool_b', 'grad_pool_scale', 'grad_mla_w_in', 'grad_mla_q_norm', 'grad_mla_kv_norm', 'grad_mla_w_uq', 'grad_mla_w_uk', 'grad_mla_w_uv', 'grad_mla_w_o', 'delta_ada_w', 'delta_ada_b', 'delta_norm_g', 'delta_ffn_w_in', 'delta_ffn_w_out', 'delta_pool_w', 'delta_pool_b', 'delta_pool_scale', 'delta_mla_w_in', 'delta_mla_q_norm', 'delta_mla_kv_norm', 'delta_mla_w_uq', 'delta_mla_w_uk', 'delta_mla_w_uv', 'delta_mla_w_o', 'new_m_ada_w', 'new_m_ada_b', 'new_m_norm_g', 'new_m_ffn_w_in', 'new_m_ffn_w_out', 'new_m_pool_w', 'new_m_pool_b', 'new_m_pool_scale', 'new_m_mla_w_in', 'new_m_mla_q_norm', 'new_m_mla_kv_norm', 'new_m_mla_w_uq', 'new_m_mla_w_uk', 'new_m_mla_w_uv', 'new_m_mla_w_o', 'new_v_ada_w', 'new_v_ada_b', 'new_v_norm_g', 'new_v_ffn_w_in', 'new_v_ffn_w_out', 'new_v_pool_w', 'new_v_pool_b', 'new_v_pool_scale', 'new_v_mla_w_in', 'new_v_mla_q_norm', 'new_v_mla_kv_norm', 'new_v_mla_w_uq', 'new_v_mla_w_uk', 'new_v_mla_w_uv', 'new_v_mla_w_o']
TWIN_LEAF_KINDS = {'loss': 'loss', 'grad_x': 'grad_x', 'grad_ada_w': 'grad_w', 'grad_ada_b': 'grad_w', 'grad_norm_g': 'grad_w', 'grad_ffn_w_in': 'grad_w', 'grad_ffn_w_out': 'grad_w', 'grad_pool_w': 'grad_w', 'grad_pool_b': 'grad_w', 'grad_pool_scale': 'grad_w', 'grad_mla_w_in': 'grad_w', 'grad_mla_q_norm': 'grad_w', 'grad_mla_kv_norm': 'grad_w', 'grad_mla_w_uq': 'grad_w', 'grad_mla_w_uk': 'grad_w', 'grad_mla_w_uv': 'grad_w', 'grad_mla_w_o': 'grad_w', 'delta_ada_w': 'delta_w', 'delta_ada_b': 'delta_w', 'delta_norm_g': 'delta_w', 'delta_ffn_w_in': 'delta_w', 'delta_ffn_w_out': 'delta_w', 'delta_pool_w': 'delta_w', 'delta_pool_b': 'delta_w', 'delta_pool_scale': 'delta_w', 'delta_mla_w_in': 'delta_w', 'delta_mla_q_norm': 'delta_w', 'delta_mla_kv_norm': 'delta_w', 'delta_mla_w_uq': 'delta_w', 'delta_mla_w_uk': 'delta_w', 'delta_mla_w_uv': 'delta_w', 'delta_mla_w_o': 'delta_w', 'new_m_ada_w': 'new_m', 'new_m_ada_b': 'new_m', 'new_m_norm_g': 'new_m', 'new_m_ffn_w_in': 'new_m', 'new_m_ffn_w_out': 'new_m', 'new_m_pool_w': 'new_m', 'new_m_pool_b': 'new_m', 'new_m_pool_scale': 'new_m', 'new_m_mla_w_in': 'new_m', 'new_m_mla_q_norm': 'new_m', 'new_m_mla_kv_norm': 'new_m', 'new_m_mla_w_uq': 'new_m', 'new_m_mla_w_uk': 'new_m', 'new_m_mla_w_uv': 'new_m', 'new_m_mla_w_o': 'new_m', 'new_v_ada_w': 'new_v', 'new_v_ada_b': 'new_v', 'new_v_norm_g': 'new_v', 'new_v_ffn_w_in': 'new_v', 'new_v_ffn_w_out': 'new_v', 'new_v_pool_w': 'new_v', 'new_v_pool_b': 'new_v', 'new_v_pool_scale': 'new_v', 'new_v_mla_w_in': 'new_v', 'new_v_mla_q_norm': 'new_v', 'new_v_mla_kv_norm': 'new_v', 'new_v_mla_w_uq': 'new_v', 'new_v_mla_w_uk': 'new_v', 'new_v_mla_w_uv': 'new_v', 'new_v_mla_w_o': 'new_v'}


def _forward(args):
    return _fwd_reference(*[args[k] for k in FWD_PARAMS])


def _output_shape():
    out = _jax.eval_shape(lambda: _forward(_fwd_setup_inputs(0)))
    return out.shape, out.dtype

N_MICROBATCH = 1
ADAM_LR = 0.001
ADAM_B1 = 0.9
ADAM_B2 = 0.999
ADAM_EPS = 1e-08
ADAM_WD = 0.01
ADAM_STEP = 10
PER_EXAMPLE_BATCH_AXIS = {'x': 0, 'c': 0, 'loss_target': 0}
SHARED_INPUTS = []
_WEIGHT_DTYPES = {'ada_w': _jnp.float32, 'ada_b': _jnp.float32, 'norm_g': _jnp.float32, 'ffn_w_in': _jnp.float32, 'ffn_w_out': _jnp.float32, 'pool_w': _jnp.float32, 'pool_b': _jnp.float32, 'pool_scale': _jnp.float32, 'mla_w_in': _jnp.float32, 'mla_q_norm': _jnp.float32, 'mla_kv_norm': _jnp.float32, 'mla_w_uq': _jnp.float32, 'mla_w_uk': _jnp.float32, 'mla_w_uv': _jnp.float32, 'mla_w_o': _jnp.float32}
MOMENT_SCALE = {'ada_w': 7.594844e+00, 'ada_b': 1.567260e+01, 'norm_g': 1.822600e+01, 'ffn_w_in': 9.612163e-01, 'ffn_w_out': 1.921135e+00, 'pool_w': 1.716628e+00, 'pool_b': 3.311775e+01, 'pool_scale': 1.237265e+01, 'mla_w_in': 1.646176e+01, 'mla_q_norm': 1.257268e+00, 'mla_kv_norm': 3.325036e+01, 'mla_w_uq': 5.655244e-01, 'mla_w_uk': 7.463165e-01, 'mla_w_uv': 1.021475e+01, 'mla_w_o': 1.042553e+01}


def _to_microbatches(a, axis):
    t = _jnp.moveaxis(a, axis, 0)
    t = t.reshape((N_MICROBATCH, t.shape[0] // N_MICROBATCH) + t.shape[1:])
    return _jnp.moveaxis(t, 1, axis + 1)


def setup_inputs(seed: int = 0) -> dict:
    inp = _fwd_setup_inputs(seed)
    key = _jax.random.fold_in(_jax.random.key(seed), 7919)
    shape, _ = _output_shape()
    out = dict(inp)
    out["loss_target"] = _jax.random.normal(_jax.random.fold_in(key, 0), shape, _jnp.float32)
    for i, name in enumerate(TWIN_WEIGHTS):
        w = inp[name].astype(_jnp.float32)
        if MOMENT_SCALE is None:
            s = _jnp.sqrt(_jnp.mean(_jnp.square(w)) + 1e-30)
        else:
            s = MOMENT_SCALE[name]
        km, kv = _jax.random.split(_jax.random.fold_in(key, i + 1))
        out[name] = w
        out["m_" + name] = s * _jax.random.normal(km, w.shape, _jnp.float32)
        out["v_" + name] = (s * s) * _jax.random.uniform(kv, w.shape, _jnp.float32, 0.5, 1.5)
    if N_MICROBATCH > 1:
        for name, axis in PER_EXAMPLE_BATCH_AXIS.items():
            out[name] = _to_microbatches(out[name], axis)
    return {'x': out['x'], 'c': out['c'], 'ada_w': out['ada_w'], 'ada_b': out['ada_b'], 'norm_g': out['norm_g'], 'ffn_w_in': out['ffn_w_in'], 'ffn_w_out': out['ffn_w_out'], 'pool_w': out['pool_w'], 'pool_b': out['pool_b'], 'pool_scale': out['pool_scale'], 'mla_w_in': out['mla_w_in'], 'mla_q_norm': out['mla_q_norm'], 'mla_kv_norm': out['mla_kv_norm'], 'mla_w_uq': out['mla_w_uq'], 'mla_w_uk': out['mla_w_uk'], 'mla_w_uv': out['mla_w_uv'], 'mla_w_o': out['mla_w_o'], 'loss_target': out['loss_target'], 'm_ada_w': out['m_ada_w'], 'm_ada_b': out['m_ada_b'], 'm_norm_g': out['m_norm_g'], 'm_ffn_w_in': out['m_ffn_w_in'], 'm_ffn_w_out': out['m_ffn_w_out'], 'm_pool_w': out['m_pool_w'], 'm_pool_b': out['m_pool_b'], 'm_pool_scale': out['m_pool_scale'], 'm_mla_w_in': out['m_mla_w_in'], 'm_mla_q_norm': out['m_mla_q_norm'], 'm_mla_kv_norm': out['m_mla_kv_norm'], 'm_mla_w_uq': out['m_mla_w_uq'], 'm_mla_w_uk': out['m_mla_w_uk'], 'm_mla_w_uv': out['m_mla_w_uv'], 'm_mla_w_o': out['m_mla_w_o'], 'v_ada_w': out['v_ada_w'], 'v_ada_b': out['v_ada_b'], 'v_norm_g': out['v_norm_g'], 'v_ffn_w_in': out['v_ffn_w_in'], 'v_ffn_w_out': out['v_ffn_w_out'], 'v_pool_w': out['v_pool_w'], 'v_pool_b': out['v_pool_b'], 'v_pool_scale': out['v_pool_scale'], 'v_mla_w_in': out['v_mla_w_in'], 'v_mla_q_norm': out['v_mla_q_norm'], 'v_mla_kv_norm': out['v_mla_kv_norm'], 'v_mla_w_uq': out['v_mla_w_uq'], 'v_mla_w_uk': out['v_mla_w_uk'], 'v_mla_w_uv': out['v_mla_w_uv'], 'v_mla_w_o': out['v_mla_w_o']}


def _loss(weights, diff, rest, loss_target):
    with _jax.named_scope("forward"):
        args = {**rest, TWIN_DIFF_INPUT: diff, **{k: w.astype(_WEIGHT_DTYPES[k]) for k, w in weights.items()}}
        y = _forward(args)
    with _jax.named_scope("loss_head"):
        err = _jnp.square(y.astype(_jnp.float32) - loss_target)
        return 0.5 * _jnp.sum(_jnp.mean(err, axis=-1)) if err.ndim else 0.5 * err


def _adamw(w, g, m, v):
    m = ADAM_B1 * m + (1.0 - ADAM_B1) * g
    v = ADAM_B2 * v + (1.0 - ADAM_B2) * _jnp.square(g)
    m_hat = m / (1.0 - ADAM_B1 ** ADAM_STEP)
    v_hat = v / (1.0 - ADAM_B2 ** ADAM_STEP)
    delta = -ADAM_LR * (m_hat / (_jnp.sqrt(v_hat) + ADAM_EPS) + ADAM_WD * w)
    return delta, m, v


def reference(x, c, ada_w, ada_b, norm_g, ffn_w_in, ffn_w_out, pool_w, pool_b, pool_scale, mla_w_in, mla_q_norm, mla_kv_norm, mla_w_uq, mla_w_uk, mla_w_uv, mla_w_o, loss_target, m_ada_w, m_ada_b, m_norm_g, m_ffn_w_in, m_ffn_w_out, m_pool_w, m_pool_b, m_pool_scale, m_mla_w_in, m_mla_q_norm, m_mla_kv_norm, m_mla_w_uq, m_mla_w_uk, m_mla_w_uv, m_mla_w_o, v_ada_w, v_ada_b, v_norm_g, v_ffn_w_in, v_ffn_w_out, v_pool_w, v_pool_b, v_pool_scale, v_mla_w_in, v_mla_q_norm, v_mla_kv_norm, v_mla_w_uq, v_mla_w_uk, v_mla_w_uv, v_mla_w_o):
    given = dict(x=x, c=c, ada_w=ada_w, ada_b=ada_b, norm_g=norm_g, ffn_w_in=ffn_w_in, ffn_w_out=ffn_w_out, pool_w=pool_w, pool_b=pool_b, pool_scale=pool_scale, mla_w_in=mla_w_in, mla_q_norm=mla_q_norm, mla_kv_norm=mla_kv_norm, mla_w_uq=mla_w_uq, mla_w_uk=mla_w_uk, mla_w_uv=mla_w_uv, mla_w_o=mla_w_o, loss_target=loss_target, m_ada_w=m_ada_w, m_ada_b=m_ada_b, m_norm_g=m_norm_g, m_ffn_w_in=m_ffn_w_in, m_ffn_w_out=m_ffn_w_out, m_pool_w=m_pool_w, m_pool_b=m_pool_b, m_pool_scale=m_pool_scale, m_mla_w_in=m_mla_w_in, m_mla_q_norm=m_mla_q_norm, m_mla_kv_norm=m_mla_kv_norm, m_mla_w_uq=m_mla_w_uq, m_mla_w_uk=m_mla_w_uk, m_mla_w_uv=m_mla_w_uv, m_mla_w_o=m_mla_w_o, v_ada_w=v_ada_w, v_ada_b=v_ada_b, v_norm_g=v_norm_g, v_ffn_w_in=v_ffn_w_in, v_ffn_w_out=v_ffn_w_out, v_pool_w=v_pool_w, v_pool_b=v_pool_b, v_pool_scale=v_pool_scale, v_mla_w_in=v_mla_w_in, v_mla_q_norm=v_mla_q_norm, v_mla_kv_norm=v_mla_kv_norm, v_mla_w_uq=v_mla_w_uq, v_mla_w_uk=v_mla_w_uk, v_mla_w_uv=v_mla_w_uv, v_mla_w_o=v_mla_w_o)
    weights = {n: given[n] for n in TWIN_WEIGHTS}
    shared = {n: given[n] for n in SHARED_INPUTS}
    per_example = {n: given[n] for n in ['x', 'c']}
    grad_fn = _jax.value_and_grad(_loss, argnums=(0, 1))

    def one_microbatch(ex, loss_target):
        ex = dict(ex)
        diff = ex.pop(TWIN_DIFF_INPUT)
        return grad_fn(weights, diff, {**shared, **ex}, loss_target)

    if N_MICROBATCH == 1:
        loss, (grad_w, grad_x) = one_microbatch(per_example, given["loss_target"])
    else:
        def body(carry, xs):
            loss_sum, grad_sum = carry
            l_k, (gw_k, gx_k) = one_microbatch(xs[0], xs[1])
            with _jax.named_scope("update"):
                return (loss_sum + l_k, _jax.tree.map(_jnp.add, grad_sum, gw_k)), gx_k

        init = (_jnp.zeros((), _jnp.float32), _jax.tree.map(_jnp.zeros_like, weights))
        (loss, grad_w), grad_x = _jax.lax.scan(body, init, (per_example, given["loss_target"]))
    with _jax.named_scope("update"):
        delta_w, new_m, new_v = {}, {}, {}
        for n in TWIN_WEIGHTS:
            delta_w[n], new_m[n], new_v[n] = _adamw(weights[n], grad_w[n], given["m_" + n], given["v_" + n])
    return (loss, grad_x, *[grad_w[n] for n in TWIN_WEIGHTS], *[delta_w[n] for n in TWIN_WEIGHTS],
            *[new_m[n] for n in TWIN_WEIGHTS], *[new_v[n] for n in TWIN_WEIGHTS])
```

```python
import functools

import jax
import jax.numpy as jnp
from jax import lax
from jax.experimental import pallas as pl
from jax.experimental.pallas import tpu as pltpu

F32 = jnp.float32
BF16 = jnp.bfloat16

D = 1024
DFF = 2816
FSH = 1408
N_CHIP = 4
N_DEV = 8
N_HEADS = 16
NOPE = 64
ROPE = 32
VH = 64
QL = 256
KVL = 128
QPAD = 256
EPS = 1e-6
ATTN_SCALE = (NOPE + ROPE) ** -0.5
ROPE_THETA = 10000.0
POOL_WINDOWS = (2, 4, 8, 16)
HALO = 8

ADAM_LR, ADAM_B1, ADAM_B2, ADAM_EPS, ADAM_WD, ADAM_STEP = 0.001, 0.9, 0.999, 1e-08, 0.01, 10

VMEM_LIMIT = 60 * 1024 * 1024
MESH = pl.DeviceIdType.MESH

NT = (((1,), (1,)), ((), ()))
TN = (((0,), (0,)), ((), ()))


def _params(*sem):
    return pltpu.CompilerParams(dimension_semantics=sem, vmem_limit_bytes=VMEM_LIMIT)


def _dot(a, b, dims=None):
    if dims is None:
        return jnp.dot(a, b, preferred_element_type=F32)
    return lax.dot_general(a, b, dims, preferred_element_type=F32)


def _rms(x):
    r = lax.rsqrt(jnp.mean(x * x, axis=-1, keepdims=True) + EPS)
    return x * r, r


def _rms_bwd(xhat, r, dxhat):
    return r * (dxhat - xhat * jnp.mean(dxhat * xhat, axis=-1, keepdims=True))


def _prenorm(x, vec_ref):
    xhat, r = _rms(x)
    h = xhat * vec_ref[0:1, :] * (1.0 + vec_ref[3:4, :]) + vec_ref[2:3, :]
    return h, xhat, r


def _postnorm_bwd(dout, u, vec_ref, weight):
    uhat, r = _rms(u)
    gt = weight * (1.0 + vec_ref[4:5, :])
    dy = dout * gt
    dgate_rows = (weight * dout) * (uhat * vec_ref[1:2, :])
    dgpost_rows = dy * uhat
    du = _rms_bwd(uhat, r, dy * vec_ref[1:2, :])
    return du, dgate_rows, dgpost_rows


def _prenorm_bwd(dh, x, vec_ref, vg_ref):
    xhat, r = _rms(x)
    sc1 = 1.0 + vec_ref[3:4, :]
    g = vec_ref[0:1, :]
    vg_ref[0:1, :] += jnp.sum(dh, axis=0, keepdims=True)
    vg_ref[1:2, :] += jnp.sum(dh * (xhat * g), axis=0, keepdims=True)
    vg_ref[3:4, :] += jnp.sum(dh * sc1 * xhat, axis=0, keepdims=True)
    return _rms_bwd(xhat, r, dh * g * sc1)


def ffn_fwd(x, vec, w_in, w_out, weight):
    S = x.shape[0]
    tm = min(256, S)

    def body(x_ref, vec_ref, wg_ref, wu_ref, wo_ref, xo_ref, a_ref, u_ref, h_ref, acc_ref):
        j = pl.program_id(1)

        @pl.when(j == 0)
        def _():
            h, _, _ = _prenorm(x_ref[...], vec_ref)
            h_ref[...] = h.astype(BF16)
            acc_ref[...] = jnp.zeros_like(acc_ref)

        hb = h_ref[...]
        g = _dot(hb, wg_ref[...])
        up = _dot(hb, wu_ref[...])
        a_ref[0] = g.astype(BF16)
        a_ref[1] = up.astype(BF16)
        act = (g * jax.nn.sigmoid(g)) * up
        acc_ref[...] += _dot(act.astype(BF16), wo_ref[...])

        @pl.when(j == 1)
        def _():
            u = acc_ref[...]
            u_ref[...] = u
            uhat, _ = _rms(u)
            xo_ref[...] = x_ref[...] + (weight * (1.0 + vec_ref[4:5, :])) * (uhat * vec_ref[1:2, :])

    return pl.pallas_call(
        body, name="ffn_fwd", grid=(S // tm, 2),
        in_specs=[pl.BlockSpec((tm, D), lambda i, j: (i, 0)),
                  pl.BlockSpec((8, D), lambda i, j: (0, 0)),
                  pl.BlockSpec((None, D, FSH), lambda i, j: (j, 0, 0)),
                  pl.BlockSpec((None, D, FSH), lambda i, j: (j + 2, 0, 0)),
                  pl.BlockSpec((None, FSH, D), lambda i, j: (j, 0, 0))],
        out_specs=[pl.BlockSpec((tm, D), lambda i, j: (i, 0)),
                   pl.BlockSpec((2, tm, FSH), lambda i, j: (0, i, j)),
                   pl.BlockSpec((tm, D), lambda i, j: (i, 0)),
                   pl.BlockSpec((tm, D), lambda i, j: (i, 0))],
        out_shape=[jax.ShapeDtypeStruct((S, D), F32), jax.ShapeDtypeStruct((2, S, DFF), BF16),
                   jax.ShapeDtypeStruct((S, D), F32), jax.ShapeDtypeStruct((S, D), BF16)],
        scratch_shapes=[pltpu.VMEM((tm, D), F32)],
        compiler_params=_params("parallel", "arbitrary"),
    )(x, vec, w_in, w_in, w_out)


def ffn_bwd(dout, x, u, a, vec, w_in, w_out, weight):
    S = x.shape[0]
    tm = min(256, S)

    def body(do_ref, x_ref, u_ref, a_ref, vec_ref, wg_ref, wu_ref, wo_ref,
             dx_ref, du_ref, act_ref, da_ref, vg_ref, dh_ref):
        i, j = pl.program_id(0), pl.program_id(1)

        @pl.when((i == 0) & (j == 0))
        def _():
            vg_ref[...] = jnp.zeros_like(vg_ref)

        @pl.when(j == 0)
        def _():
            du, dgate_rows, dgpost_rows = _postnorm_bwd(do_ref[...], u_ref[...], vec_ref, weight)
            vg_ref[2:3, :] += jnp.sum(dgate_rows, axis=0, keepdims=True)
            vg_ref[4:5, :] += jnp.sum(dgpost_rows, axis=0, keepdims=True)
            du_ref[...] = du.astype(BF16)
            dh_ref[...] = jnp.zeros_like(dh_ref)

        dact = _dot(du_ref[...], wo_ref[...], NT)
        g = a_ref[0].astype(F32)
        up = a_ref[1].astype(F32)
        s = jax.nn.sigmoid(g)
        silu = g * s
        act_ref[...] = (silu * up).astype(BF16)
        dg = (dact * up * (s * (1.0 + g * (1.0 - s)))).astype(BF16)
        dup = (dact * silu).astype(BF16)
        da_ref[0] = dg
        da_ref[1] = dup
        dh_ref[...] += _dot(dg, wg_ref[...], NT) + _dot(dup, wu_ref[...], NT)

        @pl.when(j == 1)
        def _():
            dx_ref[...] = do_ref[...] + _prenorm_bwd(dh_ref[...], x_ref[...], vec_ref, vg_ref)

    row = lambda i, j: (i, 0)
    return pl.pallas_call(
        body, name="ffn_bwd", grid=(S // tm, 2),
        in_specs=[pl.BlockSpec((tm, D), row), pl.BlockSpec((tm, D), row), pl.BlockSpec((tm, D), row),
                  pl.BlockSpec((2, tm, FSH), lambda i, j: (0, i, j)),
                  pl.BlockSpec((8, D), lambda i, j: (0, 0)),
                  pl.BlockSpec((None, D, FSH), lambda i, j: (j, 0, 0)),
                  pl.BlockSpec((None, D, FSH), lambda i, j: (j + 2, 0, 0)),
                  pl.BlockSpec((None, FSH, D), lambda i, j: (j, 0, 0))],
        out_specs=[pl.BlockSpec((tm, D), row), pl.BlockSpec((tm, D), row),
                   pl.BlockSpec((tm, FSH), lambda i, j: (i, j)),
                   pl.BlockSpec((2, tm, FSH), lambda i, j: (0, i, j)),
                   pl.BlockSpec((8, D), lambda i, j: (0, 0))],
        out_shape=[jax.ShapeDtypeStruct((S, D), F32), jax.ShapeDtypeStruct((S, D), BF16),
                   jax.ShapeDtypeStruct((S, DFF), BF16), jax.ShapeDtypeStruct((2, S, DFF), BF16),
                   jax.ShapeDtypeStruct((8, D), F32)],
        scratch_shapes=[pltpu.VMEM((tm, D), F32)],
        compiler_params=_params("arbitrary", "arbitrary"),
    )(dout, x, u, a, vec, w_in, w_in, w_out)


def dw_matmul(name, a, b, a_spec, b_spec, out_shape, out_spec, grid):
    def body(a_ref, b_ref, o_ref):
        @pl.when(pl.program_id(len(grid) - 1) == 0)
        def _():
            o_ref[...] = jnp.zeros_like(o_ref)

        o_ref[...] += _dot(a_ref[...], b_ref[...], TN)

    return pl.pallas_call(
        body, name=name, grid=grid, in_specs=[a_spec, b_spec], out_specs=out_spec,
        out_shape=jax.ShapeDtypeStruct(out_shape, F32),
        compiler_params=_params(*(["parallel"] * (len(grid) - 1) + ["arbitrary"])),
    )(a, b)


def ffn_dw(h, da, act, du):
    S = h.shape[0]
    tk = min(512, S)
    dw_in = dw_matmul("ffn_dw_in", h, da,
                      pl.BlockSpec((tk, D), lambda n, k: (k, 0)),
                      pl.BlockSpec((None, tk, FSH), lambda n, k: (n // 2, k, n % 2)),
                      (N_CHIP, D, FSH), pl.BlockSpec((None, D, FSH), lambda n, k: (n, 0, 0)),
                      (N_CHIP, S // tk))
    dw_out = dw_matmul("ffn_dw_out", act, du,
                       pl.BlockSpec((tk, FSH), lambda n, k: (k, n)),
                       pl.BlockSpec((tk, D), lambda n, k: (k, 0)),
                       (DFF, D), pl.BlockSpec((FSH, D), lambda n, k: (n, 0)),
                       (2, S // tk))
    return dw_in, dw_out


def _halo_specs(tm, S):
    nb = tm // HALO
    last = S // HALO - 1
    return [pl.BlockSpec((HALO, D), lambda i: (jnp.maximum(i * nb - 1, 0), 0)),
            pl.BlockSpec((tm, D), lambda i: (i, 0)),
            pl.BlockSpec((HALO, D), lambda i: (jnp.minimum((i + 1) * nb, last), 0))]


def _shift_rows(v, k):
    return pltpu.roll(v, k % v.shape[0], 0)


def _window_sum(v, g, forward):
    acc = v + _shift_rows(v, 1 if forward else -1)
    for step in (1, 2, 4)[:g]:
        acc = _shift_rows(acc, step) + _shift_rows(acc, -step)
    return acc


def _pool_count(t, w, S):
    return jnp.maximum(jnp.minimum(t + w // 2, S) - jnp.maximum(t - w // 2, 0), 1).astype(F32)


def pool_fwd(x, vec, pw, pvec):
    S = x.shape[0]
    tm = min(256, S)
    G = D // 4

    def body(xp_ref, x_ref, xn_ref, vec_ref, pw_ref, pv_ref, xo_ref, y_ref, z_ref):
        i = pl.program_id(0)
        xa = jnp.concatenate([xp_ref[...], x_ref[...], xn_ref[...]], axis=0)
        t = i * tm - HALO + lax.broadcasted_iota(jnp.int32, (tm + 2 * HALO, 1), 0)
        h, _, _ = _prenorm(xa, vec_ref)
        h = jnp.where((t >= 0) & (t < S), h, 0.0)
        tmain = t[HALO:HALO + tm]
        for g in range(4):
            hg = h[:, g * G:(g + 1) * G]
            pooled = _window_sum(hg, g, True)[HALO:HALO + tm] / _pool_count(tmain, POOL_WINDOWS[g], S)
            z = (pooled - hg[HALO:HALO + tm]).astype(BF16)
            z_ref[:, g * G:(g + 1) * G] = z
            y_ref[:, g * G:(g + 1) * G] = _dot(z, pw_ref[g]) + pv_ref[0:1, g * G:(g + 1) * G]
        u = y_ref[...] * pv_ref[1:2, :]
        uhat, _ = _rms(u)
        xo_ref[...] = x_ref[...] + (1.0 + vec_ref[4:5, :]) * (uhat * vec_ref[1:2, :])

    row = lambda i: (i, 0)
    full = lambda i: (0, 0)
    return pl.pallas_call(
        body, name="pool_fwd", grid=(S // tm,),
        in_specs=_halo_specs(tm, S) + [pl.BlockSpec((8, D), full), pl.BlockSpec((4, G, G), lambda i: (0, 0, 0)),
                                       pl.BlockSpec((8, D), full)],
        out_specs=[pl.BlockSpec((tm, D), row)] * 3,
        out_shape=[jax.ShapeDtypeStruct((S, D), F32), jax.ShapeDtypeStruct((S, D), F32),
                   jax.ShapeDtypeStruct((S, D), BF16)],
        compiler_params=_params("parallel"),
    )(x, x, x, vec, pw, pvec)


def pool_bwd(dout, x, y, z, vec, pw, pvec):
    S = x.shape[0]
    tm = min(256, S)
    G = D // 4
    R = G // N_CHIP

    def body(dop_ref, do_ref, don_ref, yp_ref, y_ref, yn_ref, x_ref, z_ref, vec_ref, pw_ref, pv_ref,
             dx_ref, vg_ref, pg_ref, dw_ref, dh_ref):
        i = pl.program_id(0)

        @pl.when(i == 0)
        def _():
            vg_ref[...] = jnp.zeros_like(vg_ref)
            pg_ref[...] = jnp.zeros_like(pg_ref)
            dw_ref[...] = jnp.zeros_like(dw_ref)

        doa = jnp.concatenate([dop_ref[...], do_ref[...], don_ref[...]], axis=0)
        ya = jnp.concatenate([yp_ref[...], y_ref[...], yn_ref[...]], axis=0)
        t = i * tm - HALO + lax.broadcasted_iota(jnp.int32, (tm + 2 * HALO, 1), 0)
        inside = (t >= 0) & (t < S)
        main = (t >= i * tm) & (t < (i + 1) * tm)
        du, dgate_rows, dgpost_rows = _postnorm_bwd(doa, ya * pv_ref[1:2, :], vec_ref, 1.0)
        du = jnp.where(inside, du, 0.0)
        vg_ref[2:3, :] += jnp.sum(jnp.where(main, dgate_rows, 0.0), axis=0, keepdims=True)
        vg_ref[4:5, :] += jnp.sum(jnp.where(main, dgpost_rows, 0.0), axis=0, keepdims=True)
        dy = du * pv_ref[1:2, :]
        pg_ref[0:1, :] += jnp.sum(jnp.where(main, dy, 0.0), axis=0, keepdims=True)
        pg_ref[1:2, :] += jnp.sum(jnp.where(main, du * ya, 0.0), axis=0, keepdims=True)
        for g in range(4):
            dyg = dy[:, g * G:(g + 1) * G].astype(BF16)
            dz = _dot(dyg, pw_ref[g], NT)
            e = dz / _pool_count(t, POOL_WINDOWS[g], S)
            dh_ref[:, g * G:(g + 1) * G] = (_window_sum(e, g, False) - dz)[HALO:HALO + tm]
            dwg = _dot(z_ref[:, g * G:(g + 1) * G], dyg[HALO:HALO + tm], TN)
            for q in range(N_CHIP):
                dw_ref[q, g] += dwg[q * R:(q + 1) * R, :]
        dx_ref[...] = do_ref[...] + _prenorm_bwd(dh_ref[...], x_ref[...], vec_ref, vg_ref)

    row = lambda i: (i, 0)
    full = lambda i: (0, 0)
    halo = _halo_specs(tm, S)
    return pl.pallas_call(
        body, name="pool_bwd", grid=(S // tm,),
        in_specs=halo + halo + [pl.BlockSpec((tm, D), row), pl.BlockSpec((tm, D), row), pl.BlockSpec((8, D), full),
                                pl.BlockSpec((4, G, G), lambda i: (0, 0, 0)), pl.BlockSpec((8, D), full)],
        out_specs=[pl.BlockSpec((tm, D), row), pl.BlockSpec((8, D), full), pl.BlockSpec((8, D), full),
                   pl.BlockSpec((N_CHIP, 4, R, G), lambda i: (0, 0, 0, 0))],
        out_shape=[jax.ShapeDtypeStruct((S, D), F32), jax.ShapeDtypeStruct((8, D), F32),
                   jax.ShapeDtypeStruct((8, D), F32), jax.ShapeDtypeStruct((N_CHIP, 4, R, G), F32)],
        scratch_shapes=[pltpu.VMEM((tm, D), F32)],
        compiler_params=_params("arbitrary"),
    )(dout, dout, dout, y, y, y, x, z, vec, pw, pvec)


def _w3(shape):
    return pl.BlockSpec(shape, lambda i: (0,) * len(shape))


def mla_pre(x, vec, mw, tabs):
    S = x.shape[0]
    tm = min(256, S)

    def body(x_ref, vec_ref, cos_ref, sin_ref, wq_ref, wkv_ref, wkr_ref, wkrs_ref, qn_ref, kvn_ref,
             wn_ref, wr_ref, wrs_ref, wuk_ref,
             h_ref, cq_ref, ckv_ref, cqn_ref, qnope_ref, qcat_ref, kcat_ref):
        h, _, _ = _prenorm(x_ref[...], vec_ref)
        hb = h.astype(BF16)
        h_ref[...] = hb
        cq_raw = _dot(hb, wq_ref[...])
        ckv_raw = _dot(hb, wkv_ref[...])
        cq_ref[...] = cq_raw
        ckv_ref[...] = ckv_raw
        cos, sin = cos_ref[...], sin_ref[...]
        k_rope = _dot(hb, wkr_ref[...]) * cos + _dot(hb, wkrs_ref[...]) * sin
        ckv = _rms(ckv_raw)[0] * kvn_ref[...]
        kcat_ref[:, 0:KVL] = ckv.astype(BF16)
        kcat_ref[:, KVL:KVL + ROPE] = k_rope.astype(BF16)
        kcat_ref[:, KVL + ROPE:] = jnp.zeros((tm, QPAD - KVL - ROPE), BF16)
        cqb = (_rms(cq_raw)[0] * qn_ref[...]).astype(BF16)
        cqn_ref[...] = cqb
        for hd in range(N_HEADS):
            qn = _dot(cqb, wn_ref[hd]).astype(BF16)
            qnope_ref[hd] = qn
            qcat_ref[hd, :, 0:KVL] = (_dot(qn, wuk_ref[hd], NT) * ATTN_SCALE).astype(BF16)
            qr = (_dot(cqb, wr_ref[hd]) * cos + _dot(cqb, wrs_ref[hd]) * sin) * ATTN_SCALE
            qcat_ref[hd, :, KVL:KVL + ROPE] = qr.astype(BF16)
            qcat_ref[hd, :, KVL + ROPE:] = jnp.zeros((tm, QPAD - KVL - ROPE), BF16)

    row = lambda i: (i, 0)
    hrow = lambda i: (0, i, 0)
    return pl.pallas_call(
        body, name="mla_pre", grid=(S // tm,),
        in_specs=[pl.BlockSpec((tm, D), row), _w3((8, D)), pl.BlockSpec((tm, ROPE), row), pl.BlockSpec((tm, ROPE), row),
                  _w3((D, QL)), _w3((D, KVL)), _w3((D, ROPE)), _w3((D, ROPE)), _w3((1, QL)), _w3((1, KVL)),
                  _w3((N_HEADS, QL, NOPE)), _w3((N_HEADS, QL, ROPE)), _w3((N_HEADS, QL, ROPE)),
                  _w3((N_HEADS, KVL, NOPE))],
        out_specs=[pl.BlockSpec((tm, D), row), pl.BlockSpec((tm, QL), row), pl.BlockSpec((tm, KVL), row),
                   pl.BlockSpec((tm, QL), row), pl.BlockSpec((N_HEADS, tm, NOPE), hrow),
                   pl.BlockSpec((N_HEADS, tm, QPAD), hrow), pl.BlockSpec((tm, QPAD), row)],
        out_shape=[jax.ShapeDtypeStruct((S, D), BF16), jax.ShapeDtypeStruct((S, QL), F32),
                   jax.ShapeDtypeStruct((S, KVL), F32), jax.ShapeDtypeStruct((S, QL), BF16),
                   jax.ShapeDtypeStruct((N_HEADS, S, NOPE), BF16), jax.ShapeDtypeStruct((N_HEADS, S, QPAD), BF16),
                   jax.ShapeDtypeStruct((S, QPAD), BF16)],
        compiler_params=_params("parallel"),
    )(x, vec, tabs[0], tabs[1], mw["wq"], mw["wkv"], mw["wkr"], mw["wkrs"], mw["qn"], mw["kvn"],
      mw["wn"], mw["wr"], mw["wrs"], mw["wuk"])


def attn_fwd(qcat, kcat):
    S = kcat.shape[0]
    tq = min(256, S)

    def body(q_ref, k_ref, o_ref, lse_ref):
        s = _dot(q_ref[...], k_ref[...], NT)
        m = jnp.max(s, axis=-1, keepdims=True)
        p = jnp.exp(s - m)
        l = jnp.sum(p, axis=-1, keepdims=True)
        p = (p * (1.0 / l)).astype(BF16)
        o_ref[...] = _dot(p, k_ref[:, 0:KVL]).astype(BF16)
        lse_ref[...] = m + jnp.log(l)

    return pl.pallas_call(
        body, name="attn_fwd", grid=(N_HEADS, S // tq),
        in_specs=[pl.BlockSpec((None, tq, QPAD), lambda h, i: (h, i, 0)),
                  pl.BlockSpec((S, QPAD), lambda h, i: (0, 0))],
        out_specs=[pl.BlockSpec((None, tq, KVL), lambda h, i: (h, i, 0)),
                   pl.BlockSpec((None, tq, 1), lambda h, i: (h, i, 0))],
        out_shape=[jax.ShapeDtypeStruct((N_HEADS, S, KVL), BF16), jax.ShapeDtypeStruct((N_HEADS, S, 1), F32)],
        compiler_params=_params("parallel", "parallel"),
    )(qcat, kcat)


def mla_post(olat, x, vec, wuv, wo):
    S = x.shape[0]
    tm = min(256, S)

    def body(o_ref, x_ref, vec_ref, wuv_ref, wo_ref, xo_ref, u_ref, ocat_ref):
        u = jnp.zeros((tm, D), F32)
        for hd in range(N_HEADS):
            oc = _dot(o_ref[hd], wuv_ref[hd]).astype(BF16)
            ocat_ref[hd] = oc
            u = u + _dot(oc, wo_ref[hd])
        u_ref[...] = u
        uhat, _ = _rms(u)
        xo_ref[...] = x_ref[...] + (1.0 + vec_ref[4:5, :]) * (uhat * vec_ref[1:2, :])

    row = lambda i: (i, 0)
    hrow = lambda i: (0, i, 0)
    return pl.pallas_call(
        body, name="mla_post", grid=(S // tm,),
        in_specs=[pl.BlockSpec((N_HEADS, tm, KVL), hrow), pl.BlockSpec((tm, D), row), _w3((8, D)),
                  _w3((N_HEADS, KVL, VH)), _w3((N_HEADS, VH, D))],
        out_specs=[pl.BlockSpec((tm, D), row), pl.BlockSpec((tm, D), row), pl.BlockSpec((N_HEADS, tm, VH), hrow)],
        out_shape=[jax.ShapeDtypeStruct((S, D), F32), jax.ShapeDtypeStruct((S, D), F32),
                   jax.ShapeDtypeStruct((N_HEADS, S, VH), BF16)],
        compiler_params=_params("parallel"),
    )(olat, x, vec, wuv, wo)


def mla_post_bwd(dout, u, olat, vec, wuv, wo):
    S = u.shape[0]
    tm = min(256, S)

    def body(do_ref, u_ref, o_ref, vec_ref, wuv_ref, wo_ref, du_ref, docat_ref, dolat_ref, delta_ref, vg_ref):
        @pl.when(pl.program_id(0) == 0)
        def _():
            vg_ref[...] = jnp.zeros_like(vg_ref)

        du, dgate_rows, dgpost_rows = _postnorm_bwd(do_ref[...], u_ref[...], vec_ref, 1.0)
        vg_ref[2:3, :] += jnp.sum(dgate_rows, axis=0, keepdims=True)
        vg_ref[4:5, :] += jnp.sum(dgpost_rows, axis=0, keepdims=True)
        dub = du.astype(BF16)
        du_ref[...] = dub
        for hd in range(N_HEADS):
            doc = _dot(dub, wo_ref[hd], NT).astype(BF16)
            docat_ref[hd] = doc
            dol = _dot(doc, wuv_ref[hd], NT).astype(BF16)
            dolat_ref[hd] = dol
            delta_ref[hd] = jnp.sum(dol.astype(F32) * o_ref[hd].astype(F32), axis=-1, keepdims=True)

    row = lambda i: (i, 0)
    hrow = lambda i: (0, i, 0)
    return pl.pallas_call(
        body, name="mla_post_bwd", grid=(S // tm,),
        in_specs=[pl.BlockSpec((tm, D), row), pl.BlockSpec((tm, D), row), pl.BlockSpec((N_HEADS, tm, KVL), hrow),
                  _w3((8, D)), _w3((N_HEADS, KVL, VH)), _w3((N_HEADS, VH, D))],
        out_specs=[pl.BlockSpec((tm, D), row), pl.BlockSpec((N_HEADS, tm, VH), hrow),
                   pl.BlockSpec((N_HEADS, tm, KVL), hrow), pl.BlockSpec((N_HEADS, tm, 1), hrow), _w3((8, D))],
        out_shape=[jax.ShapeDtypeStruct((S, D), BF16), jax.ShapeDtypeStruct((N_HEADS, S, VH), BF16),
                   jax.ShapeDtypeStruct((N_HEADS, S, KVL), BF16), jax.ShapeDtypeStruct((N_HEADS, S, 1), F32),
                   jax.ShapeDtypeStruct((8, D), F32)],
        compiler_params=_params("arbitrary"),
    )(dout, u, olat, vec, wuv, wo)


def attn_bwd(qcat, kcat, kcat_t, dolat, lse_row, delta_row):
    S = kcat.shape[0]
    tq = min(256, S)

    def body(q_ref, k_ref, kt_ref, do_ref, lse_ref, dl_ref, dq_ref, dk_ref, dv_ref):
        @pl.when((pl.program_id(0) == 0) & (pl.program_id(1) == 0))
        def _():
            dk_ref[...] = jnp.zeros_like(dk_ref)
            dv_ref[...] = jnp.zeros_like(dv_ref)

        q, do = q_ref[...], do_ref[...]
        st = _dot(k_ref[...], q, NT)
        pt = jnp.exp(st - lse_ref[...])
        dpt = _dot(k_ref[:, 0:KVL], do, NT)
        dst = (pt * (dpt - dl_ref[...])).astype(BF16)
        dv_ref[...] += _dot(pt.astype(BF16), do)
        dk_ref[...] += _dot(dst, q)
        dq_ref[...] = _dot(kt_ref[...], dst).T

    return pl.pallas_call(
        body, name="attn_bwd", grid=(N_HEADS, S // tq),
        in_specs=[pl.BlockSpec((None, tq, QPAD), lambda h, i: (h, i, 0)),
                  pl.BlockSpec((S, QPAD), lambda h, i: (0, 0)),
                  pl.BlockSpec((QPAD, S), lambda h, i: (0, 0)),
                  pl.BlockSpec((None, tq, KVL), lambda h, i: (h, i, 0)),
                  pl.BlockSpec((None, 1, tq), lambda h, i: (h, 0, i)),
                  pl.BlockSpec((None, 1, tq), lambda h, i: (h, 0, i))],
        out_specs=[pl.BlockSpec((None, tq, QPAD), lambda h, i: (h, i, 0)),
                   pl.BlockSpec((S, QPAD), lambda h, i: (0, 0)),
                   pl.BlockSpec((S, KVL), lambda h, i: (0, 0))],
        out_shape=[jax.ShapeDtypeStruct((N_HEADS, S, QPAD), F32), jax.ShapeDtypeStruct((S, QPAD), F32),
                   jax.ShapeDtypeStruct((S, KVL), F32)],
        compiler_params=_params("arbitrary", "arbitrary"),
    )(qcat, kcat, kcat_t, dolat, lse_row, delta_row)


def mla_pre_bwd(dout, dq, dk, dv, x, cq_raw, ckv_raw, vec, mw, tabs):
    S = x.shape[0]
    tm = min(256, S)

    def body(do_ref, dq_ref, dk_ref, dv_ref, x_ref, cq_ref, ckv_ref, vec_ref, cos_ref, sin_ref,
             wq_ref, wkv_ref, wkr_ref, wkrs_ref, qn_ref, kvn_ref, wn_ref, wr_ref, wrs_ref, wuk_ref,
             dx_ref, dlat_ref, dka_ref, dkb_ref, dqn_ref, dql_ref, dqa_ref, dqb_ref, vg_ref, ng_ref):
        @pl.when(pl.program_id(0) == 0)
        def _():
            vg_ref[...] = jnp.zeros_like(vg_ref)
            ng_ref[...] = jnp.zeros_like(ng_ref)

        cos, sin = cos_ref[...], sin_ref[...]
        dcq = jnp.zeros((tm, QL), F32)
        for hd in range(N_HEADS):
            dql = (dq_ref[hd, :, 0:KVL] * ATTN_SCALE).astype(BF16)
            dql_ref[hd] = dql
            dqn = _dot(dql, wuk_ref[hd]).astype(BF16)
            dqn_ref[hd] = dqn
            dqr = dq_ref[hd, :, KVL:KVL + ROPE] * ATTN_SCALE
            qa = (dqr * cos).astype(BF16)
            qb = (dqr * sin).astype(BF16)
            dqa_ref[hd] = qa
            dqb_ref[hd] = qb
            dcq = dcq + _dot(dqn, wn_ref[hd], NT) + _dot(qa, wr_ref[hd], NT) + _dot(qb, wrs_ref[hd], NT)
        cqh, rq = _rms(cq_ref[...])
        ng_ref[0:1, :] += jnp.sum(dcq * cqh, axis=0, keepdims=True)
        dcq_raw = _rms_bwd(cqh, rq, dcq * qn_ref[...]).astype(BF16)
        dckv = dk_ref[:, 0:KVL] + dv_ref[...]
        ckvh, rk = _rms(ckv_ref[...])
        ng_ref[1:2, 0:KVL] += jnp.sum(dckv * ckvh, axis=0, keepdims=True)
        dckv_raw = _rms_bwd(ckvh, rk, dckv * kvn_ref[...]).astype(BF16)
        dkr = dk_ref[:, KVL:KVL + ROPE]
        ka = (dkr * cos).astype(BF16)
        kb = (dkr * sin).astype(BF16)
        dlat_ref[:, 0:QL] = dcq_raw
        dlat_ref[:, QL:QL + KVL] = dckv_raw
        dka_ref[...] = ka
        dkb_ref[...] = kb
        dh = (_dot(dcq_raw, wq_ref[...], NT) + _dot(dckv_raw, wkv_ref[...], NT)
              + _dot(ka, wkr_ref[...], NT) + _dot(kb, wkrs_ref[...], NT))
        dx_ref[...] = do_ref[...] + _prenorm_bwd(dh, x_ref[...], vec_ref, vg_ref)

    row = lambda i: (i, 0)
    hrow = lambda i: (0, i, 0)
    return pl.pallas_call(
        body, name="mla_pre_bwd", grid=(S // tm,),
        in_specs=[pl.BlockSpec((tm, D), row), pl.BlockSpec((N_HEADS, tm, QPAD), hrow), pl.BlockSpec((tm, QPAD), row),
                  pl.BlockSpec((tm, KVL), row), pl.BlockSpec((tm, D), row), pl.BlockSpec((tm, QL), row),
                  pl.BlockSpec((tm, KVL), row), _w3((8, D)), pl.BlockSpec((tm, ROPE), row), pl.BlockSpec((tm, ROPE), row),
                  _w3((D, QL)), _w3((D, KVL)), _w3((D, ROPE)), _w3((D, ROPE)), _w3((1, QL)), _w3((1, KVL)),
                  _w3((N_HEADS, QL, NOPE)), _w3((N_HEADS, QL, ROPE)), _w3((N_HEADS, QL, ROPE)),
                  _w3((N_HEADS, KVL, NOPE))],
        out_specs=[pl.BlockSpec((tm, D), row), pl.BlockSpec((tm, QL + KVL), row), pl.BlockSpec((tm, ROPE), row),
                   pl.BlockSpec((tm, ROPE), row), pl.BlockSpec((N_HEADS, tm, NOPE), hrow),
                   pl.BlockSpec((N_HEADS, tm, KVL), hrow), pl.BlockSpec((N_HEADS, tm, ROPE), hrow),
                   pl.BlockSpec((N_HEADS, tm, ROPE), hrow), _w3((8, D)), _w3((8, QL))],
        out_shape=[jax.ShapeDtypeStruct((S, D), F32), jax.ShapeDtypeStruct((S, QL + KVL), BF16),
                   jax.ShapeDtypeStruct((S, ROPE), BF16), jax.ShapeDtypeStruct((S, ROPE), BF16),
                   jax.ShapeDtypeStruct((N_HEADS, S, NOPE), BF16), jax.ShapeDtypeStruct((N_HEADS, S, KVL), BF16),
                   jax.ShapeDtypeStruct((N_HEADS, S, ROPE), BF16), jax.ShapeDtypeStruct((N_HEADS, S, ROPE), BF16),
                   jax.ShapeDtypeStruct((8, D), F32), jax.ShapeDtypeStruct((8, QL), F32)],
        compiler_params=_params("arbitrary"),
    )(dout, dq, dk, dv, x, cq_raw, ckv_raw, vec, tabs[0], tabs[1], mw["wq"], mw["wkv"], mw["wkr"], mw["wkrs"],
      mw["qn"], mw["kvn"], mw["wn"], mw["wr"], mw["wrs"], mw["wuk"])


def mla_dw(h, dlat, dka, dkb, cqn, dqn, dqa, dqb, dql, qnope, olat, docat, ocat, du):
    S = h.shape[0]
    tk = min(512, S)
    nk = S // tk
    flat_a = lambda w: pl.BlockSpec((tk, w), lambda k: (k, 0))
    head_a = lambda w: pl.BlockSpec((None, tk, w), lambda n, k: (n, k, 0))
    shared = lambda w: pl.BlockSpec((tk, w), lambda n, k: (k, 0))
    head_o = lambda r, c: pl.BlockSpec((None, r, c), lambda n, k: (n, 0, 0))
    g = {}
    g["in"] = dw_matmul("mla_dw_in", h, dlat, flat_a(D), flat_a(QL + KVL), (D, QL + KVL),
                        pl.BlockSpec((D, QL + KVL), lambda k: (0, 0)), (nk,))
    g["kr"] = dw_matmul("mla_dw_kr", h, dka, flat_a(D), flat_a(ROPE), (D, ROPE),
                        pl.BlockSpec((D, ROPE), lambda k: (0, 0)), (nk,))
    g["krs"] = dw_matmul("mla_dw_krs", h, dkb, flat_a(D), flat_a(ROPE), (D, ROPE),
                         pl.BlockSpec((D, ROPE), lambda k: (0, 0)), (nk,))
    g["n"] = dw_matmul("mla_dw_n", cqn, dqn, shared(QL), head_a(NOPE), (N_HEADS, QL, NOPE), head_o(QL, NOPE),
                       (N_HEADS, nk))
    g["r"] = dw_matmul("mla_dw_r", cqn, dqa, shared(QL), head_a(ROPE), (N_HEADS, QL, ROPE), head_o(QL, ROPE),
                       (N_HEADS, nk))
    g["rs"] = dw_matmul("mla_dw_rs", cqn, dqb, shared(QL), head_a(ROPE), (N_HEADS, QL, ROPE), head_o(QL, ROPE),
                        (N_HEADS, nk))
    g["uk"] = dw_matmul("mla_dw_uk", dql, qnope, head_a(KVL), head_a(NOPE), (N_HEADS, KVL, NOPE), head_o(KVL, NOPE),
                        (N_HEADS, nk))
    g["uv"] = dw_matmul("mla_dw_uv", olat, docat, head_a(KVL), head_a(VH), (N_HEADS, KVL, VH), head_o(KVL, VH),
                        (N_HEADS, nk))
    g["o"] = dw_matmul("mla_dw_o", ocat, du, head_a(VH), shared(D), (N_HEADS, VH, D), head_o(VH, D),
                       (N_HEADS, nk))
    return g


def loss_head(y, target):
    S = y.shape[0]
    tm = min(512, S)

    def body(y_ref, t_ref, loss_ref, dy_ref):
        @pl.when(pl.program_id(0) == 0)
        def _():
            loss_ref[...] = jnp.zeros_like(loss_ref)

        err = y_ref[...] - t_ref[...]
        dy_ref[...] = err * (1.0 / D)
        loss_ref[...] += 0.5 * jnp.sum(jnp.mean(err * err, axis=-1, keepdims=True), axis=0, keepdims=True)

    row = lambda i: (i, 0)
    return pl.pallas_call(
        body, name="loss_head", grid=(S // tm,),
        in_specs=[pl.BlockSpec((tm, D), row), pl.BlockSpec((tm, D), row)],
        out_specs=[pl.BlockSpec((1, 1), lambda i: (0, 0)), pl.BlockSpec((tm, D), row)],
        out_shape=[jax.ShapeDtypeStruct((1, 1), F32), jax.ShapeDtypeStruct((S, D), F32)],
        compiler_params=_params("arbitrary"),
    )(y, target)


MOD_COLS = 9 * D // N_CHIP


def mod_fwd(c_pad, ada_w, ada_b_loc):
    tn = MOD_COLS // 3

    def body(c_ref, w_ref, b_ref, o_ref):
        c = c_ref[...]
        sc = (c * jax.nn.sigmoid(c)).astype(BF16)
        o_ref[...] = _dot(sc, w_ref[...].astype(BF16)) + b_ref[...]

    return pl.pallas_call(
        body, name="mod_fwd", grid=(2, 3),
        in_specs=[pl.BlockSpec((16, D), lambda i, n: (0, 0)), pl.BlockSpec((None, D, tn), lambda i, n: (i, 0, n)),
                  pl.BlockSpec((None, 1, tn), lambda i, n: (i, 0, n))],
        out_specs=pl.BlockSpec((None, 16, tn), lambda i, n: (i, 0, n)),
        out_shape=jax.ShapeDtypeStruct((2, 16, MOD_COLS), F32),
        compiler_params=_params("parallel", "parallel"),
    )(c_pad, ada_w, ada_b_loc)


def _adamw_math(w, g, m, v):
    m = ADAM_B1 * m + (1.0 - ADAM_B1) * g
    v = ADAM_B2 * v + (1.0 - ADAM_B2) * (g * g)
    m_hat = m / (1.0 - ADAM_B1 ** ADAM_STEP)
    v_hat = v / (1.0 - ADAM_B2 ** ADAM_STEP)
    delta = -ADAM_LR * (m_hat / (jnp.sqrt(v_hat) + ADAM_EPS) + ADAM_WD * w)
    return delta, m, v


def adamw(name, w, g, m, v):
    shape = w.shape
    cols = shape[-1]
    rows = w.size // cols
    tr = rows
    for cand in (512, 256, 128, 64, 32, 16, 8):
        if rows % cand == 0 and cand * cols * 4 <= (2 << 20):
            tr = cand
            break

    def body(w_ref, g_ref, m_ref, v_ref, d_ref, mo_ref, vo_ref):
        d_ref[...], mo_ref[...], vo_ref[...] = _adamw_math(w_ref[...], g_ref[...], m_ref[...], v_ref[...])

    spec = pl.BlockSpec((tr, cols), lambda i: (i, 0))
    outs = pl.pallas_call(
        body, name=name, grid=(rows // tr,), in_specs=[spec] * 4, out_specs=[spec] * 3,
        out_shape=[jax.ShapeDtypeStruct((rows, cols), F32)] * 3,
        compiler_params=_params("parallel"),
    )(*[a.reshape(rows, cols) for a in (w, g, m, v)])
    return [o.reshape(shape) for o in outs]


def adamw_ada(c_pad, dmod, w, m, v):
    tr = 256

    def body(c_ref, dm_ref, w_ref, m_ref, v_ref, g_ref, d_ref, mo_ref, vo_ref):
        c = c_ref[...]
        sc = (c * jax.nn.sigmoid(c)).astype(BF16)
        g = _dot(sc, dm_ref[...].astype(BF16), TN)
        g_ref[...] = g
        d_ref[...], mo_ref[...], vo_ref[...] = _adamw_math(w_ref[...], g, m_ref[...], v_ref[...])

    wspec = pl.BlockSpec((None, tr, MOD_COLS), lambda i, r: (i, r, 0))
    return pl.pallas_call(
        body, name="adamw_ada", grid=(2, D // tr),
        in_specs=[pl.BlockSpec((16, tr), lambda i, r: (0, r)),
                  pl.BlockSpec((None, 16, MOD_COLS), lambda i, r: (i, 0, 0)), wspec, wspec, wspec],
        out_specs=[wspec] * 4,
        out_shape=[jax.ShapeDtypeStruct((2, D, MOD_COLS), F32)] * 4,
        compiler_params=_params("parallel", "parallel"),
    )(c_pad, dmod, w, m, v)


def sum_devices(name, a):
    _, R, C = a.shape
    tr = R
    for cand in (64, 32, 16, 8):
        if R % cand == 0:
            tr = cand
            break

    def body(a_ref, o_ref):
        acc = a_ref[0]
        for dev in range(1, N_DEV):
            acc = acc + a_ref[dev]
        o_ref[...] = acc

    return pl.pallas_call(
        body, name=name, grid=(R // tr,),
        in_specs=[pl.BlockSpec((N_DEV, tr, C), lambda i: (0, i, 0))],
        out_specs=pl.BlockSpec((tr, C), lambda i: (i, 0)),
        out_shape=jax.ShapeDtypeStruct((R, C), F32),
        compiler_params=_params("parallel"),
    )(a)


def _place():
    return lax.axis_index("x"), lax.axis_index("y"), lax.axis_index("c")


def _other_chips(x, y):
    return [(1 - x, y), (x, 1 - y), (1 - x, 1 - y)]


def gather_devices(name, a):
    m_per, n = a.shape

    def body(x_ref, out_ref, send_sems, recv_sems, local_sem):
        x, y, c = _place()
        me, sibling = (x, y, c), (x, y, 1 - c)
        chips = _other_chips(x, y)

        def rows(px, py, pc):
            return out_ref.at[pl.ds((4 * px + 2 * py + pc) * m_per, m_per), :]

        def copy(k, block, to, src=None):
            return pltpu.make_async_remote_copy(
                src_ref=rows(*block) if src is None else src, dst_ref=rows(*block),
                send_sem=send_sems.at[k], recv_sem=recv_sems.at[k], device_id=to, device_id_type=MESH)

        mine = pltpu.make_async_copy(x_ref, rows(*me), local_sem)
        mine.start()
        first = [copy(0, me, sibling, src=x_ref)]
        first += [copy(1 + j, me, (*chip, c), src=x_ref) for j, chip in enumerate(chips)]
        for cp in first:
            cp.start()
        passed = [copy(4 + j, (*chip, c), sibling) for j, chip in enumerate(chips)]
        for j, chip in enumerate(chips):
            copy(1 + j, (*chip, c), me).wait_recv()
            passed[j].start()
        copy(0, sibling, me).wait_recv()
        for j, chip in enumerate(chips):
            copy(4 + j, (*chip, 1 - c), me).wait_recv()
        for cp in first + passed:
            cp.wait_send()
        mine.wait()

    out = pl.pallas_call(
        body, name=name,
        out_shape=jax.ShapeDtypeStruct((N_DEV * m_per, n), a.dtype),
        in_specs=[pl.BlockSpec(memory_space=pltpu.VMEM)],
        out_specs=pl.BlockSpec(memory_space=pltpu.VMEM),
        scratch_shapes=[pltpu.SemaphoreType.DMA((7,)), pltpu.SemaphoreType.DMA((7,)), pltpu.SemaphoreType.DMA],
        compiler_params=pltpu.CompilerParams(vmem_limit_bytes=VMEM_LIMIT),
    )(a)
    return out.reshape(N_DEV, m_per, n)


_ANY = pl.BlockSpec(memory_space=pl.ANY)


def gather_weights(shards):
    n = len(shards)

    def body(*refs):
        src, dst = refs[:n], refs[n:2 * n]
        ici_send, ici_recv, d2d_send, d2d_recv, local_sem = refs[2 * n:]
        x, y, c = _place()
        me = 2 * x + y
        chips = _other_chips(x, y)
        sibling = (x, y, 1 - c)
        local = [pltpu.make_async_copy(src[t], dst[t].at[me], local_sem.at[t]) for t in range(n)]
        for cp in local:
            cp.start()

        def ici(t, r, half):
            cx, cy = chips[r]
            return pltpu.make_async_remote_copy(
                src_ref=src[t].at[half], dst_ref=dst[t].at[me, half],
                send_sem=ici_send.at[t, r], recv_sem=ici_recv.at[t, r], device_id=(cx, cy, c), device_id_type=MESH)

        def d2d(t, r, half):
            cx, cy = chips[r]
            there = dst[t].at[2 * cx + cy, half]
            return pltpu.make_async_remote_copy(
                src_ref=there, dst_ref=there, send_sem=d2d_send.at[t, r], recv_sem=d2d_recv.at[t, r],
                device_id=sibling, device_id_type=MESH)

        for t in range(n):
            for r in range(3):
                ici(t, r, c).start()
        for t in range(n):
            for r in range(3):
                ici(t, r, c).wait_recv()
                d2d(t, r, c).start()
        for t in range(n):
            for r in range(3):
                d2d(t, r, 1 - c).wait_recv()
        for t in range(n):
            for r in range(3):
                ici(t, r, c).wait_send()
                d2d(t, r, c).wait_send()
        for cp in local:
            cp.wait()

    return pl.pallas_call(
        body, name="gather_weights",
        out_shape=[jax.ShapeDtypeStruct((N_CHIP,) + s.shape, s.dtype) for s in shards],
        in_specs=[_ANY] * n, out_specs=[_ANY] * n,
        scratch_shapes=[pltpu.SemaphoreType.DMA((n, 3))] * 4 + [pltpu.SemaphoreType.DMA((n,))],
    )(*shards)


def reduce_pair(grads):
    n = len(grads)

    def body(*refs):
        src, dst = refs[:n], refs[n:2 * n]
        send_sem, recv_sem = refs[2 * n:]
        x, y, c = _place()
        cps = [pltpu.make_async_remote_copy(
            src_ref=src[t].at[:, 1 - c], dst_ref=dst[t], send_sem=send_sem.at[t], recv_sem=recv_sem.at[t],
            device_id=(x, y, 1 - c), device_id_type=MESH) for t in range(n)]
        for cp in cps:
            cp.start()
        for cp in cps:
            cp.wait()

    return pl.pallas_call(
        body, name="reduce_pair",
        out_shape=[jax.ShapeDtypeStruct((N_CHIP,) + g.shape[2:], g.dtype) for g in grads],
        in_specs=[_ANY] * n, out_specs=[_ANY] * n,
        scratch_shapes=[pltpu.SemaphoreType.DMA((n,))] * 2,
    )(*grads)


def pair_add(name, core, g, got):
    _, _, R, C = g.shape

    def body(core_ref, g_ref, got_ref, o_ref):
        o_ref[...] = (g_ref[...] + got_ref[...]).astype(BF16)

    return pl.pallas_call(
        body, name=name,
        grid_spec=pltpu.PrefetchScalarGridSpec(
            num_scalar_prefetch=1, grid=(N_CHIP,),
            in_specs=[pl.BlockSpec((None, None, R, C), lambda q, core_ref: (q, core_ref[0], 0, 0)),
                      pl.BlockSpec((None, R, C), lambda q, core_ref: (q, 0, 0))],
            out_specs=pl.BlockSpec((None, R, C), lambda q, core_ref: (q, 0, 0))),
        out_shape=jax.ShapeDtypeStruct((N_CHIP, R, C), BF16),
        compiler_params=_params("parallel"),
    )(core, g, got)


def reduce_chips(sums):
    n = len(sums)

    def body(*refs):
        src, dst = refs[:n], refs[n:2 * n]
        send_sem, recv_sem = refs[2 * n:]
        x, y, c = _place()
        chips = _other_chips(x, y)
        cps = []
        for t in range(n):
            for r, (cx, cy) in enumerate(chips):
                cps.append(pltpu.make_async_remote_copy(
                    src_ref=src[t].at[2 * cx + cy], dst_ref=dst[t].at[r],
                    send_sem=send_sem.at[t, r], recv_sem=recv_sem.at[t, r],
                    device_id=(cx, cy, c), device_id_type=MESH))
        for cp in cps:
            cp.start()
        for cp in cps:
            cp.wait()

    return pl.pallas_call(
        body, name="reduce_chips",
        out_shape=[jax.ShapeDtypeStruct((3,) + s.shape[1:], s.dtype) for s in sums],
        in_specs=[_ANY] * n, out_specs=[_ANY] * n,
        scratch_shapes=[pltpu.SemaphoreType.DMA((n, 3))] * 2,
    )(*sums)


def chip_add(name, chip, s, got):
    _, R, C = s.shape

    def body(chip_ref, s_ref, got_ref, o_ref):
        o_ref[...] = ((s_ref[...].astype(F32) + got_ref[0].astype(F32)) + got_ref[1].astype(F32)) + got_ref[2].astype(F32)

    return pl.pallas_call(
        body, name=name,
        grid_spec=pltpu.PrefetchScalarGridSpec(
            num_scalar_prefetch=1, grid=(1,),
            in_specs=[pl.BlockSpec((None, R, C), lambda i, chip_ref: (chip_ref[0], 0, 0)),
                      pl.BlockSpec((3, R, C), lambda i, chip_ref: (0, 0, 0))],
            out_specs=pl.BlockSpec((R, C), lambda i, chip_ref: (0, 0))),
        out_shape=jax.ShapeDtypeStruct((R, C), F32),
        compiler_params=_params("arbitrary"),
    )(chip, s, got)


def share_halves(halves, layout):
    n = len(halves)
    n_out = len(layout)

    def body(*refs):
        src, dst = refs[:n], refs[n:n + n_out]
        send_sem, recv_sem, local_sem = refs[n + n_out:]
        x, y, c = _place()
        local, remote = [], []
        for o, (_, slots) in enumerate(layout):
            for k, t in enumerate(slots):
                local.append(pltpu.make_async_copy(src[t], dst[o].at[k, c], local_sem.at[t]))
                remote.append(pltpu.make_async_remote_copy(
                    src_ref=src[t], dst_ref=dst[o].at[k, c], send_sem=send_sem.at[t], recv_sem=recv_sem.at[t],
                    device_id=(x, y, 1 - c), device_id_type=MESH))
        for cp in local + remote:
            cp.start()
        for cp in remote:
            cp.wait()
        for cp in local:
            cp.wait()

    return pl.pallas_call(
        body, name="share_halves",
        out_shape=[jax.ShapeDtypeStruct(shape, F32) for shape, _ in layout],
        in_specs=[_ANY] * n, out_specs=[_ANY] * n_out,
        scratch_shapes=[pltpu.SemaphoreType.DMA((n,))] * 3,
    )(*halves)


def _swap_rope(a):
    return jnp.concatenate([a[..., ROPE // 2:], a[..., :ROPE // 2]], axis=-1)


def _rope_tables(S):
    inv = 1.0 / (ROPE_THETA ** (jnp.arange(0, ROPE, 2, dtype=F32) / ROPE))
    ang = jnp.arange(S, dtype=F32)[:, None] * inv[None, :]
    cos, sin = jnp.cos(ang), jnp.sin(ang)
    return jnp.concatenate([cos, cos], axis=1), jnp.concatenate([-sin, sin], axis=1)


def _vec(norm_g, mod, i, k):
    rows = [norm_g[i, 2 * k], norm_g[i, 2 * k + 1], mod[i, 3 * k], mod[i, 3 * k + 1], mod[i, 3 * k + 2]]
    return jnp.concatenate([jnp.stack(rows), jnp.zeros((3, D), F32)], axis=0)


def _example_step(x, target, mod, norm_g, pvec, ffn_in, ffn_out, pw, mw, wuv, wo):
    S = x.shape[0]
    tabs = _rope_tables(S)
    vec = [[_vec(norm_g, mod, i, k) for k in range(3)] for i in range(2)]
    saved = {}
    for i in range(2):
        xin = x
        x, a, u, h = ffn_fwd(xin, vec[i][0], ffn_in[i][0], ffn_out[i][0], 0.5)
        saved[i, 0] = (xin, a, u, h)
        xin = x
        if i == 0:
            x, y, z = pool_fwd(xin, vec[i][1], pw, pvec)
            saved[i, 1] = (xin, y, z)
        else:
            h_m, cq_raw, ckv_raw, cqn, qnope, qcat, kcat = mla_pre(xin, vec[i][1], mw, tabs)
            olat, lse = attn_fwd(qcat, kcat)
            x, u_m, ocat = mla_post(olat, xin, vec[i][1], wuv, wo)
            saved[i, 1] = (xin, h_m, cq_raw, ckv_raw, cqn, qnope, qcat, kcat, olat, lse, u_m, ocat)
        xin = x
        x, a, u, h = ffn_fwd(xin, vec[i][2], ffn_in[i][1], ffn_out[i][1], 0.5)
        saved[i, 2] = (xin, a, u, h)
    loss, dx = loss_head(x, target)

    vg = {}
    gw = {}
    for i in (1, 0):
        xin, a, u, h = saved[i, 2]
        dx, du, act, da, vg[i, 2] = ffn_bwd(dx, xin, u, a, vec[i][2], ffn_in[i][1], ffn_out[i][1], 0.5)
        gw["ffn_in", i, 1], gw["ffn_out", i, 1] = ffn_dw(h, da, act, du)
        if i == 0:
            xin, y, z = saved[i, 1]
            dx, vg[i, 1], pgrad, gw["pool"] = pool_bwd(dx, xin, y, z, vec[i][1], pw, pvec)
        else:
            xin, h_m, cq_raw, ckv_raw, cqn, qnope, qcat, kcat, olat, lse, u_m, ocat = saved[i, 1]
            du, docat, dolat, delta, vg_post = mla_post_bwd(dx, u_m, olat, vec[i][1], wuv, wo)
            dq, dk, dv = attn_bwd(qcat, kcat, kcat.T, dolat, lse.reshape(N_HEADS, 1, S), delta.reshape(N_HEADS, 1, S))
            dx, dlat, dka, dkb, dqn, dql, dqa, dqb, vg_pre, ngrad = mla_pre_bwd(
                dx, dq, dk, dv, xin, cq_raw, ckv_raw, vec[i][1], mw, tabs)
            vg[i, 1] = vg_post + vg_pre
            g = mla_dw(h_m, dlat, dka, dkb, cqn, dqn, dqa, dqb, dql, qnope, olat, docat, ocat, du)
            gw["mla_in"] = jnp.concatenate([g["in"], g["kr"] + _swap_rope(g["krs"])], axis=1)
            gw["uq"] = jnp.concatenate([jnp.transpose(g["n"], (1, 0, 2)),
                                        jnp.transpose(g["r"] + _swap_rope(g["rs"]), (1, 0, 2))], axis=-1)
            gw["uk"] = jnp.transpose(g["uk"], (1, 0, 2))
            gw["uv"] = jnp.transpose(g["uv"], (1, 0, 2))
            gw["wo"] = g["o"].reshape(D, D)
        xin, a, u, h = saved[i, 0]
        dx, du, act, da, vg[i, 0] = ffn_bwd(dx, xin, u, a, vec[i][0], ffn_in[i][0], ffn_out[i][0], 0.5)
        gw["ffn_in", i, 0], gw["ffn_out", i, 0] = ffn_dw(h, da, act, du)
    return loss, dx, gw, vg, pgrad, ngrad


SMALL_IN = 8 * 640
SMALL_GRAD = 8 * 4224
SMALL_W = 8 * 2944


def _pack(parts, total):
    flat = jnp.concatenate([p.reshape(-1) for p in parts])
    return jnp.concatenate([flat, jnp.zeros((total - flat.shape[0],), F32)]).reshape(8, total // 8)


def kernel(x, c, ada_w, ada_b, norm_g, ffn_w_in, ffn_w_out, pool_w, pool_b, pool_scale, mla_w_in, mla_q_norm, mla_kv_norm, mla_w_uq, mla_w_uk, mla_w_uv, mla_w_o, loss_target, m_ada_w, m_ada_b, m_norm_g, m_ffn_w_in, m_ffn_w_out, m_pool_w, m_pool_b, m_pool_scale, m_mla_w_in, m_mla_q_norm, m_mla_kv_norm, m_mla_w_uq, m_mla_w_uk, m_mla_w_uv, m_mla_w_o, v_ada_w, v_ada_b, v_norm_g, v_ffn_w_in, v_ffn_w_out, v_pool_w, v_pool_b, v_pool_scale, v_mla_w_in, v_mla_q_norm, v_mla_kv_norm, v_mla_w_uq, v_mla_w_uk, v_mla_w_uv, v_mla_w_o):
    ix, iy, ic = _place()
    chip = 2 * ix + iy
    dev = 2 * chip + ic
    core_arr = ic.astype(jnp.int32).reshape(1)
    chip_arr = chip.astype(jnp.int32).reshape(1)
    S = x.shape[1]
    G = D // 4
    NG = D // N_CHIP

    def chip_cols(a, width, axis):
        return lax.dynamic_slice_in_dim(a, chip * width, width, axis)

    got = gather_devices("gather_small_in", _pack([c, norm_g, pool_b, mla_q_norm], SMALL_IN)).reshape(N_DEV, SMALL_IN)
    c_all = got[:, :D]
    parts = got[0::2]
    o = D
    norm_g_full = parts[:, o:o + 12 * NG].reshape(N_CHIP, 2, 6, NG).transpose(1, 2, 0, 3).reshape(2, 6, D)
    o += 12 * NG
    pool_b_full = parts[:, o:o + G].reshape(N_CHIP, 4, G // N_CHIP).transpose(1, 0, 2).reshape(1, D)
    o += G
    q_norm_full = parts[:, o:o + QL // N_CHIP].reshape(1, QL)
    pvec = jnp.concatenate([pool_b_full, pool_scale, jnp.zeros((6, D), F32)], axis=0)

    c_pad = jnp.concatenate([c_all, jnp.zeros((8, D), F32)], axis=0)
    mod_loc = mod_fwd(c_pad, ada_w, chip_cols(ada_b, MOD_COLS, 1).reshape(2, 1, MOD_COLS))
    got = gather_devices("gather_mod", mod_loc[:, :8].transpose(1, 0, 2).reshape(8, 2 * MOD_COLS))
    mine = lax.dynamic_index_in_dim(got[0::2].reshape(N_CHIP, 8, 2, MOD_COLS), dev, axis=1, keepdims=False)
    mod = mine.transpose(1, 0, 2).reshape(2, 9, D)

    bf = lambda a: a.astype(BF16)
    shards = [bf(ffn_w_in[i, k]).reshape(2, D // 2, FSH) for i in range(2) for k in range(2)]
    shards += [bf(ffn_w_out[i, k]).reshape(2, DFF // 8, D) for i in range(2) for k in range(2)]
    shards += [bf(pool_w[0]).reshape(2, 2 * G // N_CHIP, G), bf(mla_w_in[0]).reshape(2, D // 8, QL + KVL + ROPE),
               bf(mla_w_uq[0]).reshape(2, QL // 8, N_HEADS * (NOPE + ROPE)), bf(mla_w_o[0]).reshape(2, D // 8, D)]
    full = gather_weights(shards)
    ffn_in = [[full[2 * i + k].reshape(N_CHIP, D, FSH) for k in range(2)] for i in range(2)]
    ffn_out = [[full[4 + 2 * i + k].reshape(2, FSH, D) for k in range(2)] for i in range(2)]
    pw = full[8].reshape(N_CHIP, 4, G // N_CHIP, G).transpose(1, 0, 2, 3).reshape(4, G, G)
    w_in = full[9].reshape(D, QL + KVL + ROPE)
    w_uq = full[10].reshape(QL, N_HEADS, NOPE + ROPE)
    wkr = w_in[:, QL + KVL:]
    wr = jnp.transpose(w_uq[:, :, NOPE:], (1, 0, 2))
    mw = dict(wq=w_in[:, :QL], wkv=w_in[:, QL:QL + KVL], wkr=wkr, wkrs=_swap_rope(wkr),
              qn=q_norm_full, kvn=mla_kv_norm, wn=jnp.transpose(w_uq[:, :, :NOPE], (1, 0, 2)),
              wr=wr, wrs=_swap_rope(wr), wuk=jnp.transpose(bf(mla_w_uk[0]), (1, 0, 2)))
    wuv = jnp.transpose(bf(mla_w_uv[0]), (1, 0, 2))
    wo = full[11].reshape(N_HEADS, VH, D)

    loss_mine, grad_x, gw, vg, pgrad, ngrad = _example_step(
        x[0], loss_target[0], mod, norm_g_full, pvec, ffn_in, ffn_out, pw, mw, wuv, wo)
    loss = lax.psum(loss_mine[0, 0], ("x", "y", "c"))

    grads = [gw["ffn_in", i, k].reshape(N_CHIP, 2, D // 2, FSH) for i in range(2) for k in range(2)]
    grads += [gw["ffn_out", i, k].reshape(N_CHIP, 2, DFF // 8, D) for i in range(2) for k in range(2)]
    grads += [gw["pool"].reshape(N_CHIP, 2, 2 * G // N_CHIP, G),
              gw["mla_in"].reshape(N_CHIP, 2, D // 8, QL + KVL + ROPE),
              gw["uq"].reshape(N_CHIP, 2, QL // 8, N_HEADS * (NOPE + ROPE)),
              gw["wo"].reshape(N_CHIP, 2, D // 8, D)]
    from_pair = reduce_pair(grads)
    sums = [pair_add(f"pair_add_{t}", core_arr, g, p) for t, (g, p) in enumerate(zip(grads, from_pair))]
    from_chips = reduce_chips(sums)
    halves = [chip_add(f"chip_add_{t}", chip_arr, s, p) for t, (s, p) in enumerate(zip(sums, from_chips))]
    layout = [((4, 2, D // 2, FSH), (0, 1, 2, 3)), ((4, 2, DFF // 8, D), (4, 5, 6, 7)),
              ((1, 2, 2 * G // N_CHIP, G), (8,)), ((1, 2, D // 8, QL + KVL + ROPE), (9,)),
              ((1, 2, QL // 8, N_HEADS * (NOPE + ROPE)), (10,)), ((1, 2, D // 8, D), (11,))]
    g_ffn_in, g_ffn_out, g_pool_w, g_mla_in, g_uq, g_wo = share_halves(halves, layout)
    g_ffn_in = g_ffn_in.reshape(ffn_w_in.shape)
    g_ffn_out = g_ffn_out.reshape(ffn_w_out.shape)
    g_pool_w = g_pool_w.reshape(pool_w.shape)
    g_mla_in = g_mla_in.reshape(mla_w_in.shape)
    g_uq = g_uq.reshape(mla_w_uq.shape)
    g_wo = g_wo.reshape(mla_w_o.shape)

    ukv = jnp.concatenate([gw["uk"].reshape(KVL, N_HEADS * NOPE), gw["uv"].reshape(KVL, N_HEADS * VH)], axis=0)
    ukv = sum_devices("sum_ukv", gather_devices("gather_ukv", ukv))
    g_uk = ukv[:KVL].reshape(mla_w_uk.shape)
    g_uv = ukv[KVL:].reshape(mla_w_uv.shape)

    dmod = jnp.stack([jnp.concatenate([vg[i, k][0:3] for k in range(3)]) for i in range(2)])
    dnorm = jnp.stack([jnp.concatenate([vg[i, k][3:5] for k in range(3)]) for i in range(2)])
    small = _pack([dmod, dnorm, pgrad[0], pgrad[1], ngrad[0], ngrad[1, :KVL]], SMALL_GRAD)
    got = gather_devices("gather_small_grad", small)
    tot = sum_devices("sum_small_grad", got).reshape(-1)
    n_mod = 2 * 9 * D
    g_ada_b = tot[:n_mod].reshape(ada_b.shape)
    o = n_mod
    g_norm = chip_cols(tot[o:o + 12 * D].reshape(2, 6, D), NG, 2)
    o += 12 * D
    g_pool_b = chip_cols(tot[o:o + D].reshape(1, 4, G), G // N_CHIP, 2)
    o += D
    g_pool_scale = tot[o:o + D].reshape(pool_scale.shape)
    o += D
    g_q_norm = chip_cols(tot[o:o + QL].reshape(1, QL), QL // N_CHIP, 1)
    o += QL
    g_kv_norm = tot[o:o + KVL].reshape(mla_kv_norm.shape)
    dmod_all = chip_cols(got.reshape(N_DEV, -1)[:, :n_mod].reshape(N_DEV, 2, 9 * D), MOD_COLS, 2)
    dmod_pad = jnp.concatenate([dmod_all.transpose(1, 0, 2), jnp.zeros((2, 8, MOD_COLS), F32)], axis=1)

    g_ada_w, d_ada_w, nm_ada_w, nv_ada_w = adamw_ada(c_pad, dmod_pad, ada_w, m_ada_w, v_ada_w)
    small_names = ["ada_b", "norm_g", "pool_b", "pool_scale", "mla_q_norm", "mla_kv_norm"]
    small_w = [ada_b, norm_g, pool_b, pool_scale, mla_q_norm, mla_kv_norm]
    small_g = [g_ada_b, g_norm, g_pool_b, g_pool_scale, g_q_norm, g_kv_norm]
    small_m = [m_ada_b, m_norm_g, m_pool_b, m_pool_scale, m_mla_q_norm, m_mla_kv_norm]
    small_v = [v_ada_b, v_norm_g, v_pool_b, v_pool_scale, v_mla_q_norm, v_mla_kv_norm]
    packed = adamw("adamw_small", *[_pack(p, SMALL_W) for p in (small_w, small_g, small_m, small_v)])
    upd = {}
    o = 0
    for name, w in zip(small_names, small_w):
        upd[name] = [p.reshape(-1)[o:o + w.size].reshape(w.shape) for p in packed]
        o += w.size
    big = [("ffn_w_in", ffn_w_in, g_ffn_in, m_ffn_w_in, v_ffn_w_in),
           ("ffn_w_out", ffn_w_out, g_ffn_out, m_ffn_w_out, v_ffn_w_out),
           ("pool_w", pool_w, g_pool_w, m_pool_w, v_pool_w),
           ("mla_w_in", mla_w_in, g_mla_in, m_mla_w_in, v_mla_w_in),
           ("mla_w_uq", mla_w_uq, g_uq, m_mla_w_uq, v_mla_w_uq),
           ("mla_w_uk", mla_w_uk, g_uk, m_mla_w_uk, v_mla_w_uk),
           ("mla_w_uv", mla_w_uv, g_uv, m_mla_w_uv, v_mla_w_uv),
           ("mla_w_o", mla_w_o, g_wo, m_mla_w_o, v_mla_w_o)]
    for name, w, g, m, v in big:
        upd[name] = adamw("adamw_" + name, w, g, m, v)
    upd["ada_w"] = [d_ada_w, nm_ada_w, nv_ada_w]

    order = ["ada_w", "ada_b", "norm_g", "ffn_w_in", "ffn_w_out", "pool_w", "pool_b", "pool_scale", "mla_w_in",
             "mla_q_norm", "mla_kv_norm", "mla_w_uq", "mla_w_uk", "mla_w_uv", "mla_w_o"]
    grad = dict(ada_w=g_ada_w, ada_b=g_ada_b, norm_g=g_norm, ffn_w_in=g_ffn_in, ffn_w_out=g_ffn_out, pool_w=g_pool_w,
                pool_b=g_pool_b, pool_scale=g_pool_scale, mla_w_in=g_mla_in, mla_q_norm=g_q_norm,
                mla_kv_norm=g_kv_norm, mla_w_uq=g_uq, mla_w_uk=g_uk, mla_w_uv=g_uv, mla_w_o=g_wo)
    return (loss, grad_x[None], *[grad[n] for n in order], *[upd[n][0] for n in order],
            *[upd[n][1] for n in order], *[upd[n][2] for n in order])
```

```python
import functools

import jax
import jax.numpy as jnp
from jax import lax
from jax.experimental import pallas as pl
from jax.experimental.pallas import tpu as pltpu

F32 = jnp.float32
BF16 = jnp.bfloat16

D = 1024
DFF = 2816
FSH = 1408
N_CHIP = 4
N_DEV = 8
N_HEADS = 16
NOPE = 64
ROPE = 32
VH = 64
QL = 256
KVL = 128
QPAD = 256
EPS = 1e-6
ATTN_SCALE = (NOPE + ROPE) ** -0.5
ROPE_THETA = 10000.0
POOL_WINDOWS = (2, 4, 8, 16)
HALO = 8

ADAM_LR, ADAM_B1, ADAM_B2, ADAM_EPS, ADAM_WD, ADAM_STEP = 0.001, 0.9, 0.999, 1e-08, 0.01, 10

VMEM_LIMIT = 60 * 1024 * 1024
MESH = pl.DeviceIdType.MESH

NT = (((1,), (1,)), ((), ()))
TN = (((0,), (0,)), ((), ()))


def _params(*sem):
    return pltpu.CompilerParams(dimension_semantics=sem, vmem_limit_bytes=VMEM_LIMIT)


def _dot(a, b, dims=None):
    if dims is None:
        return jnp.dot(a, b, preferred_element_type=F32)
    return lax.dot_general(a, b, dims, preferred_element_type=F32)


def _rms(x):
    r = lax.rsqrt(jnp.mean(x * x, axis=-1, keepdims=True) + EPS)
    return x * r, r


def _rms_bwd(xhat, r, dxhat):
    return r * (dxhat - xhat * jnp.mean(dxhat * xhat, axis=-1, keepdims=True))


def _prenorm(x, vec_ref):
    xhat, r = _rms(x)
    h = xhat * vec_ref[0:1, :] * (1.0 + vec_ref[3:4, :]) + vec_ref[2:3, :]
    return h, xhat, r


def _postnorm_bwd(dout, u, vec_ref, weight):
    uhat, r = _rms(u)
    gt = weight * (1.0 + vec_ref[4:5, :])
    dy = dout * gt
    dgate_rows = (weight * dout) * (uhat * vec_ref[1:2, :])
    dgpost_rows = dy * uhat
    du = _rms_bwd(uhat, r, dy * vec_ref[1:2, :])
    return du, dgate_rows, dgpost_rows


def _prenorm_bwd(dh, x, vec_ref, vg_ref):
    xhat, r = _rms(x)
    sc1 = 1.0 + vec_ref[3:4, :]
    g = vec_ref[0:1, :]
    vg_ref[0:1, :] += jnp.sum(dh, axis=0, keepdims=True)
    vg_ref[1:2, :] += jnp.sum(dh * (xhat * g), axis=0, keepdims=True)
    vg_ref[3:4, :] += jnp.sum(dh * sc1 * xhat, axis=0, keepdims=True)
    return _rms_bwd(xhat, r, dh * g * sc1)


def ffn_fwd(x, vec, w_in, w_out, weight):
    S = x.shape[0]
    tm = min(256, S)

    def body(x_ref, vec_ref, wg_ref, wu_ref, wo_ref, xo_ref, a_ref, u_ref, h_ref, acc_ref):
        j = pl.program_id(1)

        @pl.when(j == 0)
        def _():
            h, _, _ = _prenorm(x_ref[...], vec_ref)
            h_ref[...] = h.astype(BF16)
            acc_ref[...] = jnp.zeros_like(acc_ref)

        hb = h_ref[...]
        g = _dot(hb, wg_ref[...])
        up = _dot(hb, wu_ref[...])
        a_ref[0] = g.astype(BF16)
        a_ref[1] = up.astype(BF16)
        act = (g * jax.nn.sigmoid(g)) * up
        acc_ref[...] += _dot(act.astype(BF16), wo_ref[...])

        @pl.when(j == 1)
        def _():
            u = acc_ref[...]
            u_ref[...] = u
            uhat, _ = _rms(u)
            xo_ref[...] = x_ref[...] + (weight * (1.0 + vec_ref[4:5, :])) * (uhat * vec_ref[1:2, :])

    return pl.pallas_call(
        body, name="ffn_fwd", grid=(S // tm, 2),
        in_specs=[pl.BlockSpec((tm, D), lambda i, j: (i, 0)),
                  pl.BlockSpec((8, D), lambda i, j: (0, 0)),
                  pl.BlockSpec((None, D, FSH), lambda i, j: (j, 0, 0)),
                  pl.BlockSpec((None, D, FSH), lambda i, j: (j + 2, 0, 0)),
                  pl.BlockSpec((None, FSH, D), lambda i, j: (j, 0, 0))],
        out_specs=[pl.BlockSpec((tm, D), lambda i, j: (i, 0)),
                   pl.BlockSpec((2, tm, FSH), lambda i, j: (0, i, j)),
                   pl.BlockSpec((tm, D), lambda i, j: (i, 0)),
                   pl.BlockSpec((tm, D), lambda i, j: (i, 0))],
        out_shape=[jax.ShapeDtypeStruct((S, D), F32), jax.ShapeDtypeStruct((2, S, DFF), BF16),
                   jax.ShapeDtypeStruct((S, D), F32), jax.ShapeDtypeStruct((S, D), BF16)],
        scratch_shapes=[pltpu.VMEM((tm, D), F32)],
        compiler_params=_params("parallel", "arbitrary"),
    )(x, vec, w_in, w_in, w_out)


def ffn_bwd(dout, x, u, a, vec, w_in, w_out, weight):
    S = x.shape[0]
    tm = min(256, S)

    def body(do_ref, x_ref, u_ref, a_ref, vec_ref, wg_ref, wu_ref, wo_ref,
             dx_ref, du_ref, act_ref, da_ref, vg_ref, dh_ref):
        i, j = pl.program_id(0), pl.program_id(1)

        @pl.when((i == 0) & (j == 0))
        def _():
            vg_ref[...] = jnp.zeros_like(vg_ref)

        @pl.when(j == 0)
        def _():
            du, dgate_rows, dgpost_rows = _postnorm_bwd(do_ref[...], u_ref[...], vec_ref, weight)
            vg_ref[2:3, :] += jnp.sum(dgate_rows, axis=0, keepdims=True)
            vg_ref[4:5, :] += jnp.sum(dgpost_rows, axis=0, keepdims=True)
            du_ref[...] = du.astype(BF16)
            dh_ref[...] = jnp.zeros_like(dh_ref)

        dact = _dot(du_ref[...], wo_ref[...], NT)
        g = a_ref[0].astype(F32)
        up = a_ref[1].astype(F32)
        s = jax.nn.sigmoid(g)
        silu = g * s
        act_ref[...] = (silu * up).astype(BF16)
        dg = (dact * up * (s * (1.0 + g * (1.0 - s)))).astype(BF16)
        dup = (dact * silu).astype(BF16)
        da_ref[0] = dg
        da_ref[1] = dup
        dh_ref[...] += _dot(dg, wg_ref[...], NT) + _dot(dup, wu_ref[...], NT)

        @pl.when(j == 1)
        def _():
            dx_ref[...] = do_ref[...] + _prenorm_bwd(dh_ref[...], x_ref[...], vec_ref, vg_ref)

    row = lambda i, j: (i, 0)
    return pl.pallas_call(
        body, name="ffn_bwd", grid=(S // tm, 2),
        in_specs=[pl.BlockSpec((tm, D), row), pl.BlockSpec((tm, D), row), pl.BlockSpec((tm, D), row),
                  pl.BlockSpec((2, tm, FSH), lambda i, j: (0, i, j)),
                  pl.BlockSpec((8, D), lambda i, j: (0, 0)),
                  pl.BlockSpec((None, D, FSH), lambda i, j: (j, 0, 0)),
                  pl.BlockSpec((None, D, FSH), lambda i, j: (j + 2, 0, 0)),
                  pl.BlockSpec((None, FSH, D), lambda i, j: (j, 0, 0))],
        out_specs=[pl.BlockSpec((tm, D), row), pl.BlockSpec((tm, D), row),
                   pl.BlockSpec((tm, FSH), lambda i, j: (i, j)),
                   pl.BlockSpec((2, tm, FSH), lambda i, j: (0, i, j)),
                   pl.BlockSpec((8, D), lambda i, j: (0, 0))],
        out_shape=[jax.ShapeDtypeStruct((S, D), F32), jax.ShapeDtypeStruct((S, D), BF16),
                   jax.ShapeDtypeStruct((S, DFF), BF16), jax.ShapeDtypeStruct((2, S, DFF), BF16),
                   jax.ShapeDtypeStruct((8, D), F32)],
        scratch_shapes=[pltpu.VMEM((tm, D), F32)],
        compiler_params=_params("arbitrary", "arbitrary"),
    )(dout, x, u, a, vec, w_in, w_in, w_out)


def dw_matmul(name, a, b, a_spec, b_spec, out_shape, out_spec, grid):
    def body(a_ref, b_ref, o_ref):
        @pl.when(pl.program_id(len(grid) - 1) == 0)
        def _():
            o_ref[...] = jnp.zeros_like(o_ref)

        o_ref[...] += _dot(a_ref[...], b_ref[...], TN)

    return pl.pallas_call(
        body, name=name, grid=grid, in_specs=[a_spec, b_spec], out_specs=out_spec,
        out_shape=jax.ShapeDtypeStruct(out_shape, F32),
        compiler_params=_params(*(["parallel"] * (len(grid) - 1) + ["arbitrary"])),
    )(a, b)


def ffn_dw(h, da, act, du):
    S = h.shape[0]
    tk = min(512, S)
    dw_in = dw_matmul("ffn_dw_in", h, da,
                      pl.BlockSpec((tk, D), lambda n, k: (k, 0)),
                      pl.BlockSpec((None, tk, FSH), lambda n, k: (n // 2, k, n % 2)),
                      (N_CHIP, D, FSH), pl.BlockSpec((None, D, FSH), lambda n, k: (n, 0, 0)),
                      (N_CHIP, S // tk))
    dw_out = dw_matmul("ffn_dw_out", act, du,
                       pl.BlockSpec((tk, FSH), lambda n, k: (k, n)),
                       pl.BlockSpec((tk, D), lambda n, k: (k, 0)),
                       (DFF, D), pl.BlockSpec((FSH, D), lambda n, k: (n, 0)),
                       (2, S // tk))
    return dw_in, dw_out


def _halo_specs(tm, S):
    nb = tm // HALO
    last = S // HALO - 1
    return [pl.BlockSpec((HALO, D), lambda i: (jnp.maximum(i * nb - 1, 0), 0)),
            pl.BlockSpec((tm, D), lambda i: (i, 0)),
            pl.BlockSpec((HALO, D), lambda i: (jnp.minimum((i + 1) * nb, last), 0))]


def _shift_rows(v, k):
    return pltpu.roll(v, k % v.shape[0], 0)


def _window_sum(v, g, forward):
    acc = v + _shift_rows(v, 1 if forward else -1)
    for step in (1, 2, 4)[:g]:
        acc = _shift_rows(acc, step) + _shift_rows(acc, -step)
    return acc


def _pool_count(t, w, S):
    return jnp.maximum(jnp.minimum(t + w // 2, S) - jnp.maximum(t - w // 2, 0), 1).astype(F32)


def pool_fwd(x, vec, pw, pvec):
    S = x.shape[0]
    tm = min(256, S)
    G = D // 4

    def body(xp_ref, x_ref, xn_ref, vec_ref, pw_ref, pv_ref, xo_ref, y_ref, z_ref):
        i = pl.program_id(0)
        xa = jnp.concatenate([xp_ref[...], x_ref[...], xn_ref[...]], axis=0)
        t = i * tm - HALO + lax.broadcasted_iota(jnp.int32, (tm + 2 * HALO, 1), 0)
        h, _, _ = _prenorm(xa, vec_ref)
        h = jnp.where((t >= 0) & (t < S), h, 0.0)
        tmain = t[HALO:HALO + tm]
        for g in range(4):
            hg = h[:, g * G:(g + 1) * G]
            pooled = _window_sum(hg, g, True)[HALO:HALO + tm] / _pool_count(tmain, POOL_WINDOWS[g], S)
            z = (pooled - hg[HALO:HALO + tm]).astype(BF16)
            z_ref[:, g * G:(g + 1) * G] = z
            y_ref[:, g * G:(g + 1) * G] = _dot(z, pw_ref[g]) + pv_ref[0:1, g * G:(g + 1) * G]
        u = y_ref[...] * pv_ref[1:2, :]
        uhat, _ = _rms(u)
        xo_ref[...] = x_ref[...] + (1.0 + vec_ref[4:5, :]) * (uhat * vec_ref[1:2, :])

    row = lambda i: (i, 0)
    full = lambda i: (0, 0)
    return pl.pallas_call(
        body, name="pool_fwd", grid=(S // tm,),
        in_specs=_halo_specs(tm, S) + [pl.BlockSpec((8, D), full), pl.BlockSpec((4, G, G), lambda i: (0, 0, 0)),
                                       pl.BlockSpec((8, D), full)],
        out_specs=[pl.BlockSpec((tm, D), row)] * 3,
        out_shape=[jax.ShapeDtypeStruct((S, D), F32), jax.ShapeDtypeStruct((S, D), F32),
                   jax.ShapeDtypeStruct((S, D), BF16)],
        compiler_params=_params("parallel"),
    )(x, x, x, vec, pw, pvec)


def pool_bwd(dout, x, y, z, vec, pw, pvec):
    S = x.shape[0]
    tm = min(256, S)
    G = D // 4
    R = G // N_CHIP

    def body(dop_ref, do_ref, don_ref, yp_ref, y_ref, yn_ref, x_ref, z_ref, vec_ref, pw_ref, pv_ref,
             dx_ref, vg_ref, pg_ref, dw_ref, dh_ref):
        i = pl.program_id(0)

        @pl.when(i == 0)
        def _():
            vg_ref[...] = jnp.zeros_like(vg_ref)
            pg_ref[...] = jnp.zeros_like(pg_ref)
            dw_ref[...] = jnp.zeros_like(dw_ref)

        doa = jnp.concatenate([dop_ref[...], do_ref[...], don_ref[...]], axis=0)
        ya = jnp.concatenate([yp_ref[...], y_ref[...], yn_ref[...]], axis=0)
        t = i * tm - HALO + lax.broadcasted_iota(jnp.int32, (tm + 2 * HALO, 1), 0)
        inside = (t >= 0) & (t < S)
        main = (t >= i * tm) & (t < (i + 1) * tm)
        du, dgate_rows, dgpost_rows = _postnorm_bwd(doa, ya * pv_ref[1:2, :], vec_ref, 1.0)
        du = jnp.where(inside, du, 0.0)
        vg_ref[2:3, :] += jnp.sum(jnp.where(main, dgate_rows, 0.0), axis=0, keepdims=True)
        vg_ref[4:5, :] += jnp.sum(jnp.where(main, dgpost_rows, 0.0), axis=0, keepdims=True)
        dy = du * pv_ref[1:2, :]
        pg_ref[0:1, :] += jnp.sum(jnp.where(main, dy, 0.0), axis=0, keepdims=True)
        pg_ref[1:2, :] += jnp.sum(jnp.where(main, du * ya, 0.0), axis=0, keepdims=True)
        for g in range(4):
            dyg = dy[:, g * G:(g + 1) * G].astype(BF16)
            dz = _dot(dyg, pw_ref[g], NT)
            e = dz / _pool_count(t, POOL_WINDOWS[g], S)
            dh_ref[:, g * G:(g + 1) * G] = (_window_sum(e, g, False) - dz)[HALO:HALO + tm]
            dwg = _dot(z_ref[:, g * G:(g + 1) * G], dyg[HALO:HALO + tm], TN)
            for q in range(N_CHIP):
                dw_ref[q, g] += dwg[q * R:(q + 1) * R, :]
        dx_ref[...] = do_ref[...] + _prenorm_bwd(dh_ref[...], x_ref[...], vec_ref, vg_ref)

    row = lambda i: (i, 0)
    full = lambda i: (0, 0)
    halo = _halo_specs(tm, S)
    return pl.pallas_call(
        body, name="pool_bwd", grid=(S // tm,),
        in_specs=halo + halo + [pl.BlockSpec((tm, D), row), pl.BlockSpec((tm, D), row), pl.BlockSpec((8, D), full),
                                pl.BlockSpec((4, G, G), lambda i: (0, 0, 0)), pl.BlockSpec((8, D), full)],
        out_specs=[pl.BlockSpec((tm, D), row), pl.BlockSpec((8, D), full), pl.BlockSpec((8, D), full),
                   pl.BlockSpec((N_CHIP, 4, R, G), lambda i: (0, 0, 0, 0))],
        out_shape=[jax.ShapeDtypeStruct((S, D), F32), jax.ShapeDtypeStruct((8, D), F32),
                   jax.ShapeDtypeStruct((8, D), F32), jax.ShapeDtypeStruct((N_CHIP, 4, R, G), F32)],
        scratch_shapes=[pltpu.VMEM((tm, D), F32)],
        compiler_params=_params("arbitrary"),
    )(dout, dout, dout, y, y, y, x, z, vec, pw, pvec)


def _w3(shape):
    return pl.BlockSpec(shape, lambda i: (0,) * len(shape))


def mla_pre(x, vec, mw, tabs):
    S = x.shape[0]
    tm = min(256, S)

    def body(x_ref, vec_ref, cos_ref, sin_ref, wq_ref, wkv_ref, wkr_ref, wkrs_ref, qn_ref, kvn_ref,
             wn_ref, wr_ref, wrs_ref, wuk_ref,
             h_ref, cq_ref, ckv_ref, cqn_ref, qnope_ref, qcat_ref, kcat_ref, vcat_ref):
        h, _, _ = _prenorm(x_ref[...], vec_ref)
        hb = h.astype(BF16)
        h_ref[...] = hb
        cq_raw = _dot(hb, wq_ref[...])
        ckv_raw = _dot(hb, wkv_ref[...])
        cq_ref[...] = cq_raw
        ckv_ref[...] = ckv_raw
        cos, sin = cos_ref[...], sin_ref[...]
        k_rope = _dot(hb, wkr_ref[...]) * cos + _dot(hb, wkrs_ref[...]) * sin
        ckv = _rms(ckv_raw)[0] * kvn_ref[...]
        kcat_ref[:, 0:KVL] = ckv.astype(BF16)
        vcat_ref[:, 0:KVL] = ckv.astype(BF16)
        ones = lax.broadcasted_iota(jnp.int32, (tm, QPAD - KVL), 1) == 0
        vcat_ref[:, KVL:] = jnp.where(ones, 1.0, 0.0).astype(BF16)
        kcat_ref[:, KVL:KVL + ROPE] = k_rope.astype(BF16)
        kcat_ref[:, KVL + ROPE:] = jnp.zeros((tm, QPAD - KVL - ROPE), BF16)
        cqb = (_rms(cq_raw)[0] * qn_ref[...]).astype(BF16)
        cqn_ref[...] = cqb
        for hd in range(N_HEADS):
            qn = _dot(cqb, wn_ref[hd]).astype(BF16)
            qnope_ref[hd] = qn
            qcat_ref[hd, :, 0:KVL] = (_dot(qn, wuk_ref[hd], NT) * ATTN_SCALE).astype(BF16)
            qr = (_dot(cqb, wr_ref[hd]) * cos + _dot(cqb, wrs_ref[hd]) * sin) * ATTN_SCALE
            qcat_ref[hd, :, KVL:KVL + ROPE] = qr.astype(BF16)
            qcat_ref[hd, :, KVL + ROPE:] = jnp.zeros((tm, QPAD - KVL - ROPE), BF16)

    row = lambda i: (i, 0)
    hrow = lambda i: (0, i, 0)
    return pl.pallas_call(
        body, name="mla_pre", grid=(S // tm,),
        in_specs=[pl.BlockSpec((tm, D), row), _w3((8, D)), pl.BlockSpec((tm, ROPE), row), pl.BlockSpec((tm, ROPE), row),
                  _w3((D, QL)), _w3((D, KVL)), _w3((D, ROPE)), _w3((D, ROPE)), _w3((1, QL)), _w3((1, KVL)),
                  _w3((N_HEADS, QL, NOPE)), _w3((N_HEADS, QL, ROPE)), _w3((N_HEADS, QL, ROPE)),
                  _w3((N_HEADS, KVL, NOPE))],
        out_specs=[pl.BlockSpec((tm, D), row), pl.BlockSpec((tm, QL), row), pl.BlockSpec((tm, KVL), row),
                   pl.BlockSpec((tm, QL), row), pl.BlockSpec((N_HEADS, tm, NOPE), hrow),
                   pl.BlockSpec((N_HEADS, tm, QPAD), hrow), pl.BlockSpec((tm, QPAD), row),
                   pl.BlockSpec((tm, QPAD), row)],
        out_shape=[jax.ShapeDtypeStruct((S, D), BF16), jax.ShapeDtypeStruct((S, QL), F32),
                   jax.ShapeDtypeStruct((S, KVL), F32), jax.ShapeDtypeStruct((S, QL), BF16),
                   jax.ShapeDtypeStruct((N_HEADS, S, NOPE), BF16), jax.ShapeDtypeStruct((N_HEADS, S, QPAD), BF16),
                   jax.ShapeDtypeStruct((S, QPAD), BF16), jax.ShapeDtypeStruct((S, QPAD), BF16)],
        compiler_params=_params("parallel"),
    )(x, vec, tabs[0], tabs[1], mw["wq"], mw["wkv"], mw["wkr"], mw["wkrs"], mw["qn"], mw["kvn"],
      mw["wn"], mw["wr"], mw["wrs"], mw["wuk"])


def attn_fwd(qcat, kcat, vcat):
    S = kcat.shape[0]
    tq = min(256, S)

    def body(q_ref, k_ref, v_ref, o_ref, lse_ref):
        s = _dot(q_ref[...], k_ref[...], NT)
        m = jnp.max(s, axis=-1, keepdims=True)
        p = jnp.exp(s - m).astype(BF16)
        ov = _dot(p, v_ref[...])
        l = ov[:, KVL:KVL + 1]
        o_ref[...] = (ov[:, 0:KVL] * (1.0 / l)).astype(BF16)
        lse_ref[...] = m + jnp.log(l)

    return pl.pallas_call(
        body, name="attn_fwd", grid=(N_HEADS, S // tq),
        in_specs=[pl.BlockSpec((None, tq, QPAD), lambda h, i: (h, i, 0)),
                  pl.BlockSpec((S, QPAD), lambda h, i: (0, 0)),
                  pl.BlockSpec((S, QPAD), lambda h, i: (0, 0))],
        out_specs=[pl.BlockSpec((None, tq, KVL), lambda h, i: (h, i, 0)),
                   pl.BlockSpec((None, tq, 1), lambda h, i: (h, i, 0))],
        out_shape=[jax.ShapeDtypeStruct((N_HEADS, S, KVL), BF16), jax.ShapeDtypeStruct((N_HEADS, S, 1), F32)],
        compiler_params=_params("parallel", "parallel"),
    )(qcat, kcat, vcat)


def mla_post(olat, x, vec, wuv, wo):
    S = x.shape[0]
    tm = min(256, S)

    def body(o_ref, x_ref, vec_ref, wuv_ref, wo_ref, xo_ref, u_ref, ocat_ref):
        u = jnp.zeros((tm, D), F32)
        for hd in range(N_HEADS):
            oc = _dot(o_ref[hd], wuv_ref[hd]).astype(BF16)
            ocat_ref[hd] = oc
            u = u + _dot(oc, wo_ref[hd])
        u_ref[...] = u
        uhat, _ = _rms(u)
        xo_ref[...] = x_ref[...] + (1.0 + vec_ref[4:5, :]) * (uhat * vec_ref[1:2, :])

    row = lambda i: (i, 0)
    hrow = lambda i: (0, i, 0)
    return pl.pallas_call(
        body, name="mla_post", grid=(S // tm,),
        in_specs=[pl.BlockSpec((N_HEADS, tm, KVL), hrow), pl.BlockSpec((tm, D), row), _w3((8, D)),
                  _w3((N_HEADS, KVL, VH)), _w3((N_HEADS, VH, D))],
        out_specs=[pl.BlockSpec((tm, D), row), pl.BlockSpec((tm, D), row), pl.BlockSpec((N_HEADS, tm, VH), hrow)],
        out_shape=[jax.ShapeDtypeStruct((S, D), F32), jax.ShapeDtypeStruct((S, D), F32),
                   jax.ShapeDtypeStruct((N_HEADS, S, VH), BF16)],
        compiler_params=_params("parallel"),
    )(olat, x, vec, wuv, wo)


def mla_post_bwd(dout, u, olat, vec, wuv, wo):
    S = u.shape[0]
    tm = min(256, S)

    def body(do_ref, u_ref, o_ref, vec_ref, wuv_ref, wo_ref, du_ref, docat_ref, dolat_ref, delta_ref, vg_ref):
        @pl.when(pl.program_id(0) == 0)
        def _():
            vg_ref[...] = jnp.zeros_like(vg_ref)

        du, dgate_rows, dgpost_rows = _postnorm_bwd(do_ref[...], u_ref[...], vec_ref, 1.0)
        vg_ref[2:3, :] += jnp.sum(dgate_rows, axis=0, keepdims=True)
        vg_ref[4:5, :] += jnp.sum(dgpost_rows, axis=0, keepdims=True)
        dub = du.astype(BF16)
        du_ref[...] = dub
        for hd in range(N_HEADS):
            doc = _dot(dub, wo_ref[hd], NT).astype(BF16)
            docat_ref[hd] = doc
            dol = _dot(doc, wuv_ref[hd], NT).astype(BF16)
            dolat_ref[hd] = dol
            delta_ref[hd] = jnp.sum(dol.astype(F32) * o_ref[hd].astype(F32), axis=-1, keepdims=True)

    row = lambda i: (i, 0)
    hrow = lambda i: (0, i, 0)
    return pl.pallas_call(
        body, name="mla_post_bwd", grid=(S // tm,),
        in_specs=[pl.BlockSpec((tm, D), row), pl.BlockSpec((tm, D), row), pl.BlockSpec((N_HEADS, tm, KVL), hrow),
                  _w3((8, D)), _w3((N_HEADS, KVL, VH)), _w3((N_HEADS, VH, D))],
        out_specs=[pl.BlockSpec((tm, D), row), pl.BlockSpec((N_HEADS, tm, VH), hrow),
                   pl.BlockSpec((N_HEADS, tm, KVL), hrow), pl.BlockSpec((N_HEADS, tm, 1), hrow), _w3((8, D))],
        out_shape=[jax.ShapeDtypeStruct((S, D), BF16), jax.ShapeDtypeStruct((N_HEADS, S, VH), BF16),
                   jax.ShapeDtypeStruct((N_HEADS, S, KVL), BF16), jax.ShapeDtypeStruct((N_HEADS, S, 1), F32),
                   jax.ShapeDtypeStruct((8, D), F32)],
        compiler_params=_params("arbitrary"),
    )(dout, u, olat, vec, wuv, wo)


def attn_bwd(qcat, kcat, kcat_t, dolat, lse_row, delta_row):
    S = kcat.shape[0]
    tq = min(256, S)

    def body(q_ref, k_ref, kt_ref, do_ref, lse_ref, dl_ref, dq_ref, dk_ref, dv_ref):
        @pl.when((pl.program_id(0) == 0) & (pl.program_id(1) == 0))
        def _():
            dk_ref[...] = jnp.zeros_like(dk_ref)
            dv_ref[...] = jnp.zeros_like(dv_ref)

        q, do = q_ref[...], do_ref[...]
        st = _dot(k_ref[...], q, NT)
        pt = jnp.exp(st - lse_ref[...])
        dpt = _dot(k_ref[:, 0:KVL], do, NT)
        dst = (pt * (dpt - dl_ref[...])).astype(BF16)
        dv_ref[...] += _dot(pt.astype(BF16), do)
        dk_ref[...] += _dot(dst, q)
        dq_ref[...] = _dot(kt_ref[...], dst).T

    return pl.pallas_call(
        body, name="attn_bwd", grid=(N_HEADS, S // tq),
        in_specs=[pl.BlockSpec((None, tq, QPAD), lambda h, i: (h, i, 0)),
                  pl.BlockSpec((S, QPAD), lambda h, i: (0, 0)),
                  pl.BlockSpec((QPAD, S), lambda h, i: (0, 0)),
                  pl.BlockSpec((None, tq, KVL), lambda h, i: (h, i, 0)),
                  pl.BlockSpec((None, 1, tq), lambda h, i: (h, 0, i)),
                  pl.BlockSpec((None, 1, tq), lambda h, i: (h, 0, i))],
        out_specs=[pl.BlockSpec((None, tq, QPAD), lambda h, i: (h, i, 0)),
                   pl.BlockSpec((S, QPAD), lambda h, i: (0, 0)),
                   pl.BlockSpec((S, KVL), lambda h, i: (0, 0))],
        out_shape=[jax.ShapeDtypeStruct((N_HEADS, S, QPAD), F32), jax.ShapeDtypeStruct((S, QPAD), F32),
                   jax.ShapeDtypeStruct((S, KVL), F32)],
        compiler_params=_params("arbitrary", "arbitrary"),
    )(qcat, kcat, kcat_t, dolat, lse_row, delta_row)


def mla_pre_bwd(dout, dq, dk, dv, x, cq_raw, ckv_raw, vec, mw, tabs):
    S = x.shape[0]
    tm = min(256, S)

    def body(do_ref, dq_ref, dk_ref, dv_ref, x_ref, cq_ref, ckv_ref, vec_ref, cos_ref, sin_ref,
             wq_ref, wkv_ref, wkr_ref, wkrs_ref, qn_ref, kvn_ref, wn_ref, wr_ref, wrs_ref, wuk_ref,
             dx_ref, dlat_ref, dka_ref, dkb_ref, dqn_ref, dql_ref, dqa_ref, dqb_ref, vg_ref, ng_ref):
        @pl.when(pl.program_id(0) == 0)
        def _():
            vg_ref[...] = jnp.zeros_like(vg_ref)
            ng_ref[...] = jnp.zeros_like(ng_ref)

        cos, sin = cos_ref[...], sin_ref[...]
        dcq = jnp.zeros((tm, QL), F32)
        for hd in range(N_HEADS):
            dql = (dq_ref[hd, :, 0:KVL] * ATTN_SCALE).astype(BF16)
            dql_ref[hd] = dql
            dqn = _dot(dql, wuk_ref[hd]).astype(BF16)
            dqn_ref[hd] = dqn
            dqr = dq_ref[hd, :, KVL:KVL + ROPE] * ATTN_SCALE
            qa = (dqr * cos).astype(BF16)
            qb = (dqr * sin).astype(BF16)
            dqa_ref[hd] = qa
            dqb_ref[hd] = qb
            dcq = dcq + _dot(dqn, wn_ref[hd], NT) + _dot(qa, wr_ref[hd], NT) + _dot(qb, wrs_ref[hd], NT)
        cqh, rq = _rms(cq_ref[...])
        ng_ref[0:1, :] += jnp.sum(dcq * cqh, axis=0, keepdims=True)
        dcq_raw = _rms_bwd(cqh, rq, dcq * qn_ref[...]).astype(BF16)
        dckv = dk_ref[:, 0:KVL] + dv_ref[...]
        ckvh, rk = _rms(ckv_ref[...])
        ng_ref[1:2, 0:KVL] += jnp.sum(dckv * ckvh, axis=0, keepdims=True)
        dckv_raw = _rms_bwd(ckvh, rk, dckv * kvn_ref[...]).astype(BF16)
        dkr = dk_ref[:, KVL:KVL + ROPE]
        ka = (dkr * cos).astype(BF16)
        kb = (dkr * sin).astype(BF16)
        dlat_ref[:, 0:QL] = dcq_raw
        dlat_ref[:, QL:QL + KVL] = dckv_raw
        dka_ref[...] = ka
        dkb_ref[...] = kb
        dh = (_dot(dcq_raw, wq_ref[...], NT) + _dot(dckv_raw, wkv_ref[...], NT)
              + _dot(ka, wkr_ref[...], NT) + _dot(kb, wkrs_ref[...], NT))
        dx_ref[...] = do_ref[...] + _prenorm_bwd(dh, x_ref[...], vec_ref, vg_ref)

    row = lambda i: (i, 0)
    hrow = lambda i: (0, i, 0)
    return pl.pallas_call(
        body, name="mla_pre_bwd", grid=(S // tm,),
        in_specs=[pl.BlockSpec((tm, D), row), pl.BlockSpec((N_HEADS, tm, QPAD), hrow), pl.BlockSpec((tm, QPAD), row),
                  pl.BlockSpec((tm, KVL), row), pl.BlockSpec((tm, D), row), pl.BlockSpec((tm, QL), row),
                  pl.BlockSpec((tm, KVL), row), _w3((8, D)), pl.BlockSpec((tm, ROPE), row), pl.BlockSpec((tm, ROPE), row),
                  _w3((D, QL)), _w3((D, KVL)), _w3((D, ROPE)), _w3((D, ROPE)), _w3((1, QL)), _w3((1, KVL)),
                  _w3((N_HEADS, QL, NOPE)), _w3((N_HEADS, QL, ROPE)), _w3((N_HEADS, QL, ROPE)),
                  _w3((N_HEADS, KVL, NOPE))],
        out_specs=[pl.BlockSpec((tm, D), row), pl.BlockSpec((tm, QL + KVL), row), pl.BlockSpec((tm, ROPE), row),
                   pl.BlockSpec((tm, ROPE), row), pl.BlockSpec((N_HEADS, tm, NOPE), hrow),
                   pl.BlockSpec((N_HEADS, tm, KVL), hrow), pl.BlockSpec((N_HEADS, tm, ROPE), hrow),
                   pl.BlockSpec((N_HEADS, tm, ROPE), hrow), _w3((8, D)), _w3((8, QL))],
        out_shape=[jax.ShapeDtypeStruct((S, D), F32), jax.ShapeDtypeStruct((S, QL + KVL), BF16),
                   jax.ShapeDtypeStruct((S, ROPE), BF16), jax.ShapeDtypeStruct((S, ROPE), BF16),
                   jax.ShapeDtypeStruct((N_HEADS, S, NOPE), BF16), jax.ShapeDtypeStruct((N_HEADS, S, KVL), BF16),
                   jax.ShapeDtypeStruct((N_HEADS, S, ROPE), BF16), jax.ShapeDtypeStruct((N_HEADS, S, ROPE), BF16),
                   jax.ShapeDtypeStruct((8, D), F32), jax.ShapeDtypeStruct((8, QL), F32)],
        compiler_params=_params("arbitrary"),
    )(dout, dq, dk, dv, x, cq_raw, ckv_raw, vec, tabs[0], tabs[1], mw["wq"], mw["wkv"], mw["wkr"], mw["wkrs"],
      mw["qn"], mw["kvn"], mw["wn"], mw["wr"], mw["wrs"], mw["wuk"])


def mla_dw(h, dlat, dka, dkb, cqn, dqn, dqa, dqb, dql, qnope, olat, docat, ocat, du):
    S = h.shape[0]
    tk = min(512, S)
    nk = S // tk
    flat_a = lambda w: pl.BlockSpec((tk, w), lambda k: (k, 0))
    head_a = lambda w: pl.BlockSpec((None, tk, w), lambda n, k: (n, k, 0))
    shared = lambda w: pl.BlockSpec((tk, w), lambda n, k: (k, 0))
    head_o = lambda r, c: pl.BlockSpec((None, r, c), lambda n, k: (n, 0, 0))
    g = {}
    g["in"] = dw_matmul("mla_dw_in", h, dlat, flat_a(D), flat_a(QL + KVL), (D, QL + KVL),
                        pl.BlockSpec((D, QL + KVL), lambda k: (0, 0)), (nk,))
    g["kr"] = dw_matmul("mla_dw_kr", h, dka, flat_a(D), flat_a(ROPE), (D, ROPE),
                        pl.BlockSpec((D, ROPE), lambda k: (0, 0)), (nk,))
    g["krs"] = dw_matmul("mla_dw_krs", h, dkb, flat_a(D), flat_a(ROPE), (D, ROPE),
                         pl.BlockSpec((D, ROPE), lambda k: (0, 0)), (nk,))
    g["n"] = dw_matmul("mla_dw_n", cqn, dqn, shared(QL), head_a(NOPE), (N_HEADS, QL, NOPE), head_o(QL, NOPE),
                       (N_HEADS, nk))
    g["r"] = dw_matmul("mla_dw_r", cqn, dqa, shared(QL), head_a(ROPE), (N_HEADS, QL, ROPE), head_o(QL, ROPE),
                       (N_HEADS, nk))
    g["rs"] = dw_matmul("mla_dw_rs", cqn, dqb, shared(QL), head_a(ROPE), (N_HEADS, QL, ROPE), head_o(QL, ROPE),
                        (N_HEADS, nk))
    g["uk"] = dw_matmul("mla_dw_uk", dql, qnope, head_a(KVL), head_a(NOPE), (N_HEADS, KVL, NOPE), head_o(KVL, NOPE),
                        (N_HEADS, nk))
    g["uv"] = dw_matmul("mla_dw_uv", olat, docat, head_a(KVL), head_a(VH), (N_HEADS, KVL, VH), head_o(KVL, VH),
                        (N_HEADS, nk))
    g["o"] = dw_matmul("mla_dw_o", ocat, du, head_a(VH), shared(D), (N_HEADS, VH, D), head_o(VH, D),
                       (N_HEADS, nk))
    return g


def loss_head(y, target):
    S = y.shape[0]
    tm = min(512, S)

    def body(y_ref, t_ref, loss_ref, dy_ref):
        @pl.when(pl.program_id(0) == 0)
        def _():
            loss_ref[...] = jnp.zeros_like(loss_ref)

        err = y_ref[...] - t_ref[...]
        dy_ref[...] = err * (1.0 / D)
        loss_ref[...] += 0.5 * jnp.sum(jnp.mean(err * err, axis=-1, keepdims=True), axis=0, keepdims=True)

    row = lambda i: (i, 0)
    return pl.pallas_call(
        body, name="loss_head", grid=(S // tm,),
        in_specs=[pl.BlockSpec((tm, D), row), pl.BlockSpec((tm, D), row)],
        out_specs=[pl.BlockSpec((1, 1), lambda i: (0, 0)), pl.BlockSpec((tm, D), row)],
        out_shape=[jax.ShapeDtypeStruct((1, 1), F32), jax.ShapeDtypeStruct((S, D), F32)],
        compiler_params=_params("arbitrary"),
    )(y, target)


MOD_COLS = 9 * D // N_CHIP


def mod_fwd(c_pad, ada_w, ada_b_loc):
    tn = MOD_COLS // 3

    def body(c_ref, w_ref, b_ref, o_ref):
        c = c_ref[...]
        sc = (c * jax.nn.sigmoid(c)).astype(BF16)
        o_ref[...] = _dot(sc, w_ref[...].astype(BF16)) + b_ref[...]

    return pl.pallas_call(
        body, name="mod_fwd", grid=(2, 3),
        in_specs=[pl.BlockSpec((16, D), lambda i, n: (0, 0)), pl.BlockSpec((None, D, tn), lambda i, n: (i, 0, n)),
                  pl.BlockSpec((None, 1, tn), lambda i, n: (i, 0, n))],
        out_specs=pl.BlockSpec((None, 16, tn), lambda i, n: (i, 0, n)),
        out_shape=jax.ShapeDtypeStruct((2, 16, MOD_COLS), F32),
        compiler_params=_params("parallel", "parallel"),
    )(c_pad, ada_w, ada_b_loc)


def _adamw_math(w, g, m, v):
    m = ADAM_B1 * m + (1.0 - ADAM_B1) * g
    v = ADAM_B2 * v + (1.0 - ADAM_B2) * (g * g)
    m_hat = m / (1.0 - ADAM_B1 ** ADAM_STEP)
    v_hat = v / (1.0 - ADAM_B2 ** ADAM_STEP)
    delta = -ADAM_LR * (m_hat / (jnp.sqrt(v_hat) + ADAM_EPS) + ADAM_WD * w)
    return delta, m, v


def adamw(name, w, g, m, v):
    shape = w.shape
    cols = shape[-1]
    rows = w.size // cols
    tr = rows
    for cand in (512, 256, 128, 64, 32, 16, 8):
        if rows % cand == 0 and cand * cols * 4 <= (2 << 20):
            tr = cand
            break

    def body(w_ref, g_ref, m_ref, v_ref, d_ref, mo_ref, vo_ref):
        d_ref[...], mo_ref[...], vo_ref[...] = _adamw_math(w_ref[...], g_ref[...], m_ref[...], v_ref[...])

    spec = pl.BlockSpec((tr, cols), lambda i: (i, 0))
    outs = pl.pallas_call(
        body, name=name, grid=(rows // tr,), in_specs=[spec] * 4, out_specs=[spec] * 3,
        out_shape=[jax.ShapeDtypeStruct((rows, cols), F32)] * 3,
        compiler_params=_params("parallel"),
    )(*[a.reshape(rows, cols) for a in (w, g, m, v)])
    return [o.reshape(shape) for o in outs]


def adamw_ada(c_pad, dmod, w, m, v):
    tr = 256

    def body(c_ref, dm_ref, w_ref, m_ref, v_ref, g_ref, d_ref, mo_ref, vo_ref):
        c = c_ref[...]
        sc = (c * jax.nn.sigmoid(c)).astype(BF16)
        g = _dot(sc, dm_ref[...].astype(BF16), TN)
        g_ref[...] = g
        d_ref[...], mo_ref[...], vo_ref[...] = _adamw_math(w_ref[...], g, m_ref[...], v_ref[...])

    wspec = pl.BlockSpec((None, tr, MOD_COLS), lambda i, r: (i, r, 0))
    return pl.pallas_call(
        body, name="adamw_ada", grid=(2, D // tr),
        in_specs=[pl.BlockSpec((16, tr), lambda i, r: (0, r)),
                  pl.BlockSpec((None, 16, MOD_COLS), lambda i, r: (i, 0, 0)), wspec, wspec, wspec],
        out_specs=[wspec] * 4,
        out_shape=[jax.ShapeDtypeStruct((2, D, MOD_COLS), F32)] * 4,
        compiler_params=_params("parallel", "parallel"),
    )(c_pad, dmod, w, m, v)


def sum_devices(name, a):
    _, R, C = a.shape
    tr = R
    for cand in (64, 32, 16, 8):
        if R % cand == 0:
            tr = cand
            break

    def body(a_ref, o_ref):
        acc = a_ref[0]
        for dev in range(1, N_DEV):
            acc = acc + a_ref[dev]
        o_ref[...] = acc

    return pl.pallas_call(
        body, name=name, grid=(R // tr,),
        in_specs=[pl.BlockSpec((N_DEV, tr, C), lambda i: (0, i, 0))],
        out_specs=pl.BlockSpec((tr, C), lambda i: (i, 0)),
        out_shape=jax.ShapeDtypeStruct((R, C), F32),
        compiler_params=_params("parallel"),
    )(a)


def _place():
    return lax.axis_index("x"), lax.axis_index("y"), lax.axis_index("c")


def _other_chips(x, y):
    return [(1 - x, y), (x, 1 - y), (1 - x, 1 - y)]


def gather_devices(name, a):
    m_per, n = a.shape

    def body(x_ref, out_ref, send_sems, recv_sems, local_sem):
        x, y, c = _place()
        me, sibling = (x, y, c), (x, y, 1 - c)
        chips = _other_chips(x, y)

        def rows(px, py, pc):
            return out_ref.at[pl.ds((4 * px + 2 * py + pc) * m_per, m_per), :]

        def copy(k, block, to, src=None):
            return pltpu.make_async_remote_copy(
                src_ref=rows(*block) if src is None else src, dst_ref=rows(*block),
                send_sem=send_sems.at[k], recv_sem=recv_sems.at[k], device_id=to, device_id_type=MESH)

        mine = pltpu.make_async_copy(x_ref, rows(*me), local_sem)
        mine.start()
        first = [copy(0, me, sibling, src=x_ref)]
        first += [copy(1 + j, me, (*chip, c), src=x_ref) for j, chip in enumerate(chips)]
        for cp in first:
            cp.start()
        passed = [copy(4 + j, (*chip, c), sibling) for j, chip in enumerate(chips)]
        for j, chip in enumerate(chips):
            copy(1 + j, (*chip, c), me).wait_recv()
            passed[j].start()
        copy(0, sibling, me).wait_recv()
        for j, chip in enumerate(chips):
            copy(4 + j, (*chip, 1 - c), me).wait_recv()
        for cp in first + passed:
            cp.wait_send()
        mine.wait()

    out = pl.pallas_call(
        body, name=name,
        out_shape=jax.ShapeDtypeStruct((N_DEV * m_per, n), a.dtype),
        in_specs=[pl.BlockSpec(memory_space=pltpu.VMEM)],
        out_specs=pl.BlockSpec(memory_space=pltpu.VMEM),
        scratch_shapes=[pltpu.SemaphoreType.DMA((7,)), pltpu.SemaphoreType.DMA((7,)), pltpu.SemaphoreType.DMA],
        compiler_params=pltpu.CompilerParams(vmem_limit_bytes=VMEM_LIMIT),
    )(a)
    return out.reshape(N_DEV, m_per, n)


_ANY = pl.BlockSpec(memory_space=pl.ANY)


def gather_weights(shards):
    n = len(shards)

    def body(*refs):
        dst = refs[n:2 * n]
        ici_send, ici_recv, d2d_send, d2d_recv = refs[2 * n:]
        x, y, c = _place()
        me = 2 * x + y
        chips = _other_chips(x, y)
        sibling = (x, y, 1 - c)

        def ici(t, r, half):
            cx, cy = chips[r]
            mine = dst[t].at[me, half]
            return pltpu.make_async_remote_copy(
                src_ref=mine, dst_ref=mine,
                send_sem=ici_send.at[t, r], recv_sem=ici_recv.at[t, r], device_id=(cx, cy, c), device_id_type=MESH)

        def d2d(t, r, half):
            cx, cy = chips[r]
            there = dst[t].at[2 * cx + cy, half]
            return pltpu.make_async_remote_copy(
                src_ref=there, dst_ref=there, send_sem=d2d_send.at[t, r], recv_sem=d2d_recv.at[t, r],
                device_id=sibling, device_id_type=MESH)

        for t in range(n):
            for r in range(3):
                ici(t, r, c).start()
        for t in range(n):
            for r in range(3):
                ici(t, r, c).wait_recv()
                d2d(t, r, c).start()
        for t in range(n):
            for r in range(3):
                d2d(t, r, 1 - c).wait_recv()
        for t in range(n):
            for r in range(3):
                ici(t, r, c).wait_send()
                d2d(t, r, c).wait_send()

    return pl.pallas_call(
        body, name="gather_weights",
        out_shape=[jax.ShapeDtypeStruct(s.shape, s.dtype) for s in shards],
        in_specs=[_ANY] * n, out_specs=[_ANY] * n,
        input_output_aliases={t: t for t in range(n)},
        scratch_shapes=[pltpu.SemaphoreType.DMA((n, 3))] * 4,
    )(*shards)


def cast_into_slots(chip, shards):
    steps = 2

    def body(chip_ref, *refs):
        n = len(refs) // 2
        for src, dst in zip(refs[:n], refs[n:]):
            dst[...] = src[...].astype(BF16)

    def spec_in(s):
        return pl.BlockSpec((None, s.shape[1] // steps, s.shape[2]), lambda h, i, chip_ref: (h, i, 0))

    def spec_out(s):
        return pl.BlockSpec((None, None, s.shape[1] // steps, s.shape[2]), lambda h, i, chip_ref: (chip_ref[0], h, i, 0))

    return pl.pallas_call(
        body, name="cast_into_slots",
        grid_spec=pltpu.PrefetchScalarGridSpec(
            num_scalar_prefetch=1, grid=(2, steps),
            in_specs=[spec_in(s) for s in shards], out_specs=[spec_out(s) for s in shards]),
        out_shape=[jax.ShapeDtypeStruct((N_CHIP,) + s.shape, BF16) for s in shards],
        compiler_params=_params("parallel", "parallel"),
    )(chip, *shards)


def reduce_pair(grads):
    n = len(grads)

    def body(*refs):
        src, dst = refs[:n], refs[n:2 * n]
        send_sem, recv_sem = refs[2 * n:]
        x, y, c = _place()
        cps = [pltpu.make_async_remote_copy(
            src_ref=src[t].at[:, 1 - c], dst_ref=dst[t], send_sem=send_sem.at[t], recv_sem=recv_sem.at[t],
            device_id=(x, y, 1 - c), device_id_type=MESH) for t in range(n)]
        for cp in cps:
            cp.start()
        for cp in cps:
            cp.wait()

    return pl.pallas_call(
        body, name="reduce_pair",
        out_shape=[jax.ShapeDtypeStruct((N_CHIP,) + g.shape[2:], g.dtype) for g in grads],
        in_specs=[_ANY] * n, out_specs=[_ANY] * n,
        scratch_shapes=[pltpu.SemaphoreType.DMA((n,))] * 2,
    )(*grads)


def pair_add(name, core, g, got):
    _, _, R, C = g.shape

    def body(core_ref, g_ref, got_ref, o_ref):
        o_ref[...] = (g_ref[...] + got_ref[...]).astype(BF16)

    return pl.pallas_call(
        body, name=name,
        grid_spec=pltpu.PrefetchScalarGridSpec(
            num_scalar_prefetch=1, grid=(N_CHIP,),
            in_specs=[pl.BlockSpec((None, None, R, C), lambda q, core_ref: (q, core_ref[0], 0, 0)),
                      pl.BlockSpec((None, R, C), lambda q, core_ref: (q, 0, 0))],
            out_specs=pl.BlockSpec((None, R, C), lambda q, core_ref: (q, 0, 0))),
        out_shape=jax.ShapeDtypeStruct((N_CHIP, R, C), BF16),
        compiler_params=_params("parallel"),
    )(core, g, got)


def reduce_chips(sums):
    n = len(sums)

    def body(*refs):
        src, dst = refs[:n], refs[n:2 * n]
        send_sem, recv_sem = refs[2 * n:]
        x, y, c = _place()
        chips = _other_chips(x, y)
        cps = []
        for t in range(n):
            for r, (cx, cy) in enumerate(chips):
                cps.append(pltpu.make_async_remote_copy(
                    src_ref=src[t].at[2 * cx + cy], dst_ref=dst[t].at[r],
                    send_sem=send_sem.at[t, r], recv_sem=recv_sem.at[t, r],
                    device_id=(cx, cy, c), device_id_type=MESH))
        for cp in cps:
            cp.start()
        for cp in cps:
            cp.wait()

    return pl.pallas_call(
        body, name="reduce_chips",
        out_shape=[jax.ShapeDtypeStruct((3,) + s.shape[1:], s.dtype) for s in sums],
        in_specs=[_ANY] * n, out_specs=[_ANY] * n,
        scratch_shapes=[pltpu.SemaphoreType.DMA((n, 3))] * 2,
    )(*sums)


def chip_add(name, place, s, got, k, n_slots, prev=None):
    _, R, C = s.shape

    def body(place_ref, s_ref, got_ref, *rest):
        o_ref = rest[-1]
        o_ref[...] = ((s_ref[...].astype(F32) + got_ref[0].astype(F32)) + got_ref[1].astype(F32)) + got_ref[2].astype(F32)

    in_specs = [pl.BlockSpec((None, R, C), lambda i, place_ref: (place_ref[0], 0, 0)),
                pl.BlockSpec((3, R, C), lambda i, place_ref: (0, 0, 0))]
    args = [place, s, got]
    aliases = {}
    if prev is not None:
        in_specs.append(_ANY)
        args.append(prev)
        aliases = {3: 0}
    return pl.pallas_call(
        body, name=name,
        grid_spec=pltpu.PrefetchScalarGridSpec(
            num_scalar_prefetch=1, grid=(1,), in_specs=in_specs,
            out_specs=pl.BlockSpec((None, None, R, C), lambda i, place_ref: (k, place_ref[1], 0, 0))),
        out_shape=jax.ShapeDtypeStruct((n_slots, 2, R, C), F32),
        input_output_aliases=aliases,
        compiler_params=_params("arbitrary"),
    )(*args)


def share_halves(stacks):
    n = len(stacks)

    def body(*refs):
        dst = refs[n:2 * n]
        send_sem, recv_sem = refs[2 * n:]
        x, y, c = _place()
        cps = [pltpu.make_async_remote_copy(
            src_ref=dst[t].at[:, c], dst_ref=dst[t].at[:, c], send_sem=send_sem.at[t], recv_sem=recv_sem.at[t],
            device_id=(x, y, 1 - c), device_id_type=MESH) for t in range(n)]
        for cp in cps:
            cp.start()
        for cp in cps:
            cp.wait()

    return pl.pallas_call(
        body, name="share_halves",
        out_shape=[jax.ShapeDtypeStruct(s.shape, F32) for s in stacks],
        in_specs=[_ANY] * n, out_specs=[_ANY] * n,
        input_output_aliases={t: t for t in range(n)},
        scratch_shapes=[pltpu.SemaphoreType.DMA((n,))] * 2,
    )(*stacks)


def _swap_rope(a):
    return jnp.concatenate([a[..., ROPE // 2:], a[..., :ROPE // 2]], axis=-1)


def _rope_tables(S):
    inv = 1.0 / (ROPE_THETA ** (jnp.arange(0, ROPE, 2, dtype=F32) / ROPE))
    ang = jnp.arange(S, dtype=F32)[:, None] * inv[None, :]
    cos, sin = jnp.cos(ang), jnp.sin(ang)
    return jnp.concatenate([cos, cos], axis=1), jnp.concatenate([-sin, sin], axis=1)


def _vec(norm_g, mod, i, k):
    rows = [norm_g[i, 2 * k], norm_g[i, 2 * k + 1], mod[i, 3 * k], mod[i, 3 * k + 1], mod[i, 3 * k + 2]]
    return jnp.concatenate([jnp.stack(rows), jnp.zeros((3, D), F32)], axis=0)


def _example_step(x, target, mod, norm_g, pvec, ffn_in, ffn_out, pw, mw, wuv, wo):
    S = x.shape[0]
    tabs = _rope_tables(S)
    vec = [[_vec(norm_g, mod, i, k) for k in range(3)] for i in range(2)]
    saved = {}
    for i in range(2):
        xin = x
        x, a, u, h = ffn_fwd(xin, vec[i][0], ffn_in[i][0], ffn_out[i][0], 0.5)
        saved[i, 0] = (xin, a, u, h)
        xin = x
        if i == 0:
            x, y, z = pool_fwd(xin, vec[i][1], pw, pvec)
            saved[i, 1] = (xin, y, z)
        else:
            h_m, cq_raw, ckv_raw, cqn, qnope, qcat, kcat, vcat = mla_pre(xin, vec[i][1], mw, tabs)
            olat, lse = attn_fwd(qcat, kcat, vcat)
            x, u_m, ocat = mla_post(olat, xin, vec[i][1], wuv, wo)
            saved[i, 1] = (xin, h_m, cq_raw, ckv_raw, cqn, qnope, qcat, kcat, olat, lse, u_m, ocat)
        xin = x
        x, a, u, h = ffn_fwd(xin, vec[i][2], ffn_in[i][1], ffn_out[i][1], 0.5)
        saved[i, 2] = (xin, a, u, h)
    loss, dx = loss_head(x, target)

    vg = {}
    gw = {}
    for i in (1, 0):
        xin, a, u, h = saved[i, 2]
        dx, du, act, da, vg[i, 2] = ffn_bwd(dx, xin, u, a, vec[i][2], ffn_in[i][1], ffn_out[i][1], 0.5)
        gw["ffn_in", i, 1], gw["ffn_out", i, 1] = ffn_dw(h, da, act, du)
        if i == 0:
            xin, y, z = saved[i, 1]
            dx, vg[i, 1], pgrad, gw["pool"] = pool_bwd(dx, xin, y, z, vec[i][1], pw, pvec)
        else:
            xin, h_m, cq_raw, ckv_raw, cqn, qnope, qcat, kcat, olat, lse, u_m, ocat = saved[i, 1]
            du, docat, dolat, delta, vg_post = mla_post_bwd(dx, u_m, olat, vec[i][1], wuv, wo)
            dq, dk, dv = attn_bwd(qcat, kcat, kcat.T, dolat, lse.reshape(N_HEADS, 1, S), delta.reshape(N_HEADS, 1, S))
            dx, dlat, dka, dkb, dqn, dql, dqa, dqb, vg_pre, ngrad = mla_pre_bwd(
                dx, dq, dk, dv, xin, cq_raw, ckv_raw, vec[i][1], mw, tabs)
            vg[i, 1] = vg_post + vg_pre
            g = mla_dw(h_m, dlat, dka, dkb, cqn, dqn, dqa, dqb, dql, qnope, olat, docat, ocat, du)
            gw["mla_in"] = jnp.concatenate([g["in"], g["kr"] + _swap_rope(g["krs"])], axis=1)
            gw["uq"] = jnp.concatenate([jnp.transpose(g["n"], (1, 0, 2)),
                                        jnp.transpose(g["r"] + _swap_rope(g["rs"]), (1, 0, 2))], axis=-1)
            gw["uk"] = jnp.transpose(g["uk"], (1, 0, 2))
            gw["uv"] = jnp.transpose(g["uv"], (1, 0, 2))
            gw["wo"] = g["o"].reshape(D, D)
        xin, a, u, h = saved[i, 0]
        dx, du, act, da, vg[i, 0] = ffn_bwd(dx, xin, u, a, vec[i][0], ffn_in[i][0], ffn_out[i][0], 0.5)
        gw["ffn_in", i, 0], gw["ffn_out", i, 0] = ffn_dw(h, da, act, du)
    return loss, dx, gw, vg, pgrad, ngrad


SMALL_IN = 8 * 640
SMALL_GRAD = 8 * 4224
SMALL_W = 8 * 2944


def _pack(parts, total):
    flat = jnp.concatenate([p.reshape(-1) for p in parts])
    return jnp.concatenate([flat, jnp.zeros((total - flat.shape[0],), F32)]).reshape(8, total // 8)


def kernel(x, c, ada_w, ada_b, norm_g, ffn_w_in, ffn_w_out, pool_w, pool_b, pool_scale, mla_w_in, mla_q_norm, mla_kv_norm, mla_w_uq, mla_w_uk, mla_w_uv, mla_w_o, loss_target, m_ada_w, m_ada_b, m_norm_g, m_ffn_w_in, m_ffn_w_out, m_pool_w, m_pool_b, m_pool_scale, m_mla_w_in, m_mla_q_norm, m_mla_kv_norm, m_mla_w_uq, m_mla_w_uk, m_mla_w_uv, m_mla_w_o, v_ada_w, v_ada_b, v_norm_g, v_ffn_w_in, v_ffn_w_out, v_pool_w, v_pool_b, v_pool_scale, v_mla_w_in, v_mla_q_norm, v_mla_kv_norm, v_mla_w_uq, v_mla_w_uk, v_mla_w_uv, v_mla_w_o):
    ix, iy, ic = _place()
    chip = 2 * ix + iy
    dev = 2 * chip + ic
    core_arr = ic.astype(jnp.int32).reshape(1)
    chip_arr = chip.astype(jnp.int32).reshape(1)
    S = x.shape[1]
    G = D // 4
    NG = D // N_CHIP

    def chip_cols(a, width, axis):
        return lax.dynamic_slice_in_dim(a, chip * width, width, axis)

    got = gather_devices("gather_small_in", _pack([c, norm_g, pool_b, mla_q_norm], SMALL_IN)).reshape(N_DEV, SMALL_IN)
    c_all = got[:, :D]
    parts = got[0::2]
    o = D
    norm_g_full = parts[:, o:o + 12 * NG].reshape(N_CHIP, 2, 6, NG).transpose(1, 2, 0, 3).reshape(2, 6, D)
    o += 12 * NG
    pool_b_full = parts[:, o:o + G].reshape(N_CHIP, 4, G // N_CHIP).transpose(1, 0, 2).reshape(1, D)
    o += G
    q_norm_full = parts[:, o:o + QL // N_CHIP].reshape(1, QL)
    pvec = jnp.concatenate([pool_b_full, pool_scale, jnp.zeros((6, D), F32)], axis=0)

    c_pad = jnp.concatenate([c_all, jnp.zeros((8, D), F32)], axis=0)
    mod_loc = mod_fwd(c_pad, ada_w, chip_cols(ada_b, MOD_COLS, 1).reshape(2, 1, MOD_COLS))
    got = gather_devices("gather_mod", mod_loc[:, :8].transpose(1, 0, 2).reshape(8, 2 * MOD_COLS))
    mine = lax.dynamic_index_in_dim(got[0::2].reshape(N_CHIP, 8, 2, MOD_COLS), dev, axis=1, keepdims=False)
    mod = mine.transpose(1, 0, 2).reshape(2, 9, D)

    bf = lambda a: a.astype(BF16)
    shards = [ffn_w_in[i, k].reshape(2, D // 2, FSH) for i in range(2) for k in range(2)]
    shards += [ffn_w_out[i, k].reshape(2, DFF // 8, D) for i in range(2) for k in range(2)]
    shards += [pool_w[0].reshape(2, 2 * G // N_CHIP, G), mla_w_in[0].reshape(2, D // 8, QL + KVL + ROPE),
               mla_w_uq[0].reshape(2, QL // 8, N_HEADS * (NOPE + ROPE)), mla_w_o[0].reshape(2, D // 8, D)]
    full = gather_weights(cast_into_slots(chip_arr, shards))
    ffn_in = [[full[2 * i + k].reshape(N_CHIP, D, FSH) for k in range(2)] for i in range(2)]
    ffn_out = [[full[4 + 2 * i + k].reshape(2, FSH, D) for k in range(2)] for i in range(2)]
    pw = full[8].reshape(N_CHIP, 4, G // N_CHIP, G).transpose(1, 0, 2, 3).reshape(4, G, G)
    w_in = full[9].reshape(D, QL + KVL + ROPE)
    w_uq = full[10].reshape(QL, N_HEADS, NOPE + ROPE)
    wkr = w_in[:, QL + KVL:]
    wr = jnp.transpose(w_uq[:, :, NOPE:], (1, 0, 2))
    mw = dict(wq=w_in[:, :QL], wkv=w_in[:, QL:QL + KVL], wkr=wkr, wkrs=_swap_rope(wkr),
              qn=q_norm_full, kvn=mla_kv_norm, wn=jnp.transpose(w_uq[:, :, :NOPE], (1, 0, 2)),
              wr=wr, wrs=_swap_rope(wr), wuk=jnp.transpose(bf(mla_w_uk[0]), (1, 0, 2)))
    wuv = jnp.transpose(bf(mla_w_uv[0]), (1, 0, 2))
    wo = full[11].reshape(N_HEADS, VH, D)

    loss_mine, grad_x, gw, vg, pgrad, ngrad = _example_step(
        x[0], loss_target[0], mod, norm_g_full, pvec, ffn_in, ffn_out, pw, mw, wuv, wo)
    loss = lax.psum(loss_mine[0, 0], ("x", "y", "c"))

    grads = [gw["ffn_in", i, k].reshape(N_CHIP, 2, D // 2, FSH) for i in range(2) for k in range(2)]
    grads += [gw["ffn_out", i, k].reshape(N_CHIP, 2, DFF // 8, D) for i in range(2) for k in range(2)]
    grads += [gw["pool"].reshape(N_CHIP, 2, 2 * G // N_CHIP, G),
              gw["mla_in"].reshape(N_CHIP, 2, D // 8, QL + KVL + ROPE),
              gw["uq"].reshape(N_CHIP, 2, QL // 8, N_HEADS * (NOPE + ROPE)),
              gw["wo"].reshape(N_CHIP, 2, D // 8, D)]
    from_pair = reduce_pair(grads)
    sums = [pair_add(f"pair_add_{t}", core_arr, g, p) for t, (g, p) in enumerate(zip(grads, from_pair))]
    from_chips = reduce_chips(sums)
    place_arr = jnp.stack([chip, ic]).astype(jnp.int32)
    stacks = []
    for slots in ((0, 1, 2, 3), (4, 5, 6, 7), (8,), (9,), (10,), (11,)):
        stack = None
        for k, t in enumerate(slots):
            stack = chip_add(f"chip_add_{t}", place_arr, sums[t], from_chips[t], k, len(slots), stack)
        stacks.append(stack)
    g_ffn_in, g_ffn_out, g_pool_w, g_mla_in, g_uq, g_wo = share_halves(stacks)
    g_ffn_in = g_ffn_in.reshape(ffn_w_in.shape)
    g_ffn_out = g_ffn_out.reshape(ffn_w_out.shape)
    g_pool_w = g_pool_w.reshape(pool_w.shape)
    g_mla_in = g_mla_in.reshape(mla_w_in.shape)
    g_uq = g_uq.reshape(mla_w_uq.shape)
    g_wo = g_wo.reshape(mla_w_o.shape)

    ukv = jnp.concatenate([gw["uk"].reshape(KVL, N_HEADS * NOPE), gw["uv"].reshape(KVL, N_HEADS * VH)], axis=0)
    ukv = sum_devices("sum_ukv", gather_devices("gather_ukv", ukv))
    g_uk = ukv[:KVL].reshape(mla_w_uk.shape)
    g_uv = ukv[KVL:].reshape(mla_w_uv.shape)

    dmod = jnp.stack([jnp.concatenate([vg[i, k][0:3] for k in range(3)]) for i in range(2)])
    dnorm = jnp.stack([jnp.concatenate([vg[i, k][3:5] for k in range(3)]) for i in range(2)])
    small = _pack([dmod, dnorm, pgrad[0], pgrad[1], ngrad[0], ngrad[1, :KVL]], SMALL_GRAD)
    got = gather_devices("gather_small_grad", small)
    tot = sum_devices("sum_small_grad", got).reshape(-1)
    n_mod = 2 * 9 * D
    g_ada_b = tot[:n_mod].reshape(ada_b.shape)
    o = n_mod
    g_norm = chip_cols(tot[o:o + 12 * D].reshape(2, 6, D), NG, 2)
    o += 12 * D
    g_pool_b = chip_cols(tot[o:o + D].reshape(1, 4, G), G // N_CHIP, 2)
    o += D
    g_pool_scale = tot[o:o + D].reshape(pool_scale.shape)
    o += D
    g_q_norm = chip_cols(tot[o:o + QL].reshape(1, QL), QL // N_CHIP, 1)
    o += QL
    g_kv_norm = tot[o:o + KVL].reshape(mla_kv_norm.shape)
    dmod_all = chip_cols(got.reshape(N_DEV, -1)[:, :n_mod].reshape(N_DEV, 2, 9 * D), MOD_COLS, 2)
    dmod_pad = jnp.concatenate([dmod_all.transpose(1, 0, 2), jnp.zeros((2, 8, MOD_COLS), F32)], axis=1)

    g_ada_w, d_ada_w, nm_ada_w, nv_ada_w = adamw_ada(c_pad, dmod_pad, ada_w, m_ada_w, v_ada_w)
    small_names = ["ada_b", "norm_g", "pool_b", "pool_scale", "mla_q_norm", "mla_kv_norm"]
    small_w = [ada_b, norm_g, pool_b, pool_scale, mla_q_norm, mla_kv_norm]
    small_g = [g_ada_b, g_norm, g_pool_b, g_pool_scale, g_q_norm, g_kv_norm]
    small_m = [m_ada_b, m_norm_g, m_pool_b, m_pool_scale, m_mla_q_norm, m_mla_kv_norm]
    small_v = [v_ada_b, v_norm_g, v_pool_b, v_pool_scale, v_mla_q_norm, v_mla_kv_norm]
    packed = adamw("adamw_small", *[_pack(p, SMALL_W) for p in (small_w, small_g, small_m, small_v)])
    upd = {}
    o = 0
    for name, w in zip(small_names, small_w):
        upd[name] = [p.reshape(-1)[o:o + w.size].reshape(w.shape) for p in packed]
        o += w.size
    big = [("ffn_w_in", ffn_w_in, g_ffn_in, m_ffn_w_in, v_ffn_w_in),
           ("ffn_w_out", ffn_w_out, g_ffn_out, m_ffn_w_out, v_ffn_w_out),
           ("pool_w", pool_w, g_pool_w, m_pool_w, v_pool_w),
           ("mla_w_in", mla_w_in, g_mla_in, m_mla_w_in, v_mla_w_in),
           ("mla_w_uq", mla_w_uq, g_uq, m_mla_w_uq, v_mla_w_uq),
           ("mla_w_uk", mla_w_uk, g_uk, m_mla_w_uk, v_mla_w_uk),
           ("mla_w_uv", mla_w_uv, g_uv, m_mla_w_uv, v_mla_w_uv),
           ("mla_w_o", mla_w_o, g_wo, m_mla_w_o, v_mla_w_o)]
    for name, w, g, m, v in big:
        upd[name] = adamw("adamw_" + name, w, g, m, v)
    upd["ada_w"] = [d_ada_w, nm_ada_w, nv_ada_w]

    order = ["ada_w", "ada_b", "norm_g", "ffn_w_in", "ffn_w_out", "pool_w", "pool_b", "pool_scale", "mla_w_in",
             "mla_q_norm", "mla_kv_norm", "mla_w_uq", "mla_w_uk", "mla_w_uv", "mla_w_o"]
    grad = dict(ada_w=g_ada_w, ada_b=g_ada_b, norm_g=g_norm, ffn_w_in=g_ffn_in, ffn_w_out=g_ffn_out, pool_w=g_pool_w,
                pool_b=g_pool_b, pool_scale=g_pool_scale, mla_w_in=g_mla_in, mla_q_norm=g_q_norm,
                mla_kv_norm=g_kv_norm, mla_w_uq=g_uq, mla_w_uk=g_uk, mla_w_uv=g_uv, mla_w_o=g_wo)
    return (loss, grad_x[None], *[grad[n] for n in order], *[upd[n][0] for n in order],
            *[upd[n][1] for n in order], *[upd[n][2] for n in order])
```

```python
import functools

import jax
import jax.numpy as jnp
from jax import lax
from jax.experimental import pallas as pl
from jax.experimental.pallas import tpu as pltpu

F32 = jnp.float32
BF16 = jnp.bfloat16

D = 1024
DFF = 2816
FSH = 1408
N_CHIP = 4
N_DEV = 8
N_HEADS = 16
NOPE = 64
ROPE = 32
VH = 64
QL = 256
KVL = 128
QPAD = 256
EPS = 1e-6
ATTN_SCALE = (NOPE + ROPE) ** -0.5
ROPE_THETA = 10000.0
POOL_WINDOWS = (2, 4, 8, 16)
HALO = 8
ATTN_TQ = 1024
ATTN_KC = 512

ADAM_LR, ADAM_B1, ADAM_B2, ADAM_EPS, ADAM_WD, ADAM_STEP = 0.001, 0.9, 0.999, 1e-08, 0.01, 10

VMEM_LIMIT = 60 * 1024 * 1024
MESH = pl.DeviceIdType.MESH

NT = (((1,), (1,)), ((), ()))
TN = (((0,), (0,)), ((), ()))


def _params(*sem):
    return pltpu.CompilerParams(dimension_semantics=sem, vmem_limit_bytes=VMEM_LIMIT)


def _dot(a, b, dims=None):
    if dims is None:
        return jnp.dot(a, b, preferred_element_type=F32)
    return lax.dot_general(a, b, dims, preferred_element_type=F32)


def _rms(x):
    r = lax.rsqrt(jnp.mean(x * x, axis=-1, keepdims=True) + EPS)
    return x * r, r


def _rms_bwd(xhat, r, dxhat):
    return r * (dxhat - xhat * jnp.mean(dxhat * xhat, axis=-1, keepdims=True))


def _prenorm(x, vec_ref):
    xhat, r = _rms(x)
    h = xhat * vec_ref[0:1, :] * (1.0 + vec_ref[3:4, :]) + vec_ref[2:3, :]
    return h, xhat, r


def _postnorm_bwd(dout, u, vec_ref, weight):
    uhat, r = _rms(u)
    gt = weight * (1.0 + vec_ref[4:5, :])
    dy = dout * gt
    dgate_rows = (weight * dout) * (uhat * vec_ref[1:2, :])
    dgpost_rows = dy * uhat
    du = _rms_bwd(uhat, r, dy * vec_ref[1:2, :])
    return du, dgate_rows, dgpost_rows


def _prenorm_bwd(dh, x, vec_ref, vg_ref):
    xhat, r = _rms(x)
    sc1 = 1.0 + vec_ref[3:4, :]
    g = vec_ref[0:1, :]
    vg_ref[0:1, :] += jnp.sum(dh, axis=0, keepdims=True)
    vg_ref[1:2, :] += jnp.sum(dh * (xhat * g), axis=0, keepdims=True)
    vg_ref[3:4, :] += jnp.sum(dh * sc1 * xhat, axis=0, keepdims=True)
    return _rms_bwd(xhat, r, dh * g * sc1)


def ffn_fwd(x, vec, w_in, w_out, weight):
    S = x.shape[0]
    tm = min(256, S)

    def body(x_ref, vec_ref, wg_ref, wu_ref, wo_ref, xo_ref, a_ref, u_ref, h_ref, acc_ref):
        j = pl.program_id(1)

        @pl.when(j == 0)
        def _():
            h, _, _ = _prenorm(x_ref[...], vec_ref)
            h_ref[...] = h.astype(BF16)
            acc_ref[...] = jnp.zeros_like(acc_ref)

        hb = h_ref[...]
        g = _dot(hb, wg_ref[...])
        up = _dot(hb, wu_ref[...])
        a_ref[0] = g.astype(BF16)
        a_ref[1] = up.astype(BF16)
        act = (g * jax.nn.sigmoid(g)) * up
        acc_ref[...] += _dot(act.astype(BF16), wo_ref[...])

        @pl.when(j == 1)
        def _():
            u = acc_ref[...]
            u_ref[...] = u
            uhat, _ = _rms(u)
            xo_ref[...] = x_ref[...] + (weight * (1.0 + vec_ref[4:5, :])) * (uhat * vec_ref[1:2, :])

    return pl.pallas_call(
        body, name="ffn_fwd", grid=(S // tm, 2),
        in_specs=[pl.BlockSpec((tm, D), lambda i, j: (i, 0)),
                  pl.BlockSpec((8, D), lambda i, j: (0, 0)),
                  pl.BlockSpec((None, D, FSH), lambda i, j: (j, 0, 0)),
                  pl.BlockSpec((None, D, FSH), lambda i, j: (j + 2, 0, 0)),
                  pl.BlockSpec((None, FSH, D), lambda i, j: (j, 0, 0))],
        out_specs=[pl.BlockSpec((tm, D), lambda i, j: (i, 0)),
                   pl.BlockSpec((2, tm, FSH), lambda i, j: (0, i, j)),
                   pl.BlockSpec((tm, D), lambda i, j: (i, 0)),
                   pl.BlockSpec((tm, D), lambda i, j: (i, 0))],
        out_shape=[jax.ShapeDtypeStruct((S, D), F32), jax.ShapeDtypeStruct((2, S, DFF), BF16),
                   jax.ShapeDtypeStruct((S, D), F32), jax.ShapeDtypeStruct((S, D), BF16)],
        scratch_shapes=[pltpu.VMEM((tm, D), F32)],
        compiler_params=_params("parallel", "arbitrary"),
    )(x, vec, w_in, w_in, w_out)


def ffn_bwd(dout, x, u, a, vec, w_in, w_out, weight):
    S = x.shape[0]
    tm = min(256, S)

    def body(do_ref, x_ref, u_ref, a_ref, vec_ref, wg_ref, wu_ref, wo_ref,
             dx_ref, du_ref, act_ref, da_ref, vg_ref, dh_ref):
        i, j = pl.program_id(0), pl.program_id(1)

        @pl.when((i == 0) & (j == 0))
        def _():
            vg_ref[...] = jnp.zeros_like(vg_ref)

        @pl.when(j == 0)
        def _():
            du, dgate_rows, dgpost_rows = _postnorm_bwd(do_ref[...], u_ref[...], vec_ref, weight)
            vg_ref[2:3, :] += jnp.sum(dgate_rows, axis=0, keepdims=True)
            vg_ref[4:5, :] += jnp.sum(dgpost_rows, axis=0, keepdims=True)
            du_ref[...] = du.astype(BF16)
            dh_ref[...] = jnp.zeros_like(dh_ref)

        dact = _dot(du_ref[...], wo_ref[...], NT)
        g = a_ref[0].astype(F32)
        up = a_ref[1].astype(F32)
        s = jax.nn.sigmoid(g)
        silu = g * s
        act_ref[...] = (silu * up).astype(BF16)
        dg = (dact * up * (s * (1.0 + g * (1.0 - s)))).astype(BF16)
        dup = (dact * silu).astype(BF16)
        da_ref[0] = dg
        da_ref[1] = dup
        dh_ref[...] += _dot(dg, wg_ref[...], NT) + _dot(dup, wu_ref[...], NT)

        @pl.when(j == 1)
        def _():
            dx_ref[...] = do_ref[...] + _prenorm_bwd(dh_ref[...], x_ref[...], vec_ref, vg_ref)

    row = lambda i, j: (i, 0)
    return pl.pallas_call(
        body, name="ffn_bwd", grid=(S // tm, 2),
        in_specs=[pl.BlockSpec((tm, D), row), pl.BlockSpec((tm, D), row), pl.BlockSpec((tm, D), row),
                  pl.BlockSpec((2, tm, FSH), lambda i, j: (0, i, j)),
                  pl.BlockSpec((8, D), lambda i, j: (0, 0)),
                  pl.BlockSpec((None, D, FSH), lambda i, j: (j, 0, 0)),
                  pl.BlockSpec((None, D, FSH), lambda i, j: (j + 2, 0, 0)),
                  pl.BlockSpec((None, FSH, D), lambda i, j: (j, 0, 0))],
        out_specs=[pl.BlockSpec((tm, D), row), pl.BlockSpec((tm, D), row),
                   pl.BlockSpec((tm, FSH), lambda i, j: (i, j)),
                   pl.BlockSpec((2, tm, FSH), lambda i, j: (0, i, j)),
                   pl.BlockSpec((8, D), lambda i, j: (0, 0))],
        out_shape=[jax.ShapeDtypeStruct((S, D), F32), jax.ShapeDtypeStruct((S, D), BF16),
                   jax.ShapeDtypeStruct((S, DFF), BF16), jax.ShapeDtypeStruct((2, S, DFF), BF16),
                   jax.ShapeDtypeStruct((8, D), F32)],
        scratch_shapes=[pltpu.VMEM((tm, D), F32)],
        compiler_params=_params("arbitrary", "arbitrary"),
    )(dout, x, u, a, vec, w_in, w_in, w_out)


def dw_matmul(name, a, b, a_spec, b_spec, out_shape, out_spec, grid):
    def body(a_ref, b_ref, o_ref):
        @pl.when(pl.program_id(len(grid) - 1) == 0)
        def _():
            o_ref[...] = jnp.zeros_like(o_ref)

        o_ref[...] += _dot(a_ref[...], b_ref[...], TN)

    return pl.pallas_call(
        body, name=name, grid=grid, in_specs=[a_spec, b_spec], out_specs=out_spec,
        out_shape=jax.ShapeDtypeStruct(out_shape, F32),
        compiler_params=_params(*(["parallel"] * (len(grid) - 1) + ["arbitrary"])),
    )(a, b)


def ffn_dw(h, da, act, du):
    S = h.shape[0]
    tk = min(512, S)
    dw_in = dw_matmul("ffn_dw_in", h, da,
                      pl.BlockSpec((tk, D), lambda n, k: (k, 0)),
                      pl.BlockSpec((None, tk, FSH), lambda n, k: (n // 2, k, n % 2)),
                      (N_CHIP, D, FSH), pl.BlockSpec((None, D, FSH), lambda n, k: (n, 0, 0)),
                      (N_CHIP, S // tk))
    dw_out = dw_matmul("ffn_dw_out", act, du,
                       pl.BlockSpec((tk, FSH), lambda n, k: (k, n)),
                       pl.BlockSpec((tk, D), lambda n, k: (k, 0)),
                       (DFF, D), pl.BlockSpec((FSH, D), lambda n, k: (n, 0)),
                       (2, S // tk))
    return dw_in, dw_out


def _halo_specs(tm, S):
    nb = tm // HALO
    last = S // HALO - 1
    return [pl.BlockSpec((HALO, D), lambda i: (jnp.maximum(i * nb - 1, 0), 0)),
            pl.BlockSpec((tm, D), lambda i: (i, 0)),
            pl.BlockSpec((HALO, D), lambda i: (jnp.minimum((i + 1) * nb, last), 0))]


def _shift_rows(v, k):
    return pltpu.roll(v, k % v.shape[0], 0)


def _window_sum(v, g, forward):
    acc = v + _shift_rows(v, 1 if forward else -1)
    for step in (1, 2, 4)[:g]:
        acc = _shift_rows(acc, step) + _shift_rows(acc, -step)
    return acc


def _pool_count(t, w, S):
    return jnp.maximum(jnp.minimum(t + w // 2, S) - jnp.maximum(t - w // 2, 0), 1).astype(F32)


def pool_fwd(x, vec, pw, pvec):
    S = x.shape[0]
    tm = min(256, S)
    G = D // 4

    def body(xp_ref, x_ref, xn_ref, vec_ref, pw_ref, pv_ref, xo_ref, y_ref, z_ref):
        i = pl.program_id(0)
        xa = jnp.concatenate([xp_ref[...], x_ref[...], xn_ref[...]], axis=0)
        t = i * tm - HALO + lax.broadcasted_iota(jnp.int32, (tm + 2 * HALO, 1), 0)
        h, _, _ = _prenorm(xa, vec_ref)
        h = jnp.where((t >= 0) & (t < S), h, 0.0)
        tmain = t[HALO:HALO + tm]
        for g in range(4):
            hg = h[:, g * G:(g + 1) * G]
            pooled = _window_sum(hg, g, True)[HALO:HALO + tm] / _pool_count(tmain, POOL_WINDOWS[g], S)
            z = (pooled - hg[HALO:HALO + tm]).astype(BF16)
            z_ref[:, g * G:(g + 1) * G] = z
            y_ref[:, g * G:(g + 1) * G] = _dot(z, pw_ref[g]) + pv_ref[0:1, g * G:(g + 1) * G]
        u = y_ref[...] * pv_ref[1:2, :]
        uhat, _ = _rms(u)
        xo_ref[...] = x_ref[...] + (1.0 + vec_ref[4:5, :]) * (uhat * vec_ref[1:2, :])

    row = lambda i: (i, 0)
    full = lambda i: (0, 0)
    return pl.pallas_call(
        body, name="pool_fwd", grid=(S // tm,),
        in_specs=_halo_specs(tm, S) + [pl.BlockSpec((8, D), full), pl.BlockSpec((4, G, G), lambda i: (0, 0, 0)),
                                       pl.BlockSpec((8, D), full)],
        out_specs=[pl.BlockSpec((tm, D), row)] * 3,
        out_shape=[jax.ShapeDtypeStruct((S, D), F32), jax.ShapeDtypeStruct((S, D), F32),
                   jax.ShapeDtypeStruct((S, D), BF16)],
        compiler_params=_params("parallel"),
    )(x, x, x, vec, pw, pvec)


def pool_bwd(dout, x, y, z, vec, pw, pvec):
    S = x.shape[0]
    tm = min(256, S)
    G = D // 4
    R = G // N_CHIP

    def body(dop_ref, do_ref, don_ref, yp_ref, y_ref, yn_ref, x_ref, z_ref, vec_ref, pw_ref, pv_ref,
             dx_ref, vg_ref, pg_ref, dw_ref, dh_ref):
        i = pl.program_id(0)

        @pl.when(i == 0)
        def _():
            vg_ref[...] = jnp.zeros_like(vg_ref)
            pg_ref[...] = jnp.zeros_like(pg_ref)
            dw_ref[...] = jnp.zeros_like(dw_ref)

        doa = jnp.concatenate([dop_ref[...], do_ref[...], don_ref[...]], axis=0)
        ya = jnp.concatenate([yp_ref[...], y_ref[...], yn_ref[...]], axis=0)
        t = i * tm - HALO + lax.broadcasted_iota(jnp.int32, (tm + 2 * HALO, 1), 0)
        inside = (t >= 0) & (t < S)
        main = (t >= i * tm) & (t < (i + 1) * tm)
        du, dgate_rows, dgpost_rows = _postnorm_bwd(doa, ya * pv_ref[1:2, :], vec_ref, 1.0)
        du = jnp.where(inside, du, 0.0)
        vg_ref[2:3, :] += jnp.sum(jnp.where(main, dgate_rows, 0.0), axis=0, keepdims=True)
        vg_ref[4:5, :] += jnp.sum(jnp.where(main, dgpost_rows, 0.0), axis=0, keepdims=True)
        dy = du * pv_ref[1:2, :]
        pg_ref[0:1, :] += jnp.sum(jnp.where(main, dy, 0.0), axis=0, keepdims=True)
        pg_ref[1:2, :] += jnp.sum(jnp.where(main, du * ya, 0.0), axis=0, keepdims=True)
        for g in range(4):
            dyg = dy[:, g * G:(g + 1) * G].astype(BF16)
            dz = _dot(dyg, pw_ref[g], NT)
            e = dz / _pool_count(t, POOL_WINDOWS[g], S)
            dh_ref[:, g * G:(g + 1) * G] = (_window_sum(e, g, False) - dz)[HALO:HALO + tm]
            dwg = _dot(z_ref[:, g * G:(g + 1) * G], dyg[HALO:HALO + tm], TN)
            for q in range(N_CHIP):
                dw_ref[q, g] += dwg[q * R:(q + 1) * R, :]
        dx_ref[...] = do_ref[...] + _prenorm_bwd(dh_ref[...], x_ref[...], vec_ref, vg_ref)

    row = lambda i: (i, 0)
    full = lambda i: (0, 0)
    halo = _halo_specs(tm, S)
    return pl.pallas_call(
        body, name="pool_bwd", grid=(S // tm,),
        in_specs=halo + halo + [pl.BlockSpec((tm, D), row), pl.BlockSpec((tm, D), row), pl.BlockSpec((8, D), full),
                                pl.BlockSpec((4, G, G), lambda i: (0, 0, 0)), pl.BlockSpec((8, D), full)],
        out_specs=[pl.BlockSpec((tm, D), row), pl.BlockSpec((8, D), full), pl.BlockSpec((8, D), full),
                   pl.BlockSpec((N_CHIP, 4, R, G), lambda i: (0, 0, 0, 0))],
        out_shape=[jax.ShapeDtypeStruct((S, D), F32), jax.ShapeDtypeStruct((8, D), F32),
                   jax.ShapeDtypeStruct((8, D), F32), jax.ShapeDtypeStruct((N_CHIP, 4, R, G), F32)],
        scratch_shapes=[pltpu.VMEM((tm, D), F32)],
        compiler_params=_params("arbitrary"),
    )(dout, dout, dout, y, y, y, x, z, vec, pw, pvec)


def _w3(shape):
    return pl.BlockSpec(shape, lambda i: (0,) * len(shape))


def mla_pre(x, vec, mw, tabs):
    S = x.shape[0]
    tm = min(256, S)

    def body(x_ref, vec_ref, cos_ref, sin_ref, wq_ref, wkv_ref, wkr_ref, wkrs_ref, qn_ref, kvn_ref,
             wn_ref, wr_ref, wrs_ref, wuk_ref,
             h_ref, cq_ref, ckv_ref, cqn_ref, qnope_ref, qcat_ref, kcat_ref, vcat_ref):
        h, _, _ = _prenorm(x_ref[...], vec_ref)
        hb = h.astype(BF16)
        h_ref[...] = hb
        cq_raw = _dot(hb, wq_ref[...])
        ckv_raw = _dot(hb, wkv_ref[...])
        cq_ref[...] = cq_raw
        ckv_ref[...] = ckv_raw
        cos, sin = cos_ref[...], sin_ref[...]
        k_rope = _dot(hb, wkr_ref[...]) * cos + _dot(hb, wkrs_ref[...]) * sin
        ckv = _rms(ckv_raw)[0] * kvn_ref[...]
        kcat_ref[:, 0:KVL] = ckv.astype(BF16)
        vcat_ref[:, 0:KVL] = ckv.astype(BF16)
        ones = lax.broadcasted_iota(jnp.int32, (tm, QPAD - KVL), 1) == 0
        vcat_ref[:, KVL:] = jnp.where(ones, 1.0, 0.0).astype(BF16)
        kcat_ref[:, KVL:KVL + ROPE] = k_rope.astype(BF16)
        kcat_ref[:, KVL + ROPE:] = jnp.zeros((tm, QPAD - KVL - ROPE), BF16)
        cqb = (_rms(cq_raw)[0] * qn_ref[...]).astype(BF16)
        cqn_ref[...] = cqb
        for hd in range(N_HEADS):
            qn = _dot(cqb, wn_ref[hd]).astype(BF16)
            qnope_ref[hd] = qn
            qcat_ref[hd, :, 0:KVL] = (_dot(qn, wuk_ref[hd], NT) * ATTN_SCALE).astype(BF16)
            qr = (_dot(cqb, wr_ref[hd]) * cos + _dot(cqb, wrs_ref[hd]) * sin) * ATTN_SCALE
            qcat_ref[hd, :, KVL:KVL + ROPE] = qr.astype(BF16)
            qcat_ref[hd, :, KVL + ROPE:] = jnp.zeros((tm, QPAD - KVL - ROPE), BF16)

    row = lambda i: (i, 0)
    hrow = lambda i: (0, i, 0)
    return pl.pallas_call(
        body, name="mla_pre", grid=(S // tm,),
        in_specs=[pl.BlockSpec((tm, D), row), _w3((8, D)), pl.BlockSpec((tm, ROPE), row), pl.BlockSpec((tm, ROPE), row),
                  _w3((D, QL)), _w3((D, KVL)), _w3((D, ROPE)), _w3((D, ROPE)), _w3((1, QL)), _w3((1, KVL)),
                  _w3((N_HEADS, QL, NOPE)), _w3((N_HEADS, QL, ROPE)), _w3((N_HEADS, QL, ROPE)),
                  _w3((N_HEADS, KVL, NOPE))],
        out_specs=[pl.BlockSpec((tm, D), row), pl.BlockSpec((tm, QL), row), pl.BlockSpec((tm, KVL), row),
                   pl.BlockSpec((tm, QL), row), pl.BlockSpec((N_HEADS, tm, NOPE), hrow),
                   pl.BlockSpec((N_HEADS, tm, QPAD), hrow), pl.BlockSpec((tm, QPAD), row),
                   pl.BlockSpec((tm, QPAD), row)],
        out_shape=[jax.ShapeDtypeStruct((S, D), BF16), jax.ShapeDtypeStruct((S, QL), F32),
                   jax.ShapeDtypeStruct((S, KVL), F32), jax.ShapeDtypeStruct((S, QL), BF16),
                   jax.ShapeDtypeStruct((N_HEADS, S, NOPE), BF16), jax.ShapeDtypeStruct((N_HEADS, S, QPAD), BF16),
                   jax.ShapeDtypeStruct((S, QPAD), BF16), jax.ShapeDtypeStruct((S, QPAD), BF16)],
        compiler_params=_params("parallel"),
    )(x, vec, tabs[0], tabs[1], mw["wq"], mw["wkv"], mw["wkr"], mw["wkrs"], mw["qn"], mw["kvn"],
      mw["wn"], mw["wr"], mw["wrs"], mw["wuk"])


def attn_fwd(qcat, kcat, vcat):
    S = kcat.shape[0]
    tq = min(ATTN_TQ, S)
    kc = min(ATTN_KC, S)

    def body(q_ref, k_ref, v_ref, o_ref, lse_ref):
        q = q_ref[...]
        m = jnp.full((tq, 1), -jnp.inf, F32)
        ov = jnp.zeros((tq, QPAD), F32)
        for c in range(S // kc):
            s = _dot(q, k_ref[c * kc:(c + 1) * kc, :], NT)
            m_new = jnp.maximum(m, jnp.max(s, axis=-1, keepdims=True))
            p = jnp.exp(s - m_new).astype(BF16)
            ov = ov * jnp.exp(m - m_new) + _dot(p, v_ref[c * kc:(c + 1) * kc, :])
            m = m_new
        l = ov[:, KVL:KVL + 1]
        o_ref[...] = (ov[:, 0:KVL] * (1.0 / l)).astype(BF16)
        lse_ref[...] = m + jnp.log(l)

    return pl.pallas_call(
        body, name="attn_fwd", grid=(N_HEADS, S // tq),
        in_specs=[pl.BlockSpec((None, tq, QPAD), lambda h, i: (h, i, 0)),
                  pl.BlockSpec((S, QPAD), lambda h, i: (0, 0)),
                  pl.BlockSpec((S, QPAD), lambda h, i: (0, 0))],
        out_specs=[pl.BlockSpec((None, tq, KVL), lambda h, i: (h, i, 0)),
                   pl.BlockSpec((None, tq, 1), lambda h, i: (h, i, 0))],
        out_shape=[jax.ShapeDtypeStruct((N_HEADS, S, KVL), BF16), jax.ShapeDtypeStruct((N_HEADS, S, 1), F32)],
        compiler_params=_params("parallel", "parallel"),
    )(qcat, kcat, vcat)


def mla_post(olat, x, vec, wuv, wo):
    S = x.shape[0]
    tm = min(256, S)

    def body(o_ref, x_ref, vec_ref, wuv_ref, wo_ref, xo_ref, u_ref, ocat_ref):
        u = jnp.zeros((tm, D), F32)
        for hd in range(N_HEADS):
            oc = _dot(o_ref[hd], wuv_ref[hd]).astype(BF16)
            ocat_ref[hd] = oc
            u = u + _dot(oc, wo_ref[hd])
        u_ref[...] = u
        uhat, _ = _rms(u)
        xo_ref[...] = x_ref[...] + (1.0 + vec_ref[4:5, :]) * (uhat * vec_ref[1:2, :])

    row = lambda i: (i, 0)
    hrow = lambda i: (0, i, 0)
    return pl.pallas_call(
        body, name="mla_post", grid=(S // tm,),
        in_specs=[pl.BlockSpec((N_HEADS, tm, KVL), hrow), pl.BlockSpec((tm, D), row), _w3((8, D)),
                  _w3((N_HEADS, KVL, VH)), _w3((N_HEADS, VH, D))],
        out_specs=[pl.BlockSpec((tm, D), row), pl.BlockSpec((tm, D), row), pl.BlockSpec((N_HEADS, tm, VH), hrow)],
        out_shape=[jax.ShapeDtypeStruct((S, D), F32), jax.ShapeDtypeStruct((S, D), F32),
                   jax.ShapeDtypeStruct((N_HEADS, S, VH), BF16)],
        compiler_params=_params("parallel"),
    )(olat, x, vec, wuv, wo)


def mla_post_bwd(dout, u, olat, vec, wuv, wo):
    S = u.shape[0]
    tm = min(256, S)

    def body(do_ref, u_ref, o_ref, vec_ref, wuv_ref, wo_ref, du_ref, docat_ref, dolat_ref, delta_ref, vg_ref):
        @pl.when(pl.program_id(0) == 0)
        def _():
            vg_ref[...] = jnp.zeros_like(vg_ref)

        du, dgate_rows, dgpost_rows = _postnorm_bwd(do_ref[...], u_ref[...], vec_ref, 1.0)
        vg_ref[2:3, :] += jnp.sum(dgate_rows, axis=0, keepdims=True)
        vg_ref[4:5, :] += jnp.sum(dgpost_rows, axis=0, keepdims=True)
        dub = du.astype(BF16)
        du_ref[...] = dub
        for hd in range(N_HEADS):
            doc = _dot(dub, wo_ref[hd], NT).astype(BF16)
            docat_ref[hd] = doc
            dol = _dot(doc, wuv_ref[hd], NT).astype(BF16)
            dolat_ref[hd] = dol
            delta_ref[hd] = jnp.sum(dol.astype(F32) * o_ref[hd].astype(F32), axis=-1, keepdims=True)

    row = lambda i: (i, 0)
    hrow = lambda i: (0, i, 0)
    return pl.pallas_call(
        body, name="mla_post_bwd", grid=(S // tm,),
        in_specs=[pl.BlockSpec((tm, D), row), pl.BlockSpec((tm, D), row), pl.BlockSpec((N_HEADS, tm, KVL), hrow),
                  _w3((8, D)), _w3((N_HEADS, KVL, VH)), _w3((N_HEADS, VH, D))],
        out_specs=[pl.BlockSpec((tm, D), row), pl.BlockSpec((N_HEADS, tm, VH), hrow),
                   pl.BlockSpec((N_HEADS, tm, KVL), hrow), pl.BlockSpec((N_HEADS, tm, 1), hrow), _w3((8, D))],
        out_shape=[jax.ShapeDtypeStruct((S, D), BF16), jax.ShapeDtypeStruct((N_HEADS, S, VH), BF16),
                   jax.ShapeDtypeStruct((N_HEADS, S, KVL), BF16), jax.ShapeDtypeStruct((N_HEADS, S, 1), F32),
                   jax.ShapeDtypeStruct((8, D), F32)],
        compiler_params=_params("arbitrary"),
    )(dout, u, olat, vec, wuv, wo)


def attn_bwd(qcat, kcat, kcat_t, dolat, lse_row, delta_row):
    S = kcat.shape[0]
    tq = min(ATTN_TQ, S)
    kc = min(ATTN_KC, S)

    def body(q_ref, k_ref, kt_ref, do_ref, lse_ref, dl_ref, dq_ref, dk_ref, dv_ref):
        @pl.when((pl.program_id(0) == 0) & (pl.program_id(1) == 0))
        def _():
            dk_ref[...] = jnp.zeros_like(dk_ref)
            dv_ref[...] = jnp.zeros_like(dv_ref)

        q, do = q_ref[...], do_ref[...]
        lse, dl = lse_ref[...], dl_ref[...]
        dqt = jnp.zeros((QPAD, tq), F32)
        for c in range(S // kc):
            rows = slice(c * kc, (c + 1) * kc)
            st = _dot(k_ref[rows, :], q, NT)
            pt = jnp.exp(st - lse)
            dpt = _dot(k_ref[rows, 0:KVL], do, NT)
            dst = (pt * (dpt - dl)).astype(BF16)
            dv_ref[rows, :] += _dot(pt.astype(BF16), do)
            dk_ref[rows, :] += _dot(dst, q)
            dqt = dqt + _dot(kt_ref[:, rows], dst)
        dq_ref[...] = dqt.T

    return pl.pallas_call(
        body, name="attn_bwd", grid=(N_HEADS, S // tq),
        in_specs=[pl.BlockSpec((None, tq, QPAD), lambda h, i: (h, i, 0)),
                  pl.BlockSpec((S, QPAD), lambda h, i: (0, 0)),
                  pl.BlockSpec((QPAD, S), lambda h, i: (0, 0)),
                  pl.BlockSpec((None, tq, KVL), lambda h, i: (h, i, 0)),
                  pl.BlockSpec((None, 1, tq), lambda h, i: (h, 0, i)),
                  pl.BlockSpec((None, 1, tq), lambda h, i: (h, 0, i))],
        out_specs=[pl.BlockSpec((None, tq, QPAD), lambda h, i: (h, i, 0)),
                   pl.BlockSpec((S, QPAD), lambda h, i: (0, 0)),
                   pl.BlockSpec((S, KVL), lambda h, i: (0, 0))],
        out_shape=[jax.ShapeDtypeStruct((N_HEADS, S, QPAD), F32), jax.ShapeDtypeStruct((S, QPAD), F32),
                   jax.ShapeDtypeStruct((S, KVL), F32)],
        compiler_params=_params("arbitrary", "arbitrary"),
    )(qcat, kcat, kcat_t, dolat, lse_row, delta_row)


def mla_pre_bwd(dout, dq, dk, dv, x, cq_raw, ckv_raw, vec, mw, tabs):
    S = x.shape[0]
    tm = min(256, S)

    def body(do_ref, dq_ref, dk_ref, dv_ref, x_ref, cq_ref, ckv_ref, vec_ref, cos_ref, sin_ref,
             wq_ref, wkv_ref, wkr_ref, wkrs_ref, qn_ref, kvn_ref, wn_ref, wr_ref, wrs_ref, wuk_ref,
             dx_ref, dlat_ref, dka_ref, dkb_ref, dqn_ref, dql_ref, dqa_ref, dqb_ref, vg_ref, ng_ref):
        @pl.when(pl.program_id(0) == 0)
        def _():
            vg_ref[...] = jnp.zeros_like(vg_ref)
            ng_ref[...] = jnp.zeros_like(ng_ref)

        cos, sin = cos_ref[...], sin_ref[...]
        dcq = jnp.zeros((tm, QL), F32)
        for hd in range(N_HEADS):
            dql = (dq_ref[hd, :, 0:KVL] * ATTN_SCALE).astype(BF16)
            dql_ref[hd] = dql
            dqn = _dot(dql, wuk_ref[hd]).astype(BF16)
            dqn_ref[hd] = dqn
            dqr = dq_ref[hd, :, KVL:KVL + ROPE] * ATTN_SCALE
            qa = (dqr * cos).astype(BF16)
            qb = (dqr * sin).astype(BF16)
            dqa_ref[hd] = qa
            dqb_ref[hd] = qb
            dcq = dcq + _dot(dqn, wn_ref[hd], NT) + _dot(qa, wr_ref[hd], NT) + _dot(qb, wrs_ref[hd], NT)
        cqh, rq = _rms(cq_ref[...])
        ng_ref[0:1, :] += jnp.sum(dcq * cqh, axis=0, keepdims=True)
        dcq_raw = _rms_bwd(cqh, rq, dcq * qn_ref[...]).astype(BF16)
        dckv = dk_ref[:, 0:KVL] + dv_ref[...]
        ckvh, rk = _rms(ckv_ref[...])
        ng_ref[1:2, 0:KVL] += jnp.sum(dckv * ckvh, axis=0, keepdims=True)
        dckv_raw = _rms_bwd(ckvh, rk, dckv * kvn_ref[...]).astype(BF16)
        dkr = dk_ref[:, KVL:KVL + ROPE]
        ka = (dkr * cos).astype(BF16)
        kb = (dkr * sin).astype(BF16)
        dlat_ref[:, 0:QL] = dcq_raw
        dlat_ref[:, QL:QL + KVL] = dckv_raw
        dka_ref[...] = ka
        dkb_ref[...] = kb
        dh = (_dot(dcq_raw, wq_ref[...], NT) + _dot(dckv_raw, wkv_ref[...], NT)
              + _dot(ka, wkr_ref[...], NT) + _dot(kb, wkrs_ref[...], NT))
        dx_ref[...] = do_ref[...] + _prenorm_bwd(dh, x_ref[...], vec_ref, vg_ref)

    row = lambda i: (i, 0)
    hrow = lambda i: (0, i, 0)
    return pl.pallas_call(
        body, name="mla_pre_bwd", grid=(S // tm,),
        in_specs=[pl.BlockSpec((tm, D), row), pl.BlockSpec((N_HEADS, tm, QPAD), hrow), pl.BlockSpec((tm, QPAD), row),
                  pl.BlockSpec((tm, KVL), row), pl.BlockSpec((tm, D), row), pl.BlockSpec((tm, QL), row),
                  pl.BlockSpec((tm, KVL), row), _w3((8, D)), pl.BlockSpec((tm, ROPE), row), pl.BlockSpec((tm, ROPE), row),
                  _w3((D, QL)), _w3((D, KVL)), _w3((D, ROPE)), _w3((D, ROPE)), _w3((1, QL)), _w3((1, KVL)),
                  _w3((N_HEADS, QL, NOPE)), _w3((N_HEADS, QL, ROPE)), _w3((N_HEADS, QL, ROPE)),
                  _w3((N_HEADS, KVL, NOPE))],
        out_specs=[pl.BlockSpec((tm, D), row), pl.BlockSpec((tm, QL + KVL), row), pl.BlockSpec((tm, ROPE), row),
                   pl.BlockSpec((tm, ROPE), row), pl.BlockSpec((N_HEADS, tm, NOPE), hrow),
                   pl.BlockSpec((N_HEADS, tm, KVL), hrow), pl.BlockSpec((N_HEADS, tm, ROPE), hrow),
                   pl.BlockSpec((N_HEADS, tm, ROPE), hrow), _w3((8, D)), _w3((8, QL))],
        out_shape=[jax.ShapeDtypeStruct((S, D), F32), jax.ShapeDtypeStruct((S, QL + KVL), BF16),
                   jax.ShapeDtypeStruct((S, ROPE), BF16), jax.ShapeDtypeStruct((S, ROPE), BF16),
                   jax.ShapeDtypeStruct((N_HEADS, S, NOPE), BF16), jax.ShapeDtypeStruct((N_HEADS, S, KVL), BF16),
                   jax.ShapeDtypeStruct((N_HEADS, S, ROPE), BF16), jax.ShapeDtypeStruct((N_HEADS, S, ROPE), BF16),
                   jax.ShapeDtypeStruct((8, D), F32), jax.ShapeDtypeStruct((8, QL), F32)],
        compiler_params=_params("arbitrary"),
    )(dout, dq, dk, dv, x, cq_raw, ckv_raw, vec, tabs[0], tabs[1], mw["wq"], mw["wkv"], mw["wkr"], mw["wkrs"],
      mw["qn"], mw["kvn"], mw["wn"], mw["wr"], mw["wrs"], mw["wuk"])


def mla_dw(h, dlat, dka, dkb, cqn, dqn, dqa, dqb, dql, qnope, olat, docat, ocat, du):
    S = h.shape[0]
    tk = min(512, S)
    nk = S // tk
    flat_a = lambda w: pl.BlockSpec((tk, w), lambda k: (k, 0))
    head_a = lambda w: pl.BlockSpec((None, tk, w), lambda n, k: (n, k, 0))
    shared = lambda w: pl.BlockSpec((tk, w), lambda n, k: (k, 0))
    head_o = lambda r, c: pl.BlockSpec((None, r, c), lambda n, k: (n, 0, 0))
    g = {}
    g["in"] = dw_matmul("mla_dw_in", h, dlat, flat_a(D), flat_a(QL + KVL), (D, QL + KVL),
                        pl.BlockSpec((D, QL + KVL), lambda k: (0, 0)), (nk,))
    g["kr"] = dw_matmul("mla_dw_kr", h, dka, flat_a(D), flat_a(ROPE), (D, ROPE),
                        pl.BlockSpec((D, ROPE), lambda k: (0, 0)), (nk,))
    g["krs"] = dw_matmul("mla_dw_krs", h, dkb, flat_a(D), flat_a(ROPE), (D, ROPE),
                         pl.BlockSpec((D, ROPE), lambda k: (0, 0)), (nk,))
    g["n"] = dw_matmul("mla_dw_n", cqn, dqn, shared(QL), head_a(NOPE), (N_HEADS, QL, NOPE), head_o(QL, NOPE),
                       (N_HEADS, nk))
    g["r"] = dw_matmul("mla_dw_r", cqn, dqa, shared(QL), head_a(ROPE), (N_HEADS, QL, ROPE), head_o(QL, ROPE),
                       (N_HEADS, nk))
    g["rs"] = dw_matmul("mla_dw_rs", cqn, dqb, shared(QL), head_a(ROPE), (N_HEADS, QL, ROPE), head_o(QL, ROPE),
                        (N_HEADS, nk))
    g["uk"] = dw_matmul("mla_dw_uk", dql, qnope, head_a(KVL), head_a(NOPE), (N_HEADS, KVL, NOPE), head_o(KVL, NOPE),
                        (N_HEADS, nk))
    g["uv"] = dw_matmul("mla_dw_uv", olat, docat, head_a(KVL), head_a(VH), (N_HEADS, KVL, VH), head_o(KVL, VH),
                        (N_HEADS, nk))
    g["o"] = dw_matmul("mla_dw_o", ocat, du, head_a(VH), shared(D), (N_HEADS, VH, D), head_o(VH, D),
                       (N_HEADS, nk))
    return g


def loss_head(y, target):
    S = y.shape[0]
    tm = min(512, S)

    def body(y_ref, t_ref, loss_ref, dy_ref):
        @pl.when(pl.program_id(0) == 0)
        def _():
            loss_ref[...] = jnp.zeros_like(loss_ref)

        err = y_ref[...] - t_ref[...]
        dy_ref[...] = err * (1.0 / D)
        loss_ref[...] += 0.5 * jnp.sum(jnp.mean(err * err, axis=-1, keepdims=True), axis=0, keepdims=True)

    row = lambda i: (i, 0)
    return pl.pallas_call(
        body, name="loss_head", grid=(S // tm,),
        in_specs=[pl.BlockSpec((tm, D), row), pl.BlockSpec((tm, D), row)],
        out_specs=[pl.BlockSpec((1, 1), lambda i: (0, 0)), pl.BlockSpec((tm, D), row)],
        out_shape=[jax.ShapeDtypeStruct((1, 1), F32), jax.ShapeDtypeStruct((S, D), F32)],
        compiler_params=_params("arbitrary"),
    )(y, target)


MOD_COLS = 9 * D // N_CHIP


def mod_fwd(c_pad, ada_w, ada_b_loc):
    tn = MOD_COLS // 3

    def body(c_ref, w_ref, b_ref, o_ref):
        c = c_ref[...]
        sc = (c * jax.nn.sigmoid(c)).astype(BF16)
        o_ref[...] = _dot(sc, w_ref[...].astype(BF16)) + b_ref[...]

    return pl.pallas_call(
        body, name="mod_fwd", grid=(2, 3),
        in_specs=[pl.BlockSpec((16, D), lambda i, n: (0, 0)), pl.BlockSpec((None, D, tn), lambda i, n: (i, 0, n)),
                  pl.BlockSpec((None, 1, tn), lambda i, n: (i, 0, n))],
        out_specs=pl.BlockSpec((None, 16, tn), lambda i, n: (i, 0, n)),
        out_shape=jax.ShapeDtypeStruct((2, 16, MOD_COLS), F32),
        compiler_params=_params("parallel", "parallel"),
    )(c_pad, ada_w, ada_b_loc)


def _adamw_math(w, g, m, v):
    m = ADAM_B1 * m + (1.0 - ADAM_B1) * g
    v = ADAM_B2 * v + (1.0 - ADAM_B2) * (g * g)
    m_hat = m / (1.0 - ADAM_B1 ** ADAM_STEP)
    v_hat = v / (1.0 - ADAM_B2 ** ADAM_STEP)
    delta = -ADAM_LR * (m_hat / (jnp.sqrt(v_hat) + ADAM_EPS) + ADAM_WD * w)
    return delta, m, v


def adamw(name, w, g, m, v):
    shape = w.shape
    cols = shape[-1]
    rows = w.size // cols
    tr = rows
    for cand in (512, 256, 128, 64, 32, 16, 8):
        if rows % cand == 0 and cand * cols * 4 <= (2 << 20):
            tr = cand
            break

    def body(w_ref, g_ref, m_ref, v_ref, d_ref, mo_ref, vo_ref):
        d_ref[...], mo_ref[...], vo_ref[...] = _adamw_math(w_ref[...], g_ref[...], m_ref[...], v_ref[...])

    spec = pl.BlockSpec((tr, cols), lambda i: (i, 0))
    outs = pl.pallas_call(
        body, name=name, grid=(rows // tr,), in_specs=[spec] * 4, out_specs=[spec] * 3,
        out_shape=[jax.ShapeDtypeStruct((rows, cols), F32)] * 3,
        compiler_params=_params("parallel"),
    )(*[a.reshape(rows, cols) for a in (w, g, m, v)])
    return [o.reshape(shape) for o in outs]


def adamw_ada(c_pad, dmod, w, m, v):
    tr = 256

    def body(c_ref, dm_ref, w_ref, m_ref, v_ref, g_ref, d_ref, mo_ref, vo_ref):
        c = c_ref[...]
        sc = (c * jax.nn.sigmoid(c)).astype(BF16)
        g = _dot(sc, dm_ref[...].astype(BF16), TN)
        g_ref[...] = g
        d_ref[...], mo_ref[...], vo_ref[...] = _adamw_math(w_ref[...], g, m_ref[...], v_ref[...])

    wspec = pl.BlockSpec((None, tr, MOD_COLS), lambda i, r: (i, r, 0))
    return pl.pallas_call(
        body, name="adamw_ada", grid=(2, D // tr),
        in_specs=[pl.BlockSpec((16, tr), lambda i, r: (0, r)),
                  pl.BlockSpec((None, 16, MOD_COLS), lambda i, r: (i, 0, 0)), wspec, wspec, wspec],
        out_specs=[wspec] * 4,
        out_shape=[jax.ShapeDtypeStruct((2, D, MOD_COLS), F32)] * 4,
        compiler_params=_params("parallel", "parallel"),
    )(c_pad, dmod, w, m, v)


def sum_devices(name, a):
    _, R, C = a.shape
    tr = R
    for cand in (64, 32, 16, 8):
        if R % cand == 0:
            tr = cand
            break

    def body(a_ref, o_ref):
        acc = a_ref[0]
        for dev in range(1, N_DEV):
            acc = acc + a_ref[dev]
        o_ref[...] = acc

    return pl.pallas_call(
        body, name=name, grid=(R // tr,),
        in_specs=[pl.BlockSpec((N_DEV, tr, C), lambda i: (0, i, 0))],
        out_specs=pl.BlockSpec((tr, C), lambda i: (i, 0)),
        out_shape=jax.ShapeDtypeStruct((R, C), F32),
        compiler_params=_params("parallel"),
    )(a)


def _place():
    return lax.axis_index("x"), lax.axis_index("y"), lax.axis_index("c")


def _other_chips(x, y):
    return [(1 - x, y), (x, 1 - y), (1 - x, 1 - y)]


def gather_devices(name, a):
    m_per, n = a.shape

    def body(x_ref, out_ref, send_sems, recv_sems, local_sem):
        x, y, c = _place()
        me, sibling = (x, y, c), (x, y, 1 - c)
        chips = _other_chips(x, y)

        def rows(px, py, pc):
            return out_ref.at[pl.ds((4 * px + 2 * py + pc) * m_per, m_per), :]

        def copy(k, block, to, src=None):
            return pltpu.make_async_remote_copy(
                src_ref=rows(*block) if src is None else src, dst_ref=rows(*block),
                send_sem=send_sems.at[k], recv_sem=recv_sems.at[k], device_id=to, device_id_type=MESH)

        mine = pltpu.make_async_copy(x_ref, rows(*me), local_sem)
        mine.start()
        first = [copy(0, me, sibling, src=x_ref)]
        first += [copy(1 + j, me, (*chip, c), src=x_ref) for j, chip in enumerate(chips)]
        for cp in first:
            cp.start()
        passed = [copy(4 + j, (*chip, c), sibling) for j, chip in enumerate(chips)]
        for j, chip in enumerate(chips):
            copy(1 + j, (*chip, c), me).wait_recv()
            passed[j].start()
        copy(0, sibling, me).wait_recv()
        for j, chip in enumerate(chips):
            copy(4 + j, (*chip, 1 - c), me).wait_recv()
        for cp in first + passed:
            cp.wait_send()
        mine.wait()

    out = pl.pallas_call(
        body, name=name,
        out_shape=jax.ShapeDtypeStruct((N_DEV * m_per, n), a.dtype),
        in_specs=[pl.BlockSpec(memory_space=pltpu.VMEM)],
        out_specs=pl.BlockSpec(memory_space=pltpu.VMEM),
        scratch_shapes=[pltpu.SemaphoreType.DMA((7,)), pltpu.SemaphoreType.DMA((7,)), pltpu.SemaphoreType.DMA],
        compiler_params=pltpu.CompilerParams(vmem_limit_bytes=VMEM_LIMIT),
    )(a)
    return out.reshape(N_DEV, m_per, n)


_ANY = pl.BlockSpec(memory_space=pl.ANY)


def gather_weights(shards):
    n = len(shards)

    def body(*refs):
        dst = refs[n:2 * n]
        ici_send, ici_recv, d2d_send, d2d_recv = refs[2 * n:]
        x, y, c = _place()
        me = 2 * x + y
        chips = _other_chips(x, y)
        sibling = (x, y, 1 - c)

        def ici(t, r, half):
            cx, cy = chips[r]
            mine = dst[t].at[me, half]
            return pltpu.make_async_remote_copy(
                src_ref=mine, dst_ref=mine,
                send_sem=ici_send.at[t, r], recv_sem=ici_recv.at[t, r], device_id=(cx, cy, c), device_id_type=MESH)

        def d2d(t, r, half):
            cx, cy = chips[r]
            there = dst[t].at[2 * cx + cy, half]
            return pltpu.make_async_remote_copy(
                src_ref=there, dst_ref=there, send_sem=d2d_send.at[t, r], recv_sem=d2d_recv.at[t, r],
                device_id=sibling, device_id_type=MESH)

        for t in range(n):
            for r in range(3):
                ici(t, r, c).start()
        for t in range(n):
            for r in range(3):
                ici(t, r, c).wait_recv()
                d2d(t, r, c).start()
        for t in range(n):
            for r in range(3):
                d2d(t, r, 1 - c).wait_recv()
        for t in range(n):
            for r in range(3):
                ici(t, r, c).wait_send()
                d2d(t, r, c).wait_send()

    return pl.pallas_call(
        body, name="gather_weights",
        out_shape=[jax.ShapeDtypeStruct(s.shape, s.dtype) for s in shards],
        in_specs=[_ANY] * n, out_specs=[_ANY] * n,
        input_output_aliases={t: t for t in range(n)},
        scratch_shapes=[pltpu.SemaphoreType.DMA((n, 3))] * 4,
    )(*shards)


def cast_into_slots(chip, shards):
    steps = 2

    def body(chip_ref, *refs):
        n = len(refs) // 2
        for src, dst in zip(refs[:n], refs[n:]):
            dst[...] = src[...].astype(BF16)

    def spec_in(s):
        return pl.BlockSpec((None, s.shape[1] // steps, s.shape[2]), lambda h, i, chip_ref: (h, i, 0))

    def spec_out(s):
        return pl.BlockSpec((None, None, s.shape[1] // steps, s.shape[2]), lambda h, i, chip_ref: (chip_ref[0], h, i, 0))

    return pl.pallas_call(
        body, name="cast_into_slots",
        grid_spec=pltpu.PrefetchScalarGridSpec(
            num_scalar_prefetch=1, grid=(2, steps),
            in_specs=[spec_in(s) for s in shards], out_specs=[spec_out(s) for s in shards]),
        out_shape=[jax.ShapeDtypeStruct((N_CHIP,) + s.shape, BF16) for s in shards],
        compiler_params=_params("parallel", "parallel"),
    )(chip, *shards)


def reduce_pair(grads):
    n = len(grads)

    def body(*refs):
        src, dst = refs[:n], refs[n:2 * n]
        send_sem, recv_sem = refs[2 * n:]
        x, y, c = _place()
        cps = [pltpu.make_async_remote_copy(
            src_ref=src[t].at[:, 1 - c], dst_ref=dst[t], send_sem=send_sem.at[t], recv_sem=recv_sem.at[t],
            device_id=(x, y, 1 - c), device_id_type=MESH) for t in range(n)]
        for cp in cps:
            cp.start()
        for cp in cps:
            cp.wait()

    return pl.pallas_call(
        body, name="reduce_pair",
        out_shape=[jax.ShapeDtypeStruct((N_CHIP,) + g.shape[2:], g.dtype) for g in grads],
        in_specs=[_ANY] * n, out_specs=[_ANY] * n,
        scratch_shapes=[pltpu.SemaphoreType.DMA((n,))] * 2,
    )(*grads)


def pair_add(name, core, g, got):
    _, _, R, C = g.shape

    def body(core_ref, g_ref, got_ref, o_ref):
        o_ref[...] = (g_ref[...] + got_ref[...]).astype(BF16)

    return pl.pallas_call(
        body, name=name,
        grid_spec=pltpu.PrefetchScalarGridSpec(
            num_scalar_prefetch=1, grid=(N_CHIP,),
            in_specs=[pl.BlockSpec((None, None, R, C), lambda q, core_ref: (q, core_ref[0], 0, 0)),
                      pl.BlockSpec((None, R, C), lambda q, core_ref: (q, 0, 0))],
            out_specs=pl.BlockSpec((None, R, C), lambda q, core_ref: (q, 0, 0))),
        out_shape=jax.ShapeDtypeStruct((N_CHIP, R, C), BF16),
        compiler_params=_params("parallel"),
    )(core, g, got)


def reduce_chips(sums):
    n = len(sums)

    def body(*refs):
        src, dst = refs[:n], refs[n:2 * n]
        send_sem, recv_sem = refs[2 * n:]
        x, y, c = _place()
        chips = _other_chips(x, y)
        cps = []
        for t in range(n):
            for r, (cx, cy) in enumerate(chips):
                cps.append(pltpu.make_async_remote_copy(
                    src_ref=src[t].at[2 * cx + cy], dst_ref=dst[t].at[r],
                    send_sem=send_sem.at[t, r], recv_sem=recv_sem.at[t, r],
                    device_id=(cx, cy, c), device_id_type=MESH))
        for cp in cps:
            cp.start()
        for cp in cps:
            cp.wait()

    return pl.pallas_call(
        body, name="reduce_chips",
        out_shape=[jax.ShapeDtypeStruct((3,) + s.shape[1:], s.dtype) for s in sums],
        in_specs=[_ANY] * n, out_specs=[_ANY] * n,
        scratch_shapes=[pltpu.SemaphoreType.DMA((n, 3))] * 2,
    )(*sums)


def chip_add(name, place, s, got, k, n_slots, prev=None):
    _, R, C = s.shape

    def body(place_ref, s_ref, got_ref, *rest):
        o_ref = rest[-1]
        o_ref[...] = ((s_ref[...].astype(F32) + got_ref[0].astype(F32)) + got_ref[1].astype(F32)) + got_ref[2].astype(F32)

    in_specs = [pl.BlockSpec((None, R, C), lambda i, place_ref: (place_ref[0], 0, 0)),
                pl.BlockSpec((3, R, C), lambda i, place_ref: (0, 0, 0))]
    args = [place, s, got]
    aliases = {}
    if prev is not None:
        in_specs.append(_ANY)
        args.append(prev)
        aliases = {3: 0}
    return pl.pallas_call(
        body, name=name,
        grid_spec=pltpu.PrefetchScalarGridSpec(
            num_scalar_prefetch=1, grid=(1,), in_specs=in_specs,
            out_specs=pl.BlockSpec((None, None, R, C), lambda i, place_ref: (k, place_ref[1], 0, 0))),
        out_shape=jax.ShapeDtypeStruct((n_slots, 2, R, C), F32),
        input_output_aliases=aliases,
        compiler_params=_params("arbitrary"),
    )(*args)


def share_halves(stacks):
    n = len(stacks)

    def body(*refs):
        dst = refs[n:2 * n]
        send_sem, recv_sem = refs[2 * n:]
        x, y, c = _place()
        cps = [pltpu.make_async_remote_copy(
            src_ref=dst[t].at[:, c], dst_ref=dst[t].at[:, c], send_sem=send_sem.at[t], recv_sem=recv_sem.at[t],
            device_id=(x, y, 1 - c), device_id_type=MESH) for t in range(n)]
        for cp in cps:
            cp.start()
        for cp in cps:
            cp.wait()

    return pl.pallas_call(
        body, name="share_halves",
        out_shape=[jax.ShapeDtypeStruct(s.shape, F32) for s in stacks],
        in_specs=[_ANY] * n, out_specs=[_ANY] * n,
        input_output_aliases={t: t for t in range(n)},
        scratch_shapes=[pltpu.SemaphoreType.DMA((n,))] * 2,
    )(*stacks)


def _swap_rope(a):
    return jnp.concatenate([a[..., ROPE // 2:], a[..., :ROPE // 2]], axis=-1)


def _rope_tables(S):
    inv = 1.0 / (ROPE_THETA ** (jnp.arange(0, ROPE, 2, dtype=F32) / ROPE))
    ang = jnp.arange(S, dtype=F32)[:, None] * inv[None, :]
    cos, sin = jnp.cos(ang), jnp.sin(ang)
    return jnp.concatenate([cos, cos], axis=1), jnp.concatenate([-sin, sin], axis=1)


def _vec(norm_g, mod, i, k):
    rows = [norm_g[i, 2 * k], norm_g[i, 2 * k + 1], mod[i, 3 * k], mod[i, 3 * k + 1], mod[i, 3 * k + 2]]
    return jnp.concatenate([jnp.stack(rows), jnp.zeros((3, D), F32)], axis=0)


def _example_step(x, target, mod, norm_g, pvec, ffn_in, ffn_out, pw, mw, wuv, wo):
    S = x.shape[0]
    tabs = _rope_tables(S)
    vec = [[_vec(norm_g, mod, i, k) for k in range(3)] for i in range(2)]
    saved = {}
    for i in range(2):
        xin = x
        x, a, u, h = ffn_fwd(xin, vec[i][0], ffn_in[i][0], ffn_out[i][0], 0.5)
        saved[i, 0] = (xin, a, u, h)
        xin = x
        if i == 0:
            x, y, z = pool_fwd(xin, vec[i][1], pw, pvec)
            saved[i, 1] = (xin, y, z)
        else:
            h_m, cq_raw, ckv_raw, cqn, qnope, qcat, kcat, vcat = mla_pre(xin, vec[i][1], mw, tabs)
            olat, lse = attn_fwd(qcat, kcat, vcat)
            x, u_m, ocat = mla_post(olat, xin, vec[i][1], wuv, wo)
            saved[i, 1] = (xin, h_m, cq_raw, ckv_raw, cqn, qnope, qcat, kcat, olat, lse, u_m, ocat)
        xin = x
        x, a, u, h = ffn_fwd(xin, vec[i][2], ffn_in[i][1], ffn_out[i][1], 0.5)
        saved[i, 2] = (xin, a, u, h)
    loss, dx = loss_head(x, target)

    vg = {}
    gw = {}
    for i in (1, 0):
        xin, a, u, h = saved[i, 2]
        dx, du, act, da, vg[i, 2] = ffn_bwd(dx, xin, u, a, vec[i][2], ffn_in[i][1], ffn_out[i][1], 0.5)
        gw["ffn_in", i, 1], gw["ffn_out", i, 1] = ffn_dw(h, da, act, du)
        if i == 0:
            xin, y, z = saved[i, 1]
            dx, vg[i, 1], pgrad, gw["pool"] = pool_bwd(dx, xin, y, z, vec[i][1], pw, pvec)
        else:
            xin, h_m, cq_raw, ckv_raw, cqn, qnope, qcat, kcat, olat, lse, u_m, ocat = saved[i, 1]
            du, docat, dolat, delta, vg_post = mla_post_bwd(dx, u_m, olat, vec[i][1], wuv, wo)
            dq, dk, dv = attn_bwd(qcat, kcat, kcat.T, dolat, lse.reshape(N_HEADS, 1, S), delta.reshape(N_HEADS, 1, S))
            dx, dlat, dka, dkb, dqn, dql, dqa, dqb, vg_pre, ngrad = mla_pre_bwd(
                dx, dq, dk, dv, xin, cq_raw, ckv_raw, vec[i][1], mw, tabs)
            vg[i, 1] = vg_post + vg_pre
            g = mla_dw(h_m, dlat, dka, dkb, cqn, dqn, dqa, dqb, dql, qnope, olat, docat, ocat, du)
            gw["mla_in"] = jnp.concatenate([g["in"], g["kr"] + _swap_rope(g["krs"])], axis=1)
            gw["uq"] = jnp.concatenate([jnp.transpose(g["n"], (1, 0, 2)),
                                        jnp.transpose(g["r"] + _swap_rope(g["rs"]), (1, 0, 2))], axis=-1)
            gw["uk"] = jnp.transpose(g["uk"], (1, 0, 2))
            gw["uv"] = jnp.transpose(g["uv"], (1, 0, 2))
            gw["wo"] = g["o"].reshape(D, D)
        xin, a, u, h = saved[i, 0]
        dx, du, act, da, vg[i, 0] = ffn_bwd(dx, xin, u, a, vec[i][0], ffn_in[i][0], ffn_out[i][0], 0.5)
        gw["ffn_in", i, 0], gw["ffn_out", i, 0] = ffn_dw(h, da, act, du)
    return loss, dx, gw, vg, pgrad, ngrad


SMALL_IN = 8 * 640
SMALL_GRAD = 8 * 4224
SMALL_W = 8 * 2944


def _pack(parts, total):
    flat = jnp.concatenate([p.reshape(-1) for p in parts])
    return jnp.concatenate([flat, jnp.zeros((total - flat.shape[0],), F32)]).reshape(8, total // 8)


def kernel(x, c, ada_w, ada_b, norm_g, ffn_w_in, ffn_w_out, pool_w, pool_b, pool_scale, mla_w_in, mla_q_norm, mla_kv_norm, mla_w_uq, mla_w_uk, mla_w_uv, mla_w_o, loss_target, m_ada_w, m_ada_b, m_norm_g, m_ffn_w_in, m_ffn_w_out, m_pool_w, m_pool_b, m_pool_scale, m_mla_w_in, m_mla_q_norm, m_mla_kv_norm, m_mla_w_uq, m_mla_w_uk, m_mla_w_uv, m_mla_w_o, v_ada_w, v_ada_b, v_norm_g, v_ffn_w_in, v_ffn_w_out, v_pool_w, v_pool_b, v_pool_scale, v_mla_w_in, v_mla_q_norm, v_mla_kv_norm, v_mla_w_uq, v_mla_w_uk, v_mla_w_uv, v_mla_w_o):
    ix, iy, ic = _place()
    chip = 2 * ix + iy
    dev = 2 * chip + ic
    core_arr = ic.astype(jnp.int32).reshape(1)
    chip_arr = chip.astype(jnp.int32).reshape(1)
    S = x.shape[1]
    G = D // 4
    NG = D // N_CHIP

    def chip_cols(a, width, axis):
        return lax.dynamic_slice_in_dim(a, chip * width, width, axis)

    got = gather_devices("gather_small_in", _pack([c, norm_g, pool_b, mla_q_norm], SMALL_IN)).reshape(N_DEV, SMALL_IN)
    c_all = got[:, :D]
    parts = got[0::2]
    o = D
    norm_g_full = parts[:, o:o + 12 * NG].reshape(N_CHIP, 2, 6, NG).transpose(1, 2, 0, 3).reshape(2, 6, D)
    o += 12 * NG
    pool_b_full = parts[:, o:o + G].reshape(N_CHIP, 4, G // N_CHIP).transpose(1, 0, 2).reshape(1, D)
    o += G
    q_norm_full = parts[:, o:o + QL // N_CHIP].reshape(1, QL)
    pvec = jnp.concatenate([pool_b_full, pool_scale, jnp.zeros((6, D), F32)], axis=0)

    c_pad = jnp.concatenate([c_all, jnp.zeros((8, D), F32)], axis=0)
    mod_loc = mod_fwd(c_pad, ada_w, chip_cols(ada_b, MOD_COLS, 1).reshape(2, 1, MOD_COLS))
    got = gather_devices("gather_mod", mod_loc[:, :8].transpose(1, 0, 2).reshape(8, 2 * MOD_COLS))
    mine = lax.dynamic_index_in_dim(got[0::2].reshape(N_CHIP, 8, 2, MOD_COLS), dev, axis=1, keepdims=False)
    mod = mine.transpose(1, 0, 2).reshape(2, 9, D)

    bf = lambda a: a.astype(BF16)
    shards = [ffn_w_in[i, k].reshape(2, D // 2, FSH) for i in range(2) for k in range(2)]
    shards += [ffn_w_out[i, k].reshape(2, DFF // 8, D) for i in range(2) for k in range(2)]
    shards += [pool_w[0].reshape(2, 2 * G // N_CHIP, G), mla_w_in[0].reshape(2, D // 8, QL + KVL + ROPE),
               mla_w_uq[0].reshape(2, QL // 8, N_HEADS * (NOPE + ROPE)), mla_w_o[0].reshape(2, D // 8, D)]
    full = gather_weights(cast_into_slots(chip_arr, shards))
    ffn_in = [[full[2 * i + k].reshape(N_CHIP, D, FSH) for k in range(2)] for i in range(2)]
    ffn_out = [[full[4 + 2 * i + k].reshape(2, FSH, D) for k in range(2)] for i in range(2)]
    pw = full[8].reshape(N_CHIP, 4, G // N_CHIP, G).transpose(1, 0, 2, 3).reshape(4, G, G)
    w_in = full[9].reshape(D, QL + KVL + ROPE)
    w_uq = full[10].reshape(QL, N_HEADS, NOPE + ROPE)
    wkr = w_in[:, QL + KVL:]
    wr = jnp.transpose(w_uq[:, :, NOPE:], (1, 0, 2))
    mw = dict(wq=w_in[:, :QL], wkv=w_in[:, QL:QL + KVL], wkr=wkr, wkrs=_swap_rope(wkr),
              qn=q_norm_full, kvn=mla_kv_norm, wn=jnp.transpose(w_uq[:, :, :NOPE], (1, 0, 2)),
              wr=wr, wrs=_swap_rope(wr), wuk=jnp.transpose(bf(mla_w_uk[0]), (1, 0, 2)))
    wuv = jnp.transpose(bf(mla_w_uv[0]), (1, 0, 2))
    wo = full[11].reshape(N_HEADS, VH, D)

    loss_mine, grad_x, gw, vg, pgrad, ngrad = _example_step(
        x[0], loss_target[0], mod, norm_g_full, pvec, ffn_in, ffn_out, pw, mw, wuv, wo)
    loss = lax.psum(loss_mine[0, 0], ("x", "y", "c"))

    grads = [gw["ffn_in", i, k].reshape(N_CHIP, 2, D // 2, FSH) for i in range(2) for k in range(2)]
    grads += [gw["ffn_out", i, k].reshape(N_CHIP, 2, DFF // 8, D) for i in range(2) for k in range(2)]
    grads += [gw["pool"].reshape(N_CHIP, 2, 2 * G // N_CHIP, G),
              gw["mla_in"].reshape(N_CHIP, 2, D // 8, QL + KVL + ROPE),
              gw["uq"].reshape(N_CHIP, 2, QL // 8, N_HEADS * (NOPE + ROPE)),
              gw["wo"].reshape(N_CHIP, 2, D // 8, D)]
    from_pair = reduce_pair(grads)
    sums = [pair_add(f"pair_add_{t}", core_arr, g, p) for t, (g, p) in enumerate(zip(grads, from_pair))]
    from_chips = reduce_chips(sums)
    place_arr = jnp.stack([chip, ic]).astype(jnp.int32)
    stacks = []
    for slots in ((0, 1, 2, 3), (4, 5, 6, 7), (8,), (9,), (10,), (11,)):
        stack = None
        for k, t in enumerate(slots):
            stack = chip_add(f"chip_add_{t}", place_arr, sums[t], from_chips[t], k, len(slots), stack)
        stacks.append(stack)
    g_ffn_in, g_ffn_out, g_pool_w, g_mla_in, g_uq, g_wo = share_halves(stacks)
    g_ffn_in = g_ffn_in.reshape(ffn_w_in.shape)
    g_ffn_out = g_ffn_out.reshape(ffn_w_out.shape)
    g_pool_w = g_pool_w.reshape(pool_w.shape)
    g_mla_in = g_mla_in.reshape(mla_w_in.shape)
    g_uq = g_uq.reshape(mla_w_uq.shape)
    g_wo = g_wo.reshape(mla_w_o.shape)

    ukv = jnp.concatenate([gw["uk"].reshape(KVL, N_HEADS * NOPE), gw["uv"].reshape(KVL, N_HEADS * VH)], axis=0)
    ukv = sum_devices("sum_ukv", gather_devices("gather_ukv", ukv))
    g_uk = ukv[:KVL].reshape(mla_w_uk.shape)
    g_uv = ukv[KVL:].reshape(mla_w_uv.shape)

    dmod = jnp.stack([jnp.concatenate([vg[i, k][0:3] for k in range(3)]) for i in range(2)])
    dnorm = jnp.stack([jnp.concatenate([vg[i, k][3:5] for k in range(3)]) for i in range(2)])
    small = _pack([dmod, dnorm, pgrad[0], pgrad[1], ngrad[0], ngrad[1, :KVL]], SMALL_GRAD)
    got = gather_devices("gather_small_grad", small)
    tot = sum_devices("sum_small_grad", got).reshape(-1)
    n_mod = 2 * 9 * D
    g_ada_b = tot[:n_mod].reshape(ada_b.shape)
    o = n_mod
    g_norm = chip_cols(tot[o:o + 12 * D].reshape(2, 6, D), NG, 2)
    o += 12 * D
    g_pool_b = chip_cols(tot[o:o + D].reshape(1, 4, G), G // N_CHIP, 2)
    o += D
    g_pool_scale = tot[o:o + D].reshape(pool_scale.shape)
    o += D
    g_q_norm = chip_cols(tot[o:o + QL].reshape(1, QL), QL // N_CHIP, 1)
    o += QL
    g_kv_norm = tot[o:o + KVL].reshape(mla_kv_norm.shape)
    dmod_all = chip_cols(got.reshape(N_DEV, -1)[:, :n_mod].reshape(N_DEV, 2, 9 * D), MOD_COLS, 2)
    dmod_pad = jnp.concatenate([dmod_all.transpose(1, 0, 2), jnp.zeros((2, 8, MOD_COLS), F32)], axis=1)

    g_ada_w, d_ada_w, nm_ada_w, nv_ada_w = adamw_ada(c_pad, dmod_pad, ada_w, m_ada_w, v_ada_w)
    small_names = ["ada_b", "norm_g", "pool_b", "pool_scale", "mla_q_norm", "mla_kv_norm"]
    small_w = [ada_b, norm_g, pool_b, pool_scale, mla_q_norm, mla_kv_norm]
    small_g = [g_ada_b, g_norm, g_pool_b, g_pool_scale, g_q_norm, g_kv_norm]
    small_m = [m_ada_b, m_norm_g, m_pool_b, m_pool_scale, m_mla_q_norm, m_mla_kv_norm]
    small_v = [v_ada_b, v_norm_g, v_pool_b, v_pool_scale, v_mla_q_norm, v_mla_kv_norm]
    packed = adamw("adamw_small", *[_pack(p, SMALL_W) for p in (small_w, small_g, small_m, small_v)])
    upd = {}
    o = 0
    for name, w in zip(small_names, small_w):
        upd[name] = [p.reshape(-1)[o:o + w.size].reshape(w.shape) for p in packed]
        o += w.size
    big = [("ffn_w_in", ffn_w_in, g_ffn_in, m_ffn_w_in, v_ffn_w_in),
           ("ffn_w_out", ffn_w_out, g_ffn_out, m_ffn_w_out, v_ffn_w_out),
           ("pool_w", pool_w, g_pool_w, m_pool_w, v_pool_w),
           ("mla_w_in", mla_w_in, g_mla_in, m_mla_w_in, v_mla_w_in),
           ("mla_w_uq", mla_w_uq, g_uq, m_mla_w_uq, v_mla_w_uq),
           ("mla_w_uk", mla_w_uk, g_uk, m_mla_w_uk, v_mla_w_uk),
           ("mla_w_uv", mla_w_uv, g_uv, m_mla_w_uv, v_mla_w_uv),
           ("mla_w_o", mla_w_o, g_wo, m_mla_w_o, v_mla_w_o)]
    for name, w, g, m, v in big:
        upd[name] = adamw("adamw_" + name, w, g, m, v)
    upd["ada_w"] = [d_ada_w, nm_ada_w, nv_ada_w]

    order = ["ada_w", "ada_b", "norm_g", "ffn_w_in", "ffn_w_out", "pool_w", "pool_b", "pool_scale", "mla_w_in",
             "mla_q_norm", "mla_kv_norm", "mla_w_uq", "mla_w_uk", "mla_w_uv", "mla_w_o"]
    grad = dict(ada_w=g_ada_w, ada_b=g_ada_b, norm_g=g_norm, ffn_w_in=g_ffn_in, ffn_w_out=g_ffn_out, pool_w=g_pool_w,
                pool_b=g_pool_b, pool_scale=g_pool_scale, mla_w_in=g_mla_in, mla_q_norm=g_q_norm,
                mla_kv_norm=g_kv_norm, mla_w_uq=g_uq, mla_w_uk=g_uk, mla_w_uv=g_uv, mla_w_o=g_wo)
    return (loss, grad_x[None], *[grad[n] for n in order], *[upd[n][0] for n in order],
            *[upd[n][1] for n in order], *[upd[n][2] for n in order])
```

```python
import functools

import jax
import jax.numpy as jnp
from jax import lax
from jax.experimental import pallas as pl
from jax.experimental.pallas import tpu as pltpu
from jax.experimental.pallas import tpu_sc as plsc

F32 = jnp.float32
BF16 = jnp.bfloat16

D = 1024
DFF = 2816
FSH = 1408
N_CHIP = 4
N_DEV = 8
N_HEADS = 16
NOPE = 64
ROPE = 32
VH = 64
QL = 256
KVL = 128
QPAD = 256
EPS = 1e-6
ATTN_SCALE = (NOPE + ROPE) ** -0.5
ROPE_THETA = 10000.0
POOL_WINDOWS = (2, 4, 8, 16)
HALO = 8
ATTN_TQ = 1024
ATTN_KC = 512

ADAM_LR, ADAM_B1, ADAM_B2, ADAM_EPS, ADAM_WD, ADAM_STEP = 0.001, 0.9, 0.999, 1e-08, 0.01, 10

VMEM_LIMIT = 60 * 1024 * 1024
MESH = pl.DeviceIdType.MESH

NT = (((1,), (1,)), ((), ()))
TN = (((0,), (0,)), ((), ()))


def _params(*sem):
    return pltpu.CompilerParams(dimension_semantics=sem, vmem_limit_bytes=VMEM_LIMIT)


def _dot(a, b, dims=None):
    if dims is None:
        return jnp.dot(a, b, preferred_element_type=F32)
    return lax.dot_general(a, b, dims, preferred_element_type=F32)


def _rms(x):
    r = lax.rsqrt(jnp.mean(x * x, axis=-1, keepdims=True) + EPS)
    return x * r, r


def _rms_bwd(xhat, r, dxhat):
    return r * (dxhat - xhat * jnp.mean(dxhat * xhat, axis=-1, keepdims=True))


def _prenorm(x, vec_ref):
    xhat, r = _rms(x)
    h = xhat * vec_ref[0:1, :] * (1.0 + vec_ref[3:4, :]) + vec_ref[2:3, :]
    return h, xhat, r


def _postnorm_bwd(dout, u, vec_ref, weight):
    uhat, r = _rms(u)
    gt = weight * (1.0 + vec_ref[4:5, :])
    dy = dout * gt
    dgate_rows = (weight * dout) * (uhat * vec_ref[1:2, :])
    dgpost_rows = dy * uhat
    du = _rms_bwd(uhat, r, dy * vec_ref[1:2, :])
    return du, dgate_rows, dgpost_rows


def _prenorm_bwd(dh, x, vec_ref, vg_ref):
    xhat, r = _rms(x)
    sc1 = 1.0 + vec_ref[3:4, :]
    g = vec_ref[0:1, :]
    vg_ref[0:1, :] += jnp.sum(dh, axis=0, keepdims=True)
    vg_ref[1:2, :] += jnp.sum(dh * (xhat * g), axis=0, keepdims=True)
    vg_ref[3:4, :] += jnp.sum(dh * sc1 * xhat, axis=0, keepdims=True)
    return _rms_bwd(xhat, r, dh * g * sc1)


def ffn_fwd(x, vec, w_in, w_out, weight):
    S = x.shape[0]
    tm = min(256, S)

    def body(x_ref, vec_ref, wg_ref, wu_ref, wo_ref, xo_ref, a_ref, u_ref, h_ref, acc_ref):
        j = pl.program_id(1)

        @pl.when(j == 0)
        def _():
            h, _, _ = _prenorm(x_ref[...], vec_ref)
            h_ref[...] = h.astype(BF16)
            acc_ref[...] = jnp.zeros_like(acc_ref)

        hb = h_ref[...]
        g = _dot(hb, wg_ref[...])
        up = _dot(hb, wu_ref[...])
        a_ref[0] = g.astype(BF16)
        a_ref[1] = up.astype(BF16)
        act = (g * jax.nn.sigmoid(g)) * up
        acc_ref[...] += _dot(act.astype(BF16), wo_ref[...])

        @pl.when(j == 1)
        def _():
            u = acc_ref[...]
            u_ref[...] = u
            uhat, _ = _rms(u)
            xo_ref[...] = x_ref[...] + (weight * (1.0 + vec_ref[4:5, :])) * (uhat * vec_ref[1:2, :])

    return pl.pallas_call(
        body, name="ffn_fwd", grid=(S // tm, 2),
        in_specs=[pl.BlockSpec((tm, D), lambda i, j: (i, 0)),
                  pl.BlockSpec((8, D), lambda i, j: (0, 0)),
                  pl.BlockSpec((None, D, FSH), lambda i, j: (j, 0, 0)),
                  pl.BlockSpec((None, D, FSH), lambda i, j: (j + 2, 0, 0)),
                  pl.BlockSpec((None, FSH, D), lambda i, j: (j, 0, 0))],
        out_specs=[pl.BlockSpec((tm, D), lambda i, j: (i, 0)),
                   pl.BlockSpec((2, tm, FSH), lambda i, j: (0, i, j)),
                   pl.BlockSpec((tm, D), lambda i, j: (i, 0)),
                   pl.BlockSpec((tm, D), lambda i, j: (i, 0))],
        out_shape=[jax.ShapeDtypeStruct((S, D), F32), jax.ShapeDtypeStruct((2, S, DFF), BF16),
                   jax.ShapeDtypeStruct((S, D), F32), jax.ShapeDtypeStruct((S, D), BF16)],
        scratch_shapes=[pltpu.VMEM((tm, D), F32)],
        compiler_params=_params("parallel", "arbitrary"),
    )(x, vec, w_in, w_in, w_out)


def ffn_bwd(dout, x, u, a, vec, w_in, w_out, weight):
    S = x.shape[0]
    tm = min(256, S)

    def body(do_ref, x_ref, u_ref, a_ref, vec_ref, wg_ref, wu_ref, wo_ref,
             dx_ref, du_ref, act_ref, da_ref, vg_ref, dh_ref):
        i, j = pl.program_id(0), pl.program_id(1)

        @pl.when((i == 0) & (j == 0))
        def _():
            vg_ref[...] = jnp.zeros_like(vg_ref)

        @pl.when(j == 0)
        def _():
            du, dgate_rows, dgpost_rows = _postnorm_bwd(do_ref[...], u_ref[...], vec_ref, weight)
            vg_ref[2:3, :] += jnp.sum(dgate_rows, axis=0, keepdims=True)
            vg_ref[4:5, :] += jnp.sum(dgpost_rows, axis=0, keepdims=True)
            du_ref[...] = du.astype(BF16)
            dh_ref[...] = jnp.zeros_like(dh_ref)

        dact = _dot(du_ref[...], wo_ref[...], NT)
        g = a_ref[0].astype(F32)
        up = a_ref[1].astype(F32)
        s = jax.nn.sigmoid(g)
        silu = g * s
        act_ref[...] = (silu * up).astype(BF16)
        dg = (dact * up * (s * (1.0 + g * (1.0 - s)))).astype(BF16)
        dup = (dact * silu).astype(BF16)
        da_ref[0] = dg
        da_ref[1] = dup
        dh_ref[...] += _dot(dg, wg_ref[...], NT) + _dot(dup, wu_ref[...], NT)

        @pl.when(j == 1)
        def _():
            dx_ref[...] = do_ref[...] + _prenorm_bwd(dh_ref[...], x_ref[...], vec_ref, vg_ref)

    row = lambda i, j: (i, 0)
    return pl.pallas_call(
        body, name="ffn_bwd", grid=(S // tm, 2),
        in_specs=[pl.BlockSpec((tm, D), row), pl.BlockSpec((tm, D), row), pl.BlockSpec((tm, D), row),
                  pl.BlockSpec((2, tm, FSH), lambda i, j: (0, i, j)),
                  pl.BlockSpec((8, D), lambda i, j: (0, 0)),
                  pl.BlockSpec((None, D, FSH), lambda i, j: (j, 0, 0)),
                  pl.BlockSpec((None, D, FSH), lambda i, j: (j + 2, 0, 0)),
                  pl.BlockSpec((None, FSH, D), lambda i, j: (j, 0, 0))],
        out_specs=[pl.BlockSpec((tm, D), row), pl.BlockSpec((tm, D), row),
                   pl.BlockSpec((tm, FSH), lambda i, j: (i, j)),
                   pl.BlockSpec((2, tm, FSH), lambda i, j: (0, i, j)),
                   pl.BlockSpec((8, D), lambda i, j: (0, 0))],
        out_shape=[jax.ShapeDtypeStruct((S, D), F32), jax.ShapeDtypeStruct((S, D), BF16),
                   jax.ShapeDtypeStruct((S, DFF), BF16), jax.ShapeDtypeStruct((2, S, DFF), BF16),
                   jax.ShapeDtypeStruct((8, D), F32)],
        scratch_shapes=[pltpu.VMEM((tm, D), F32)],
        compiler_params=_params("arbitrary", "arbitrary"),
    )(dout, x, u, a, vec, w_in, w_in, w_out)


def dw_matmul(name, a, b, a_spec, b_spec, out_shape, out_spec, grid):
    def body(a_ref, b_ref, o_ref):
        @pl.when(pl.program_id(len(grid) - 1) == 0)
        def _():
            o_ref[...] = jnp.zeros_like(o_ref)

        o_ref[...] += _dot(a_ref[...], b_ref[...], TN)

    return pl.pallas_call(
        body, name=name, grid=grid, in_specs=[a_spec, b_spec], out_specs=out_spec,
        out_shape=jax.ShapeDtypeStruct(out_shape, F32),
        compiler_params=_params(*(["parallel"] * (len(grid) - 1) + ["arbitrary"])),
    )(a, b)


def ffn_dw(h, da, act, du):
    S = h.shape[0]
    tk = min(512, S)
    dw_in = dw_matmul("ffn_dw_in", h, da,
                      pl.BlockSpec((tk, D), lambda n, k: (k, 0)),
                      pl.BlockSpec((None, tk, FSH), lambda n, k: (n // 2, k, n % 2)),
                      (N_CHIP, D, FSH), pl.BlockSpec((None, D, FSH), lambda n, k: (n, 0, 0)),
                      (N_CHIP, S // tk))
    dw_out = dw_matmul("ffn_dw_out", act, du,
                       pl.BlockSpec((tk, FSH), lambda n, k: (k, n)),
                       pl.BlockSpec((tk, D), lambda n, k: (k, 0)),
                       (DFF, D), pl.BlockSpec((FSH, D), lambda n, k: (n, 0)),
                       (2, S // tk))
    return dw_in, dw_out


def _halo_specs(tm, S):
    nb = tm // HALO
    last = S // HALO - 1
    return [pl.BlockSpec((HALO, D), lambda i: (jnp.maximum(i * nb - 1, 0), 0)),
            pl.BlockSpec((tm, D), lambda i: (i, 0)),
            pl.BlockSpec((HALO, D), lambda i: (jnp.minimum((i + 1) * nb, last), 0))]


def _shift_rows(v, k):
    return pltpu.roll(v, k % v.shape[0], 0)


def _window_sum(v, g, forward):
    acc = v + _shift_rows(v, 1 if forward else -1)
    for step in (1, 2, 4)[:g]:
        acc = _shift_rows(acc, step) + _shift_rows(acc, -step)
    return acc


def _pool_count(t, w, S):
    return jnp.maximum(jnp.minimum(t + w // 2, S) - jnp.maximum(t - w // 2, 0), 1).astype(F32)


def pool_fwd(x, vec, pw, pvec):
    S = x.shape[0]
    tm = min(256, S)
    G = D // 4

    def body(xp_ref, x_ref, xn_ref, vec_ref, pw_ref, pv_ref, xo_ref, y_ref, z_ref):
        i = pl.program_id(0)
        xa = jnp.concatenate([xp_ref[...], x_ref[...], xn_ref[...]], axis=0)
        t = i * tm - HALO + lax.broadcasted_iota(jnp.int32, (tm + 2 * HALO, 1), 0)
        h, _, _ = _prenorm(xa, vec_ref)
        h = jnp.where((t >= 0) & (t < S), h, 0.0)
        tmain = t[HALO:HALO + tm]
        for g in range(4):
            hg = h[:, g * G:(g + 1) * G]
            pooled = _window_sum(hg, g, True)[HALO:HALO + tm] / _pool_count(tmain, POOL_WINDOWS[g], S)
            z = (pooled - hg[HALO:HALO + tm]).astype(BF16)
            z_ref[:, g * G:(g + 1) * G] = z
            y_ref[:, g * G:(g + 1) * G] = _dot(z, pw_ref[g]) + pv_ref[0:1, g * G:(g + 1) * G]
        u = y_ref[...] * pv_ref[1:2, :]
        uhat, _ = _rms(u)
        xo_ref[...] = x_ref[...] + (1.0 + vec_ref[4:5, :]) * (uhat * vec_ref[1:2, :])

    row = lambda i: (i, 0)
    full = lambda i: (0, 0)
    return pl.pallas_call(
        body, name="pool_fwd", grid=(S // tm,),
        in_specs=_halo_specs(tm, S) + [pl.BlockSpec((8, D), full), pl.BlockSpec((4, G, G), lambda i: (0, 0, 0)),
                                       pl.BlockSpec((8, D), full)],
        out_specs=[pl.BlockSpec((tm, D), row)] * 3,
        out_shape=[jax.ShapeDtypeStruct((S, D), F32), jax.ShapeDtypeStruct((S, D), F32),
                   jax.ShapeDtypeStruct((S, D), BF16)],
        compiler_params=_params("parallel"),
    )(x, x, x, vec, pw, pvec)


def pool_bwd(dout, x, y, z, vec, pw, pvec):
    S = x.shape[0]
    tm = min(256, S)
    G = D // 4
    R = G // N_CHIP

    def body(dop_ref, do_ref, don_ref, yp_ref, y_ref, yn_ref, x_ref, z_ref, vec_ref, pw_ref, pv_ref,
             dx_ref, vg_ref, pg_ref, dw_ref, dh_ref):
        i = pl.program_id(0)

        @pl.when(i == 0)
        def _():
            vg_ref[...] = jnp.zeros_like(vg_ref)
            pg_ref[...] = jnp.zeros_like(pg_ref)
            dw_ref[...] = jnp.zeros_like(dw_ref)

        doa = jnp.concatenate([dop_ref[...], do_ref[...], don_ref[...]], axis=0)
        ya = jnp.concatenate([yp_ref[...], y_ref[...], yn_ref[...]], axis=0)
        t = i * tm - HALO + lax.broadcasted_iota(jnp.int32, (tm + 2 * HALO, 1), 0)
        inside = (t >= 0) & (t < S)
        main = (t >= i * tm) & (t < (i + 1) * tm)
        du, dgate_rows, dgpost_rows = _postnorm_bwd(doa, ya * pv_ref[1:2, :], vec_ref, 1.0)
        du = jnp.where(inside, du, 0.0)
        vg_ref[2:3, :] += jnp.sum(jnp.where(main, dgate_rows, 0.0), axis=0, keepdims=True)
        vg_ref[4:5, :] += jnp.sum(jnp.where(main, dgpost_rows, 0.0), axis=0, keepdims=True)
        dy = du * pv_ref[1:2, :]
        pg_ref[0:1, :] += jnp.sum(jnp.where(main, dy, 0.0), axis=0, keepdims=True)
        pg_ref[1:2, :] += jnp.sum(jnp.where(main, du * ya, 0.0), axis=0, keepdims=True)
        for g in range(4):
            dyg = dy[:, g * G:(g + 1) * G].astype(BF16)
            dz = _dot(dyg, pw_ref[g], NT)
            e = dz / _pool_count(t, POOL_WINDOWS[g], S)
            dh_ref[:, g * G:(g + 1) * G] = (_window_sum(e, g, False) - dz)[HALO:HALO + tm]
            dwg = _dot(z_ref[:, g * G:(g + 1) * G], dyg[HALO:HALO + tm], TN)
            for q in range(N_CHIP):
                dw_ref[q, g] += dwg[q * R:(q + 1) * R, :]
        dx_ref[...] = do_ref[...] + _prenorm_bwd(dh_ref[...], x_ref[...], vec_ref, vg_ref)

    row = lambda i: (i, 0)
    full = lambda i: (0, 0)
    halo = _halo_specs(tm, S)
    return pl.pallas_call(
        body, name="pool_bwd", grid=(S // tm,),
        in_specs=halo + halo + [pl.BlockSpec((tm, D), row), pl.BlockSpec((tm, D), row), pl.BlockSpec((8, D), full),
                                pl.BlockSpec((4, G, G), lambda i: (0, 0, 0)), pl.BlockSpec((8, D), full)],
        out_specs=[pl.BlockSpec((tm, D), row), pl.BlockSpec((8, D), full), pl.BlockSpec((8, D), full),
                   pl.BlockSpec((N_CHIP, 4, R, G), lambda i: (0, 0, 0, 0))],
        out_shape=[jax.ShapeDtypeStruct((S, D), F32), jax.ShapeDtypeStruct((8, D), F32),
                   jax.ShapeDtypeStruct((8, D), F32), jax.ShapeDtypeStruct((N_CHIP, 4, R, G), F32)],
        scratch_shapes=[pltpu.VMEM((tm, D), F32)],
        compiler_params=_params("arbitrary"),
    )(dout, dout, dout, y, y, y, x, z, vec, pw, pvec)


def _w3(shape):
    return pl.BlockSpec(shape, lambda i: (0,) * len(shape))


def mla_pre(x, vec, mw, tabs):
    S = x.shape[0]
    tm = min(256, S)

    def body(x_ref, vec_ref, cos_ref, sin_ref, wq_ref, wkv_ref, wkr_ref, wkrs_ref, qn_ref, kvn_ref,
             wn_ref, wr_ref, wrs_ref, wuk_ref,
             h_ref, cq_ref, ckv_ref, cqn_ref, qnope_ref, qcat_ref, kcat_ref, vcat_ref):
        h, _, _ = _prenorm(x_ref[...], vec_ref)
        hb = h.astype(BF16)
        h_ref[...] = hb
        cq_raw = _dot(hb, wq_ref[...])
        ckv_raw = _dot(hb, wkv_ref[...])
        cq_ref[...] = cq_raw
        ckv_ref[...] = ckv_raw
        cos, sin = cos_ref[...], sin_ref[...]
        k_rope = _dot(hb, wkr_ref[...]) * cos + _dot(hb, wkrs_ref[...]) * sin
        ckv = _rms(ckv_raw)[0] * kvn_ref[...]
        kcat_ref[:, 0:KVL] = ckv.astype(BF16)
        vcat_ref[:, 0:KVL] = ckv.astype(BF16)
        ones = lax.broadcasted_iota(jnp.int32, (tm, QPAD - KVL), 1) == 0
        vcat_ref[:, KVL:] = jnp.where(ones, 1.0, 0.0).astype(BF16)
        kcat_ref[:, KVL:KVL + ROPE] = k_rope.astype(BF16)
        kcat_ref[:, KVL + ROPE:] = jnp.zeros((tm, QPAD - KVL - ROPE), BF16)
        cqb = (_rms(cq_raw)[0] * qn_ref[...]).astype(BF16)
        cqn_ref[...] = cqb
        for hd in range(N_HEADS):
            qn = _dot(cqb, wn_ref[hd]).astype(BF16)
            qnope_ref[hd] = qn
            qcat_ref[hd, :, 0:KVL] = (_dot(qn, wuk_ref[hd], NT) * ATTN_SCALE).astype(BF16)
            qr = (_dot(cqb, wr_ref[hd]) * cos + _dot(cqb, wrs_ref[hd]) * sin) * ATTN_SCALE
            qcat_ref[hd, :, KVL:KVL + ROPE] = qr.astype(BF16)
            qcat_ref[hd, :, KVL + ROPE:] = jnp.zeros((tm, QPAD - KVL - ROPE), BF16)

    row = lambda i: (i, 0)
    hrow = lambda i: (0, i, 0)
    return pl.pallas_call(
        body, name="mla_pre", grid=(S // tm,),
        in_specs=[pl.BlockSpec((tm, D), row), _w3((8, D)), pl.BlockSpec((tm, ROPE), row), pl.BlockSpec((tm, ROPE), row),
                  _w3((D, QL)), _w3((D, KVL)), _w3((D, ROPE)), _w3((D, ROPE)), _w3((1, QL)), _w3((1, KVL)),
                  _w3((N_HEADS, QL, NOPE)), _w3((N_HEADS, QL, ROPE)), _w3((N_HEADS, QL, ROPE)),
                  _w3((N_HEADS, KVL, NOPE))],
        out_specs=[pl.BlockSpec((tm, D), row), pl.BlockSpec((tm, QL), row), pl.BlockSpec((tm, KVL), row),
                   pl.BlockSpec((tm, QL), row), pl.BlockSpec((N_HEADS, tm, NOPE), hrow),
                   pl.BlockSpec((N_HEADS, tm, QPAD), hrow), pl.BlockSpec((tm, QPAD), row),
                   pl.BlockSpec((tm, QPAD), row)],
        out_shape=[jax.ShapeDtypeStruct((S, D), BF16), jax.ShapeDtypeStruct((S, QL), F32),
                   jax.ShapeDtypeStruct((S, KVL), F32), jax.ShapeDtypeStruct((S, QL), BF16),
                   jax.ShapeDtypeStruct((N_HEADS, S, NOPE), BF16), jax.ShapeDtypeStruct((N_HEADS, S, QPAD), BF16),
                   jax.ShapeDtypeStruct((S, QPAD), BF16), jax.ShapeDtypeStruct((S, QPAD), BF16)],
        compiler_params=_params("parallel"),
    )(x, vec, tabs[0], tabs[1], mw["wq"], mw["wkv"], mw["wkr"], mw["wkrs"], mw["qn"], mw["kvn"],
      mw["wn"], mw["wr"], mw["wrs"], mw["wuk"])


def attn_fwd(qcat, kcat, vcat):
    S = kcat.shape[0]
    tq = min(ATTN_TQ, S)
    kc = min(ATTN_KC, S)

    def body(q_ref, k_ref, v_ref, o_ref, lse_ref):
        q = q_ref[...]
        m = jnp.full((tq, 1), -jnp.inf, F32)
        ov = jnp.zeros((tq, QPAD), F32)
        for c in range(S // kc):
            s = _dot(q, k_ref[c * kc:(c + 1) * kc, :], NT)
            m_new = jnp.maximum(m, jnp.max(s, axis=-1, keepdims=True))
            p = jnp.exp(s - m_new).astype(BF16)
            ov = ov * jnp.exp(m - m_new) + _dot(p, v_ref[c * kc:(c + 1) * kc, :])
            m = m_new
        l = ov[:, KVL:KVL + 1]
        o_ref[...] = (ov[:, 0:KVL] * (1.0 / l)).astype(BF16)
        lse_ref[...] = m + jnp.log(l)

    return pl.pallas_call(
        body, name="attn_fwd", grid=(N_HEADS, S // tq),
        in_specs=[pl.BlockSpec((None, tq, QPAD), lambda h, i: (h, i, 0)),
                  pl.BlockSpec((S, QPAD), lambda h, i: (0, 0)),
                  pl.BlockSpec((S, QPAD), lambda h, i: (0, 0))],
        out_specs=[pl.BlockSpec((None, tq, KVL), lambda h, i: (h, i, 0)),
                   pl.BlockSpec((None, tq, 1), lambda h, i: (h, i, 0))],
        out_shape=[jax.ShapeDtypeStruct((N_HEADS, S, KVL), BF16), jax.ShapeDtypeStruct((N_HEADS, S, 1), F32)],
        compiler_params=_params("parallel", "parallel"),
    )(qcat, kcat, vcat)


def mla_post(olat, x, vec, wuv, wo):
    S = x.shape[0]
    tm = min(256, S)

    def body(o_ref, x_ref, vec_ref, wuv_ref, wo_ref, xo_ref, u_ref, ocat_ref):
        u = jnp.zeros((tm, D), F32)
        for hd in range(N_HEADS):
            oc = _dot(o_ref[hd], wuv_ref[hd]).astype(BF16)
            ocat_ref[hd] = oc
            u = u + _dot(oc, wo_ref[hd])
        u_ref[...] = u
        uhat, _ = _rms(u)
        xo_ref[...] = x_ref[...] + (1.0 + vec_ref[4:5, :]) * (uhat * vec_ref[1:2, :])

    row = lambda i: (i, 0)
    hrow = lambda i: (0, i, 0)
    return pl.pallas_call(
        body, name="mla_post", grid=(S // tm,),
        in_specs=[pl.BlockSpec((N_HEADS, tm, KVL), hrow), pl.BlockSpec((tm, D), row), _w3((8, D)),
                  _w3((N_HEADS, KVL, VH)), _w3((N_HEADS, VH, D))],
        out_specs=[pl.BlockSpec((tm, D), row), pl.BlockSpec((tm, D), row), pl.BlockSpec((N_HEADS, tm, VH), hrow)],
        out_shape=[jax.ShapeDtypeStruct((S, D), F32), jax.ShapeDtypeStruct((S, D), F32),
                   jax.ShapeDtypeStruct((N_HEADS, S, VH), BF16)],
        compiler_params=_params("parallel"),
    )(olat, x, vec, wuv, wo)


def mla_post_bwd(dout, u, olat, vec, wuv, wo):
    S = u.shape[0]
    tm = min(256, S)

    def body(do_ref, u_ref, o_ref, vec_ref, wuv_ref, wo_ref, du_ref, docat_ref, dolat_ref, delta_ref, vg_ref):
        @pl.when(pl.program_id(0) == 0)
        def _():
            vg_ref[...] = jnp.zeros_like(vg_ref)

        du, dgate_rows, dgpost_rows = _postnorm_bwd(do_ref[...], u_ref[...], vec_ref, 1.0)
        vg_ref[2:3, :] += jnp.sum(dgate_rows, axis=0, keepdims=True)
        vg_ref[4:5, :] += jnp.sum(dgpost_rows, axis=0, keepdims=True)
        dub = du.astype(BF16)
        du_ref[...] = dub
        for hd in range(N_HEADS):
            doc = _dot(dub, wo_ref[hd], NT).astype(BF16)
            docat_ref[hd] = doc
            dol = _dot(doc, wuv_ref[hd], NT).astype(BF16)
            dolat_ref[hd] = dol
            delta_ref[hd] = jnp.sum(dol.astype(F32) * o_ref[hd].astype(F32), axis=-1, keepdims=True)

    row = lambda i: (i, 0)
    hrow = lambda i: (0, i, 0)
    return pl.pallas_call(
        body, name="mla_post_bwd", grid=(S // tm,),
        in_specs=[pl.BlockSpec((tm, D), row), pl.BlockSpec((tm, D), row), pl.BlockSpec((N_HEADS, tm, KVL), hrow),
                  _w3((8, D)), _w3((N_HEADS, KVL, VH)), _w3((N_HEADS, VH, D))],
        out_specs=[pl.BlockSpec((tm, D), row), pl.BlockSpec((N_HEADS, tm, VH), hrow),
                   pl.BlockSpec((N_HEADS, tm, KVL), hrow), pl.BlockSpec((N_HEADS, tm, 1), hrow), _w3((8, D))],
        out_shape=[jax.ShapeDtypeStruct((S, D), BF16), jax.ShapeDtypeStruct((N_HEADS, S, VH), BF16),
                   jax.ShapeDtypeStruct((N_HEADS, S, KVL), BF16), jax.ShapeDtypeStruct((N_HEADS, S, 1), F32),
                   jax.ShapeDtypeStruct((8, D), F32)],
        compiler_params=_params("arbitrary"),
    )(dout, u, olat, vec, wuv, wo)


def attn_bwd(qcat, kcat, kcat_t, dolat, lse_row, delta_row):
    S = kcat.shape[0]
    tq = min(ATTN_TQ, S)
    kc = min(ATTN_KC, S)

    def body(q_ref, k_ref, kt_ref, do_ref, lse_ref, dl_ref, dq_ref, dk_ref, dv_ref):
        @pl.when((pl.program_id(0) == 0) & (pl.program_id(1) == 0))
        def _():
            dk_ref[...] = jnp.zeros_like(dk_ref)
            dv_ref[...] = jnp.zeros_like(dv_ref)

        q, do = q_ref[...], do_ref[...]
        lse, dl = lse_ref[...], dl_ref[...]
        dqt = jnp.zeros((QPAD, tq), F32)
        for c in range(S // kc):
            rows = slice(c * kc, (c + 1) * kc)
            st = _dot(k_ref[rows, :], q, NT)
            pt = jnp.exp(st - lse)
            dpt = _dot(k_ref[rows, 0:KVL], do, NT)
            dst = (pt * (dpt - dl)).astype(BF16)
            dv_ref[rows, :] += _dot(pt.astype(BF16), do)
            dk_ref[rows, :] += _dot(dst, q)
            dqt = dqt + _dot(kt_ref[:, rows], dst)
        dq_ref[...] = dqt.T

    return pl.pallas_call(
        body, name="attn_bwd", grid=(N_HEADS, S // tq),
        in_specs=[pl.BlockSpec((None, tq, QPAD), lambda h, i: (h, i, 0)),
                  pl.BlockSpec((S, QPAD), lambda h, i: (0, 0)),
                  pl.BlockSpec((QPAD, S), lambda h, i: (0, 0)),
                  pl.BlockSpec((None, tq, KVL), lambda h, i: (h, i, 0)),
                  pl.BlockSpec((None, 1, tq), lambda h, i: (h, 0, i)),
                  pl.BlockSpec((None, 1, tq), lambda h, i: (h, 0, i))],
        out_specs=[pl.BlockSpec((None, tq, QPAD), lambda h, i: (h, i, 0)),
                   pl.BlockSpec((S, QPAD), lambda h, i: (0, 0)),
                   pl.BlockSpec((S, KVL), lambda h, i: (0, 0))],
        out_shape=[jax.ShapeDtypeStruct((N_HEADS, S, QPAD), F32), jax.ShapeDtypeStruct((S, QPAD), F32),
                   jax.ShapeDtypeStruct((S, KVL), F32)],
        compiler_params=_params("arbitrary", "arbitrary"),
    )(qcat, kcat, kcat_t, dolat, lse_row, delta_row)


def mla_pre_bwd(dout, dq, dk, dv, x, cq_raw, ckv_raw, vec, mw, tabs):
    S = x.shape[0]
    tm = min(256, S)

    def body(do_ref, dq_ref, dk_ref, dv_ref, x_ref, cq_ref, ckv_ref, vec_ref, cos_ref, sin_ref,
             wq_ref, wkv_ref, wkr_ref, wkrs_ref, qn_ref, kvn_ref, wn_ref, wr_ref, wrs_ref, wuk_ref,
             dx_ref, dlat_ref, dka_ref, dkb_ref, dqn_ref, dql_ref, dqa_ref, dqb_ref, vg_ref, ng_ref):
        @pl.when(pl.program_id(0) == 0)
        def _():
            vg_ref[...] = jnp.zeros_like(vg_ref)
            ng_ref[...] = jnp.zeros_like(ng_ref)

        cos, sin = cos_ref[...], sin_ref[...]
        dcq = jnp.zeros((tm, QL), F32)
        for hd in range(N_HEADS):
            dql = (dq_ref[hd, :, 0:KVL] * ATTN_SCALE).astype(BF16)
            dql_ref[hd] = dql
            dqn = _dot(dql, wuk_ref[hd]).astype(BF16)
            dqn_ref[hd] = dqn
            dqr = dq_ref[hd, :, KVL:KVL + ROPE] * ATTN_SCALE
            qa = (dqr * cos).astype(BF16)
            qb = (dqr * sin).astype(BF16)
            dqa_ref[hd] = qa
            dqb_ref[hd] = qb
            dcq = dcq + _dot(dqn, wn_ref[hd], NT) + _dot(qa, wr_ref[hd], NT) + _dot(qb, wrs_ref[hd], NT)
        cqh, rq = _rms(cq_ref[...])
        ng_ref[0:1, :] += jnp.sum(dcq * cqh, axis=0, keepdims=True)
        dcq_raw = _rms_bwd(cqh, rq, dcq * qn_ref[...]).astype(BF16)
        dckv = dk_ref[:, 0:KVL] + dv_ref[...]
        ckvh, rk = _rms(ckv_ref[...])
        ng_ref[1:2, 0:KVL] += jnp.sum(dckv * ckvh, axis=0, keepdims=True)
        dckv_raw = _rms_bwd(ckvh, rk, dckv * kvn_ref[...]).astype(BF16)
        dkr = dk_ref[:, KVL:KVL + ROPE]
        ka = (dkr * cos).astype(BF16)
        kb = (dkr * sin).astype(BF16)
        dlat_ref[:, 0:QL] = dcq_raw
        dlat_ref[:, QL:QL + KVL] = dckv_raw
        dka_ref[...] = ka
        dkb_ref[...] = kb
        dh = (_dot(dcq_raw, wq_ref[...], NT) + _dot(dckv_raw, wkv_ref[...], NT)
              + _dot(ka, wkr_ref[...], NT) + _dot(kb, wkrs_ref[...], NT))
        dx_ref[...] = do_ref[...] + _prenorm_bwd(dh, x_ref[...], vec_ref, vg_ref)

    row = lambda i: (i, 0)
    hrow = lambda i: (0, i, 0)
    return pl.pallas_call(
        body, name="mla_pre_bwd", grid=(S // tm,),
        in_specs=[pl.BlockSpec((tm, D), row), pl.BlockSpec((N_HEADS, tm, QPAD), hrow), pl.BlockSpec((tm, QPAD), row),
                  pl.BlockSpec((tm, KVL), row), pl.BlockSpec((tm, D), row), pl.BlockSpec((tm, QL), row),
                  pl.BlockSpec((tm, KVL), row), _w3((8, D)), pl.BlockSpec((tm, ROPE), row), pl.BlockSpec((tm, ROPE), row),
                  _w3((D, QL)), _w3((D, KVL)), _w3((D, ROPE)), _w3((D, ROPE)), _w3((1, QL)), _w3((1, KVL)),
                  _w3((N_HEADS, QL, NOPE)), _w3((N_HEADS, QL, ROPE)), _w3((N_HEADS, QL, ROPE)),
                  _w3((N_HEADS, KVL, NOPE))],
        out_specs=[pl.BlockSpec((tm, D), row), pl.BlockSpec((tm, QL + KVL), row), pl.BlockSpec((tm, ROPE), row),
                   pl.BlockSpec((tm, ROPE), row), pl.BlockSpec((N_HEADS, tm, NOPE), hrow),
                   pl.BlockSpec((N_HEADS, tm, KVL), hrow), pl.BlockSpec((N_HEADS, tm, ROPE), hrow),
                   pl.BlockSpec((N_HEADS, tm, ROPE), hrow), _w3((8, D)), _w3((8, QL))],
        out_shape=[jax.ShapeDtypeStruct((S, D), F32), jax.ShapeDtypeStruct((S, QL + KVL), BF16),
                   jax.ShapeDtypeStruct((S, ROPE), BF16), jax.ShapeDtypeStruct((S, ROPE), BF16),
                   jax.ShapeDtypeStruct((N_HEADS, S, NOPE), BF16), jax.ShapeDtypeStruct((N_HEADS, S, KVL), BF16),
                   jax.ShapeDtypeStruct((N_HEADS, S, ROPE), BF16), jax.ShapeDtypeStruct((N_HEADS, S, ROPE), BF16),
                   jax.ShapeDtypeStruct((8, D), F32), jax.ShapeDtypeStruct((8, QL), F32)],
        compiler_params=_params("arbitrary"),
    )(dout, dq, dk, dv, x, cq_raw, ckv_raw, vec, tabs[0], tabs[1], mw["wq"], mw["wkv"], mw["wkr"], mw["wkrs"],
      mw["qn"], mw["kvn"], mw["wn"], mw["wr"], mw["wrs"], mw["wuk"])


def mla_dw(h, dlat, dka, dkb, cqn, dqn, dqa, dqb, dql, qnope, olat, docat, ocat, du):
    S = h.shape[0]
    tk = min(512, S)
    nk = S // tk
    flat_a = lambda w: pl.BlockSpec((tk, w), lambda k: (k, 0))
    head_a = lambda w: pl.BlockSpec((None, tk, w), lambda n, k: (n, k, 0))
    shared = lambda w: pl.BlockSpec((tk, w), lambda n, k: (k, 0))
    head_o = lambda r, c: pl.BlockSpec((None, r, c), lambda n, k: (n, 0, 0))
    g = {}
    g["in"] = dw_matmul("mla_dw_in", h, dlat, flat_a(D), flat_a(QL + KVL), (D, QL + KVL),
                        pl.BlockSpec((D, QL + KVL), lambda k: (0, 0)), (nk,))
    g["kr"] = dw_matmul("mla_dw_kr", h, dka, flat_a(D), flat_a(ROPE), (D, ROPE),
                        pl.BlockSpec((D, ROPE), lambda k: (0, 0)), (nk,))
    g["krs"] = dw_matmul("mla_dw_krs", h, dkb, flat_a(D), flat_a(ROPE), (D, ROPE),
                         pl.BlockSpec((D, ROPE), lambda k: (0, 0)), (nk,))
    g["n"] = dw_matmul("mla_dw_n", cqn, dqn, shared(QL), head_a(NOPE), (N_HEADS, QL, NOPE), head_o(QL, NOPE),
                       (N_HEADS, nk))
    g["r"] = dw_matmul("mla_dw_r", cqn, dqa, shared(QL), head_a(ROPE), (N_HEADS, QL, ROPE), head_o(QL, ROPE),
                       (N_HEADS, nk))
    g["rs"] = dw_matmul("mla_dw_rs", cqn, dqb, shared(QL), head_a(ROPE), (N_HEADS, QL, ROPE), head_o(QL, ROPE),
                        (N_HEADS, nk))
    g["uk"] = dw_matmul("mla_dw_uk", dql, qnope, head_a(KVL), head_a(NOPE), (N_HEADS, KVL, NOPE), head_o(KVL, NOPE),
                        (N_HEADS, nk))
    g["uv"] = dw_matmul("mla_dw_uv", olat, docat, head_a(KVL), head_a(VH), (N_HEADS, KVL, VH), head_o(KVL, VH),
                        (N_HEADS, nk))
    g["o"] = dw_matmul("mla_dw_o", ocat, du, head_a(VH), shared(D), (N_HEADS, VH, D), head_o(VH, D),
                       (N_HEADS, nk))
    return g


def loss_head(y, target):
    S = y.shape[0]
    tm = min(512, S)

    def body(y_ref, t_ref, loss_ref, dy_ref):
        @pl.when(pl.program_id(0) == 0)
        def _():
            loss_ref[...] = jnp.zeros_like(loss_ref)

        err = y_ref[...] - t_ref[...]
        dy_ref[...] = err * (1.0 / D)
        loss_ref[...] += 0.5 * jnp.sum(jnp.mean(err * err, axis=-1, keepdims=True), axis=0, keepdims=True)

    row = lambda i: (i, 0)
    return pl.pallas_call(
        body, name="loss_head", grid=(S // tm,),
        in_specs=[pl.BlockSpec((tm, D), row), pl.BlockSpec((tm, D), row)],
        out_specs=[pl.BlockSpec((1, 1), lambda i: (0, 0)), pl.BlockSpec((tm, D), row)],
        out_shape=[jax.ShapeDtypeStruct((1, 1), F32), jax.ShapeDtypeStruct((S, D), F32)],
        compiler_params=_params("arbitrary"),
    )(y, target)


MOD_COLS = 9 * D // N_CHIP


def mod_fwd(c_pad, ada_w, ada_b_loc):
    tn = MOD_COLS // 3

    def body(c_ref, w_ref, b_ref, o_ref):
        c = c_ref[...]
        sc = (c * jax.nn.sigmoid(c)).astype(BF16)
        o_ref[...] = _dot(sc, w_ref[...].astype(BF16)) + b_ref[...]

    return pl.pallas_call(
        body, name="mod_fwd", grid=(2, 3),
        in_specs=[pl.BlockSpec((16, D), lambda i, n: (0, 0)), pl.BlockSpec((None, D, tn), lambda i, n: (i, 0, n)),
                  pl.BlockSpec((None, 1, tn), lambda i, n: (i, 0, n))],
        out_specs=pl.BlockSpec((None, 16, tn), lambda i, n: (i, 0, n)),
        out_shape=jax.ShapeDtypeStruct((2, 16, MOD_COLS), F32),
        compiler_params=_params("parallel", "parallel"),
    )(c_pad, ada_w, ada_b_loc)


def _adamw_math(w, g, m, v):
    m = ADAM_B1 * m + (1.0 - ADAM_B1) * g
    v = ADAM_B2 * v + (1.0 - ADAM_B2) * (g * g)
    m_hat = m / (1.0 - ADAM_B1 ** ADAM_STEP)
    v_hat = v / (1.0 - ADAM_B2 ** ADAM_STEP)
    delta = -ADAM_LR * (m_hat / (jnp.sqrt(v_hat) + ADAM_EPS) + ADAM_WD * w)
    return delta, m, v


def adamw(name, w, g, m, v):
    shape = w.shape
    cols = shape[-1]
    rows = w.size // cols
    tr = rows
    for cand in (512, 256, 128, 64, 32, 16, 8):
        if rows % cand == 0 and cand * cols * 4 <= (2 << 20):
            tr = cand
            break

    def body(w_ref, g_ref, m_ref, v_ref, d_ref, mo_ref, vo_ref):
        d_ref[...], mo_ref[...], vo_ref[...] = _adamw_math(w_ref[...], g_ref[...], m_ref[...], v_ref[...])

    spec = pl.BlockSpec((tr, cols), lambda i: (i, 0))
    outs = pl.pallas_call(
        body, name=name, grid=(rows // tr,), in_specs=[spec] * 4, out_specs=[spec] * 3,
        out_shape=[jax.ShapeDtypeStruct((rows, cols), F32)] * 3,
        compiler_params=_params("parallel"),
    )(*[a.reshape(rows, cols) for a in (w, g, m, v)])
    return [o.reshape(shape) for o in outs]


def adamw_ada(c_pad, dmod, w, m, v):
    tr = 256

    def body(c_ref, dm_ref, w_ref, m_ref, v_ref, g_ref, d_ref, mo_ref, vo_ref):
        c = c_ref[...]
        sc = (c * jax.nn.sigmoid(c)).astype(BF16)
        g = _dot(sc, dm_ref[...].astype(BF16), TN)
        g_ref[...] = g
        d_ref[...], mo_ref[...], vo_ref[...] = _adamw_math(w_ref[...], g, m_ref[...], v_ref[...])

    wspec = pl.BlockSpec((None, tr, MOD_COLS), lambda i, r: (i, r, 0))
    return pl.pallas_call(
        body, name="adamw_ada", grid=(2, D // tr),
        in_specs=[pl.BlockSpec((16, tr), lambda i, r: (0, r)),
                  pl.BlockSpec((None, 16, MOD_COLS), lambda i, r: (i, 0, 0)), wspec, wspec, wspec],
        out_specs=[wspec] * 4,
        out_shape=[jax.ShapeDtypeStruct((2, D, MOD_COLS), F32)] * 4,
        compiler_params=_params("parallel", "parallel"),
    )(c_pad, dmod, w, m, v)


def sum_devices(name, a):
    _, R, C = a.shape
    tr = R
    for cand in (64, 32, 16, 8):
        if R % cand == 0:
            tr = cand
            break

    def body(a_ref, o_ref):
        acc = a_ref[0]
        for dev in range(1, N_DEV):
            acc = acc + a_ref[dev]
        o_ref[...] = acc

    return pl.pallas_call(
        body, name=name, grid=(R // tr,),
        in_specs=[pl.BlockSpec((N_DEV, tr, C), lambda i: (0, i, 0))],
        out_specs=pl.BlockSpec((tr, C), lambda i: (i, 0)),
        out_shape=jax.ShapeDtypeStruct((R, C), F32),
        compiler_params=_params("parallel"),
    )(a)


def _place():
    return lax.axis_index("x"), lax.axis_index("y"), lax.axis_index("c")


def _other_chips(x, y):
    return [(1 - x, y), (x, 1 - y), (1 - x, 1 - y)]


def gather_devices(name, a):
    m_per, n = a.shape

    def body(x_ref, out_ref, send_sems, recv_sems, local_sem):
        x, y, c = _place()
        me, sibling = (x, y, c), (x, y, 1 - c)
        chips = _other_chips(x, y)

        def rows(px, py, pc):
            return out_ref.at[pl.ds((4 * px + 2 * py + pc) * m_per, m_per), :]

        def copy(k, block, to, src=None):
            return pltpu.make_async_remote_copy(
                src_ref=rows(*block) if src is None else src, dst_ref=rows(*block),
                send_sem=send_sems.at[k], recv_sem=recv_sems.at[k], device_id=to, device_id_type=MESH)

        mine = pltpu.make_async_copy(x_ref, rows(*me), local_sem)
        mine.start()
        first = [copy(0, me, sibling, src=x_ref)]
        first += [copy(1 + j, me, (*chip, c), src=x_ref) for j, chip in enumerate(chips)]
        for cp in first:
            cp.start()
        passed = [copy(4 + j, (*chip, c), sibling) for j, chip in enumerate(chips)]
        for j, chip in enumerate(chips):
            copy(1 + j, (*chip, c), me).wait_recv()
            passed[j].start()
        copy(0, sibling, me).wait_recv()
        for j, chip in enumerate(chips):
            copy(4 + j, (*chip, 1 - c), me).wait_recv()
        for cp in first + passed:
            cp.wait_send()
        mine.wait()

    out = pl.pallas_call(
        body, name=name,
        out_shape=jax.ShapeDtypeStruct((N_DEV * m_per, n), a.dtype),
        in_specs=[pl.BlockSpec(memory_space=pltpu.VMEM)],
        out_specs=pl.BlockSpec(memory_space=pltpu.VMEM),
        scratch_shapes=[pltpu.SemaphoreType.DMA((7,)), pltpu.SemaphoreType.DMA((7,)), pltpu.SemaphoreType.DMA],
        compiler_params=pltpu.CompilerParams(vmem_limit_bytes=VMEM_LIMIT),
    )(a)
    return out.reshape(N_DEV, m_per, n)


_ANY = pl.BlockSpec(memory_space=pl.ANY)


def _hbm_ref(a):
    return jax.new_ref(a, memory_space=pltpu.MemorySpace.HBM)


def _hbm_empty(shape, dtype):
    return jax.empty_ref(jax.ShapeDtypeStruct(shape, dtype), memory_space=pltpu.MemorySpace.HBM)


ID_PAIR, ID_CHIPS, ID_SHARE, ID_UKV = 8, 9, 10, 11


def _sequencer(name, collective_id, n_sem, peers_of, program):
    sems = pltpu.SemaphoreType.DMA((n_sem,))

    @pl.kernel(mesh=plsc.ScalarSubcoreMesh(axis_name="seq", num_cores=1), name=name, scratch_types=[sems, sems],
               compiler_params=pltpu.CompilerParams(collective_id=collective_id))
    def launch(send_sem, recv_sem):
        x, y, c = _place()
        peers = peers_of(x, y, c)
        barrier = pltpu.get_barrier_semaphore()
        for peer in peers:
            pl.semaphore_signal(barrier, inc=1, device_id=peer, device_id_type=MESH)
        pl.semaphore_wait(barrier, len(peers))
        program(x, y, c, send_sem, recv_sem)

    launch()


def gather_weights(name, stage, arrays):
    n = len(arrays)
    refs = [_hbm_ref(a) for a in arrays]

    def program(x, y, c, send_sem, recv_sem):
        me = 2 * x + y
        chips = _other_chips(x, y)

        def ici(t, r, half):
            cx, cy = chips[r]
            mine = refs[t].at[me, half]
            return pltpu.make_async_remote_copy(
                src_ref=mine, dst_ref=mine, send_sem=send_sem.at[3 * t + r], recv_sem=recv_sem.at[3 * t + r],
                device_id=(cx, cy, c), device_id_type=MESH)

        def d2d(t, r, half):
            cx, cy = chips[r]
            there = refs[t].at[2 * cx + cy, half]
            k = 3 * n + 3 * t + r
            return pltpu.make_async_remote_copy(
                src_ref=there, dst_ref=there, send_sem=send_sem.at[k], recv_sem=recv_sem.at[k],
                device_id=(x, y, 1 - c), device_id_type=MESH)

        for t in range(n):
            for r in range(3):
                ici(t, r, c).start()
        for t in range(n):
            for r in range(3):
                ici(t, r, c).wait_recv()
                d2d(t, r, c).start()
        for t in range(n):
            for r in range(3):
                d2d(t, r, 1 - c).wait_recv()
        for t in range(n):
            for r in range(3):
                ici(t, r, c).wait_send()
                d2d(t, r, c).wait_send()

    _sequencer(name, stage, 6 * n, lambda x, y, c: [(x, y, 1 - c)] + [(cx, cy, c) for cx, cy in _other_chips(x, y)],
               program)
    return [r[...] for r in refs]


def cast_into_slots(chip, shards):
    steps = 2

    def body(chip_ref, *refs):
        n = len(refs) // 2
        for src, dst in zip(refs[:n], refs[n:]):
            dst[...] = src[...].astype(BF16)

    def spec_in(s):
        return pl.BlockSpec((None, s.shape[1] // steps, s.shape[2]), lambda h, i, chip_ref: (h, i, 0))

    def spec_out(s):
        return pl.BlockSpec((None, None, s.shape[1] // steps, s.shape[2]), lambda h, i, chip_ref: (chip_ref[0], h, i, 0))

    return pl.pallas_call(
        body, name="cast_into_slots",
        grid_spec=pltpu.PrefetchScalarGridSpec(
            num_scalar_prefetch=1, grid=(2, steps),
            in_specs=[spec_in(s) for s in shards], out_specs=[spec_out(s) for s in shards]),
        out_shape=[jax.ShapeDtypeStruct((N_CHIP,) + s.shape, BF16) for s in shards],
        compiler_params=_params("parallel", "parallel"),
    )(chip, *shards)


def reduce_pair(name, grads):
    n = len(grads)
    src = [_hbm_ref(g) for g in grads]
    dst = [_hbm_empty((N_CHIP,) + g.shape[2:], g.dtype) for g in grads]

    def program(x, y, c, send_sem, recv_sem):
        cps = [pltpu.make_async_remote_copy(
            src_ref=src[t].at[:, 1 - c], dst_ref=dst[t], send_sem=send_sem.at[t], recv_sem=recv_sem.at[t],
            device_id=(x, y, 1 - c), device_id_type=MESH) for t in range(n)]
        for cp in cps:
            cp.start()
        for cp in cps:
            cp.wait()

    _sequencer(name, ID_PAIR, n, lambda x, y, c: [(x, y, 1 - c)], program)
    return [r[...] for r in src], [r[...] for r in dst]


def pair_add(name, core, g, got):
    _, _, R, C = g.shape

    def body(core_ref, g_ref, got_ref, o_ref):
        o_ref[...] = (g_ref[...] + got_ref[...]).astype(BF16)

    return pl.pallas_call(
        body, name=name,
        grid_spec=pltpu.PrefetchScalarGridSpec(
            num_scalar_prefetch=1, grid=(N_CHIP,),
            in_specs=[pl.BlockSpec((None, None, R, C), lambda q, core_ref: (q, core_ref[0], 0, 0)),
                      pl.BlockSpec((None, R, C), lambda q, core_ref: (q, 0, 0))],
            out_specs=pl.BlockSpec((None, R, C), lambda q, core_ref: (q, 0, 0))),
        out_shape=jax.ShapeDtypeStruct((N_CHIP, R, C), BF16),
        compiler_params=_params("parallel"),
    )(core, g, got)


def reduce_chips(name, sums):
    n = len(sums)
    src = [_hbm_ref(s) for s in sums]
    dst = [_hbm_empty((3,) + s.shape[1:], s.dtype) for s in sums]

    def program(x, y, c, send_sem, recv_sem):
        cps = []
        for t in range(n):
            for r, (cx, cy) in enumerate(_other_chips(x, y)):
                cps.append(pltpu.make_async_remote_copy(
                    src_ref=src[t].at[2 * cx + cy], dst_ref=dst[t].at[r],
                    send_sem=send_sem.at[3 * t + r], recv_sem=recv_sem.at[3 * t + r],
                    device_id=(cx, cy, c), device_id_type=MESH))
        for cp in cps:
            cp.start()
        for cp in cps:
            cp.wait()

    _sequencer(name, ID_CHIPS, 3 * n, lambda x, y, c: [(cx, cy, c) for cx, cy in _other_chips(x, y)], program)
    return [r[...] for r in src], [r[...] for r in dst]


def chip_add(name, place, s, got, k, n_slots, prev=None):
    _, R, C = s.shape

    def body(place_ref, s_ref, got_ref, *rest):
        o_ref = rest[-1]
        o_ref[...] = ((s_ref[...].astype(F32) + got_ref[0].astype(F32)) + got_ref[1].astype(F32)) + got_ref[2].astype(F32)

    in_specs = [pl.BlockSpec((None, R, C), lambda i, place_ref: (place_ref[0], 0, 0)),
                pl.BlockSpec((3, R, C), lambda i, place_ref: (0, 0, 0))]
    args = [place, s, got]
    aliases = {}
    if prev is not None:
        in_specs.append(_ANY)
        args.append(prev)
        aliases = {3: 0}
    return pl.pallas_call(
        body, name=name,
        grid_spec=pltpu.PrefetchScalarGridSpec(
            num_scalar_prefetch=1, grid=(1,), in_specs=in_specs,
            out_specs=pl.BlockSpec((None, None, R, C), lambda i, place_ref: (k, place_ref[1], 0, 0))),
        out_shape=jax.ShapeDtypeStruct((n_slots, 2, R, C), F32),
        input_output_aliases=aliases,
        compiler_params=_params("arbitrary"),
    )(*args)


def share_halves(name, stacks, slots):
    n = len(stacks)
    dst = [_hbm_ref(s) for s in stacks]

    def program(x, y, c, send_sem, recv_sem):
        cps = [pltpu.make_async_remote_copy(
            src_ref=dst[t].at[slots[t], c], dst_ref=dst[t].at[slots[t], c],
            send_sem=send_sem.at[t], recv_sem=recv_sem.at[t],
            device_id=(x, y, 1 - c), device_id_type=MESH) for t in range(n)]
        for cp in cps:
            cp.start()
        for cp in cps:
            cp.wait()

    _sequencer(name, ID_SHARE, n, lambda x, y, c: [(x, y, 1 - c)], program)
    return [r[...] for r in dst]


def gather_blocks(name, slotted):
    out = _hbm_ref(slotted)

    def program(x, y, c, send_sem, recv_sem):
        sibling = (x, y, 1 - c)
        chips = _other_chips(x, y)

        def copy(k, px, py, pc, to):
            block = out.at[4 * px + 2 * py + pc]
            return pltpu.make_async_remote_copy(src_ref=block, dst_ref=block, send_sem=send_sem.at[k],
                                                recv_sem=recv_sem.at[k], device_id=to, device_id_type=MESH)

        first = [copy(0, x, y, c, sibling)] + [copy(1 + j, x, y, c, (cx, cy, c)) for j, (cx, cy) in enumerate(chips)]
        for cp in first:
            cp.start()
        passed = [copy(4 + j, cx, cy, c, sibling) for j, (cx, cy) in enumerate(chips)]
        for j, (cx, cy) in enumerate(chips):
            copy(1 + j, cx, cy, c, (x, y, c)).wait_recv()
            passed[j].start()
        copy(0, x, y, 1 - c, (x, y, c)).wait_recv()
        for j, (cx, cy) in enumerate(chips):
            copy(4 + j, cx, cy, 1 - c, (x, y, c)).wait_recv()
        for cp in first + passed:
            cp.wait_send()

    _sequencer(name, ID_UKV, 7, lambda x, y, c: [(x, y, 1 - c)] + [(cx, cy, c) for cx, cy in _other_chips(x, y)],
               program)
    return out[...]


def place_block(name, dev, a):
    M, N = a.shape
    tr = min(M, 64)

    def body(dev_ref, a_ref, o_ref):
        o_ref[...] = a_ref[...]

    return pl.pallas_call(
        body, name=name,
        grid_spec=pltpu.PrefetchScalarGridSpec(
            num_scalar_prefetch=1, grid=(M // tr,),
            in_specs=[pl.BlockSpec((tr, N), lambda i, dev_ref: (i, 0))],
            out_specs=pl.BlockSpec((None, tr, N), lambda i, dev_ref: (dev_ref[0], i, 0))),
        out_shape=jax.ShapeDtypeStruct((N_DEV, M, N), a.dtype),
        compiler_params=_params("parallel"),
    )(dev, a)


def _swap_rope(a):
    return jnp.concatenate([a[..., ROPE // 2:], a[..., :ROPE // 2]], axis=-1)


def _rope_tables(S):
    inv = 1.0 / (ROPE_THETA ** (jnp.arange(0, ROPE, 2, dtype=F32) / ROPE))
    ang = jnp.arange(S, dtype=F32)[:, None] * inv[None, :]
    cos, sin = jnp.cos(ang), jnp.sin(ang)
    return jnp.concatenate([cos, cos], axis=1), jnp.concatenate([-sin, sin], axis=1)


def _vec(norm_g, mod, i, k):
    rows = [norm_g[i, 2 * k], norm_g[i, 2 * k + 1], mod[i, 3 * k], mod[i, 3 * k + 1], mod[i, 3 * k + 2]]
    return jnp.concatenate([jnp.stack(rows), jnp.zeros((3, D), F32)], axis=0)


def _example_step(x, target, mod, norm_g, pvec, ffn_in, ffn_out, pw, mw, wuv, wo, reducer):
    S = x.shape[0]
    tabs = _rope_tables(S)
    vec = [[_vec(norm_g, mod, i, k) for k in range(3)] for i in range(2)]
    saved = {}
    for i in range(2):
        xin = x
        x, a, u, h = ffn_fwd(xin, vec[i][0], ffn_in[i][0], ffn_out[i][0], 0.5)
        saved[i, 0] = (xin, a, u, h)
        xin = x
        if i == 0:
            x, y, z = pool_fwd(xin, vec[i][1], pw, pvec)
            saved[i, 1] = (xin, y, z)
        else:
            h_m, cq_raw, ckv_raw, cqn, qnope, qcat, kcat, vcat = mla_pre(xin, vec[i][1], mw, tabs)
            olat, lse = attn_fwd(qcat, kcat, vcat)
            x, u_m, ocat = mla_post(olat, xin, vec[i][1], wuv, wo)
            saved[i, 1] = (xin, h_m, cq_raw, ckv_raw, cqn, qnope, qcat, kcat, olat, lse, u_m, ocat)
        xin = x
        x, a, u, h = ffn_fwd(xin, vec[i][2], ffn_in[i][1], ffn_out[i][1], 0.5)
        saved[i, 2] = (xin, a, u, h)
    loss, dx = loss_head(x, target)

    vg = {}
    G = D // 4

    def ffn_grads(i, k, dw_in, dw_out):
        return [(0, 2 * i + k, 4, dw_in.reshape(N_CHIP, 2, D // 2, FSH)),
                (1, 2 * i + k, 4, dw_out.reshape(N_CHIP, 2, DFF // 8, D))]

    for i in (1, 0):
        xin, a, u, h = saved[i, 2]
        dx, du, act, da, vg[i, 2] = ffn_bwd(dx, xin, u, a, vec[i][2], ffn_in[i][1], ffn_out[i][1], 0.5)
        reducer.advance()
        reducer.add(f"f{i}1", ffn_grads(i, 1, *ffn_dw(h, da, act, du)))
        if i == 0:
            xin, y, z = saved[i, 1]
            dx, vg[i, 1], pgrad, g_pool = pool_bwd(dx, xin, y, z, vec[i][1], pw, pvec)
            reducer.advance()
        else:
            xin, h_m, cq_raw, ckv_raw, cqn, qnope, qcat, kcat, olat, lse, u_m, ocat = saved[i, 1]
            du, docat, dolat, delta, vg_post = mla_post_bwd(dx, u_m, olat, vec[i][1], wuv, wo)
            reducer.advance()
            dq, dk, dv = attn_bwd(qcat, kcat, kcat.T, dolat, lse.reshape(N_HEADS, 1, S), delta.reshape(N_HEADS, 1, S))
            reducer.advance()
            dx, dlat, dka, dkb, dqn, dql, dqa, dqb, vg_pre, ngrad = mla_pre_bwd(
                dx, dq, dk, dv, xin, cq_raw, ckv_raw, vec[i][1], mw, tabs)
            vg[i, 1] = vg_post + vg_pre
            g = mla_dw(h_m, dlat, dka, dkb, cqn, dqn, dqa, dqb, dql, qnope, olat, docat, ocat, du)
            g_in = jnp.concatenate([g["in"], g["kr"] + _swap_rope(g["krs"])], axis=1)
            g_uq = jnp.concatenate([jnp.transpose(g["n"], (1, 0, 2)),
                                    jnp.transpose(g["r"] + _swap_rope(g["rs"]), (1, 0, 2))], axis=-1)
            reducer.add("mla", [(3, 0, 1, g_in.reshape(N_CHIP, 2, D // 8, QL + KVL + ROPE)),
                                (4, 0, 1, g_uq.reshape(N_CHIP, 2, QL // 8, N_HEADS * (NOPE + ROPE))),
                                (5, 0, 1, g["o"].reshape(N_CHIP, 2, D // 8, D))])
            reducer.add_replicated(jnp.concatenate(
                [jnp.transpose(g["uk"], (1, 0, 2)).reshape(KVL, N_HEADS * NOPE),
                 jnp.transpose(g["uv"], (1, 0, 2)).reshape(KVL, N_HEADS * VH)], axis=0))
        xin, a, u, h = saved[i, 0]
        dx, du, act, da, vg[i, 0] = ffn_bwd(dx, xin, u, a, vec[i][0], ffn_in[i][0], ffn_out[i][0], 0.5)
        reducer.advance()
        grads = ffn_grads(i, 0, *ffn_dw(h, da, act, du))
        if i == 0:
            grads.append((2, 0, 1, g_pool.reshape(N_CHIP, 2, 2 * G // N_CHIP, G)))
        reducer.add(f"f{i}0", grads)
    return loss, dx, vg, pgrad, ngrad


class _GradReducer:
    def __init__(self, core, place, dev):
        self.core, self.place, self.dev = core, place, dev
        self.stacks = {}
        self.live = []
        self.replicated = None

    def add(self, tag, items):
        gen = self._run(tag, items)
        next(gen)
        self.live.append(gen)

    def add_replicated(self, block):
        self.replicated = gather_blocks("gather_ukv", place_block("place_ukv", self.dev, block))

    def advance(self):
        live = []
        for gen in self.live:
            try:
                next(gen)
                live.append(gen)
            except StopIteration:
                pass
        self.live = live

    def finish(self):
        while self.live:
            self.advance()
        return self.stacks, self.replicated

    def _run(self, tag, items):
        grads, from_pair = reduce_pair(f"reduce_pair_{tag}", [g for *_, g in items])
        yield
        sums = [pair_add(f"pair_add_{tag}_{j}", self.core, g, p) for j, (g, p) in enumerate(zip(grads, from_pair))]
        sums, from_chips = reduce_chips(f"reduce_chips_{tag}", sums)
        yield
        for j, ((o, k, n_slots, _), s, p) in enumerate(zip(items, sums, from_chips)):
            self.stacks[o] = chip_add(f"chip_add_{tag}_{j}", self.place, s, p, k, n_slots, self.stacks.get(o))
        shared = share_halves(f"share_halves_{tag}", [self.stacks[o] for o, *_ in items], [k for _, k, *_ in items])
        for (o, *_), v in zip(items, shared):
            self.stacks[o] = v


SMALL_IN = 8 * 640
SMALL_GRAD = 8 * 4224
SMALL_W = 8 * 2944


def _pack(parts, total):
    flat = jnp.concatenate([p.reshape(-1) for p in parts])
    return jnp.concatenate([flat, jnp.zeros((total - flat.shape[0],), F32)]).reshape(8, total // 8)


def kernel(x, c, ada_w, ada_b, norm_g, ffn_w_in, ffn_w_out, pool_w, pool_b, pool_scale, mla_w_in, mla_q_norm, mla_kv_norm, mla_w_uq, mla_w_uk, mla_w_uv, mla_w_o, loss_target, m_ada_w, m_ada_b, m_norm_g, m_ffn_w_in, m_ffn_w_out, m_pool_w, m_pool_b, m_pool_scale, m_mla_w_in, m_mla_q_norm, m_mla_kv_norm, m_mla_w_uq, m_mla_w_uk, m_mla_w_uv, m_mla_w_o, v_ada_w, v_ada_b, v_norm_g, v_ffn_w_in, v_ffn_w_out, v_pool_w, v_pool_b, v_pool_scale, v_mla_w_in, v_mla_q_norm, v_mla_kv_norm, v_mla_w_uq, v_mla_w_uk, v_mla_w_uv, v_mla_w_o):
    ix, iy, ic = _place()
    chip = 2 * ix + iy
    dev = 2 * chip + ic
    core_arr = ic.astype(jnp.int32).reshape(1)
    chip_arr = chip.astype(jnp.int32).reshape(1)
    S = x.shape[1]
    G = D // 4
    NG = D // N_CHIP

    def chip_cols(a, width, axis):
        return lax.dynamic_slice_in_dim(a, chip * width, width, axis)

    got = gather_devices("gather_small_in", _pack([c, norm_g, pool_b, mla_q_norm], SMALL_IN)).reshape(N_DEV, SMALL_IN)
    c_all = got[:, :D]
    parts = got[0::2]
    o = D
    norm_g_full = parts[:, o:o + 12 * NG].reshape(N_CHIP, 2, 6, NG).transpose(1, 2, 0, 3).reshape(2, 6, D)
    o += 12 * NG
    pool_b_full = parts[:, o:o + G].reshape(N_CHIP, 4, G // N_CHIP).transpose(1, 0, 2).reshape(1, D)
    o += G
    q_norm_full = parts[:, o:o + QL // N_CHIP].reshape(1, QL)
    pvec = jnp.concatenate([pool_b_full, pool_scale, jnp.zeros((6, D), F32)], axis=0)

    c_pad = jnp.concatenate([c_all, jnp.zeros((8, D), F32)], axis=0)
    mod_loc = mod_fwd(c_pad, ada_w, chip_cols(ada_b, MOD_COLS, 1).reshape(2, 1, MOD_COLS))
    got = gather_devices("gather_mod", mod_loc[:, :8].transpose(1, 0, 2).reshape(8, 2 * MOD_COLS))
    mine = lax.dynamic_index_in_dim(got[0::2].reshape(N_CHIP, 8, 2, MOD_COLS), dev, axis=1, keepdims=False)
    mod = mine.transpose(1, 0, 2).reshape(2, 9, D)

    bf = lambda a: a.astype(BF16)
    shards = [ffn_w_in[i, k].reshape(2, D // 2, FSH) for i in range(2) for k in range(2)]
    shards += [ffn_w_out[i, k].reshape(2, DFF // 8, D) for i in range(2) for k in range(2)]
    shards += [pool_w[0].reshape(2, 2 * G // N_CHIP, G), mla_w_in[0].reshape(2, D // 8, QL + KVL + ROPE),
               mla_w_uq[0].reshape(2, QL // 8, N_HEADS * (NOPE + ROPE)), mla_w_o[0].reshape(2, D // 8, D)]
    slotted = cast_into_slots(chip_arr, shards)
    full = [None] * len(slotted)
    stages = [(0, 4, 8), (1, 5), (2, 6), (9, 10, 11), (3, 7)]
    for stage, members in enumerate(stages):
        got_w = gather_weights(f"gather_weights_{stage}", stage, [slotted[t] for t in members])
        for t, a in zip(members, got_w):
            full[t] = a
    ffn_in = [[full[2 * i + k].reshape(N_CHIP, D, FSH) for k in range(2)] for i in range(2)]
    ffn_out = [[full[4 + 2 * i + k].reshape(2, FSH, D) for k in range(2)] for i in range(2)]
    pw = full[8].reshape(N_CHIP, 4, G // N_CHIP, G).transpose(1, 0, 2, 3).reshape(4, G, G)
    w_in = full[9].reshape(D, QL + KVL + ROPE)
    w_uq = full[10].reshape(QL, N_HEADS, NOPE + ROPE)
    wkr = w_in[:, QL + KVL:]
    wr = jnp.transpose(w_uq[:, :, NOPE:], (1, 0, 2))
    mw = dict(wq=w_in[:, :QL], wkv=w_in[:, QL:QL + KVL], wkr=wkr, wkrs=_swap_rope(wkr),
              qn=q_norm_full, kvn=mla_kv_norm, wn=jnp.transpose(w_uq[:, :, :NOPE], (1, 0, 2)),
              wr=wr, wrs=_swap_rope(wr), wuk=jnp.transpose(bf(mla_w_uk[0]), (1, 0, 2)))
    wuv = jnp.transpose(bf(mla_w_uv[0]), (1, 0, 2))
    wo = full[11].reshape(N_HEADS, VH, D)

    place_arr = jnp.stack([chip, ic]).astype(jnp.int32)
    reducer = _GradReducer(core_arr, place_arr, dev.astype(jnp.int32).reshape(1))
    loss_mine, grad_x, vg, pgrad, ngrad = _example_step(
        x[0], loss_target[0], mod, norm_g_full, pvec, ffn_in, ffn_out, pw, mw, wuv, wo, reducer)
    loss = lax.psum(loss_mine[0, 0], ("x", "y", "c"))
    stacks, ukv = reducer.finish()
    g_ffn_in, g_ffn_out, g_pool_w, g_mla_in, g_uq, g_wo = [stacks[o] for o in range(6)]
    g_ffn_in = g_ffn_in.reshape(ffn_w_in.shape)
    g_ffn_out = g_ffn_out.reshape(ffn_w_out.shape)
    g_pool_w = g_pool_w.reshape(pool_w.shape)
    g_mla_in = g_mla_in.reshape(mla_w_in.shape)
    g_uq = g_uq.reshape(mla_w_uq.shape)
    g_wo = g_wo.reshape(mla_w_o.shape)

    ukv = sum_devices("sum_ukv", ukv)
    g_uk = ukv[:KVL].reshape(mla_w_uk.shape)
    g_uv = ukv[KVL:].reshape(mla_w_uv.shape)

    dmod = jnp.stack([jnp.concatenate([vg[i, k][0:3] for k in range(3)]) for i in range(2)])
    dnorm = jnp.stack([jnp.concatenate([vg[i, k][3:5] for k in range(3)]) for i in range(2)])
    small = _pack([dmod, dnorm, pgrad[0], pgrad[1], ngrad[0], ngrad[1, :KVL]], SMALL_GRAD)
    got = gather_devices("gather_small_grad", small)
    tot = sum_devices("sum_small_grad", got).reshape(-1)
    n_mod = 2 * 9 * D
    g_ada_b = tot[:n_mod].reshape(ada_b.shape)
    o = n_mod
    g_norm = chip_cols(tot[o:o + 12 * D].reshape(2, 6, D), NG, 2)
    o += 12 * D
    g_pool_b = chip_cols(tot[o:o + D].reshape(1, 4, G), G // N_CHIP, 2)
    o += D
    g_pool_scale = tot[o:o + D].reshape(pool_scale.shape)
    o += D
    g_q_norm = chip_cols(tot[o:o + QL].reshape(1, QL), QL // N_CHIP, 1)
    o += QL
    g_kv_norm = tot[o:o + KVL].reshape(mla_kv_norm.shape)
    dmod_all = chip_cols(got.reshape(N_DEV, -1)[:, :n_mod].reshape(N_DEV, 2, 9 * D), MOD_COLS, 2)
    dmod_pad = jnp.concatenate([dmod_all.transpose(1, 0, 2), jnp.zeros((2, 8, MOD_COLS), F32)], axis=1)

    g_ada_w, d_ada_w, nm_ada_w, nv_ada_w = adamw_ada(c_pad, dmod_pad, ada_w, m_ada_w, v_ada_w)
    small_names = ["ada_b", "norm_g", "pool_b", "pool_scale", "mla_q_norm", "mla_kv_norm"]
    small_w = [ada_b, norm_g, pool_b, pool_scale, mla_q_norm, mla_kv_norm]
    small_g = [g_ada_b, g_norm, g_pool_b, g_pool_scale, g_q_norm, g_kv_norm]
    small_m = [m_ada_b, m_norm_g, m_pool_b, m_pool_scale, m_mla_q_norm, m_mla_kv_norm]
    small_v = [v_ada_b, v_norm_g, v_pool_b, v_pool_scale, v_mla_q_norm, v_mla_kv_norm]
    packed = adamw("adamw_small", *[_pack(p, SMALL_W) for p in (small_w, small_g, small_m, small_v)])
    upd = {}
    o = 0
    for name, w in zip(small_names, small_w):
        upd[name] = [p.reshape(-1)[o:o + w.size].reshape(w.shape) for p in packed]
        o += w.size
    big = [("ffn_w_in", ffn_w_in, g_ffn_in, m_ffn_w_in, v_ffn_w_in),
           ("ffn_w_out", ffn_w_out, g_ffn_out, m_ffn_w_out, v_ffn_w_out),
           ("pool_w", pool_w, g_pool_w, m_pool_w, v_pool_w),
           ("mla_w_in", mla_w_in, g_mla_in, m_mla_w_in, v_mla_w_in),
           ("mla_w_uq", mla_w_uq, g_uq, m_mla_w_uq, v_mla_w_uq),
           ("mla_w_uk", mla_w_uk, g_uk, m_mla_w_uk, v_mla_w_uk),
           ("mla_w_uv", mla_w_uv, g_uv, m_mla_w_uv, v_mla_w_uv),
           ("mla_w_o", mla_w_o, g_wo, m_mla_w_o, v_mla_w_o)]
    for name, w, g, m, v in big:
        upd[name] = adamw("adamw_" + name, w, g, m, v)
    upd["ada_w"] = [d_ada_w, nm_ada_w, nv_ada_w]

    order = ["ada_w", "ada_b", "norm_g", "ffn_w_in", "ffn_w_out", "pool_w", "pool_b", "pool_scale", "mla_w_in",
             "mla_q_norm", "mla_kv_norm", "mla_w_uq", "mla_w_uk", "mla_w_uv", "mla_w_o"]
    grad = dict(ada_w=g_ada_w, ada_b=g_ada_b, norm_g=g_norm, ffn_w_in=g_ffn_in, ffn_w_out=g_ffn_out, pool_w=g_pool_w,
                pool_b=g_pool_b, pool_scale=g_pool_scale, mla_w_in=g_mla_in, mla_q_norm=g_q_norm,
                mla_kv_norm=g_kv_norm, mla_w_uq=g_uq, mla_w_uk=g_uk, mla_w_uv=g_uv, mla_w_o=g_wo)
    return (loss, grad_x[None], *[grad[n] for n in order], *[upd[n][0] for n in order],
            *[upd[n][1] for n in order], *[upd[n][2] for n in order])
```

```python
import functools

import jax
import jax.numpy as jnp
from jax import lax
from jax.experimental import pallas as pl
from jax.experimental.pallas import tpu as pltpu
from jax.experimental.pallas import tpu_sc as plsc

F32 = jnp.float32
BF16 = jnp.bfloat16

D = 1024
DFF = 2816
FSH = 1408
N_CHIP = 4
N_DEV = 8
N_HEADS = 16
NOPE = 64
ROPE = 32
VH = 64
QL = 256
KVL = 128
QPAD = 256
EPS = 1e-6
ATTN_SCALE = (NOPE + ROPE) ** -0.5
ROPE_THETA = 10000.0
POOL_WINDOWS = (2, 4, 8, 16)
HALO = 8
ATTN_TQ = 1024
ATTN_KC = 512

ADAM_LR, ADAM_B1, ADAM_B2, ADAM_EPS, ADAM_WD, ADAM_STEP = 0.001, 0.9, 0.999, 1e-08, 0.01, 10

VMEM_LIMIT = 60 * 1024 * 1024
MESH = pl.DeviceIdType.MESH

NT = (((1,), (1,)), ((), ()))
TN = (((0,), (0,)), ((), ()))


def _params(*sem):
    return pltpu.CompilerParams(dimension_semantics=sem, vmem_limit_bytes=VMEM_LIMIT)


def _dot(a, b, dims=None):
    if dims is None:
        return jnp.dot(a, b, preferred_element_type=F32)
    return lax.dot_general(a, b, dims, preferred_element_type=F32)


def _rms(x):
    r = lax.rsqrt(jnp.mean(x * x, axis=-1, keepdims=True) + EPS)
    return x * r, r


def _rms_bwd(xhat, r, dxhat):
    return r * (dxhat - xhat * jnp.mean(dxhat * xhat, axis=-1, keepdims=True))


def _prenorm(x, vec_ref):
    xhat, r = _rms(x)
    h = xhat * vec_ref[0:1, :] * (1.0 + vec_ref[3:4, :]) + vec_ref[2:3, :]
    return h, xhat, r


def _postnorm_bwd(dout, u, vec_ref, weight):
    uhat, r = _rms(u)
    gt = weight * (1.0 + vec_ref[4:5, :])
    dy = dout * gt
    dgate_rows = (weight * dout) * (uhat * vec_ref[1:2, :])
    dgpost_rows = dy * uhat
    du = _rms_bwd(uhat, r, dy * vec_ref[1:2, :])
    return du, dgate_rows, dgpost_rows


def _prenorm_bwd(dh, x, vec_ref, vg_ref):
    xhat, r = _rms(x)
    sc1 = 1.0 + vec_ref[3:4, :]
    g = vec_ref[0:1, :]
    vg_ref[0:1, :] += jnp.sum(dh, axis=0, keepdims=True)
    vg_ref[1:2, :] += jnp.sum(dh * (xhat * g), axis=0, keepdims=True)
    vg_ref[3:4, :] += jnp.sum(dh * sc1 * xhat, axis=0, keepdims=True)
    return _rms_bwd(xhat, r, dh * g * sc1)


def ffn_fwd(x, vec, w_in, w_out, weight):
    S = x.shape[0]
    tm = min(512, S)

    def body(x_ref, vec_ref, wg_ref, wu_ref, wo_ref, xo_ref, a_ref, u_ref, h_ref, acc_ref):
        j = pl.program_id(1)

        @pl.when(j == 0)
        def _():
            h, _, _ = _prenorm(x_ref[...], vec_ref)
            h_ref[...] = h.astype(BF16)
            acc_ref[...] = jnp.zeros_like(acc_ref)

        hb = h_ref[...]
        g = _dot(hb, wg_ref[...])
        up = _dot(hb, wu_ref[...])
        a_ref[0] = g.astype(BF16)
        a_ref[1] = up.astype(BF16)
        act = (g * jax.nn.sigmoid(g)) * up
        acc_ref[...] += _dot(act.astype(BF16), wo_ref[...])

        @pl.when(j == 1)
        def _():
            u = acc_ref[...]
            u_ref[...] = u
            uhat, _ = _rms(u)
            xo_ref[...] = x_ref[...] + (weight * (1.0 + vec_ref[4:5, :])) * (uhat * vec_ref[1:2, :])

    return pl.pallas_call(
        body, name="ffn_fwd", grid=(S // tm, 2),
        in_specs=[pl.BlockSpec((tm, D), lambda i, j: (i, 0)),
                  pl.BlockSpec((8, D), lambda i, j: (0, 0)),
                  pl.BlockSpec((None, D, FSH), lambda i, j: (j, 0, 0)),
                  pl.BlockSpec((None, D, FSH), lambda i, j: (j + 2, 0, 0)),
                  pl.BlockSpec((None, FSH, D), lambda i, j: (j, 0, 0))],
        out_specs=[pl.BlockSpec((tm, D), lambda i, j: (i, 0)),
                   pl.BlockSpec((2, tm, FSH), lambda i, j: (0, i, j)),
                   pl.BlockSpec((tm, D), lambda i, j: (i, 0)),
                   pl.BlockSpec((tm, D), lambda i, j: (i, 0))],
        out_shape=[jax.ShapeDtypeStruct((S, D), F32), jax.ShapeDtypeStruct((2, S, DFF), BF16),
                   jax.ShapeDtypeStruct((S, D), F32), jax.ShapeDtypeStruct((S, D), BF16)],
        scratch_shapes=[pltpu.VMEM((tm, D), F32)],
        compiler_params=_params("parallel", "arbitrary"),
    )(x, vec, w_in, w_in, w_out)


def ffn_bwd(dout, x, u, a, vec, w_in, w_out, weight):
    S = x.shape[0]
    tm = min(256, S)

    def body(do_ref, x_ref, u_ref, a_ref, vec_ref, wg_ref, wu_ref, wo_ref,
             dx_ref, du_ref, act_ref, da_ref, vg_ref, dh_ref):
        i, j = pl.program_id(0), pl.program_id(1)

        @pl.when((i == 0) & (j == 0))
        def _():
            vg_ref[...] = jnp.zeros_like(vg_ref)

        @pl.when(j == 0)
        def _():
            du, dgate_rows, dgpost_rows = _postnorm_bwd(do_ref[...], u_ref[...], vec_ref, weight)
            vg_ref[2:3, :] += jnp.sum(dgate_rows, axis=0, keepdims=True)
            vg_ref[4:5, :] += jnp.sum(dgpost_rows, axis=0, keepdims=True)
            du_ref[...] = du.astype(BF16)
            dh_ref[...] = jnp.zeros_like(dh_ref)

        dact = _dot(du_ref[...], wo_ref[...], NT)
        g = a_ref[0].astype(F32)
        up = a_ref[1].astype(F32)
        s = jax.nn.sigmoid(g)
        silu = g * s
        act_ref[...] = (silu * up).astype(BF16)
        dg = (dact * up * (s * (1.0 + g * (1.0 - s)))).astype(BF16)
        dup = (dact * silu).astype(BF16)
        da_ref[0] = dg
        da_ref[1] = dup
        dh_ref[...] += _dot(dg, wg_ref[...], NT) + _dot(dup, wu_ref[...], NT)

        @pl.when(j == 1)
        def _():
            dx_ref[...] = do_ref[...] + _prenorm_bwd(dh_ref[...], x_ref[...], vec_ref, vg_ref)

    row = lambda i, j: (i, 0)
    return pl.pallas_call(
        body, name="ffn_bwd", grid=(S // tm, 2),
        in_specs=[pl.BlockSpec((tm, D), row), pl.BlockSpec((tm, D), row), pl.BlockSpec((tm, D), row),
                  pl.BlockSpec((2, tm, FSH), lambda i, j: (0, i, j)),
                  pl.BlockSpec((8, D), lambda i, j: (0, 0)),
                  pl.BlockSpec((None, D, FSH), lambda i, j: (j, 0, 0)),
                  pl.BlockSpec((None, D, FSH), lambda i, j: (j + 2, 0, 0)),
                  pl.BlockSpec((None, FSH, D), lambda i, j: (j, 0, 0))],
        out_specs=[pl.BlockSpec((tm, D), row), pl.BlockSpec((tm, D), row),
                   pl.BlockSpec((tm, FSH), lambda i, j: (i, j)),
                   pl.BlockSpec((2, tm, FSH), lambda i, j: (0, i, j)),
                   pl.BlockSpec((8, D), lambda i, j: (0, 0))],
        out_shape=[jax.ShapeDtypeStruct((S, D), F32), jax.ShapeDtypeStruct((S, D), BF16),
                   jax.ShapeDtypeStruct((S, DFF), BF16), jax.ShapeDtypeStruct((2, S, DFF), BF16),
                   jax.ShapeDtypeStruct((8, D), F32)],
        scratch_shapes=[pltpu.VMEM((tm, D), F32)],
        compiler_params=_params("arbitrary", "arbitrary"),
    )(dout, x, u, a, vec, w_in, w_in, w_out)


def dw_matmul(name, a, b, a_spec, b_spec, out_shape, out_spec, grid):
    def body(a_ref, b_ref, o_ref):
        @pl.when(pl.program_id(len(grid) - 1) == 0)
        def _():
            o_ref[...] = jnp.zeros_like(o_ref)

        o_ref[...] += _dot(a_ref[...], b_ref[...], TN)

    return pl.pallas_call(
        body, name=name, grid=grid, in_specs=[a_spec, b_spec], out_specs=out_spec,
        out_shape=jax.ShapeDtypeStruct(out_shape, F32),
        compiler_params=_params(*(["parallel"] * (len(grid) - 1) + ["arbitrary"])),
    )(a, b)


def ffn_dw(h, da, act, du):
    S = h.shape[0]
    tk = min(512, S)
    dw_in = dw_matmul("ffn_dw_in", h, da,
                      pl.BlockSpec((tk, D), lambda n, k: (k, 0)),
                      pl.BlockSpec((None, tk, FSH), lambda n, k: (n // 2, k, n % 2)),
                      (N_CHIP, D, FSH), pl.BlockSpec((None, D, FSH), lambda n, k: (n, 0, 0)),
                      (N_CHIP, S // tk))
    dw_out = dw_matmul("ffn_dw_out", act, du,
                       pl.BlockSpec((tk, FSH), lambda n, k: (k, n)),
                       pl.BlockSpec((tk, D), lambda n, k: (k, 0)),
                       (DFF, D), pl.BlockSpec((FSH, D), lambda n, k: (n, 0)),
                       (2, S // tk))
    return dw_in, dw_out


def _halo_specs(tm, S):
    nb = tm // HALO
    last = S // HALO - 1
    return [pl.BlockSpec((HALO, D), lambda i: (jnp.maximum(i * nb - 1, 0), 0)),
            pl.BlockSpec((tm, D), lambda i: (i, 0)),
            pl.BlockSpec((HALO, D), lambda i: (jnp.minimum((i + 1) * nb, last), 0))]


def _shift_rows(v, k):
    return pltpu.roll(v, k % v.shape[0], 0)


def _window_sum(v, g, forward):
    acc = v + _shift_rows(v, 1 if forward else -1)
    for step in (1, 2, 4)[:g]:
        acc = _shift_rows(acc, step) + _shift_rows(acc, -step)
    return acc


def _pool_count(t, w, S):
    return jnp.maximum(jnp.minimum(t + w // 2, S) - jnp.maximum(t - w // 2, 0), 1).astype(F32)


def pool_fwd(x, vec, pw, pvec):
    S = x.shape[0]
    tm = min(256, S)
    G = D // 4

    def body(xp_ref, x_ref, xn_ref, vec_ref, pw_ref, pv_ref, xo_ref, y_ref, z_ref):
        i = pl.program_id(0)
        xa = jnp.concatenate([xp_ref[...], x_ref[...], xn_ref[...]], axis=0)
        t = i * tm - HALO + lax.broadcasted_iota(jnp.int32, (tm + 2 * HALO, 1), 0)
        h, _, _ = _prenorm(xa, vec_ref)
        h = jnp.where((t >= 0) & (t < S), h, 0.0)
        tmain = t[HALO:HALO + tm]
        for g in range(4):
            hg = h[:, g * G:(g + 1) * G]
            pooled = _window_sum(hg, g, True)[HALO:HALO + tm] / _pool_count(tmain, POOL_WINDOWS[g], S)
            z = (pooled - hg[HALO:HALO + tm]).astype(BF16)
            z_ref[:, g * G:(g + 1) * G] = z
            y_ref[:, g * G:(g + 1) * G] = _dot(z, pw_ref[g]) + pv_ref[0:1, g * G:(g + 1) * G]
        u = y_ref[...] * pv_ref[1:2, :]
        uhat, _ = _rms(u)
        xo_ref[...] = x_ref[...] + (1.0 + vec_ref[4:5, :]) * (uhat * vec_ref[1:2, :])

    row = lambda i: (i, 0)
    full = lambda i: (0, 0)
    return pl.pallas_call(
        body, name="pool_fwd", grid=(S // tm,),
        in_specs=_halo_specs(tm, S) + [pl.BlockSpec((8, D), full), pl.BlockSpec((4, G, G), lambda i: (0, 0, 0)),
                                       pl.BlockSpec((8, D), full)],
        out_specs=[pl.BlockSpec((tm, D), row)] * 3,
        out_shape=[jax.ShapeDtypeStruct((S, D), F32), jax.ShapeDtypeStruct((S, D), F32),
                   jax.ShapeDtypeStruct((S, D), BF16)],
        compiler_params=_params("parallel"),
    )(x, x, x, vec, pw, pvec)


def pool_bwd(dout, x, y, z, vec, pw, pvec):
    S = x.shape[0]
    tm = min(256, S)
    G = D // 4
    R = G // N_CHIP

    def body(dop_ref, do_ref, don_ref, yp_ref, y_ref, yn_ref, x_ref, z_ref, vec_ref, pw_ref, pv_ref,
             dx_ref, vg_ref, pg_ref, dw_ref, dh_ref):
        i = pl.program_id(0)

        @pl.when(i == 0)
        def _():
            vg_ref[...] = jnp.zeros_like(vg_ref)
            pg_ref[...] = jnp.zeros_like(pg_ref)
            dw_ref[...] = jnp.zeros_like(dw_ref)

        doa = jnp.concatenate([dop_ref[...], do_ref[...], don_ref[...]], axis=0)
        ya = jnp.concatenate([yp_ref[...], y_ref[...], yn_ref[...]], axis=0)
        t = i * tm - HALO + lax.broadcasted_iota(jnp.int32, (tm + 2 * HALO, 1), 0)
        inside = (t >= 0) & (t < S)
        main = (t >= i * tm) & (t < (i + 1) * tm)
        du, dgate_rows, dgpost_rows = _postnorm_bwd(doa, ya * pv_ref[1:2, :], vec_ref, 1.0)
        du = jnp.where(inside, du, 0.0)
        vg_ref[2:3, :] += jnp.sum(jnp.where(main, dgate_rows, 0.0), axis=0, keepdims=True)
        vg_ref[4:5, :] += jnp.sum(jnp.where(main, dgpost_rows, 0.0), axis=0, keepdims=True)
        dy = du * pv_ref[1:2, :]
        pg_ref[0:1, :] += jnp.sum(jnp.where(main, dy, 0.0), axis=0, keepdims=True)
        pg_ref[1:2, :] += jnp.sum(jnp.where(main, du * ya, 0.0), axis=0, keepdims=True)
        for g in range(4):
            dyg = dy[:, g * G:(g + 1) * G].astype(BF16)
            dz = _dot(dyg, pw_ref[g], NT)
            e = dz / _pool_count(t, POOL_WINDOWS[g], S)
            dh_ref[:, g * G:(g + 1) * G] = (_window_sum(e, g, False) - dz)[HALO:HALO + tm]
            dwg = _dot(z_ref[:, g * G:(g + 1) * G], dyg[HALO:HALO + tm], TN)
            for q in range(N_CHIP):
                dw_ref[q, g] += dwg[q * R:(q + 1) * R, :]
        dx_ref[...] = do_ref[...] + _prenorm_bwd(dh_ref[...], x_ref[...], vec_ref, vg_ref)

    row = lambda i: (i, 0)
    full = lambda i: (0, 0)
    halo = _halo_specs(tm, S)
    return pl.pallas_call(
        body, name="pool_bwd", grid=(S // tm,),
        in_specs=halo + halo + [pl.BlockSpec((tm, D), row), pl.BlockSpec((tm, D), row), pl.BlockSpec((8, D), full),
                                pl.BlockSpec((4, G, G), lambda i: (0, 0, 0)), pl.BlockSpec((8, D), full)],
        out_specs=[pl.BlockSpec((tm, D), row), pl.BlockSpec((8, D), full), pl.BlockSpec((8, D), full),
                   pl.BlockSpec((N_CHIP, 4, R, G), lambda i: (0, 0, 0, 0))],
        out_shape=[jax.ShapeDtypeStruct((S, D), F32), jax.ShapeDtypeStruct((8, D), F32),
                   jax.ShapeDtypeStruct((8, D), F32), jax.ShapeDtypeStruct((N_CHIP, 4, R, G), F32)],
        scratch_shapes=[pltpu.VMEM((tm, D), F32)],
        compiler_params=_params("arbitrary"),
    )(dout, dout, dout, y, y, y, x, z, vec, pw, pvec)


N_PAIR = N_HEADS // 2
SLOTS = 128 // ROPE
ROPE_ALL = N_HEADS * ROPE
NOPE_ALL = N_HEADS * NOPE
LAT_ALL = N_HEADS * KVL
DLAT = QL + KVL + 2 * 128
DQ_ALL = NOPE_ALL + 2 * ROPE_ALL


def _w3(shape):
    return pl.BlockSpec(shape, lambda i: (0,) * len(shape))


def _slot_mask(hd, rows):
    lane = lax.broadcasted_iota(jnp.int32, (rows, 128), 1)
    return (lane // ROPE) == (hd % SLOTS)


MLA_WEIGHTS = ("wq", "wkv", "wkr4", "wkrs4", "qn", "kvn", "wn", "wr", "wrs", "bduk")


def _mla_weight_specs():
    return [_w3((D, QL)), _w3((D, KVL)), _w3((D, 128)), _w3((D, 128)), _w3((1, QL)), _w3((1, KVL)),
            _w3((QL, NOPE_ALL)), _w3((QL, ROPE_ALL)), _w3((QL, ROPE_ALL)), _w3((N_PAIR, 2 * NOPE, 2 * KVL))]


def mla_pre(x, vec, mw, tabs):
    S = x.shape[0]
    tm = min(256, S)

    def body(x_ref, vec_ref, cos_ref, sin_ref, wq_ref, wkv_ref, wkr_ref, wkrs_ref, qn_ref, kvn_ref,
             wn_ref, wr_ref, wrs_ref, bduk_ref,
             h_ref, cq_ref, ckv_ref, cqn_ref, qnope_ref, qcat_ref, kcat_ref, vcat_ref):
        h, _, _ = _prenorm(x_ref[...], vec_ref)
        hb = h.astype(BF16)
        h_ref[...] = hb
        cq_raw = _dot(hb, wq_ref[...])
        ckv_raw = _dot(hb, wkv_ref[...])
        cq_ref[...] = cq_raw
        ckv_ref[...] = ckv_raw
        cos, sin = cos_ref[...], sin_ref[...]
        ckv = (_rms(ckv_raw)[0] * kvn_ref[...]).astype(BF16)
        kcat_ref[:, 0:KVL] = ckv
        kcat_ref[:, KVL:] = (_dot(hb, wkr_ref[...]) * cos + _dot(hb, wkrs_ref[...]) * sin).astype(BF16)
        vcat_ref[:, 0:KVL] = ckv
        ones = lax.broadcasted_iota(jnp.int32, (tm, QPAD - KVL), 1) == 0
        vcat_ref[:, KVL:] = jnp.where(ones, 1.0, 0.0).astype(BF16)
        cqb = (_rms(cq_raw)[0] * qn_ref[...]).astype(BF16)
        cqn_ref[...] = cqb
        qn = _dot(cqb, wn_ref[...]).astype(BF16)
        qnope_ref[...] = qn
        cos4, sin4 = jnp.tile(cos, (1, SLOTS)), jnp.tile(sin, (1, SLOTS))
        qr = ((_dot(cqb, wr_ref[...]) * cos4 + _dot(cqb, wrs_ref[...]) * sin4) * ATTN_SCALE).astype(BF16)
        for j in range(N_PAIR):
            ql = (_dot(qn[:, 128 * j:128 * (j + 1)], bduk_ref[j]) * ATTN_SCALE).astype(BF16)
            for hd in (2 * j, 2 * j + 1):
                qcat_ref[hd, :, 0:KVL] = ql[:, KVL * (hd - 2 * j):KVL * (hd - 2 * j + 1)]
                group = qr[:, 128 * (hd // SLOTS):128 * (hd // SLOTS + 1)]
                qcat_ref[hd, :, KVL:] = jnp.where(_slot_mask(hd, tm), group, jnp.zeros_like(group))

    row = lambda i: (i, 0)
    hrow = lambda i: (0, i, 0)
    return pl.pallas_call(
        body, name="mla_pre", grid=(S // tm,),
        in_specs=[pl.BlockSpec((tm, D), row), _w3((8, D)), pl.BlockSpec((tm, 128), row), pl.BlockSpec((tm, 128), row)]
        + _mla_weight_specs(),
        out_specs=[pl.BlockSpec((tm, D), row), pl.BlockSpec((tm, QL), row), pl.BlockSpec((tm, KVL), row),
                   pl.BlockSpec((tm, QL), row), pl.BlockSpec((tm, NOPE_ALL), row),
                   pl.BlockSpec((N_HEADS, tm, QPAD), hrow), pl.BlockSpec((tm, QPAD), row),
                   pl.BlockSpec((tm, QPAD), row)],
        out_shape=[jax.ShapeDtypeStruct((S, D), BF16), jax.ShapeDtypeStruct((S, QL), F32),
                   jax.ShapeDtypeStruct((S, KVL), F32), jax.ShapeDtypeStruct((S, QL), BF16),
                   jax.ShapeDtypeStruct((S, NOPE_ALL), BF16), jax.ShapeDtypeStruct((N_HEADS, S, QPAD), BF16),
                   jax.ShapeDtypeStruct((S, QPAD), BF16), jax.ShapeDtypeStruct((S, QPAD), BF16)],
        compiler_params=_params("parallel"),
    )(x, vec, tabs[0], tabs[1], *[mw[k] for k in MLA_WEIGHTS])


def attn_fwd(qcat, kcat, vcat):
    S = kcat.shape[0]
    tq = min(ATTN_TQ, S)
    kc = min(ATTN_KC, S)

    def body(q_ref, k_ref, v_ref, o_ref, lse_ref):
        q = q_ref[...]
        m = jnp.full((tq, 1), -jnp.inf, F32)
        ov = jnp.zeros((tq, QPAD), F32)
        for c in range(S // kc):
            s = _dot(q, k_ref[c * kc:(c + 1) * kc, :], NT)
            m_new = jnp.maximum(m, jnp.max(s, axis=-1, keepdims=True))
            p = jnp.exp(s - m_new).astype(BF16)
            ov = ov * jnp.exp(m - m_new) + _dot(p, v_ref[c * kc:(c + 1) * kc, :])
            m = m_new
        l = ov[:, KVL:KVL + 1]
        o_ref[...] = (ov[:, 0:KVL] * (1.0 / l)).astype(BF16)
        lse_ref[...] = m + jnp.log(l)

    return pl.pallas_call(
        body, name="attn_fwd", grid=(N_HEADS, S // tq),
        in_specs=[pl.BlockSpec((None, tq, QPAD), lambda h, i: (h, i, 0)),
                  pl.BlockSpec((S, QPAD), lambda h, i: (0, 0)),
                  pl.BlockSpec((S, QPAD), lambda h, i: (0, 0))],
        out_specs=[pl.BlockSpec((tq, KVL), lambda h, i: (i, h)),
                   pl.BlockSpec((None, tq, 1), lambda h, i: (h, i, 0))],
        out_shape=[jax.ShapeDtypeStruct((S, LAT_ALL), BF16), jax.ShapeDtypeStruct((N_HEADS, S, 1), F32)],
        compiler_params=_params("parallel", "parallel"),
    )(qcat, kcat, vcat)


def mla_post(olat, x, vec, bduv, wo):
    S = x.shape[0]
    tm = min(256, S)

    def body(o_ref, x_ref, vec_ref, bduv_ref, wo_ref, xo_ref, u_ref, ocat_ref):
        for j in range(N_PAIR):
            oc = _dot(o_ref[:, 2 * KVL * j:2 * KVL * (j + 1)], bduv_ref[j])
            ocat_ref[:, 2 * VH * j:2 * VH * (j + 1)] = oc.astype(BF16)
        u = _dot(ocat_ref[...], wo_ref[...])
        u_ref[...] = u
        uhat, _ = _rms(u)
        xo_ref[...] = x_ref[...] + (1.0 + vec_ref[4:5, :]) * (uhat * vec_ref[1:2, :])

    row = lambda i: (i, 0)
    return pl.pallas_call(
        body, name="mla_post", grid=(S // tm,),
        in_specs=[pl.BlockSpec((tm, LAT_ALL), row), pl.BlockSpec((tm, D), row), _w3((8, D)),
                  _w3((N_PAIR, 2 * KVL, 2 * VH)), _w3((D, D))],
        out_specs=[pl.BlockSpec((tm, D), row), pl.BlockSpec((tm, D), row), pl.BlockSpec((tm, D), row)],
        out_shape=[jax.ShapeDtypeStruct((S, D), F32), jax.ShapeDtypeStruct((S, D), F32),
                   jax.ShapeDtypeStruct((S, D), BF16)],
        compiler_params=_params("parallel"),
    )(olat, x, vec, bduv, wo)


def mla_post_bwd(dout, u, olat, vec, bduv, wo):
    S = u.shape[0]
    tm = min(256, S)

    def body(do_ref, u_ref, o_ref, vec_ref, bduv_ref, wo_ref, du_ref, docat_ref, dolat_ref, delta_ref, vg_ref):
        @pl.when(pl.program_id(0) == 0)
        def _():
            vg_ref[...] = jnp.zeros_like(vg_ref)

        du, dgate_rows, dgpost_rows = _postnorm_bwd(do_ref[...], u_ref[...], vec_ref, 1.0)
        vg_ref[2:3, :] += jnp.sum(dgate_rows, axis=0, keepdims=True)
        vg_ref[4:5, :] += jnp.sum(dgpost_rows, axis=0, keepdims=True)
        dub = du.astype(BF16)
        du_ref[...] = dub
        docat_ref[...] = _dot(dub, wo_ref[...], NT).astype(BF16)
        for j in range(N_PAIR):
            dol = _dot(docat_ref[:, 2 * VH * j:2 * VH * (j + 1)], bduv_ref[j], NT).astype(BF16)
            dolat_ref[:, 2 * KVL * j:2 * KVL * (j + 1)] = dol
            prod = dol.astype(F32) * o_ref[:, 2 * KVL * j:2 * KVL * (j + 1)].astype(F32)
            delta_ref[2 * j] = jnp.sum(prod[:, 0:KVL], axis=-1, keepdims=True)
            delta_ref[2 * j + 1] = jnp.sum(prod[:, KVL:], axis=-1, keepdims=True)

    row = lambda i: (i, 0)
    hrow = lambda i: (0, i, 0)
    return pl.pallas_call(
        body, name="mla_post_bwd", grid=(S // tm,),
        in_specs=[pl.BlockSpec((tm, D), row), pl.BlockSpec((tm, D), row), pl.BlockSpec((tm, LAT_ALL), row),
                  _w3((8, D)), _w3((N_PAIR, 2 * KVL, 2 * VH)), _w3((D, D))],
        out_specs=[pl.BlockSpec((tm, D), row), pl.BlockSpec((tm, D), row),
                   pl.BlockSpec((tm, LAT_ALL), row), pl.BlockSpec((N_HEADS, tm, 1), hrow), _w3((8, D))],
        out_shape=[jax.ShapeDtypeStruct((S, D), BF16), jax.ShapeDtypeStruct((S, D), BF16),
                   jax.ShapeDtypeStruct((S, LAT_ALL), BF16), jax.ShapeDtypeStruct((N_HEADS, S, 1), F32),
                   jax.ShapeDtypeStruct((8, D), F32)],
        compiler_params=_params("arbitrary"),
    )(dout, u, olat, vec, bduv, wo)


def attn_bwd(qcat, kcat, kcat_t, dolat, lse_row, delta_row):
    S = kcat.shape[0]
    tq = min(ATTN_TQ, S)
    kc = min(ATTN_KC, S)

    def body(q_ref, k_ref, kt_ref, do_ref, lse_ref, dl_ref, dq_ref, dk_ref, dv_ref):
        @pl.when((pl.program_id(0) == 0) & (pl.program_id(1) == 0))
        def _():
            dk_ref[...] = jnp.zeros_like(dk_ref)
            dv_ref[...] = jnp.zeros_like(dv_ref)

        q, do = q_ref[...], do_ref[...]
        lse, dl = lse_ref[...], dl_ref[...]
        dqt = jnp.zeros((QPAD, tq), F32)
        for c in range(S // kc):
            rows = slice(c * kc, (c + 1) * kc)
            st = _dot(k_ref[rows, :], q, NT)
            pt = jnp.exp(st - lse)
            dpt = _dot(k_ref[rows, 0:KVL], do, NT)
            dst = (pt * (dpt - dl)).astype(BF16)
            dv_ref[rows, :] += _dot(pt.astype(BF16), do)
            dk_ref[rows, :] += _dot(dst, q)
            dqt = dqt + _dot(kt_ref[:, rows], dst)
        dq_ref[...] = dqt.T

    return pl.pallas_call(
        body, name="attn_bwd", grid=(N_HEADS, S // tq),
        in_specs=[pl.BlockSpec((None, tq, QPAD), lambda h, i: (h, i, 0)),
                  pl.BlockSpec((S, QPAD), lambda h, i: (0, 0)),
                  pl.BlockSpec((QPAD, S), lambda h, i: (0, 0)),
                  pl.BlockSpec((tq, KVL), lambda h, i: (i, h)),
                  pl.BlockSpec((None, 1, tq), lambda h, i: (h, 0, i)),
                  pl.BlockSpec((None, 1, tq), lambda h, i: (h, 0, i))],
        out_specs=[pl.BlockSpec((None, tq, QPAD), lambda h, i: (h, i, 0)),
                   pl.BlockSpec((S, QPAD), lambda h, i: (0, 0)),
                   pl.BlockSpec((S, KVL), lambda h, i: (0, 0))],
        out_shape=[jax.ShapeDtypeStruct((N_HEADS, S, QPAD), F32), jax.ShapeDtypeStruct((S, QPAD), F32),
                   jax.ShapeDtypeStruct((S, KVL), F32)],
        compiler_params=_params("arbitrary", "arbitrary"),
    )(qcat, kcat, kcat_t, dolat, lse_row, delta_row)


def mla_pre_bwd(dout, dq, dk, dv, x, cq_raw, ckv_raw, vec, mw, tabs):
    S = x.shape[0]
    tm = min(256, S)

    def body(do_ref, dq_ref, dk_ref, dv_ref, x_ref, cq_ref, ckv_ref, vec_ref, cos_ref, sin_ref,
             wq_ref, wkv_ref, wkr_ref, wkrs_ref, qn_ref, kvn_ref, wn_ref, wr_ref, wrs_ref, bduk_ref,
             dx_ref, dlat_ref, dql_ref, dqcat_ref, vg_ref, ng_ref):
        @pl.when(pl.program_id(0) == 0)
        def _():
            vg_ref[...] = jnp.zeros_like(vg_ref)
            ng_ref[...] = jnp.zeros_like(ng_ref)

        cos, sin = cos_ref[...], sin_ref[...]
        for j in range(N_PAIR):
            dql = jnp.concatenate([dq_ref[2 * j, :, 0:KVL], dq_ref[2 * j + 1, :, 0:KVL]], axis=1) * ATTN_SCALE
            dql = dql.astype(BF16)
            dql_ref[:, 2 * KVL * j:2 * KVL * (j + 1)] = dql
            dqcat_ref[:, 2 * NOPE * j:2 * NOPE * (j + 1)] = _dot(dql, bduk_ref[j], NT).astype(BF16)
        groups = []
        for grp in range(N_HEADS // SLOTS):
            acc = jnp.zeros((tm, 128), F32)
            for hd in range(SLOTS * grp, SLOTS * (grp + 1)):
                acc = acc + jnp.where(_slot_mask(hd, tm), dq_ref[hd, :, KVL:], 0.0)
            groups.append(acc)
        dqr = jnp.concatenate(groups, axis=1) * ATTN_SCALE
        qa = (dqr * jnp.tile(cos, (1, SLOTS))).astype(BF16)
        qb = (dqr * jnp.tile(sin, (1, SLOTS))).astype(BF16)
        dqcat_ref[:, NOPE_ALL:NOPE_ALL + ROPE_ALL] = qa
        dqcat_ref[:, NOPE_ALL + ROPE_ALL:] = qb
        dcq = _dot(dqcat_ref[:, 0:NOPE_ALL], wn_ref[...], NT) + _dot(qa, wr_ref[...], NT) + _dot(qb, wrs_ref[...], NT)
        cqh, rq = _rms(cq_ref[...])
        ng_ref[0:1, :] += jnp.sum(dcq * cqh, axis=0, keepdims=True)
        dcq_raw = _rms_bwd(cqh, rq, dcq * qn_ref[...]).astype(BF16)
        dckv = dk_ref[:, 0:KVL] + dv_ref[...]
        ckvh, rk = _rms(ckv_ref[...])
        ng_ref[1:2, 0:KVL] += jnp.sum(dckv * ckvh, axis=0, keepdims=True)
        dckv_raw = _rms_bwd(ckvh, rk, dckv * kvn_ref[...]).astype(BF16)
        dkr = dk_ref[:, KVL:]
        ka = (dkr * cos).astype(BF16)
        kb = (dkr * sin).astype(BF16)
        dlat_ref[:, 0:QL] = dcq_raw
        dlat_ref[:, QL:QL + KVL] = dckv_raw
        dlat_ref[:, QL + KVL:QL + KVL + 128] = ka
        dlat_ref[:, QL + KVL + 128:] = kb
        dh = (_dot(dcq_raw, wq_ref[...], NT) + _dot(dckv_raw, wkv_ref[...], NT)
              + _dot(ka, wkr_ref[...], NT) + _dot(kb, wkrs_ref[...], NT))
        dx_ref[...] = do_ref[...] + _prenorm_bwd(dh, x_ref[...], vec_ref, vg_ref)

    row = lambda i: (i, 0)
    hrow = lambda i: (0, i, 0)
    return pl.pallas_call(
        body, name="mla_pre_bwd", grid=(S // tm,),
        in_specs=[pl.BlockSpec((tm, D), row), pl.BlockSpec((N_HEADS, tm, QPAD), hrow), pl.BlockSpec((tm, QPAD), row),
                  pl.BlockSpec((tm, KVL), row), pl.BlockSpec((tm, D), row), pl.BlockSpec((tm, QL), row),
                  pl.BlockSpec((tm, KVL), row), _w3((8, D)), pl.BlockSpec((tm, 128), row), pl.BlockSpec((tm, 128), row)]
        + _mla_weight_specs(),
        out_specs=[pl.BlockSpec((tm, D), row), pl.BlockSpec((tm, DLAT), row), pl.BlockSpec((tm, LAT_ALL), row),
                   pl.BlockSpec((tm, DQ_ALL), row), _w3((8, D)), _w3((8, QL))],
        out_shape=[jax.ShapeDtypeStruct((S, D), F32), jax.ShapeDtypeStruct((S, DLAT), BF16),
                   jax.ShapeDtypeStruct((S, LAT_ALL), BF16), jax.ShapeDtypeStruct((S, DQ_ALL), BF16),
                   jax.ShapeDtypeStruct((8, D), F32), jax.ShapeDtypeStruct((8, QL), F32)],
        compiler_params=_params("arbitrary"),
    )(dout, dq, dk, dv, x, cq_raw, ckv_raw, vec, tabs[0], tabs[1], *[mw[k] for k in MLA_WEIGHTS])


def mla_dw(h, dlat, cqn, dqcat, dql, qnope, olat, docat, ocat, du):
    S = h.shape[0]
    tk = min(512, S)
    nk = S // tk
    flat = lambda w: pl.BlockSpec((tk, w), lambda k: (k, 0))
    cols = lambda w: pl.BlockSpec((tk, w), lambda n, k: (k, n))
    pair_o = pl.BlockSpec((None, 2 * KVL, 128), lambda n, k: (n, 0, 0))
    g = {}
    g["in"] = dw_matmul("mla_dw_in", h, dlat, flat(D), flat(DLAT), (D, DLAT),
                        pl.BlockSpec((D, DLAT), lambda k: (0, 0)), (nk,))
    g["q"] = dw_matmul("mla_dw_q", cqn, dqcat, flat(QL), flat(DQ_ALL), (QL, DQ_ALL),
                       pl.BlockSpec((QL, DQ_ALL), lambda k: (0, 0)), (nk,))
    g["uk"] = dw_matmul("mla_dw_uk", dql, qnope, cols(2 * KVL), cols(2 * NOPE), (N_PAIR, 2 * KVL, 2 * NOPE), pair_o,
                        (N_PAIR, nk))
    g["uv"] = dw_matmul("mla_dw_uv", olat, docat, cols(2 * KVL), cols(2 * VH), (N_PAIR, 2 * KVL, 2 * VH), pair_o,
                        (N_PAIR, nk))
    g["o"] = dw_matmul("mla_dw_o", ocat, du, cols(256), pl.BlockSpec((tk, D), lambda n, k: (k, 0)), (D, D),
                       pl.BlockSpec((256, D), lambda n, k: (n, 0)), (D // 256, nk))
    return g


def loss_head(y, target):
    S = y.shape[0]
    tm = min(512, S)

    def body(y_ref, t_ref, loss_ref, dy_ref):
        @pl.when(pl.program_id(0) == 0)
        def _():
            loss_ref[...] = jnp.zeros_like(loss_ref)

        err = y_ref[...] - t_ref[...]
        dy_ref[...] = err * (1.0 / D)
        loss_ref[...] += 0.5 * jnp.sum(jnp.mean(err * err, axis=-1, keepdims=True), axis=0, keepdims=True)

    row = lambda i: (i, 0)
    return pl.pallas_call(
        body, name="loss_head", grid=(S // tm,),
        in_specs=[pl.BlockSpec((tm, D), row), pl.BlockSpec((tm, D), row)],
        out_specs=[pl.BlockSpec((1, 1), lambda i: (0, 0)), pl.BlockSpec((tm, D), row)],
        out_shape=[jax.ShapeDtypeStruct((1, 1), F32), jax.ShapeDtypeStruct((S, D), F32)],
        compiler_params=_params("arbitrary"),
    )(y, target)


MOD_COLS = 9 * D // N_CHIP


def mod_fwd(c_pad, ada_w, ada_b_loc):
    tn = MOD_COLS // 3

    def body(c_ref, w_ref, b_ref, o_ref):
        c = c_ref[...]
        sc = (c * jax.nn.sigmoid(c)).astype(BF16)
        o_ref[...] = _dot(sc, w_ref[...].astype(BF16)) + b_ref[...]

    return pl.pallas_call(
        body, name="mod_fwd", grid=(2, 3),
        in_specs=[pl.BlockSpec((16, D), lambda i, n: (0, 0)), pl.BlockSpec((None, D, tn), lambda i, n: (i, 0, n)),
                  pl.BlockSpec((None, 1, tn), lambda i, n: (i, 0, n))],
        out_specs=pl.BlockSpec((None, 16, tn), lambda i, n: (i, 0, n)),
        out_shape=jax.ShapeDtypeStruct((2, 16, MOD_COLS), F32),
        compiler_params=_params("parallel", "parallel"),
    )(c_pad, ada_w, ada_b_loc)


def _adamw_math(w, g, m, v):
    m = ADAM_B1 * m + (1.0 - ADAM_B1) * g
    v = ADAM_B2 * v + (1.0 - ADAM_B2) * (g * g)
    m_hat = m / (1.0 - ADAM_B1 ** ADAM_STEP)
    v_hat = v / (1.0 - ADAM_B2 ** ADAM_STEP)
    delta = -ADAM_LR * (m_hat / (jnp.sqrt(v_hat) + ADAM_EPS) + ADAM_WD * w)
    return delta, m, v


def adamw(name, w, g, m, v):
    shape = w.shape
    cols = shape[-1]
    rows = w.size // cols
    tr = rows
    for cand in (512, 256, 128, 64, 32, 16, 8):
        if rows % cand == 0 and cand * cols * 4 <= (2 << 20):
            tr = cand
            break

    def body(w_ref, g_ref, m_ref, v_ref, d_ref, mo_ref, vo_ref):
        d_ref[...], mo_ref[...], vo_ref[...] = _adamw_math(w_ref[...], g_ref[...], m_ref[...], v_ref[...])

    spec = pl.BlockSpec((tr, cols), lambda i: (i, 0))
    outs = pl.pallas_call(
        body, name=name, grid=(rows // tr,), in_specs=[spec] * 4, out_specs=[spec] * 3,
        out_shape=[jax.ShapeDtypeStruct((rows, cols), F32)] * 3,
        compiler_params=_params("parallel"),
    )(*[a.reshape(rows, cols) for a in (w, g, m, v)])
    return [o.reshape(shape) for o in outs]


def adamw_ada(c_pad, dmod, w, m, v):
    tr = 256

    def body(c_ref, dm_ref, w_ref, m_ref, v_ref, g_ref, d_ref, mo_ref, vo_ref):
        c = c_ref[...]
        sc = (c * jax.nn.sigmoid(c)).astype(BF16)
        g = _dot(sc, dm_ref[...].astype(BF16), TN)
        g_ref[...] = g
        d_ref[...], mo_ref[...], vo_ref[...] = _adamw_math(w_ref[...], g, m_ref[...], v_ref[...])

    wspec = pl.BlockSpec((None, tr, MOD_COLS), lambda i, r: (i, r, 0))
    return pl.pallas_call(
        body, name="adamw_ada", grid=(2, D // tr),
        in_specs=[pl.BlockSpec((16, tr), lambda i, r: (0, r)),
                  pl.BlockSpec((None, 16, MOD_COLS), lambda i, r: (i, 0, 0)), wspec, wspec, wspec],
        out_specs=[wspec] * 4,
        out_shape=[jax.ShapeDtypeStruct((2, D, MOD_COLS), F32)] * 4,
        compiler_params=_params("parallel", "parallel"),
    )(c_pad, dmod, w, m, v)


def sum_devices(name, a):
    _, R, C = a.shape
    tr = R
    for cand in (64, 32, 16, 8):
        if R % cand == 0:
            tr = cand
            break

    def body(a_ref, o_ref):
        acc = a_ref[0]
        for dev in range(1, N_DEV):
            acc = acc + a_ref[dev]
        o_ref[...] = acc

    return pl.pallas_call(
        body, name=name, grid=(R // tr,),
        in_specs=[pl.BlockSpec((N_DEV, tr, C), lambda i: (0, i, 0))],
        out_specs=pl.BlockSpec((tr, C), lambda i: (i, 0)),
        out_shape=jax.ShapeDtypeStruct((R, C), F32),
        compiler_params=_params("parallel"),
    )(a)


def _place():
    return lax.axis_index("x"), lax.axis_index("y"), lax.axis_index("c")


def _other_chips(x, y):
    return [(1 - x, y), (x, 1 - y), (1 - x, 1 - y)]


def gather_devices(name, a):
    m_per, n = a.shape

    def body(x_ref, out_ref, send_sems, recv_sems, local_sem):
        x, y, c = _place()
        me, sibling = (x, y, c), (x, y, 1 - c)
        chips = _other_chips(x, y)

        def rows(px, py, pc):
            return out_ref.at[pl.ds((4 * px + 2 * py + pc) * m_per, m_per), :]

        def copy(k, block, to, src=None):
            return pltpu.make_async_remote_copy(
                src_ref=rows(*block) if src is None else src, dst_ref=rows(*block),
                send_sem=send_sems.at[k], recv_sem=recv_sems.at[k], device_id=to, device_id_type=MESH)

        mine = pltpu.make_async_copy(x_ref, rows(*me), local_sem)
        mine.start()
        first = [copy(0, me, sibling, src=x_ref)]
        first += [copy(1 + j, me, (*chip, c), src=x_ref) for j, chip in enumerate(chips)]
        for cp in first:
            cp.start()
        passed = [copy(4 + j, (*chip, c), sibling) for j, chip in enumerate(chips)]
        for j, chip in enumerate(chips):
            copy(1 + j, (*chip, c), me).wait_recv()
            passed[j].start()
        copy(0, sibling, me).wait_recv()
        for j, chip in enumerate(chips):
            copy(4 + j, (*chip, 1 - c), me).wait_recv()
        for cp in first + passed:
            cp.wait_send()
        mine.wait()

    out = pl.pallas_call(
        body, name=name,
        out_shape=jax.ShapeDtypeStruct((N_DEV * m_per, n), a.dtype),
        in_specs=[pl.BlockSpec(memory_space=pltpu.VMEM)],
        out_specs=pl.BlockSpec(memory_space=pltpu.VMEM),
        scratch_shapes=[pltpu.SemaphoreType.DMA((7,)), pltpu.SemaphoreType.DMA((7,)), pltpu.SemaphoreType.DMA],
        compiler_params=pltpu.CompilerParams(vmem_limit_bytes=VMEM_LIMIT),
    )(a)
    return out.reshape(N_DEV, m_per, n)


_ANY = pl.BlockSpec(memory_space=pl.ANY)


def _hbm_ref(a):
    return jax.new_ref(a, memory_space=pltpu.MemorySpace.HBM)


def _hbm_empty(shape, dtype):
    return jax.empty_ref(jax.ShapeDtypeStruct(shape, dtype), memory_space=pltpu.MemorySpace.HBM)


ID_PAIR, ID_CHIPS, ID_SHARE, ID_UKV = 8, 9, 10, 11


def _sequencer(name, collective_id, n_sem, peers_of, program):
    sems = pltpu.SemaphoreType.DMA((n_sem,))

    @pl.kernel(mesh=plsc.ScalarSubcoreMesh(axis_name="seq", num_cores=1), name=name, scratch_types=[sems, sems],
               compiler_params=pltpu.CompilerParams(collective_id=collective_id))
    def launch(send_sem, recv_sem):
        x, y, c = _place()
        peers = peers_of(x, y, c)
        barrier = pltpu.get_barrier_semaphore()
        for peer in peers:
            pl.semaphore_signal(barrier, inc=1, device_id=peer, device_id_type=MESH)
        pl.semaphore_wait(barrier, len(peers))
        program(x, y, c, send_sem, recv_sem)

    launch()


def gather_weights(name, stage, arrays):
    n = len(arrays)
    refs = [_hbm_ref(a) for a in arrays]

    def program(x, y, c, send_sem, recv_sem):
        me = 2 * x + y
        chips = _other_chips(x, y)

        def ici(t, r, half):
            cx, cy = chips[r]
            mine = refs[t].at[me, half]
            return pltpu.make_async_remote_copy(
                src_ref=mine, dst_ref=mine, send_sem=send_sem.at[3 * t + r], recv_sem=recv_sem.at[3 * t + r],
                device_id=(cx, cy, c), device_id_type=MESH)

        def d2d(t, r, half):
            cx, cy = chips[r]
            there = refs[t].at[2 * cx + cy, half]
            k = 3 * n + 3 * t + r
            return pltpu.make_async_remote_copy(
                src_ref=there, dst_ref=there, send_sem=send_sem.at[k], recv_sem=recv_sem.at[k],
                device_id=(x, y, 1 - c), device_id_type=MESH)

        for t in range(n):
            for r in range(3):
                ici(t, r, c).start()
        for t in range(n):
            for r in range(3):
                ici(t, r, c).wait_recv()
                d2d(t, r, c).start()
        for t in range(n):
            for r in range(3):
                d2d(t, r, 1 - c).wait_recv()
        for t in range(n):
            for r in range(3):
                ici(t, r, c).wait_send()
                d2d(t, r, c).wait_send()

    _sequencer(name, stage, 6 * n, lambda x, y, c: [(x, y, 1 - c)] + [(cx, cy, c) for cx, cy in _other_chips(x, y)],
               program)
    return [r[...] for r in refs]


def cast_into_slots(chip, shards):
    steps = 2

    def body(chip_ref, *refs):
        n = len(refs) // 2
        for src, dst in zip(refs[:n], refs[n:]):
            dst[...] = src[...].astype(BF16)

    def spec_in(s):
        return pl.BlockSpec((None, s.shape[1] // steps, s.shape[2]), lambda h, i, chip_ref: (h, i, 0))

    def spec_out(s):
        return pl.BlockSpec((None, None, s.shape[1] // steps, s.shape[2]), lambda h, i, chip_ref: (chip_ref[0], h, i, 0))

    return pl.pallas_call(
        body, name="cast_into_slots",
        grid_spec=pltpu.PrefetchScalarGridSpec(
            num_scalar_prefetch=1, grid=(2, steps),
            in_specs=[spec_in(s) for s in shards], out_specs=[spec_out(s) for s in shards]),
        out_shape=[jax.ShapeDtypeStruct((N_CHIP,) + s.shape, BF16) for s in shards],
        compiler_params=_params("parallel", "parallel"),
    )(chip, *shards)


def reduce_pair(name, grads):
    n = len(grads)
    src = [_hbm_ref(g) for g in grads]
    dst = [_hbm_empty((N_CHIP,) + g.shape[2:], g.dtype) for g in grads]

    def program(x, y, c, send_sem, recv_sem):
        cps = [pltpu.make_async_remote_copy(
            src_ref=src[t].at[:, 1 - c], dst_ref=dst[t], send_sem=send_sem.at[t], recv_sem=recv_sem.at[t],
            device_id=(x, y, 1 - c), device_id_type=MESH) for t in range(n)]
        for cp in cps:
            cp.start()
        for cp in cps:
            cp.wait()

    _sequencer(name, ID_PAIR, n, lambda x, y, c: [(x, y, 1 - c)], program)
    return [r[...] for r in src], [r[...] for r in dst]


def pair_add(name, core, g, got):
    _, _, R, C = g.shape

    def body(core_ref, g_ref, got_ref, o_ref):
        o_ref[...] = (g_ref[...] + got_ref[...]).astype(BF16)

    return pl.pallas_call(
        body, name=name,
        grid_spec=pltpu.PrefetchScalarGridSpec(
            num_scalar_prefetch=1, grid=(N_CHIP,),
            in_specs=[pl.BlockSpec((None, None, R, C), lambda q, core_ref: (q, core_ref[0], 0, 0)),
                      pl.BlockSpec((None, R, C), lambda q, core_ref: (q, 0, 0))],
            out_specs=pl.BlockSpec((None, R, C), lambda q, core_ref: (q, 0, 0))),
        out_shape=jax.ShapeDtypeStruct((N_CHIP, R, C), BF16),
        compiler_params=_params("parallel"),
    )(core, g, got)


def reduce_chips(name, sums):
    n = len(sums)
    src = [_hbm_ref(s) for s in sums]
    dst = [_hbm_empty((3,) + s.shape[1:], s.dtype) for s in sums]

    def program(x, y, c, send_sem, recv_sem):
        cps = []
        for t in range(n):
            for r, (cx, cy) in enumerate(_other_chips(x, y)):
                cps.append(pltpu.make_async_remote_copy(
                    src_ref=src[t].at[2 * cx + cy], dst_ref=dst[t].at[r],
                    send_sem=send_sem.at[3 * t + r], recv_sem=recv_sem.at[3 * t + r],
                    device_id=(cx, cy, c), device_id_type=MESH))
        for cp in cps:
            cp.start()
        for cp in cps:
            cp.wait()

    _sequencer(name, ID_CHIPS, 3 * n, lambda x, y, c: [(cx, cy, c) for cx, cy in _other_chips(x, y)], program)
    return [r[...] for r in src], [r[...] for r in dst]


def chip_add(name, place, s, got, k, n_slots, prev=None):
    _, R, C = s.shape

    def body(place_ref, s_ref, got_ref, *rest):
        o_ref = rest[-1]
        o_ref[...] = ((s_ref[...].astype(F32) + got_ref[0].astype(F32)) + got_ref[1].astype(F32)) + got_ref[2].astype(F32)

    in_specs = [pl.BlockSpec((None, R, C), lambda i, place_ref: (place_ref[0], 0, 0)),
                pl.BlockSpec((3, R, C), lambda i, place_ref: (0, 0, 0))]
    args = [place, s, got]
    aliases = {}
    if prev is not None:
        in_specs.append(_ANY)
        args.append(prev)
        aliases = {3: 0}
    return pl.pallas_call(
        body, name=name,
        grid_spec=pltpu.PrefetchScalarGridSpec(
            num_scalar_prefetch=1, grid=(1,), in_specs=in_specs,
            out_specs=pl.BlockSpec((None, None, R, C), lambda i, place_ref: (k, place_ref[1], 0, 0))),
        out_shape=jax.ShapeDtypeStruct((n_slots, 2, R, C), F32),
        input_output_aliases=aliases,
        compiler_params=_params("arbitrary"),
    )(*args)


def share_halves(name, stacks, slots):
    n = len(stacks)
    dst = [_hbm_ref(s) for s in stacks]

    def program(x, y, c, send_sem, recv_sem):
        cps = [pltpu.make_async_remote_copy(
            src_ref=dst[t].at[slots[t], c], dst_ref=dst[t].at[slots[t], c],
            send_sem=send_sem.at[t], recv_sem=recv_sem.at[t],
            device_id=(x, y, 1 - c), device_id_type=MESH) for t in range(n)]
        for cp in cps:
            cp.start()
        for cp in cps:
            cp.wait()

    _sequencer(name, ID_SHARE, n, lambda x, y, c: [(x, y, 1 - c)], program)
    return [r[...] for r in dst]


def gather_blocks(name, slotted):
    out = _hbm_ref(slotted)

    def program(x, y, c, send_sem, recv_sem):
        sibling = (x, y, 1 - c)
        chips = _other_chips(x, y)

        def copy(k, px, py, pc, to):
            block = out.at[4 * px + 2 * py + pc]
            return pltpu.make_async_remote_copy(src_ref=block, dst_ref=block, send_sem=send_sem.at[k],
                                                recv_sem=recv_sem.at[k], device_id=to, device_id_type=MESH)

        first = [copy(0, x, y, c, sibling)] + [copy(1 + j, x, y, c, (cx, cy, c)) for j, (cx, cy) in enumerate(chips)]
        for cp in first:
            cp.start()
        passed = [copy(4 + j, cx, cy, c, sibling) for j, (cx, cy) in enumerate(chips)]
        for j, (cx, cy) in enumerate(chips):
            copy(1 + j, cx, cy, c, (x, y, c)).wait_recv()
            passed[j].start()
        copy(0, x, y, 1 - c, (x, y, c)).wait_recv()
        for j, (cx, cy) in enumerate(chips):
            copy(4 + j, cx, cy, 1 - c, (x, y, c)).wait_recv()
        for cp in first + passed:
            cp.wait_send()

    _sequencer(name, ID_UKV, 7, lambda x, y, c: [(x, y, 1 - c)] + [(cx, cy, c) for cx, cy in _other_chips(x, y)],
               program)
    return out[...]


def place_block(name, dev, a):
    M, N = a.shape
    tr = min(M, 64)

    def body(dev_ref, a_ref, o_ref):
        o_ref[...] = a_ref[...]

    return pl.pallas_call(
        body, name=name,
        grid_spec=pltpu.PrefetchScalarGridSpec(
            num_scalar_prefetch=1, grid=(M // tr,),
            in_specs=[pl.BlockSpec((tr, N), lambda i, dev_ref: (i, 0))],
            out_specs=pl.BlockSpec((None, tr, N), lambda i, dev_ref: (dev_ref[0], i, 0))),
        out_shape=jax.ShapeDtypeStruct((N_DEV, M, N), a.dtype),
        compiler_params=_params("parallel"),
    )(dev, a)


def _swap_rope(a):
    return jnp.concatenate([a[..., ROPE // 2:], a[..., :ROPE // 2]], axis=-1)


def _rope_tables(S):
    inv = 1.0 / (ROPE_THETA ** (jnp.arange(0, ROPE, 2, dtype=F32) / ROPE))
    ang = jnp.arange(S, dtype=F32)[:, None] * inv[None, :]
    cos, sin = jnp.cos(ang), jnp.sin(ang)
    return (jnp.tile(jnp.concatenate([cos, cos], axis=1), (1, SLOTS)),
            jnp.tile(jnp.concatenate([-sin, sin], axis=1), (1, SLOTS)))


def _vec(norm_g, mod, i, k):
    rows = [norm_g[i, 2 * k], norm_g[i, 2 * k + 1], mod[i, 3 * k], mod[i, 3 * k + 1], mod[i, 3 * k + 2]]
    return jnp.concatenate([jnp.stack(rows), jnp.zeros((3, D), F32)], axis=0)


def _unpack_weights(full, w_uk, w_uv, q_norm, kv_norm):
    G = D // 4
    ffn_in = [[full[2 * i + k].reshape(N_CHIP, D, FSH) for k in range(2)] for i in range(2)]
    ffn_out = [[full[4 + 2 * i + k].reshape(2, FSH, D) for k in range(2)] for i in range(2)]
    pw = full[8].reshape(N_CHIP, 4, G // N_CHIP, G).transpose(1, 0, 2, 3).reshape(4, G, G)
    w_in = full[9].reshape(D, QL + KVL + ROPE)
    w_uq = full[10].reshape(QL, N_HEADS, NOPE + ROPE)
    wkr = w_in[:, QL + KVL:]
    wr = w_uq[:, :, NOPE:]
    eye2 = jnp.eye(2, dtype=BF16)
    uk_t = jnp.transpose(w_uk, (1, 2, 0)).reshape(N_PAIR, 2, NOPE, KVL)
    bduk = jnp.einsum("janc,ab->janbc", uk_t, eye2).reshape(N_PAIR, 2 * NOPE, 2 * KVL)
    uv = jnp.transpose(w_uv, (1, 0, 2)).reshape(N_PAIR, 2, KVL, VH)
    bduv = jnp.einsum("jacn,ab->jacbn", uv, eye2).reshape(N_PAIR, 2 * KVL, 2 * VH)
    mw = dict(wq=w_in[:, :QL], wkv=w_in[:, QL:QL + KVL], wkr4=jnp.tile(wkr, (1, SLOTS)),
              wkrs4=jnp.tile(_swap_rope(wkr), (1, SLOTS)), qn=q_norm, kvn=kv_norm,
              wn=w_uq[:, :, :NOPE].reshape(QL, NOPE_ALL), wr=wr.reshape(QL, ROPE_ALL),
              wrs=_swap_rope(wr).reshape(QL, ROPE_ALL), bduk=bduk)
    return ffn_in, ffn_out, pw, mw, bduv, full[11].reshape(D, D)


def _example_step(x, target, mod, norm_g, pvec, ffn_in, ffn_out, pw, mw, bduv, wo, reducer):
    S = x.shape[0]
    tabs = _rope_tables(S)
    vec = [[_vec(norm_g, mod, i, k) for k in range(3)] for i in range(2)]
    saved = {}
    for i in range(2):
        xin = x
        x, a, u, h = ffn_fwd(xin, vec[i][0], ffn_in[i][0], ffn_out[i][0], 0.5)
        saved[i, 0] = (xin, a, u, h)
        xin = x
        if i == 0:
            x, y, z = pool_fwd(xin, vec[i][1], pw, pvec)
            saved[i, 1] = (xin, y, z)
        else:
            h_m, cq_raw, ckv_raw, cqn, qnope, qcat, kcat, vcat = mla_pre(xin, vec[i][1], mw, tabs)
            olat, lse = attn_fwd(qcat, kcat, vcat)
            x, u_m, ocat = mla_post(olat, xin, vec[i][1], bduv, wo)
            saved[i, 1] = (xin, h_m, cq_raw, ckv_raw, cqn, qnope, qcat, kcat, olat, lse, u_m, ocat)
        xin = x
        x, a, u, h = ffn_fwd(xin, vec[i][2], ffn_in[i][1], ffn_out[i][1], 0.5)
        saved[i, 2] = (xin, a, u, h)
    loss, dx = loss_head(x, target)

    vg = {}
    G = D // 4

    def ffn_grads(i, k, dw_in, dw_out):
        return [(0, 2 * i + k, 4, dw_in.reshape(N_CHIP, 2, D // 2, FSH)),
                (1, 2 * i + k, 4, dw_out.reshape(N_CHIP, 2, DFF // 8, D))]

    for i in (1, 0):
        xin, a, u, h = saved[i, 2]
        dx, du, act, da, vg[i, 2] = ffn_bwd(dx, xin, u, a, vec[i][2], ffn_in[i][1], ffn_out[i][1], 0.5)
        reducer.advance()
        reducer.add(f"f{i}1", ffn_grads(i, 1, *ffn_dw(h, da, act, du)))
        if i == 0:
            xin, y, z = saved[i, 1]
            dx, vg[i, 1], pgrad, g_pool = pool_bwd(dx, xin, y, z, vec[i][1], pw, pvec)
            reducer.advance()
        else:
            xin, h_m, cq_raw, ckv_raw, cqn, qnope, qcat, kcat, olat, lse, u_m, ocat = saved[i, 1]
            du, docat, dolat, delta, vg_post = mla_post_bwd(dx, u_m, olat, vec[i][1], bduv, wo)
            reducer.advance()
            dq, dk, dv = attn_bwd(qcat, kcat, kcat.T, dolat, lse.reshape(N_HEADS, 1, S), delta.reshape(N_HEADS, 1, S))
            reducer.advance()
            dx, dlat, dql, dqcat, vg_pre, ngrad = mla_pre_bwd(
                dx, dq, dk, dv, xin, cq_raw, ckv_raw, vec[i][1], mw, tabs)
            vg[i, 1] = vg_post + vg_pre
            g = mla_dw(h_m, dlat, cqn, dqcat, dql, qnope, olat, docat, ocat, du)
            slots = lambda a: a.reshape(D, SLOTS, ROPE).sum(axis=1)
            g_kr = slots(g["in"][:, QL + KVL:QL + KVL + 128]) + _swap_rope(slots(g["in"][:, QL + KVL + 128:]))
            g_in = jnp.concatenate([g["in"][:, :QL + KVL], g_kr], axis=1)
            g_r = g["q"][:, NOPE_ALL:NOPE_ALL + ROPE_ALL].reshape(QL, N_HEADS, ROPE)
            g_rs = g["q"][:, NOPE_ALL + ROPE_ALL:].reshape(QL, N_HEADS, ROPE)
            g_uq = jnp.concatenate([g["q"][:, :NOPE_ALL].reshape(QL, N_HEADS, NOPE), g_r + _swap_rope(g_rs)], axis=-1)

            def heads(pairs):
                blk = pairs.reshape(N_PAIR, 2, KVL, 2, NOPE)
                per_head = jnp.stack([blk[:, 0, :, 0, :], blk[:, 1, :, 1, :]], axis=1).reshape(N_HEADS, KVL, NOPE)
                return jnp.transpose(per_head, (1, 0, 2)).reshape(KVL, N_HEADS * NOPE)

            reducer.add("mla", [(3, 0, 1, g_in.reshape(N_CHIP, 2, D // 8, QL + KVL + ROPE)),
                                (4, 0, 1, g_uq.reshape(N_CHIP, 2, QL // 8, N_HEADS * (NOPE + ROPE))),
                                (5, 0, 1, g["o"].reshape(N_CHIP, 2, D // 8, D))])
            reducer.add_replicated(jnp.concatenate([heads(g["uk"]), heads(g["uv"])], axis=0))
        xin, a, u, h = saved[i, 0]
        dx, du, act, da, vg[i, 0] = ffn_bwd(dx, xin, u, a, vec[i][0], ffn_in[i][0], ffn_out[i][0], 0.5)
        reducer.advance()
        grads = ffn_grads(i, 0, *ffn_dw(h, da, act, du))
        if i == 0:
            grads.append((2, 0, 1, g_pool.reshape(N_CHIP, 2, 2 * G // N_CHIP, G)))
        reducer.add(f"f{i}0", grads)
    return loss, dx, vg, pgrad, ngrad


class _GradReducer:
    def __init__(self, core, place, dev):
        self.core, self.place, self.dev = core, place, dev
        self.stacks = {}
        self.live = []
        self.replicated = None

    def add(self, tag, items):
        gen = self._run(tag, items)
        next(gen)
        self.live.append(gen)

    def add_replicated(self, block):
        self.replicated = gather_blocks("gather_ukv", place_block("place_ukv", self.dev, block))

    def advance(self):
        live = []
        for gen in self.live:
            try:
                next(gen)
                live.append(gen)
            except StopIteration:
                pass
        self.live = live

    def finish(self):
        while self.live:
            self.advance()
        return self.stacks, self.replicated

    def _run(self, tag, items):
        grads, from_pair = reduce_pair(f"reduce_pair_{tag}", [g for *_, g in items])
        yield
        sums = [pair_add(f"pair_add_{tag}_{j}", self.core, g, p) for j, (g, p) in enumerate(zip(grads, from_pair))]
        sums, from_chips = reduce_chips(f"reduce_chips_{tag}", sums)
        yield
        for j, ((o, k, n_slots, _), s, p) in enumerate(zip(items, sums, from_chips)):
            self.stacks[o] = chip_add(f"chip_add_{tag}_{j}", self.place, s, p, k, n_slots, self.stacks.get(o))
        shared = share_halves(f"share_halves_{tag}", [self.stacks[o] for o, *_ in items], [k for _, k, *_ in items])
        for (o, *_), v in zip(items, shared):
            self.stacks[o] = v


SMALL_IN = 8 * 640
SMALL_GRAD = 8 * 4224
SMALL_W = 8 * 2944


def _pack(parts, total):
    flat = jnp.concatenate([p.reshape(-1) for p in parts])
    return jnp.concatenate([flat, jnp.zeros((total - flat.shape[0],), F32)]).reshape(8, total // 8)


def kernel(x, c, ada_w, ada_b, norm_g, ffn_w_in, ffn_w_out, pool_w, pool_b, pool_scale, mla_w_in, mla_q_norm, mla_kv_norm, mla_w_uq, mla_w_uk, mla_w_uv, mla_w_o, loss_target, m_ada_w, m_ada_b, m_norm_g, m_ffn_w_in, m_ffn_w_out, m_pool_w, m_pool_b, m_pool_scale, m_mla_w_in, m_mla_q_norm, m_mla_kv_norm, m_mla_w_uq, m_mla_w_uk, m_mla_w_uv, m_mla_w_o, v_ada_w, v_ada_b, v_norm_g, v_ffn_w_in, v_ffn_w_out, v_pool_w, v_pool_b, v_pool_scale, v_mla_w_in, v_mla_q_norm, v_mla_kv_norm, v_mla_w_uq, v_mla_w_uk, v_mla_w_uv, v_mla_w_o):
    ix, iy, ic = _place()
    chip = 2 * ix + iy
    dev = 2 * chip + ic
    core_arr = ic.astype(jnp.int32).reshape(1)
    chip_arr = chip.astype(jnp.int32).reshape(1)
    S = x.shape[1]
    G = D // 4
    NG = D // N_CHIP

    def chip_cols(a, width, axis):
        return lax.dynamic_slice_in_dim(a, chip * width, width, axis)

    got = gather_devices("gather_small_in", _pack([c, norm_g, pool_b, mla_q_norm], SMALL_IN)).reshape(N_DEV, SMALL_IN)
    c_all = got[:, :D]
    parts = got[0::2]
    o = D
    norm_g_full = parts[:, o:o + 12 * NG].reshape(N_CHIP, 2, 6, NG).transpose(1, 2, 0, 3).reshape(2, 6, D)
    o += 12 * NG
    pool_b_full = parts[:, o:o + G].reshape(N_CHIP, 4, G // N_CHIP).transpose(1, 0, 2).reshape(1, D)
    o += G
    q_norm_full = parts[:, o:o + QL // N_CHIP].reshape(1, QL)
    pvec = jnp.concatenate([pool_b_full, pool_scale, jnp.zeros((6, D), F32)], axis=0)

    c_pad = jnp.concatenate([c_all, jnp.zeros((8, D), F32)], axis=0)
    mod_loc = mod_fwd(c_pad, ada_w, chip_cols(ada_b, MOD_COLS, 1).reshape(2, 1, MOD_COLS))
    got = gather_devices("gather_mod", mod_loc[:, :8].transpose(1, 0, 2).reshape(8, 2 * MOD_COLS))
    mine = lax.dynamic_index_in_dim(got[0::2].reshape(N_CHIP, 8, 2, MOD_COLS), dev, axis=1, keepdims=False)
    mod = mine.transpose(1, 0, 2).reshape(2, 9, D)

    bf = lambda a: a.astype(BF16)
    shards = [ffn_w_in[i, k].reshape(2, D // 2, FSH) for i in range(2) for k in range(2)]
    shards += [ffn_w_out[i, k].reshape(2, DFF // 8, D) for i in range(2) for k in range(2)]
    shards += [pool_w[0].reshape(2, 2 * G // N_CHIP, G), mla_w_in[0].reshape(2, D // 8, QL + KVL + ROPE),
               mla_w_uq[0].reshape(2, QL // 8, N_HEADS * (NOPE + ROPE)), mla_w_o[0].reshape(2, D // 8, D)]
    slotted = cast_into_slots(chip_arr, shards)
    full = [None] * len(slotted)
    stages = [(0, 4, 8), (1, 5), (2, 6), (9, 10, 11), (3, 7)]
    for stage, members in enumerate(stages):
        got_w = gather_weights(f"gather_weights_{stage}", stage, [slotted[t] for t in members])
        for t, a in zip(members, got_w):
            full[t] = a
    ffn_in, ffn_out, pw, mw, bduv, wo = _unpack_weights(full, bf(mla_w_uk[0]), bf(mla_w_uv[0]), q_norm_full,
                                                        mla_kv_norm)

    place_arr = jnp.stack([chip, ic]).astype(jnp.int32)
    reducer = _GradReducer(core_arr, place_arr, dev.astype(jnp.int32).reshape(1))
    loss_mine, grad_x, vg, pgrad, ngrad = _example_step(
        x[0], loss_target[0], mod, norm_g_full, pvec, ffn_in, ffn_out, pw, mw, bduv, wo, reducer)
    loss = lax.psum(loss_mine[0, 0], ("x", "y", "c"))
    stacks, ukv = reducer.finish()
    g_ffn_in, g_ffn_out, g_pool_w, g_mla_in, g_uq, g_wo = [stacks[o] for o in range(6)]
    g_ffn_in = g_ffn_in.reshape(ffn_w_in.shape)
    g_ffn_out = g_ffn_out.reshape(ffn_w_out.shape)
    g_pool_w = g_pool_w.reshape(pool_w.shape)
    g_mla_in = g_mla_in.reshape(mla_w_in.shape)
    g_uq = g_uq.reshape(mla_w_uq.shape)
    g_wo = g_wo.reshape(mla_w_o.shape)

    ukv = sum_devices("sum_ukv", ukv)
    g_uk = ukv[:KVL].reshape(mla_w_uk.shape)
    g_uv = ukv[KVL:].reshape(mla_w_uv.shape)

    dmod = jnp.stack([jnp.concatenate([vg[i, k][0:3] for k in range(3)]) for i in range(2)])
    dnorm = jnp.stack([jnp.concatenate([vg[i, k][3:5] for k in range(3)]) for i in range(2)])
    small = _pack([dmod, dnorm, pgrad[0], pgrad[1], ngrad[0], ngrad[1, :KVL]], SMALL_GRAD)
    got = gather_devices("gather_small_grad", small)
    tot = sum_devices("sum_small_grad", got).reshape(-1)
    n_mod = 2 * 9 * D
    g_ada_b = tot[:n_mod].reshape(ada_b.shape)
    o = n_mod
    g_norm = chip_cols(tot[o:o + 12 * D].reshape(2, 6, D), NG, 2)
    o += 12 * D
    g_pool_b = chip_cols(tot[o:o + D].reshape(1, 4, G), G // N_CHIP, 2)
    o += D
    g_pool_scale = tot[o:o + D].reshape(pool_scale.shape)
    o += D
    g_q_norm = chip_cols(tot[o:o + QL].reshape(1, QL), QL // N_CHIP, 1)
    o += QL
    g_kv_norm = tot[o:o + KVL].reshape(mla_kv_norm.shape)
    dmod_all = chip_cols(got.reshape(N_DEV, -1)[:, :n_mod].reshape(N_DEV, 2, 9 * D), MOD_COLS, 2)
    dmod_pad = jnp.concatenate([dmod_all.transpose(1, 0, 2), jnp.zeros((2, 8, MOD_COLS), F32)], axis=1)

    g_ada_w, d_ada_w, nm_ada_w, nv_ada_w = adamw_ada(c_pad, dmod_pad, ada_w, m_ada_w, v_ada_w)
    small_names = ["ada_b", "norm_g", "pool_b", "pool_scale", "mla_q_norm", "mla_kv_norm"]
    small_w = [ada_b, norm_g, pool_b, pool_scale, mla_q_norm, mla_kv_norm]
    small_g = [g_ada_b, g_norm, g_pool_b, g_pool_scale, g_q_norm, g_kv_norm]
    small_m = [m_ada_b, m_norm_g, m_pool_b, m_pool_scale, m_mla_q_norm, m_mla_kv_norm]
    small_v = [v_ada_b, v_norm_g, v_pool_b, v_pool_scale, v_mla_q_norm, v_mla_kv_norm]
    packed = adamw("adamw_small", *[_pack(p, SMALL_W) for p in (small_w, small_g, small_m, small_v)])
    upd = {}
    o = 0
    for name, w in zip(small_names, small_w):
        upd[name] = [p.reshape(-1)[o:o + w.size].reshape(w.shape) for p in packed]
        o += w.size
    big = [("ffn_w_in", ffn_w_in, g_ffn_in, m_ffn_w_in, v_ffn_w_in),
           ("ffn_w_out", ffn_w_out, g_ffn_out, m_ffn_w_out, v_ffn_w_out),
           ("pool_w", pool_w, g_pool_w, m_pool_w, v_pool_w),
           ("mla_w_in", mla_w_in, g_mla_in, m_mla_w_in, v_mla_w_in),
           ("mla_w_uq", mla_w_uq, g_uq, m_mla_w_uq, v_mla_w_uq),
           ("mla_w_uk", mla_w_uk, g_uk, m_mla_w_uk, v_mla_w_uk),
           ("mla_w_uv", mla_w_uv, g_uv, m_mla_w_uv, v_mla_w_uv),
           ("mla_w_o", mla_w_o, g_wo, m_mla_w_o, v_mla_w_o)]
    for name, w, g, m, v in big:
        upd[name] = adamw("adamw_" + name, w, g, m, v)
    upd["ada_w"] = [d_ada_w, nm_ada_w, nv_ada_w]

    order = ["ada_w", "ada_b", "norm_g", "ffn_w_in", "ffn_w_out", "pool_w", "pool_b", "pool_scale", "mla_w_in",
             "mla_q_norm", "mla_kv_norm", "mla_w_uq", "mla_w_uk", "mla_w_uv", "mla_w_o"]
    grad = dict(ada_w=g_ada_w, ada_b=g_ada_b, norm_g=g_norm, ffn_w_in=g_ffn_in, ffn_w_out=g_ffn_out, pool_w=g_pool_w,
                pool_b=g_pool_b, pool_scale=g_pool_scale, mla_w_in=g_mla_in, mla_q_norm=g_q_norm,
                mla_kv_norm=g_kv_norm, mla_w_uq=g_uq, mla_w_uk=g_uk, mla_w_uv=g_uv, mla_w_o=g_wo)
    return (loss, grad_x[None], *[grad[n] for n in order], *[upd[n][0] for n in order],
            *[upd[n][1] for n in order], *[upd[n][2] for n in order])
```

```python
import functools

import jax
import jax.numpy as jnp
from jax import lax
from jax.experimental import pallas as pl
from jax.experimental.pallas import tpu as pltpu
from jax.experimental.pallas import tpu_sc as plsc

F32 = jnp.float32
BF16 = jnp.bfloat16

D = 1024
DFF = 2816
FSH = 1408
N_CHIP = 4
N_DEV = 8
N_HEADS = 16
NOPE = 64
ROPE = 32
VH = 64
QL = 256
KVL = 128
QPAD = 256
EPS = 1e-6
ATTN_SCALE = (NOPE + ROPE) ** -0.5
ROPE_THETA = 10000.0
POOL_WINDOWS = (2, 4, 8, 16)
HALO = 8
ATTN_TQ = 1024
ATTN_KC = 512

ADAM_LR, ADAM_B1, ADAM_B2, ADAM_EPS, ADAM_WD, ADAM_STEP = 0.001, 0.9, 0.999, 1e-08, 0.01, 10

VMEM_LIMIT = 60 * 1024 * 1024
MESH = pl.DeviceIdType.MESH

NT = (((1,), (1,)), ((), ()))
TN = (((0,), (0,)), ((), ()))


def _params(*sem):
    return pltpu.CompilerParams(dimension_semantics=sem, vmem_limit_bytes=VMEM_LIMIT)


def _dot(a, b, dims=None):
    if dims is None:
        return jnp.dot(a, b, preferred_element_type=F32)
    return lax.dot_general(a, b, dims, preferred_element_type=F32)


def _rms(x):
    r = lax.rsqrt(jnp.mean(x * x, axis=-1, keepdims=True) + EPS)
    return x * r, r


def _rms_bwd(xhat, r, dxhat):
    return r * (dxhat - xhat * jnp.mean(dxhat * xhat, axis=-1, keepdims=True))


def _as_row(col):
    return jnp.broadcast_to(col, (col.shape[0], 128)).T[0:1, :]


def _prenorm(x, vec_ref):
    xhat, r = _rms(x)
    h = xhat * vec_ref[0:1, :] * (1.0 + vec_ref[3:4, :]) + vec_ref[2:3, :]
    return h, xhat, r


def _postnorm_bwd(dout, u, vec_ref, weight):
    uhat, r = _rms(u)
    gt = weight * (1.0 + vec_ref[4:5, :])
    dy = dout * gt
    dgate_rows = (weight * dout) * (uhat * vec_ref[1:2, :])
    dgpost_rows = dy * uhat
    du = _rms_bwd(uhat, r, dy * vec_ref[1:2, :])
    return du, dgate_rows, dgpost_rows


def _prenorm_bwd(dh, x, vec_ref, vg_ref):
    xhat, r = _rms(x)
    sc1 = 1.0 + vec_ref[3:4, :]
    g = vec_ref[0:1, :]
    vg_ref[0:1, :] += jnp.sum(dh, axis=0, keepdims=True)
    vg_ref[1:2, :] += jnp.sum(dh * (xhat * g), axis=0, keepdims=True)
    vg_ref[3:4, :] += jnp.sum(dh * sc1 * xhat, axis=0, keepdims=True)
    return _rms_bwd(xhat, r, dh * g * sc1)


def ffn_fwd(x, vec, w_in, w_out, weight):
    S = x.shape[0]
    tm = min(512, S)

    def body(x_ref, vec_ref, wg_ref, wu_ref, wo_ref, xo_ref, a_ref, u_ref, h_ref, acc_ref):
        j = pl.program_id(1)

        @pl.when(j == 0)
        def _():
            h, _, _ = _prenorm(x_ref[...], vec_ref)
            h_ref[...] = h.astype(BF16)
            acc_ref[...] = jnp.zeros_like(acc_ref)

        hb = h_ref[...]
        g = _dot(hb, wg_ref[...])
        up = _dot(hb, wu_ref[...])
        a_ref[0] = g.astype(BF16)
        a_ref[1] = up.astype(BF16)
        act = (g * jax.nn.sigmoid(g)) * up
        acc_ref[...] += _dot(act.astype(BF16), wo_ref[...])

        @pl.when(j == 1)
        def _():
            u = acc_ref[...]
            u_ref[...] = u
            uhat, _ = _rms(u)
            xo_ref[...] = x_ref[...] + (weight * (1.0 + vec_ref[4:5, :])) * (uhat * vec_ref[1:2, :])

    return pl.pallas_call(
        body, name="ffn_fwd", grid=(S // tm, 2),
        in_specs=[pl.BlockSpec((tm, D), lambda i, j: (i, 0)),
                  pl.BlockSpec((8, D), lambda i, j: (0, 0)),
                  pl.BlockSpec((None, D, FSH), lambda i, j: (j, 0, 0)),
                  pl.BlockSpec((None, D, FSH), lambda i, j: (j + 2, 0, 0)),
                  pl.BlockSpec((None, FSH, D), lambda i, j: (j, 0, 0))],
        out_specs=[pl.BlockSpec((tm, D), lambda i, j: (i, 0)),
                   pl.BlockSpec((2, tm, FSH), lambda i, j: (0, i, j)),
                   pl.BlockSpec((tm, D), lambda i, j: (i, 0)),
                   pl.BlockSpec((tm, D), lambda i, j: (i, 0))],
        out_shape=[jax.ShapeDtypeStruct((S, D), F32), jax.ShapeDtypeStruct((2, S, DFF), BF16),
                   jax.ShapeDtypeStruct((S, D), F32), jax.ShapeDtypeStruct((S, D), BF16)],
        scratch_shapes=[pltpu.VMEM((tm, D), F32)],
        compiler_params=_params("parallel", "arbitrary"),
    )(x, vec, w_in, w_in, w_out)


def ffn_bwd(dout, x, u, a, vec, w_in, w_out, weight):
    S = x.shape[0]
    tm = min(256, S)

    def body(do_ref, x_ref, u_ref, a_ref, vec_ref, wg_ref, wu_ref, wo_ref,
             dx_ref, du_ref, act_ref, da_ref, vg_ref, dh_ref):
        i, j = pl.program_id(0), pl.program_id(1)

        @pl.when((i == 0) & (j == 0))
        def _():
            vg_ref[...] = jnp.zeros_like(vg_ref)

        @pl.when(j == 0)
        def _():
            du, dgate_rows, dgpost_rows = _postnorm_bwd(do_ref[...], u_ref[...], vec_ref, weight)
            vg_ref[2:3, :] += jnp.sum(dgate_rows, axis=0, keepdims=True)
            vg_ref[4:5, :] += jnp.sum(dgpost_rows, axis=0, keepdims=True)
            du_ref[...] = du.astype(BF16)
            dh_ref[...] = jnp.zeros_like(dh_ref)

        dact = _dot(du_ref[...], wo_ref[...], NT)
        g = a_ref[0].astype(F32)
        up = a_ref[1].astype(F32)
        s = jax.nn.sigmoid(g)
        silu = g * s
        act_ref[...] = (silu * up).astype(BF16)
        dg = (dact * up * (s * (1.0 + g * (1.0 - s)))).astype(BF16)
        dup = (dact * silu).astype(BF16)
        da_ref[0] = dg
        da_ref[1] = dup
        dh_ref[...] += _dot(dg, wg_ref[...], NT) + _dot(dup, wu_ref[...], NT)

        @pl.when(j == 1)
        def _():
            dx_ref[...] = do_ref[...] + _prenorm_bwd(dh_ref[...], x_ref[...], vec_ref, vg_ref)

    row = lambda i, j: (i, 0)
    return pl.pallas_call(
        body, name="ffn_bwd", grid=(S // tm, 2),
        in_specs=[pl.BlockSpec((tm, D), row), pl.BlockSpec((tm, D), row), pl.BlockSpec((tm, D), row),
                  pl.BlockSpec((2, tm, FSH), lambda i, j: (0, i, j)),
                  pl.BlockSpec((8, D), lambda i, j: (0, 0)),
                  pl.BlockSpec((None, D, FSH), lambda i, j: (j, 0, 0)),
                  pl.BlockSpec((None, D, FSH), lambda i, j: (j + 2, 0, 0)),
                  pl.BlockSpec((None, FSH, D), lambda i, j: (j, 0, 0))],
        out_specs=[pl.BlockSpec((tm, D), row), pl.BlockSpec((tm, D), row),
                   pl.BlockSpec((tm, FSH), lambda i, j: (i, j)),
                   pl.BlockSpec((2, tm, FSH), lambda i, j: (0, i, j)),
                   pl.BlockSpec((8, D), lambda i, j: (0, 0))],
        out_shape=[jax.ShapeDtypeStruct((S, D), F32), jax.ShapeDtypeStruct((S, D), BF16),
                   jax.ShapeDtypeStruct((S, DFF), BF16), jax.ShapeDtypeStruct((2, S, DFF), BF16),
                   jax.ShapeDtypeStruct((8, D), F32)],
        scratch_shapes=[pltpu.VMEM((tm, D), F32)],
        compiler_params=_params("arbitrary", "arbitrary"),
    )(dout, x, u, a, vec, w_in, w_in, w_out)


def dw_matmul(name, a, b, a_spec, b_spec, out_shape, out_spec, grid):
    def body(a_ref, b_ref, o_ref):
        @pl.when(pl.program_id(len(grid) - 1) == 0)
        def _():
            o_ref[...] = jnp.zeros_like(o_ref)

        o_ref[...] += _dot(a_ref[...], b_ref[...], TN)

    return pl.pallas_call(
        body, name=name, grid=grid, in_specs=[a_spec, b_spec], out_specs=out_spec,
        out_shape=jax.ShapeDtypeStruct(out_shape, F32),
        compiler_params=_params(*(["parallel"] * (len(grid) - 1) + ["arbitrary"])),
    )(a, b)


def ffn_dw(h, da, act, du):
    S = h.shape[0]
    tk = min(512, S)
    dw_in = dw_matmul("ffn_dw_in", h, da,
                      pl.BlockSpec((tk, D), lambda n, k: (k, 0)),
                      pl.BlockSpec((None, tk, FSH), lambda n, k: (n // 2, k, n % 2)),
                      (N_CHIP, D, FSH), pl.BlockSpec((None, D, FSH), lambda n, k: (n, 0, 0)),
                      (N_CHIP, S // tk))
    dw_out = dw_matmul("ffn_dw_out", act, du,
                       pl.BlockSpec((tk, FSH), lambda n, k: (k, n)),
                       pl.BlockSpec((tk, D), lambda n, k: (k, 0)),
                       (DFF, D), pl.BlockSpec((FSH, D), lambda n, k: (n, 0)),
                       (2, S // tk))
    return dw_in, dw_out


def _halo_specs(tm, S):
    nb = tm // HALO
    last = S // HALO - 1
    return [pl.BlockSpec((HALO, D), lambda i: (jnp.maximum(i * nb - 1, 0), 0)),
            pl.BlockSpec((tm, D), lambda i: (i, 0)),
            pl.BlockSpec((HALO, D), lambda i: (jnp.minimum((i + 1) * nb, last), 0))]


def _shift_rows(v, k):
    return pltpu.roll(v, k % v.shape[0], 0)


def _window_sum(v, g, forward):
    acc = v + _shift_rows(v, 1 if forward else -1)
    for step in (1, 2, 4)[:g]:
        acc = _shift_rows(acc, step) + _shift_rows(acc, -step)
    return acc


def _pool_count(t, w, S):
    return jnp.maximum(jnp.minimum(t + w // 2, S) - jnp.maximum(t - w // 2, 0), 1).astype(F32)


def pool_fwd(x, vec, pw, pvec):
    S = x.shape[0]
    tm = min(256, S)
    G = D // 4

    def body(xp_ref, x_ref, xn_ref, vec_ref, pw_ref, pv_ref, xo_ref, y_ref, z_ref):
        i = pl.program_id(0)
        xa = jnp.concatenate([xp_ref[...], x_ref[...], xn_ref[...]], axis=0)
        t = i * tm - HALO + lax.broadcasted_iota(jnp.int32, (tm + 2 * HALO, 1), 0)
        h, _, _ = _prenorm(xa, vec_ref)
        h = jnp.where((t >= 0) & (t < S), h, 0.0)
        tmain = t[HALO:HALO + tm]
        for g in range(4):
            hg = h[:, g * G:(g + 1) * G]
            pooled = _window_sum(hg, g, True)[HALO:HALO + tm] / _pool_count(tmain, POOL_WINDOWS[g], S)
            z = (pooled - hg[HALO:HALO + tm]).astype(BF16)
            z_ref[:, g * G:(g + 1) * G] = z
            y_ref[:, g * G:(g + 1) * G] = _dot(z, pw_ref[g]) + pv_ref[0:1, g * G:(g + 1) * G]
        u = y_ref[...] * pv_ref[1:2, :]
        uhat, _ = _rms(u)
        xo_ref[...] = x_ref[...] + (1.0 + vec_ref[4:5, :]) * (uhat * vec_ref[1:2, :])

    row = lambda i: (i, 0)
    full = lambda i: (0, 0)
    return pl.pallas_call(
        body, name="pool_fwd", grid=(S // tm,),
        in_specs=_halo_specs(tm, S) + [pl.BlockSpec((8, D), full), pl.BlockSpec((4, G, G), lambda i: (0, 0, 0)),
                                       pl.BlockSpec((8, D), full)],
        out_specs=[pl.BlockSpec((tm, D), row)] * 3,
        out_shape=[jax.ShapeDtypeStruct((S, D), F32), jax.ShapeDtypeStruct((S, D), F32),
                   jax.ShapeDtypeStruct((S, D), BF16)],
        compiler_params=_params("parallel"),
    )(x, x, x, vec, pw, pvec)


def pool_bwd(dout, x, y, z, vec, pw, pvec):
    S = x.shape[0]
    tm = min(256, S)
    G = D // 4
    R = G // N_CHIP

    def body(dop_ref, do_ref, don_ref, yp_ref, y_ref, yn_ref, x_ref, z_ref, vec_ref, pw_ref, pv_ref,
             dx_ref, vg_ref, pg_ref, dw_ref, dh_ref):
        i = pl.program_id(0)

        @pl.when(i == 0)
        def _():
            vg_ref[...] = jnp.zeros_like(vg_ref)
            pg_ref[...] = jnp.zeros_like(pg_ref)
            dw_ref[...] = jnp.zeros_like(dw_ref)

        doa = jnp.concatenate([dop_ref[...], do_ref[...], don_ref[...]], axis=0)
        ya = jnp.concatenate([yp_ref[...], y_ref[...], yn_ref[...]], axis=0)
        t = i * tm - HALO + lax.broadcasted_iota(jnp.int32, (tm + 2 * HALO, 1), 0)
        inside = (t >= 0) & (t < S)
        main = (t >= i * tm) & (t < (i + 1) * tm)
        du, dgate_rows, dgpost_rows = _postnorm_bwd(doa, ya * pv_ref[1:2, :], vec_ref, 1.0)
        du = jnp.where(inside, du, 0.0)
        vg_ref[2:3, :] += jnp.sum(jnp.where(main, dgate_rows, 0.0), axis=0, keepdims=True)
        vg_ref[4:5, :] += jnp.sum(jnp.where(main, dgpost_rows, 0.0), axis=0, keepdims=True)
        dy = du * pv_ref[1:2, :]
        pg_ref[0:1, :] += jnp.sum(jnp.where(main, dy, 0.0), axis=0, keepdims=True)
        pg_ref[1:2, :] += jnp.sum(jnp.where(main, du * ya, 0.0), axis=0, keepdims=True)
        for g in range(4):
            dyg = dy[:, g * G:(g + 1) * G].astype(BF16)
            dz = _dot(dyg, pw_ref[g], NT)
            e = dz / _pool_count(t, POOL_WINDOWS[g], S)
            dh_ref[:, g * G:(g + 1) * G] = (_window_sum(e, g, False) - dz)[HALO:HALO + tm]
            dwg = _dot(z_ref[:, g * G:(g + 1) * G], dyg[HALO:HALO + tm], TN)
            for q in range(N_CHIP):
                dw_ref[q, g] += dwg[q * R:(q + 1) * R, :]
        dx_ref[...] = do_ref[...] + _prenorm_bwd(dh_ref[...], x_ref[...], vec_ref, vg_ref)

    row = lambda i: (i, 0)
    full = lambda i: (0, 0)
    halo = _halo_specs(tm, S)
    return pl.pallas_call(
        body, name="pool_bwd", grid=(S // tm,),
        in_specs=halo + halo + [pl.BlockSpec((tm, D), row), pl.BlockSpec((tm, D), row), pl.BlockSpec((8, D), full),
                                pl.BlockSpec((4, G, G), lambda i: (0, 0, 0)), pl.BlockSpec((8, D), full)],
        out_specs=[pl.BlockSpec((tm, D), row), pl.BlockSpec((8, D), full), pl.BlockSpec((8, D), full),
                   pl.BlockSpec((N_CHIP, 4, R, G), lambda i: (0, 0, 0, 0))],
        out_shape=[jax.ShapeDtypeStruct((S, D), F32), jax.ShapeDtypeStruct((8, D), F32),
                   jax.ShapeDtypeStruct((8, D), F32), jax.ShapeDtypeStruct((N_CHIP, 4, R, G), F32)],
        scratch_shapes=[pltpu.VMEM((tm, D), F32)],
        compiler_params=_params("arbitrary"),
    )(dout, dout, dout, y, y, y, x, z, vec, pw, pvec)


N_PAIR = N_HEADS // 2
SLOTS = 128 // ROPE
ROPE_ALL = N_HEADS * ROPE
NOPE_ALL = N_HEADS * NOPE
LAT_ALL = N_HEADS * KVL
DLAT = QL + KVL + 2 * 128
DQ_ALL = NOPE_ALL + 2 * ROPE_ALL


def _w3(shape):
    return pl.BlockSpec(shape, lambda i: (0,) * len(shape))


def _slot_mask(hd, rows):
    lane = lax.broadcasted_iota(jnp.int32, (rows, 128), 1)
    return (lane // ROPE) == (hd % SLOTS)


MLA_WEIGHTS = ("wq", "wkv", "wkr4", "wkrs4", "qn", "kvn", "wn", "wr", "wrs", "bduk")


def _mla_weight_specs():
    return [_w3((D, QL)), _w3((D, KVL)), _w3((D, 128)), _w3((D, 128)), _w3((1, QL)), _w3((1, KVL)),
            _w3((QL, NOPE_ALL)), _w3((QL, ROPE_ALL)), _w3((QL, ROPE_ALL)), _w3((N_PAIR, 2 * NOPE, 2 * KVL))]


def mla_pre(x, vec, mw, tabs):
    S = x.shape[0]
    tm = min(256, S)

    def body(x_ref, vec_ref, cos_ref, sin_ref, wq_ref, wkv_ref, wkr_ref, wkrs_ref, qn_ref, kvn_ref,
             wn_ref, wr_ref, wrs_ref, bduk_ref,
             h_ref, cq_ref, ckv_ref, cqn_ref, qnope_ref, qcat_ref, kcat_ref, vcat_ref):
        h, _, _ = _prenorm(x_ref[...], vec_ref)
        hb = h.astype(BF16)
        h_ref[...] = hb
        cq_raw = _dot(hb, wq_ref[...])
        ckv_raw = _dot(hb, wkv_ref[...])
        cq_ref[...] = cq_raw
        ckv_ref[...] = ckv_raw
        cos, sin = cos_ref[...], sin_ref[...]
        ckv = (_rms(ckv_raw)[0] * kvn_ref[...]).astype(BF16)
        kcat_ref[:, 0:KVL] = ckv
        kcat_ref[:, KVL:] = (_dot(hb, wkr_ref[...]) * cos + _dot(hb, wkrs_ref[...]) * sin).astype(BF16)
        vcat_ref[:, 0:KVL] = ckv
        ones = lax.broadcasted_iota(jnp.int32, (tm, QPAD - KVL), 1) == 0
        vcat_ref[:, KVL:] = jnp.where(ones, 1.0, 0.0).astype(BF16)
        cqb = (_rms(cq_raw)[0] * qn_ref[...]).astype(BF16)
        cqn_ref[...] = cqb
        qn = _dot(cqb, wn_ref[...]).astype(BF16)
        qnope_ref[...] = qn
        cos4, sin4 = jnp.tile(cos, (1, SLOTS)), jnp.tile(sin, (1, SLOTS))
        qr = ((_dot(cqb, wr_ref[...]) * cos4 + _dot(cqb, wrs_ref[...]) * sin4) * ATTN_SCALE).astype(BF16)
        for j in range(N_PAIR):
            ql = (_dot(qn[:, 128 * j:128 * (j + 1)], bduk_ref[j]) * ATTN_SCALE).astype(BF16)
            for hd in (2 * j, 2 * j + 1):
                qcat_ref[hd, :, 0:KVL] = ql[:, KVL * (hd - 2 * j):KVL * (hd - 2 * j + 1)]
                group = qr[:, 128 * (hd // SLOTS):128 * (hd // SLOTS + 1)]
                qcat_ref[hd, :, KVL:] = jnp.where(_slot_mask(hd, tm), group, jnp.zeros_like(group))

    row = lambda i: (i, 0)
    hrow = lambda i: (0, i, 0)
    return pl.pallas_call(
        body, name="mla_pre", grid=(S // tm,),
        in_specs=[pl.BlockSpec((tm, D), row), _w3((8, D)), pl.BlockSpec((tm, 128), row), pl.BlockSpec((tm, 128), row)]
        + _mla_weight_specs(),
        out_specs=[pl.BlockSpec((tm, D), row), pl.BlockSpec((tm, QL), row), pl.BlockSpec((tm, KVL), row),
                   pl.BlockSpec((tm, QL), row), pl.BlockSpec((tm, NOPE_ALL), row),
                   pl.BlockSpec((N_HEADS, tm, QPAD), hrow), pl.BlockSpec((tm, QPAD), row),
                   pl.BlockSpec((tm, QPAD), row)],
        out_shape=[jax.ShapeDtypeStruct((S, D), BF16), jax.ShapeDtypeStruct((S, QL), F32),
                   jax.ShapeDtypeStruct((S, KVL), F32), jax.ShapeDtypeStruct((S, QL), BF16),
                   jax.ShapeDtypeStruct((S, NOPE_ALL), BF16), jax.ShapeDtypeStruct((N_HEADS, S, QPAD), BF16),
                   jax.ShapeDtypeStruct((S, QPAD), BF16), jax.ShapeDtypeStruct((S, QPAD), BF16)],
        compiler_params=_params("parallel"),
    )(x, vec, tabs[0], tabs[1], *[mw[k] for k in MLA_WEIGHTS])


def attn_fwd(qcat, kcat, vcat):
    S = kcat.shape[0]
    tq = min(ATTN_TQ, S)
    kc = min(ATTN_KC, S)

    def body(q_ref, k_ref, v_ref, o_ref, lse_ref):
        q = q_ref[...]
        m = jnp.full((tq, 1), -jnp.inf, F32)
        ov = jnp.zeros((tq, QPAD), F32)
        for c in range(S // kc):
            s = _dot(q, k_ref[c * kc:(c + 1) * kc, :], NT)
            m_new = jnp.maximum(m, jnp.max(s, axis=-1, keepdims=True))
            p = jnp.exp(s - m_new).astype(BF16)
            ov = ov * jnp.exp(m - m_new) + _dot(p, v_ref[c * kc:(c + 1) * kc, :])
            m = m_new
        l = ov[:, KVL:KVL + 1]
        o_ref[...] = (ov[:, 0:KVL] * (1.0 / l)).astype(BF16)
        lse_ref[...] = _as_row(m + jnp.log(l))

    return pl.pallas_call(
        body, name="attn_fwd", grid=(N_HEADS, S // tq),
        in_specs=[pl.BlockSpec((None, tq, QPAD), lambda h, i: (h, i, 0)),
                  pl.BlockSpec((S, QPAD), lambda h, i: (0, 0)),
                  pl.BlockSpec((S, QPAD), lambda h, i: (0, 0))],
        out_specs=[pl.BlockSpec((tq, KVL), lambda h, i: (i, h)),
                   pl.BlockSpec((None, 1, tq), lambda h, i: (h, 0, i))],
        out_shape=[jax.ShapeDtypeStruct((S, LAT_ALL), BF16), jax.ShapeDtypeStruct((N_HEADS, 1, S), F32)],
        compiler_params=_params("parallel", "parallel"),
    )(qcat, kcat, vcat)


def mla_post(olat, x, vec, bduv, wo):
    S = x.shape[0]
    tm = min(256, S)

    def body(o_ref, x_ref, vec_ref, bduv_ref, wo_ref, xo_ref, u_ref, ocat_ref):
        for j in range(N_PAIR):
            oc = _dot(o_ref[:, 2 * KVL * j:2 * KVL * (j + 1)], bduv_ref[j])
            ocat_ref[:, 2 * VH * j:2 * VH * (j + 1)] = oc.astype(BF16)
        u = _dot(ocat_ref[...], wo_ref[...])
        u_ref[...] = u
        uhat, _ = _rms(u)
        xo_ref[...] = x_ref[...] + (1.0 + vec_ref[4:5, :]) * (uhat * vec_ref[1:2, :])

    row = lambda i: (i, 0)
    return pl.pallas_call(
        body, name="mla_post", grid=(S // tm,),
        in_specs=[pl.BlockSpec((tm, LAT_ALL), row), pl.BlockSpec((tm, D), row), _w3((8, D)),
                  _w3((N_PAIR, 2 * KVL, 2 * VH)), _w3((D, D))],
        out_specs=[pl.BlockSpec((tm, D), row), pl.BlockSpec((tm, D), row), pl.BlockSpec((tm, D), row)],
        out_shape=[jax.ShapeDtypeStruct((S, D), F32), jax.ShapeDtypeStruct((S, D), F32),
                   jax.ShapeDtypeStruct((S, D), BF16)],
        compiler_params=_params("parallel"),
    )(olat, x, vec, bduv, wo)


def mla_post_bwd(dout, u, olat, vec, bduv, wo):
    S = u.shape[0]
    tm = min(256, S)

    def body(do_ref, u_ref, o_ref, vec_ref, bduv_ref, wo_ref, du_ref, docat_ref, dolat_ref, delta_ref, vg_ref):
        @pl.when(pl.program_id(0) == 0)
        def _():
            vg_ref[...] = jnp.zeros_like(vg_ref)

        du, dgate_rows, dgpost_rows = _postnorm_bwd(do_ref[...], u_ref[...], vec_ref, 1.0)
        vg_ref[2:3, :] += jnp.sum(dgate_rows, axis=0, keepdims=True)
        vg_ref[4:5, :] += jnp.sum(dgpost_rows, axis=0, keepdims=True)
        dub = du.astype(BF16)
        du_ref[...] = dub
        docat_ref[...] = _dot(dub, wo_ref[...], NT).astype(BF16)
        for j in range(N_PAIR):
            dol = _dot(docat_ref[:, 2 * VH * j:2 * VH * (j + 1)], bduv_ref[j], NT).astype(BF16)
            dolat_ref[:, 2 * KVL * j:2 * KVL * (j + 1)] = dol
            prod = dol.astype(F32) * o_ref[:, 2 * KVL * j:2 * KVL * (j + 1)].astype(F32)
            delta_ref[2 * j] = _as_row(jnp.sum(prod[:, 0:KVL], axis=-1, keepdims=True))
            delta_ref[2 * j + 1] = _as_row(jnp.sum(prod[:, KVL:], axis=-1, keepdims=True))

    row = lambda i: (i, 0)
    hrow = lambda i: (0, i, 0)
    return pl.pallas_call(
        body, name="mla_post_bwd", grid=(S // tm,),
        in_specs=[pl.BlockSpec((tm, D), row), pl.BlockSpec((tm, D), row), pl.BlockSpec((tm, LAT_ALL), row),
                  _w3((8, D)), _w3((N_PAIR, 2 * KVL, 2 * VH)), _w3((D, D))],
        out_specs=[pl.BlockSpec((tm, D), row), pl.BlockSpec((tm, D), row),
                   pl.BlockSpec((tm, LAT_ALL), row), pl.BlockSpec((N_HEADS, 1, tm), lambda i: (0, 0, i)), _w3((8, D))],
        out_shape=[jax.ShapeDtypeStruct((S, D), BF16), jax.ShapeDtypeStruct((S, D), BF16),
                   jax.ShapeDtypeStruct((S, LAT_ALL), BF16), jax.ShapeDtypeStruct((N_HEADS, 1, S), F32),
                   jax.ShapeDtypeStruct((8, D), F32)],
        compiler_params=_params("arbitrary"),
    )(dout, u, olat, vec, bduv, wo)


def attn_bwd(qcat, kcat, kcat_t, dolat, lse_row, delta_row):
    S = kcat.shape[0]
    tq = min(ATTN_TQ, S)
    kc = min(ATTN_KC, S)

    def body(q_ref, k_ref, kt_ref, do_ref, lse_ref, dl_ref, dq_ref, dk_ref, dv_ref):
        @pl.when((pl.program_id(0) == 0) & (pl.program_id(1) == 0))
        def _():
            dk_ref[...] = jnp.zeros_like(dk_ref)
            dv_ref[...] = jnp.zeros_like(dv_ref)

        q, do = q_ref[...], do_ref[...]
        lse, dl = lse_ref[...], dl_ref[...]
        dqt = jnp.zeros((QPAD, tq), F32)
        for c in range(S // kc):
            rows = slice(c * kc, (c + 1) * kc)
            st = _dot(k_ref[rows, :], q, NT)
            pt = jnp.exp(st - lse)
            dpt = _dot(k_ref[rows, 0:KVL], do, NT)
            dst = (pt * (dpt - dl)).astype(BF16)
            dv_ref[rows, :] += _dot(pt.astype(BF16), do)
            dk_ref[rows, :] += _dot(dst, q)
            dqt = dqt + _dot(kt_ref[:, rows], dst)
        dq_ref[...] = dqt.T

    return pl.pallas_call(
        body, name="attn_bwd", grid=(N_HEADS, S // tq),
        in_specs=[pl.BlockSpec((None, tq, QPAD), lambda h, i: (h, i, 0)),
                  pl.BlockSpec((S, QPAD), lambda h, i: (0, 0)),
                  pl.BlockSpec((QPAD, S), lambda h, i: (0, 0)),
                  pl.BlockSpec((tq, KVL), lambda h, i: (i, h)),
                  pl.BlockSpec((None, 1, tq), lambda h, i: (h, 0, i)),
                  pl.BlockSpec((None, 1, tq), lambda h, i: (h, 0, i))],
        out_specs=[pl.BlockSpec((None, tq, QPAD), lambda h, i: (h, i, 0)),
                   pl.BlockSpec((S, QPAD), lambda h, i: (0, 0)),
                   pl.BlockSpec((S, KVL), lambda h, i: (0, 0))],
        out_shape=[jax.ShapeDtypeStruct((N_HEADS, S, QPAD), F32), jax.ShapeDtypeStruct((S, QPAD), F32),
                   jax.ShapeDtypeStruct((S, KVL), F32)],
        compiler_params=_params("arbitrary", "arbitrary"),
    )(qcat, kcat, kcat_t, dolat, lse_row, delta_row)


def mla_pre_bwd(dout, dq, dk, dv, x, cq_raw, ckv_raw, vec, mw, tabs):
    S = x.shape[0]
    tm = min(256, S)

    def body(do_ref, dq_ref, dk_ref, dv_ref, x_ref, cq_ref, ckv_ref, vec_ref, cos_ref, sin_ref,
             wq_ref, wkv_ref, wkr_ref, wkrs_ref, qn_ref, kvn_ref, wn_ref, wr_ref, wrs_ref, bduk_ref,
             dx_ref, dlat_ref, dql_ref, dqcat_ref, vg_ref, ng_ref):
        @pl.when(pl.program_id(0) == 0)
        def _():
            vg_ref[...] = jnp.zeros_like(vg_ref)
            ng_ref[...] = jnp.zeros_like(ng_ref)

        cos, sin = cos_ref[...], sin_ref[...]
        for j in range(N_PAIR):
            dql = jnp.concatenate([dq_ref[2 * j, :, 0:KVL], dq_ref[2 * j + 1, :, 0:KVL]], axis=1) * ATTN_SCALE
            dql = dql.astype(BF16)
            dql_ref[:, 2 * KVL * j:2 * KVL * (j + 1)] = dql
            dqcat_ref[:, 2 * NOPE * j:2 * NOPE * (j + 1)] = _dot(dql, bduk_ref[j], NT).astype(BF16)
        groups = []
        for grp in range(N_HEADS // SLOTS):
            acc = jnp.zeros((tm, 128), F32)
            for hd in range(SLOTS * grp, SLOTS * (grp + 1)):
                acc = acc + jnp.where(_slot_mask(hd, tm), dq_ref[hd, :, KVL:], 0.0)
            groups.append(acc)
        dqr = jnp.concatenate(groups, axis=1) * ATTN_SCALE
        qa = (dqr * jnp.tile(cos, (1, SLOTS))).astype(BF16)
        qb = (dqr * jnp.tile(sin, (1, SLOTS))).astype(BF16)
        dqcat_ref[:, NOPE_ALL:NOPE_ALL + ROPE_ALL] = qa
        dqcat_ref[:, NOPE_ALL + ROPE_ALL:] = qb
        dcq = _dot(dqcat_ref[:, 0:NOPE_ALL], wn_ref[...], NT) + _dot(qa, wr_ref[...], NT) + _dot(qb, wrs_ref[...], NT)
        cqh, rq = _rms(cq_ref[...])
        ng_ref[0:1, :] += jnp.sum(dcq * cqh, axis=0, keepdims=True)
        dcq_raw = _rms_bwd(cqh, rq, dcq * qn_ref[...]).astype(BF16)
        dckv = dk_ref[:, 0:KVL] + dv_ref[...]
        ckvh, rk = _rms(ckv_ref[...])
        ng_ref[1:2, 0:KVL] += jnp.sum(dckv * ckvh, axis=0, keepdims=True)
        dckv_raw = _rms_bwd(ckvh, rk, dckv * kvn_ref[...]).astype(BF16)
        dkr = dk_ref[:, KVL:]
        ka = (dkr * cos).astype(BF16)
        kb = (dkr * sin).astype(BF16)
        dlat_ref[:, 0:QL] = dcq_raw
        dlat_ref[:, QL:QL + KVL] = dckv_raw
        dlat_ref[:, QL + KVL:QL + KVL + 128] = ka
        dlat_ref[:, QL + KVL + 128:] = kb
        dh = (_dot(dcq_raw, wq_ref[...], NT) + _dot(dckv_raw, wkv_ref[...], NT)
              + _dot(ka, wkr_ref[...], NT) + _dot(kb, wkrs_ref[...], NT))
        dx_ref[...] = do_ref[...] + _prenorm_bwd(dh, x_ref[...], vec_ref, vg_ref)

    row = lambda i: (i, 0)
    hrow = lambda i: (0, i, 0)
    return pl.pallas_call(
        body, name="mla_pre_bwd", grid=(S // tm,),
        in_specs=[pl.BlockSpec((tm, D), row), pl.BlockSpec((N_HEADS, tm, QPAD), hrow), pl.BlockSpec((tm, QPAD), row),
                  pl.BlockSpec((tm, KVL), row), pl.BlockSpec((tm, D), row), pl.BlockSpec((tm, QL), row),
                  pl.BlockSpec((tm, KVL), row), _w3((8, D)), pl.BlockSpec((tm, 128), row), pl.BlockSpec((tm, 128), row)]
        + _mla_weight_specs(),
        out_specs=[pl.BlockSpec((tm, D), row), pl.BlockSpec((tm, DLAT), row), pl.BlockSpec((tm, LAT_ALL), row),
                   pl.BlockSpec((tm, DQ_ALL), row), _w3((8, D)), _w3((8, QL))],
        out_shape=[jax.ShapeDtypeStruct((S, D), F32), jax.ShapeDtypeStruct((S, DLAT), BF16),
                   jax.ShapeDtypeStruct((S, LAT_ALL), BF16), jax.ShapeDtypeStruct((S, DQ_ALL), BF16),
                   jax.ShapeDtypeStruct((8, D), F32), jax.ShapeDtypeStruct((8, QL), F32)],
        compiler_params=_params("arbitrary"),
    )(dout, dq, dk, dv, x, cq_raw, ckv_raw, vec, tabs[0], tabs[1], *[mw[k] for k in MLA_WEIGHTS])


def mla_dw(h, dlat, cqn, dqcat, dql, qnope, olat, docat, ocat, du):
    S = h.shape[0]
    tk = min(512, S)
    nk = S // tk
    flat = lambda w: pl.BlockSpec((tk, w), lambda k: (k, 0))
    cols = lambda w: pl.BlockSpec((tk, w), lambda n, k: (k, n))
    pair_o = pl.BlockSpec((None, 2 * KVL, 128), lambda n, k: (n, 0, 0))
    g = {}
    g["in"] = dw_matmul("mla_dw_in", h, dlat, flat(D), flat(DLAT), (D, DLAT),
                        pl.BlockSpec((D, DLAT), lambda k: (0, 0)), (nk,))
    g["q"] = dw_matmul("mla_dw_q", cqn, dqcat, flat(QL), flat(DQ_ALL), (QL, DQ_ALL),
                       pl.BlockSpec((QL, DQ_ALL), lambda k: (0, 0)), (nk,))
    g["uk"] = dw_matmul("mla_dw_uk", dql, qnope, cols(2 * KVL), cols(2 * NOPE), (N_PAIR, 2 * KVL, 2 * NOPE), pair_o,
                        (N_PAIR, nk))
    g["uv"] = dw_matmul("mla_dw_uv", olat, docat, cols(2 * KVL), cols(2 * VH), (N_PAIR, 2 * KVL, 2 * VH), pair_o,
                        (N_PAIR, nk))
    g["o"] = dw_matmul("mla_dw_o", ocat, du, cols(256), pl.BlockSpec((tk, D), lambda n, k: (k, 0)), (D, D),
                       pl.BlockSpec((256, D), lambda n, k: (n, 0)), (D // 256, nk))
    return g


def loss_head(y, target):
    S = y.shape[0]
    tm = min(512, S)

    def body(y_ref, t_ref, loss_ref, dy_ref):
        @pl.when(pl.program_id(0) == 0)
        def _():
            loss_ref[...] = jnp.zeros_like(loss_ref)

        err = y_ref[...] - t_ref[...]
        dy_ref[...] = err * (1.0 / D)
        loss_ref[...] += 0.5 * jnp.sum(jnp.mean(err * err, axis=-1, keepdims=True), axis=0, keepdims=True)

    row = lambda i: (i, 0)
    return pl.pallas_call(
        body, name="loss_head", grid=(S // tm,),
        in_specs=[pl.BlockSpec((tm, D), row), pl.BlockSpec((tm, D), row)],
        out_specs=[pl.BlockSpec((1, 1), lambda i: (0, 0)), pl.BlockSpec((tm, D), row)],
        out_shape=[jax.ShapeDtypeStruct((1, 1), F32), jax.ShapeDtypeStruct((S, D), F32)],
        compiler_params=_params("arbitrary"),
    )(y, target)


MOD_COLS = 9 * D // N_CHIP


def mod_fwd(c_pad, ada_w, ada_b_loc):
    tn = MOD_COLS // 3

    def body(c_ref, w_ref, b_ref, o_ref):
        c = c_ref[...]
        sc = (c * jax.nn.sigmoid(c)).astype(BF16)
        o_ref[...] = _dot(sc, w_ref[...].astype(BF16)) + b_ref[...]

    return pl.pallas_call(
        body, name="mod_fwd", grid=(2, 3),
        in_specs=[pl.BlockSpec((16, D), lambda i, n: (0, 0)), pl.BlockSpec((None, D, tn), lambda i, n: (i, 0, n)),
                  pl.BlockSpec((None, 1, tn), lambda i, n: (i, 0, n))],
        out_specs=pl.BlockSpec((None, 16, tn), lambda i, n: (i, 0, n)),
        out_shape=jax.ShapeDtypeStruct((2, 16, MOD_COLS), F32),
        compiler_params=_params("parallel", "parallel"),
    )(c_pad, ada_w, ada_b_loc)


def _adamw_math(w, g, m, v):
    m = ADAM_B1 * m + (1.0 - ADAM_B1) * g
    v = ADAM_B2 * v + (1.0 - ADAM_B2) * (g * g)
    m_hat = m / (1.0 - ADAM_B1 ** ADAM_STEP)
    v_hat = v / (1.0 - ADAM_B2 ** ADAM_STEP)
    delta = -ADAM_LR * (m_hat / (jnp.sqrt(v_hat) + ADAM_EPS) + ADAM_WD * w)
    return delta, m, v


def adamw(name, w, g, m, v):
    shape = w.shape
    cols = shape[-1]
    rows = w.size // cols
    tr = rows
    for cand in (512, 256, 128, 64, 32, 16, 8):
        if rows % cand == 0 and cand * cols * 4 <= (2 << 20):
            tr = cand
            break

    def body(w_ref, g_ref, m_ref, v_ref, d_ref, mo_ref, vo_ref):
        d_ref[...], mo_ref[...], vo_ref[...] = _adamw_math(w_ref[...], g_ref[...], m_ref[...], v_ref[...])

    spec = pl.BlockSpec((tr, cols), lambda i: (i, 0))
    outs = pl.pallas_call(
        body, name=name, grid=(rows // tr,), in_specs=[spec] * 4, out_specs=[spec] * 3,
        out_shape=[jax.ShapeDtypeStruct((rows, cols), F32)] * 3,
        compiler_params=_params("parallel"),
    )(*[a.reshape(rows, cols) for a in (w, g, m, v)])
    return [o.reshape(shape) for o in outs]


def adamw_ada(c_pad, dmod, w, m, v):
    tr = 256

    def body(c_ref, dm_ref, w_ref, m_ref, v_ref, g_ref, d_ref, mo_ref, vo_ref):
        c = c_ref[...]
        sc = (c * jax.nn.sigmoid(c)).astype(BF16)
        g = _dot(sc, dm_ref[...].astype(BF16), TN)
        g_ref[...] = g
        d_ref[...], mo_ref[...], vo_ref[...] = _adamw_math(w_ref[...], g, m_ref[...], v_ref[...])

    wspec = pl.BlockSpec((None, tr, MOD_COLS), lambda i, r: (i, r, 0))
    return pl.pallas_call(
        body, name="adamw_ada", grid=(2, D // tr),
        in_specs=[pl.BlockSpec((16, tr), lambda i, r: (0, r)),
                  pl.BlockSpec((None, 16, MOD_COLS), lambda i, r: (i, 0, 0)), wspec, wspec, wspec],
        out_specs=[wspec] * 4,
        out_shape=[jax.ShapeDtypeStruct((2, D, MOD_COLS), F32)] * 4,
        compiler_params=_params("parallel", "parallel"),
    )(c_pad, dmod, w, m, v)


def sum_devices(name, a):
    _, R, C = a.shape
    tr = R
    for cand in (64, 32, 16, 8):
        if R % cand == 0:
            tr = cand
            break

    def body(a_ref, o_ref):
        acc = a_ref[0]
        for dev in range(1, N_DEV):
            acc = acc + a_ref[dev]
        o_ref[...] = acc

    return pl.pallas_call(
        body, name=name, grid=(R // tr,),
        in_specs=[pl.BlockSpec((N_DEV, tr, C), lambda i: (0, i, 0))],
        out_specs=pl.BlockSpec((tr, C), lambda i: (i, 0)),
        out_shape=jax.ShapeDtypeStruct((R, C), F32),
        compiler_params=_params("parallel"),
    )(a)


def _place():
    return lax.axis_index("x"), lax.axis_index("y"), lax.axis_index("c")


def _other_chips(x, y):
    return [(1 - x, y), (x, 1 - y), (1 - x, 1 - y)]


def gather_devices(name, a):
    m_per, n = a.shape

    def body(x_ref, out_ref, send_sems, recv_sems, local_sem):
        x, y, c = _place()
        me, sibling = (x, y, c), (x, y, 1 - c)
        chips = _other_chips(x, y)

        def rows(px, py, pc):
            return out_ref.at[pl.ds((4 * px + 2 * py + pc) * m_per, m_per), :]

        def copy(k, block, to, src=None):
            return pltpu.make_async_remote_copy(
                src_ref=rows(*block) if src is None else src, dst_ref=rows(*block),
                send_sem=send_sems.at[k], recv_sem=recv_sems.at[k], device_id=to, device_id_type=MESH)

        mine = pltpu.make_async_copy(x_ref, rows(*me), local_sem)
        mine.start()
        first = [copy(0, me, sibling, src=x_ref)]
        first += [copy(1 + j, me, (*chip, c), src=x_ref) for j, chip in enumerate(chips)]
        for cp in first:
            cp.start()
        passed = [copy(4 + j, (*chip, c), sibling) for j, chip in enumerate(chips)]
        for j, chip in enumerate(chips):
            copy(1 + j, (*chip, c), me).wait_recv()
            passed[j].start()
        copy(0, sibling, me).wait_recv()
        for j, chip in enumerate(chips):
            copy(4 + j, (*chip, 1 - c), me).wait_recv()
        for cp in first + passed:
            cp.wait_send()
        mine.wait()

    out = pl.pallas_call(
        body, name=name,
        out_shape=jax.ShapeDtypeStruct((N_DEV * m_per, n), a.dtype),
        in_specs=[pl.BlockSpec(memory_space=pltpu.VMEM)],
        out_specs=pl.BlockSpec(memory_space=pltpu.VMEM),
        scratch_shapes=[pltpu.SemaphoreType.DMA((7,)), pltpu.SemaphoreType.DMA((7,)), pltpu.SemaphoreType.DMA],
        compiler_params=pltpu.CompilerParams(vmem_limit_bytes=VMEM_LIMIT),
    )(a)
    return out.reshape(N_DEV, m_per, n)


_ANY = pl.BlockSpec(memory_space=pl.ANY)


def _hbm_ref(a):
    return jax.new_ref(a, memory_space=pltpu.MemorySpace.HBM)


def _hbm_empty(shape, dtype):
    return jax.empty_ref(jax.ShapeDtypeStruct(shape, dtype), memory_space=pltpu.MemorySpace.HBM)


ID_PAIR, ID_CHIPS, ID_SHARE, ID_UKV = 8, 9, 10, 11


def _sequencer(name, collective_id, n_sem, peers_of, program):
    sems = pltpu.SemaphoreType.DMA((n_sem,))

    @pl.kernel(mesh=plsc.ScalarSubcoreMesh(axis_name="seq", num_cores=1), name=name, scratch_types=[sems, sems],
               compiler_params=pltpu.CompilerParams(collective_id=collective_id))
    def launch(send_sem, recv_sem):
        x, y, c = _place()
        peers = peers_of(x, y, c)
        barrier = pltpu.get_barrier_semaphore()
        for peer in peers:
            pl.semaphore_signal(barrier, inc=1, device_id=peer, device_id_type=MESH)
        pl.semaphore_wait(barrier, len(peers))
        program(x, y, c, send_sem, recv_sem)

    launch()


def gather_weights(name, stage, arrays):
    n = len(arrays)
    refs = [_hbm_ref(a) for a in arrays]

    def program(x, y, c, send_sem, recv_sem):
        me = 2 * x + y
        chips = _other_chips(x, y)

        def ici(t, r, half):
            cx, cy = chips[r]
            mine = refs[t].at[me, half]
            return pltpu.make_async_remote_copy(
                src_ref=mine, dst_ref=mine, send_sem=send_sem.at[3 * t + r], recv_sem=recv_sem.at[3 * t + r],
                device_id=(cx, cy, c), device_id_type=MESH)

        def d2d(t, r, half):
            cx, cy = chips[r]
            there = refs[t].at[2 * cx + cy, half]
            k = 3 * n + 3 * t + r
            return pltpu.make_async_remote_copy(
                src_ref=there, dst_ref=there, send_sem=send_sem.at[k], recv_sem=recv_sem.at[k],
                device_id=(x, y, 1 - c), device_id_type=MESH)

        for t in range(n):
            for r in range(3):
                ici(t, r, c).start()
        for t in range(n):
            for r in range(3):
                ici(t, r, c).wait_recv()
                d2d(t, r, c).start()
        for t in range(n):
            for r in range(3):
                d2d(t, r, 1 - c).wait_recv()
        for t in range(n):
            for r in range(3):
                ici(t, r, c).wait_send()
                d2d(t, r, c).wait_send()

    _sequencer(name, stage, 6 * n, lambda x, y, c: [(x, y, 1 - c)] + [(cx, cy, c) for cx, cy in _other_chips(x, y)],
               program)
    return [r[...] for r in refs]


def cast_into_slots(name, chip, shards):
    steps = 2

    def body(chip_ref, *refs):
        n = len(refs) // 2
        for src, dst in zip(refs[:n], refs[n:]):
            dst[...] = src[...].astype(BF16)

    def spec_in(a, prefix):
        R, C = a.shape[-2:]
        return pl.BlockSpec((None,) * (len(prefix) + 1) + (R // steps, C), lambda h, i, chip_ref: prefix + (h, i, 0))

    def spec_out(a):
        R, C = a.shape[-2:]
        return pl.BlockSpec((None, None, R // steps, C), lambda h, i, chip_ref: (chip_ref[0], h, i, 0))

    return pl.pallas_call(
        body, name=name,
        grid_spec=pltpu.PrefetchScalarGridSpec(
            num_scalar_prefetch=1, grid=(2, steps),
            in_specs=[spec_in(a, p) for a, p in shards], out_specs=[spec_out(a) for a, _ in shards]),
        out_shape=[jax.ShapeDtypeStruct((N_CHIP, 2) + a.shape[-2:], BF16) for a, _ in shards],
        compiler_params=_params("parallel", "parallel"),
    )(chip, *[a for a, _ in shards])


def reduce_pair(name, grads):
    n = len(grads)
    src = [_hbm_ref(g) for g in grads]
    dst = [_hbm_empty((N_CHIP,) + g.shape[2:], g.dtype) for g in grads]

    def program(x, y, c, send_sem, recv_sem):
        cps = [pltpu.make_async_remote_copy(
            src_ref=src[t].at[:, 1 - c], dst_ref=dst[t], send_sem=send_sem.at[t], recv_sem=recv_sem.at[t],
            device_id=(x, y, 1 - c), device_id_type=MESH) for t in range(n)]
        for cp in cps:
            cp.start()
        for cp in cps:
            cp.wait()

    _sequencer(name, ID_PAIR, n, lambda x, y, c: [(x, y, 1 - c)], program)
    return [r[...] for r in src], [r[...] for r in dst]


def pair_add(name, core, g, got):
    _, _, R, C = g.shape

    def body(core_ref, g_ref, got_ref, o_ref, token_ref):
        o_ref[...] = (g_ref[...] + got_ref[...]).astype(BF16)
        token_ref[...] = jnp.zeros_like(token_ref)

    return pl.pallas_call(
        body, name=name,
        grid_spec=pltpu.PrefetchScalarGridSpec(
            num_scalar_prefetch=1, grid=(N_CHIP,),
            in_specs=[pl.BlockSpec((None, None, R, C), lambda q, core_ref: (q, core_ref[0], 0, 0)),
                      pl.BlockSpec((None, R, C), lambda q, core_ref: (q, 0, 0))],
            out_specs=[pl.BlockSpec((None, R, C), lambda q, core_ref: (q, 0, 0)),
                       pl.BlockSpec((8, 128), lambda q, core_ref: (0, 0))]),
        out_shape=[jax.ShapeDtypeStruct((N_CHIP, R, C), BF16), jax.ShapeDtypeStruct((8, 128), F32)],
        compiler_params=_params("arbitrary"),
    )(core, g, got)


def reduce_chips(name, sums):
    n = len(sums)
    src = [_hbm_ref(s) for s in sums]
    dst = [_hbm_empty((3,) + s.shape[1:], s.dtype) for s in sums]

    def program(x, y, c, send_sem, recv_sem):
        cps = []
        for t in range(n):
            for r, (cx, cy) in enumerate(_other_chips(x, y)):
                cps.append(pltpu.make_async_remote_copy(
                    src_ref=src[t].at[2 * cx + cy], dst_ref=dst[t].at[r],
                    send_sem=send_sem.at[3 * t + r], recv_sem=recv_sem.at[3 * t + r],
                    device_id=(cx, cy, c), device_id_type=MESH))
        for cp in cps:
            cp.start()
        for cp in cps:
            cp.wait()

    _sequencer(name, ID_CHIPS, 3 * n, lambda x, y, c: [(cx, cy, c) for cx, cy in _other_chips(x, y)], program)
    return [r[...] for r in src], [r[...] for r in dst]


def chip_add(name, place, s, got, k, n_slots, prev=None):
    _, R, C = s.shape

    def body(place_ref, s_ref, got_ref, *rest):
        o_ref, token_ref = rest[-2:]
        o_ref[...] = ((s_ref[...].astype(F32) + got_ref[0].astype(F32)) + got_ref[1].astype(F32)) + got_ref[2].astype(F32)
        token_ref[...] = jnp.zeros_like(token_ref)

    in_specs = [pl.BlockSpec((None, R, C), lambda i, place_ref: (place_ref[0], 0, 0)),
                pl.BlockSpec((3, R, C), lambda i, place_ref: (0, 0, 0))]
    args = [place, s, got]
    aliases = {}
    if prev is not None:
        in_specs.append(_ANY)
        args.append(prev)
        aliases = {3: 0}
    return pl.pallas_call(
        body, name=name,
        grid_spec=pltpu.PrefetchScalarGridSpec(
            num_scalar_prefetch=1, grid=(1,), in_specs=in_specs,
            out_specs=[pl.BlockSpec((None, None, R, C), lambda i, place_ref: (k, place_ref[1], 0, 0)),
                       pl.BlockSpec((8, 128), lambda i, place_ref: (0, 0))]),
        out_shape=[jax.ShapeDtypeStruct((n_slots, 2, R, C), F32), jax.ShapeDtypeStruct((8, 128), F32)],
        input_output_aliases=aliases,
        compiler_params=_params("arbitrary"),
    )(*args)


def share_halves(name, stacks, slots):
    n = len(stacks)
    dst = [_hbm_ref(s) for s in stacks]

    def program(x, y, c, send_sem, recv_sem):
        cps = [pltpu.make_async_remote_copy(
            src_ref=dst[t].at[slots[t], c], dst_ref=dst[t].at[slots[t], c],
            send_sem=send_sem.at[t], recv_sem=recv_sem.at[t],
            device_id=(x, y, 1 - c), device_id_type=MESH) for t in range(n)]
        for cp in cps:
            cp.start()
        for cp in cps:
            cp.wait()

    _sequencer(name, ID_SHARE, n, lambda x, y, c: [(x, y, 1 - c)], program)
    return [r[...] for r in dst]


def gather_blocks(name, slotted):
    out = _hbm_ref(slotted)

    def program(x, y, c, send_sem, recv_sem):
        sibling = (x, y, 1 - c)
        chips = _other_chips(x, y)

        def copy(k, px, py, pc, to):
            block = out.at[4 * px + 2 * py + pc]
            return pltpu.make_async_remote_copy(src_ref=block, dst_ref=block, send_sem=send_sem.at[k],
                                                recv_sem=recv_sem.at[k], device_id=to, device_id_type=MESH)

        first = [copy(0, x, y, c, sibling)] + [copy(1 + j, x, y, c, (cx, cy, c)) for j, (cx, cy) in enumerate(chips)]
        for cp in first:
            cp.start()
        passed = [copy(4 + j, cx, cy, c, sibling) for j, (cx, cy) in enumerate(chips)]
        for j, (cx, cy) in enumerate(chips):
            copy(1 + j, cx, cy, c, (x, y, c)).wait_recv()
            passed[j].start()
        copy(0, x, y, 1 - c, (x, y, c)).wait_recv()
        for j, (cx, cy) in enumerate(chips):
            copy(4 + j, cx, cy, 1 - c, (x, y, c)).wait_recv()
        for cp in first + passed:
            cp.wait_send()

    _sequencer(name, ID_UKV, 7, lambda x, y, c: [(x, y, 1 - c)] + [(cx, cy, c) for cx, cy in _other_chips(x, y)],
               program)
    return out[...]


def place_block(name, dev, a):
    M, N = a.shape
    tr = min(M, 64)

    def body(dev_ref, a_ref, o_ref):
        o_ref[...] = a_ref[...]

    return pl.pallas_call(
        body, name=name,
        grid_spec=pltpu.PrefetchScalarGridSpec(
            num_scalar_prefetch=1, grid=(M // tr,),
            in_specs=[pl.BlockSpec((tr, N), lambda i, dev_ref: (i, 0))],
            out_specs=pl.BlockSpec((None, tr, N), lambda i, dev_ref: (dev_ref[0], i, 0))),
        out_shape=jax.ShapeDtypeStruct((N_DEV, M, N), a.dtype),
        compiler_params=_params("parallel"),
    )(dev, a)


def _swap_rope(a):
    return jnp.concatenate([a[..., ROPE // 2:], a[..., :ROPE // 2]], axis=-1)


def _rope_tables(S):
    inv = 1.0 / (ROPE_THETA ** (jnp.arange(0, ROPE, 2, dtype=F32) / ROPE))
    ang = jnp.arange(S, dtype=F32)[:, None] * inv[None, :]
    cos, sin = jnp.cos(ang), jnp.sin(ang)
    return (jnp.tile(jnp.concatenate([cos, cos], axis=1), (1, SLOTS)),
            jnp.tile(jnp.concatenate([-sin, sin], axis=1), (1, SLOTS)))


def _vec(norm_g, mod, i, k):
    rows = [norm_g[i, 2 * k], norm_g[i, 2 * k + 1], mod[i, 3 * k], mod[i, 3 * k + 1], mod[i, 3 * k + 2]]
    return jnp.concatenate([jnp.stack(rows), jnp.zeros((3, D), F32)], axis=0)


def _unpack_weights(full, w_uk, w_uv, q_norm, kv_norm):
    G = D // 4
    ffn_in = [[full[2 * i + k].reshape(N_CHIP, D, FSH) for k in range(2)] for i in range(2)]
    ffn_out = [[full[4 + 2 * i + k].reshape(2, FSH, D) for k in range(2)] for i in range(2)]
    pw = full[8].reshape(N_CHIP, 4, G // N_CHIP, G).transpose(1, 0, 2, 3).reshape(4, G, G)
    w_in = full[9].reshape(D, QL + KVL + ROPE)
    w_uq = full[10].reshape(QL, N_HEADS, NOPE + ROPE)
    wkr = w_in[:, QL + KVL:]
    wr = w_uq[:, :, NOPE:]
    eye2 = jnp.eye(2, dtype=BF16)
    uk_t = jnp.transpose(w_uk, (1, 2, 0)).reshape(N_PAIR, 2, NOPE, KVL)
    bduk = jnp.einsum("janc,ab->janbc", uk_t, eye2).reshape(N_PAIR, 2 * NOPE, 2 * KVL)
    uv = jnp.transpose(w_uv, (1, 0, 2)).reshape(N_PAIR, 2, KVL, VH)
    bduv = jnp.einsum("jacn,ab->jacbn", uv, eye2).reshape(N_PAIR, 2 * KVL, 2 * VH)
    mw = dict(wq=w_in[:, :QL], wkv=w_in[:, QL:QL + KVL], wkr4=jnp.tile(wkr, (1, SLOTS)),
              wkrs4=jnp.tile(_swap_rope(wkr), (1, SLOTS)), qn=q_norm, kvn=kv_norm,
              wn=w_uq[:, :, :NOPE].reshape(QL, NOPE_ALL), wr=wr.reshape(QL, ROPE_ALL),
              wrs=_swap_rope(wr).reshape(QL, ROPE_ALL), bduk=bduk)
    return ffn_in, ffn_out, pw, mw, bduv, full[11].reshape(D, D)


def _example_step(x, target, mod, norm_g, pvec, ffn_in, ffn_out, pw, mw, bduv, wo, reducer):
    S = x.shape[0]
    tabs = _rope_tables(S)
    vec = [[_vec(norm_g, mod, i, k) for k in range(3)] for i in range(2)]
    saved = {}
    for i in range(2):
        xin = x
        x, a, u, h = ffn_fwd(xin, vec[i][0], ffn_in[i][0], ffn_out[i][0], 0.5)
        saved[i, 0] = (xin, a, u, h)
        xin = x
        if i == 0:
            x, y, z = pool_fwd(xin, vec[i][1], pw, pvec)
            saved[i, 1] = (xin, y, z)
        else:
            h_m, cq_raw, ckv_raw, cqn, qnope, qcat, kcat, vcat = mla_pre(xin, vec[i][1], mw, tabs)
            olat, lse = attn_fwd(qcat, kcat, vcat)
            x, u_m, ocat = mla_post(olat, xin, vec[i][1], bduv, wo)
            saved[i, 1] = (xin, h_m, cq_raw, ckv_raw, cqn, qnope, qcat, kcat, olat, lse, u_m, ocat)
        xin = x
        x, a, u, h = ffn_fwd(xin, vec[i][2], ffn_in[i][1], ffn_out[i][1], 0.5)
        saved[i, 2] = (xin, a, u, h)
    loss, dx = loss_head(x, target)

    vg = {}
    G = D // 4

    def ffn_grads(i, k, dw_in, dw_out):
        return [(0, 2 * i + k, 4, dw_in.reshape(N_CHIP, 2, D // 2, FSH)),
                (1, 2 * i + k, 4, dw_out.reshape(N_CHIP, 2, DFF // 8, D))]

    for i in (1, 0):
        xin, a, u, h = saved[i, 2]
        dx, du, act, da, vg[i, 2] = ffn_bwd(dx, xin, u, a, vec[i][2], ffn_in[i][1], ffn_out[i][1], 0.5)
        dx = reducer.advance(dx)
        reducer.add(f"f{i}1", ffn_grads(i, 1, *ffn_dw(h, da, act, du)))
        if i == 0:
            xin, y, z = saved[i, 1]
            dx, vg[i, 1], pgrad, g_pool = pool_bwd(dx, xin, y, z, vec[i][1], pw, pvec)
            dx = reducer.advance(dx)
        else:
            xin, h_m, cq_raw, ckv_raw, cqn, qnope, qcat, kcat, olat, lse, u_m, ocat = saved[i, 1]
            du, docat, dolat, delta, vg_post = mla_post_bwd(dx, u_m, olat, vec[i][1], bduv, wo)
            dolat = reducer.advance(dolat)
            dq, dk, dv = attn_bwd(qcat, kcat, kcat.T, dolat, lse, delta)
            dq = reducer.advance(dq)
            dx, dlat, dql, dqcat, vg_pre, ngrad = mla_pre_bwd(
                dx, dq, dk, dv, xin, cq_raw, ckv_raw, vec[i][1], mw, tabs)
            vg[i, 1] = vg_post + vg_pre
            g = mla_dw(h_m, dlat, cqn, dqcat, dql, qnope, olat, docat, ocat, du)
            slots = lambda a: a.reshape(D, SLOTS, ROPE).sum(axis=1)
            g_kr = slots(g["in"][:, QL + KVL:QL + KVL + 128]) + _swap_rope(slots(g["in"][:, QL + KVL + 128:]))
            g_in = jnp.concatenate([g["in"][:, :QL + KVL], g_kr], axis=1)
            g_r = g["q"][:, NOPE_ALL:NOPE_ALL + ROPE_ALL].reshape(QL, N_HEADS, ROPE)
            g_rs = g["q"][:, NOPE_ALL + ROPE_ALL:].reshape(QL, N_HEADS, ROPE)
            g_uq = jnp.concatenate([g["q"][:, :NOPE_ALL].reshape(QL, N_HEADS, NOPE), g_r + _swap_rope(g_rs)], axis=-1)

            def heads(pairs):
                blk = pairs.reshape(N_PAIR, 2, KVL, 2, NOPE)
                per_head = jnp.stack([blk[:, 0, :, 0, :], blk[:, 1, :, 1, :]], axis=1).reshape(N_HEADS, KVL, NOPE)
                return jnp.transpose(per_head, (1, 0, 2)).reshape(KVL, N_HEADS * NOPE)

            reducer.add("mla", [(3, 0, 1, g_in.reshape(N_CHIP, 2, D // 8, QL + KVL + ROPE)),
                                (4, 0, 1, g_uq.reshape(N_CHIP, 2, QL // 8, N_HEADS * (NOPE + ROPE))),
                                (5, 0, 1, g["o"].reshape(N_CHIP, 2, D // 8, D))])
            reducer.add_replicated(jnp.concatenate([heads(g["uk"]), heads(g["uv"])], axis=0))
        xin, a, u, h = saved[i, 0]
        dx, du, act, da, vg[i, 0] = ffn_bwd(dx, xin, u, a, vec[i][0], ffn_in[i][0], ffn_out[i][0], 0.5)
        dx = reducer.advance(dx)
        grads = ffn_grads(i, 0, *ffn_dw(h, da, act, du))
        if i == 0:
            grads.append((2, 0, 1, g_pool.reshape(N_CHIP, 2, 2 * G // N_CHIP, G)))
        reducer.add(f"f{i}0", grads)
    return loss, dx, vg, pgrad, ngrad


class _GradReducer:
    def __init__(self, core, place, dev):
        self.core, self.place, self.dev = core, place, dev
        self.stacks = {}
        self.live = []
        self.replicated = None

    def add(self, tag, items):
        gen = self._run(tag, items)
        next(gen)
        self.live.append(gen)

    def add_replicated(self, block):
        self.replicated = gather_blocks("gather_ukv", place_block("place_ukv", self.dev, block))

    def advance(self, carry=None):
        live, tokens = [], []
        for gen in self.live:
            try:
                tokens += next(gen)
                live.append(gen)
            except StopIteration:
                pass
        self.live = live
        if tokens and carry is not None:
            carry, _ = lax.optimization_barrier((carry, tokens))
        return carry

    def finish(self):
        while self.live:
            self.advance()
        return self.stacks, self.replicated

    def _run(self, tag, items):
        grads, from_pair = reduce_pair(f"reduce_pair_{tag}", [g for *_, g in items])
        yield []
        sums, tokens = [], []
        for j, (g, p) in enumerate(zip(grads, from_pair)):
            s, token = pair_add(f"pair_add_{tag}_{j}", self.core, g, p)
            sums.append(s)
            tokens.append(token)
        sums, from_chips = reduce_chips(f"reduce_chips_{tag}", sums)
        yield tokens
        tokens = []
        for j, ((o, k, n_slots, _), s, p) in enumerate(zip(items, sums, from_chips)):
            self.stacks[o], token = chip_add(f"chip_add_{tag}_{j}", self.place, s, p, k, n_slots, self.stacks.get(o))
            tokens.append(token)
        shared = share_halves(f"share_halves_{tag}", [self.stacks[o] for o, *_ in items], [k for _, k, *_ in items])
        for (o, *_), v in zip(items, shared):
            self.stacks[o] = v
        yield tokens


SMALL_IN = 8 * 640
SMALL_GRAD = 8 * 4224
SMALL_W = 8 * 2944


def _pack(parts, total):
    flat = jnp.concatenate([p.reshape(-1) for p in parts])
    return jnp.concatenate([flat, jnp.zeros((total - flat.shape[0],), F32)]).reshape(8, total // 8)


def kernel(x, c, ada_w, ada_b, norm_g, ffn_w_in, ffn_w_out, pool_w, pool_b, pool_scale, mla_w_in, mla_q_norm, mla_kv_norm, mla_w_uq, mla_w_uk, mla_w_uv, mla_w_o, loss_target, m_ada_w, m_ada_b, m_norm_g, m_ffn_w_in, m_ffn_w_out, m_pool_w, m_pool_b, m_pool_scale, m_mla_w_in, m_mla_q_norm, m_mla_kv_norm, m_mla_w_uq, m_mla_w_uk, m_mla_w_uv, m_mla_w_o, v_ada_w, v_ada_b, v_norm_g, v_ffn_w_in, v_ffn_w_out, v_pool_w, v_pool_b, v_pool_scale, v_mla_w_in, v_mla_q_norm, v_mla_kv_norm, v_mla_w_uq, v_mla_w_uk, v_mla_w_uv, v_mla_w_o):
    ix, iy, ic = _place()
    chip = 2 * ix + iy
    dev = 2 * chip + ic
    core_arr = ic.astype(jnp.int32).reshape(1)
    chip_arr = chip.astype(jnp.int32).reshape(1)
    S = x.shape[1]
    G = D // 4
    NG = D // N_CHIP

    def chip_cols(a, width, axis):
        return lax.dynamic_slice_in_dim(a, chip * width, width, axis)

    got = gather_devices("gather_small_in", _pack([c, norm_g, pool_b, mla_q_norm], SMALL_IN)).reshape(N_DEV, SMALL_IN)
    c_all = got[:, :D]
    parts = got[0::2]
    o = D
    norm_g_full = parts[:, o:o + 12 * NG].reshape(N_CHIP, 2, 6, NG).transpose(1, 2, 0, 3).reshape(2, 6, D)
    o += 12 * NG
    pool_b_full = parts[:, o:o + G].reshape(N_CHIP, 4, G // N_CHIP).transpose(1, 0, 2).reshape(1, D)
    o += G
    q_norm_full = parts[:, o:o + QL // N_CHIP].reshape(1, QL)
    pvec = jnp.concatenate([pool_b_full, pool_scale, jnp.zeros((6, D), F32)], axis=0)

    c_pad = jnp.concatenate([c_all, jnp.zeros((8, D), F32)], axis=0)
    mod_loc = mod_fwd(c_pad, ada_w, chip_cols(ada_b, MOD_COLS, 1).reshape(2, 1, MOD_COLS))
    got = gather_devices("gather_mod", mod_loc[:, :8].transpose(1, 0, 2).reshape(8, 2 * MOD_COLS))
    mine = lax.dynamic_index_in_dim(got[0::2].reshape(N_CHIP, 8, 2, MOD_COLS), dev, axis=1, keepdims=False)
    mod = mine.transpose(1, 0, 2).reshape(2, 9, D)

    bf = lambda a: a.astype(BF16)
    w_in_halves = ffn_w_in.reshape(2, 2, 2, D // 2, FSH)
    w_out_halves = ffn_w_out.reshape(2, 2, 2, DFF // 8, D)
    shards = [(w_in_halves, (i, k)) for i in range(2) for k in range(2)]
    shards += [(w_out_halves, (i, k)) for i in range(2) for k in range(2)]
    shards += [(pool_w.reshape(2, 2 * G // N_CHIP, G), ()), (mla_w_in.reshape(2, D // 8, QL + KVL + ROPE), ()),
               (mla_w_uq.reshape(2, QL // 8, N_HEADS * (NOPE + ROPE)), ()), (mla_w_o.reshape(2, D // 8, D), ())]
    full = [None] * len(shards)
    stages = [(0, 4), (1, 5, 8), (2, 6), (9, 10, 11), (3, 7)]
    slotted = dict(zip(stages[0], cast_into_slots("cast_first", chip_arr, [shards[t] for t in stages[0]])))
    rest = [t for members in stages[1:] for t in members]
    for stage, members in enumerate(stages):
        got_w = gather_weights(f"gather_weights_{stage}", stage, [slotted[t] for t in members])
        for t, a in zip(members, got_w):
            full[t] = a
        if stage == 0:
            slotted.update(zip(rest, cast_into_slots("cast_rest", chip_arr, [shards[t] for t in rest])))
    ffn_in, ffn_out, pw, mw, bduv, wo = _unpack_weights(full, bf(mla_w_uk[0]), bf(mla_w_uv[0]), q_norm_full,
                                                        mla_kv_norm)

    place_arr = jnp.stack([chip, ic]).astype(jnp.int32)
    reducer = _GradReducer(core_arr, place_arr, dev.astype(jnp.int32).reshape(1))
    loss_mine, grad_x, vg, pgrad, ngrad = _example_step(
        x[0], loss_target[0], mod, norm_g_full, pvec, ffn_in, ffn_out, pw, mw, bduv, wo, reducer)
    loss = lax.psum(loss_mine[0, 0], ("x", "y", "c"))
    stacks, ukv = reducer.finish()
    g_ffn_in, g_ffn_out, g_pool_w, g_mla_in, g_uq, g_wo = [stacks[o] for o in range(6)]
    g_ffn_in = g_ffn_in.reshape(ffn_w_in.shape)
    g_ffn_out = g_ffn_out.reshape(ffn_w_out.shape)
    g_pool_w = g_pool_w.reshape(pool_w.shape)
    g_mla_in = g_mla_in.reshape(mla_w_in.shape)
    g_uq = g_uq.reshape(mla_w_uq.shape)
    g_wo = g_wo.reshape(mla_w_o.shape)

    ukv = sum_devices("sum_ukv", ukv)
    g_uk = ukv[:KVL].reshape(mla_w_uk.shape)
    g_uv = ukv[KVL:].reshape(mla_w_uv.shape)

    dmod = jnp.stack([jnp.concatenate([vg[i, k][0:3] for k in range(3)]) for i in range(2)])
    dnorm = jnp.stack([jnp.concatenate([vg[i, k][3:5] for k in range(3)]) for i in range(2)])
    small = _pack([dmod, dnorm, pgrad[0], pgrad[1], ngrad[0], ngrad[1, :KVL]], SMALL_GRAD)
    got = gather_devices("gather_small_grad", small)
    tot = sum_devices("sum_small_grad", got).reshape(-1)
    n_mod = 2 * 9 * D
    g_ada_b = tot[:n_mod].reshape(ada_b.shape)
    o = n_mod
    g_norm = chip_cols(tot[o:o + 12 * D].reshape(2, 6, D), NG, 2)
    o += 12 * D
    g_pool_b = chip_cols(tot[o:o + D].reshape(1, 4, G), G // N_CHIP, 2)
    o += D
    g_pool_scale = tot[o:o + D].reshape(pool_scale.shape)
    o += D
    g_q_norm = chip_cols(tot[o:o + QL].reshape(1, QL), QL // N_CHIP, 1)
    o += QL
    g_kv_norm = tot[o:o + KVL].reshape(mla_kv_norm.shape)
    dmod_all = chip_cols(got.reshape(N_DEV, -1)[:, :n_mod].reshape(N_DEV, 2, 9 * D), MOD_COLS, 2)
    dmod_pad = jnp.concatenate([dmod_all.transpose(1, 0, 2), jnp.zeros((2, 8, MOD_COLS), F32)], axis=1)

    g_ada_w, d_ada_w, nm_ada_w, nv_ada_w = adamw_ada(c_pad, dmod_pad, ada_w, m_ada_w, v_ada_w)
    small_names = ["ada_b", "norm_g", "pool_b", "pool_scale", "mla_q_norm", "mla_kv_norm"]
    small_w = [ada_b, norm_g, pool_b, pool_scale, mla_q_norm, mla_kv_norm]
    small_g = [g_ada_b, g_norm, g_pool_b, g_pool_scale, g_q_norm, g_kv_norm]
    small_m = [m_ada_b, m_norm_g, m_pool_b, m_pool_scale, m_mla_q_norm, m_mla_kv_norm]
    small_v = [v_ada_b, v_norm_g, v_pool_b, v_pool_scale, v_mla_q_norm, v_mla_kv_norm]
    packed = adamw("adamw_small", *[_pack(p, SMALL_W) for p in (small_w, small_g, small_m, small_v)])
    upd = {}
    o = 0
    for name, w in zip(small_names, small_w):
        upd[name] = [p.reshape(-1)[o:o + w.size].reshape(w.shape) for p in packed]
        o += w.size
    big = [("ffn_w_in", ffn_w_in, g_ffn_in, m_ffn_w_in, v_ffn_w_in),
           ("ffn_w_out", ffn_w_out, g_ffn_out, m_ffn_w_out, v_ffn_w_out),
           ("pool_w", pool_w, g_pool_w, m_pool_w, v_pool_w),
           ("mla_w_in", mla_w_in, g_mla_in, m_mla_w_in, v_mla_w_in),
           ("mla_w_uq", mla_w_uq, g_uq, m_mla_w_uq, v_mla_w_uq),
           ("mla_w_uk", mla_w_uk, g_uk, m_mla_w_uk, v_mla_w_uk),
           ("mla_w_uv", mla_w_uv, g_uv, m_mla_w_uv, v_mla_w_uv),
           ("mla_w_o", mla_w_o, g_wo, m_mla_w_o, v_mla_w_o)]
    for name, w, g, m, v in big:
        upd[name] = adamw("adamw_" + name, w, g, m, v)
    upd["ada_w"] = [d_ada_w, nm_ada_w, nv_ada_w]

    order = ["ada_w", "ada_b", "norm_g", "ffn_w_in", "ffn_w_out", "pool_w", "pool_b", "pool_scale", "mla_w_in",
             "mla_q_norm", "mla_kv_norm", "mla_w_uq", "mla_w_uk", "mla_w_uv", "mla_w_o"]
    grad = dict(ada_w=g_ada_w, ada_b=g_ada_b, norm_g=g_norm, ffn_w_in=g_ffn_in, ffn_w_out=g_ffn_out, pool_w=g_pool_w,
                pool_b=g_pool_b, pool_scale=g_pool_scale, mla_w_in=g_mla_in, mla_q_norm=g_q_norm,
                mla_kv_norm=g_kv_norm, mla_w_uq=g_uq, mla_w_uk=g_uk, mla_w_uv=g_uv, mla_w_o=g_wo)
    return (loss, grad_x[None], *[grad[n] for n in order], *[upd[n][0] for n in order],
            *[upd[n][1] for n in order], *[upd[n][2] for n in order])
```

```python
import functools

import jax
import jax.numpy as jnp
from jax import lax
from jax.experimental import pallas as pl
from jax.experimental.pallas import tpu as pltpu
from jax.experimental.pallas import tpu_sc as plsc

F32 = jnp.float32
BF16 = jnp.bfloat16

D = 1024
DFF = 2816
FSH = 1408
N_CHIP = 4
N_DEV = 8
N_HEADS = 16
NOPE = 64
ROPE = 32
VH = 64
QL = 256
KVL = 128
QPAD = 256
EPS = 1e-6
ATTN_SCALE = (NOPE + ROPE) ** -0.5
ROPE_THETA = 10000.0
POOL_WINDOWS = (2, 4, 8, 16)
HALO = 8
ATTN_TQ = 1024
ATTN_KC = 512

ADAM_LR, ADAM_B1, ADAM_B2, ADAM_EPS, ADAM_WD, ADAM_STEP = 0.001, 0.9, 0.999, 1e-08, 0.01, 10

VMEM_LIMIT = 60 * 1024 * 1024
MESH = pl.DeviceIdType.MESH

NT = (((1,), (1,)), ((), ()))
TN = (((0,), (0,)), ((), ()))


def _params(*sem):
    return pltpu.CompilerParams(dimension_semantics=sem, vmem_limit_bytes=VMEM_LIMIT)


def _dot(a, b, dims=None):
    if dims is None:
        return jnp.dot(a, b, preferred_element_type=F32)
    return lax.dot_general(a, b, dims, preferred_element_type=F32)


def _rms(x):
    r = lax.rsqrt(jnp.mean(x * x, axis=-1, keepdims=True) + EPS)
    return x * r, r


def _rms_bwd(xhat, r, dxhat):
    return r * (dxhat - xhat * jnp.mean(dxhat * xhat, axis=-1, keepdims=True))


def _as_row(col):
    return jnp.broadcast_to(col, (col.shape[0], 128)).T[0:1, :]


def _prenorm(x, vec_ref):
    xhat, r = _rms(x)
    h = xhat * vec_ref[0:1, :] * (1.0 + vec_ref[3:4, :]) + vec_ref[2:3, :]
    return h, xhat, r


def _postnorm_bwd(dout, u, vec_ref, weight):
    uhat, r = _rms(u)
    gt = weight * (1.0 + vec_ref[4:5, :])
    dy = dout * gt
    dgate_rows = (weight * dout) * (uhat * vec_ref[1:2, :])
    dgpost_rows = dy * uhat
    du = _rms_bwd(uhat, r, dy * vec_ref[1:2, :])
    return du, dgate_rows, dgpost_rows


def _prenorm_bwd(dh, x, vec_ref, vg_ref):
    xhat, r = _rms(x)
    sc1 = 1.0 + vec_ref[3:4, :]
    g = vec_ref[0:1, :]
    vg_ref[0:1, :] += jnp.sum(dh, axis=0, keepdims=True)
    vg_ref[1:2, :] += jnp.sum(dh * (xhat * g), axis=0, keepdims=True)
    vg_ref[3:4, :] += jnp.sum(dh * sc1 * xhat, axis=0, keepdims=True)
    return _rms_bwd(xhat, r, dh * g * sc1)


def ffn_fwd(x, vec, w_in, w_out, weight):
    S = x.shape[0]
    tm = min(512, S)

    def body(x_ref, vec_ref, wg_ref, wu_ref, wo_ref, xo_ref, a_ref, u_ref, h_ref, acc_ref):
        j = pl.program_id(1)

        @pl.when(j == 0)
        def _():
            h, _, _ = _prenorm(x_ref[...], vec_ref)
            h_ref[...] = h.astype(BF16)
            acc_ref[...] = jnp.zeros_like(acc_ref)

        hb = h_ref[...]
        g = _dot(hb, wg_ref[...])
        up = _dot(hb, wu_ref[...])
        a_ref[0] = g.astype(BF16)
        a_ref[1] = up.astype(BF16)
        act = (g * jax.nn.sigmoid(g)) * up
        acc_ref[...] += _dot(act.astype(BF16), wo_ref[...])

        @pl.when(j == 1)
        def _():
            u = acc_ref[...]
            u_ref[...] = u
            uhat, _ = _rms(u)
            xo_ref[...] = x_ref[...] + (weight * (1.0 + vec_ref[4:5, :])) * (uhat * vec_ref[1:2, :])

    return pl.pallas_call(
        body, name="ffn_fwd", grid=(S // tm, 2),
        in_specs=[pl.BlockSpec((tm, D), lambda i, j: (i, 0)),
                  pl.BlockSpec((8, D), lambda i, j: (0, 0)),
                  pl.BlockSpec((None, D, FSH), lambda i, j: (j, 0, 0)),
                  pl.BlockSpec((None, D, FSH), lambda i, j: (j + 2, 0, 0)),
                  pl.BlockSpec((None, FSH, D), lambda i, j: (j, 0, 0))],
        out_specs=[pl.BlockSpec((tm, D), lambda i, j: (i, 0)),
                   pl.BlockSpec((2, tm, FSH), lambda i, j: (0, i, j)),
                   pl.BlockSpec((tm, D), lambda i, j: (i, 0)),
                   pl.BlockSpec((tm, D), lambda i, j: (i, 0))],
        out_shape=[jax.ShapeDtypeStruct((S, D), F32), jax.ShapeDtypeStruct((2, S, DFF), BF16),
                   jax.ShapeDtypeStruct((S, D), F32), jax.ShapeDtypeStruct((S, D), BF16)],
        scratch_shapes=[pltpu.VMEM((tm, D), F32)],
        compiler_params=_params("parallel", "arbitrary"),
    )(x, vec, w_in, w_in, w_out)


def ffn_bwd(dout, x, u, a, vec, w_in, w_out, weight):
    S = x.shape[0]
    tm = min(256, S)

    def body(do_ref, x_ref, u_ref, a_ref, vec_ref, wg_ref, wu_ref, wo_ref,
             dx_ref, du_ref, act_ref, da_ref, vg_ref, dh_ref):
        i, j = pl.program_id(0), pl.program_id(1)

        @pl.when((i == 0) & (j == 0))
        def _():
            vg_ref[...] = jnp.zeros_like(vg_ref)

        @pl.when(j == 0)
        def _():
            du, dgate_rows, dgpost_rows = _postnorm_bwd(do_ref[...], u_ref[...], vec_ref, weight)
            vg_ref[2:3, :] += jnp.sum(dgate_rows, axis=0, keepdims=True)
            vg_ref[4:5, :] += jnp.sum(dgpost_rows, axis=0, keepdims=True)
            du_ref[...] = du.astype(BF16)
            dh_ref[...] = jnp.zeros_like(dh_ref)

        dact = _dot(du_ref[...], wo_ref[...], NT)
        g = a_ref[0].astype(F32)
        up = a_ref[1].astype(F32)
        s = jax.nn.sigmoid(g)
        silu = g * s
        act_ref[...] = (silu * up).astype(BF16)
        dg = (dact * up * (s * (1.0 + g * (1.0 - s)))).astype(BF16)
        dup = (dact * silu).astype(BF16)
        da_ref[0] = dg
        da_ref[1] = dup
        dh_ref[...] += _dot(dg, wg_ref[...], NT) + _dot(dup, wu_ref[...], NT)

        @pl.when(j == 1)
        def _():
            dx_ref[...] = do_ref[...] + _prenorm_bwd(dh_ref[...], x_ref[...], vec_ref, vg_ref)

    row = lambda i, j: (i, 0)
    return pl.pallas_call(
        body, name="ffn_bwd", grid=(S // tm, 2),
        in_specs=[pl.BlockSpec((tm, D), row), pl.BlockSpec((tm, D), row), pl.BlockSpec((tm, D), row),
                  pl.BlockSpec((2, tm, FSH), lambda i, j: (0, i, j)),
                  pl.BlockSpec((8, D), lambda i, j: (0, 0)),
                  pl.BlockSpec((None, D, FSH), lambda i, j: (j, 0, 0)),
                  pl.BlockSpec((None, D, FSH), lambda i, j: (j + 2, 0, 0)),
                  pl.BlockSpec((None, FSH, D), lambda i, j: (j, 0, 0))],
        out_specs=[pl.BlockSpec((tm, D), row), pl.BlockSpec((tm, D), row),
                   pl.BlockSpec((tm, FSH), lambda i, j: (i, j)),
                   pl.BlockSpec((2, tm, FSH), lambda i, j: (0, i, j)),
                   pl.BlockSpec((8, D), lambda i, j: (0, 0))],
        out_shape=[jax.ShapeDtypeStruct((S, D), F32), jax.ShapeDtypeStruct((S, D), BF16),
                   jax.ShapeDtypeStruct((S, DFF), BF16), jax.ShapeDtypeStruct((2, S, DFF), BF16),
                   jax.ShapeDtypeStruct((8, D), F32)],
        scratch_shapes=[pltpu.VMEM((tm, D), F32)],
        compiler_params=_params("arbitrary", "arbitrary"),
    )(dout, x, u, a, vec, w_in, w_in, w_out)


def dw_matmul(name, a, b, a_spec, b_spec, out_shape, out_spec, grid):
    def body(a_ref, b_ref, o_ref):
        @pl.when(pl.program_id(len(grid) - 1) == 0)
        def _():
            o_ref[...] = jnp.zeros_like(o_ref)

        o_ref[...] += _dot(a_ref[...], b_ref[...], TN)

    return pl.pallas_call(
        body, name=name, grid=grid, in_specs=[a_spec, b_spec], out_specs=out_spec,
        out_shape=jax.ShapeDtypeStruct(out_shape, F32),
        compiler_params=_params(*(["parallel"] * (len(grid) - 1) + ["arbitrary"])),
    )(a, b)


def ffn_dw(h, da, act, du):
    S = h.shape[0]
    tk = min(512, S)
    dw_in = dw_matmul("ffn_dw_in", h, da,
                      pl.BlockSpec((tk, D), lambda n, k: (k, 0)),
                      pl.BlockSpec((None, tk, FSH), lambda n, k: (n // 2, k, n % 2)),
                      (N_CHIP, D, FSH), pl.BlockSpec((None, D, FSH), lambda n, k: (n, 0, 0)),
                      (N_CHIP, S // tk))
    dw_out = dw_matmul("ffn_dw_out", act, du,
                       pl.BlockSpec((tk, FSH), lambda n, k: (k, n)),
                       pl.BlockSpec((tk, D), lambda n, k: (k, 0)),
                       (DFF, D), pl.BlockSpec((FSH, D), lambda n, k: (n, 0)),
                       (2, S // tk))
    return dw_in, dw_out


def _halo_specs(tm, S):
    nb = tm // HALO
    last = S // HALO - 1
    return [pl.BlockSpec((HALO, D), lambda i: (jnp.maximum(i * nb - 1, 0), 0)),
            pl.BlockSpec((tm, D), lambda i: (i, 0)),
            pl.BlockSpec((HALO, D), lambda i: (jnp.minimum((i + 1) * nb, last), 0))]


def _shift_rows(v, k):
    return pltpu.roll(v, k % v.shape[0], 0)


def _window_sum(v, g, forward):
    acc = v + _shift_rows(v, 1 if forward else -1)
    for step in (1, 2, 4)[:g]:
        acc = _shift_rows(acc, step) + _shift_rows(acc, -step)
    return acc


def _pool_count(t, w, S):
    return jnp.maximum(jnp.minimum(t + w // 2, S) - jnp.maximum(t - w // 2, 0), 1).astype(F32)


def pool_fwd(x, vec, pw, pvec):
    S = x.shape[0]
    tm = min(256, S)
    G = D // 4

    def body(xp_ref, x_ref, xn_ref, vec_ref, pw_ref, pv_ref, xo_ref, y_ref, z_ref):
        i = pl.program_id(0)
        xa = jnp.concatenate([xp_ref[...], x_ref[...], xn_ref[...]], axis=0)
        t = i * tm - HALO + lax.broadcasted_iota(jnp.int32, (tm + 2 * HALO, 1), 0)
        h, _, _ = _prenorm(xa, vec_ref)
        h = jnp.where((t >= 0) & (t < S), h, 0.0)
        tmain = t[HALO:HALO + tm]
        for g in range(4):
            hg = h[:, g * G:(g + 1) * G]
            pooled = _window_sum(hg, g, True)[HALO:HALO + tm] / _pool_count(tmain, POOL_WINDOWS[g], S)
            z = (pooled - hg[HALO:HALO + tm]).astype(BF16)
            z_ref[:, g * G:(g + 1) * G] = z
            y_ref[:, g * G:(g + 1) * G] = _dot(z, pw_ref[g]) + pv_ref[0:1, g * G:(g + 1) * G]
        u = y_ref[...] * pv_ref[1:2, :]
        uhat, _ = _rms(u)
        xo_ref[...] = x_ref[...] + (1.0 + vec_ref[4:5, :]) * (uhat * vec_ref[1:2, :])

    row = lambda i: (i, 0)
    full = lambda i: (0, 0)
    return pl.pallas_call(
        body, name="pool_fwd", grid=(S // tm,),
        in_specs=_halo_specs(tm, S) + [pl.BlockSpec((8, D), full), pl.BlockSpec((4, G, G), lambda i: (0, 0, 0)),
                                       pl.BlockSpec((8, D), full)],
        out_specs=[pl.BlockSpec((tm, D), row)] * 3,
        out_shape=[jax.ShapeDtypeStruct((S, D), F32), jax.ShapeDtypeStruct((S, D), F32),
                   jax.ShapeDtypeStruct((S, D), BF16)],
        compiler_params=_params("parallel"),
    )(x, x, x, vec, pw, pvec)


def pool_bwd(dout, x, y, z, vec, pw, pvec):
    S = x.shape[0]
    tm = min(256, S)
    G = D // 4
    R = G // N_CHIP

    def body(dop_ref, do_ref, don_ref, yp_ref, y_ref, yn_ref, x_ref, z_ref, vec_ref, pw_ref, pv_ref,
             dx_ref, vg_ref, pg_ref, dw_ref, dh_ref):
        i = pl.program_id(0)

        @pl.when(i == 0)
        def _():
            vg_ref[...] = jnp.zeros_like(vg_ref)
            pg_ref[...] = jnp.zeros_like(pg_ref)
            dw_ref[...] = jnp.zeros_like(dw_ref)

        doa = jnp.concatenate([dop_ref[...], do_ref[...], don_ref[...]], axis=0)
        ya = jnp.concatenate([yp_ref[...], y_ref[...], yn_ref[...]], axis=0)
        t = i * tm - HALO + lax.broadcasted_iota(jnp.int32, (tm + 2 * HALO, 1), 0)
        inside = (t >= 0) & (t < S)
        main = (t >= i * tm) & (t < (i + 1) * tm)
        du, dgate_rows, dgpost_rows = _postnorm_bwd(doa, ya * pv_ref[1:2, :], vec_ref, 1.0)
        du = jnp.where(inside, du, 0.0)
        vg_ref[2:3, :] += jnp.sum(jnp.where(main, dgate_rows, 0.0), axis=0, keepdims=True)
        vg_ref[4:5, :] += jnp.sum(jnp.where(main, dgpost_rows, 0.0), axis=0, keepdims=True)
        dy = du * pv_ref[1:2, :]
        pg_ref[0:1, :] += jnp.sum(jnp.where(main, dy, 0.0), axis=0, keepdims=True)
        pg_ref[1:2, :] += jnp.sum(jnp.where(main, du * ya, 0.0), axis=0, keepdims=True)
        for g in range(4):
            dyg = dy[:, g * G:(g + 1) * G].astype(BF16)
            dz = _dot(dyg, pw_ref[g], NT)
            e = dz / _pool_count(t, POOL_WINDOWS[g], S)
            dh_ref[:, g * G:(g + 1) * G] = (_window_sum(e, g, False) - dz)[HALO:HALO + tm]
            dwg = _dot(z_ref[:, g * G:(g + 1) * G], dyg[HALO:HALO + tm], TN)
            for q in range(N_CHIP):
                dw_ref[q, g] += dwg[q * R:(q + 1) * R, :]
        dx_ref[...] = do_ref[...] + _prenorm_bwd(dh_ref[...], x_ref[...], vec_ref, vg_ref)

    row = lambda i: (i, 0)
    full = lambda i: (0, 0)
    halo = _halo_specs(tm, S)
    return pl.pallas_call(
        body, name="pool_bwd", grid=(S // tm,),
        in_specs=halo + halo + [pl.BlockSpec((tm, D), row), pl.BlockSpec((tm, D), row), pl.BlockSpec((8, D), full),
                                pl.BlockSpec((4, G, G), lambda i: (0, 0, 0)), pl.BlockSpec((8, D), full)],
        out_specs=[pl.BlockSpec((tm, D), row), pl.BlockSpec((8, D), full), pl.BlockSpec((8, D), full),
                   pl.BlockSpec((N_CHIP, 4, R, G), lambda i: (0, 0, 0, 0))],
        out_shape=[jax.ShapeDtypeStruct((S, D), F32), jax.ShapeDtypeStruct((8, D), F32),
                   jax.ShapeDtypeStruct((8, D), F32), jax.ShapeDtypeStruct((N_CHIP, 4, R, G), F32)],
        scratch_shapes=[pltpu.VMEM((tm, D), F32)],
        compiler_params=_params("arbitrary"),
    )(dout, dout, dout, y, y, y, x, z, vec, pw, pvec)


N_PAIR = N_HEADS // 2
SLOTS = 128 // ROPE
ROPE_ALL = N_HEADS * ROPE
NOPE_ALL = N_HEADS * NOPE
LAT_ALL = N_HEADS * KVL
DLAT = QL + KVL + 2 * 128
DQ_ALL = NOPE_ALL + 2 * ROPE_ALL


def _w3(shape):
    return pl.BlockSpec(shape, lambda i: (0,) * len(shape))


def _slot_mask(hd, rows):
    lane = lax.broadcasted_iota(jnp.int32, (rows, 128), 1)
    return (lane // ROPE) == (hd % SLOTS)


MLA_WEIGHTS = ("wq", "wkv", "wkr4", "wkrs4", "qn", "kvn", "wn", "wr", "wrs", "bduk")


def _mla_weight_specs():
    return [_w3((D, QL)), _w3((D, KVL)), _w3((D, 128)), _w3((D, 128)), _w3((1, QL)), _w3((1, KVL)),
            _w3((QL, NOPE_ALL)), _w3((QL, ROPE_ALL)), _w3((QL, ROPE_ALL)), _w3((N_PAIR, 2 * NOPE, 2 * KVL))]


def mla_pre(x, vec, mw, tabs):
    S = x.shape[0]
    tm = min(256, S)

    def body(x_ref, vec_ref, cos_ref, sin_ref, wq_ref, wkv_ref, wkr_ref, wkrs_ref, qn_ref, kvn_ref,
             wn_ref, wr_ref, wrs_ref, bduk_ref,
             h_ref, cq_ref, ckv_ref, cqn_ref, qnope_ref, qcat_ref, kcat_ref, vcat_ref):
        h, _, _ = _prenorm(x_ref[...], vec_ref)
        hb = h.astype(BF16)
        h_ref[...] = hb
        cq_raw = _dot(hb, wq_ref[...])
        ckv_raw = _dot(hb, wkv_ref[...])
        cq_ref[...] = cq_raw
        ckv_ref[...] = ckv_raw
        cos, sin = cos_ref[...], sin_ref[...]
        ckv = (_rms(ckv_raw)[0] * kvn_ref[...]).astype(BF16)
        kcat_ref[:, 0:KVL] = ckv
        kcat_ref[:, KVL:] = (_dot(hb, wkr_ref[...]) * cos + _dot(hb, wkrs_ref[...]) * sin).astype(BF16)
        vcat_ref[:, 0:KVL] = ckv
        ones = lax.broadcasted_iota(jnp.int32, (tm, QPAD - KVL), 1) == 0
        vcat_ref[:, KVL:] = jnp.where(ones, 1.0, 0.0).astype(BF16)
        cqb = (_rms(cq_raw)[0] * qn_ref[...]).astype(BF16)
        cqn_ref[...] = cqb
        qn = _dot(cqb, wn_ref[...]).astype(BF16)
        qnope_ref[...] = qn
        cos4, sin4 = jnp.tile(cos, (1, SLOTS)), jnp.tile(sin, (1, SLOTS))
        qr = ((_dot(cqb, wr_ref[...]) * cos4 + _dot(cqb, wrs_ref[...]) * sin4) * ATTN_SCALE).astype(BF16)
        for j in range(N_PAIR):
            ql = (_dot(qn[:, 128 * j:128 * (j + 1)], bduk_ref[j]) * ATTN_SCALE).astype(BF16)
            for hd in (2 * j, 2 * j + 1):
                qcat_ref[hd, :, 0:KVL] = ql[:, KVL * (hd - 2 * j):KVL * (hd - 2 * j + 1)]
                group = qr[:, 128 * (hd // SLOTS):128 * (hd // SLOTS + 1)]
                qcat_ref[hd, :, KVL:] = jnp.where(_slot_mask(hd, tm), group, jnp.zeros_like(group))

    row = lambda i: (i, 0)
    hrow = lambda i: (0, i, 0)
    return pl.pallas_call(
        body, name="mla_pre", grid=(S // tm,),
        in_specs=[pl.BlockSpec((tm, D), row), _w3((8, D)), pl.BlockSpec((tm, 128), row), pl.BlockSpec((tm, 128), row)]
        + _mla_weight_specs(),
        out_specs=[pl.BlockSpec((tm, D), row), pl.BlockSpec((tm, QL), row), pl.BlockSpec((tm, KVL), row),
                   pl.BlockSpec((tm, QL), row), pl.BlockSpec((tm, NOPE_ALL), row),
                   pl.BlockSpec((N_HEADS, tm, QPAD), hrow), pl.BlockSpec((tm, QPAD), row),
                   pl.BlockSpec((tm, QPAD), row)],
        out_shape=[jax.ShapeDtypeStruct((S, D), BF16), jax.ShapeDtypeStruct((S, QL), F32),
                   jax.ShapeDtypeStruct((S, KVL), F32), jax.ShapeDtypeStruct((S, QL), BF16),
                   jax.ShapeDtypeStruct((S, NOPE_ALL), BF16), jax.ShapeDtypeStruct((N_HEADS, S, QPAD), BF16),
                   jax.ShapeDtypeStruct((S, QPAD), BF16), jax.ShapeDtypeStruct((S, QPAD), BF16)],
        compiler_params=_params("parallel"),
    )(x, vec, tabs[0], tabs[1], *[mw[k] for k in MLA_WEIGHTS])


def attn_fwd(qcat, kcat, vcat):
    S = kcat.shape[0]
    tq = min(ATTN_TQ, S)
    kc = min(ATTN_KC, S)

    def body(q_ref, k_ref, v_ref, o_ref, lse_ref):
        q = q_ref[...]
        m = jnp.full((tq, 1), -jnp.inf, F32)
        ov = jnp.zeros((tq, QPAD), F32)
        for c in range(S // kc):
            s = _dot(q, k_ref[c * kc:(c + 1) * kc, :], NT)
            m_new = jnp.maximum(m, jnp.max(s, axis=-1, keepdims=True))
            p = jnp.exp(s - m_new).astype(BF16)
            ov = ov * jnp.exp(m - m_new) + _dot(p, v_ref[c * kc:(c + 1) * kc, :])
            m = m_new
        l = ov[:, KVL:KVL + 1]
        o_ref[...] = (ov[:, 0:KVL] * (1.0 / l)).astype(BF16)
        lse_ref[...] = _as_row(m + jnp.log(l))

    return pl.pallas_call(
        body, name="attn_fwd", grid=(N_HEADS, S // tq),
        in_specs=[pl.BlockSpec((None, tq, QPAD), lambda h, i: (h, i, 0)),
                  pl.BlockSpec((S, QPAD), lambda h, i: (0, 0)),
                  pl.BlockSpec((S, QPAD), lambda h, i: (0, 0))],
        out_specs=[pl.BlockSpec((tq, KVL), lambda h, i: (i, h)),
                   pl.BlockSpec((None, 1, tq), lambda h, i: (h, 0, i))],
        out_shape=[jax.ShapeDtypeStruct((S, LAT_ALL), BF16), jax.ShapeDtypeStruct((N_HEADS, 1, S), F32)],
        compiler_params=_params("parallel", "parallel"),
    )(qcat, kcat, vcat)


def mla_post(olat, x, vec, bduv, wo):
    S = x.shape[0]
    tm = min(256, S)

    def body(o_ref, x_ref, vec_ref, bduv_ref, wo_ref, xo_ref, u_ref, ocat_ref):
        for j in range(N_PAIR):
            oc = _dot(o_ref[:, 2 * KVL * j:2 * KVL * (j + 1)], bduv_ref[j])
            ocat_ref[:, 2 * VH * j:2 * VH * (j + 1)] = oc.astype(BF16)
        u = _dot(ocat_ref[...], wo_ref[...])
        u_ref[...] = u
        uhat, _ = _rms(u)
        xo_ref[...] = x_ref[...] + (1.0 + vec_ref[4:5, :]) * (uhat * vec_ref[1:2, :])

    row = lambda i: (i, 0)
    return pl.pallas_call(
        body, name="mla_post", grid=(S // tm,),
        in_specs=[pl.BlockSpec((tm, LAT_ALL), row), pl.BlockSpec((tm, D), row), _w3((8, D)),
                  _w3((N_PAIR, 2 * KVL, 2 * VH)), _w3((D, D))],
        out_specs=[pl.BlockSpec((tm, D), row), pl.BlockSpec((tm, D), row), pl.BlockSpec((tm, D), row)],
        out_shape=[jax.ShapeDtypeStruct((S, D), F32), jax.ShapeDtypeStruct((S, D), F32),
                   jax.ShapeDtypeStruct((S, D), BF16)],
        compiler_params=_params("parallel"),
    )(olat, x, vec, bduv, wo)


def mla_post_bwd(dout, u, olat, vec, bduv, wo):
    S = u.shape[0]
    tm = min(256, S)

    def body(do_ref, u_ref, o_ref, vec_ref, bduv_ref, wo_ref, du_ref, docat_ref, dolat_ref, delta_ref, vg_ref):
        @pl.when(pl.program_id(0) == 0)
        def _():
            vg_ref[...] = jnp.zeros_like(vg_ref)

        du, dgate_rows, dgpost_rows = _postnorm_bwd(do_ref[...], u_ref[...], vec_ref, 1.0)
        vg_ref[2:3, :] += jnp.sum(dgate_rows, axis=0, keepdims=True)
        vg_ref[4:5, :] += jnp.sum(dgpost_rows, axis=0, keepdims=True)
        dub = du.astype(BF16)
        du_ref[...] = dub
        docat_ref[...] = _dot(dub, wo_ref[...], NT).astype(BF16)
        for j in range(N_PAIR):
            dol = _dot(docat_ref[:, 2 * VH * j:2 * VH * (j + 1)], bduv_ref[j], NT).astype(BF16)
            dolat_ref[:, 2 * KVL * j:2 * KVL * (j + 1)] = dol
            prod = dol.astype(F32) * o_ref[:, 2 * KVL * j:2 * KVL * (j + 1)].astype(F32)
            delta_ref[2 * j] = _as_row(jnp.sum(prod[:, 0:KVL], axis=-1, keepdims=True))
            delta_ref[2 * j + 1] = _as_row(jnp.sum(prod[:, KVL:], axis=-1, keepdims=True))

    row = lambda i: (i, 0)
    hrow = lambda i: (0, i, 0)
    return pl.pallas_call(
        body, name="mla_post_bwd", grid=(S // tm,),
        in_specs=[pl.BlockSpec((tm, D), row), pl.BlockSpec((tm, D), row), pl.BlockSpec((tm, LAT_ALL), row),
                  _w3((8, D)), _w3((N_PAIR, 2 * KVL, 2 * VH)), _w3((D, D))],
        out_specs=[pl.BlockSpec((tm, D), row), pl.BlockSpec((tm, D), row),
                   pl.BlockSpec((tm, LAT_ALL), row), pl.BlockSpec((N_HEADS, 1, tm), lambda i: (0, 0, i)), _w3((8, D))],
        out_shape=[jax.ShapeDtypeStruct((S, D), BF16), jax.ShapeDtypeStruct((S, D), BF16),
                   jax.ShapeDtypeStruct((S, LAT_ALL), BF16), jax.ShapeDtypeStruct((N_HEADS, 1, S), F32),
                   jax.ShapeDtypeStruct((8, D), F32)],
        compiler_params=_params("arbitrary"),
    )(dout, u, olat, vec, bduv, wo)


def attn_bwd(qcat, kcat, kcat_t, dolat, lse_row, delta_row):
    S = kcat.shape[0]
    tq = min(ATTN_TQ, S)
    kc = min(ATTN_KC, S)

    def body(q_ref, k_ref, kt_ref, do_ref, lse_ref, dl_ref, dq_ref, dk_ref, dv_ref):
        @pl.when((pl.program_id(0) == 0) & (pl.program_id(1) == 0))
        def _():
            dk_ref[...] = jnp.zeros_like(dk_ref)
            dv_ref[...] = jnp.zeros_like(dv_ref)

        q, do = q_ref[...], do_ref[...]
        lse, dl = lse_ref[...], dl_ref[...]
        dqt = jnp.zeros((QPAD, tq), F32)
        for c in range(S // kc):
            rows = slice(c * kc, (c + 1) * kc)
            st = _dot(k_ref[rows, :], q, NT)
            pt = jnp.exp(st - lse)
            dpt = _dot(k_ref[rows, 0:KVL], do, NT)
            dst = (pt * (dpt - dl)).astype(BF16)
            dv_ref[rows, :] += _dot(pt.astype(BF16), do)
            dk_ref[rows, :] += _dot(dst, q)
            dqt = dqt + _dot(kt_ref[:, rows], dst)
        dq_ref[...] = dqt.T

    return pl.pallas_call(
        body, name="attn_bwd", grid=(N_HEADS, S // tq),
        in_specs=[pl.BlockSpec((None, tq, QPAD), lambda h, i: (h, i, 0)),
                  pl.BlockSpec((S, QPAD), lambda h, i: (0, 0)),
                  pl.BlockSpec((QPAD, S), lambda h, i: (0, 0)),
                  pl.BlockSpec((tq, KVL), lambda h, i: (i, h)),
                  pl.BlockSpec((None, 1, tq), lambda h, i: (h, 0, i)),
                  pl.BlockSpec((None, 1, tq), lambda h, i: (h, 0, i))],
        out_specs=[pl.BlockSpec((None, tq, QPAD), lambda h, i: (h, i, 0)),
                   pl.BlockSpec((S, QPAD), lambda h, i: (0, 0)),
                   pl.BlockSpec((S, KVL), lambda h, i: (0, 0))],
        out_shape=[jax.ShapeDtypeStruct((N_HEADS, S, QPAD), F32), jax.ShapeDtypeStruct((S, QPAD), F32),
                   jax.ShapeDtypeStruct((S, KVL), F32)],
        compiler_params=_params("arbitrary", "arbitrary"),
    )(qcat, kcat, kcat_t, dolat, lse_row, delta_row)


def mla_pre_bwd(dout, dq, dk, dv, x, cq_raw, ckv_raw, vec, mw, tabs):
    S = x.shape[0]
    tm = min(256, S)

    def body(do_ref, dq_ref, dk_ref, dv_ref, x_ref, cq_ref, ckv_ref, vec_ref, cos_ref, sin_ref,
             wq_ref, wkv_ref, wkr_ref, wkrs_ref, qn_ref, kvn_ref, wn_ref, wr_ref, wrs_ref, bduk_ref,
             dx_ref, dlat_ref, dql_ref, dqcat_ref, vg_ref, ng_ref):
        @pl.when(pl.program_id(0) == 0)
        def _():
            vg_ref[...] = jnp.zeros_like(vg_ref)
            ng_ref[...] = jnp.zeros_like(ng_ref)

        cos, sin = cos_ref[...], sin_ref[...]
        for j in range(N_PAIR):
            dql = jnp.concatenate([dq_ref[2 * j, :, 0:KVL], dq_ref[2 * j + 1, :, 0:KVL]], axis=1) * ATTN_SCALE
            dql = dql.astype(BF16)
            dql_ref[:, 2 * KVL * j:2 * KVL * (j + 1)] = dql
            dqcat_ref[:, 2 * NOPE * j:2 * NOPE * (j + 1)] = _dot(dql, bduk_ref[j], NT).astype(BF16)
        groups = []
        for grp in range(N_HEADS // SLOTS):
            acc = jnp.zeros((tm, 128), F32)
            for hd in range(SLOTS * grp, SLOTS * (grp + 1)):
                acc = acc + jnp.where(_slot_mask(hd, tm), dq_ref[hd, :, KVL:], 0.0)
            groups.append(acc)
        dqr = jnp.concatenate(groups, axis=1) * ATTN_SCALE
        qa = (dqr * jnp.tile(cos, (1, SLOTS))).astype(BF16)
        qb = (dqr * jnp.tile(sin, (1, SLOTS))).astype(BF16)
        dqcat_ref[:, NOPE_ALL:NOPE_ALL + ROPE_ALL] = qa
        dqcat_ref[:, NOPE_ALL + ROPE_ALL:] = qb
        dcq = _dot(dqcat_ref[:, 0:NOPE_ALL], wn_ref[...], NT) + _dot(qa, wr_ref[...], NT) + _dot(qb, wrs_ref[...], NT)
        cqh, rq = _rms(cq_ref[...])
        ng_ref[0:1, :] += jnp.sum(dcq * cqh, axis=0, keepdims=True)
        dcq_raw = _rms_bwd(cqh, rq, dcq * qn_ref[...]).astype(BF16)
        dckv = dk_ref[:, 0:KVL] + dv_ref[...]
        ckvh, rk = _rms(ckv_ref[...])
        ng_ref[1:2, 0:KVL] += jnp.sum(dckv * ckvh, axis=0, keepdims=True)
        dckv_raw = _rms_bwd(ckvh, rk, dckv * kvn_ref[...]).astype(BF16)
        dkr = dk_ref[:, KVL:]
        ka = (dkr * cos).astype(BF16)
        kb = (dkr * sin).astype(BF16)
        dlat_ref[:, 0:QL] = dcq_raw
        dlat_ref[:, QL:QL + KVL] = dckv_raw
        dlat_ref[:, QL + KVL:QL + KVL + 128] = ka
        dlat_ref[:, QL + KVL + 128:] = kb
        dh = (_dot(dcq_raw, wq_ref[...], NT) + _dot(dckv_raw, wkv_ref[...], NT)
              + _dot(ka, wkr_ref[...], NT) + _dot(kb, wkrs_ref[...], NT))
        dx_ref[...] = do_ref[...] + _prenorm_bwd(dh, x_ref[...], vec_ref, vg_ref)

    row = lambda i: (i, 0)
    hrow = lambda i: (0, i, 0)
    return pl.pallas_call(
        body, name="mla_pre_bwd", grid=(S // tm,),
        in_specs=[pl.BlockSpec((tm, D), row), pl.BlockSpec((N_HEADS, tm, QPAD), hrow), pl.BlockSpec((tm, QPAD), row),
                  pl.BlockSpec((tm, KVL), row), pl.BlockSpec((tm, D), row), pl.BlockSpec((tm, QL), row),
                  pl.BlockSpec((tm, KVL), row), _w3((8, D)), pl.BlockSpec((tm, 128), row), pl.BlockSpec((tm, 128), row)]
        + _mla_weight_specs(),
        out_specs=[pl.BlockSpec((tm, D), row), pl.BlockSpec((tm, DLAT), row), pl.BlockSpec((tm, LAT_ALL), row),
                   pl.BlockSpec((tm, DQ_ALL), row), _w3((8, D)), _w3((8, QL))],
        out_shape=[jax.ShapeDtypeStruct((S, D), F32), jax.ShapeDtypeStruct((S, DLAT), BF16),
                   jax.ShapeDtypeStruct((S, LAT_ALL), BF16), jax.ShapeDtypeStruct((S, DQ_ALL), BF16),
                   jax.ShapeDtypeStruct((8, D), F32), jax.ShapeDtypeStruct((8, QL), F32)],
        compiler_params=_params("arbitrary"),
    )(dout, dq, dk, dv, x, cq_raw, ckv_raw, vec, tabs[0], tabs[1], *[mw[k] for k in MLA_WEIGHTS])


def mla_dw(h, dlat, cqn, dqcat, dql, qnope, olat, docat, ocat, du):
    S = h.shape[0]
    tk = min(512, S)
    nk = S // tk
    flat = lambda w: pl.BlockSpec((tk, w), lambda k: (k, 0))
    cols = lambda w: pl.BlockSpec((tk, w), lambda n, k: (k, n))
    pair_o = pl.BlockSpec((None, 2 * KVL, 128), lambda n, k: (n, 0, 0))
    g = {}
    g["in"] = dw_matmul("mla_dw_in", h, dlat, flat(D), flat(DLAT), (D, DLAT),
                        pl.BlockSpec((D, DLAT), lambda k: (0, 0)), (nk,))
    g["q"] = dw_matmul("mla_dw_q", cqn, dqcat, flat(QL), flat(DQ_ALL), (QL, DQ_ALL),
                       pl.BlockSpec((QL, DQ_ALL), lambda k: (0, 0)), (nk,))
    g["uk"] = dw_matmul("mla_dw_uk", dql, qnope, cols(2 * KVL), cols(2 * NOPE), (N_PAIR, 2 * KVL, 2 * NOPE), pair_o,
                        (N_PAIR, nk))
    g["uv"] = dw_matmul("mla_dw_uv", olat, docat, cols(2 * KVL), cols(2 * VH), (N_PAIR, 2 * KVL, 2 * VH), pair_o,
                        (N_PAIR, nk))
    g["o"] = dw_matmul("mla_dw_o", ocat, du, cols(256), pl.BlockSpec((tk, D), lambda n, k: (k, 0)), (D, D),
                       pl.BlockSpec((256, D), lambda n, k: (n, 0)), (D // 256, nk))
    return g


def loss_head(y, target):
    S = y.shape[0]
    tm = min(512, S)

    def body(y_ref, t_ref, loss_ref, dy_ref):
        @pl.when(pl.program_id(0) == 0)
        def _():
            loss_ref[...] = jnp.zeros_like(loss_ref)

        err = y_ref[...] - t_ref[...]
        dy_ref[...] = err * (1.0 / D)
        loss_ref[...] += 0.5 * jnp.sum(jnp.mean(err * err, axis=-1, keepdims=True), axis=0, keepdims=True)

    row = lambda i: (i, 0)
    return pl.pallas_call(
        body, name="loss_head", grid=(S // tm,),
        in_specs=[pl.BlockSpec((tm, D), row), pl.BlockSpec((tm, D), row)],
        out_specs=[pl.BlockSpec((1, 1), lambda i: (0, 0)), pl.BlockSpec((tm, D), row)],
        out_shape=[jax.ShapeDtypeStruct((1, 1), F32), jax.ShapeDtypeStruct((S, D), F32)],
        compiler_params=_params("arbitrary"),
    )(y, target)


MOD_COLS = 9 * D // N_CHIP


def mod_fwd(c_pad, ada_w, ada_b_loc):
    tn = MOD_COLS // 3

    def body(c_ref, w_ref, b_ref, o_ref):
        c = c_ref[...]
        sc = (c * jax.nn.sigmoid(c)).astype(BF16)
        o_ref[...] = _dot(sc, w_ref[...].astype(BF16)) + b_ref[...]

    return pl.pallas_call(
        body, name="mod_fwd", grid=(2, 3),
        in_specs=[pl.BlockSpec((16, D), lambda i, n: (0, 0)), pl.BlockSpec((None, D, tn), lambda i, n: (i, 0, n)),
                  pl.BlockSpec((None, 1, tn), lambda i, n: (i, 0, n))],
        out_specs=pl.BlockSpec((None, 16, tn), lambda i, n: (i, 0, n)),
        out_shape=jax.ShapeDtypeStruct((2, 16, MOD_COLS), F32),
        compiler_params=_params("parallel", "parallel"),
    )(c_pad, ada_w, ada_b_loc)


def _adamw_math(w, g, m, v):
    m = ADAM_B1 * m + (1.0 - ADAM_B1) * g
    v = ADAM_B2 * v + (1.0 - ADAM_B2) * (g * g)
    m_hat = m / (1.0 - ADAM_B1 ** ADAM_STEP)
    v_hat = v / (1.0 - ADAM_B2 ** ADAM_STEP)
    delta = -ADAM_LR * (m_hat / (jnp.sqrt(v_hat) + ADAM_EPS) + ADAM_WD * w)
    return delta, m, v


def adamw(name, w, g, m, v):
    shape = w.shape
    cols = shape[-1]
    rows = w.size // cols
    tr = rows
    for cand in (512, 256, 128, 64, 32, 16, 8):
        if rows % cand == 0 and cand * cols * 4 <= (2 << 20):
            tr = cand
            break

    def body(w_ref, g_ref, m_ref, v_ref, d_ref, mo_ref, vo_ref):
        d_ref[...], mo_ref[...], vo_ref[...] = _adamw_math(w_ref[...], g_ref[...], m_ref[...], v_ref[...])

    spec = pl.BlockSpec((tr, cols), lambda i: (i, 0))
    outs = pl.pallas_call(
        body, name=name, grid=(rows // tr,), in_specs=[spec] * 4, out_specs=[spec] * 3,
        out_shape=[jax.ShapeDtypeStruct((rows, cols), F32)] * 3,
        compiler_params=_params("parallel"),
    )(*[a.reshape(rows, cols) for a in (w, g, m, v)])
    return [o.reshape(shape) for o in outs]


def adamw_ada(c_pad, dmod, w, m, v):
    tr = 256

    def body(c_ref, dm_ref, w_ref, m_ref, v_ref, g_ref, d_ref, mo_ref, vo_ref):
        c = c_ref[...]
        sc = (c * jax.nn.sigmoid(c)).astype(BF16)
        g = _dot(sc, dm_ref[...].astype(BF16), TN)
        g_ref[...] = g
        d_ref[...], mo_ref[...], vo_ref[...] = _adamw_math(w_ref[...], g, m_ref[...], v_ref[...])

    wspec = pl.BlockSpec((None, tr, MOD_COLS), lambda i, r: (i, r, 0))
    return pl.pallas_call(
        body, name="adamw_ada", grid=(2, D // tr),
        in_specs=[pl.BlockSpec((16, tr), lambda i, r: (0, r)),
                  pl.BlockSpec((None, 16, MOD_COLS), lambda i, r: (i, 0, 0)), wspec, wspec, wspec],
        out_specs=[wspec] * 4,
        out_shape=[jax.ShapeDtypeStruct((2, D, MOD_COLS), F32)] * 4,
        compiler_params=_params("parallel", "parallel"),
    )(c_pad, dmod, w, m, v)


def sum_devices(name, a):
    _, R, C = a.shape
    tr = R
    for cand in (64, 32, 16, 8):
        if R % cand == 0:
            tr = cand
            break

    def body(a_ref, o_ref):
        acc = a_ref[0]
        for dev in range(1, N_DEV):
            acc = acc + a_ref[dev]
        o_ref[...] = acc

    return pl.pallas_call(
        body, name=name, grid=(R // tr,),
        in_specs=[pl.BlockSpec((N_DEV, tr, C), lambda i: (0, i, 0))],
        out_specs=pl.BlockSpec((tr, C), lambda i: (i, 0)),
        out_shape=jax.ShapeDtypeStruct((R, C), F32),
        compiler_params=_params("parallel"),
    )(a)


def _place():
    return lax.axis_index("x"), lax.axis_index("y"), lax.axis_index("c")


def _other_chips(x, y):
    return [(1 - x, y), (x, 1 - y), (1 - x, 1 - y)]


def gather_devices(name, a):
    m_per, n = a.shape

    def body(x_ref, out_ref, send_sems, recv_sems, local_sem):
        x, y, c = _place()
        me, sibling = (x, y, c), (x, y, 1 - c)
        chips = _other_chips(x, y)

        def rows(px, py, pc):
            return out_ref.at[pl.ds((4 * px + 2 * py + pc) * m_per, m_per), :]

        def copy(k, block, to, src=None):
            return pltpu.make_async_remote_copy(
                src_ref=rows(*block) if src is None else src, dst_ref=rows(*block),
                send_sem=send_sems.at[k], recv_sem=recv_sems.at[k], device_id=to, device_id_type=MESH)

        mine = pltpu.make_async_copy(x_ref, rows(*me), local_sem)
        mine.start()
        first = [copy(0, me, sibling, src=x_ref)]
        first += [copy(1 + j, me, (*chip, c), src=x_ref) for j, chip in enumerate(chips)]
        for cp in first:
            cp.start()
        passed = [copy(4 + j, (*chip, c), sibling) for j, chip in enumerate(chips)]
        for j, chip in enumerate(chips):
            copy(1 + j, (*chip, c), me).wait_recv()
            passed[j].start()
        copy(0, sibling, me).wait_recv()
        for j, chip in enumerate(chips):
            copy(4 + j, (*chip, 1 - c), me).wait_recv()
        for cp in first + passed:
            cp.wait_send()
        mine.wait()

    out = pl.pallas_call(
        body, name=name,
        out_shape=jax.ShapeDtypeStruct((N_DEV * m_per, n), a.dtype),
        in_specs=[pl.BlockSpec(memory_space=pltpu.VMEM)],
        out_specs=pl.BlockSpec(memory_space=pltpu.VMEM),
        scratch_shapes=[pltpu.SemaphoreType.DMA((7,)), pltpu.SemaphoreType.DMA((7,)), pltpu.SemaphoreType.DMA],
        compiler_params=pltpu.CompilerParams(vmem_limit_bytes=VMEM_LIMIT),
    )(a)
    return out.reshape(N_DEV, m_per, n)


_ANY = pl.BlockSpec(memory_space=pl.ANY)


def _hbm_ref(a):
    return jax.new_ref(a, memory_space=pltpu.MemorySpace.HBM)


def _hbm_empty(shape, dtype):
    return jax.empty_ref(jax.ShapeDtypeStruct(shape, dtype), memory_space=pltpu.MemorySpace.HBM)


ID_PAIR, ID_CHIPS, ID_SHARE, ID_UKV = 8, 9, 10, 11


def _sequencer(name, collective_id, n_sem, peers_of, program):
    sems = pltpu.SemaphoreType.DMA((n_sem,))

    @pl.kernel(mesh=plsc.ScalarSubcoreMesh(axis_name="seq", num_cores=1), name=name, scratch_types=[sems, sems],
               compiler_params=pltpu.CompilerParams(collective_id=collective_id))
    def launch(send_sem, recv_sem):
        x, y, c = _place()
        peers = peers_of(x, y, c)
        barrier = pltpu.get_barrier_semaphore()
        for peer in peers:
            pl.semaphore_signal(barrier, inc=1, device_id=peer, device_id_type=MESH)
        pl.semaphore_wait(barrier, len(peers))
        program(x, y, c, send_sem, recv_sem)

    launch()


def gather_weights(name, stage, arrays):
    n = len(arrays)
    refs = [_hbm_ref(a) for a in arrays]

    def program(x, y, c, send_sem, recv_sem):
        me = 2 * x + y
        chips = _other_chips(x, y)

        def ici(t, r, half):
            cx, cy = chips[r]
            mine = refs[t].at[me, half]
            return pltpu.make_async_remote_copy(
                src_ref=mine, dst_ref=mine, send_sem=send_sem.at[3 * t + r], recv_sem=recv_sem.at[3 * t + r],
                device_id=(cx, cy, c), device_id_type=MESH)

        def d2d(t, r, half):
            cx, cy = chips[r]
            there = refs[t].at[2 * cx + cy, half]
            k = 3 * n + 3 * t + r
            return pltpu.make_async_remote_copy(
                src_ref=there, dst_ref=there, send_sem=send_sem.at[k], recv_sem=recv_sem.at[k],
                device_id=(x, y, 1 - c), device_id_type=MESH)

        for t in range(n):
            for r in range(3):
                ici(t, r, c).start()
        for t in range(n):
            for r in range(3):
                ici(t, r, c).wait_recv()
                d2d(t, r, c).start()
        for t in range(n):
            for r in range(3):
                d2d(t, r, 1 - c).wait_recv()
        for t in range(n):
            for r in range(3):
                ici(t, r, c).wait_send()
                d2d(t, r, c).wait_send()

    _sequencer(name, stage, 6 * n, lambda x, y, c: [(x, y, 1 - c)] + [(cx, cy, c) for cx, cy in _other_chips(x, y)],
               program)
    return [r[...] for r in refs]


def cast_into_slots(name, chip, shards, after=None):
    steps = 2
    n = len(shards)

    def body(chip_ref, *refs):
        for src, dst in zip(refs[:n], refs[-n - 1:-1]):
            dst[...] = src[...].astype(BF16)
        refs[-1][...] = jnp.zeros_like(refs[-1])

    token_spec = pl.BlockSpec((8, 128), lambda h, i, chip_ref: (0, 0))

    def spec_in(a, prefix):
        R, C = a.shape[-2:]
        return pl.BlockSpec((None,) * (len(prefix) + 1) + (R // steps, C), lambda h, i, chip_ref: prefix + (h, i, 0))

    def spec_out(a):
        R, C = a.shape[-2:]
        return pl.BlockSpec((None, None, R // steps, C), lambda h, i, chip_ref: (chip_ref[0], h, i, 0))

    outs = pl.pallas_call(
        body, name=name,
        grid_spec=pltpu.PrefetchScalarGridSpec(
            num_scalar_prefetch=1, grid=(2, steps),
            in_specs=[spec_in(a, p) for a, p in shards] + ([token_spec] if after is not None else []),
            out_specs=[spec_out(a) for a, _ in shards] + [token_spec]),
        out_shape=[jax.ShapeDtypeStruct((N_CHIP, 2) + a.shape[-2:], BF16) for a, _ in shards]
        + [jax.ShapeDtypeStruct((8, 128), F32)],
        compiler_params=_params("arbitrary", "arbitrary"),
    )(chip, *[a for a, _ in shards], *([after] if after is not None else []))
    return outs[:-1], outs[-1]


def reduce_pair(name, grads):
    n = len(grads)
    src = [_hbm_ref(g) for g in grads]
    dst = [_hbm_empty((N_CHIP,) + g.shape[2:], g.dtype) for g in grads]

    def program(x, y, c, send_sem, recv_sem):
        cps = [pltpu.make_async_remote_copy(
            src_ref=src[t].at[:, 1 - c], dst_ref=dst[t], send_sem=send_sem.at[t], recv_sem=recv_sem.at[t],
            device_id=(x, y, 1 - c), device_id_type=MESH) for t in range(n)]
        for cp in cps:
            cp.start()
        for cp in cps:
            cp.wait()

    _sequencer(name, ID_PAIR, n, lambda x, y, c: [(x, y, 1 - c)], program)
    return [r[...] for r in src], [r[...] for r in dst]


def pair_add(name, core, g, got):
    _, _, R, C = g.shape

    def body(core_ref, g_ref, got_ref, o_ref, token_ref):
        o_ref[...] = (g_ref[...] + got_ref[...]).astype(BF16)
        token_ref[...] = jnp.zeros_like(token_ref)

    return pl.pallas_call(
        body, name=name,
        grid_spec=pltpu.PrefetchScalarGridSpec(
            num_scalar_prefetch=1, grid=(N_CHIP,),
            in_specs=[pl.BlockSpec((None, None, R, C), lambda q, core_ref: (q, core_ref[0], 0, 0)),
                      pl.BlockSpec((None, R, C), lambda q, core_ref: (q, 0, 0))],
            out_specs=[pl.BlockSpec((None, R, C), lambda q, core_ref: (q, 0, 0)),
                       pl.BlockSpec((8, 128), lambda q, core_ref: (0, 0))]),
        out_shape=[jax.ShapeDtypeStruct((N_CHIP, R, C), BF16), jax.ShapeDtypeStruct((8, 128), F32)],
        compiler_params=_params("arbitrary"),
    )(core, g, got)


def reduce_chips(name, sums):
    n = len(sums)
    src = [_hbm_ref(s) for s in sums]
    dst = [_hbm_empty((3,) + s.shape[1:], s.dtype) for s in sums]

    def program(x, y, c, send_sem, recv_sem):
        cps = []
        for t in range(n):
            for r, (cx, cy) in enumerate(_other_chips(x, y)):
                cps.append(pltpu.make_async_remote_copy(
                    src_ref=src[t].at[2 * cx + cy], dst_ref=dst[t].at[r],
                    send_sem=send_sem.at[3 * t + r], recv_sem=recv_sem.at[3 * t + r],
                    device_id=(cx, cy, c), device_id_type=MESH))
        for cp in cps:
            cp.start()
        for cp in cps:
            cp.wait()

    _sequencer(name, ID_CHIPS, 3 * n, lambda x, y, c: [(cx, cy, c) for cx, cy in _other_chips(x, y)], program)
    return [r[...] for r in src], [r[...] for r in dst]


def chip_add(name, place, s, got, k, n_slots, prev=None):
    _, R, C = s.shape

    def body(place_ref, s_ref, got_ref, *rest):
        o_ref, token_ref = rest[-2:]
        o_ref[...] = ((s_ref[...].astype(F32) + got_ref[0].astype(F32)) + got_ref[1].astype(F32)) + got_ref[2].astype(F32)
        token_ref[...] = jnp.zeros_like(token_ref)

    in_specs = [pl.BlockSpec((None, R, C), lambda i, place_ref: (place_ref[0], 0, 0)),
                pl.BlockSpec((3, R, C), lambda i, place_ref: (0, 0, 0))]
    args = [place, s, got]
    aliases = {}
    if prev is not None:
        in_specs.append(_ANY)
        args.append(prev)
        aliases = {3: 0}
    return pl.pallas_call(
        body, name=name,
        grid_spec=pltpu.PrefetchScalarGridSpec(
            num_scalar_prefetch=1, grid=(1,), in_specs=in_specs,
            out_specs=[pl.BlockSpec((None, None, R, C), lambda i, place_ref: (k, place_ref[1], 0, 0)),
                       pl.BlockSpec((8, 128), lambda i, place_ref: (0, 0))]),
        out_shape=[jax.ShapeDtypeStruct((n_slots, 2, R, C), F32), jax.ShapeDtypeStruct((8, 128), F32)],
        input_output_aliases=aliases,
        compiler_params=_params("arbitrary"),
    )(*args)


def share_halves(name, stacks, slots):
    n = len(stacks)
    dst = [_hbm_ref(s) for s in stacks]

    def program(x, y, c, send_sem, recv_sem):
        cps = [pltpu.make_async_remote_copy(
            src_ref=dst[t].at[slots[t], c], dst_ref=dst[t].at[slots[t], c],
            send_sem=send_sem.at[t], recv_sem=recv_sem.at[t],
            device_id=(x, y, 1 - c), device_id_type=MESH) for t in range(n)]
        for cp in cps:
            cp.start()
        for cp in cps:
            cp.wait()

    _sequencer(name, ID_SHARE, n, lambda x, y, c: [(x, y, 1 - c)], program)
    return [r[...] for r in dst]


def gather_blocks(name, slotted):
    out = _hbm_ref(slotted)

    def program(x, y, c, send_sem, recv_sem):
        sibling = (x, y, 1 - c)
        chips = _other_chips(x, y)

        def copy(k, px, py, pc, to):
            block = out.at[4 * px + 2 * py + pc]
            return pltpu.make_async_remote_copy(src_ref=block, dst_ref=block, send_sem=send_sem.at[k],
                                                recv_sem=recv_sem.at[k], device_id=to, device_id_type=MESH)

        first = [copy(0, x, y, c, sibling)] + [copy(1 + j, x, y, c, (cx, cy, c)) for j, (cx, cy) in enumerate(chips)]
        for cp in first:
            cp.start()
        passed = [copy(4 + j, cx, cy, c, sibling) for j, (cx, cy) in enumerate(chips)]
        for j, (cx, cy) in enumerate(chips):
            copy(1 + j, cx, cy, c, (x, y, c)).wait_recv()
            passed[j].start()
        copy(0, x, y, 1 - c, (x, y, c)).wait_recv()
        for j, (cx, cy) in enumerate(chips):
            copy(4 + j, cx, cy, 1 - c, (x, y, c)).wait_recv()
        for cp in first + passed:
            cp.wait_send()

    _sequencer(name, ID_UKV, 7, lambda x, y, c: [(x, y, 1 - c)] + [(cx, cy, c) for cx, cy in _other_chips(x, y)],
               program)
    return out[...]


def place_block(name, dev, a):
    M, N = a.shape
    tr = min(M, 64)

    def body(dev_ref, a_ref, o_ref):
        o_ref[...] = a_ref[...]

    return pl.pallas_call(
        body, name=name,
        grid_spec=pltpu.PrefetchScalarGridSpec(
            num_scalar_prefetch=1, grid=(M // tr,),
            in_specs=[pl.BlockSpec((tr, N), lambda i, dev_ref: (i, 0))],
            out_specs=pl.BlockSpec((None, tr, N), lambda i, dev_ref: (dev_ref[0], i, 0))),
        out_shape=jax.ShapeDtypeStruct((N_DEV, M, N), a.dtype),
        compiler_params=_params("parallel"),
    )(dev, a)


def _swap_rope(a):
    return jnp.concatenate([a[..., ROPE // 2:], a[..., :ROPE // 2]], axis=-1)


def _rope_tables(S):
    inv = 1.0 / (ROPE_THETA ** (jnp.arange(0, ROPE, 2, dtype=F32) / ROPE))
    ang = jnp.arange(S, dtype=F32)[:, None] * inv[None, :]
    cos, sin = jnp.cos(ang), jnp.sin(ang)
    return (jnp.tile(jnp.concatenate([cos, cos], axis=1), (1, SLOTS)),
            jnp.tile(jnp.concatenate([-sin, sin], axis=1), (1, SLOTS)))


def _vec(norm_g, mod, i, k):
    rows = [norm_g[i, 2 * k], norm_g[i, 2 * k + 1], mod[i, 3 * k], mod[i, 3 * k + 1], mod[i, 3 * k + 2]]
    return jnp.concatenate([jnp.stack(rows), jnp.zeros((3, D), F32)], axis=0)


def _unpack_weights(full, w_uk, w_uv, q_norm, kv_norm):
    G = D // 4
    ffn_in = [[full[2 * i + k].reshape(N_CHIP, D, FSH) for k in range(2)] for i in range(2)]
    ffn_out = [[full[4 + 2 * i + k].reshape(2, FSH, D) for k in range(2)] for i in range(2)]
    pw = full[8].reshape(N_CHIP, 4, G // N_CHIP, G).transpose(1, 0, 2, 3).reshape(4, G, G)
    w_in = full[9].reshape(D, QL + KVL + ROPE)
    w_uq = full[10].reshape(QL, N_HEADS, NOPE + ROPE)
    wkr = w_in[:, QL + KVL:]
    wr = w_uq[:, :, NOPE:]
    eye2 = jnp.eye(2, dtype=BF16)
    uk_t = jnp.transpose(w_uk, (1, 2, 0)).reshape(N_PAIR, 2, NOPE, KVL)
    bduk = jnp.einsum("janc,ab->janbc", uk_t, eye2).reshape(N_PAIR, 2 * NOPE, 2 * KVL)
    uv = jnp.transpose(w_uv, (1, 0, 2)).reshape(N_PAIR, 2, KVL, VH)
    bduv = jnp.einsum("jacn,ab->jacbn", uv, eye2).reshape(N_PAIR, 2 * KVL, 2 * VH)
    mw = dict(wq=w_in[:, :QL], wkv=w_in[:, QL:QL + KVL], wkr4=jnp.tile(wkr, (1, SLOTS)),
              wkrs4=jnp.tile(_swap_rope(wkr), (1, SLOTS)), qn=q_norm, kvn=kv_norm,
              wn=w_uq[:, :, :NOPE].reshape(QL, NOPE_ALL), wr=wr.reshape(QL, ROPE_ALL),
              wrs=_swap_rope(wr).reshape(QL, ROPE_ALL), bduk=bduk)
    return ffn_in, ffn_out, pw, mw, bduv, full[11].reshape(D, D)


def _example_step(x, target, mod, norm_g, pvec, ffn_in, ffn_out, pw, mw, bduv, wo, reducer):
    S = x.shape[0]
    tabs = _rope_tables(S)
    vec = [[_vec(norm_g, mod, i, k) for k in range(3)] for i in range(2)]
    saved = {}
    for i in range(2):
        xin = x
        x, a, u, h = ffn_fwd(xin, vec[i][0], ffn_in[i][0], ffn_out[i][0], 0.5)
        saved[i, 0] = (xin, a, u, h)
        xin = x
        if i == 0:
            x, y, z = pool_fwd(xin, vec[i][1], pw, pvec)
            saved[i, 1] = (xin, y, z)
        else:
            h_m, cq_raw, ckv_raw, cqn, qnope, qcat, kcat, vcat = mla_pre(xin, vec[i][1], mw, tabs)
            olat, lse = attn_fwd(qcat, kcat, vcat)
            x, u_m, ocat = mla_post(olat, xin, vec[i][1], bduv, wo)
            saved[i, 1] = (xin, h_m, cq_raw, ckv_raw, cqn, qnope, qcat, kcat, olat, lse, u_m, ocat)
        xin = x
        x, a, u, h = ffn_fwd(xin, vec[i][2], ffn_in[i][1], ffn_out[i][1], 0.5)
        saved[i, 2] = (xin, a, u, h)
    loss, dx = loss_head(x, target)

    vg = {}
    G = D // 4

    def ffn_grads(i, k, dw_in, dw_out):
        return [(0, 2 * i + k, 4, dw_in.reshape(N_CHIP, 2, D // 2, FSH)),
                (1, 2 * i + k, 4, dw_out.reshape(N_CHIP, 2, DFF // 8, D))]

    vec_ffn2 = vec[1][2]
    for i in (1, 0):
        xin, a, u, h = saved[i, 2]
        dx, du, act, da, vg[i, 2] = ffn_bwd(dx, xin, u, a, vec_ffn2, ffn_in[i][1], ffn_out[i][1], 0.5)
        vec_mixer = reducer.advance(vec[i][1])
        reducer.add(f"f{i}1", ffn_grads(i, 1, *ffn_dw(h, da, act, du)))
        if i == 0:
            xin, y, z = saved[i, 1]
            dx, vg[i, 1], pgrad, g_pool = pool_bwd(dx, xin, y, z, vec_mixer, pw, pvec)
            vec_next = reducer.advance(vec[i][0])
        else:
            xin, h_m, cq_raw, ckv_raw, cqn, qnope, qcat, kcat, olat, lse, u_m, ocat = saved[i, 1]
            du, docat, dolat, delta, vg_post = mla_post_bwd(dx, u_m, olat, vec_mixer, bduv, wo)
            dq, dk, dv = attn_bwd(qcat, kcat, kcat.T, dolat, lse, delta)
            dx, dlat, dql, dqcat, vg_pre, ngrad = mla_pre_bwd(
                dx, dq, dk, dv, xin, cq_raw, ckv_raw, reducer.advance(vec[i][1]), mw, tabs)
            vec_next = vec[i][0]
            vg[i, 1] = vg_post + vg_pre
            g = mla_dw(h_m, dlat, cqn, dqcat, dql, qnope, olat, docat, ocat, du)
            slots = lambda a: a.reshape(D, SLOTS, ROPE).sum(axis=1)
            g_kr = slots(g["in"][:, QL + KVL:QL + KVL + 128]) + _swap_rope(slots(g["in"][:, QL + KVL + 128:]))
            g_in = jnp.concatenate([g["in"][:, :QL + KVL], g_kr], axis=1)
            g_r = g["q"][:, NOPE_ALL:NOPE_ALL + ROPE_ALL].reshape(QL, N_HEADS, ROPE)
            g_rs = g["q"][:, NOPE_ALL + ROPE_ALL:].reshape(QL, N_HEADS, ROPE)
            g_uq = jnp.concatenate([g["q"][:, :NOPE_ALL].reshape(QL, N_HEADS, NOPE), g_r + _swap_rope(g_rs)], axis=-1)

            def heads(pairs):
                blk = pairs.reshape(N_PAIR, 2, KVL, 2, NOPE)
                per_head = jnp.stack([blk[:, 0, :, 0, :], blk[:, 1, :, 1, :]], axis=1).reshape(N_HEADS, KVL, NOPE)
                return jnp.transpose(per_head, (1, 0, 2)).reshape(KVL, N_HEADS * NOPE)

            reducer.add("mla", [(3, 0, 1, g_in.reshape(N_CHIP, 2, D // 8, QL + KVL + ROPE)),
                                (4, 0, 1, g_uq.reshape(N_CHIP, 2, QL // 8, N_HEADS * (NOPE + ROPE))),
                                (5, 0, 1, g["o"].reshape(N_CHIP, 2, D // 8, D))])
            reducer.add_replicated(jnp.concatenate([heads(g["uk"]), heads(g["uv"])], axis=0))
        xin, a, u, h = saved[i, 0]
        dx, du, act, da, vg[i, 0] = ffn_bwd(dx, xin, u, a, vec_next, ffn_in[i][0], ffn_out[i][0], 0.5)
        vec_ffn2 = reducer.advance(vec[0][2])
        grads = ffn_grads(i, 0, *ffn_dw(h, da, act, du))
        if i == 0:
            grads.append((2, 0, 1, g_pool.reshape(N_CHIP, 2, 2 * G // N_CHIP, G)))
        reducer.add(f"f{i}0", grads)
    return loss, dx, vg, pgrad, ngrad


class _GradReducer:
    def __init__(self, core, place, dev):
        self.core, self.place, self.dev = core, place, dev
        self.stacks = {}
        self.live = []
        self.replicated = None

    def add(self, tag, items):
        gen = self._run(tag, items)
        next(gen)
        self.live.append(gen)

    def add_replicated(self, block):
        self.replicated = gather_blocks("gather_ukv", place_block("place_ukv", self.dev, block))

    def advance(self, operand=None):
        live, tokens = [], []
        for gen in self.live:
            try:
                tokens += next(gen)
                live.append(gen)
            except StopIteration:
                pass
        self.live = live
        if tokens and operand is not None:
            operand = operand + sum(token[0, 0] for token in tokens)
        return operand

    def finish(self):
        while self.live:
            self.advance()
        return self.stacks, self.replicated

    def _run(self, tag, items):
        grads, from_pair = reduce_pair(f"reduce_pair_{tag}", [g for *_, g in items])
        yield []
        sums, tokens = [], []
        for j, (g, p) in enumerate(zip(grads, from_pair)):
            s, token = pair_add(f"pair_add_{tag}_{j}", self.core, g, p)
            sums.append(s)
            tokens.append(token)
        sums, from_chips = reduce_chips(f"reduce_chips_{tag}", sums)
        yield tokens
        tokens = []
        for j, ((o, k, n_slots, _), s, p) in enumerate(zip(items, sums, from_chips)):
            self.stacks[o], token = chip_add(f"chip_add_{tag}_{j}", self.place, s, p, k, n_slots, self.stacks.get(o))
            tokens.append(token)
        shared = share_halves(f"share_halves_{tag}", [self.stacks[o] for o, *_ in items], [k for _, k, *_ in items])
        for (o, *_), v in zip(items, shared):
            self.stacks[o] = v
        yield tokens


SMALL_IN = 8 * 640
SMALL_GRAD = 8 * 4224
SMALL_W = 8 * 2944


def _pack(parts, total):
    flat = jnp.concatenate([p.reshape(-1) for p in parts])
    return jnp.concatenate([flat, jnp.zeros((total - flat.shape[0],), F32)]).reshape(8, total // 8)


def kernel(x, c, ada_w, ada_b, norm_g, ffn_w_in, ffn_w_out, pool_w, pool_b, pool_scale, mla_w_in, mla_q_norm, mla_kv_norm, mla_w_uq, mla_w_uk, mla_w_uv, mla_w_o, loss_target, m_ada_w, m_ada_b, m_norm_g, m_ffn_w_in, m_ffn_w_out, m_pool_w, m_pool_b, m_pool_scale, m_mla_w_in, m_mla_q_norm, m_mla_kv_norm, m_mla_w_uq, m_mla_w_uk, m_mla_w_uv, m_mla_w_o, v_ada_w, v_ada_b, v_norm_g, v_ffn_w_in, v_ffn_w_out, v_pool_w, v_pool_b, v_pool_scale, v_mla_w_in, v_mla_q_norm, v_mla_kv_norm, v_mla_w_uq, v_mla_w_uk, v_mla_w_uv, v_mla_w_o):
    ix, iy, ic = _place()
    chip = 2 * ix + iy
    dev = 2 * chip + ic
    core_arr = ic.astype(jnp.int32).reshape(1)
    chip_arr = chip.astype(jnp.int32).reshape(1)
    S = x.shape[1]
    G = D // 4
    NG = D // N_CHIP

    def chip_cols(a, width, axis):
        return lax.dynamic_slice_in_dim(a, chip * width, width, axis)

    got = gather_devices("gather_small_in", _pack([c, norm_g, pool_b, mla_q_norm], SMALL_IN)).reshape(N_DEV, SMALL_IN)
    c_all = got[:, :D]
    parts = got[0::2]
    o = D
    norm_g_full = parts[:, o:o + 12 * NG].reshape(N_CHIP, 2, 6, NG).transpose(1, 2, 0, 3).reshape(2, 6, D)
    o += 12 * NG
    pool_b_full = parts[:, o:o + G].reshape(N_CHIP, 4, G // N_CHIP).transpose(1, 0, 2).reshape(1, D)
    o += G
    q_norm_full = parts[:, o:o + QL // N_CHIP].reshape(1, QL)
    pvec = jnp.concatenate([pool_b_full, pool_scale, jnp.zeros((6, D), F32)], axis=0)

    c_pad = jnp.concatenate([c_all, jnp.zeros((8, D), F32)], axis=0)
    mod_loc = mod_fwd(c_pad, ada_w, chip_cols(ada_b, MOD_COLS, 1).reshape(2, 1, MOD_COLS))
    got = gather_devices("gather_mod", mod_loc[:, :8].transpose(1, 0, 2).reshape(8, 2 * MOD_COLS))
    mine = lax.dynamic_index_in_dim(got[0::2].reshape(N_CHIP, 8, 2, MOD_COLS), dev, axis=1, keepdims=False)
    mod = mine.transpose(1, 0, 2).reshape(2, 9, D)

    bf = lambda a: a.astype(BF16)
    w_in_halves = ffn_w_in.reshape(2, 2, 2, D // 2, FSH)
    w_out_halves = ffn_w_out.reshape(2, 2, 2, DFF // 8, D)
    shards = [(w_in_halves, (i, k)) for i in range(2) for k in range(2)]
    shards += [(w_out_halves, (i, k)) for i in range(2) for k in range(2)]
    shards += [(pool_w.reshape(2, 2 * G // N_CHIP, G), ()), (mla_w_in.reshape(2, D // 8, QL + KVL + ROPE), ()),
               (mla_w_uq.reshape(2, QL // 8, N_HEADS * (NOPE + ROPE)), ()), (mla_w_o.reshape(2, D // 8, D), ())]
    full = [None] * len(shards)
    stages = [(0, 4, 8), (1, 5), (2, 6), (9, 10, 11), (3, 7)]
    first, token = cast_into_slots("cast_first", chip_arr, [shards[t] for t in stages[0]])
    slotted = dict(zip(stages[0], first))
    rest = [t for members in stages[1:] for t in members]
    for stage, members in enumerate(stages):
        got_w = gather_weights(f"gather_weights_{stage}", stage, [slotted[t] for t in members])
        for t, a in zip(members, got_w):
            full[t] = a
        if stage == 0:
            slotted.update(zip(rest, cast_into_slots("cast_rest", chip_arr, [shards[t] for t in rest], token)[0]))
    ffn_in, ffn_out, pw, mw, bduv, wo = _unpack_weights(full, bf(mla_w_uk[0]), bf(mla_w_uv[0]), q_norm_full,
                                                        mla_kv_norm)

    place_arr = jnp.stack([chip, ic]).astype(jnp.int32)
    reducer = _GradReducer(core_arr, place_arr, dev.astype(jnp.int32).reshape(1))
    loss_mine, grad_x, vg, pgrad, ngrad = _example_step(
        x[0], loss_target[0], mod, norm_g_full, pvec, ffn_in, ffn_out, pw, mw, bduv, wo, reducer)
    loss = lax.psum(loss_mine[0, 0], ("x", "y", "c"))
    stacks, ukv = reducer.finish()
    g_ffn_in, g_ffn_out, g_pool_w, g_mla_in, g_uq, g_wo = [stacks[o] for o in range(6)]
    g_ffn_in = g_ffn_in.reshape(ffn_w_in.shape)
    g_ffn_out = g_ffn_out.reshape(ffn_w_out.shape)
    g_pool_w = g_pool_w.reshape(pool_w.shape)
    g_mla_in = g_mla_in.reshape(mla_w_in.shape)
    g_uq = g_uq.reshape(mla_w_uq.shape)
    g_wo = g_wo.reshape(mla_w_o.shape)

    ukv = sum_devices("sum_ukv", ukv)
    g_uk = ukv[:KVL].reshape(mla_w_uk.shape)
    g_uv = ukv[KVL:].reshape(mla_w_uv.shape)

    dmod = jnp.stack([jnp.concatenate([vg[i, k][0:3] for k in range(3)]) for i in range(2)])
    dnorm = jnp.stack([jnp.concatenate([vg[i, k][3:5] for k in range(3)]) for i in range(2)])
    small = _pack([dmod, dnorm, pgrad[0], pgrad[1], ngrad[0], ngrad[1, :KVL]], SMALL_GRAD)
    got = gather_devices("gather_small_grad", small)
    tot = sum_devices("sum_small_grad", got).reshape(-1)
    n_mod = 2 * 9 * D
    g_ada_b = tot[:n_mod].reshape(ada_b.shape)
    o = n_mod
    g_norm = chip_cols(tot[o:o + 12 * D].reshape(2, 6, D), NG, 2)
    o += 12 * D
    g_pool_b = chip_cols(tot[o:o + D].reshape(1, 4, G), G // N_CHIP, 2)
    o += D
    g_pool_scale = tot[o:o + D].reshape(pool_scale.shape)
    o += D
    g_q_norm = chip_cols(tot[o:o + QL].reshape(1, QL), QL // N_CHIP, 1)
    o += QL
    g_kv_norm = tot[o:o + KVL].reshape(mla_kv_norm.shape)
    dmod_all = chip_cols(got.reshape(N_DEV, -1)[:, :n_mod].reshape(N_DEV, 2, 9 * D), MOD_COLS, 2)
    dmod_pad = jnp.concatenate([dmod_all.transpose(1, 0, 2), jnp.zeros((2, 8, MOD_COLS), F32)], axis=1)

    g_ada_w, d_ada_w, nm_ada_w, nv_ada_w = adamw_ada(c_pad, dmod_pad, ada_w, m_ada_w, v_ada_w)
    small_names = ["ada_b", "norm_g", "pool_b", "pool_scale", "mla_q_norm", "mla_kv_norm"]
    small_w = [ada_b, norm_g, pool_b, pool_scale, mla_q_norm, mla_kv_norm]
    small_g = [g_ada_b, g_norm, g_pool_b, g_pool_scale, g_q_norm, g_kv_norm]
    small_m = [m_ada_b, m_norm_g, m_pool_b, m_pool_scale, m_mla_q_norm, m_mla_kv_norm]
    small_v = [v_ada_b, v_norm_g, v_pool_b, v_pool_scale, v_mla_q_norm, v_mla_kv_norm]
    packed = adamw("adamw_small", *[_pack(p, SMALL_W) for p in (small_w, small_g, small_m, small_v)])
    upd = {}
    o = 0
    for name, w in zip(small_names, small_w):
        upd[name] = [p.reshape(-1)[o:o + w.size].reshape(w.shape) for p in packed]
        o += w.size
    big = [("ffn_w_in", ffn_w_in, g_ffn_in, m_ffn_w_in, v_ffn_w_in),
           ("ffn_w_out", ffn_w_out, g_ffn_out, m_ffn_w_out, v_ffn_w_out),
           ("pool_w", pool_w, g_pool_w, m_pool_w, v_pool_w),
           ("mla_w_in", mla_w_in, g_mla_in, m_mla_w_in, v_mla_w_in),
           ("mla_w_uq", mla_w_uq, g_uq, m_mla_w_uq, v_mla_w_uq),
           ("mla_w_uk", mla_w_uk, g_uk, m_mla_w_uk, v_mla_w_uk),
           ("mla_w_uv", mla_w_uv, g_uv, m_mla_w_uv, v_mla_w_uv),
           ("mla_w_o", mla_w_o, g_wo, m_mla_w_o, v_mla_w_o)]
    for name, w, g, m, v in big:
        upd[name] = adamw("adamw_" + name, w, g, m, v)
    upd["ada_w"] = [d_ada_w, nm_ada_w, nv_ada_w]

    order = ["ada_w", "ada_b", "norm_g", "ffn_w_in", "ffn_w_out", "pool_w", "pool_b", "pool_scale", "mla_w_in",
             "mla_q_norm", "mla_kv_norm", "mla_w_uq", "mla_w_uk", "mla_w_uv", "mla_w_o"]
    grad = dict(ada_w=g_ada_w, ada_b=g_ada_b, norm_g=g_norm, ffn_w_in=g_ffn_in, ffn_w_out=g_ffn_out, pool_w=g_pool_w,
                pool_b=g_pool_b, pool_scale=g_pool_scale, mla_w_in=g_mla_in, mla_q_norm=g_q_norm,
                mla_kv_norm=g_kv_norm, mla_w_uq=g_uq, mla_w_uk=g_uk, mla_w_uv=g_uv, mla_w_o=g_wo)
    return (loss, grad_x[None], *[grad[n] for n in order], *[upd[n][0] for n in order],
            *[upd[n][1] for n in order], *[upd[n][2] for n in order])
```

```python
import functools

import jax
import jax.numpy as jnp
from jax import lax
from jax.experimental import pallas as pl
from jax.experimental.pallas import tpu as pltpu
from jax.experimental.pallas import tpu_sc as plsc

F32 = jnp.float32
BF16 = jnp.bfloat16

D = 1024
DFF = 2816
FSH = 1408
N_CHIP = 4
N_DEV = 8
N_HEADS = 16
NOPE = 64
ROPE = 32
VH = 64
QL = 256
KVL = 128
QPAD = 256
EPS = 1e-6
ATTN_SCALE = (NOPE + ROPE) ** -0.5
ROPE_THETA = 10000.0
POOL_WINDOWS = (2, 4, 8, 16)
HALO = 8
ATTN_TQ = 1024
ATTN_KC = 512

ADAM_LR, ADAM_B1, ADAM_B2, ADAM_EPS, ADAM_WD, ADAM_STEP = 0.001, 0.9, 0.999, 1e-08, 0.01, 10

VMEM_LIMIT = 60 * 1024 * 1024
MESH = pl.DeviceIdType.MESH

NT = (((1,), (1,)), ((), ()))
TN = (((0,), (0,)), ((), ()))


def _params(*sem):
    return pltpu.CompilerParams(dimension_semantics=sem, vmem_limit_bytes=VMEM_LIMIT)


def _dot(a, b, dims=None):
    if dims is None:
        return jnp.dot(a, b, preferred_element_type=F32)
    return lax.dot_general(a, b, dims, preferred_element_type=F32)


def _rms(x):
    r = lax.rsqrt(jnp.mean(x * x, axis=-1, keepdims=True) + EPS)
    return x * r, r


def _rms_bwd(xhat, r, dxhat):
    return r * (dxhat - xhat * jnp.mean(dxhat * xhat, axis=-1, keepdims=True))


def _as_row(col):
    return jnp.broadcast_to(col, (col.shape[0], 128)).T[0:1, :]


def _prenorm(x, vec_ref):
    xhat, r = _rms(x)
    h = xhat * vec_ref[0:1, :] * (1.0 + vec_ref[3:4, :]) + vec_ref[2:3, :]
    return h, xhat, r


def _postnorm_bwd(dout, u, vec_ref, weight):
    uhat, r = _rms(u)
    gt = weight * (1.0 + vec_ref[4:5, :])
    dy = dout * gt
    dgate_rows = (weight * dout) * (uhat * vec_ref[1:2, :])
    dgpost_rows = dy * uhat
    du = _rms_bwd(uhat, r, dy * vec_ref[1:2, :])
    return du, dgate_rows, dgpost_rows


def _prenorm_bwd(dh, x, vec_ref, vg_ref):
    xhat, r = _rms(x)
    sc1 = 1.0 + vec_ref[3:4, :]
    g = vec_ref[0:1, :]
    vg_ref[0:1, :] += jnp.sum(dh, axis=0, keepdims=True)
    vg_ref[1:2, :] += jnp.sum(dh * (xhat * g), axis=0, keepdims=True)
    vg_ref[3:4, :] += jnp.sum(dh * sc1 * xhat, axis=0, keepdims=True)
    return _rms_bwd(xhat, r, dh * g * sc1)


def ffn_fwd(x, vec, w_in, w_out, weight):
    S = x.shape[0]
    tm = min(512, S)

    def body(x_ref, vec_ref, wg_ref, wu_ref, wo_ref, xo_ref, a_ref, u_ref, h_ref, acc_ref):
        j = pl.program_id(1)

        @pl.when(j == 0)
        def _():
            h, _, _ = _prenorm(x_ref[...], vec_ref)
            h_ref[...] = h.astype(BF16)
            acc_ref[...] = jnp.zeros_like(acc_ref)

        hb = h_ref[...]
        g = _dot(hb, wg_ref[...])
        up = _dot(hb, wu_ref[...])
        a_ref[0] = g.astype(BF16)
        a_ref[1] = up.astype(BF16)
        act = (g * jax.nn.sigmoid(g)) * up
        acc_ref[...] += _dot(act.astype(BF16), wo_ref[...])

        @pl.when(j == 1)
        def _():
            u = acc_ref[...]
            u_ref[...] = u
            uhat, _ = _rms(u)
            xo_ref[...] = x_ref[...] + (weight * (1.0 + vec_ref[4:5, :])) * (uhat * vec_ref[1:2, :])

    return pl.pallas_call(
        body, name="ffn_fwd", grid=(S // tm, 2),
        in_specs=[pl.BlockSpec((tm, D), lambda i, j: (i, 0)),
                  pl.BlockSpec((8, D), lambda i, j: (0, 0)),
                  pl.BlockSpec((None, D, FSH), lambda i, j: (j, 0, 0)),
                  pl.BlockSpec((None, D, FSH), lambda i, j: (j + 2, 0, 0)),
                  pl.BlockSpec((None, FSH, D), lambda i, j: (j, 0, 0))],
        out_specs=[pl.BlockSpec((tm, D), lambda i, j: (i, 0)),
                   pl.BlockSpec((2, tm, FSH), lambda i, j: (0, i, j)),
                   pl.BlockSpec((tm, D), lambda i, j: (i, 0)),
                   pl.BlockSpec((tm, D), lambda i, j: (i, 0))],
        out_shape=[jax.ShapeDtypeStruct((S, D), F32), jax.ShapeDtypeStruct((2, S, DFF), BF16),
                   jax.ShapeDtypeStruct((S, D), F32), jax.ShapeDtypeStruct((S, D), BF16)],
        scratch_shapes=[pltpu.VMEM((tm, D), F32)],
        compiler_params=_params("parallel", "arbitrary"),
    )(x, vec, w_in, w_in, w_out)


def ffn_bwd(dout, x, u, a, vec, w_in, w_out, weight):
    S = x.shape[0]
    tm = min(256, S)
    row = lambda i: (i, 0)
    half = lambda j: [pl.BlockSpec((2, tm, FSH), lambda i: (0, i, j)), _w3((8, D)),
                      pl.BlockSpec((None, D, FSH), lambda i: (j, 0, 0)),
                      pl.BlockSpec((None, D, FSH), lambda i: (j + 2, 0, 0)),
                      pl.BlockSpec((None, FSH, D), lambda i: (j, 0, 0))]
    half_out = lambda j: [pl.BlockSpec((tm, FSH), lambda i: (i, j)), pl.BlockSpec((2, tm, FSH), lambda i: (0, i, j))]
    half_shape = [jax.ShapeDtypeStruct((S, DFF), BF16), jax.ShapeDtypeStruct((2, S, DFF), BF16)]

    def hidden_bwd(du, a_ref, wg_ref, wu_ref, wo_ref, act_ref, da_ref):
        dact = _dot(du, wo_ref[...], NT)
        g = a_ref[0].astype(F32)
        up = a_ref[1].astype(F32)
        s = jax.nn.sigmoid(g)
        silu = g * s
        act_ref[...] = (silu * up).astype(BF16)
        dg = (dact * up * (s * (1.0 + g * (1.0 - s)))).astype(BF16)
        dup = (dact * silu).astype(BF16)
        da_ref[0] = dg
        da_ref[1] = dup
        return _dot(dg, wg_ref[...], NT) + _dot(dup, wu_ref[...], NT)

    def first(do_ref, u_ref, a_ref, vec_ref, wg_ref, wu_ref, wo_ref, du_ref, dh_ref, act_ref, da_ref, vg_ref):
        @pl.when(pl.program_id(0) == 0)
        def _():
            vg_ref[...] = jnp.zeros_like(vg_ref)

        du, dgate_rows, dgpost_rows = _postnorm_bwd(do_ref[...], u_ref[...], vec_ref, weight)
        vg_ref[2:3, :] += jnp.sum(dgate_rows, axis=0, keepdims=True)
        vg_ref[4:5, :] += jnp.sum(dgpost_rows, axis=0, keepdims=True)
        du = du.astype(BF16)
        du_ref[...] = du
        dh_ref[...] = hidden_bwd(du, a_ref, wg_ref, wu_ref, wo_ref, act_ref, da_ref)

    du, dh, act, da, vg_post = pl.pallas_call(
        first, name="ffn_bwd_first", grid=(S // tm,),
        in_specs=[pl.BlockSpec((tm, D), row), pl.BlockSpec((tm, D), row)] + half(0),
        out_specs=[pl.BlockSpec((tm, D), row), pl.BlockSpec((tm, D), row)] + half_out(0) + [_w3((8, D))],
        out_shape=[jax.ShapeDtypeStruct((S, D), BF16), jax.ShapeDtypeStruct((S, D), F32)] + half_shape
        + [jax.ShapeDtypeStruct((8, D), F32)],
        compiler_params=_params("arbitrary"),
    )(dout, u, a, vec, w_in, w_in, w_out)

    def second(do_ref, x_ref, du_ref, dh_ref, a_ref, vec_ref, wg_ref, wu_ref, wo_ref, act_in, da_in,
               dx_ref, act_ref, da_ref, vg_ref):
        @pl.when(pl.program_id(0) == 0)
        def _():
            vg_ref[...] = jnp.zeros_like(vg_ref)

        dh = dh_ref[...] + hidden_bwd(du_ref[...], a_ref, wg_ref, wu_ref, wo_ref, act_ref, da_ref)
        dx_ref[...] = do_ref[...] + _prenorm_bwd(dh, x_ref[...], vec_ref, vg_ref)

    dx, act, da, vg_pre = pl.pallas_call(
        second, name="ffn_bwd_second", grid=(S // tm,),
        in_specs=[pl.BlockSpec((tm, D), row), pl.BlockSpec((tm, D), row), pl.BlockSpec((tm, D), row),
                  pl.BlockSpec((tm, D), row)] + half(1) + [_ANY, _ANY],
        out_specs=[pl.BlockSpec((tm, D), row)] + half_out(1) + [_w3((8, D))],
        out_shape=[jax.ShapeDtypeStruct((S, D), F32)] + half_shape + [jax.ShapeDtypeStruct((8, D), F32)],
        input_output_aliases={9: 1, 10: 2},
        compiler_params=_params("arbitrary"),
    )(dout, x, du, dh, a, vec, w_in, w_in, w_out, act, da)
    return dx, du, act, da, vg_post + vg_pre


def dw_matmul(name, a, b, a_spec, b_spec, out_shape, out_spec, grid):
    def body(a_ref, b_ref, o_ref):
        @pl.when(pl.program_id(len(grid) - 1) == 0)
        def _():
            o_ref[...] = jnp.zeros_like(o_ref)

        o_ref[...] += _dot(a_ref[...], b_ref[...], TN)

    return pl.pallas_call(
        body, name=name, grid=grid, in_specs=[a_spec, b_spec], out_specs=out_spec,
        out_shape=jax.ShapeDtypeStruct(out_shape, F32),
        compiler_params=_params(*(["parallel"] * (len(grid) - 1) + ["arbitrary"])),
    )(a, b)


def ffn_dw(h, da, act, du):
    S = h.shape[0]
    tk = min(512, S)
    dw_in = dw_matmul("ffn_dw_in", h, da,
                      pl.BlockSpec((tk, D), lambda n, k: (k, 0)),
                      pl.BlockSpec((None, tk, FSH), lambda n, k: (n // 2, k, n % 2)),
                      (N_CHIP, D, FSH), pl.BlockSpec((None, D, FSH), lambda n, k: (n, 0, 0)),
                      (N_CHIP, S // tk))
    dw_out = dw_matmul("ffn_dw_out", act, du,
                       pl.BlockSpec((tk, FSH), lambda n, k: (k, n)),
                       pl.BlockSpec((tk, D), lambda n, k: (k, 0)),
                       (DFF, D), pl.BlockSpec((FSH, D), lambda n, k: (n, 0)),
                       (2, S // tk))
    return dw_in, dw_out


def _halo_specs(tm, S):
    nb = tm // HALO
    last = S // HALO - 1
    return [pl.BlockSpec((HALO, D), lambda i: (jnp.maximum(i * nb - 1, 0), 0)),
            pl.BlockSpec((tm, D), lambda i: (i, 0)),
            pl.BlockSpec((HALO, D), lambda i: (jnp.minimum((i + 1) * nb, last), 0))]


def _shift_rows(v, k):
    return pltpu.roll(v, k % v.shape[0], 0)


def _window_sum(v, g, forward):
    acc = v + _shift_rows(v, 1 if forward else -1)
    for step in (1, 2, 4)[:g]:
        acc = _shift_rows(acc, step) + _shift_rows(acc, -step)
    return acc


def _pool_count(t, w, S):
    return jnp.maximum(jnp.minimum(t + w // 2, S) - jnp.maximum(t - w // 2, 0), 1).astype(F32)


def pool_fwd(x, vec, pw, pvec):
    S = x.shape[0]
    tm = min(256, S)
    G = D // 4

    def body(xp_ref, x_ref, xn_ref, vec_ref, pw_ref, pv_ref, xo_ref, y_ref, z_ref):
        i = pl.program_id(0)
        xa = jnp.concatenate([xp_ref[...], x_ref[...], xn_ref[...]], axis=0)
        t = i * tm - HALO + lax.broadcasted_iota(jnp.int32, (tm + 2 * HALO, 1), 0)
        h, _, _ = _prenorm(xa, vec_ref)
        h = jnp.where((t >= 0) & (t < S), h, 0.0)
        tmain = t[HALO:HALO + tm]
        for g in range(4):
            hg = h[:, g * G:(g + 1) * G]
            pooled = _window_sum(hg, g, True)[HALO:HALO + tm] / _pool_count(tmain, POOL_WINDOWS[g], S)
            z = (pooled - hg[HALO:HALO + tm]).astype(BF16)
            z_ref[:, g * G:(g + 1) * G] = z
            y_ref[:, g * G:(g + 1) * G] = _dot(z, pw_ref[g]) + pv_ref[0:1, g * G:(g + 1) * G]
        u = y_ref[...] * pv_ref[1:2, :]
        uhat, _ = _rms(u)
        xo_ref[...] = x_ref[...] + (1.0 + vec_ref[4:5, :]) * (uhat * vec_ref[1:2, :])

    row = lambda i: (i, 0)
    full = lambda i: (0, 0)
    return pl.pallas_call(
        body, name="pool_fwd", grid=(S // tm,),
        in_specs=_halo_specs(tm, S) + [pl.BlockSpec((8, D), full), pl.BlockSpec((4, G, G), lambda i: (0, 0, 0)),
                                       pl.BlockSpec((8, D), full)],
        out_specs=[pl.BlockSpec((tm, D), row)] * 3,
        out_shape=[jax.ShapeDtypeStruct((S, D), F32), jax.ShapeDtypeStruct((S, D), F32),
                   jax.ShapeDtypeStruct((S, D), BF16)],
        compiler_params=_params("parallel"),
    )(x, x, x, vec, pw, pvec)


def pool_bwd(dout, x, y, z, vec, pw, pvec):
    S = x.shape[0]
    tm = min(256, S)
    G = D // 4
    R = G // N_CHIP

    def body(dop_ref, do_ref, don_ref, yp_ref, y_ref, yn_ref, x_ref, z_ref, vec_ref, pw_ref, pv_ref,
             dx_ref, vg_ref, pg_ref, dw_ref, dh_ref):
        i = pl.program_id(0)

        @pl.when(i == 0)
        def _():
            vg_ref[...] = jnp.zeros_like(vg_ref)
            pg_ref[...] = jnp.zeros_like(pg_ref)
            dw_ref[...] = jnp.zeros_like(dw_ref)

        doa = jnp.concatenate([dop_ref[...], do_ref[...], don_ref[...]], axis=0)
        ya = jnp.concatenate([yp_ref[...], y_ref[...], yn_ref[...]], axis=0)
        t = i * tm - HALO + lax.broadcasted_iota(jnp.int32, (tm + 2 * HALO, 1), 0)
        inside = (t >= 0) & (t < S)
        main = (t >= i * tm) & (t < (i + 1) * tm)
        du, dgate_rows, dgpost_rows = _postnorm_bwd(doa, ya * pv_ref[1:2, :], vec_ref, 1.0)
        du = jnp.where(inside, du, 0.0)
        vg_ref[2:3, :] += jnp.sum(jnp.where(main, dgate_rows, 0.0), axis=0, keepdims=True)
        vg_ref[4:5, :] += jnp.sum(jnp.where(main, dgpost_rows, 0.0), axis=0, keepdims=True)
        dy = du * pv_ref[1:2, :]
        pg_ref[0:1, :] += jnp.sum(jnp.where(main, dy, 0.0), axis=0, keepdims=True)
        pg_ref[1:2, :] += jnp.sum(jnp.where(main, du * ya, 0.0), axis=0, keepdims=True)
        for g in range(4):
            dyg = dy[:, g * G:(g + 1) * G].astype(BF16)
            dz = _dot(dyg, pw_ref[g], NT)
            e = dz / _pool_count(t, POOL_WINDOWS[g], S)
            dh_ref[:, g * G:(g + 1) * G] = (_window_sum(e, g, False) - dz)[HALO:HALO + tm]
            dwg = _dot(z_ref[:, g * G:(g + 1) * G], dyg[HALO:HALO + tm], TN)
            for q in range(N_CHIP):
                dw_ref[q, g] += dwg[q * R:(q + 1) * R, :]
        dx_ref[...] = do_ref[...] + _prenorm_bwd(dh_ref[...], x_ref[...], vec_ref, vg_ref)

    row = lambda i: (i, 0)
    full = lambda i: (0, 0)
    halo = _halo_specs(tm, S)
    return pl.pallas_call(
        body, name="pool_bwd", grid=(S // tm,),
        in_specs=halo + halo + [pl.BlockSpec((tm, D), row), pl.BlockSpec((tm, D), row), pl.BlockSpec((8, D), full),
                                pl.BlockSpec((4, G, G), lambda i: (0, 0, 0)), pl.BlockSpec((8, D), full)],
        out_specs=[pl.BlockSpec((tm, D), row), pl.BlockSpec((8, D), full), pl.BlockSpec((8, D), full),
                   pl.BlockSpec((N_CHIP, 4, R, G), lambda i: (0, 0, 0, 0))],
        out_shape=[jax.ShapeDtypeStruct((S, D), F32), jax.ShapeDtypeStruct((8, D), F32),
                   jax.ShapeDtypeStruct((8, D), F32), jax.ShapeDtypeStruct((N_CHIP, 4, R, G), F32)],
        scratch_shapes=[pltpu.VMEM((tm, D), F32)],
        compiler_params=_params("arbitrary"),
    )(dout, dout, dout, y, y, y, x, z, vec, pw, pvec)


N_PAIR = N_HEADS // 2
SLOTS = 128 // ROPE
ROPE_ALL = N_HEADS * ROPE
NOPE_ALL = N_HEADS * NOPE
LAT_ALL = N_HEADS * KVL
DLAT = QL + KVL + 2 * 128
DQ_ALL = NOPE_ALL + 2 * ROPE_ALL


def _w3(shape):
    return pl.BlockSpec(shape, lambda i: (0,) * len(shape))


def _slot_mask(hd, rows):
    lane = lax.broadcasted_iota(jnp.int32, (rows, 128), 1)
    return (lane // ROPE) == (hd % SLOTS)


MLA_WEIGHTS = ("wq", "wkv", "wkr4", "wkrs4", "qn", "kvn", "wn", "wr", "wrs", "bduk")


def _mla_weight_specs():
    return [_w3((D, QL)), _w3((D, KVL)), _w3((D, 128)), _w3((D, 128)), _w3((1, QL)), _w3((1, KVL)),
            _w3((QL, NOPE_ALL)), _w3((QL, ROPE_ALL)), _w3((QL, ROPE_ALL)), _w3((N_PAIR, 2 * NOPE, 2 * KVL))]


def mla_pre(x, vec, mw, tabs):
    S = x.shape[0]
    tm = min(256, S)

    def body(x_ref, vec_ref, cos_ref, sin_ref, wq_ref, wkv_ref, wkr_ref, wkrs_ref, qn_ref, kvn_ref,
             wn_ref, wr_ref, wrs_ref, bduk_ref,
             h_ref, cq_ref, ckv_ref, cqn_ref, qnope_ref, qcat_ref, kcat_ref, vcat_ref):
        h, _, _ = _prenorm(x_ref[...], vec_ref)
        hb = h.astype(BF16)
        h_ref[...] = hb
        cq_raw = _dot(hb, wq_ref[...])
        ckv_raw = _dot(hb, wkv_ref[...])
        cq_ref[...] = cq_raw
        ckv_ref[...] = ckv_raw
        cos, sin = cos_ref[...], sin_ref[...]
        ckv = (_rms(ckv_raw)[0] * kvn_ref[...]).astype(BF16)
        kcat_ref[:, 0:KVL] = ckv
        kcat_ref[:, KVL:] = (_dot(hb, wkr_ref[...]) * cos + _dot(hb, wkrs_ref[...]) * sin).astype(BF16)
        vcat_ref[:, 0:KVL] = ckv
        ones = lax.broadcasted_iota(jnp.int32, (tm, QPAD - KVL), 1) == 0
        vcat_ref[:, KVL:] = jnp.where(ones, 1.0, 0.0).astype(BF16)
        cqb = (_rms(cq_raw)[0] * qn_ref[...]).astype(BF16)
        cqn_ref[...] = cqb
        qn = _dot(cqb, wn_ref[...]).astype(BF16)
        qnope_ref[...] = qn
        cos4, sin4 = jnp.tile(cos, (1, SLOTS)), jnp.tile(sin, (1, SLOTS))
        qr = ((_dot(cqb, wr_ref[...]) * cos4 + _dot(cqb, wrs_ref[...]) * sin4) * ATTN_SCALE).astype(BF16)
        for j in range(N_PAIR):
            ql = (_dot(qn[:, 128 * j:128 * (j + 1)], bduk_ref[j]) * ATTN_SCALE).astype(BF16)
            for hd in (2 * j, 2 * j + 1):
                qcat_ref[hd, :, 0:KVL] = ql[:, KVL * (hd - 2 * j):KVL * (hd - 2 * j + 1)]
                group = qr[:, 128 * (hd // SLOTS):128 * (hd // SLOTS + 1)]
                qcat_ref[hd, :, KVL:] = jnp.where(_slot_mask(hd, tm), group, jnp.zeros_like(group))

    row = lambda i: (i, 0)
    hrow = lambda i: (0, i, 0)
    return pl.pallas_call(
        body, name="mla_pre", grid=(S // tm,),
        in_specs=[pl.BlockSpec((tm, D), row), _w3((8, D)), pl.BlockSpec((tm, 128), row), pl.BlockSpec((tm, 128), row)]
        + _mla_weight_specs(),
        out_specs=[pl.BlockSpec((tm, D), row), pl.BlockSpec((tm, QL), row), pl.BlockSpec((tm, KVL), row),
                   pl.BlockSpec((tm, QL), row), pl.BlockSpec((tm, NOPE_ALL), row),
                   pl.BlockSpec((N_HEADS, tm, QPAD), hrow), pl.BlockSpec((tm, QPAD), row),
                   pl.BlockSpec((tm, QPAD), row)],
        out_shape=[jax.ShapeDtypeStruct((S, D), BF16), jax.ShapeDtypeStruct((S, QL), F32),
                   jax.ShapeDtypeStruct((S, KVL), F32), jax.ShapeDtypeStruct((S, QL), BF16),
                   jax.ShapeDtypeStruct((S, NOPE_ALL), BF16), jax.ShapeDtypeStruct((N_HEADS, S, QPAD), BF16),
                   jax.ShapeDtypeStruct((S, QPAD), BF16), jax.ShapeDtypeStruct((S, QPAD), BF16)],
        compiler_params=_params("parallel"),
    )(x, vec, tabs[0], tabs[1], *[mw[k] for k in MLA_WEIGHTS])


def attn_fwd(qcat, kcat, vcat):
    S = kcat.shape[0]
    tq = min(ATTN_TQ, S)
    kc = min(ATTN_KC, S)

    def body(q_ref, k_ref, v_ref, o_ref, lse_ref):
        q = q_ref[...]
        m = jnp.full((tq, 1), -jnp.inf, F32)
        ov = jnp.zeros((tq, QPAD), F32)
        for c in range(S // kc):
            s = _dot(q, k_ref[c * kc:(c + 1) * kc, :], NT)
            m_new = jnp.maximum(m, jnp.max(s, axis=-1, keepdims=True))
            p = jnp.exp(s - m_new).astype(BF16)
            ov = ov * jnp.exp(m - m_new) + _dot(p, v_ref[c * kc:(c + 1) * kc, :])
            m = m_new
        l = ov[:, KVL:KVL + 1]
        o_ref[...] = (ov[:, 0:KVL] * (1.0 / l)).astype(BF16)
        lse_ref[...] = _as_row(m + jnp.log(l))

    return pl.pallas_call(
        body, name="attn_fwd", grid=(N_HEADS, S // tq),
        in_specs=[pl.BlockSpec((None, tq, QPAD), lambda h, i: (h, i, 0)),
                  pl.BlockSpec((S, QPAD), lambda h, i: (0, 0)),
                  pl.BlockSpec((S, QPAD), lambda h, i: (0, 0))],
        out_specs=[pl.BlockSpec((tq, KVL), lambda h, i: (i, h)),
                   pl.BlockSpec((None, 1, tq), lambda h, i: (h, 0, i))],
        out_shape=[jax.ShapeDtypeStruct((S, LAT_ALL), BF16), jax.ShapeDtypeStruct((N_HEADS, 1, S), F32)],
        compiler_params=_params("parallel", "parallel"),
    )(qcat, kcat, vcat)


def mla_post(olat, x, vec, bduv, wo):
    S = x.shape[0]
    tm = min(256, S)

    def body(o_ref, x_ref, vec_ref, bduv_ref, wo_ref, xo_ref, u_ref, ocat_ref):
        for j in range(N_PAIR):
            oc = _dot(o_ref[:, 2 * KVL * j:2 * KVL * (j + 1)], bduv_ref[j])
            ocat_ref[:, 2 * VH * j:2 * VH * (j + 1)] = oc.astype(BF16)
        u = _dot(ocat_ref[...], wo_ref[...])
        u_ref[...] = u
        uhat, _ = _rms(u)
        xo_ref[...] = x_ref[...] + (1.0 + vec_ref[4:5, :]) * (uhat * vec_ref[1:2, :])

    row = lambda i: (i, 0)
    return pl.pallas_call(
        body, name="mla_post", grid=(S // tm,),
        in_specs=[pl.BlockSpec((tm, LAT_ALL), row), pl.BlockSpec((tm, D), row), _w3((8, D)),
                  _w3((N_PAIR, 2 * KVL, 2 * VH)), _w3((D, D))],
        out_specs=[pl.BlockSpec((tm, D), row), pl.BlockSpec((tm, D), row), pl.BlockSpec((tm, D), row)],
        out_shape=[jax.ShapeDtypeStruct((S, D), F32), jax.ShapeDtypeStruct((S, D), F32),
                   jax.ShapeDtypeStruct((S, D), BF16)],
        compiler_params=_params("parallel"),
    )(olat, x, vec, bduv, wo)


def mla_post_bwd(dout, u, olat, vec, bduv, wo):
    S = u.shape[0]
    tm = min(256, S)

    def body(do_ref, u_ref, o_ref, vec_ref, bduv_ref, wo_ref, du_ref, docat_ref, dolat_ref, delta_ref, vg_ref):
        @pl.when(pl.program_id(0) == 0)
        def _():
            vg_ref[...] = jnp.zeros_like(vg_ref)

        du, dgate_rows, dgpost_rows = _postnorm_bwd(do_ref[...], u_ref[...], vec_ref, 1.0)
        vg_ref[2:3, :] += jnp.sum(dgate_rows, axis=0, keepdims=True)
        vg_ref[4:5, :] += jnp.sum(dgpost_rows, axis=0, keepdims=True)
        dub = du.astype(BF16)
        du_ref[...] = dub
        docat_ref[...] = _dot(dub, wo_ref[...], NT).astype(BF16)
        for j in range(N_PAIR):
            dol = _dot(docat_ref[:, 2 * VH * j:2 * VH * (j + 1)], bduv_ref[j], NT).astype(BF16)
            dolat_ref[:, 2 * KVL * j:2 * KVL * (j + 1)] = dol
            prod = dol.astype(F32) * o_ref[:, 2 * KVL * j:2 * KVL * (j + 1)].astype(F32)
            delta_ref[2 * j] = _as_row(jnp.sum(prod[:, 0:KVL], axis=-1, keepdims=True))
            delta_ref[2 * j + 1] = _as_row(jnp.sum(prod[:, KVL:], axis=-1, keepdims=True))

    row = lambda i: (i, 0)
    hrow = lambda i: (0, i, 0)
    return pl.pallas_call(
        body, name="mla_post_bwd", grid=(S // tm,),
        in_specs=[pl.BlockSpec((tm, D), row), pl.BlockSpec((tm, D), row), pl.BlockSpec((tm, LAT_ALL), row),
                  _w3((8, D)), _w3((N_PAIR, 2 * KVL, 2 * VH)), _w3((D, D))],
        out_specs=[pl.BlockSpec((tm, D), row), pl.BlockSpec((tm, D), row),
                   pl.BlockSpec((tm, LAT_ALL), row), pl.BlockSpec((N_HEADS, 1, tm), lambda i: (0, 0, i)), _w3((8, D))],
        out_shape=[jax.ShapeDtypeStruct((S, D), BF16), jax.ShapeDtypeStruct((S, D), BF16),
                   jax.ShapeDtypeStruct((S, LAT_ALL), BF16), jax.ShapeDtypeStruct((N_HEADS, 1, S), F32),
                   jax.ShapeDtypeStruct((8, D), F32)],
        compiler_params=_params("arbitrary"),
    )(dout, u, olat, vec, bduv, wo)


def attn_bwd(qcat, kcat, kcat_t, dolat, lse_row, delta_row):
    S = kcat.shape[0]
    tq = min(ATTN_TQ, S)
    kc = min(ATTN_KC, S)

    def body(q_ref, k_ref, kt_ref, do_ref, lse_ref, dl_ref, dq_ref, dk_ref, dv_ref):
        @pl.when((pl.program_id(0) == 0) & (pl.program_id(1) == 0))
        def _():
            dk_ref[...] = jnp.zeros_like(dk_ref)
            dv_ref[...] = jnp.zeros_like(dv_ref)

        q, do = q_ref[...], do_ref[...]
        lse, dl = lse_ref[...], dl_ref[...]
        dqt = jnp.zeros((QPAD, tq), F32)
        for c in range(S // kc):
            rows = slice(c * kc, (c + 1) * kc)
            st = _dot(k_ref[rows, :], q, NT)
            pt = jnp.exp(st - lse)
            dpt = _dot(k_ref[rows, 0:KVL], do, NT)
            dst = (pt * (dpt - dl)).astype(BF16)
            dv_ref[rows, :] += _dot(pt.astype(BF16), do)
            dk_ref[rows, :] += _dot(dst, q)
            dqt = dqt + _dot(kt_ref[:, rows], dst)
        dq_ref[...] = dqt.T

    return pl.pallas_call(
        body, name="attn_bwd", grid=(N_HEADS, S // tq),
        in_specs=[pl.BlockSpec((None, tq, QPAD), lambda h, i: (h, i, 0)),
                  pl.BlockSpec((S, QPAD), lambda h, i: (0, 0)),
                  pl.BlockSpec((QPAD, S), lambda h, i: (0, 0)),
                  pl.BlockSpec((tq, KVL), lambda h, i: (i, h)),
                  pl.BlockSpec((None, 1, tq), lambda h, i: (h, 0, i)),
                  pl.BlockSpec((None, 1, tq), lambda h, i: (h, 0, i))],
        out_specs=[pl.BlockSpec((None, tq, QPAD), lambda h, i: (h, i, 0)),
                   pl.BlockSpec((S, QPAD), lambda h, i: (0, 0)),
                   pl.BlockSpec((S, KVL), lambda h, i: (0, 0))],
        out_shape=[jax.ShapeDtypeStruct((N_HEADS, S, QPAD), F32), jax.ShapeDtypeStruct((S, QPAD), F32),
                   jax.ShapeDtypeStruct((S, KVL), F32)],
        compiler_params=_params("arbitrary", "arbitrary"),
    )(qcat, kcat, kcat_t, dolat, lse_row, delta_row)


def mla_pre_bwd(dout, dq, dk, dv, x, cq_raw, ckv_raw, vec, mw, tabs):
    S = x.shape[0]
    tm = min(256, S)

    def body(do_ref, dq_ref, dk_ref, dv_ref, x_ref, cq_ref, ckv_ref, vec_ref, cos_ref, sin_ref,
             wq_ref, wkv_ref, wkr_ref, wkrs_ref, qn_ref, kvn_ref, wn_ref, wr_ref, wrs_ref, bduk_ref,
             dx_ref, dlat_ref, dql_ref, dqcat_ref, vg_ref, ng_ref):
        @pl.when(pl.program_id(0) == 0)
        def _():
            vg_ref[...] = jnp.zeros_like(vg_ref)
            ng_ref[...] = jnp.zeros_like(ng_ref)

        cos, sin = cos_ref[...], sin_ref[...]
        for j in range(N_PAIR):
            dql = jnp.concatenate([dq_ref[2 * j, :, 0:KVL], dq_ref[2 * j + 1, :, 0:KVL]], axis=1) * ATTN_SCALE
            dql = dql.astype(BF16)
            dql_ref[:, 2 * KVL * j:2 * KVL * (j + 1)] = dql
            dqcat_ref[:, 2 * NOPE * j:2 * NOPE * (j + 1)] = _dot(dql, bduk_ref[j], NT).astype(BF16)
        groups = []
        for grp in range(N_HEADS // SLOTS):
            acc = jnp.zeros((tm, 128), F32)
            for hd in range(SLOTS * grp, SLOTS * (grp + 1)):
                acc = acc + jnp.where(_slot_mask(hd, tm), dq_ref[hd, :, KVL:], 0.0)
            groups.append(acc)
        dqr = jnp.concatenate(groups, axis=1) * ATTN_SCALE
        qa = (dqr * jnp.tile(cos, (1, SLOTS))).astype(BF16)
        qb = (dqr * jnp.tile(sin, (1, SLOTS))).astype(BF16)
        dqcat_ref[:, NOPE_ALL:NOPE_ALL + ROPE_ALL] = qa
        dqcat_ref[:, NOPE_ALL + ROPE_ALL:] = qb
        dcq = _dot(dqcat_ref[:, 0:NOPE_ALL], wn_ref[...], NT) + _dot(qa, wr_ref[...], NT) + _dot(qb, wrs_ref[...], NT)
        cqh, rq = _rms(cq_ref[...])
        ng_ref[0:1, :] += jnp.sum(dcq * cqh, axis=0, keepdims=True)
        dcq_raw = _rms_bwd(cqh, rq, dcq * qn_ref[...]).astype(BF16)
        dckv = dk_ref[:, 0:KVL] + dv_ref[...]
        ckvh, rk = _rms(ckv_ref[...])
        ng_ref[1:2, 0:KVL] += jnp.sum(dckv * ckvh, axis=0, keepdims=True)
        dckv_raw = _rms_bwd(ckvh, rk, dckv * kvn_ref[...]).astype(BF16)
        dkr = dk_ref[:, KVL:]
        ka = (dkr * cos).astype(BF16)
        kb = (dkr * sin).astype(BF16)
        dlat_ref[:, 0:QL] = dcq_raw
        dlat_ref[:, QL:QL + KVL] = dckv_raw
        dlat_ref[:, QL + KVL:QL + KVL + 128] = ka
        dlat_ref[:, QL + KVL + 128:] = kb
        dh = (_dot(dcq_raw, wq_ref[...], NT) + _dot(dckv_raw, wkv_ref[...], NT)
              + _dot(ka, wkr_ref[...], NT) + _dot(kb, wkrs_ref[...], NT))
        dx_ref[...] = do_ref[...] + _prenorm_bwd(dh, x_ref[...], vec_ref, vg_ref)

    row = lambda i: (i, 0)
    hrow = lambda i: (0, i, 0)
    return pl.pallas_call(
        body, name="mla_pre_bwd", grid=(S // tm,),
        in_specs=[pl.BlockSpec((tm, D), row), pl.BlockSpec((N_HEADS, tm, QPAD), hrow), pl.BlockSpec((tm, QPAD), row),
                  pl.BlockSpec((tm, KVL), row), pl.BlockSpec((tm, D), row), pl.BlockSpec((tm, QL), row),
                  pl.BlockSpec((tm, KVL), row), _w3((8, D)), pl.BlockSpec((tm, 128), row), pl.BlockSpec((tm, 128), row)]
        + _mla_weight_specs(),
        out_specs=[pl.BlockSpec((tm, D), row), pl.BlockSpec((tm, DLAT), row), pl.BlockSpec((tm, LAT_ALL), row),
                   pl.BlockSpec((tm, DQ_ALL), row), _w3((8, D)), _w3((8, QL))],
        out_shape=[jax.ShapeDtypeStruct((S, D), F32), jax.ShapeDtypeStruct((S, DLAT), BF16),
                   jax.ShapeDtypeStruct((S, LAT_ALL), BF16), jax.ShapeDtypeStruct((S, DQ_ALL), BF16),
                   jax.ShapeDtypeStruct((8, D), F32), jax.ShapeDtypeStruct((8, QL), F32)],
        compiler_params=_params("arbitrary"),
    )(dout, dq, dk, dv, x, cq_raw, ckv_raw, vec, tabs[0], tabs[1], *[mw[k] for k in MLA_WEIGHTS])


def mla_dw(h, dlat, cqn, dqcat, dql, qnope, olat, docat, ocat, du):
    S = h.shape[0]
    tk = min(512, S)
    nk = S // tk
    flat = lambda w: pl.BlockSpec((tk, w), lambda k: (k, 0))
    cols = lambda w: pl.BlockSpec((tk, w), lambda n, k: (k, n))
    pair_o = pl.BlockSpec((None, 2 * KVL, 128), lambda n, k: (n, 0, 0))
    g = {}
    g["in"] = dw_matmul("mla_dw_in", h, dlat, flat(D), flat(DLAT), (D, DLAT),
                        pl.BlockSpec((D, DLAT), lambda k: (0, 0)), (nk,))
    g["q"] = dw_matmul("mla_dw_q", cqn, dqcat, flat(QL), flat(DQ_ALL), (QL, DQ_ALL),
                       pl.BlockSpec((QL, DQ_ALL), lambda k: (0, 0)), (nk,))
    g["uk"] = dw_matmul("mla_dw_uk", dql, qnope, cols(2 * KVL), cols(2 * NOPE), (N_PAIR, 2 * KVL, 2 * NOPE), pair_o,
                        (N_PAIR, nk))
    g["uv"] = dw_matmul("mla_dw_uv", olat, docat, cols(2 * KVL), cols(2 * VH), (N_PAIR, 2 * KVL, 2 * VH), pair_o,
                        (N_PAIR, nk))
    g["o"] = dw_matmul("mla_dw_o", ocat, du, cols(256), pl.BlockSpec((tk, D), lambda n, k: (k, 0)), (D, D),
                       pl.BlockSpec((256, D), lambda n, k: (n, 0)), (D // 256, nk))
    return g


def loss_head(y, target):
    S = y.shape[0]
    tm = min(512, S)

    def body(y_ref, t_ref, loss_ref, dy_ref):
        @pl.when(pl.program_id(0) == 0)
        def _():
            loss_ref[...] = jnp.zeros_like(loss_ref)

        err = y_ref[...] - t_ref[...]
        dy_ref[...] = err * (1.0 / D)
        loss_ref[...] += 0.5 * jnp.sum(jnp.mean(err * err, axis=-1, keepdims=True), axis=0, keepdims=True)

    row = lambda i: (i, 0)
    return pl.pallas_call(
        body, name="loss_head", grid=(S // tm,),
        in_specs=[pl.BlockSpec((tm, D), row), pl.BlockSpec((tm, D), row)],
        out_specs=[pl.BlockSpec((1, 1), lambda i: (0, 0)), pl.BlockSpec((tm, D), row)],
        out_shape=[jax.ShapeDtypeStruct((1, 1), F32), jax.ShapeDtypeStruct((S, D), F32)],
        compiler_params=_params("arbitrary"),
    )(y, target)


MOD_COLS = 9 * D // N_CHIP


def mod_fwd(c_pad, ada_w, ada_b_loc):
    tn = MOD_COLS // 3

    def body(c_ref, w_ref, b_ref, o_ref):
        c = c_ref[...]
        sc = (c * jax.nn.sigmoid(c)).astype(BF16)
        o_ref[...] = _dot(sc, w_ref[...].astype(BF16)) + b_ref[...]

    return pl.pallas_call(
        body, name="mod_fwd", grid=(2, 3),
        in_specs=[pl.BlockSpec((16, D), lambda i, n: (0, 0)), pl.BlockSpec((None, D, tn), lambda i, n: (i, 0, n)),
                  pl.BlockSpec((None, 1, tn), lambda i, n: (i, 0, n))],
        out_specs=pl.BlockSpec((None, 16, tn), lambda i, n: (i, 0, n)),
        out_shape=jax.ShapeDtypeStruct((2, 16, MOD_COLS), F32),
        compiler_params=_params("parallel", "parallel"),
    )(c_pad, ada_w, ada_b_loc)


def _adamw_math(w, g, m, v):
    m = ADAM_B1 * m + (1.0 - ADAM_B1) * g
    v = ADAM_B2 * v + (1.0 - ADAM_B2) * (g * g)
    m_hat = m / (1.0 - ADAM_B1 ** ADAM_STEP)
    v_hat = v / (1.0 - ADAM_B2 ** ADAM_STEP)
    delta = -ADAM_LR * (m_hat / (jnp.sqrt(v_hat) + ADAM_EPS) + ADAM_WD * w)
    return delta, m, v


def adamw(name, w, g, m, v):
    shape = w.shape
    cols = shape[-1]
    rows = w.size // cols
    tr = rows
    for cand in (512, 256, 128, 64, 32, 16, 8):
        if rows % cand == 0 and cand * cols * 4 <= (2 << 20):
            tr = cand
            break

    def body(w_ref, g_ref, m_ref, v_ref, d_ref, mo_ref, vo_ref):
        d_ref[...], mo_ref[...], vo_ref[...] = _adamw_math(w_ref[...], g_ref[...], m_ref[...], v_ref[...])

    spec = pl.BlockSpec((tr, cols), lambda i: (i, 0))
    outs = pl.pallas_call(
        body, name=name, grid=(rows // tr,), in_specs=[spec] * 4, out_specs=[spec] * 3,
        out_shape=[jax.ShapeDtypeStruct((rows, cols), F32)] * 3,
        compiler_params=_params("parallel"),
    )(*[a.reshape(rows, cols) for a in (w, g, m, v)])
    return [o.reshape(shape) for o in outs]


def adamw_ada(c_pad, dmod, w, m, v):
    tr = 256

    def body(c_ref, dm_ref, w_ref, m_ref, v_ref, g_ref, d_ref, mo_ref, vo_ref):
        c = c_ref[...]
        sc = (c * jax.nn.sigmoid(c)).astype(BF16)
        g = _dot(sc, dm_ref[...].astype(BF16), TN)
        g_ref[...] = g
        d_ref[...], mo_ref[...], vo_ref[...] = _adamw_math(w_ref[...], g, m_ref[...], v_ref[...])

    wspec = pl.BlockSpec((None, tr, MOD_COLS), lambda i, r: (i, r, 0))
    return pl.pallas_call(
        body, name="adamw_ada", grid=(2, D // tr),
        in_specs=[pl.BlockSpec((16, tr), lambda i, r: (0, r)),
                  pl.BlockSpec((None, 16, MOD_COLS), lambda i, r: (i, 0, 0)), wspec, wspec, wspec],
        out_specs=[wspec] * 4,
        out_shape=[jax.ShapeDtypeStruct((2, D, MOD_COLS), F32)] * 4,
        compiler_params=_params("parallel", "parallel"),
    )(c_pad, dmod, w, m, v)


def sum_devices(name, a):
    _, R, C = a.shape
    tr = R
    for cand in (64, 32, 16, 8):
        if R % cand == 0:
            tr = cand
            break

    def body(a_ref, o_ref):
        acc = a_ref[0]
        for dev in range(1, N_DEV):
            acc = acc + a_ref[dev]
        o_ref[...] = acc

    return pl.pallas_call(
        body, name=name, grid=(R // tr,),
        in_specs=[pl.BlockSpec((N_DEV, tr, C), lambda i: (0, i, 0))],
        out_specs=pl.BlockSpec((tr, C), lambda i: (i, 0)),
        out_shape=jax.ShapeDtypeStruct((R, C), F32),
        compiler_params=_params("parallel"),
    )(a)


def _place():
    return lax.axis_index("x"), lax.axis_index("y"), lax.axis_index("c")


def _other_chips(x, y):
    return [(1 - x, y), (x, 1 - y), (1 - x, 1 - y)]


def gather_devices(name, a):
    m_per, n = a.shape

    def body(x_ref, out_ref, send_sems, recv_sems, local_sem):
        x, y, c = _place()
        me, sibling = (x, y, c), (x, y, 1 - c)
        chips = _other_chips(x, y)

        def rows(px, py, pc):
            return out_ref.at[pl.ds((4 * px + 2 * py + pc) * m_per, m_per), :]

        def copy(k, block, to, src=None):
            return pltpu.make_async_remote_copy(
                src_ref=rows(*block) if src is None else src, dst_ref=rows(*block),
                send_sem=send_sems.at[k], recv_sem=recv_sems.at[k], device_id=to, device_id_type=MESH)

        mine = pltpu.make_async_copy(x_ref, rows(*me), local_sem)
        mine.start()
        first = [copy(0, me, sibling, src=x_ref)]
        first += [copy(1 + j, me, (*chip, c), src=x_ref) for j, chip in enumerate(chips)]
        for cp in first:
            cp.start()
        passed = [copy(4 + j, (*chip, c), sibling) for j, chip in enumerate(chips)]
        for j, chip in enumerate(chips):
            copy(1 + j, (*chip, c), me).wait_recv()
            passed[j].start()
        copy(0, sibling, me).wait_recv()
        for j, chip in enumerate(chips):
            copy(4 + j, (*chip, 1 - c), me).wait_recv()
        for cp in first + passed:
            cp.wait_send()
        mine.wait()

    out = pl.pallas_call(
        body, name=name,
        out_shape=jax.ShapeDtypeStruct((N_DEV * m_per, n), a.dtype),
        in_specs=[pl.BlockSpec(memory_space=pltpu.VMEM)],
        out_specs=pl.BlockSpec(memory_space=pltpu.VMEM),
        scratch_shapes=[pltpu.SemaphoreType.DMA((7,)), pltpu.SemaphoreType.DMA((7,)), pltpu.SemaphoreType.DMA],
        compiler_params=pltpu.CompilerParams(vmem_limit_bytes=VMEM_LIMIT),
    )(a)
    return out.reshape(N_DEV, m_per, n)


_ANY = pl.BlockSpec(memory_space=pl.ANY)


def _hbm_ref(a):
    return jax.new_ref(a, memory_space=pltpu.MemorySpace.HBM)


def _hbm_empty(shape, dtype):
    return jax.empty_ref(jax.ShapeDtypeStruct(shape, dtype), memory_space=pltpu.MemorySpace.HBM)


ID_PAIR, ID_CHIPS, ID_SHARE, ID_UKV = 8, 9, 10, 11


def _sequencer(name, collective_id, n_sem, peers_of, program):
    sems = pltpu.SemaphoreType.DMA((n_sem,))

    @pl.kernel(mesh=plsc.ScalarSubcoreMesh(axis_name="seq", num_cores=1), name=name, scratch_types=[sems, sems],
               compiler_params=pltpu.CompilerParams(collective_id=collective_id))
    def launch(send_sem, recv_sem):
        x, y, c = _place()
        peers = peers_of(x, y, c)
        barrier = pltpu.get_barrier_semaphore()
        for peer in peers:
            pl.semaphore_signal(barrier, inc=1, device_id=peer, device_id_type=MESH)
        pl.semaphore_wait(barrier, len(peers))
        program(x, y, c, send_sem, recv_sem)

    launch()


def gather_weights(name, stage, arrays):
    n = len(arrays)
    refs = [_hbm_ref(a) for a in arrays]

    def program(x, y, c, send_sem, recv_sem):
        me = 2 * x + y
        chips = _other_chips(x, y)

        def ici(t, r, half):
            cx, cy = chips[r]
            mine = refs[t].at[me, half]
            return pltpu.make_async_remote_copy(
                src_ref=mine, dst_ref=mine, send_sem=send_sem.at[3 * t + r], recv_sem=recv_sem.at[3 * t + r],
                device_id=(cx, cy, c), device_id_type=MESH)

        def d2d(t, r, half):
            cx, cy = chips[r]
            there = refs[t].at[2 * cx + cy, half]
            k = 3 * n + 3 * t + r
            return pltpu.make_async_remote_copy(
                src_ref=there, dst_ref=there, send_sem=send_sem.at[k], recv_sem=recv_sem.at[k],
                device_id=(x, y, 1 - c), device_id_type=MESH)

        for t in range(n):
            for r in range(3):
                ici(t, r, c).start()
        for t in range(n):
            for r in range(3):
                ici(t, r, c).wait_recv()
                d2d(t, r, c).start()
        for t in range(n):
            for r in range(3):
                d2d(t, r, 1 - c).wait_recv()
        for t in range(n):
            for r in range(3):
                ici(t, r, c).wait_send()
                d2d(t, r, c).wait_send()

    _sequencer(name, stage, 6 * n, lambda x, y, c: [(x, y, 1 - c)] + [(cx, cy, c) for cx, cy in _other_chips(x, y)],
               program)
    return [r[...] for r in refs]


def cast_into_slots(name, chip, shards, after=None):
    steps = 2
    n = len(shards)

    def body(chip_ref, *refs):
        for src, dst in zip(refs[:n], refs[-n - 1:-1]):
            dst[...] = src[...].astype(BF16)
        refs[-1][...] = jnp.zeros_like(refs[-1])

    token_spec = pl.BlockSpec((8, 128), lambda h, i, chip_ref: (0, 0))

    def spec_in(a, prefix):
        R, C = a.shape[-2:]
        return pl.BlockSpec((None,) * (len(prefix) + 1) + (R // steps, C), lambda h, i, chip_ref: prefix + (h, i, 0))

    def spec_out(a):
        R, C = a.shape[-2:]
        return pl.BlockSpec((None, None, R // steps, C), lambda h, i, chip_ref: (chip_ref[0], h, i, 0))

    outs = pl.pallas_call(
        body, name=name,
        grid_spec=pltpu.PrefetchScalarGridSpec(
            num_scalar_prefetch=1, grid=(2, steps),
            in_specs=[spec_in(a, p) for a, p in shards] + ([token_spec] if after is not None else []),
            out_specs=[spec_out(a) for a, _ in shards] + [token_spec]),
        out_shape=[jax.ShapeDtypeStruct((N_CHIP, 2) + a.shape[-2:], BF16) for a, _ in shards]
        + [jax.ShapeDtypeStruct((8, 128), F32)],
        compiler_params=_params("arbitrary", "arbitrary"),
    )(chip, *[a for a, _ in shards], *([after] if after is not None else []))
    return outs[:-1], outs[-1]


def reduce_pair(name, grads):
    n = len(grads)
    src = [_hbm_ref(g) for g in grads]
    dst = [_hbm_empty((N_CHIP,) + g.shape[2:], g.dtype) for g in grads]

    def program(x, y, c, send_sem, recv_sem):
        cps = [pltpu.make_async_remote_copy(
            src_ref=src[t].at[:, 1 - c], dst_ref=dst[t], send_sem=send_sem.at[t], recv_sem=recv_sem.at[t],
            device_id=(x, y, 1 - c), device_id_type=MESH) for t in range(n)]
        for cp in cps:
            cp.start()
        for cp in cps:
            cp.wait()

    _sequencer(name, ID_PAIR, n, lambda x, y, c: [(x, y, 1 - c)], program)
    return [r[...] for r in src], [r[...] for r in dst]


def pair_add(name, core, g, got):
    _, _, R, C = g.shape

    def body(core_ref, g_ref, got_ref, o_ref, token_ref):
        o_ref[...] = (g_ref[...] + got_ref[...]).astype(BF16)
        token_ref[...] = jnp.zeros_like(token_ref)

    return pl.pallas_call(
        body, name=name,
        grid_spec=pltpu.PrefetchScalarGridSpec(
            num_scalar_prefetch=1, grid=(N_CHIP,),
            in_specs=[pl.BlockSpec((None, None, R, C), lambda q, core_ref: (q, core_ref[0], 0, 0)),
                      pl.BlockSpec((None, R, C), lambda q, core_ref: (q, 0, 0))],
            out_specs=[pl.BlockSpec((None, R, C), lambda q, core_ref: (q, 0, 0)),
                       pl.BlockSpec((8, 128), lambda q, core_ref: (0, 0))]),
        out_shape=[jax.ShapeDtypeStruct((N_CHIP, R, C), BF16), jax.ShapeDtypeStruct((8, 128), F32)],
        compiler_params=_params("arbitrary"),
    )(core, g, got)


def reduce_chips(name, sums):
    n = len(sums)
    src = [_hbm_ref(s) for s in sums]
    dst = [_hbm_empty((3,) + s.shape[1:], s.dtype) for s in sums]

    def program(x, y, c, send_sem, recv_sem):
        cps = []
        for t in range(n):
            for r, (cx, cy) in enumerate(_other_chips(x, y)):
                cps.append(pltpu.make_async_remote_copy(
                    src_ref=src[t].at[2 * cx + cy], dst_ref=dst[t].at[r],
                    send_sem=send_sem.at[3 * t + r], recv_sem=recv_sem.at[3 * t + r],
                    device_id=(cx, cy, c), device_id_type=MESH))
        for cp in cps:
            cp.start()
        for cp in cps:
            cp.wait()

    _sequencer(name, ID_CHIPS, 3 * n, lambda x, y, c: [(cx, cy, c) for cx, cy in _other_chips(x, y)], program)
    return [r[...] for r in src], [r[...] for r in dst]


def chip_add(name, place, s, got, k, n_slots, prev=None):
    _, R, C = s.shape

    def body(place_ref, s_ref, got_ref, *rest):
        o_ref, token_ref = rest[-2:]
        o_ref[...] = ((s_ref[...].astype(F32) + got_ref[0].astype(F32)) + got_ref[1].astype(F32)) + got_ref[2].astype(F32)
        token_ref[...] = jnp.zeros_like(token_ref)

    in_specs = [pl.BlockSpec((None, R, C), lambda i, place_ref: (place_ref[0], 0, 0)),
                pl.BlockSpec((3, R, C), lambda i, place_ref: (0, 0, 0))]
    args = [place, s, got]
    aliases = {}
    if prev is not None:
        in_specs.append(_ANY)
        args.append(prev)
        aliases = {3: 0}
    return pl.pallas_call(
        body, name=name,
        grid_spec=pltpu.PrefetchScalarGridSpec(
            num_scalar_prefetch=1, grid=(1,), in_specs=in_specs,
            out_specs=[pl.BlockSpec((None, None, R, C), lambda i, place_ref: (k, place_ref[1], 0, 0)),
                       pl.BlockSpec((8, 128), lambda i, place_ref: (0, 0))]),
        out_shape=[jax.ShapeDtypeStruct((n_slots, 2, R, C), F32), jax.ShapeDtypeStruct((8, 128), F32)],
        input_output_aliases=aliases,
        compiler_params=_params("arbitrary"),
    )(*args)


def share_halves(name, stacks, slots):
    n = len(stacks)
    dst = [_hbm_ref(s) for s in stacks]

    def program(x, y, c, send_sem, recv_sem):
        cps = [pltpu.make_async_remote_copy(
            src_ref=dst[t].at[slots[t], c], dst_ref=dst[t].at[slots[t], c],
            send_sem=send_sem.at[t], recv_sem=recv_sem.at[t],
            device_id=(x, y, 1 - c), device_id_type=MESH) for t in range(n)]
        for cp in cps:
            cp.start()
        for cp in cps:
            cp.wait()

    _sequencer(name, ID_SHARE, n, lambda x, y, c: [(x, y, 1 - c)], program)
    return [r[...] for r in dst]


def gather_blocks(name, slotted):
    out = _hbm_ref(slotted)

    def program(x, y, c, send_sem, recv_sem):
        sibling = (x, y, 1 - c)
        chips = _other_chips(x, y)

        def copy(k, px, py, pc, to):
            block = out.at[4 * px + 2 * py + pc]
            return pltpu.make_async_remote_copy(src_ref=block, dst_ref=block, send_sem=send_sem.at[k],
                                                recv_sem=recv_sem.at[k], device_id=to, device_id_type=MESH)

        first = [copy(0, x, y, c, sibling)] + [copy(1 + j, x, y, c, (cx, cy, c)) for j, (cx, cy) in enumerate(chips)]
        for cp in first:
            cp.start()
        passed = [copy(4 + j, cx, cy, c, sibling) for j, (cx, cy) in enumerate(chips)]
        for j, (cx, cy) in enumerate(chips):
            copy(1 + j, cx, cy, c, (x, y, c)).wait_recv()
            passed[j].start()
        copy(0, x, y, 1 - c, (x, y, c)).wait_recv()
        for j, (cx, cy) in enumerate(chips):
            copy(4 + j, cx, cy, 1 - c, (x, y, c)).wait_recv()
        for cp in first + passed:
            cp.wait_send()

    _sequencer(name, ID_UKV, 7, lambda x, y, c: [(x, y, 1 - c)] + [(cx, cy, c) for cx, cy in _other_chips(x, y)],
               program)
    return out[...]


def place_block(name, dev, a):
    M, N = a.shape
    tr = min(M, 64)

    def body(dev_ref, a_ref, o_ref):
        o_ref[...] = a_ref[...]

    return pl.pallas_call(
        body, name=name,
        grid_spec=pltpu.PrefetchScalarGridSpec(
            num_scalar_prefetch=1, grid=(M // tr,),
            in_specs=[pl.BlockSpec((tr, N), lambda i, dev_ref: (i, 0))],
            out_specs=pl.BlockSpec((None, tr, N), lambda i, dev_ref: (dev_ref[0], i, 0))),
        out_shape=jax.ShapeDtypeStruct((N_DEV, M, N), a.dtype),
        compiler_params=_params("parallel"),
    )(dev, a)


def _swap_rope(a):
    return jnp.concatenate([a[..., ROPE // 2:], a[..., :ROPE // 2]], axis=-1)


def _rope_tables(S):
    inv = 1.0 / (ROPE_THETA ** (jnp.arange(0, ROPE, 2, dtype=F32) / ROPE))
    ang = jnp.arange(S, dtype=F32)[:, None] * inv[None, :]
    cos, sin = jnp.cos(ang), jnp.sin(ang)
    return (jnp.tile(jnp.concatenate([cos, cos], axis=1), (1, SLOTS)),
            jnp.tile(jnp.concatenate([-sin, sin], axis=1), (1, SLOTS)))


def _vec(norm_g, mod, i, k):
    rows = [norm_g[i, 2 * k], norm_g[i, 2 * k + 1], mod[i, 3 * k], mod[i, 3 * k + 1], mod[i, 3 * k + 2]]
    return jnp.concatenate([jnp.stack(rows), jnp.zeros((3, D), F32)], axis=0)


def _unpack_weights(full, w_uk, w_uv, q_norm, kv_norm):
    G = D // 4
    ffn_in = [[full[2 * i + k].reshape(N_CHIP, D, FSH) for k in range(2)] for i in range(2)]
    ffn_out = [[full[4 + 2 * i + k].reshape(2, FSH, D) for k in range(2)] for i in range(2)]
    pw = full[8].reshape(N_CHIP, 4, G // N_CHIP, G).transpose(1, 0, 2, 3).reshape(4, G, G)
    w_in = full[9].reshape(D, QL + KVL + ROPE)
    w_uq = full[10].reshape(QL, N_HEADS, NOPE + ROPE)
    wkr = w_in[:, QL + KVL:]
    wr = w_uq[:, :, NOPE:]
    eye2 = jnp.eye(2, dtype=BF16)
    uk_t = jnp.transpose(w_uk, (1, 2, 0)).reshape(N_PAIR, 2, NOPE, KVL)
    bduk = jnp.einsum("janc,ab->janbc", uk_t, eye2).reshape(N_PAIR, 2 * NOPE, 2 * KVL)
    uv = jnp.transpose(w_uv, (1, 0, 2)).reshape(N_PAIR, 2, KVL, VH)
    bduv = jnp.einsum("jacn,ab->jacbn", uv, eye2).reshape(N_PAIR, 2 * KVL, 2 * VH)
    mw = dict(wq=w_in[:, :QL], wkv=w_in[:, QL:QL + KVL], wkr4=jnp.tile(wkr, (1, SLOTS)),
              wkrs4=jnp.tile(_swap_rope(wkr), (1, SLOTS)), qn=q_norm, kvn=kv_norm,
              wn=w_uq[:, :, :NOPE].reshape(QL, NOPE_ALL), wr=wr.reshape(QL, ROPE_ALL),
              wrs=_swap_rope(wr).reshape(QL, ROPE_ALL), bduk=bduk)
    return ffn_in, ffn_out, pw, mw, bduv, full[11].reshape(D, D)


def _example_step(x, target, mod, norm_g, pvec, ffn_in, ffn_out, pw, mw, bduv, wo, reducer):
    S = x.shape[0]
    tabs = _rope_tables(S)
    vec = [[_vec(norm_g, mod, i, k) for k in range(3)] for i in range(2)]
    saved = {}
    for i in range(2):
        xin = x
        x, a, u, h = ffn_fwd(xin, vec[i][0], ffn_in[i][0], ffn_out[i][0], 0.5)
        saved[i, 0] = (xin, a, u, h)
        xin = x
        if i == 0:
            x, y, z = pool_fwd(xin, vec[i][1], pw, pvec)
            saved[i, 1] = (xin, y, z)
        else:
            h_m, cq_raw, ckv_raw, cqn, qnope, qcat, kcat, vcat = mla_pre(xin, vec[i][1], mw, tabs)
            olat, lse = attn_fwd(qcat, kcat, vcat)
            x, u_m, ocat = mla_post(olat, xin, vec[i][1], bduv, wo)
            saved[i, 1] = (xin, h_m, cq_raw, ckv_raw, cqn, qnope, qcat, kcat, olat, lse, u_m, ocat)
        xin = x
        x, a, u, h = ffn_fwd(xin, vec[i][2], ffn_in[i][1], ffn_out[i][1], 0.5)
        saved[i, 2] = (xin, a, u, h)
    loss, dx = loss_head(x, target)

    vg = {}
    G = D // 4

    def ffn_grads(i, k, dw_in, dw_out):
        return [(0, 2 * i + k, 4, dw_in.reshape(N_CHIP, 2, D // 2, FSH)),
                (1, 2 * i + k, 4, dw_out.reshape(N_CHIP, 2, DFF // 8, D))]

    vec_ffn2 = vec[1][2]
    for i in (1, 0):
        xin, a, u, h = saved[i, 2]
        dx, du, act, da, vg[i, 2] = ffn_bwd(dx, xin, u, a, vec_ffn2, ffn_in[i][1], ffn_out[i][1], 0.5)
        vec_mixer = reducer.advance(vec[i][1])
        reducer.add(f"f{i}1", ffn_grads(i, 1, *ffn_dw(h, da, act, du)))
        if i == 0:
            xin, y, z = saved[i, 1]
            dx, vg[i, 1], pgrad, g_pool = pool_bwd(dx, xin, y, z, vec_mixer, pw, pvec)
            vec_next = reducer.advance(vec[i][0])
        else:
            xin, h_m, cq_raw, ckv_raw, cqn, qnope, qcat, kcat, olat, lse, u_m, ocat = saved[i, 1]
            du, docat, dolat, delta, vg_post = mla_post_bwd(dx, u_m, olat, vec_mixer, bduv, wo)
            reducer.advance()
            dq, dk, dv = attn_bwd(qcat, kcat, kcat.T, dolat, lse, delta)
            dx, dlat, dql, dqcat, vg_pre, ngrad = mla_pre_bwd(
                dx, dq, dk, dv, xin, cq_raw, ckv_raw, reducer.advance(vec[i][1]), mw, tabs)
            vec_next = vec[i][0]
            vg[i, 1] = vg_post + vg_pre
            g = mla_dw(h_m, dlat, cqn, dqcat, dql, qnope, olat, docat, ocat, du)
            slots = lambda a: a.reshape(D, SLOTS, ROPE).sum(axis=1)
            g_kr = slots(g["in"][:, QL + KVL:QL + KVL + 128]) + _swap_rope(slots(g["in"][:, QL + KVL + 128:]))
            g_in = jnp.concatenate([g["in"][:, :QL + KVL], g_kr], axis=1)
            g_r = g["q"][:, NOPE_ALL:NOPE_ALL + ROPE_ALL].reshape(QL, N_HEADS, ROPE)
            g_rs = g["q"][:, NOPE_ALL + ROPE_ALL:].reshape(QL, N_HEADS, ROPE)
            g_uq = jnp.concatenate([g["q"][:, :NOPE_ALL].reshape(QL, N_HEADS, NOPE), g_r + _swap_rope(g_rs)], axis=-1)

            def heads(pairs):
                blk = pairs.reshape(N_PAIR, 2, KVL, 2, NOPE)
                per_head = jnp.stack([blk[:, 0, :, 0, :], blk[:, 1, :, 1, :]], axis=1).reshape(N_HEADS, KVL, NOPE)
                return jnp.transpose(per_head, (1, 0, 2)).reshape(KVL, N_HEADS * NOPE)

            reducer.add("mla", [(3, 0, 1, g_in.reshape(N_CHIP, 2, D // 8, QL + KVL + ROPE)),
                                (4, 0, 1, g_uq.reshape(N_CHIP, 2, QL // 8, N_HEADS * (NOPE + ROPE))),
                                (5, 0, 1, g["o"].reshape(N_CHIP, 2, D // 8, D))])
            reducer.add_replicated(jnp.concatenate([heads(g["uk"]), heads(g["uv"])], axis=0))
        xin, a, u, h = saved[i, 0]
        dx, du, act, da, vg[i, 0] = ffn_bwd(dx, xin, u, a, vec_next, ffn_in[i][0], ffn_out[i][0], 0.5)
        vec_ffn2 = reducer.advance(vec[0][2])
        grads = ffn_grads(i, 0, *ffn_dw(h, da, act, du))
        if i == 0:
            grads.append((2, 0, 1, g_pool.reshape(N_CHIP, 2, 2 * G // N_CHIP, G)))
        reducer.add(f"f{i}0", grads)
    return loss, dx, vg, pgrad, ngrad


class _GradReducer:
    def __init__(self, core, place, dev):
        self.core, self.place, self.dev = core, place, dev
        self.stacks = {}
        self.live = []
        self.replicated = None

    def add(self, tag, items):
        gen = self._run(tag, items)
        next(gen)
        self.live.append(gen)

    def add_replicated(self, block):
        self.replicated = gather_blocks("gather_ukv", place_block("place_ukv", self.dev, block))

    def advance(self, operand=None):
        live = []
        for gen in self.live:
            try:
                next(gen)
                live.append(gen)
            except StopIteration:
                pass
        self.live = live
        return operand

    def finish(self):
        while self.live:
            self.advance()
        return self.stacks, self.replicated

    def _run(self, tag, items):
        grads, from_pair = reduce_pair(f"reduce_pair_{tag}", [g for *_, g in items])
        yield []
        sums, tokens = [], []
        for j, (g, p) in enumerate(zip(grads, from_pair)):
            s, token = pair_add(f"pair_add_{tag}_{j}", self.core, g, p)
            sums.append(s)
            tokens.append(token)
        sums, from_chips = reduce_chips(f"reduce_chips_{tag}", sums)
        yield tokens
        tokens = []
        for j, ((o, k, n_slots, _), s, p) in enumerate(zip(items, sums, from_chips)):
            self.stacks[o], token = chip_add(f"chip_add_{tag}_{j}", self.place, s, p, k, n_slots, self.stacks.get(o))
            tokens.append(token)
        shared = share_halves(f"share_halves_{tag}", [self.stacks[o] for o, *_ in items], [k for _, k, *_ in items])
        for (o, *_), v in zip(items, shared):
            self.stacks[o] = v
        yield tokens


SMALL_IN = 8 * 640
SMALL_GRAD = 8 * 4224
SMALL_W = 8 * 2944


def _pack(parts, total):
    flat = jnp.concatenate([p.reshape(-1) for p in parts])
    return jnp.concatenate([flat, jnp.zeros((total - flat.shape[0],), F32)]).reshape(8, total // 8)


def kernel(x, c, ada_w, ada_b, norm_g, ffn_w_in, ffn_w_out, pool_w, pool_b, pool_scale, mla_w_in, mla_q_norm, mla_kv_norm, mla_w_uq, mla_w_uk, mla_w_uv, mla_w_o, loss_target, m_ada_w, m_ada_b, m_norm_g, m_ffn_w_in, m_ffn_w_out, m_pool_w, m_pool_b, m_pool_scale, m_mla_w_in, m_mla_q_norm, m_mla_kv_norm, m_mla_w_uq, m_mla_w_uk, m_mla_w_uv, m_mla_w_o, v_ada_w, v_ada_b, v_norm_g, v_ffn_w_in, v_ffn_w_out, v_pool_w, v_pool_b, v_pool_scale, v_mla_w_in, v_mla_q_norm, v_mla_kv_norm, v_mla_w_uq, v_mla_w_uk, v_mla_w_uv, v_mla_w_o):
    ix, iy, ic = _place()
    chip = 2 * ix + iy
    dev = 2 * chip + ic
    core_arr = ic.astype(jnp.int32).reshape(1)
    chip_arr = chip.astype(jnp.int32).reshape(1)
    S = x.shape[1]
    G = D // 4
    NG = D // N_CHIP

    def chip_cols(a, width, axis):
        return lax.dynamic_slice_in_dim(a, chip * width, width, axis)

    got = gather_devices("gather_small_in", _pack([c, norm_g, pool_b, mla_q_norm], SMALL_IN)).reshape(N_DEV, SMALL_IN)
    c_all = got[:, :D]
    parts = got[0::2]
    o = D
    norm_g_full = parts[:, o:o + 12 * NG].reshape(N_CHIP, 2, 6, NG).transpose(1, 2, 0, 3).reshape(2, 6, D)
    o += 12 * NG
    pool_b_full = parts[:, o:o + G].reshape(N_CHIP, 4, G // N_CHIP).transpose(1, 0, 2).reshape(1, D)
    o += G
    q_norm_full = parts[:, o:o + QL // N_CHIP].reshape(1, QL)
    pvec = jnp.concatenate([pool_b_full, pool_scale, jnp.zeros((6, D), F32)], axis=0)

    c_pad = jnp.concatenate([c_all, jnp.zeros((8, D), F32)], axis=0)
    mod_loc = mod_fwd(c_pad, ada_w, chip_cols(ada_b, MOD_COLS, 1).reshape(2, 1, MOD_COLS))
    got = gather_devices("gather_mod", mod_loc[:, :8].transpose(1, 0, 2).reshape(8, 2 * MOD_COLS))
    mine = lax.dynamic_index_in_dim(got[0::2].reshape(N_CHIP, 8, 2, MOD_COLS), dev, axis=1, keepdims=False)
    mod = mine.transpose(1, 0, 2).reshape(2, 9, D)

    bf = lambda a: a.astype(BF16)
    w_in_halves = ffn_w_in.reshape(2, 2, 2, D // 2, FSH)
    w_out_halves = ffn_w_out.reshape(2, 2, 2, DFF // 8, D)
    shards = [(w_in_halves, (i, k)) for i in range(2) for k in range(2)]
    shards += [(w_out_halves, (i, k)) for i in range(2) for k in range(2)]
    shards += [(pool_w.reshape(2, 2 * G // N_CHIP, G), ()), (mla_w_in.reshape(2, D // 8, QL + KVL + ROPE), ()),
               (mla_w_uq.reshape(2, QL // 8, N_HEADS * (NOPE + ROPE)), ()), (mla_w_o.reshape(2, D // 8, D), ())]
    full = [None] * len(shards)
    stages = [(0, 4, 8), (1, 5), (2, 6), (9, 10, 11), (3, 7)]
    first, token = cast_into_slots("cast_first", chip_arr, [shards[t] for t in stages[0]])
    slotted = dict(zip(stages[0], first))
    rest = [t for members in stages[1:] for t in members]
    for stage, members in enumerate(stages):
        got_w = gather_weights(f"gather_weights_{stage}", stage, [slotted[t] for t in members])
        for t, a in zip(members, got_w):
            full[t] = a
        if stage == 0:
            slotted.update(zip(rest, cast_into_slots("cast_rest", chip_arr, [shards[t] for t in rest], token)[0]))
    ffn_in, ffn_out, pw, mw, bduv, wo = _unpack_weights(full, bf(mla_w_uk[0]), bf(mla_w_uv[0]), q_norm_full,
                                                        mla_kv_norm)

    place_arr = jnp.stack([chip, ic]).astype(jnp.int32)
    reducer = _GradReducer(core_arr, place_arr, dev.astype(jnp.int32).reshape(1))
    loss_mine, grad_x, vg, pgrad, ngrad = _example_step(
        x[0], loss_target[0], mod, norm_g_full, pvec, ffn_in, ffn_out, pw, mw, bduv, wo, reducer)
    loss = lax.psum(loss_mine[0, 0], ("x", "y", "c"))
    stacks, ukv = reducer.finish()
    g_ffn_in, g_ffn_out, g_pool_w, g_mla_in, g_uq, g_wo = [stacks[o] for o in range(6)]
    g_ffn_in = g_ffn_in.reshape(ffn_w_in.shape)
    g_ffn_out = g_ffn_out.reshape(ffn_w_out.shape)
    g_pool_w = g_pool_w.reshape(pool_w.shape)
    g_mla_in = g_mla_in.reshape(mla_w_in.shape)
    g_uq = g_uq.reshape(mla_w_uq.shape)
    g_wo = g_wo.reshape(mla_w_o.shape)

    ukv = sum_devices("sum_ukv", ukv)
    g_uk = ukv[:KVL].reshape(mla_w_uk.shape)
    g_uv = ukv[KVL:].reshape(mla_w_uv.shape)

    dmod = jnp.stack([jnp.concatenate([vg[i, k][0:3] for k in range(3)]) for i in range(2)])
    dnorm = jnp.stack([jnp.concatenate([vg[i, k][3:5] for k in range(3)]) for i in range(2)])
    small = _pack([dmod, dnorm, pgrad[0], pgrad[1], ngrad[0], ngrad[1, :KVL]], SMALL_GRAD)
    got = gather_devices("gather_small_grad", small)
    tot = sum_devices("sum_small_grad", got).reshape(-1)
    n_mod = 2 * 9 * D
    g_ada_b = tot[:n_mod].reshape(ada_b.shape)
    o = n_mod
    g_norm = chip_cols(tot[o:o + 12 * D].reshape(2, 6, D), NG, 2)
    o += 12 * D
    g_pool_b = chip_cols(tot[o:o + D].reshape(1, 4, G), G // N_CHIP, 2)
    o += D
    g_pool_scale = tot[o:o + D].reshape(pool_scale.shape)
    o += D
    g_q_norm = chip_cols(tot[o:o + QL].reshape(1, QL), QL // N_CHIP, 1)
    o += QL
    g_kv_norm = tot[o:o + KVL].reshape(mla_kv_norm.shape)
    dmod_all = chip_cols(got.reshape(N_DEV, -1)[:, :n_mod].reshape(N_DEV, 2, 9 * D), MOD_COLS, 2)
    dmod_pad = jnp.concatenate([dmod_all.transpose(1, 0, 2), jnp.zeros((2, 8, MOD_COLS), F32)], axis=1)

    g_ada_w, d_ada_w, nm_ada_w, nv_ada_w = adamw_ada(c_pad, dmod_pad, ada_w, m_ada_w, v_ada_w)
    small_names = ["ada_b", "norm_g", "pool_b", "pool_scale", "mla_q_norm", "mla_kv_norm"]
    small_w = [ada_b, norm_g, pool_b, pool_scale, mla_q_norm, mla_kv_norm]
    small_g = [g_ada_b, g_norm, g_pool_b, g_pool_scale, g_q_norm, g_kv_norm]
    small_m = [m_ada_b, m_norm_g, m_pool_b, m_pool_scale, m_mla_q_norm, m_mla_kv_norm]
    small_v = [v_ada_b, v_norm_g, v_pool_b, v_pool_scale, v_mla_q_norm, v_mla_kv_norm]
    packed = adamw("adamw_small", *[_pack(p, SMALL_W) for p in (small_w, small_g, small_m, small_v)])
    upd = {}
    o = 0
    for name, w in zip(small_names, small_w):
        upd[name] = [p.reshape(-1)[o:o + w.size].reshape(w.shape) for p in packed]
        o += w.size
    big = [("ffn_w_in", ffn_w_in, g_ffn_in, m_ffn_w_in, v_ffn_w_in),
           ("ffn_w_out", ffn_w_out, g_ffn_out, m_ffn_w_out, v_ffn_w_out),
           ("pool_w", pool_w, g_pool_w, m_pool_w, v_pool_w),
           ("mla_w_in", mla_w_in, g_mla_in, m_mla_w_in, v_mla_w_in),
           ("mla_w_uq", mla_w_uq, g_uq, m_mla_w_uq, v_mla_w_uq),
           ("mla_w_uk", mla_w_uk, g_uk, m_mla_w_uk, v_mla_w_uk),
           ("mla_w_uv", mla_w_uv, g_uv, m_mla_w_uv, v_mla_w_uv),
           ("mla_w_o", mla_w_o, g_wo, m_mla_w_o, v_mla_w_o)]
    for name, w, g, m, v in big:
        upd[name] = adamw("adamw_" + name, w, g, m, v)
    upd["ada_w"] = [d_ada_w, nm_ada_w, nv_ada_w]

    order = ["ada_w", "ada_b", "norm_g", "ffn_w_in", "ffn_w_out", "pool_w", "pool_b", "pool_scale", "mla_w_in",
             "mla_q_norm", "mla_kv_norm", "mla_w_uq", "mla_w_uk", "mla_w_uv", "mla_w_o"]
    grad = dict(ada_w=g_ada_w, ada_b=g_ada_b, norm_g=g_norm, ffn_w_in=g_ffn_in, ffn_w_out=g_ffn_out, pool_w=g_pool_w,
                pool_b=g_pool_b, pool_scale=g_pool_scale, mla_w_in=g_mla_in, mla_q_norm=g_q_norm,
                mla_kv_norm=g_kv_norm, mla_w_uq=g_uq, mla_w_uk=g_uk, mla_w_uv=g_uv, mla_w_o=g_wo)
    return (loss, grad_x[None], *[grad[n] for n in order], *[upd[n][0] for n in order],
            *[upd[n][1] for n in order], *[upd[n][2] for n in order])
```

```python
import functools

import jax
import jax.numpy as jnp
from jax import lax
from jax.experimental import pallas as pl
from jax.experimental.pallas import tpu as pltpu
from jax.experimental.pallas import tpu_sc as plsc

F32 = jnp.float32
BF16 = jnp.bfloat16

D = 1024
DFF = 2816
FSH = 1408
N_CHIP = 4
N_DEV = 8
N_HEADS = 16
NOPE = 64
ROPE = 32
VH = 64
QL = 256
KVL = 128
QPAD = 256
EPS = 1e-6
ATTN_SCALE = (NOPE + ROPE) ** -0.5
ROPE_THETA = 10000.0
POOL_WINDOWS = (2, 4, 8, 16)
HALO = 8
ATTN_TQ = 1024
ATTN_KC = 512
DW_TK = 2048

ADAM_LR, ADAM_B1, ADAM_B2, ADAM_EPS, ADAM_WD, ADAM_STEP = 0.001, 0.9, 0.999, 1e-08, 0.01, 10

VMEM_LIMIT = 60 * 1024 * 1024
MESH = pl.DeviceIdType.MESH

NT = (((1,), (1,)), ((), ()))
TN = (((0,), (0,)), ((), ()))


def _params(*sem):
    return pltpu.CompilerParams(dimension_semantics=sem, vmem_limit_bytes=VMEM_LIMIT)


def _dot(a, b, dims=None):
    if dims is None:
        return jnp.dot(a, b, preferred_element_type=F32)
    return lax.dot_general(a, b, dims, preferred_element_type=F32)


def _rms(x):
    r = lax.rsqrt(jnp.mean(x * x, axis=-1, keepdims=True) + EPS)
    return x * r, r


def _rms_bwd(xhat, r, dxhat):
    return r * (dxhat - xhat * jnp.mean(dxhat * xhat, axis=-1, keepdims=True))


def _as_row(col):
    return jnp.broadcast_to(col, (col.shape[0], 128)).T[0:1, :]


def _prenorm(x, vec_ref):
    xhat, r = _rms(x)
    h = xhat * vec_ref[0:1, :] * (1.0 + vec_ref[3:4, :]) + vec_ref[2:3, :]
    return h, xhat, r


def _postnorm_bwd(dout, u, vec_ref, weight):
    uhat, r = _rms(u)
    gt = weight * (1.0 + vec_ref[4:5, :])
    dy = dout * gt
    dgate_rows = (weight * dout) * (uhat * vec_ref[1:2, :])
    dgpost_rows = dy * uhat
    du = _rms_bwd(uhat, r, dy * vec_ref[1:2, :])
    return du, dgate_rows, dgpost_rows


def _prenorm_bwd(dh, x, vec_ref, vg_ref):
    xhat, r = _rms(x)
    sc1 = 1.0 + vec_ref[3:4, :]
    g = vec_ref[0:1, :]
    vg_ref[0:1, :] += jnp.sum(dh, axis=0, keepdims=True)
    vg_ref[1:2, :] += jnp.sum(dh * (xhat * g), axis=0, keepdims=True)
    vg_ref[3:4, :] += jnp.sum(dh * sc1 * xhat, axis=0, keepdims=True)
    return _rms_bwd(xhat, r, dh * g * sc1)


def ffn_fwd(x, vec, w_in, w_out, weight):
    S = x.shape[0]
    tm = min(512, S)

    def body(x_ref, vec_ref, wg_ref, wu_ref, wo_ref, xo_ref, a_ref, u_ref, h_ref, acc_ref):
        j = pl.program_id(1)

        @pl.when(j == 0)
        def _():
            h, _, _ = _prenorm(x_ref[...], vec_ref)
            h_ref[...] = h.astype(BF16)
            acc_ref[...] = jnp.zeros_like(acc_ref)

        hb = h_ref[...]
        g = _dot(hb, wg_ref[...])
        up = _dot(hb, wu_ref[...])
        a_ref[0] = g.astype(BF16)
        a_ref[1] = up.astype(BF16)
        act = (g * jax.nn.sigmoid(g)) * up
        acc_ref[...] += _dot(act.astype(BF16), wo_ref[...])

        @pl.when(j == 1)
        def _():
            u = acc_ref[...]
            u_ref[...] = u
            uhat, _ = _rms(u)
            xo_ref[...] = x_ref[...] + (weight * (1.0 + vec_ref[4:5, :])) * (uhat * vec_ref[1:2, :])

    return pl.pallas_call(
        body, name="ffn_fwd", grid=(S // tm, 2),
        in_specs=[pl.BlockSpec((tm, D), lambda i, j: (i, 0)),
                  pl.BlockSpec((8, D), lambda i, j: (0, 0)),
                  pl.BlockSpec((None, D, FSH), lambda i, j: (j, 0, 0)),
                  pl.BlockSpec((None, D, FSH), lambda i, j: (j + 2, 0, 0)),
                  pl.BlockSpec((None, FSH, D), lambda i, j: (j, 0, 0))],
        out_specs=[pl.BlockSpec((tm, D), lambda i, j: (i, 0)),
                   pl.BlockSpec((2, tm, FSH), lambda i, j: (0, i, j)),
                   pl.BlockSpec((tm, D), lambda i, j: (i, 0)),
                   pl.BlockSpec((tm, D), lambda i, j: (i, 0))],
        out_shape=[jax.ShapeDtypeStruct((S, D), F32), jax.ShapeDtypeStruct((2, S, DFF), BF16),
                   jax.ShapeDtypeStruct((S, D), F32), jax.ShapeDtypeStruct((S, D), BF16)],
        scratch_shapes=[pltpu.VMEM((tm, D), F32)],
        compiler_params=_params("parallel", "arbitrary"),
    )(x, vec, w_in, w_in, w_out)


def ffn_bwd(dout, x, u, a, vec, w_in, w_out, weight):
    S = x.shape[0]
    tm = min(512, S)
    row = lambda i: (i, 0)
    half = lambda j: [pl.BlockSpec((2, tm, FSH), lambda i: (0, i, j)), _w3((8, D)),
                      pl.BlockSpec((None, D, FSH), lambda i: (j, 0, 0)),
                      pl.BlockSpec((None, D, FSH), lambda i: (j + 2, 0, 0)),
                      pl.BlockSpec((None, FSH, D), lambda i: (j, 0, 0))]
    half_out = lambda j: [pl.BlockSpec((tm, FSH), lambda i: (i, j)), pl.BlockSpec((2, tm, FSH), lambda i: (0, i, j))]
    half_shape = [jax.ShapeDtypeStruct((S, DFF), BF16), jax.ShapeDtypeStruct((2, S, DFF), BF16)]

    def hidden_bwd(du, a_ref, wg_ref, wu_ref, wo_ref, act_ref, da_ref):
        dact = _dot(du, wo_ref[...], NT)
        g = a_ref[0].astype(F32)
        up = a_ref[1].astype(F32)
        s = jax.nn.sigmoid(g)
        silu = g * s
        act_ref[...] = (silu * up).astype(BF16)
        dg = (dact * up * (s * (1.0 + g * (1.0 - s)))).astype(BF16)
        dup = (dact * silu).astype(BF16)
        da_ref[0] = dg
        da_ref[1] = dup
        return _dot(dg, wg_ref[...], NT) + _dot(dup, wu_ref[...], NT)

    def first(do_ref, u_ref, a_ref, vec_ref, wg_ref, wu_ref, wo_ref, du_ref, dh_ref, act_ref, da_ref, vg_ref):
        @pl.when(pl.program_id(0) == 0)
        def _():
            vg_ref[...] = jnp.zeros_like(vg_ref)

        du, dgate_rows, dgpost_rows = _postnorm_bwd(do_ref[...], u_ref[...], vec_ref, weight)
        vg_ref[2:3, :] += jnp.sum(dgate_rows, axis=0, keepdims=True)
        vg_ref[4:5, :] += jnp.sum(dgpost_rows, axis=0, keepdims=True)
        du = du.astype(BF16)
        du_ref[...] = du
        dh_ref[...] = hidden_bwd(du, a_ref, wg_ref, wu_ref, wo_ref, act_ref, da_ref)

    du, dh, act, da, vg_post = pl.pallas_call(
        first, name="ffn_bwd_first", grid=(S // tm,),
        in_specs=[pl.BlockSpec((tm, D), row), pl.BlockSpec((tm, D), row)] + half(0),
        out_specs=[pl.BlockSpec((tm, D), row), pl.BlockSpec((tm, D), row)] + half_out(0) + [_w3((8, D))],
        out_shape=[jax.ShapeDtypeStruct((S, D), BF16), jax.ShapeDtypeStruct((S, D), F32)] + half_shape
        + [jax.ShapeDtypeStruct((8, D), F32)],
        compiler_params=_params("arbitrary"),
    )(dout, u, a, vec, w_in, w_in, w_out)

    def second(do_ref, x_ref, du_ref, dh_ref, a_ref, vec_ref, wg_ref, wu_ref, wo_ref, act_in, da_in,
               dx_ref, act_ref, da_ref, vg_ref):
        @pl.when(pl.program_id(0) == 0)
        def _():
            vg_ref[...] = jnp.zeros_like(vg_ref)

        dh = dh_ref[...] + hidden_bwd(du_ref[...], a_ref, wg_ref, wu_ref, wo_ref, act_ref, da_ref)
        dx_ref[...] = do_ref[...] + _prenorm_bwd(dh, x_ref[...], vec_ref, vg_ref)

    dx, act, da, vg_pre = pl.pallas_call(
        second, name="ffn_bwd_second", grid=(S // tm,),
        in_specs=[pl.BlockSpec((tm, D), row), pl.BlockSpec((tm, D), row), pl.BlockSpec((tm, D), row),
                  pl.BlockSpec((tm, D), row)] + half(1) + [_ANY, _ANY],
        out_specs=[pl.BlockSpec((tm, D), row)] + half_out(1) + [_w3((8, D))],
        out_shape=[jax.ShapeDtypeStruct((S, D), F32)] + half_shape + [jax.ShapeDtypeStruct((8, D), F32)],
        input_output_aliases={9: 1, 10: 2},
        compiler_params=_params("arbitrary"),
    )(dout, x, du, dh, a, vec, w_in, w_in, w_out, act, da)
    return dx, du, act, da, vg_post + vg_pre


def dw_matmul(name, a, b, a_spec, b_spec, out_shape, out_spec, grid):
    def body(a_ref, b_ref, o_ref):
        @pl.when(pl.program_id(len(grid) - 1) == 0)
        def _():
            o_ref[...] = jnp.zeros_like(o_ref)

        o_ref[...] += _dot(a_ref[...], b_ref[...], TN)

    return pl.pallas_call(
        body, name=name, grid=grid, in_specs=[a_spec, b_spec], out_specs=out_spec,
        out_shape=jax.ShapeDtypeStruct(out_shape, F32),
        compiler_params=_params(*(["parallel"] * (len(grid) - 1) + ["arbitrary"])),
    )(a, b)


def ffn_dw(h, da, act, du):
    S = h.shape[0]
    tk = min(DW_TK, S)
    dw_in = dw_matmul("ffn_dw_in", h, da,
                      pl.BlockSpec((tk, D), lambda n, k: (k, 0)),
                      pl.BlockSpec((None, tk, FSH), lambda n, k: (n // 2, k, n % 2)),
                      (N_CHIP, D, FSH), pl.BlockSpec((None, D, FSH), lambda n, k: (n, 0, 0)),
                      (N_CHIP, S // tk))
    dw_out = dw_matmul("ffn_dw_out", act, du,
                       pl.BlockSpec((tk, FSH), lambda n, k: (k, n)),
                       pl.BlockSpec((tk, D), lambda n, k: (k, 0)),
                       (DFF, D), pl.BlockSpec((FSH, D), lambda n, k: (n, 0)),
                       (2, S // tk))
    return dw_in, dw_out


def _halo_specs(tm, S):
    nb = tm // HALO
    last = S // HALO - 1
    return [pl.BlockSpec((HALO, D), lambda i: (jnp.maximum(i * nb - 1, 0), 0)),
            pl.BlockSpec((tm, D), lambda i: (i, 0)),
            pl.BlockSpec((HALO, D), lambda i: (jnp.minimum((i + 1) * nb, last), 0))]


def _shift_rows(v, k):
    return pltpu.roll(v, k % v.shape[0], 0)


def _window_sum(v, g, forward):
    acc = v + _shift_rows(v, 1 if forward else -1)
    for step in (1, 2, 4)[:g]:
        acc = _shift_rows(acc, step) + _shift_rows(acc, -step)
    return acc


def _pool_count(t, w, S):
    return jnp.maximum(jnp.minimum(t + w // 2, S) - jnp.maximum(t - w // 2, 0), 1).astype(F32)


def pool_fwd(x, vec, pw, pvec):
    S = x.shape[0]
    tm = min(256, S)
    G = D // 4

    def body(xp_ref, x_ref, xn_ref, vec_ref, pw_ref, pv_ref, xo_ref, y_ref, z_ref):
        i = pl.program_id(0)
        xa = jnp.concatenate([xp_ref[...], x_ref[...], xn_ref[...]], axis=0)
        t = i * tm - HALO + lax.broadcasted_iota(jnp.int32, (tm + 2 * HALO, 1), 0)
        h, _, _ = _prenorm(xa, vec_ref)
        h = jnp.where((t >= 0) & (t < S), h, 0.0)
        tmain = t[HALO:HALO + tm]
        for g in range(4):
            hg = h[:, g * G:(g + 1) * G]
            pooled = _window_sum(hg, g, True)[HALO:HALO + tm] / _pool_count(tmain, POOL_WINDOWS[g], S)
            z = (pooled - hg[HALO:HALO + tm]).astype(BF16)
            z_ref[:, g * G:(g + 1) * G] = z
            y_ref[:, g * G:(g + 1) * G] = _dot(z, pw_ref[g]) + pv_ref[0:1, g * G:(g + 1) * G]
        u = y_ref[...] * pv_ref[1:2, :]
        uhat, _ = _rms(u)
        xo_ref[...] = x_ref[...] + (1.0 + vec_ref[4:5, :]) * (uhat * vec_ref[1:2, :])

    row = lambda i: (i, 0)
    full = lambda i: (0, 0)
    return pl.pallas_call(
        body, name="pool_fwd", grid=(S // tm,),
        in_specs=_halo_specs(tm, S) + [pl.BlockSpec((8, D), full), pl.BlockSpec((4, G, G), lambda i: (0, 0, 0)),
                                       pl.BlockSpec((8, D), full)],
        out_specs=[pl.BlockSpec((tm, D), row)] * 3,
        out_shape=[jax.ShapeDtypeStruct((S, D), F32), jax.ShapeDtypeStruct((S, D), F32),
                   jax.ShapeDtypeStruct((S, D), BF16)],
        compiler_params=_params("parallel"),
    )(x, x, x, vec, pw, pvec)


def pool_bwd(dout, x, y, z, vec, pw, pvec):
    S = x.shape[0]
    tm = min(256, S)
    G = D // 4
    R = G // N_CHIP

    def body(dop_ref, do_ref, don_ref, yp_ref, y_ref, yn_ref, x_ref, z_ref, vec_ref, pw_ref, pv_ref,
             dx_ref, vg_ref, pg_ref, dw_ref, dh_ref):
        i = pl.program_id(0)

        @pl.when(i == 0)
        def _():
            vg_ref[...] = jnp.zeros_like(vg_ref)
            pg_ref[...] = jnp.zeros_like(pg_ref)
            dw_ref[...] = jnp.zeros_like(dw_ref)

        doa = jnp.concatenate([dop_ref[...], do_ref[...], don_ref[...]], axis=0)
        ya = jnp.concatenate([yp_ref[...], y_ref[...], yn_ref[...]], axis=0)
        t = i * tm - HALO + lax.broadcasted_iota(jnp.int32, (tm + 2 * HALO, 1), 0)
        inside = (t >= 0) & (t < S)
        main = (t >= i * tm) & (t < (i + 1) * tm)
        du, dgate_rows, dgpost_rows = _postnorm_bwd(doa, ya * pv_ref[1:2, :], vec_ref, 1.0)
        du = jnp.where(inside, du, 0.0)
        vg_ref[2:3, :] += jnp.sum(jnp.where(main, dgate_rows, 0.0), axis=0, keepdims=True)
        vg_ref[4:5, :] += jnp.sum(jnp.where(main, dgpost_rows, 0.0), axis=0, keepdims=True)
        dy = du * pv_ref[1:2, :]
        pg_ref[0:1, :] += jnp.sum(jnp.where(main, dy, 0.0), axis=0, keepdims=True)
        pg_ref[1:2, :] += jnp.sum(jnp.where(main, du * ya, 0.0), axis=0, keepdims=True)
        for g in range(4):
            dyg = dy[:, g * G:(g + 1) * G].astype(BF16)
            dz = _dot(dyg, pw_ref[g], NT)
            e = dz / _pool_count(t, POOL_WINDOWS[g], S)
            dh_ref[:, g * G:(g + 1) * G] = (_window_sum(e, g, False) - dz)[HALO:HALO + tm]
            dwg = _dot(z_ref[:, g * G:(g + 1) * G], dyg[HALO:HALO + tm], TN)
            for q in range(N_CHIP):
                dw_ref[q, g] += dwg[q * R:(q + 1) * R, :]
        dx_ref[...] = do_ref[...] + _prenorm_bwd(dh_ref[...], x_ref[...], vec_ref, vg_ref)

    row = lambda i: (i, 0)
    full = lambda i: (0, 0)
    halo = _halo_specs(tm, S)
    return pl.pallas_call(
        body, name="pool_bwd", grid=(S // tm,),
        in_specs=halo + halo + [pl.BlockSpec((tm, D), row), pl.BlockSpec((tm, D), row), pl.BlockSpec((8, D), full),
                                pl.BlockSpec((4, G, G), lambda i: (0, 0, 0)), pl.BlockSpec((8, D), full)],
        out_specs=[pl.BlockSpec((tm, D), row), pl.BlockSpec((8, D), full), pl.BlockSpec((8, D), full),
                   pl.BlockSpec((N_CHIP, 4, R, G), lambda i: (0, 0, 0, 0))],
        out_shape=[jax.ShapeDtypeStruct((S, D), F32), jax.ShapeDtypeStruct((8, D), F32),
                   jax.ShapeDtypeStruct((8, D), F32), jax.ShapeDtypeStruct((N_CHIP, 4, R, G), F32)],
        scratch_shapes=[pltpu.VMEM((tm, D), F32)],
        compiler_params=_params("arbitrary"),
    )(dout, dout, dout, y, y, y, x, z, vec, pw, pvec)


N_PAIR = N_HEADS // 2
SLOTS = 128 // ROPE
ROPE_ALL = N_HEADS * ROPE
NOPE_ALL = N_HEADS * NOPE
LAT_ALL = N_HEADS * KVL
DLAT = QL + KVL + 2 * 128
DQ_ALL = NOPE_ALL + 2 * ROPE_ALL


def _w3(shape):
    return pl.BlockSpec(shape, lambda i: (0,) * len(shape))


def _slot_mask(hd, rows):
    lane = lax.broadcasted_iota(jnp.int32, (rows, 128), 1)
    return (lane // ROPE) == (hd % SLOTS)


MLA_WEIGHTS = ("wq", "wkv", "wkr4", "wkrs4", "qn", "kvn", "wn", "wr", "wrs", "bduk")


def _mla_weight_specs():
    return [_w3((D, QL)), _w3((D, KVL)), _w3((D, 128)), _w3((D, 128)), _w3((1, QL)), _w3((1, KVL)),
            _w3((QL, NOPE_ALL)), _w3((QL, ROPE_ALL)), _w3((QL, ROPE_ALL)), _w3((N_PAIR, 2 * NOPE, 2 * KVL))]


def mla_pre(x, vec, mw, tabs):
    S = x.shape[0]
    tm = min(256, S)

    def body(x_ref, vec_ref, cos_ref, sin_ref, wq_ref, wkv_ref, wkr_ref, wkrs_ref, qn_ref, kvn_ref,
             wn_ref, wr_ref, wrs_ref, bduk_ref,
             h_ref, cq_ref, ckv_ref, cqn_ref, qnope_ref, qcat_ref, kcat_ref, vcat_ref):
        h, _, _ = _prenorm(x_ref[...], vec_ref)
        hb = h.astype(BF16)
        h_ref[...] = hb
        cq_raw = _dot(hb, wq_ref[...])
        ckv_raw = _dot(hb, wkv_ref[...])
        cq_ref[...] = cq_raw
        ckv_ref[...] = ckv_raw
        cos, sin = cos_ref[...], sin_ref[...]
        ckv = (_rms(ckv_raw)[0] * kvn_ref[...]).astype(BF16)
        kcat_ref[:, 0:KVL] = ckv
        kcat_ref[:, KVL:] = (_dot(hb, wkr_ref[...]) * cos + _dot(hb, wkrs_ref[...]) * sin).astype(BF16)
        vcat_ref[:, 0:KVL] = ckv
        ones = lax.broadcasted_iota(jnp.int32, (tm, QPAD - KVL), 1) == 0
        vcat_ref[:, KVL:] = jnp.where(ones, 1.0, 0.0).astype(BF16)
        cqb = (_rms(cq_raw)[0] * qn_ref[...]).astype(BF16)
        cqn_ref[...] = cqb
        qn = _dot(cqb, wn_ref[...]).astype(BF16)
        qnope_ref[...] = qn
        cos4, sin4 = jnp.tile(cos, (1, SLOTS)), jnp.tile(sin, (1, SLOTS))
        qr = ((_dot(cqb, wr_ref[...]) * cos4 + _dot(cqb, wrs_ref[...]) * sin4) * ATTN_SCALE).astype(BF16)
        for j in range(N_PAIR):
            ql = (_dot(qn[:, 128 * j:128 * (j + 1)], bduk_ref[j]) * ATTN_SCALE).astype(BF16)
            for hd in (2 * j, 2 * j + 1):
                qcat_ref[hd, :, 0:KVL] = ql[:, KVL * (hd - 2 * j):KVL * (hd - 2 * j + 1)]
                group = qr[:, 128 * (hd // SLOTS):128 * (hd // SLOTS + 1)]
                qcat_ref[hd, :, KVL:] = jnp.where(_slot_mask(hd, tm), group, jnp.zeros_like(group))

    row = lambda i: (i, 0)
    hrow = lambda i: (0, i, 0)
    return pl.pallas_call(
        body, name="mla_pre", grid=(S // tm,),
        in_specs=[pl.BlockSpec((tm, D), row), _w3((8, D)), pl.BlockSpec((tm, 128), row), pl.BlockSpec((tm, 128), row)]
        + _mla_weight_specs(),
        out_specs=[pl.BlockSpec((tm, D), row), pl.BlockSpec((tm, QL), row), pl.BlockSpec((tm, KVL), row),
                   pl.BlockSpec((tm, QL), row), pl.BlockSpec((tm, NOPE_ALL), row),
                   pl.BlockSpec((N_HEADS, tm, QPAD), hrow), pl.BlockSpec((tm, QPAD), row),
                   pl.BlockSpec((tm, QPAD), row)],
        out_shape=[jax.ShapeDtypeStruct((S, D), BF16), jax.ShapeDtypeStruct((S, QL), F32),
                   jax.ShapeDtypeStruct((S, KVL), F32), jax.ShapeDtypeStruct((S, QL), BF16),
                   jax.ShapeDtypeStruct((S, NOPE_ALL), BF16), jax.ShapeDtypeStruct((N_HEADS, S, QPAD), BF16),
                   jax.ShapeDtypeStruct((S, QPAD), BF16), jax.ShapeDtypeStruct((S, QPAD), BF16)],
        compiler_params=_params("parallel"),
    )(x, vec, tabs[0], tabs[1], *[mw[k] for k in MLA_WEIGHTS])


def attn_fwd(qcat, kcat, vcat):
    S = kcat.shape[0]
    tq = min(ATTN_TQ, S)
    kc = min(ATTN_KC, S)

    def body(q_ref, k_ref, v_ref, o_ref, lse_ref):
        q = q_ref[...]
        m = jnp.full((tq, 1), -jnp.inf, F32)
        ov = jnp.zeros((tq, QPAD), F32)
        for c in range(S // kc):
            s = _dot(q, k_ref[c * kc:(c + 1) * kc, :], NT)
            m_new = jnp.maximum(m, jnp.max(s, axis=-1, keepdims=True))
            p = jnp.exp(s - m_new).astype(BF16)
            ov = ov * jnp.exp(m - m_new) + _dot(p, v_ref[c * kc:(c + 1) * kc, :])
            m = m_new
        l = ov[:, KVL:KVL + 1]
        o_ref[...] = (ov[:, 0:KVL] * (1.0 / l)).astype(BF16)
        lse_ref[...] = _as_row(m + jnp.log(l))

    return pl.pallas_call(
        body, name="attn_fwd", grid=(N_HEADS, S // tq),
        in_specs=[pl.BlockSpec((None, tq, QPAD), lambda h, i: (h, i, 0)),
                  pl.BlockSpec((S, QPAD), lambda h, i: (0, 0)),
                  pl.BlockSpec((S, QPAD), lambda h, i: (0, 0))],
        out_specs=[pl.BlockSpec((tq, KVL), lambda h, i: (i, h)),
                   pl.BlockSpec((None, 1, tq), lambda h, i: (h, 0, i))],
        out_shape=[jax.ShapeDtypeStruct((S, LAT_ALL), BF16), jax.ShapeDtypeStruct((N_HEADS, 1, S), F32)],
        compiler_params=_params("parallel", "parallel"),
    )(qcat, kcat, vcat)


def mla_post(olat, x, vec, bduv, wo):
    S = x.shape[0]
    tm = min(256, S)

    def body(o_ref, x_ref, vec_ref, bduv_ref, wo_ref, xo_ref, u_ref, ocat_ref):
        for j in range(N_PAIR):
            oc = _dot(o_ref[:, 2 * KVL * j:2 * KVL * (j + 1)], bduv_ref[j])
            ocat_ref[:, 2 * VH * j:2 * VH * (j + 1)] = oc.astype(BF16)
        u = _dot(ocat_ref[...], wo_ref[...])
        u_ref[...] = u
        uhat, _ = _rms(u)
        xo_ref[...] = x_ref[...] + (1.0 + vec_ref[4:5, :]) * (uhat * vec_ref[1:2, :])

    row = lambda i: (i, 0)
    return pl.pallas_call(
        body, name="mla_post", grid=(S // tm,),
        in_specs=[pl.BlockSpec((tm, LAT_ALL), row), pl.BlockSpec((tm, D), row), _w3((8, D)),
                  _w3((N_PAIR, 2 * KVL, 2 * VH)), _w3((D, D))],
        out_specs=[pl.BlockSpec((tm, D), row), pl.BlockSpec((tm, D), row), pl.BlockSpec((tm, D), row)],
        out_shape=[jax.ShapeDtypeStruct((S, D), F32), jax.ShapeDtypeStruct((S, D), F32),
                   jax.ShapeDtypeStruct((S, D), BF16)],
        compiler_params=_params("parallel"),
    )(olat, x, vec, bduv, wo)


def mla_post_bwd(dout, u, olat, vec, bduv, wo):
    S = u.shape[0]
    tm = min(256, S)

    def body(do_ref, u_ref, o_ref, vec_ref, bduv_ref, wo_ref, du_ref, docat_ref, dolat_ref, delta_ref, vg_ref):
        @pl.when(pl.program_id(0) == 0)
        def _():
            vg_ref[...] = jnp.zeros_like(vg_ref)

        du, dgate_rows, dgpost_rows = _postnorm_bwd(do_ref[...], u_ref[...], vec_ref, 1.0)
        vg_ref[2:3, :] += jnp.sum(dgate_rows, axis=0, keepdims=True)
        vg_ref[4:5, :] += jnp.sum(dgpost_rows, axis=0, keepdims=True)
        dub = du.astype(BF16)
        du_ref[...] = dub
        docat_ref[...] = _dot(dub, wo_ref[...], NT).astype(BF16)
        for j in range(N_PAIR):
            dol = _dot(docat_ref[:, 2 * VH * j:2 * VH * (j + 1)], bduv_ref[j], NT).astype(BF16)
            dolat_ref[:, 2 * KVL * j:2 * KVL * (j + 1)] = dol
            prod = dol.astype(F32) * o_ref[:, 2 * KVL * j:2 * KVL * (j + 1)].astype(F32)
            delta_ref[2 * j] = _as_row(jnp.sum(prod[:, 0:KVL], axis=-1, keepdims=True))
            delta_ref[2 * j + 1] = _as_row(jnp.sum(prod[:, KVL:], axis=-1, keepdims=True))

    row = lambda i: (i, 0)
    hrow = lambda i: (0, i, 0)
    return pl.pallas_call(
        body, name="mla_post_bwd", grid=(S // tm,),
        in_specs=[pl.BlockSpec((tm, D), row), pl.BlockSpec((tm, D), row), pl.BlockSpec((tm, LAT_ALL), row),
                  _w3((8, D)), _w3((N_PAIR, 2 * KVL, 2 * VH)), _w3((D, D))],
        out_specs=[pl.BlockSpec((tm, D), row), pl.BlockSpec((tm, D), row),
                   pl.BlockSpec((tm, LAT_ALL), row), pl.BlockSpec((N_HEADS, 1, tm), lambda i: (0, 0, i)), _w3((8, D))],
        out_shape=[jax.ShapeDtypeStruct((S, D), BF16), jax.ShapeDtypeStruct((S, D), BF16),
                   jax.ShapeDtypeStruct((S, LAT_ALL), BF16), jax.ShapeDtypeStruct((N_HEADS, 1, S), F32),
                   jax.ShapeDtypeStruct((8, D), F32)],
        compiler_params=_params("arbitrary"),
    )(dout, u, olat, vec, bduv, wo)


def attn_bwd(qcat, kcat, kcat_t, dolat, lse_row, delta_row):
    S = kcat.shape[0]
    tq = min(ATTN_TQ, S)
    kc = min(ATTN_KC, S)

    def body(q_ref, k_ref, kt_ref, do_ref, lse_ref, dl_ref, dq_ref, dk_ref, dv_ref):
        @pl.when((pl.program_id(0) == 0) & (pl.program_id(1) == 0))
        def _():
            dk_ref[...] = jnp.zeros_like(dk_ref)
            dv_ref[...] = jnp.zeros_like(dv_ref)

        q, do = q_ref[...], do_ref[...]
        lse, dl = lse_ref[...], dl_ref[...]
        dqt = jnp.zeros((QPAD, tq), F32)
        for c in range(S // kc):
            rows = slice(c * kc, (c + 1) * kc)
            st = _dot(k_ref[rows, :], q, NT)
            pt = jnp.exp(st - lse)
            dpt = _dot(k_ref[rows, 0:KVL], do, NT)
            dst = (pt * (dpt - dl)).astype(BF16)
            dv_ref[rows, :] += _dot(pt.astype(BF16), do)
            dk_ref[rows, :] += _dot(dst, q)
            dqt = dqt + _dot(kt_ref[:, rows], dst)
        dq_ref[...] = dqt.T

    return pl.pallas_call(
        body, name="attn_bwd", grid=(N_HEADS, S // tq),
        in_specs=[pl.BlockSpec((None, tq, QPAD), lambda h, i: (h, i, 0)),
                  pl.BlockSpec((S, QPAD), lambda h, i: (0, 0)),
                  pl.BlockSpec((QPAD, S), lambda h, i: (0, 0)),
                  pl.BlockSpec((tq, KVL), lambda h, i: (i, h)),
                  pl.BlockSpec((None, 1, tq), lambda h, i: (h, 0, i)),
                  pl.BlockSpec((None, 1, tq), lambda h, i: (h, 0, i))],
        out_specs=[pl.BlockSpec((None, tq, QPAD), lambda h, i: (h, i, 0)),
                   pl.BlockSpec((S, QPAD), lambda h, i: (0, 0)),
                   pl.BlockSpec((S, KVL), lambda h, i: (0, 0))],
        out_shape=[jax.ShapeDtypeStruct((N_HEADS, S, QPAD), F32), jax.ShapeDtypeStruct((S, QPAD), F32),
                   jax.ShapeDtypeStruct((S, KVL), F32)],
        compiler_params=_params("arbitrary", "arbitrary"),
    )(qcat, kcat, kcat_t, dolat, lse_row, delta_row)


def mla_pre_bwd(dout, dq, dk, dv, x, cq_raw, ckv_raw, vec, mw, tabs):
    S = x.shape[0]
    tm = min(256, S)

    def body(do_ref, dq_ref, dk_ref, dv_ref, x_ref, cq_ref, ckv_ref, vec_ref, cos_ref, sin_ref,
             wq_ref, wkv_ref, wkr_ref, wkrs_ref, qn_ref, kvn_ref, wn_ref, wr_ref, wrs_ref, bduk_ref,
             dx_ref, dlat_ref, dql_ref, dqcat_ref, vg_ref, ng_ref):
        @pl.when(pl.program_id(0) == 0)
        def _():
            vg_ref[...] = jnp.zeros_like(vg_ref)
            ng_ref[...] = jnp.zeros_like(ng_ref)

        cos, sin = cos_ref[...], sin_ref[...]
        for j in range(N_PAIR):
            dql = jnp.concatenate([dq_ref[2 * j, :, 0:KVL], dq_ref[2 * j + 1, :, 0:KVL]], axis=1) * ATTN_SCALE
            dql = dql.astype(BF16)
            dql_ref[:, 2 * KVL * j:2 * KVL * (j + 1)] = dql
            dqcat_ref[:, 2 * NOPE * j:2 * NOPE * (j + 1)] = _dot(dql, bduk_ref[j], NT).astype(BF16)
        groups = []
        for grp in range(N_HEADS // SLOTS):
            acc = jnp.zeros((tm, 128), F32)
            for hd in range(SLOTS * grp, SLOTS * (grp + 1)):
                acc = acc + jnp.where(_slot_mask(hd, tm), dq_ref[hd, :, KVL:], 0.0)
            groups.append(acc)
        dqr = jnp.concatenate(groups, axis=1) * ATTN_SCALE
        qa = (dqr * jnp.tile(cos, (1, SLOTS))).astype(BF16)
        qb = (dqr * jnp.tile(sin, (1, SLOTS))).astype(BF16)
        dqcat_ref[:, NOPE_ALL:NOPE_ALL + ROPE_ALL] = qa
        dqcat_ref[:, NOPE_ALL + ROPE_ALL:] = qb
        dcq = _dot(dqcat_ref[:, 0:NOPE_ALL], wn_ref[...], NT) + _dot(qa, wr_ref[...], NT) + _dot(qb, wrs_ref[...], NT)
        cqh, rq = _rms(cq_ref[...])
        ng_ref[0:1, :] += jnp.sum(dcq * cqh, axis=0, keepdims=True)
        dcq_raw = _rms_bwd(cqh, rq, dcq * qn_ref[...]).astype(BF16)
        dckv = dk_ref[:, 0:KVL] + dv_ref[...]
        ckvh, rk = _rms(ckv_ref[...])
        ng_ref[1:2, 0:KVL] += jnp.sum(dckv * ckvh, axis=0, keepdims=True)
        dckv_raw = _rms_bwd(ckvh, rk, dckv * kvn_ref[...]).astype(BF16)
        dkr = dk_ref[:, KVL:]
        ka = (dkr * cos).astype(BF16)
        kb = (dkr * sin).astype(BF16)
        dlat_ref[:, 0:QL] = dcq_raw
        dlat_ref[:, QL:QL + KVL] = dckv_raw
        dlat_ref[:, QL + KVL:QL + KVL + 128] = ka
        dlat_ref[:, QL + KVL + 128:] = kb
        dh = (_dot(dcq_raw, wq_ref[...], NT) + _dot(dckv_raw, wkv_ref[...], NT)
              + _dot(ka, wkr_ref[...], NT) + _dot(kb, wkrs_ref[...], NT))
        dx_ref[...] = do_ref[...] + _prenorm_bwd(dh, x_ref[...], vec_ref, vg_ref)

    row = lambda i: (i, 0)
    hrow = lambda i: (0, i, 0)
    return pl.pallas_call(
        body, name="mla_pre_bwd", grid=(S // tm,),
        in_specs=[pl.BlockSpec((tm, D), row), pl.BlockSpec((N_HEADS, tm, QPAD), hrow), pl.BlockSpec((tm, QPAD), row),
                  pl.BlockSpec((tm, KVL), row), pl.BlockSpec((tm, D), row), pl.BlockSpec((tm, QL), row),
                  pl.BlockSpec((tm, KVL), row), _w3((8, D)), pl.BlockSpec((tm, 128), row), pl.BlockSpec((tm, 128), row)]
        + _mla_weight_specs(),
        out_specs=[pl.BlockSpec((tm, D), row), pl.BlockSpec((tm, DLAT), row), pl.BlockSpec((tm, LAT_ALL), row),
                   pl.BlockSpec((tm, DQ_ALL), row), _w3((8, D)), _w3((8, QL))],
        out_shape=[jax.ShapeDtypeStruct((S, D), F32), jax.ShapeDtypeStruct((S, DLAT), BF16),
                   jax.ShapeDtypeStruct((S, LAT_ALL), BF16), jax.ShapeDtypeStruct((S, DQ_ALL), BF16),
                   jax.ShapeDtypeStruct((8, D), F32), jax.ShapeDtypeStruct((8, QL), F32)],
        compiler_params=_params("arbitrary"),
    )(dout, dq, dk, dv, x, cq_raw, ckv_raw, vec, tabs[0], tabs[1], *[mw[k] for k in MLA_WEIGHTS])


def mla_dw(h, dlat, cqn, dqcat, dql, qnope, olat, docat, ocat, du):
    S = h.shape[0]
    tk = min(DW_TK, S)
    nk = S // tk
    flat = lambda w: pl.BlockSpec((tk, w), lambda k: (k, 0))
    cols = lambda w: pl.BlockSpec((tk, w), lambda n, k: (k, n))
    pair_o = pl.BlockSpec((None, 2 * KVL, 128), lambda n, k: (n, 0, 0))
    g = {}
    g["in"] = dw_matmul("mla_dw_in", h, dlat, flat(D), flat(DLAT), (D, DLAT),
                        pl.BlockSpec((D, DLAT), lambda k: (0, 0)), (nk,))
    g["q"] = dw_matmul("mla_dw_q", cqn, dqcat, flat(QL), flat(DQ_ALL), (QL, DQ_ALL),
                       pl.BlockSpec((QL, DQ_ALL), lambda k: (0, 0)), (nk,))
    g["uk"] = dw_matmul("mla_dw_uk", dql, qnope, cols(2 * KVL), cols(2 * NOPE), (N_PAIR, 2 * KVL, 2 * NOPE), pair_o,
                        (N_PAIR, nk))
    g["uv"] = dw_matmul("mla_dw_uv", olat, docat, cols(2 * KVL), cols(2 * VH), (N_PAIR, 2 * KVL, 2 * VH), pair_o,
                        (N_PAIR, nk))
    g["o"] = dw_matmul("mla_dw_o", ocat, du, cols(256), pl.BlockSpec((tk, D), lambda n, k: (k, 0)), (D, D),
                       pl.BlockSpec((256, D), lambda n, k: (n, 0)), (D // 256, nk))
    return g


def loss_head(y, target):
    S = y.shape[0]
    tm = min(512, S)

    def body(y_ref, t_ref, loss_ref, dy_ref):
        @pl.when(pl.program_id(0) == 0)
        def _():
            loss_ref[...] = jnp.zeros_like(loss_ref)

        err = y_ref[...] - t_ref[...]
        dy_ref[...] = err * (1.0 / D)
        loss_ref[...] += 0.5 * jnp.sum(jnp.mean(err * err, axis=-1, keepdims=True), axis=0, keepdims=True)

    row = lambda i: (i, 0)
    return pl.pallas_call(
        body, name="loss_head", grid=(S // tm,),
        in_specs=[pl.BlockSpec((tm, D), row), pl.BlockSpec((tm, D), row)],
        out_specs=[pl.BlockSpec((1, 1), lambda i: (0, 0)), pl.BlockSpec((tm, D), row)],
        out_shape=[jax.ShapeDtypeStruct((1, 1), F32), jax.ShapeDtypeStruct((S, D), F32)],
        compiler_params=_params("arbitrary"),
    )(y, target)


MOD_COLS = 9 * D // N_CHIP


def mod_fwd(c_pad, ada_w, ada_b_loc):
    tn = MOD_COLS // 3

    def body(c_ref, w_ref, b_ref, o_ref):
        c = c_ref[...]
        sc = (c * jax.nn.sigmoid(c)).astype(BF16)
        o_ref[...] = _dot(sc, w_ref[...].astype(BF16)) + b_ref[...]

    return pl.pallas_call(
        body, name="mod_fwd", grid=(2, 3),
        in_specs=[pl.BlockSpec((16, D), lambda i, n: (0, 0)), pl.BlockSpec((None, D, tn), lambda i, n: (i, 0, n)),
                  pl.BlockSpec((None, 1, tn), lambda i, n: (i, 0, n))],
        out_specs=pl.BlockSpec((None, 16, tn), lambda i, n: (i, 0, n)),
        out_shape=jax.ShapeDtypeStruct((2, 16, MOD_COLS), F32),
        compiler_params=_params("parallel", "parallel"),
    )(c_pad, ada_w, ada_b_loc)


def _adamw_math(w, g, m, v):
    m = ADAM_B1 * m + (1.0 - ADAM_B1) * g
    v = ADAM_B2 * v + (1.0 - ADAM_B2) * (g * g)
    m_hat = m / (1.0 - ADAM_B1 ** ADAM_STEP)
    v_hat = v / (1.0 - ADAM_B2 ** ADAM_STEP)
    delta = -ADAM_LR * (m_hat / (jnp.sqrt(v_hat) + ADAM_EPS) + ADAM_WD * w)
    return delta, m, v


def adamw(name, w, g, m, v, part=None, prev=None):
    shape = w.shape
    cols = shape[-1]
    rows = w.size // cols
    per_entry = rows // shape[0] if part is not None else rows
    tr = per_entry
    budget_rows = (2 << 20) // (cols * 4)
    for cand in range(min(per_entry, budget_rows) // 8 * 8, 0, -8):
        if per_entry % cand == 0:
            tr = cand
            break
    first, count = part if part is not None else (0, 1)
    tiles = per_entry // tr

    def body(w_ref, g_ref, m_ref, v_ref, *rest):
        d_ref, mo_ref, vo_ref = rest[-3:]
        d_ref[...], mo_ref[...], vo_ref[...] = _adamw_math(w_ref[...], g_ref[...], m_ref[...], v_ref[...])

    spec = pl.BlockSpec((tr, cols), lambda i: (i + first * tiles, 0))
    operands = [a.reshape(rows, cols) for a in (w, g, m, v)]
    aliases = {}
    if prev is not None:
        operands += [p.reshape(rows, cols) for p in prev]
        aliases = {4: 0, 5: 1, 6: 2}
    outs = pl.pallas_call(
        body, name=name, grid=(count * tiles,), in_specs=[spec] * 4 + [_ANY] * (len(operands) - 4),
        out_specs=[spec] * 3, out_shape=[jax.ShapeDtypeStruct((rows, cols), F32)] * 3,
        input_output_aliases=aliases, compiler_params=_params("parallel"),
    )(*operands)
    return [o.reshape(shape) for o in outs]


def adamw_ada(c_pad, dmod, w, m, v):
    tr = 256

    def body(c_ref, dm_ref, w_ref, m_ref, v_ref, g_ref, d_ref, mo_ref, vo_ref):
        c = c_ref[...]
        sc = (c * jax.nn.sigmoid(c)).astype(BF16)
        g = _dot(sc, dm_ref[...].astype(BF16), TN)
        g_ref[...] = g
        d_ref[...], mo_ref[...], vo_ref[...] = _adamw_math(w_ref[...], g, m_ref[...], v_ref[...])

    wspec = pl.BlockSpec((None, tr, MOD_COLS), lambda i, r: (i, r, 0))
    return pl.pallas_call(
        body, name="adamw_ada", grid=(2, D // tr),
        in_specs=[pl.BlockSpec((16, tr), lambda i, r: (0, r)),
                  pl.BlockSpec((None, 16, MOD_COLS), lambda i, r: (i, 0, 0)), wspec, wspec, wspec],
        out_specs=[wspec] * 4,
        out_shape=[jax.ShapeDtypeStruct((2, D, MOD_COLS), F32)] * 4,
        compiler_params=_params("parallel", "parallel"),
    )(c_pad, dmod, w, m, v)


def sum_devices(name, a):
    _, R, C = a.shape
    tr = R
    for cand in (64, 32, 16, 8):
        if R % cand == 0:
            tr = cand
            break

    def body(a_ref, o_ref):
        acc = a_ref[0]
        for dev in range(1, N_DEV):
            acc = acc + a_ref[dev]
        o_ref[...] = acc

    return pl.pallas_call(
        body, name=name, grid=(R // tr,),
        in_specs=[pl.BlockSpec((N_DEV, tr, C), lambda i: (0, i, 0))],
        out_specs=pl.BlockSpec((tr, C), lambda i: (i, 0)),
        out_shape=jax.ShapeDtypeStruct((R, C), F32),
        compiler_params=_params("parallel"),
    )(a)


def _place():
    return lax.axis_index("x"), lax.axis_index("y"), lax.axis_index("c")


def _other_chips(x, y):
    return [(1 - x, y), (x, 1 - y), (1 - x, 1 - y)]


def gather_devices(name, a):
    m_per, n = a.shape

    def body(x_ref, out_ref, send_sems, recv_sems, local_sem):
        x, y, c = _place()
        me, sibling = (x, y, c), (x, y, 1 - c)
        chips = _other_chips(x, y)

        def rows(px, py, pc):
            return out_ref.at[pl.ds((4 * px + 2 * py + pc) * m_per, m_per), :]

        def copy(k, block, to, src=None):
            return pltpu.make_async_remote_copy(
                src_ref=rows(*block) if src is None else src, dst_ref=rows(*block),
                send_sem=send_sems.at[k], recv_sem=recv_sems.at[k], device_id=to, device_id_type=MESH)

        mine = pltpu.make_async_copy(x_ref, rows(*me), local_sem)
        mine.start()
        first = [copy(0, me, sibling, src=x_ref)]
        first += [copy(1 + j, me, (*chip, c), src=x_ref) for j, chip in enumerate(chips)]
        for cp in first:
            cp.start()
        passed = [copy(4 + j, (*chip, c), sibling) for j, chip in enumerate(chips)]
        for j, chip in enumerate(chips):
            copy(1 + j, (*chip, c), me).wait_recv()
            passed[j].start()
        copy(0, sibling, me).wait_recv()
        for j, chip in enumerate(chips):
            copy(4 + j, (*chip, 1 - c), me).wait_recv()
        for cp in first + passed:
            cp.wait_send()
        mine.wait()

    out = pl.pallas_call(
        body, name=name,
        out_shape=jax.ShapeDtypeStruct((N_DEV * m_per, n), a.dtype),
        in_specs=[pl.BlockSpec(memory_space=pltpu.VMEM)],
        out_specs=pl.BlockSpec(memory_space=pltpu.VMEM),
        scratch_shapes=[pltpu.SemaphoreType.DMA((7,)), pltpu.SemaphoreType.DMA((7,)), pltpu.SemaphoreType.DMA],
        compiler_params=pltpu.CompilerParams(vmem_limit_bytes=VMEM_LIMIT),
    )(a)
    return out.reshape(N_DEV, m_per, n)


_ANY = pl.BlockSpec(memory_space=pl.ANY)


def _hbm_ref(a):
    return jax.new_ref(a, memory_space=pltpu.MemorySpace.HBM)


def _hbm_empty(shape, dtype):
    return jax.empty_ref(jax.ShapeDtypeStruct(shape, dtype), memory_space=pltpu.MemorySpace.HBM)


ID_PAIR, ID_CHIPS, ID_SHARE, ID_UKV = 8, 9, 10, 11


def _sequencer(name, collective_id, n_sem, peers_of, program):
    sems = pltpu.SemaphoreType.DMA((n_sem,))

    @pl.kernel(mesh=plsc.ScalarSubcoreMesh(axis_name="seq", num_cores=1), name=name, scratch_types=[sems, sems],
               compiler_params=pltpu.CompilerParams(collective_id=collective_id))
    def launch(send_sem, recv_sem):
        x, y, c = _place()
        peers = peers_of(x, y, c)
        barrier = pltpu.get_barrier_semaphore()
        for peer in peers:
            pl.semaphore_signal(barrier, inc=1, device_id=peer, device_id_type=MESH)
        pl.semaphore_wait(barrier, len(peers))
        program(x, y, c, send_sem, recv_sem)

    launch()


def gather_weights(name, stage, arrays):
    n = len(arrays)
    refs = [_hbm_ref(a) for a in arrays]

    def program(x, y, c, send_sem, recv_sem):
        me = 2 * x + y
        chips = _other_chips(x, y)

        def ici(t, r, half):
            cx, cy = chips[r]
            mine = refs[t].at[me, half]
            return pltpu.make_async_remote_copy(
                src_ref=mine, dst_ref=mine, send_sem=send_sem.at[3 * t + r], recv_sem=recv_sem.at[3 * t + r],
                device_id=(cx, cy, c), device_id_type=MESH)

        def d2d(t, r, half):
            cx, cy = chips[r]
            there = refs[t].at[2 * cx + cy, half]
            k = 3 * n + 3 * t + r
            return pltpu.make_async_remote_copy(
                src_ref=there, dst_ref=there, send_sem=send_sem.at[k], recv_sem=recv_sem.at[k],
                device_id=(x, y, 1 - c), device_id_type=MESH)

        for t in range(n):
            for r in range(3):
                ici(t, r, c).start()
        for t in range(n):
            for r in range(3):
                ici(t, r, c).wait_recv()
                d2d(t, r, c).start()
        for t in range(n):
            for r in range(3):
                d2d(t, r, 1 - c).wait_recv()
        for t in range(n):
            for r in range(3):
                ici(t, r, c).wait_send()
                d2d(t, r, c).wait_send()

    _sequencer(name, stage, 6 * n, lambda x, y, c: [(x, y, 1 - c)] + [(cx, cy, c) for cx, cy in _other_chips(x, y)],
               program)
    return [r[...] for r in refs]


def cast_into_slots(name, chip, shards, after=None):
    steps = 2
    n = len(shards)

    def body(chip_ref, *refs):
        for src, dst in zip(refs[:n], refs[-n - 1:-1]):
            dst[...] = src[...].astype(BF16)
        refs[-1][...] = jnp.zeros_like(refs[-1])

    token_spec = pl.BlockSpec((8, 128), lambda h, i, chip_ref: (0, 0))

    def spec_in(a, prefix):
        R, C = a.shape[-2:]
        return pl.BlockSpec((None,) * (len(prefix) + 1) + (R // steps, C), lambda h, i, chip_ref: prefix + (h, i, 0))

    def spec_out(a):
        R, C = a.shape[-2:]
        return pl.BlockSpec((None, None, R // steps, C), lambda h, i, chip_ref: (chip_ref[0], h, i, 0))

    outs = pl.pallas_call(
        body, name=name,
        grid_spec=pltpu.PrefetchScalarGridSpec(
            num_scalar_prefetch=1, grid=(2, steps),
            in_specs=[spec_in(a, p) for a, p in shards] + ([token_spec] if after is not None else []),
            out_specs=[spec_out(a) for a, _ in shards] + [token_spec]),
        out_shape=[jax.ShapeDtypeStruct((N_CHIP, 2) + a.shape[-2:], BF16) for a, _ in shards]
        + [jax.ShapeDtypeStruct((8, 128), F32)],
        compiler_params=_params("arbitrary", "arbitrary"),
    )(chip, *[a for a, _ in shards], *([after] if after is not None else []))
    return outs[:-1], outs[-1]


def reduce_pair(name, grads):
    n = len(grads)
    src = [_hbm_ref(g) for g in grads]
    dst = [_hbm_empty((N_CHIP,) + g.shape[2:], g.dtype) for g in grads]

    def program(x, y, c, send_sem, recv_sem):
        cps = [pltpu.make_async_remote_copy(
            src_ref=src[t].at[:, 1 - c], dst_ref=dst[t], send_sem=send_sem.at[t], recv_sem=recv_sem.at[t],
            device_id=(x, y, 1 - c), device_id_type=MESH) for t in range(n)]
        for cp in cps:
            cp.start()
        for cp in cps:
            cp.wait()

    _sequencer(name, ID_PAIR, n, lambda x, y, c: [(x, y, 1 - c)], program)
    return [r[...] for r in src], [r[...] for r in dst]


def pair_add(name, core, g, got):
    _, _, R, C = g.shape

    def body(core_ref, g_ref, got_ref, o_ref, token_ref):
        o_ref[...] = (g_ref[...] + got_ref[...]).astype(BF16)
        token_ref[...] = jnp.zeros_like(token_ref)

    return pl.pallas_call(
        body, name=name,
        grid_spec=pltpu.PrefetchScalarGridSpec(
            num_scalar_prefetch=1, grid=(N_CHIP,),
            in_specs=[pl.BlockSpec((None, None, R, C), lambda q, core_ref: (q, core_ref[0], 0, 0)),
                      pl.BlockSpec((None, R, C), lambda q, core_ref: (q, 0, 0))],
            out_specs=[pl.BlockSpec((None, R, C), lambda q, core_ref: (q, 0, 0)),
                       pl.BlockSpec((8, 128), lambda q, core_ref: (0, 0))]),
        out_shape=[jax.ShapeDtypeStruct((N_CHIP, R, C), BF16), jax.ShapeDtypeStruct((8, 128), F32)],
        compiler_params=_params("arbitrary"),
    )(core, g, got)


def reduce_chips(name, sums):
    n = len(sums)
    src = [_hbm_ref(s) for s in sums]
    dst = [_hbm_empty((3,) + s.shape[1:], s.dtype) for s in sums]

    def program(x, y, c, send_sem, recv_sem):
        cps = []
        for t in range(n):
            for r, (cx, cy) in enumerate(_other_chips(x, y)):
                cps.append(pltpu.make_async_remote_copy(
                    src_ref=src[t].at[2 * cx + cy], dst_ref=dst[t].at[r],
                    send_sem=send_sem.at[3 * t + r], recv_sem=recv_sem.at[3 * t + r],
                    device_id=(cx, cy, c), device_id_type=MESH))
        for cp in cps:
            cp.start()
        for cp in cps:
            cp.wait()

    _sequencer(name, ID_CHIPS, 3 * n, lambda x, y, c: [(cx, cy, c) for cx, cy in _other_chips(x, y)], program)
    return [r[...] for r in src], [r[...] for r in dst]


def chip_add(name, place, s, got, k, n_slots, prev=None):
    _, R, C = s.shape

    def body(place_ref, s_ref, got_ref, *rest):
        o_ref, token_ref = rest[-2:]
        o_ref[...] = ((s_ref[...].astype(F32) + got_ref[0].astype(F32)) + got_ref[1].astype(F32)) + got_ref[2].astype(F32)
        token_ref[...] = jnp.zeros_like(token_ref)

    in_specs = [pl.BlockSpec((None, R, C), lambda i, place_ref: (place_ref[0], 0, 0)),
                pl.BlockSpec((3, R, C), lambda i, place_ref: (0, 0, 0))]
    args = [place, s, got]
    aliases = {}
    if prev is not None:
        in_specs.append(_ANY)
        args.append(prev)
        aliases = {3: 0}
    return pl.pallas_call(
        body, name=name,
        grid_spec=pltpu.PrefetchScalarGridSpec(
            num_scalar_prefetch=1, grid=(1,), in_specs=in_specs,
            out_specs=[pl.BlockSpec((None, None, R, C), lambda i, place_ref: (k, place_ref[1], 0, 0)),
                       pl.BlockSpec((8, 128), lambda i, place_ref: (0, 0))]),
        out_shape=[jax.ShapeDtypeStruct((n_slots, 2, R, C), F32), jax.ShapeDtypeStruct((8, 128), F32)],
        input_output_aliases=aliases,
        compiler_params=_params("arbitrary"),
    )(*args)


def share_halves(name, stacks, slots):
    n = len(stacks)
    dst = [_hbm_ref(s) for s in stacks]

    def program(x, y, c, send_sem, recv_sem):
        cps = [pltpu.make_async_remote_copy(
            src_ref=dst[t].at[slots[t], c], dst_ref=dst[t].at[slots[t], c],
            send_sem=send_sem.at[t], recv_sem=recv_sem.at[t],
            device_id=(x, y, 1 - c), device_id_type=MESH) for t in range(n)]
        for cp in cps:
            cp.start()
        for cp in cps:
            cp.wait()

    _sequencer(name, ID_SHARE, n, lambda x, y, c: [(x, y, 1 - c)], program)
    return [r[...] for r in dst]


def gather_blocks(name, slotted):
    out = _hbm_ref(slotted)

    def program(x, y, c, send_sem, recv_sem):
        sibling = (x, y, 1 - c)
        chips = _other_chips(x, y)

        def copy(k, px, py, pc, to):
            block = out.at[4 * px + 2 * py + pc]
            return pltpu.make_async_remote_copy(src_ref=block, dst_ref=block, send_sem=send_sem.at[k],
                                                recv_sem=recv_sem.at[k], device_id=to, device_id_type=MESH)

        first = [copy(0, x, y, c, sibling)] + [copy(1 + j, x, y, c, (cx, cy, c)) for j, (cx, cy) in enumerate(chips)]
        for cp in first:
            cp.start()
        passed = [copy(4 + j, cx, cy, c, sibling) for j, (cx, cy) in enumerate(chips)]
        for j, (cx, cy) in enumerate(chips):
            copy(1 + j, cx, cy, c, (x, y, c)).wait_recv()
            passed[j].start()
        copy(0, x, y, 1 - c, (x, y, c)).wait_recv()
        for j, (cx, cy) in enumerate(chips):
            copy(4 + j, cx, cy, 1 - c, (x, y, c)).wait_recv()
        for cp in first + passed:
            cp.wait_send()

    _sequencer(name, ID_UKV, 7, lambda x, y, c: [(x, y, 1 - c)] + [(cx, cy, c) for cx, cy in _other_chips(x, y)],
               program)
    return out[...]


def place_block(name, dev, a):
    M, N = a.shape
    tr = min(M, 64)

    def body(dev_ref, a_ref, o_ref):
        o_ref[...] = a_ref[...]

    return pl.pallas_call(
        body, name=name,
        grid_spec=pltpu.PrefetchScalarGridSpec(
            num_scalar_prefetch=1, grid=(M // tr,),
            in_specs=[pl.BlockSpec((tr, N), lambda i, dev_ref: (i, 0))],
            out_specs=pl.BlockSpec((None, tr, N), lambda i, dev_ref: (dev_ref[0], i, 0))),
        out_shape=jax.ShapeDtypeStruct((N_DEV, M, N), a.dtype),
        compiler_params=_params("parallel"),
    )(dev, a)


def _swap_rope(a):
    return jnp.concatenate([a[..., ROPE // 2:], a[..., :ROPE // 2]], axis=-1)


def _rope_tables(S):
    inv = 1.0 / (ROPE_THETA ** (jnp.arange(0, ROPE, 2, dtype=F32) / ROPE))
    ang = jnp.arange(S, dtype=F32)[:, None] * inv[None, :]
    cos, sin = jnp.cos(ang), jnp.sin(ang)
    return (jnp.tile(jnp.concatenate([cos, cos], axis=1), (1, SLOTS)),
            jnp.tile(jnp.concatenate([-sin, sin], axis=1), (1, SLOTS)))


def _vec(norm_g, mod, i, k):
    rows = [norm_g[i, 2 * k], norm_g[i, 2 * k + 1], mod[i, 3 * k], mod[i, 3 * k + 1], mod[i, 3 * k + 2]]
    return jnp.concatenate([jnp.stack(rows), jnp.zeros((3, D), F32)], axis=0)


def _unpack_weights(full, w_uk, w_uv, q_norm, kv_norm):
    G = D // 4
    ffn_in = [[full[2 * i + k].reshape(N_CHIP, D, FSH) for k in range(2)] for i in range(2)]
    ffn_out = [[full[4 + 2 * i + k].reshape(2, FSH, D) for k in range(2)] for i in range(2)]
    pw = full[8].reshape(N_CHIP, 4, G // N_CHIP, G).transpose(1, 0, 2, 3).reshape(4, G, G)
    w_in = full[9].reshape(D, QL + KVL + ROPE)
    w_uq = full[10].reshape(QL, N_HEADS, NOPE + ROPE)
    wkr = w_in[:, QL + KVL:]
    wr = w_uq[:, :, NOPE:]
    eye2 = jnp.eye(2, dtype=BF16)
    uk_t = jnp.transpose(w_uk, (1, 2, 0)).reshape(N_PAIR, 2, NOPE, KVL)
    bduk = jnp.einsum("janc,ab->janbc", uk_t, eye2).reshape(N_PAIR, 2 * NOPE, 2 * KVL)
    uv = jnp.transpose(w_uv, (1, 0, 2)).reshape(N_PAIR, 2, KVL, VH)
    bduv = jnp.einsum("jacn,ab->jacbn", uv, eye2).reshape(N_PAIR, 2 * KVL, 2 * VH)
    mw = dict(wq=w_in[:, :QL], wkv=w_in[:, QL:QL + KVL], wkr4=jnp.tile(wkr, (1, SLOTS)),
              wkrs4=jnp.tile(_swap_rope(wkr), (1, SLOTS)), qn=q_norm, kvn=kv_norm,
              wn=w_uq[:, :, :NOPE].reshape(QL, NOPE_ALL), wr=wr.reshape(QL, ROPE_ALL),
              wrs=_swap_rope(wr).reshape(QL, ROPE_ALL), bduk=bduk)
    return ffn_in, ffn_out, pw, mw, bduv, full[11].reshape(D, D)


def _example_step(x, target, mod, norm_g, pvec, ffn_in, ffn_out, pw, mw, bduv, wo, reducer):
    S = x.shape[0]
    tabs = _rope_tables(S)
    vec = [[_vec(norm_g, mod, i, k) for k in range(3)] for i in range(2)]
    saved = {}
    for i in range(2):
        xin = x
        x, a, u, h = ffn_fwd(xin, vec[i][0], ffn_in[i][0], ffn_out[i][0], 0.5)
        saved[i, 0] = (xin, a, u, h)
        xin = x
        if i == 0:
            x, y, z = pool_fwd(xin, vec[i][1], pw, pvec)
            saved[i, 1] = (xin, y, z)
        else:
            h_m, cq_raw, ckv_raw, cqn, qnope, qcat, kcat, vcat = mla_pre(xin, vec[i][1], mw, tabs)
            olat, lse = attn_fwd(qcat, kcat, vcat)
            x, u_m, ocat = mla_post(olat, xin, vec[i][1], bduv, wo)
            saved[i, 1] = (xin, h_m, cq_raw, ckv_raw, cqn, qnope, qcat, kcat, olat, lse, u_m, ocat)
        xin = x
        x, a, u, h = ffn_fwd(xin, vec[i][2], ffn_in[i][1], ffn_out[i][1], 0.5)
        saved[i, 2] = (xin, a, u, h)
    loss, dx = loss_head(x, target)

    vg = {}
    G = D // 4

    def ffn_grads(i, k, dw_in, dw_out):
        return [(0, 2 * i + k, 4, dw_in.reshape(N_CHIP, 2, D // 2, FSH)),
                (1, 2 * i + k, 4, dw_out.reshape(N_CHIP, 2, DFF // 8, D))]

    vec_ffn2 = vec[1][2]
    for i in (1, 0):
        xin, a, u, h = saved[i, 2]
        dx, du, act, da, vg[i, 2] = ffn_bwd(dx, xin, u, a, vec_ffn2, ffn_in[i][1], ffn_out[i][1], 0.5)
        vec_mixer = reducer.advance(vec[i][1])
        reducer.add(f"f{i}1", ffn_grads(i, 1, *ffn_dw(h, da, act, du)))
        if i == 0:
            xin, y, z = saved[i, 1]
            dx, vg[i, 1], pgrad, g_pool = pool_bwd(dx, xin, y, z, vec_mixer, pw, pvec)
            vec_next = reducer.advance(vec[i][0])
        else:
            xin, h_m, cq_raw, ckv_raw, cqn, qnope, qcat, kcat, olat, lse, u_m, ocat = saved[i, 1]
            du, docat, dolat, delta, vg_post = mla_post_bwd(dx, u_m, olat, vec_mixer, bduv, wo)
            reducer.advance()
            dq, dk, dv = attn_bwd(qcat, kcat, kcat.T, dolat, lse, delta)
            dx, dlat, dql, dqcat, vg_pre, ngrad = mla_pre_bwd(
                dx, dq, dk, dv, xin, cq_raw, ckv_raw, reducer.advance(vec[i][1]), mw, tabs)
            vec_next = vec[i][0]
            vg[i, 1] = vg_post + vg_pre
            g = mla_dw(h_m, dlat, cqn, dqcat, dql, qnope, olat, docat, ocat, du)
            slots = lambda a: a.reshape(D, SLOTS, ROPE).sum(axis=1)
            g_kr = slots(g["in"][:, QL + KVL:QL + KVL + 128]) + _swap_rope(slots(g["in"][:, QL + KVL + 128:]))
            g_in = jnp.concatenate([g["in"][:, :QL + KVL], g_kr], axis=1)
            g_r = g["q"][:, NOPE_ALL:NOPE_ALL + ROPE_ALL].reshape(QL, N_HEADS, ROPE)
            g_rs = g["q"][:, NOPE_ALL + ROPE_ALL:].reshape(QL, N_HEADS, ROPE)
            g_uq = jnp.concatenate([g["q"][:, :NOPE_ALL].reshape(QL, N_HEADS, NOPE), g_r + _swap_rope(g_rs)], axis=-1)

            def heads(pairs):
                blk = pairs.reshape(N_PAIR, 2, KVL, 2, NOPE)
                per_head = jnp.stack([blk[:, 0, :, 0, :], blk[:, 1, :, 1, :]], axis=1).reshape(N_HEADS, KVL, NOPE)
                return jnp.transpose(per_head, (1, 0, 2)).reshape(KVL, N_HEADS * NOPE)

            reducer.add("mla", [(3, 0, 1, g_in.reshape(N_CHIP, 2, D // 8, QL + KVL + ROPE)),
                                (4, 0, 1, g_uq.reshape(N_CHIP, 2, QL // 8, N_HEADS * (NOPE + ROPE))),
                                (5, 0, 1, g["o"].reshape(N_CHIP, 2, D // 8, D))])
            reducer.add_replicated(jnp.concatenate([heads(g["uk"]), heads(g["uv"])], axis=0))
        xin, a, u, h = saved[i, 0]
        dx, du, act, da, vg[i, 0] = ffn_bwd(dx, xin, u, a, vec_next, ffn_in[i][0], ffn_out[i][0], 0.5)
        vec_ffn2 = reducer.advance(vec[0][2])
        grads = ffn_grads(i, 0, *ffn_dw(h, da, act, du))
        if i == 0:
            grads.append((2, 0, 1, g_pool.reshape(N_CHIP, 2, 2 * G // N_CHIP, G)))
        reducer.add(f"f{i}0", grads)
    return loss, dx, vg, pgrad, ngrad


class _GradReducer:
    def __init__(self, core, place, dev):
        self.core, self.place, self.dev = core, place, dev
        self.stacks = {}
        self.live = []
        self.replicated = None

    def add(self, tag, items):
        gen = self._run(tag, items)
        next(gen)
        self.live.append(gen)

    def add_replicated(self, block):
        self.replicated = gather_blocks("gather_ukv", place_block("place_ukv", self.dev, block))

    def advance(self, operand=None):
        live = []
        for gen in self.live:
            try:
                next(gen)
                live.append(gen)
            except StopIteration:
                pass
        self.live = live
        return operand

    def finish(self):
        while self.live:
            self.advance()
        return self.stacks, self.replicated

    def _run(self, tag, items):
        grads, from_pair = reduce_pair(f"reduce_pair_{tag}", [g for *_, g in items])
        yield []
        sums, tokens = [], []
        for j, (g, p) in enumerate(zip(grads, from_pair)):
            s, token = pair_add(f"pair_add_{tag}_{j}", self.core, g, p)
            sums.append(s)
            tokens.append(token)
        sums, from_chips = reduce_chips(f"reduce_chips_{tag}", sums)
        yield tokens
        tokens = []
        for j, ((o, k, n_slots, _), s, p) in enumerate(zip(items, sums, from_chips)):
            self.stacks[o], token = chip_add(f"chip_add_{tag}_{j}", self.place, s, p, k, n_slots, self.stacks.get(o))
            tokens.append(token)
        shared = share_halves(f"share_halves_{tag}", [self.stacks[o] for o, *_ in items], [k for _, k, *_ in items])
        for (o, *_), v in zip(items, shared):
            self.stacks[o] = v
        yield tokens


SMALL_IN = 8 * 640
SMALL_GRAD = 8 * 4224
SMALL_W = 8 * 2944


def _pack(parts, total):
    flat = jnp.concatenate([p.reshape(-1) for p in parts])
    return jnp.concatenate([flat, jnp.zeros((total - flat.shape[0],), F32)]).reshape(8, total // 8)


def kernel(x, c, ada_w, ada_b, norm_g, ffn_w_in, ffn_w_out, pool_w, pool_b, pool_scale, mla_w_in, mla_q_norm, mla_kv_norm, mla_w_uq, mla_w_uk, mla_w_uv, mla_w_o, loss_target, m_ada_w, m_ada_b, m_norm_g, m_ffn_w_in, m_ffn_w_out, m_pool_w, m_pool_b, m_pool_scale, m_mla_w_in, m_mla_q_norm, m_mla_kv_norm, m_mla_w_uq, m_mla_w_uk, m_mla_w_uv, m_mla_w_o, v_ada_w, v_ada_b, v_norm_g, v_ffn_w_in, v_ffn_w_out, v_pool_w, v_pool_b, v_pool_scale, v_mla_w_in, v_mla_q_norm, v_mla_kv_norm, v_mla_w_uq, v_mla_w_uk, v_mla_w_uv, v_mla_w_o):
    ix, iy, ic = _place()
    chip = 2 * ix + iy
    dev = 2 * chip + ic
    core_arr = ic.astype(jnp.int32).reshape(1)
    chip_arr = chip.astype(jnp.int32).reshape(1)
    S = x.shape[1]
    G = D // 4
    NG = D // N_CHIP

    def chip_cols(a, width, axis):
        return lax.dynamic_slice_in_dim(a, chip * width, width, axis)

    got = gather_devices("gather_small_in", _pack([c, norm_g, pool_b, mla_q_norm], SMALL_IN)).reshape(N_DEV, SMALL_IN)
    c_all = got[:, :D]
    parts = got[0::2]
    o = D
    norm_g_full = parts[:, o:o + 12 * NG].reshape(N_CHIP, 2, 6, NG).transpose(1, 2, 0, 3).reshape(2, 6, D)
    o += 12 * NG
    pool_b_full = parts[:, o:o + G].reshape(N_CHIP, 4, G // N_CHIP).transpose(1, 0, 2).reshape(1, D)
    o += G
    q_norm_full = parts[:, o:o + QL // N_CHIP].reshape(1, QL)
    pvec = jnp.concatenate([pool_b_full, pool_scale, jnp.zeros((6, D), F32)], axis=0)

    c_pad = jnp.concatenate([c_all, jnp.zeros((8, D), F32)], axis=0)
    mod_loc = mod_fwd(c_pad, ada_w, chip_cols(ada_b, MOD_COLS, 1).reshape(2, 1, MOD_COLS))
    got = gather_devices("gather_mod", mod_loc[:, :8].transpose(1, 0, 2).reshape(8, 2 * MOD_COLS))
    mine = lax.dynamic_index_in_dim(got[0::2].reshape(N_CHIP, 8, 2, MOD_COLS), dev, axis=1, keepdims=False)
    mod = mine.transpose(1, 0, 2).reshape(2, 9, D)

    bf = lambda a: a.astype(BF16)
    w_in_halves = ffn_w_in.reshape(2, 2, 2, D // 2, FSH)
    w_out_halves = ffn_w_out.reshape(2, 2, 2, DFF // 8, D)
    shards = [(w_in_halves, (i, k)) for i in range(2) for k in range(2)]
    shards += [(w_out_halves, (i, k)) for i in range(2) for k in range(2)]
    shards += [(pool_w.reshape(2, 2 * G // N_CHIP, G), ()), (mla_w_in.reshape(2, D // 8, QL + KVL + ROPE), ()),
               (mla_w_uq.reshape(2, QL // 8, N_HEADS * (NOPE + ROPE)), ()), (mla_w_o.reshape(2, D // 8, D), ())]
    full = [None] * len(shards)
    stages = [(0, 4, 8), (1, 5), (2, 6), (9, 10, 11), (3, 7)]
    first, token = cast_into_slots("cast_first", chip_arr, [shards[t] for t in stages[0]])
    slotted = dict(zip(stages[0], first))
    rest = [t for members in stages[1:] for t in members]
    for stage, members in enumerate(stages):
        got_w = gather_weights(f"gather_weights_{stage}", stage, [slotted[t] for t in members])
        for t, a in zip(members, got_w):
            full[t] = a
        if stage == 0:
            slotted.update(zip(rest, cast_into_slots("cast_rest", chip_arr, [shards[t] for t in rest], token)[0]))
    ffn_in, ffn_out, pw, mw, bduv, wo = _unpack_weights(full, bf(mla_w_uk[0]), bf(mla_w_uv[0]), q_norm_full,
                                                        mla_kv_norm)

    place_arr = jnp.stack([chip, ic]).astype(jnp.int32)
    reducer = _GradReducer(core_arr, place_arr, dev.astype(jnp.int32).reshape(1))
    loss_mine, grad_x, vg, pgrad, ngrad = _example_step(
        x[0], loss_target[0], mod, norm_g_full, pvec, ffn_in, ffn_out, pw, mw, bduv, wo, reducer)

    dmod = jnp.stack([jnp.concatenate([vg[i, k][0:3] for k in range(3)]) for i in range(2)])
    dnorm = jnp.stack([jnp.concatenate([vg[i, k][3:5] for k in range(3)]) for i in range(2)])
    small = _pack([dmod, dnorm, pgrad[0], pgrad[1], ngrad[0], ngrad[1, :KVL], loss_mine], SMALL_GRAD)
    got = gather_devices("gather_small_grad", small)
    tot = sum_devices("sum_small_grad", got).reshape(-1)
    n_mod = 2 * 9 * D
    g_ada_b = tot[:n_mod].reshape(ada_b.shape)
    o = n_mod
    g_norm = chip_cols(tot[o:o + 12 * D].reshape(2, 6, D), NG, 2)
    o += 12 * D
    g_pool_b = chip_cols(tot[o:o + D].reshape(1, 4, G), G // N_CHIP, 2)
    o += D
    g_pool_scale = tot[o:o + D].reshape(pool_scale.shape)
    o += D
    g_q_norm = chip_cols(tot[o:o + QL].reshape(1, QL), QL // N_CHIP, 1)
    o += QL
    g_kv_norm = tot[o:o + KVL].reshape(mla_kv_norm.shape)
    loss = tot[o + KVL]
    dmod_all = chip_cols(got.reshape(N_DEV, -1)[:, :n_mod].reshape(N_DEV, 2, 9 * D), MOD_COLS, 2)
    dmod_pad = jnp.concatenate([dmod_all.transpose(1, 0, 2), jnp.zeros((2, 8, MOD_COLS), F32)], axis=1)

    g_ada_w, d_ada_w, nm_ada_w, nv_ada_w = adamw_ada(c_pad, dmod_pad, ada_w, m_ada_w, v_ada_w)
    small_names = ["ada_b", "norm_g", "pool_b", "pool_scale", "mla_q_norm", "mla_kv_norm"]
    small_w = [ada_b, norm_g, pool_b, pool_scale, mla_q_norm, mla_kv_norm]
    small_g = [g_ada_b, g_norm, g_pool_b, g_pool_scale, g_q_norm, g_kv_norm]
    small_m = [m_ada_b, m_norm_g, m_pool_b, m_pool_scale, m_mla_q_norm, m_mla_kv_norm]
    small_v = [v_ada_b, v_norm_g, v_pool_b, v_pool_scale, v_mla_q_norm, v_mla_kv_norm]
    packed = adamw("adamw_small", *[_pack(p, SMALL_W) for p in (small_w, small_g, small_m, small_v)])
    upd = {}
    o = 0
    for name, w in zip(small_names, small_w):
        upd[name] = [p.reshape(-1)[o:o + w.size].reshape(w.shape) for p in packed]
        o += w.size
    upd["ada_w"] = [d_ada_w, nm_ada_w, nv_ada_w]

    reducer.advance()
    ukv = sum_devices("sum_ukv", reducer.replicated)
    g_uk = ukv[:KVL].reshape(mla_w_uk.shape)
    g_uv = ukv[KVL:].reshape(mla_w_uv.shape)
    g_mla_in = reducer.stacks[3].reshape(mla_w_in.shape)
    g_uq = reducer.stacks[4].reshape(mla_w_uq.shape)
    g_wo = reducer.stacks[5].reshape(mla_w_o.shape)
    for name, w, g, m, v in [("mla_w_in", mla_w_in, g_mla_in, m_mla_w_in, v_mla_w_in),
                             ("mla_w_uq", mla_w_uq, g_uq, m_mla_w_uq, v_mla_w_uq),
                             ("mla_w_uk", mla_w_uk, g_uk, m_mla_w_uk, v_mla_w_uk),
                             ("mla_w_uv", mla_w_uv, g_uv, m_mla_w_uv, v_mla_w_uv),
                             ("mla_w_o", mla_w_o, g_wo, m_mla_w_o, v_mla_w_o)]:
        upd[name] = adamw("adamw_" + name, w, g, m, v)
    ffn = [("ffn_w_in", 0, ffn_w_in, m_ffn_w_in, v_ffn_w_in), ("ffn_w_out", 1, ffn_w_out, m_ffn_w_out, v_ffn_w_out)]
    slots = lambda a: a.reshape((4,) + a.shape[2:])
    early = {name: adamw(f"adamw_{name}_early", slots(w), slots(reducer.stacks[o].reshape(w.shape)), slots(m),
                         slots(v), part=(1, 3)) for name, o, w, m, v in ffn}

    reducer.advance()
    stacks, _ = reducer.finish()
    g_ffn_in = stacks[0].reshape(ffn_w_in.shape)
    g_ffn_out = stacks[1].reshape(ffn_w_out.shape)
    g_pool_w = stacks[2].reshape(pool_w.shape)
    for name, o, w, m, v in ffn:
        done = adamw(f"adamw_{name}_last", slots(w), slots(stacks[o].reshape(w.shape)), slots(m), slots(v),
                     part=(0, 1), prev=early[name])
        upd[name] = [p.reshape(w.shape) for p in done]
    upd["pool_w"] = adamw("adamw_pool_w", pool_w, g_pool_w, m_pool_w, v_pool_w)

    order = ["ada_w", "ada_b", "norm_g", "ffn_w_in", "ffn_w_out", "pool_w", "pool_b", "pool_scale", "mla_w_in",
             "mla_q_norm", "mla_kv_norm", "mla_w_uq", "mla_w_uk", "mla_w_uv", "mla_w_o"]
    grad = dict(ada_w=g_ada_w, ada_b=g_ada_b, norm_g=g_norm, ffn_w_in=g_ffn_in, ffn_w_out=g_ffn_out, pool_w=g_pool_w,
                pool_b=g_pool_b, pool_scale=g_pool_scale, mla_w_in=g_mla_in, mla_q_norm=g_q_norm,
                mla_kv_norm=g_kv_norm, mla_w_uq=g_uq, mla_w_uk=g_uk, mla_w_uv=g_uv, mla_w_o=g_wo)
    return (loss, grad_x[None], *[grad[n] for n in order], *[upd[n][0] for n in order],
            *[upd[n][1] for n in order], *[upd[n][2] for n in order])
```

```python
import functools

import jax
import jax.numpy as jnp
from jax import lax
from jax.experimental import pallas as pl
from jax.experimental.pallas import tpu as pltpu
from jax.experimental.pallas import tpu_sc as plsc

F32 = jnp.float32
BF16 = jnp.bfloat16

D = 1024
DFF = 2816
FSH = 1408
N_CHIP = 4
N_DEV = 8
N_HEADS = 16
NOPE = 64
ROPE = 32
VH = 64
QL = 256
KVL = 128
QPAD = 256
EPS = 1e-6
ATTN_SCALE = (NOPE + ROPE) ** -0.5
ROPE_THETA = 10000.0
POOL_WINDOWS = (2, 4, 8, 16)
HALO = 8
ATTN_TQ = 1024
ATTN_KC = 512
DW_TK = 2048

ADAM_LR, ADAM_B1, ADAM_B2, ADAM_EPS, ADAM_WD, ADAM_STEP = 0.001, 0.9, 0.999, 1e-08, 0.01, 10

VMEM_LIMIT = 60 * 1024 * 1024
MESH = pl.DeviceIdType.MESH

NT = (((1,), (1,)), ((), ()))
TN = (((0,), (0,)), ((), ()))


def _params(*sem):
    return pltpu.CompilerParams(dimension_semantics=sem, vmem_limit_bytes=VMEM_LIMIT)


def _dot(a, b, dims=None):
    if dims is None:
        return jnp.dot(a, b, preferred_element_type=F32)
    return lax.dot_general(a, b, dims, preferred_element_type=F32)


def _rms(x):
    r = lax.rsqrt(jnp.mean(x * x, axis=-1, keepdims=True) + EPS)
    return x * r, r


def _rms_bwd(xhat, r, dxhat):
    return r * (dxhat - xhat * jnp.mean(dxhat * xhat, axis=-1, keepdims=True))


def _as_row(col):
    return jnp.broadcast_to(col, (col.shape[0], 128)).T[0:1, :]


def _prenorm(x, vec_ref):
    xhat, r = _rms(x)
    h = xhat * vec_ref[0:1, :] * (1.0 + vec_ref[3:4, :]) + vec_ref[2:3, :]
    return h, xhat, r


def _postnorm_bwd(dout, u, vec_ref, weight):
    uhat, r = _rms(u)
    gt = weight * (1.0 + vec_ref[4:5, :])
    dy = dout * gt
    dgate_rows = (weight * dout) * (uhat * vec_ref[1:2, :])
    dgpost_rows = dy * uhat
    du = _rms_bwd(uhat, r, dy * vec_ref[1:2, :])
    return du, dgate_rows, dgpost_rows


def _prenorm_bwd(dh, x, vec_ref, vg_ref):
    xhat, r = _rms(x)
    sc1 = 1.0 + vec_ref[3:4, :]
    g = vec_ref[0:1, :]
    vg_ref[0:1, :] += jnp.sum(dh, axis=0, keepdims=True)
    vg_ref[1:2, :] += jnp.sum(dh * (xhat * g), axis=0, keepdims=True)
    vg_ref[3:4, :] += jnp.sum(dh * sc1 * xhat, axis=0, keepdims=True)
    return _rms_bwd(xhat, r, dh * g * sc1)


def ffn_fwd(x, vec, w_in, w_out, weight):
    S = x.shape[0]
    tm = min(512, S)

    def body(x_ref, vec_ref, wg_ref, wu_ref, wo_ref, xo_ref, a_ref, u_ref, h_ref, acc_ref):
        j = pl.program_id(1)

        @pl.when(j == 0)
        def _():
            h, _, _ = _prenorm(x_ref[...], vec_ref)
            h_ref[...] = h.astype(BF16)
            acc_ref[...] = jnp.zeros_like(acc_ref)

        hb = h_ref[...]
        g = _dot(hb, wg_ref[...])
        up = _dot(hb, wu_ref[...])
        a_ref[0] = g.astype(BF16)
        a_ref[1] = up.astype(BF16)
        act = (g * jax.nn.sigmoid(g)) * up
        acc_ref[...] += _dot(act.astype(BF16), wo_ref[...])

        @pl.when(j == 1)
        def _():
            u = acc_ref[...]
            u_ref[...] = u
            uhat, _ = _rms(u)
            xo_ref[...] = x_ref[...] + (weight * (1.0 + vec_ref[4:5, :])) * (uhat * vec_ref[1:2, :])

    return pl.pallas_call(
        body, name="ffn_fwd", grid=(S // tm, 2),
        in_specs=[pl.BlockSpec((tm, D), lambda i, j: (i, 0)),
                  pl.BlockSpec((8, D), lambda i, j: (0, 0)),
                  pl.BlockSpec((None, D, FSH), lambda i, j: (j, 0, 0)),
                  pl.BlockSpec((None, D, FSH), lambda i, j: (j + 2, 0, 0)),
                  pl.BlockSpec((None, FSH, D), lambda i, j: (j, 0, 0))],
        out_specs=[pl.BlockSpec((tm, D), lambda i, j: (i, 0)),
                   pl.BlockSpec((2, tm, FSH), lambda i, j: (0, i, j)),
                   pl.BlockSpec((tm, D), lambda i, j: (i, 0)),
                   pl.BlockSpec((tm, D), lambda i, j: (i, 0))],
        out_shape=[jax.ShapeDtypeStruct((S, D), F32), jax.ShapeDtypeStruct((2, S, DFF), BF16),
                   jax.ShapeDtypeStruct((S, D), F32), jax.ShapeDtypeStruct((S, D), BF16)],
        scratch_shapes=[pltpu.VMEM((tm, D), F32)],
        compiler_params=_params("parallel", "arbitrary"),
    )(x, vec, w_in, w_in, w_out)


def ffn_bwd(dout, x, u, a, vec, w_in, w_out, weight):
    S = x.shape[0]
    tm = min(512, S)
    row = lambda i: (i, 0)
    half = lambda j: [pl.BlockSpec((2, tm, FSH), lambda i: (0, i, j)), _w3((8, D)),
                      pl.BlockSpec((None, D, FSH), lambda i: (j, 0, 0)),
                      pl.BlockSpec((None, D, FSH), lambda i: (j + 2, 0, 0)),
                      pl.BlockSpec((None, FSH, D), lambda i: (j, 0, 0))]
    half_out = lambda j: [pl.BlockSpec((tm, FSH), lambda i: (i, j)), pl.BlockSpec((2, tm, FSH), lambda i: (0, i, j))]
    half_shape = [jax.ShapeDtypeStruct((S, DFF), BF16), jax.ShapeDtypeStruct((2, S, DFF), BF16)]

    def hidden_bwd(du, a_ref, wg_ref, wu_ref, wo_ref, act_ref, da_ref):
        dact = _dot(du, wo_ref[...], NT)
        g = a_ref[0].astype(F32)
        up = a_ref[1].astype(F32)
        s = jax.nn.sigmoid(g)
        silu = g * s
        act_ref[...] = (silu * up).astype(BF16)
        dg = (dact * up * (s * (1.0 + g * (1.0 - s)))).astype(BF16)
        dup = (dact * silu).astype(BF16)
        da_ref[0] = dg
        da_ref[1] = dup
        return _dot(dg, wg_ref[...], NT) + _dot(dup, wu_ref[...], NT)

    def first(do_ref, u_ref, a_ref, vec_ref, wg_ref, wu_ref, wo_ref, du_ref, dh_ref, act_ref, da_ref, vg_ref):
        @pl.when(pl.program_id(0) == 0)
        def _():
            vg_ref[...] = jnp.zeros_like(vg_ref)

        du, dgate_rows, dgpost_rows = _postnorm_bwd(do_ref[...], u_ref[...], vec_ref, weight)
        vg_ref[2:3, :] += jnp.sum(dgate_rows, axis=0, keepdims=True)
        vg_ref[4:5, :] += jnp.sum(dgpost_rows, axis=0, keepdims=True)
        du = du.astype(BF16)
        du_ref[...] = du
        dh_ref[...] = hidden_bwd(du, a_ref, wg_ref, wu_ref, wo_ref, act_ref, da_ref)

    du, dh, act, da, vg_post = pl.pallas_call(
        first, name="ffn_bwd_first", grid=(S // tm,),
        in_specs=[pl.BlockSpec((tm, D), row), pl.BlockSpec((tm, D), row)] + half(0),
        out_specs=[pl.BlockSpec((tm, D), row), pl.BlockSpec((tm, D), row)] + half_out(0) + [_w3((8, D))],
        out_shape=[jax.ShapeDtypeStruct((S, D), BF16), jax.ShapeDtypeStruct((S, D), F32)] + half_shape
        + [jax.ShapeDtypeStruct((8, D), F32)],
        compiler_params=_params("arbitrary"),
    )(dout, u, a, vec, w_in, w_in, w_out)

    def second(do_ref, x_ref, du_ref, dh_ref, a_ref, vec_ref, wg_ref, wu_ref, wo_ref, act_in, da_in,
               dx_ref, act_ref, da_ref, vg_ref):
        @pl.when(pl.program_id(0) == 0)
        def _():
            vg_ref[...] = jnp.zeros_like(vg_ref)

        dh = dh_ref[...] + hidden_bwd(du_ref[...], a_ref, wg_ref, wu_ref, wo_ref, act_ref, da_ref)
        dx_ref[...] = do_ref[...] + _prenorm_bwd(dh, x_ref[...], vec_ref, vg_ref)

    dx, act, da, vg_pre = pl.pallas_call(
        second, name="ffn_bwd_second", grid=(S // tm,),
        in_specs=[pl.BlockSpec((tm, D), row), pl.BlockSpec((tm, D), row), pl.BlockSpec((tm, D), row),
                  pl.BlockSpec((tm, D), row)] + half(1) + [_ANY, _ANY],
        out_specs=[pl.BlockSpec((tm, D), row)] + half_out(1) + [_w3((8, D))],
        out_shape=[jax.ShapeDtypeStruct((S, D), F32)] + half_shape + [jax.ShapeDtypeStruct((8, D), F32)],
        input_output_aliases={9: 1, 10: 2},
        compiler_params=_params("arbitrary"),
    )(dout, x, du, dh, a, vec, w_in, w_in, w_out, act, da)
    return dx, du, act, da, vg_post + vg_pre


def dw_matmul(name, a, b, a_spec, b_spec, out_shape, out_spec, grid):
    def body(a_ref, b_ref, o_ref):
        @pl.when(pl.program_id(len(grid) - 1) == 0)
        def _():
            o_ref[...] = jnp.zeros_like(o_ref)

        o_ref[...] += _dot(a_ref[...], b_ref[...], TN)

    return pl.pallas_call(
        body, name=name, grid=grid, in_specs=[a_spec, b_spec], out_specs=out_spec,
        out_shape=jax.ShapeDtypeStruct(out_shape, F32),
        compiler_params=_params(*(["parallel"] * (len(grid) - 1) + ["arbitrary"])),
    )(a, b)


def ffn_dw(h, da, act, du):
    S = h.shape[0]
    tk = min(DW_TK, S)
    dw_in = dw_matmul("ffn_dw_in", h, da,
                      pl.BlockSpec((tk, D), lambda n, k: (k, 0)),
                      pl.BlockSpec((None, tk, FSH), lambda n, k: (n // 2, k, n % 2)),
                      (N_CHIP, D, FSH), pl.BlockSpec((None, D, FSH), lambda n, k: (n, 0, 0)),
                      (N_CHIP, S // tk))
    dw_out = dw_matmul("ffn_dw_out", act, du,
                       pl.BlockSpec((tk, FSH), lambda n, k: (k, n)),
                       pl.BlockSpec((tk, D), lambda n, k: (k, 0)),
                       (DFF, D), pl.BlockSpec((FSH, D), lambda n, k: (n, 0)),
                       (2, S // tk))
    return dw_in, dw_out


def _halo_specs(tm, S):
    nb = tm // HALO
    last = S // HALO - 1
    return [pl.BlockSpec((HALO, D), lambda i: (jnp.maximum(i * nb - 1, 0), 0)),
            pl.BlockSpec((tm, D), lambda i: (i, 0)),
            pl.BlockSpec((HALO, D), lambda i: (jnp.minimum((i + 1) * nb, last), 0))]


def _shift_rows(v, k):
    return pltpu.roll(v, k % v.shape[0], 0)


def _window_sum(v, g, forward):
    acc = v + _shift_rows(v, 1 if forward else -1)
    for step in (1, 2, 4)[:g]:
        acc = _shift_rows(acc, step) + _shift_rows(acc, -step)
    return acc


def _pool_count(t, w, S):
    return jnp.maximum(jnp.minimum(t + w // 2, S) - jnp.maximum(t - w // 2, 0), 1).astype(F32)


def pool_fwd(x, vec, pw, pvec):
    S = x.shape[0]
    tm = min(256, S)
    G = D // 4

    def body(xp_ref, x_ref, xn_ref, vec_ref, pw_ref, pv_ref, xo_ref, y_ref, z_ref):
        i = pl.program_id(0)
        xa = jnp.concatenate([xp_ref[...], x_ref[...], xn_ref[...]], axis=0)
        t = i * tm - HALO + lax.broadcasted_iota(jnp.int32, (tm + 2 * HALO, 1), 0)
        h, _, _ = _prenorm(xa, vec_ref)
        h = jnp.where((t >= 0) & (t < S), h, 0.0)
        tmain = t[HALO:HALO + tm]
        for g in range(4):
            hg = h[:, g * G:(g + 1) * G]
            pooled = _window_sum(hg, g, True)[HALO:HALO + tm] / _pool_count(tmain, POOL_WINDOWS[g], S)
            z = (pooled - hg[HALO:HALO + tm]).astype(BF16)
            z_ref[:, g * G:(g + 1) * G] = z
            y_ref[:, g * G:(g + 1) * G] = _dot(z, pw_ref[g]) + pv_ref[0:1, g * G:(g + 1) * G]
        u = y_ref[...] * pv_ref[1:2, :]
        uhat, _ = _rms(u)
        xo_ref[...] = x_ref[...] + (1.0 + vec_ref[4:5, :]) * (uhat * vec_ref[1:2, :])

    row = lambda i: (i, 0)
    full = lambda i: (0, 0)
    return pl.pallas_call(
        body, name="pool_fwd", grid=(S // tm,),
        in_specs=_halo_specs(tm, S) + [pl.BlockSpec((8, D), full), pl.BlockSpec((4, G, G), lambda i: (0, 0, 0)),
                                       pl.BlockSpec((8, D), full)],
        out_specs=[pl.BlockSpec((tm, D), row)] * 3,
        out_shape=[jax.ShapeDtypeStruct((S, D), F32), jax.ShapeDtypeStruct((S, D), F32),
                   jax.ShapeDtypeStruct((S, D), BF16)],
        compiler_params=_params("parallel"),
    )(x, x, x, vec, pw, pvec)


def pool_bwd(dout, x, y, z, vec, pw, pvec):
    S = x.shape[0]
    tm = min(256, S)
    G = D // 4
    R = G // N_CHIP

    def body(dop_ref, do_ref, don_ref, yp_ref, y_ref, yn_ref, x_ref, z_ref, vec_ref, pw_ref, pv_ref,
             dx_ref, vg_ref, pg_ref, dw_ref, dh_ref):
        i = pl.program_id(0)

        @pl.when(i == 0)
        def _():
            vg_ref[...] = jnp.zeros_like(vg_ref)
            pg_ref[...] = jnp.zeros_like(pg_ref)
            dw_ref[...] = jnp.zeros_like(dw_ref)

        doa = jnp.concatenate([dop_ref[...], do_ref[...], don_ref[...]], axis=0)
        ya = jnp.concatenate([yp_ref[...], y_ref[...], yn_ref[...]], axis=0)
        t = i * tm - HALO + lax.broadcasted_iota(jnp.int32, (tm + 2 * HALO, 1), 0)
        inside = (t >= 0) & (t < S)
        main = (t >= i * tm) & (t < (i + 1) * tm)
        du, dgate_rows, dgpost_rows = _postnorm_bwd(doa, ya * pv_ref[1:2, :], vec_ref, 1.0)
        du = jnp.where(inside, du, 0.0)
        vg_ref[2:3, :] += jnp.sum(jnp.where(main, dgate_rows, 0.0), axis=0, keepdims=True)
        vg_ref[4:5, :] += jnp.sum(jnp.where(main, dgpost_rows, 0.0), axis=0, keepdims=True)
        dy = du * pv_ref[1:2, :]
        pg_ref[0:1, :] += jnp.sum(jnp.where(main, dy, 0.0), axis=0, keepdims=True)
        pg_ref[1:2, :] += jnp.sum(jnp.where(main, du * ya, 0.0), axis=0, keepdims=True)
        for g in range(4):
            dyg = dy[:, g * G:(g + 1) * G].astype(BF16)
            dz = _dot(dyg, pw_ref[g], NT)
            e = dz / _pool_count(t, POOL_WINDOWS[g], S)
            dh_ref[:, g * G:(g + 1) * G] = (_window_sum(e, g, False) - dz)[HALO:HALO + tm]
            dwg = _dot(z_ref[:, g * G:(g + 1) * G], dyg[HALO:HALO + tm], TN)
            for q in range(N_CHIP):
                dw_ref[q, g] += dwg[q * R:(q + 1) * R, :]
        dx_ref[...] = do_ref[...] + _prenorm_bwd(dh_ref[...], x_ref[...], vec_ref, vg_ref)

    row = lambda i: (i, 0)
    full = lambda i: (0, 0)
    halo = _halo_specs(tm, S)
    return pl.pallas_call(
        body, name="pool_bwd", grid=(S // tm,),
        in_specs=halo + halo + [pl.BlockSpec((tm, D), row), pl.BlockSpec((tm, D), row), pl.BlockSpec((8, D), full),
                                pl.BlockSpec((4, G, G), lambda i: (0, 0, 0)), pl.BlockSpec((8, D), full)],
        out_specs=[pl.BlockSpec((tm, D), row), pl.BlockSpec((8, D), full), pl.BlockSpec((8, D), full),
                   pl.BlockSpec((N_CHIP, 4, R, G), lambda i: (0, 0, 0, 0))],
        out_shape=[jax.ShapeDtypeStruct((S, D), F32), jax.ShapeDtypeStruct((8, D), F32),
                   jax.ShapeDtypeStruct((8, D), F32), jax.ShapeDtypeStruct((N_CHIP, 4, R, G), F32)],
        scratch_shapes=[pltpu.VMEM((tm, D), F32)],
        compiler_params=_params("arbitrary"),
    )(dout, dout, dout, y, y, y, x, z, vec, pw, pvec)


N_PAIR = N_HEADS // 2
SLOTS = 128 // ROPE
ROPE_ALL = N_HEADS * ROPE
NOPE_ALL = N_HEADS * NOPE
LAT_ALL = N_HEADS * KVL
DLAT = QL + KVL + 2 * 128
DQ_ALL = NOPE_ALL + 2 * ROPE_ALL


def _w3(shape):
    return pl.BlockSpec(shape, lambda i: (0,) * len(shape))


def _slot_mask(hd, rows):
    lane = lax.broadcasted_iota(jnp.int32, (rows, 128), 1)
    return (lane // ROPE) == (hd % SLOTS)


MLA_WEIGHTS = ("wq", "wkv", "wkr4", "wkrs4", "qn", "kvn", "wn", "wr", "wrs", "bduk")


def _mla_weight_specs():
    return [_w3((D, QL)), _w3((D, KVL)), _w3((D, 128)), _w3((D, 128)), _w3((1, QL)), _w3((1, KVL)),
            _w3((QL, NOPE_ALL)), _w3((QL, ROPE_ALL)), _w3((QL, ROPE_ALL)), _w3((N_PAIR, 2 * NOPE, 2 * KVL))]


def mla_pre(x, vec, mw, tabs):
    S = x.shape[0]
    tm = min(256, S)

    def body(x_ref, vec_ref, cos_ref, sin_ref, wq_ref, wkv_ref, wkr_ref, wkrs_ref, qn_ref, kvn_ref,
             wn_ref, wr_ref, wrs_ref, bduk_ref,
             h_ref, cq_ref, ckv_ref, cqn_ref, qnope_ref, qcat_ref, kcat_ref, vcat_ref):
        h, _, _ = _prenorm(x_ref[...], vec_ref)
        hb = h.astype(BF16)
        h_ref[...] = hb
        cq_raw = _dot(hb, wq_ref[...])
        ckv_raw = _dot(hb, wkv_ref[...])
        cq_ref[...] = cq_raw
        ckv_ref[...] = ckv_raw
        cos, sin = cos_ref[...], sin_ref[...]
        ckv = (_rms(ckv_raw)[0] * kvn_ref[...]).astype(BF16)
        kcat_ref[:, 0:KVL] = ckv
        kcat_ref[:, KVL:] = (_dot(hb, wkr_ref[...]) * cos + _dot(hb, wkrs_ref[...]) * sin).astype(BF16)
        vcat_ref[:, 0:KVL] = ckv
        ones = lax.broadcasted_iota(jnp.int32, (tm, QPAD - KVL), 1) == 0
        vcat_ref[:, KVL:] = jnp.where(ones, 1.0, 0.0).astype(BF16)
        cqb = (_rms(cq_raw)[0] * qn_ref[...]).astype(BF16)
        cqn_ref[...] = cqb
        qn = _dot(cqb, wn_ref[...]).astype(BF16)
        qnope_ref[...] = qn
        cos4, sin4 = jnp.tile(cos, (1, SLOTS)), jnp.tile(sin, (1, SLOTS))
        qr = ((_dot(cqb, wr_ref[...]) * cos4 + _dot(cqb, wrs_ref[...]) * sin4) * ATTN_SCALE).astype(BF16)
        for j in range(N_PAIR):
            ql = (_dot(qn[:, 128 * j:128 * (j + 1)], bduk_ref[j]) * ATTN_SCALE).astype(BF16)
            for hd in (2 * j, 2 * j + 1):
                qcat_ref[hd, :, 0:KVL] = ql[:, KVL * (hd - 2 * j):KVL * (hd - 2 * j + 1)]
                group = qr[:, 128 * (hd // SLOTS):128 * (hd // SLOTS + 1)]
                qcat_ref[hd, :, KVL:] = jnp.where(_slot_mask(hd, tm), group, jnp.zeros_like(group))

    row = lambda i: (i, 0)
    hrow = lambda i: (0, i, 0)
    return pl.pallas_call(
        body, name="mla_pre", grid=(S // tm,),
        in_specs=[pl.BlockSpec((tm, D), row), _w3((8, D)), pl.BlockSpec((tm, 128), row), pl.BlockSpec((tm, 128), row)]
        + _mla_weight_specs(),
        out_specs=[pl.BlockSpec((tm, D), row), pl.BlockSpec((tm, QL), row), pl.BlockSpec((tm, KVL), row),
                   pl.BlockSpec((tm, QL), row), pl.BlockSpec((tm, NOPE_ALL), row),
                   pl.BlockSpec((N_HEADS, tm, QPAD), hrow), pl.BlockSpec((tm, QPAD), row),
                   pl.BlockSpec((tm, QPAD), row)],
        out_shape=[jax.ShapeDtypeStruct((S, D), BF16), jax.ShapeDtypeStruct((S, QL), F32),
                   jax.ShapeDtypeStruct((S, KVL), F32), jax.ShapeDtypeStruct((S, QL), BF16),
                   jax.ShapeDtypeStruct((S, NOPE_ALL), BF16), jax.ShapeDtypeStruct((N_HEADS, S, QPAD), BF16),
                   jax.ShapeDtypeStruct((S, QPAD), BF16), jax.ShapeDtypeStruct((S, QPAD), BF16)],
        compiler_params=_params("parallel"),
    )(x, vec, tabs[0], tabs[1], *[mw[k] for k in MLA_WEIGHTS])


def attn_fwd(qcat, kcat, vcat):
    S = kcat.shape[0]
    tq = min(ATTN_TQ, S)
    kc = min(ATTN_KC, S)

    def body(q_ref, k_ref, v_ref, o_ref, lse_ref):
        q = q_ref[...]
        m = jnp.full((tq, 1), -jnp.inf, F32)
        ov = jnp.zeros((tq, QPAD), F32)
        for c in range(S // kc):
            s = _dot(q, k_ref[c * kc:(c + 1) * kc, :], NT)
            m_new = jnp.maximum(m, jnp.max(s, axis=-1, keepdims=True))
            p = jnp.exp(s - m_new).astype(BF16)
            ov = ov * jnp.exp(m - m_new) + _dot(p, v_ref[c * kc:(c + 1) * kc, :])
            m = m_new
        l = ov[:, KVL:KVL + 1]
        o_ref[...] = (ov[:, 0:KVL] * (1.0 / l)).astype(BF16)
        lse_ref[...] = _as_row(m + jnp.log(l))

    return pl.pallas_call(
        body, name="attn_fwd", grid=(N_HEADS, S // tq),
        in_specs=[pl.BlockSpec((None, tq, QPAD), lambda h, i: (h, i, 0)),
                  pl.BlockSpec((S, QPAD), lambda h, i: (0, 0)),
                  pl.BlockSpec((S, QPAD), lambda h, i: (0, 0))],
        out_specs=[pl.BlockSpec((tq, KVL), lambda h, i: (i, h)),
                   pl.BlockSpec((None, 1, tq), lambda h, i: (h, 0, i))],
        out_shape=[jax.ShapeDtypeStruct((S, LAT_ALL), BF16), jax.ShapeDtypeStruct((N_HEADS, 1, S), F32)],
        compiler_params=_params("parallel", "parallel"),
    )(qcat, kcat, vcat)


def mla_post(olat, x, vec, bduv, wo):
    S = x.shape[0]
    tm = min(256, S)

    def body(o_ref, x_ref, vec_ref, bduv_ref, wo_ref, xo_ref, u_ref, ocat_ref):
        for j in range(N_PAIR):
            oc = _dot(o_ref[:, 2 * KVL * j:2 * KVL * (j + 1)], bduv_ref[j])
            ocat_ref[:, 2 * VH * j:2 * VH * (j + 1)] = oc.astype(BF16)
        u = _dot(ocat_ref[...], wo_ref[...])
        u_ref[...] = u
        uhat, _ = _rms(u)
        xo_ref[...] = x_ref[...] + (1.0 + vec_ref[4:5, :]) * (uhat * vec_ref[1:2, :])

    row = lambda i: (i, 0)
    return pl.pallas_call(
        body, name="mla_post", grid=(S // tm,),
        in_specs=[pl.BlockSpec((tm, LAT_ALL), row), pl.BlockSpec((tm, D), row), _w3((8, D)),
                  _w3((N_PAIR, 2 * KVL, 2 * VH)), _w3((D, D))],
        out_specs=[pl.BlockSpec((tm, D), row), pl.BlockSpec((tm, D), row), pl.BlockSpec((tm, D), row)],
        out_shape=[jax.ShapeDtypeStruct((S, D), F32), jax.ShapeDtypeStruct((S, D), F32),
                   jax.ShapeDtypeStruct((S, D), BF16)],
        compiler_params=_params("parallel"),
    )(olat, x, vec, bduv, wo)


def mla_post_bwd(dout, u, olat, vec, bduv, wo):
    S = u.shape[0]
    tm = min(256, S)

    def body(do_ref, u_ref, o_ref, vec_ref, bduv_ref, wo_ref, du_ref, docat_ref, dolat_ref, delta_ref, vg_ref):
        @pl.when(pl.program_id(0) == 0)
        def _():
            vg_ref[...] = jnp.zeros_like(vg_ref)

        du, dgate_rows, dgpost_rows = _postnorm_bwd(do_ref[...], u_ref[...], vec_ref, 1.0)
        vg_ref[2:3, :] += jnp.sum(dgate_rows, axis=0, keepdims=True)
        vg_ref[4:5, :] += jnp.sum(dgpost_rows, axis=0, keepdims=True)
        dub = du.astype(BF16)
        du_ref[...] = dub
        docat_ref[...] = _dot(dub, wo_ref[...], NT).astype(BF16)
        for j in range(N_PAIR):
            dol = _dot(docat_ref[:, 2 * VH * j:2 * VH * (j + 1)], bduv_ref[j], NT).astype(BF16)
            dolat_ref[:, 2 * KVL * j:2 * KVL * (j + 1)] = dol
            prod = dol.astype(F32) * o_ref[:, 2 * KVL * j:2 * KVL * (j + 1)].astype(F32)
            delta_ref[2 * j] = _as_row(jnp.sum(prod[:, 0:KVL], axis=-1, keepdims=True))
            delta_ref[2 * j + 1] = _as_row(jnp.sum(prod[:, KVL:], axis=-1, keepdims=True))

    row = lambda i: (i, 0)
    hrow = lambda i: (0, i, 0)
    return pl.pallas_call(
        body, name="mla_post_bwd", grid=(S // tm,),
        in_specs=[pl.BlockSpec((tm, D), row), pl.BlockSpec((tm, D), row), pl.BlockSpec((tm, LAT_ALL), row),
                  _w3((8, D)), _w3((N_PAIR, 2 * KVL, 2 * VH)), _w3((D, D))],
        out_specs=[pl.BlockSpec((tm, D), row), pl.BlockSpec((tm, D), row),
                   pl.BlockSpec((tm, LAT_ALL), row), pl.BlockSpec((N_HEADS, 1, tm), lambda i: (0, 0, i)), _w3((8, D))],
        out_shape=[jax.ShapeDtypeStruct((S, D), BF16), jax.ShapeDtypeStruct((S, D), BF16),
                   jax.ShapeDtypeStruct((S, LAT_ALL), BF16), jax.ShapeDtypeStruct((N_HEADS, 1, S), F32),
                   jax.ShapeDtypeStruct((8, D), F32)],
        compiler_params=_params("arbitrary"),
    )(dout, u, olat, vec, bduv, wo)


def attn_bwd(qcat, kcat, kcat_t, dolat, lse_row, delta_row):
    S = kcat.shape[0]
    tq = min(ATTN_TQ, S)
    kc = min(ATTN_KC, S)

    def body(q_ref, k_ref, kt_ref, do_ref, lse_ref, dl_ref, dq_ref, dk_ref, dv_ref):
        @pl.when((pl.program_id(0) == 0) & (pl.program_id(1) == 0))
        def _():
            dk_ref[...] = jnp.zeros_like(dk_ref)
            dv_ref[...] = jnp.zeros_like(dv_ref)

        q, do = q_ref[...], do_ref[...]
        lse, dl = lse_ref[...], dl_ref[...]
        dqt = jnp.zeros((QPAD, tq), F32)
        for c in range(S // kc):
            rows = slice(c * kc, (c + 1) * kc)
            st = _dot(k_ref[rows, :], q, NT)
            pt = jnp.exp(st - lse)
            dpt = _dot(k_ref[rows, 0:KVL], do, NT)
            dst = (pt * (dpt - dl)).astype(BF16)
            dv_ref[rows, :] += _dot(pt.astype(BF16), do)
            dk_ref[rows, :] += _dot(dst, q)
            dqt = dqt + _dot(kt_ref[:, rows], dst)
        dq_ref[...] = dqt.T

    return pl.pallas_call(
        body, name="attn_bwd", grid=(N_HEADS, S // tq),
        in_specs=[pl.BlockSpec((None, tq, QPAD), lambda h, i: (h, i, 0)),
                  pl.BlockSpec((S, QPAD), lambda h, i: (0, 0)),
                  pl.BlockSpec((QPAD, S), lambda h, i: (0, 0)),
                  pl.BlockSpec((tq, KVL), lambda h, i: (i, h)),
                  pl.BlockSpec((None, 1, tq), lambda h, i: (h, 0, i)),
                  pl.BlockSpec((None, 1, tq), lambda h, i: (h, 0, i))],
        out_specs=[pl.BlockSpec((None, tq, QPAD), lambda h, i: (h, i, 0)),
                   pl.BlockSpec((S, QPAD), lambda h, i: (0, 0)),
                   pl.BlockSpec((S, KVL), lambda h, i: (0, 0))],
        out_shape=[jax.ShapeDtypeStruct((N_HEADS, S, QPAD), F32), jax.ShapeDtypeStruct((S, QPAD), F32),
                   jax.ShapeDtypeStruct((S, KVL), F32)],
        compiler_params=_params("arbitrary", "arbitrary"),
    )(qcat, kcat, kcat_t, dolat, lse_row, delta_row)


def mla_pre_bwd(dout, dq, dk, dv, x, cq_raw, ckv_raw, vec, mw, tabs):
    S = x.shape[0]
    tm = min(256, S)

    def body(do_ref, dq_ref, dk_ref, dv_ref, x_ref, cq_ref, ckv_ref, vec_ref, cos_ref, sin_ref,
             wq_ref, wkv_ref, wkr_ref, wkrs_ref, qn_ref, kvn_ref, wn_ref, wr_ref, wrs_ref, bduk_ref,
             dx_ref, dlat_ref, dql_ref, dqcat_ref, vg_ref, ng_ref):
        @pl.when(pl.program_id(0) == 0)
        def _():
            vg_ref[...] = jnp.zeros_like(vg_ref)
            ng_ref[...] = jnp.zeros_like(ng_ref)

        cos, sin = cos_ref[...], sin_ref[...]
        for j in range(N_PAIR):
            dql = jnp.concatenate([dq_ref[2 * j, :, 0:KVL], dq_ref[2 * j + 1, :, 0:KVL]], axis=1) * ATTN_SCALE
            dql = dql.astype(BF16)
            dql_ref[:, 2 * KVL * j:2 * KVL * (j + 1)] = dql
            dqcat_ref[:, 2 * NOPE * j:2 * NOPE * (j + 1)] = _dot(dql, bduk_ref[j], NT).astype(BF16)
        groups = []
        for grp in range(N_HEADS // SLOTS):
            acc = jnp.zeros((tm, 128), F32)
            for hd in range(SLOTS * grp, SLOTS * (grp + 1)):
                acc = acc + jnp.where(_slot_mask(hd, tm), dq_ref[hd, :, KVL:], 0.0)
            groups.append(acc)
        dqr = jnp.concatenate(groups, axis=1) * ATTN_SCALE
        qa = (dqr * jnp.tile(cos, (1, SLOTS))).astype(BF16)
        qb = (dqr * jnp.tile(sin, (1, SLOTS))).astype(BF16)
        dqcat_ref[:, NOPE_ALL:NOPE_ALL + ROPE_ALL] = qa
        dqcat_ref[:, NOPE_ALL + ROPE_ALL:] = qb
        dcq = _dot(dqcat_ref[:, 0:NOPE_ALL], wn_ref[...], NT) + _dot(qa, wr_ref[...], NT) + _dot(qb, wrs_ref[...], NT)
        cqh, rq = _rms(cq_ref[...])
        ng_ref[0:1, :] += jnp.sum(dcq * cqh, axis=0, keepdims=True)
        dcq_raw = _rms_bwd(cqh, rq, dcq * qn_ref[...]).astype(BF16)
        dckv = dk_ref[:, 0:KVL] + dv_ref[...]
        ckvh, rk = _rms(ckv_ref[...])
        ng_ref[1:2, 0:KVL] += jnp.sum(dckv * ckvh, axis=0, keepdims=True)
        dckv_raw = _rms_bwd(ckvh, rk, dckv * kvn_ref[...]).astype(BF16)
        dkr = dk_ref[:, KVL:]
        ka = (dkr * cos).astype(BF16)
        kb = (dkr * sin).astype(BF16)
        dlat_ref[:, 0:QL] = dcq_raw
        dlat_ref[:, QL:QL + KVL] = dckv_raw
        dlat_ref[:, QL + KVL:QL + KVL + 128] = ka
        dlat_ref[:, QL + KVL + 128:] = kb
        dh = (_dot(dcq_raw, wq_ref[...], NT) + _dot(dckv_raw, wkv_ref[...], NT)
              + _dot(ka, wkr_ref[...], NT) + _dot(kb, wkrs_ref[...], NT))
        dx_ref[...] = do_ref[...] + _prenorm_bwd(dh, x_ref[...], vec_ref, vg_ref)

    row = lambda i: (i, 0)
    hrow = lambda i: (0, i, 0)
    return pl.pallas_call(
        body, name="mla_pre_bwd", grid=(S // tm,),
        in_specs=[pl.BlockSpec((tm, D), row), pl.BlockSpec((N_HEADS, tm, QPAD), hrow), pl.BlockSpec((tm, QPAD), row),
                  pl.BlockSpec((tm, KVL), row), pl.BlockSpec((tm, D), row), pl.BlockSpec((tm, QL), row),
                  pl.BlockSpec((tm, KVL), row), _w3((8, D)), pl.BlockSpec((tm, 128), row), pl.BlockSpec((tm, 128), row)]
        + _mla_weight_specs(),
        out_specs=[pl.BlockSpec((tm, D), row), pl.BlockSpec((tm, DLAT), row), pl.BlockSpec((tm, LAT_ALL), row),
                   pl.BlockSpec((tm, DQ_ALL), row), _w3((8, D)), _w3((8, QL))],
        out_shape=[jax.ShapeDtypeStruct((S, D), F32), jax.ShapeDtypeStruct((S, DLAT), BF16),
                   jax.ShapeDtypeStruct((S, LAT_ALL), BF16), jax.ShapeDtypeStruct((S, DQ_ALL), BF16),
                   jax.ShapeDtypeStruct((8, D), F32), jax.ShapeDtypeStruct((8, QL), F32)],
        compiler_params=_params("arbitrary"),
    )(dout, dq, dk, dv, x, cq_raw, ckv_raw, vec, tabs[0], tabs[1], *[mw[k] for k in MLA_WEIGHTS])


def mla_dw(h, dlat, cqn, dqcat, dql, qnope, olat, docat, ocat, du):
    S = h.shape[0]
    tk = min(DW_TK, S)
    nk = S // tk
    flat = lambda w: pl.BlockSpec((tk, w), lambda k: (k, 0))
    cols = lambda w: pl.BlockSpec((tk, w), lambda n, k: (k, n))
    pair_o = pl.BlockSpec((None, 2 * KVL, 128), lambda n, k: (n, 0, 0))
    g = {}
    g["in"] = dw_matmul("mla_dw_in", h, dlat, flat(D), flat(DLAT), (D, DLAT),
                        pl.BlockSpec((D, DLAT), lambda k: (0, 0)), (nk,))
    g["q"] = dw_matmul("mla_dw_q", cqn, dqcat, flat(QL), flat(DQ_ALL), (QL, DQ_ALL),
                       pl.BlockSpec((QL, DQ_ALL), lambda k: (0, 0)), (nk,))
    g["uk"] = dw_matmul("mla_dw_uk", dql, qnope, cols(2 * KVL), cols(2 * NOPE), (N_PAIR, 2 * KVL, 2 * NOPE), pair_o,
                        (N_PAIR, nk))
    g["uv"] = dw_matmul("mla_dw_uv", olat, docat, cols(2 * KVL), cols(2 * VH), (N_PAIR, 2 * KVL, 2 * VH), pair_o,
                        (N_PAIR, nk))
    g["o"] = dw_matmul("mla_dw_o", ocat, du, cols(256), pl.BlockSpec((tk, D), lambda n, k: (k, 0)), (D, D),
                       pl.BlockSpec((256, D), lambda n, k: (n, 0)), (D // 256, nk))
    return g


def loss_head(y, target):
    S = y.shape[0]
    tm = min(512, S)

    def body(y_ref, t_ref, loss_ref, dy_ref):
        @pl.when(pl.program_id(0) == 0)
        def _():
            loss_ref[...] = jnp.zeros_like(loss_ref)

        err = y_ref[...] - t_ref[...]
        dy_ref[...] = err * (1.0 / D)
        loss_ref[...] += 0.5 * jnp.sum(jnp.mean(err * err, axis=-1, keepdims=True), axis=0, keepdims=True)

    row = lambda i: (i, 0)
    return pl.pallas_call(
        body, name="loss_head", grid=(S // tm,),
        in_specs=[pl.BlockSpec((tm, D), row), pl.BlockSpec((tm, D), row)],
        out_specs=[pl.BlockSpec((1, 1), lambda i: (0, 0)), pl.BlockSpec((tm, D), row)],
        out_shape=[jax.ShapeDtypeStruct((1, 1), F32), jax.ShapeDtypeStruct((S, D), F32)],
        compiler_params=_params("arbitrary"),
    )(y, target)


MOD_COLS = 9 * D // N_CHIP


def mod_fwd(c_pad, ada_w, ada_b_loc):
    tn = MOD_COLS // 3

    def body(c_ref, w_ref, b_ref, o_ref):
        c = c_ref[...]
        sc = (c * jax.nn.sigmoid(c)).astype(BF16)
        o_ref[...] = _dot(sc, w_ref[...].astype(BF16)) + b_ref[...]

    return pl.pallas_call(
        body, name="mod_fwd", grid=(2, 3),
        in_specs=[pl.BlockSpec((16, D), lambda i, n: (0, 0)), pl.BlockSpec((None, D, tn), lambda i, n: (i, 0, n)),
                  pl.BlockSpec((None, 1, tn), lambda i, n: (i, 0, n))],
        out_specs=pl.BlockSpec((None, 16, tn), lambda i, n: (i, 0, n)),
        out_shape=jax.ShapeDtypeStruct((2, 16, MOD_COLS), F32),
        compiler_params=_params("parallel", "parallel"),
    )(c_pad, ada_w, ada_b_loc)


def _adamw_math(w, g, m, v):
    m = ADAM_B1 * m + (1.0 - ADAM_B1) * g
    v = ADAM_B2 * v + (1.0 - ADAM_B2) * (g * g)
    m_hat = m / (1.0 - ADAM_B1 ** ADAM_STEP)
    v_hat = v / (1.0 - ADAM_B2 ** ADAM_STEP)
    delta = -ADAM_LR * (m_hat / (jnp.sqrt(v_hat) + ADAM_EPS) + ADAM_WD * w)
    return delta, m, v


def adamw(name, w, g, m, v, part=None, prev=None):
    shape = w.shape
    cols = shape[-1]
    rows = w.size // cols
    per_entry = rows // shape[0] if part is not None else rows
    tr = per_entry
    budget_rows = (2 << 20) // (cols * 4)
    for cand in range(min(per_entry, budget_rows) // 8 * 8, 0, -8):
        if per_entry % cand == 0:
            tr = cand
            break
    first, count = part if part is not None else (0, 1)
    tiles = per_entry // tr

    def body(w_ref, g_ref, m_ref, v_ref, *rest):
        d_ref, mo_ref, vo_ref = rest[-3:]
        d_ref[...], mo_ref[...], vo_ref[...] = _adamw_math(w_ref[...], g_ref[...], m_ref[...], v_ref[...])

    spec = pl.BlockSpec((tr, cols), lambda i: (i + first * tiles, 0))
    operands = [a.reshape(rows, cols) for a in (w, g, m, v)]
    aliases = {}
    if prev is not None:
        operands += [p.reshape(rows, cols) for p in prev]
        aliases = {4: 0, 5: 1, 6: 2}
    outs = pl.pallas_call(
        body, name=name, grid=(count * tiles,), in_specs=[spec] * 4 + [_ANY] * (len(operands) - 4),
        out_specs=[spec] * 3, out_shape=[jax.ShapeDtypeStruct((rows, cols), F32)] * 3,
        input_output_aliases=aliases, compiler_params=_params("parallel"),
    )(*operands)
    return [o.reshape(shape) for o in outs]


def adamw_ada(c_pad, dmod, w, m, v):
    tr = 256

    def body(c_ref, dm_ref, w_ref, m_ref, v_ref, g_ref, d_ref, mo_ref, vo_ref):
        c = c_ref[...]
        sc = (c * jax.nn.sigmoid(c)).astype(BF16)
        g = _dot(sc, dm_ref[...].astype(BF16), TN)
        g_ref[...] = g
        d_ref[...], mo_ref[...], vo_ref[...] = _adamw_math(w_ref[...], g, m_ref[...], v_ref[...])

    wspec = pl.BlockSpec((None, tr, MOD_COLS), lambda i, r: (i, r, 0))
    return pl.pallas_call(
        body, name="adamw_ada", grid=(2, D // tr),
        in_specs=[pl.BlockSpec((16, tr), lambda i, r: (0, r)),
                  pl.BlockSpec((None, 16, MOD_COLS), lambda i, r: (i, 0, 0)), wspec, wspec, wspec],
        out_specs=[wspec] * 4,
        out_shape=[jax.ShapeDtypeStruct((2, D, MOD_COLS), F32)] * 4,
        compiler_params=_params("parallel", "parallel"),
    )(c_pad, dmod, w, m, v)


def sum_devices(name, a):
    _, R, C = a.shape
    tr = R
    for cand in (64, 32, 16, 8):
        if R % cand == 0:
            tr = cand
            break

    def body(a_ref, o_ref):
        acc = a_ref[0]
        for dev in range(1, N_DEV):
            acc = acc + a_ref[dev]
        o_ref[...] = acc

    return pl.pallas_call(
        body, name=name, grid=(R // tr,),
        in_specs=[pl.BlockSpec((N_DEV, tr, C), lambda i: (0, i, 0))],
        out_specs=pl.BlockSpec((tr, C), lambda i: (i, 0)),
        out_shape=jax.ShapeDtypeStruct((R, C), F32),
        compiler_params=_params("parallel"),
    )(a)


def _place():
    return lax.axis_index("x"), lax.axis_index("y"), lax.axis_index("c")


def _other_chips(x, y):
    return [(1 - x, y), (x, 1 - y), (1 - x, 1 - y)]


def gather_devices(name, a):
    m_per, n = a.shape

    def body(x_ref, out_ref, send_sems, recv_sems, local_sem):
        x, y, c = _place()
        me, sibling = (x, y, c), (x, y, 1 - c)
        chips = _other_chips(x, y)

        def rows(px, py, pc):
            return out_ref.at[pl.ds((4 * px + 2 * py + pc) * m_per, m_per), :]

        def copy(k, block, to, src=None):
            return pltpu.make_async_remote_copy(
                src_ref=rows(*block) if src is None else src, dst_ref=rows(*block),
                send_sem=send_sems.at[k], recv_sem=recv_sems.at[k], device_id=to, device_id_type=MESH)

        mine = pltpu.make_async_copy(x_ref, rows(*me), local_sem)
        mine.start()
        first = [copy(0, me, sibling, src=x_ref)]
        first += [copy(1 + j, me, (*chip, c), src=x_ref) for j, chip in enumerate(chips)]
        for cp in first:
            cp.start()
        passed = [copy(4 + j, (*chip, c), sibling) for j, chip in enumerate(chips)]
        for j, chip in enumerate(chips):
            copy(1 + j, (*chip, c), me).wait_recv()
            passed[j].start()
        copy(0, sibling, me).wait_recv()
        for j, chip in enumerate(chips):
            copy(4 + j, (*chip, 1 - c), me).wait_recv()
        for cp in first + passed:
            cp.wait_send()
        mine.wait()

    out = pl.pallas_call(
        body, name=name,
        out_shape=jax.ShapeDtypeStruct((N_DEV * m_per, n), a.dtype),
        in_specs=[pl.BlockSpec(memory_space=pltpu.VMEM)],
        out_specs=pl.BlockSpec(memory_space=pltpu.VMEM),
        scratch_shapes=[pltpu.SemaphoreType.DMA((7,)), pltpu.SemaphoreType.DMA((7,)), pltpu.SemaphoreType.DMA],
        compiler_params=pltpu.CompilerParams(vmem_limit_bytes=VMEM_LIMIT),
    )(a)
    return out.reshape(N_DEV, m_per, n)


_ANY = pl.BlockSpec(memory_space=pl.ANY)


def _hbm_ref(a):
    return jax.new_ref(a, memory_space=pltpu.MemorySpace.HBM)


def _hbm_empty(shape, dtype):
    return jax.empty_ref(jax.ShapeDtypeStruct(shape, dtype), memory_space=pltpu.MemorySpace.HBM)


ID_PAIR, ID_CHIPS, ID_SHARE, ID_UKV = 8, 9, 10, 11


def _sequencer(name, collective_id, n_sem, peers_of, program):
    sems = pltpu.SemaphoreType.DMA((n_sem,))

    @pl.kernel(mesh=plsc.ScalarSubcoreMesh(axis_name="seq", num_cores=1), name=name, scratch_types=[sems, sems],
               compiler_params=pltpu.CompilerParams(collective_id=collective_id))
    def launch(send_sem, recv_sem):
        x, y, c = _place()
        peers = peers_of(x, y, c)
        barrier = pltpu.get_barrier_semaphore()
        for peer in peers:
            pl.semaphore_signal(barrier, inc=1, device_id=peer, device_id_type=MESH)
        pl.semaphore_wait(barrier, len(peers))
        program(x, y, c, send_sem, recv_sem)

    launch()


def gather_weights(name, stage, arrays):
    n = len(arrays)
    refs = [_hbm_ref(a) for a in arrays]

    def program(x, y, c, send_sem, recv_sem):
        me = 2 * x + y
        chips = _other_chips(x, y)

        def ici(t, r, half):
            cx, cy = chips[r]
            mine = refs[t].at[me, half]
            return pltpu.make_async_remote_copy(
                src_ref=mine, dst_ref=mine, send_sem=send_sem.at[3 * t + r], recv_sem=recv_sem.at[3 * t + r],
                device_id=(cx, cy, c), device_id_type=MESH)

        def d2d(t, r, half):
            cx, cy = chips[r]
            there = refs[t].at[2 * cx + cy, half]
            k = 3 * n + 3 * t + r
            return pltpu.make_async_remote_copy(
                src_ref=there, dst_ref=there, send_sem=send_sem.at[k], recv_sem=recv_sem.at[k],
                device_id=(x, y, 1 - c), device_id_type=MESH)

        for t in range(n):
            for r in range(3):
                ici(t, r, c).start()
        for t in range(n):
            for r in range(3):
                ici(t, r, c).wait_recv()
                d2d(t, r, c).start()
        for t in range(n):
            for r in range(3):
                d2d(t, r, 1 - c).wait_recv()
        for t in range(n):
            for r in range(3):
                ici(t, r, c).wait_send()
                d2d(t, r, c).wait_send()

    _sequencer(name, stage, 6 * n, lambda x, y, c: [(x, y, 1 - c)] + [(cx, cy, c) for cx, cy in _other_chips(x, y)],
               program)
    return [r[...] for r in refs]


def cast_into_slots(name, chip, shards, after=None):
    steps = 2
    n = len(shards)

    def body(chip_ref, *refs):
        for src, dst in zip(refs[:n], refs[-n - 1:-1]):
            dst[...] = src[...].astype(BF16)
        refs[-1][...] = jnp.zeros_like(refs[-1])

    token_spec = pl.BlockSpec((8, 128), lambda h, i, chip_ref: (0, 0))

    def spec_in(a, prefix):
        R, C = a.shape[-2:]
        return pl.BlockSpec((None,) * (len(prefix) + 1) + (R // steps, C), lambda h, i, chip_ref: prefix + (h, i, 0))

    def spec_out(a):
        R, C = a.shape[-2:]
        return pl.BlockSpec((None, None, R // steps, C), lambda h, i, chip_ref: (chip_ref[0], h, i, 0))

    outs = pl.pallas_call(
        body, name=name,
        grid_spec=pltpu.PrefetchScalarGridSpec(
            num_scalar_prefetch=1, grid=(2, steps),
            in_specs=[spec_in(a, p) for a, p in shards] + ([token_spec] if after is not None else []),
            out_specs=[spec_out(a) for a, _ in shards] + [token_spec]),
        out_shape=[jax.ShapeDtypeStruct((N_CHIP, 2) + a.shape[-2:], BF16) for a, _ in shards]
        + [jax.ShapeDtypeStruct((8, 128), F32)],
        compiler_params=_params("arbitrary", "arbitrary"),
    )(chip, *[a for a, _ in shards], *([after] if after is not None else []))
    return outs[:-1], outs[-1]


def reduce_pair(name, grads):
    n = len(grads)
    src = [_hbm_ref(g) for g in grads]
    dst = [_hbm_empty((N_CHIP,) + g.shape[2:], g.dtype) for g in grads]

    def program(x, y, c, send_sem, recv_sem):
        cps = [pltpu.make_async_remote_copy(
            src_ref=src[t].at[:, 1 - c], dst_ref=dst[t], send_sem=send_sem.at[t], recv_sem=recv_sem.at[t],
            device_id=(x, y, 1 - c), device_id_type=MESH) for t in range(n)]
        for cp in cps:
            cp.start()
        for cp in cps:
            cp.wait()

    _sequencer(name, ID_PAIR, n, lambda x, y, c: [(x, y, 1 - c)], program)
    return [r[...] for r in src], [r[...] for r in dst]


def pair_add(name, core, g, got):
    _, _, R, C = g.shape

    def body(core_ref, g_ref, got_ref, o_ref):
        o_ref[...] = (g_ref[...] + got_ref[...]).astype(BF16)

    return pl.pallas_call(
        body, name=name,
        grid_spec=pltpu.PrefetchScalarGridSpec(
            num_scalar_prefetch=1, grid=(N_CHIP,),
            in_specs=[pl.BlockSpec((None, None, R, C), lambda q, core_ref: (q, core_ref[0], 0, 0)),
                      pl.BlockSpec((None, R, C), lambda q, core_ref: (q, 0, 0))],
            out_specs=pl.BlockSpec((None, R, C), lambda q, core_ref: (q, 0, 0))),
        out_shape=jax.ShapeDtypeStruct((N_CHIP, R, C), BF16),
        compiler_params=_params("parallel"),
    )(core, g, got)


def reduce_chips(name, sums):
    n = len(sums)
    src = [_hbm_ref(s) for s in sums]
    dst = [_hbm_empty((3,) + s.shape[1:], s.dtype) for s in sums]

    def program(x, y, c, send_sem, recv_sem):
        cps = []
        for t in range(n):
            for r, (cx, cy) in enumerate(_other_chips(x, y)):
                cps.append(pltpu.make_async_remote_copy(
                    src_ref=src[t].at[2 * cx + cy], dst_ref=dst[t].at[r],
                    send_sem=send_sem.at[3 * t + r], recv_sem=recv_sem.at[3 * t + r],
                    device_id=(cx, cy, c), device_id_type=MESH))
        for cp in cps:
            cp.start()
        for cp in cps:
            cp.wait()

    _sequencer(name, ID_CHIPS, 3 * n, lambda x, y, c: [(cx, cy, c) for cx, cy in _other_chips(x, y)], program)
    return [r[...] for r in src], [r[...] for r in dst]


def chip_add(name, place, s, got, k, n_slots, prev=None, after=None):
    _, R, C = s.shape

    def body(place_ref, s_ref, got_ref, *rest):
        o_ref = rest[-1]
        o_ref[...] = ((s_ref[...].astype(F32) + got_ref[0].astype(F32)) + got_ref[1].astype(F32)) + got_ref[2].astype(F32)

    in_specs = [pl.BlockSpec((None, R, C), lambda i, place_ref: (place_ref[0], 0, 0)),
                pl.BlockSpec((3, R, C), lambda i, place_ref: (0, 0, 0))]
    args = [place, s, got]
    aliases = {}
    if prev is not None:
        in_specs.append(_ANY)
        args.append(prev)
        aliases = {3: 0}
    if after is not None:
        in_specs.append(pl.BlockSpec((8, 128), lambda i, place_ref: (0, 0)))
        args.append(after)
    return pl.pallas_call(
        body, name=name,
        grid_spec=pltpu.PrefetchScalarGridSpec(
            num_scalar_prefetch=1, grid=(1,), in_specs=in_specs,
            out_specs=pl.BlockSpec((None, None, R, C), lambda i, place_ref: (k, place_ref[1], 0, 0))),
        out_shape=jax.ShapeDtypeStruct((n_slots, 2, R, C), F32),
        input_output_aliases=aliases,
        compiler_params=_params("arbitrary"),
    )(*args)


def share_halves(name, stacks, slots):
    n = len(stacks)
    dst = [_hbm_ref(s) for s in stacks]

    def program(x, y, c, send_sem, recv_sem):
        cps = [pltpu.make_async_remote_copy(
            src_ref=dst[t].at[slots[t], c], dst_ref=dst[t].at[slots[t], c],
            send_sem=send_sem.at[t], recv_sem=recv_sem.at[t],
            device_id=(x, y, 1 - c), device_id_type=MESH) for t in range(n)]
        for cp in cps:
            cp.start()
        for cp in cps:
            cp.wait()

    _sequencer(name, ID_SHARE, n, lambda x, y, c: [(x, y, 1 - c)], program)
    return [r[...] for r in dst]


def gather_blocks(name, slotted):
    out = _hbm_ref(slotted)

    def program(x, y, c, send_sem, recv_sem):
        sibling = (x, y, 1 - c)
        chips = _other_chips(x, y)

        def copy(k, px, py, pc, to):
            block = out.at[4 * px + 2 * py + pc]
            return pltpu.make_async_remote_copy(src_ref=block, dst_ref=block, send_sem=send_sem.at[k],
                                                recv_sem=recv_sem.at[k], device_id=to, device_id_type=MESH)

        first = [copy(0, x, y, c, sibling)] + [copy(1 + j, x, y, c, (cx, cy, c)) for j, (cx, cy) in enumerate(chips)]
        for cp in first:
            cp.start()
        passed = [copy(4 + j, cx, cy, c, sibling) for j, (cx, cy) in enumerate(chips)]
        for j, (cx, cy) in enumerate(chips):
            copy(1 + j, cx, cy, c, (x, y, c)).wait_recv()
            passed[j].start()
        copy(0, x, y, 1 - c, (x, y, c)).wait_recv()
        for j, (cx, cy) in enumerate(chips):
            copy(4 + j, cx, cy, 1 - c, (x, y, c)).wait_recv()
        for cp in first + passed:
            cp.wait_send()

    _sequencer(name, ID_UKV, 7, lambda x, y, c: [(x, y, 1 - c)] + [(cx, cy, c) for cx, cy in _other_chips(x, y)],
               program)
    return out[...]


def place_block(name, dev, a):
    M, N = a.shape
    tr = min(M, 64)

    def body(dev_ref, a_ref, o_ref):
        o_ref[...] = a_ref[...]

    return pl.pallas_call(
        body, name=name,
        grid_spec=pltpu.PrefetchScalarGridSpec(
            num_scalar_prefetch=1, grid=(M // tr,),
            in_specs=[pl.BlockSpec((tr, N), lambda i, dev_ref: (i, 0))],
            out_specs=pl.BlockSpec((None, tr, N), lambda i, dev_ref: (dev_ref[0], i, 0))),
        out_shape=jax.ShapeDtypeStruct((N_DEV, M, N), a.dtype),
        compiler_params=_params("parallel"),
    )(dev, a)


def _swap_rope(a):
    return jnp.concatenate([a[..., ROPE // 2:], a[..., :ROPE // 2]], axis=-1)


def _rope_tables(S):
    inv = 1.0 / (ROPE_THETA ** (jnp.arange(0, ROPE, 2, dtype=F32) / ROPE))
    ang = jnp.arange(S, dtype=F32)[:, None] * inv[None, :]
    cos, sin = jnp.cos(ang), jnp.sin(ang)
    return (jnp.tile(jnp.concatenate([cos, cos], axis=1), (1, SLOTS)),
            jnp.tile(jnp.concatenate([-sin, sin], axis=1), (1, SLOTS)))


def _vec(norm_g, mod, i, k):
    rows = [norm_g[i, 2 * k], norm_g[i, 2 * k + 1], mod[i, 3 * k], mod[i, 3 * k + 1], mod[i, 3 * k + 2]]
    return jnp.concatenate([jnp.stack(rows), jnp.zeros((3, D), F32)], axis=0)


def _unpack_weights(full, w_uk, w_uv, q_norm, kv_norm):
    G = D // 4
    ffn_in = [[full[2 * i + k].reshape(N_CHIP, D, FSH) for k in range(2)] for i in range(2)]
    ffn_out = [[full[4 + 2 * i + k].reshape(2, FSH, D) for k in range(2)] for i in range(2)]
    pw = full[8].reshape(N_CHIP, 4, G // N_CHIP, G).transpose(1, 0, 2, 3).reshape(4, G, G)
    w_in = full[9].reshape(D, QL + KVL + ROPE)
    w_uq = full[10].reshape(QL, N_HEADS, NOPE + ROPE)
    wkr = w_in[:, QL + KVL:]
    wr = w_uq[:, :, NOPE:]
    eye2 = jnp.eye(2, dtype=BF16)
    uk_t = jnp.transpose(w_uk, (1, 2, 0)).reshape(N_PAIR, 2, NOPE, KVL)
    bduk = jnp.einsum("janc,ab->janbc", uk_t, eye2).reshape(N_PAIR, 2 * NOPE, 2 * KVL)
    uv = jnp.transpose(w_uv, (1, 0, 2)).reshape(N_PAIR, 2, KVL, VH)
    bduv = jnp.einsum("jacn,ab->jacbn", uv, eye2).reshape(N_PAIR, 2 * KVL, 2 * VH)
    mw = dict(wq=w_in[:, :QL], wkv=w_in[:, QL:QL + KVL], wkr4=jnp.tile(wkr, (1, SLOTS)),
              wkrs4=jnp.tile(_swap_rope(wkr), (1, SLOTS)), qn=q_norm, kvn=kv_norm,
              wn=w_uq[:, :, :NOPE].reshape(QL, NOPE_ALL), wr=wr.reshape(QL, ROPE_ALL),
              wrs=_swap_rope(wr).reshape(QL, ROPE_ALL), bduk=bduk)
    return ffn_in, ffn_out, pw, mw, bduv, full[11].reshape(D, D)


def _example_step(x, target, mod, norm_g, pvec, ffn_in, ffn_out, pw, mw, bduv, wo, reducer):
    S = x.shape[0]
    tabs = _rope_tables(S)
    vec = [[_vec(norm_g, mod, i, k) for k in range(3)] for i in range(2)]
    saved = {}
    for i in range(2):
        xin = x
        x, a, u, h = ffn_fwd(xin, vec[i][0], ffn_in[i][0], ffn_out[i][0], 0.5)
        saved[i, 0] = (xin, a, u, h)
        xin = x
        if i == 0:
            x, y, z = pool_fwd(xin, vec[i][1], pw, pvec)
            saved[i, 1] = (xin, y, z)
        else:
            h_m, cq_raw, ckv_raw, cqn, qnope, qcat, kcat, vcat = mla_pre(xin, vec[i][1], mw, tabs)
            olat, lse = attn_fwd(qcat, kcat, vcat)
            x, u_m, ocat = mla_post(olat, xin, vec[i][1], bduv, wo)
            saved[i, 1] = (xin, h_m, cq_raw, ckv_raw, cqn, qnope, qcat, kcat, olat, lse, u_m, ocat)
        xin = x
        x, a, u, h = ffn_fwd(xin, vec[i][2], ffn_in[i][1], ffn_out[i][1], 0.5)
        saved[i, 2] = (xin, a, u, h)
    loss, dx = loss_head(x, target)

    vg = {}
    G = D // 4

    def ffn_grads(i, k, dw_in, dw_out):
        return [(0, 2 * i + k, 4, dw_in.reshape(N_CHIP, 2, D // 2, FSH)),
                (1, 2 * i + k, 4, dw_out.reshape(N_CHIP, 2, DFF // 8, D))]

    for i in (1, 0):
        xin, a, u, h = saved[i, 2]
        dx, du, act, da, vg[i, 2] = ffn_bwd(dx, xin, u, a, vec[i][2], ffn_in[i][1], ffn_out[i][1], 0.5)
        reducer.advance()
        reducer.add(f"f{i}1", ffn_grads(i, 1, *ffn_dw(h, da, act, du)))
        if i == 0:
            xin, y, z = saved[i, 1]
            dx, vg[i, 1], pgrad, g_pool = pool_bwd(dx, xin, y, z, vec[i][1], pw, pvec)
            reducer.advance()
        else:
            xin, h_m, cq_raw, ckv_raw, cqn, qnope, qcat, kcat, olat, lse, u_m, ocat = saved[i, 1]
            du, docat, dolat, delta, vg_post = mla_post_bwd(dx, u_m, olat, vec[i][1], bduv, wo)
            reducer.advance()
            dq, dk, dv = attn_bwd(qcat, kcat, kcat.T, dolat, lse, delta)
            reducer.advance()
            dx, dlat, dql, dqcat, vg_pre, ngrad = mla_pre_bwd(
                dx, dq, dk, dv, xin, cq_raw, ckv_raw, vec[i][1], mw, tabs)
            vg[i, 1] = vg_post + vg_pre
            g = mla_dw(h_m, dlat, cqn, dqcat, dql, qnope, olat, docat, ocat, du)
            slots = lambda a: a.reshape(D, SLOTS, ROPE).sum(axis=1)
            g_kr = slots(g["in"][:, QL + KVL:QL + KVL + 128]) + _swap_rope(slots(g["in"][:, QL + KVL + 128:]))
            g_in = jnp.concatenate([g["in"][:, :QL + KVL], g_kr], axis=1)
            g_r = g["q"][:, NOPE_ALL:NOPE_ALL + ROPE_ALL].reshape(QL, N_HEADS, ROPE)
            g_rs = g["q"][:, NOPE_ALL + ROPE_ALL:].reshape(QL, N_HEADS, ROPE)
            g_uq = jnp.concatenate([g["q"][:, :NOPE_ALL].reshape(QL, N_HEADS, NOPE), g_r + _swap_rope(g_rs)], axis=-1)

            def heads(pairs):
                blk = pairs.reshape(N_PAIR, 2, KVL, 2, NOPE)
                per_head = jnp.stack([blk[:, 0, :, 0, :], blk[:, 1, :, 1, :]], axis=1).reshape(N_HEADS, KVL, NOPE)
                return jnp.transpose(per_head, (1, 0, 2)).reshape(KVL, N_HEADS * NOPE)

            reducer.add("mla", [(3, 0, 1, g_in.reshape(N_CHIP, 2, D // 8, QL + KVL + ROPE)),
                                (4, 0, 1, g_uq.reshape(N_CHIP, 2, QL // 8, N_HEADS * (NOPE + ROPE))),
                                (5, 0, 1, g["o"].reshape(N_CHIP, 2, D // 8, D))])
            reducer.add_replicated(jnp.concatenate([heads(g["uk"]), heads(g["uv"])], axis=0))
        xin, a, u, h = saved[i, 0]
        dx, du, act, da, vg[i, 0] = ffn_bwd(dx, xin, u, a, vec[i][0], ffn_in[i][0], ffn_out[i][0], 0.5)
        if i == 1:
            reducer.advance()
        grads = ffn_grads(i, 0, *ffn_dw(h, da, act, du))
        if i == 0:
            grads.append((2, 0, 1, g_pool.reshape(N_CHIP, 2, 2 * G // N_CHIP, G)))
        reducer.add(f"f{i}0", grads)
    return loss, dx, vg, pgrad, ngrad


class _GradReducer:
    def __init__(self, core, place, dev):
        self.core, self.place, self.dev = core, place, dev
        self.stacks = {}
        self.live = []
        self.replicated = None

    def add(self, tag, items):
        gen = self._run(tag, items)
        next(gen)
        self.live.append(gen)

    def add_replicated(self, block):
        self.replicated = gather_blocks("gather_ukv", place_block("place_ukv", self.dev, block))

    def advance(self, after=None):
        self.after = after
        live = []
        for gen in self.live:
            try:
                next(gen)
                live.append(gen)
            except StopIteration:
                pass
        self.live = live

    def finish(self):
        while self.live:
            self.advance()
        return self.stacks, self.replicated

    def _run(self, tag, items):
        grads, from_pair = reduce_pair(f"reduce_pair_{tag}", [g for *_, g in items])
        yield
        sums = [pair_add(f"pair_add_{tag}_{j}", self.core, g, p) for j, (g, p) in enumerate(zip(grads, from_pair))]
        sums, from_chips = reduce_chips(f"reduce_chips_{tag}", sums)
        yield
        for j, ((o, k, n_slots, _), s, p) in enumerate(zip(items, sums, from_chips)):
            self.stacks[o] = chip_add(f"chip_add_{tag}_{j}", self.place, s, p, k, n_slots, self.stacks.get(o),
                                      self.after)
        shared = share_halves(f"share_halves_{tag}", [self.stacks[o] for o, *_ in items], [k for _, k, *_ in items])
        for (o, *_), v in zip(items, shared):
            self.stacks[o] = v


SMALL_IN = 8 * 640
SMALL_GRAD = 8 * 4224
SMALL_W = 8 * 2944


def _pack(parts, total):
    flat = jnp.concatenate([p.reshape(-1) for p in parts])
    return jnp.concatenate([flat, jnp.zeros((total - flat.shape[0],), F32)]).reshape(8, total // 8)


def kernel(x, c, ada_w, ada_b, norm_g, ffn_w_in, ffn_w_out, pool_w, pool_b, pool_scale, mla_w_in, mla_q_norm, mla_kv_norm, mla_w_uq, mla_w_uk, mla_w_uv, mla_w_o, loss_target, m_ada_w, m_ada_b, m_norm_g, m_ffn_w_in, m_ffn_w_out, m_pool_w, m_pool_b, m_pool_scale, m_mla_w_in, m_mla_q_norm, m_mla_kv_norm, m_mla_w_uq, m_mla_w_uk, m_mla_w_uv, m_mla_w_o, v_ada_w, v_ada_b, v_norm_g, v_ffn_w_in, v_ffn_w_out, v_pool_w, v_pool_b, v_pool_scale, v_mla_w_in, v_mla_q_norm, v_mla_kv_norm, v_mla_w_uq, v_mla_w_uk, v_mla_w_uv, v_mla_w_o):
    ix, iy, ic = _place()
    chip = 2 * ix + iy
    dev = 2 * chip + ic
    core_arr = ic.astype(jnp.int32).reshape(1)
    chip_arr = chip.astype(jnp.int32).reshape(1)
    S = x.shape[1]
    G = D // 4
    NG = D // N_CHIP

    def chip_cols(a, width, axis):
        return lax.dynamic_slice_in_dim(a, chip * width, width, axis)

    got = gather_devices("gather_small_in", _pack([c, norm_g, pool_b, mla_q_norm], SMALL_IN)).reshape(N_DEV, SMALL_IN)
    c_all = got[:, :D]
    parts = got[0::2]
    o = D
    norm_g_full = parts[:, o:o + 12 * NG].reshape(N_CHIP, 2, 6, NG).transpose(1, 2, 0, 3).reshape(2, 6, D)
    o += 12 * NG
    pool_b_full = parts[:, o:o + G].reshape(N_CHIP, 4, G // N_CHIP).transpose(1, 0, 2).reshape(1, D)
    o += G
    q_norm_full = parts[:, o:o + QL // N_CHIP].reshape(1, QL)
    pvec = jnp.concatenate([pool_b_full, pool_scale, jnp.zeros((6, D), F32)], axis=0)

    c_pad = jnp.concatenate([c_all, jnp.zeros((8, D), F32)], axis=0)
    mod_loc = mod_fwd(c_pad, ada_w, chip_cols(ada_b, MOD_COLS, 1).reshape(2, 1, MOD_COLS))
    got = gather_devices("gather_mod", mod_loc[:, :8].transpose(1, 0, 2).reshape(8, 2 * MOD_COLS))
    mine = lax.dynamic_index_in_dim(got[0::2].reshape(N_CHIP, 8, 2, MOD_COLS), dev, axis=1, keepdims=False)
    mod = mine.transpose(1, 0, 2).reshape(2, 9, D)

    bf = lambda a: a.astype(BF16)
    w_in_halves = ffn_w_in.reshape(2, 2, 2, D // 2, FSH)
    w_out_halves = ffn_w_out.reshape(2, 2, 2, DFF // 8, D)
    shards = [(w_in_halves, (i, k)) for i in range(2) for k in range(2)]
    shards += [(w_out_halves, (i, k)) for i in range(2) for k in range(2)]
    shards += [(pool_w.reshape(2, 2 * G // N_CHIP, G), ()), (mla_w_in.reshape(2, D // 8, QL + KVL + ROPE), ()),
               (mla_w_uq.reshape(2, QL // 8, N_HEADS * (NOPE + ROPE)), ()), (mla_w_o.reshape(2, D // 8, D), ())]
    full = [None] * len(shards)
    stages = [(0, 4, 8), (1, 5), (2, 6), (9, 10, 11), (3, 7)]
    first, token = cast_into_slots("cast_first", chip_arr, [shards[t] for t in stages[0]])
    slotted = dict(zip(stages[0], first))
    rest = [t for members in stages[1:] for t in members]
    for stage, members in enumerate(stages):
        got_w = gather_weights(f"gather_weights_{stage}", stage, [slotted[t] for t in members])
        for t, a in zip(members, got_w):
            full[t] = a
        if stage == 0:
            slotted.update(zip(rest, cast_into_slots("cast_rest", chip_arr, [shards[t] for t in rest], token)[0]))
    ffn_in, ffn_out, pw, mw, bduv, wo = _unpack_weights(full, bf(mla_w_uk[0]), bf(mla_w_uv[0]), q_norm_full,
                                                        mla_kv_norm)

    place_arr = jnp.stack([chip, ic]).astype(jnp.int32)
    reducer = _GradReducer(core_arr, place_arr, dev.astype(jnp.int32).reshape(1))
    loss_mine, grad_x, vg, pgrad, ngrad = _example_step(
        x[0], loss_target[0], mod, norm_g_full, pvec, ffn_in, ffn_out, pw, mw, bduv, wo, reducer)

    dmod = jnp.stack([jnp.concatenate([vg[i, k][0:3] for k in range(3)]) for i in range(2)])
    dnorm = jnp.stack([jnp.concatenate([vg[i, k][3:5] for k in range(3)]) for i in range(2)])
    small = _pack([dmod, dnorm, pgrad[0], pgrad[1], ngrad[0], ngrad[1, :KVL], loss_mine], SMALL_GRAD)
    got = gather_devices("gather_small_grad", small)
    tot = sum_devices("sum_small_grad", got).reshape(-1)
    n_mod = 2 * 9 * D
    g_ada_b = tot[:n_mod].reshape(ada_b.shape)
    o = n_mod
    g_norm = chip_cols(tot[o:o + 12 * D].reshape(2, 6, D), NG, 2)
    o += 12 * D
    g_pool_b = chip_cols(tot[o:o + D].reshape(1, 4, G), G // N_CHIP, 2)
    o += D
    g_pool_scale = tot[o:o + D].reshape(pool_scale.shape)
    o += D
    g_q_norm = chip_cols(tot[o:o + QL].reshape(1, QL), QL // N_CHIP, 1)
    o += QL
    g_kv_norm = tot[o:o + KVL].reshape(mla_kv_norm.shape)
    loss = tot[o + KVL]
    dmod_all = chip_cols(got.reshape(N_DEV, -1)[:, :n_mod].reshape(N_DEV, 2, 9 * D), MOD_COLS, 2)
    dmod_pad = jnp.concatenate([dmod_all.transpose(1, 0, 2), jnp.zeros((2, 8, MOD_COLS), F32)], axis=1)

    g_ada_w, d_ada_w, nm_ada_w, nv_ada_w = adamw_ada(c_pad, dmod_pad, ada_w, m_ada_w, v_ada_w)
    small_names = ["ada_b", "norm_g", "pool_b", "pool_scale", "mla_q_norm", "mla_kv_norm"]
    small_w = [ada_b, norm_g, pool_b, pool_scale, mla_q_norm, mla_kv_norm]
    small_g = [g_ada_b, g_norm, g_pool_b, g_pool_scale, g_q_norm, g_kv_norm]
    small_m = [m_ada_b, m_norm_g, m_pool_b, m_pool_scale, m_mla_q_norm, m_mla_kv_norm]
    small_v = [v_ada_b, v_norm_g, v_pool_b, v_pool_scale, v_mla_q_norm, v_mla_kv_norm]
    packed = adamw("adamw_small", *[_pack(p, SMALL_W) for p in (small_w, small_g, small_m, small_v)])
    upd = {}
    o = 0
    for name, w in zip(small_names, small_w):
        upd[name] = [p.reshape(-1)[o:o + w.size].reshape(w.shape) for p in packed]
        o += w.size
    upd["ada_w"] = [d_ada_w, nm_ada_w, nv_ada_w]

    reducer.advance(after=d_ada_w[0, :8, :128])
    ukv = sum_devices("sum_ukv", reducer.replicated)
    g_uk = ukv[:KVL].reshape(mla_w_uk.shape)
    g_uv = ukv[KVL:].reshape(mla_w_uv.shape)
    g_mla_in = reducer.stacks[3].reshape(mla_w_in.shape)
    g_uq = reducer.stacks[4].reshape(mla_w_uq.shape)
    g_wo = reducer.stacks[5].reshape(mla_w_o.shape)
    for name, w, g, m, v in [("mla_w_in", mla_w_in, g_mla_in, m_mla_w_in, v_mla_w_in),
                             ("mla_w_uq", mla_w_uq, g_uq, m_mla_w_uq, v_mla_w_uq),
                             ("mla_w_uk", mla_w_uk, g_uk, m_mla_w_uk, v_mla_w_uk),
                             ("mla_w_uv", mla_w_uv, g_uv, m_mla_w_uv, v_mla_w_uv),
                             ("mla_w_o", mla_w_o, g_wo, m_mla_w_o, v_mla_w_o)]:
        upd[name] = adamw("adamw_" + name, w, g, m, v)
    ffn = [("ffn_w_in", 0, ffn_w_in, m_ffn_w_in, v_ffn_w_in), ("ffn_w_out", 1, ffn_w_out, m_ffn_w_out, v_ffn_w_out)]
    slots = lambda a: a.reshape((4,) + a.shape[2:])
    early = {name: adamw(f"adamw_{name}_early", slots(w), slots(reducer.stacks[o].reshape(w.shape)), slots(m),
                         slots(v), part=(1, 3)) for name, o, w, m, v in ffn}

    reducer.advance(after=early["ffn_w_out"][0][1, :8, :128])
    stacks, _ = reducer.finish()
    g_ffn_in = stacks[0].reshape(ffn_w_in.shape)
    g_ffn_out = stacks[1].reshape(ffn_w_out.shape)
    g_pool_w = stacks[2].reshape(pool_w.shape)
    for name, o, w, m, v in ffn:
        done = adamw(f"adamw_{name}_last", slots(w), slots(stacks[o].reshape(w.shape)), slots(m), slots(v),
                     part=(0, 1), prev=early[name])
        upd[name] = [p.reshape(w.shape) for p in done]
    upd["pool_w"] = adamw("adamw_pool_w", pool_w, g_pool_w, m_pool_w, v_pool_w)

    order = ["ada_w", "ada_b", "norm_g", "ffn_w_in", "ffn_w_out", "pool_w", "pool_b", "pool_scale", "mla_w_in",
             "mla_q_norm", "mla_kv_norm", "mla_w_uq", "mla_w_uk", "mla_w_uv", "mla_w_o"]
    grad = dict(ada_w=g_ada_w, ada_b=g_ada_b, norm_g=g_norm, ffn_w_in=g_ffn_in, ffn_w_out=g_ffn_out, pool_w=g_pool_w,
                pool_b=g_pool_b, pool_scale=g_pool_scale, mla_w_in=g_mla_in, mla_q_norm=g_q_norm,
                mla_kv_norm=g_kv_norm, mla_w_uq=g_uq, mla_w_uk=g_uk, mla_w_uv=g_uv, mla_w_o=g_wo)
    return (loss, grad_x[None], *[grad[n] for n in order], *[upd[n][0] for n in order],
            *[upd[n][1] for n in order], *[upd[n][2] for n in order])
```

```python
import functools

import jax
import jax.numpy as jnp
from jax import lax
from jax.experimental import pallas as pl
from jax.experimental.pallas import tpu as pltpu
from jax.experimental.pallas import tpu_sc as plsc

F32 = jnp.float32
BF16 = jnp.bfloat16

D = 1024
DFF = 2816
FSH = 1408
N_CHIP = 4
N_DEV = 8
N_HEADS = 16
NOPE = 64
ROPE = 32
VH = 64
QL = 256
KVL = 128
QPAD = 256
EPS = 1e-6
ATTN_SCALE = (NOPE + ROPE) ** -0.5
ROPE_THETA = 10000.0
POOL_WINDOWS = (2, 4, 8, 16)
HALO = 8
ATTN_TQ = 1024
ATTN_KC = 512
ROW_TILE = 512
DW_TK = 2048

ADAM_LR, ADAM_B1, ADAM_B2, ADAM_EPS, ADAM_WD, ADAM_STEP = 0.001, 0.9, 0.999, 1e-08, 0.01, 10

VMEM_LIMIT = 60 * 1024 * 1024
MESH = pl.DeviceIdType.MESH

NT = (((1,), (1,)), ((), ()))
TN = (((0,), (0,)), ((), ()))


def _params(*sem):
    return pltpu.CompilerParams(dimension_semantics=sem, vmem_limit_bytes=VMEM_LIMIT)


def _dot(a, b, dims=None):
    if dims is None:
        return jnp.dot(a, b, preferred_element_type=F32)
    return lax.dot_general(a, b, dims, preferred_element_type=F32)


def _rms(x):
    r = lax.rsqrt(jnp.mean(x * x, axis=-1, keepdims=True) + EPS)
    return x * r, r


def _rms_bwd(xhat, r, dxhat):
    return r * (dxhat - xhat * jnp.mean(dxhat * xhat, axis=-1, keepdims=True))


def _as_row(col):
    return jnp.broadcast_to(col, (col.shape[0], 128)).T[0:1, :]


def _prenorm(x, vec_ref):
    xhat, r = _rms(x)
    h = xhat * vec_ref[0:1, :] * (1.0 + vec_ref[3:4, :]) + vec_ref[2:3, :]
    return h, xhat, r


def _postnorm_bwd(dout, u, vec_ref, weight):
    uhat, r = _rms(u)
    gt = weight * (1.0 + vec_ref[4:5, :])
    dy = dout * gt
    dgate_rows = (weight * dout) * (uhat * vec_ref[1:2, :])
    dgpost_rows = dy * uhat
    du = _rms_bwd(uhat, r, dy * vec_ref[1:2, :])
    return du, dgate_rows, dgpost_rows


def _prenorm_bwd(dh, x, vec_ref, vg_ref):
    xhat, r = _rms(x)
    sc1 = 1.0 + vec_ref[3:4, :]
    g = vec_ref[0:1, :]
    vg_ref[0:1, :] += jnp.sum(dh, axis=0, keepdims=True)
    vg_ref[1:2, :] += jnp.sum(dh * (xhat * g), axis=0, keepdims=True)
    vg_ref[3:4, :] += jnp.sum(dh * sc1 * xhat, axis=0, keepdims=True)
    return _rms_bwd(xhat, r, dh * g * sc1)


def ffn_fwd(x, vec, w_in, w_out, weight):
    S = x.shape[0]
    tm = min(512, S)
    row = lambda i: (i, 0)
    half = lambda j: [_w3((8, D)), pl.BlockSpec((None, D, FSH), lambda i: (j, 0, 0)),
                      pl.BlockSpec((None, D, FSH), lambda i: (j + 2, 0, 0)),
                      pl.BlockSpec((None, FSH, D), lambda i: (j, 0, 0))]
    a_spec = lambda j: pl.BlockSpec((2, tm, FSH), lambda i: (0, i, j))
    a_shape = jax.ShapeDtypeStruct((2, S, DFF), BF16)

    def hidden(hb, wg_ref, wu_ref, wo_ref, a_ref):
        g = _dot(hb, wg_ref[...])
        up = _dot(hb, wu_ref[...])
        a_ref[0] = g.astype(BF16)
        a_ref[1] = up.astype(BF16)
        act = (g * jax.nn.sigmoid(g)) * up
        return _dot(act.astype(BF16), wo_ref[...])

    def first(x_ref, vec_ref, wg_ref, wu_ref, wo_ref, h_ref, a_ref, u_ref):
        h, _, _ = _prenorm(x_ref[...], vec_ref)
        hb = h.astype(BF16)
        h_ref[...] = hb
        u_ref[...] = hidden(hb, wg_ref, wu_ref, wo_ref, a_ref)

    h, a, u_half = pl.pallas_call(
        first, name="ffn_fwd_first", grid=(S // tm,),
        in_specs=[pl.BlockSpec((tm, D), row)] + half(0),
        out_specs=[pl.BlockSpec((tm, D), row), a_spec(0), pl.BlockSpec((tm, D), row)],
        out_shape=[jax.ShapeDtypeStruct((S, D), BF16), a_shape, jax.ShapeDtypeStruct((S, D), F32)],
        compiler_params=_params("parallel"),
    )(x, vec, w_in, w_in, w_out)

    def second(x_ref, h_ref, uh_ref, vec_ref, wg_ref, wu_ref, wo_ref, a_in, xo_ref, a_ref, u_ref):
        u = uh_ref[...] + hidden(h_ref[...], wg_ref, wu_ref, wo_ref, a_ref)
        u_ref[...] = u
        uhat, _ = _rms(u)
        xo_ref[...] = x_ref[...] + (weight * (1.0 + vec_ref[4:5, :])) * (uhat * vec_ref[1:2, :])

    xo, a, u = pl.pallas_call(
        second, name="ffn_fwd_second", grid=(S // tm,),
        in_specs=[pl.BlockSpec((tm, D), row), pl.BlockSpec((tm, D), row), pl.BlockSpec((tm, D), row)] + half(1) + [_ANY],
        out_specs=[pl.BlockSpec((tm, D), row), a_spec(1), pl.BlockSpec((tm, D), row)],
        out_shape=[jax.ShapeDtypeStruct((S, D), F32), a_shape, jax.ShapeDtypeStruct((S, D), F32)],
        input_output_aliases={7: 1},
        compiler_params=_params("parallel"),
    )(x, h, u_half, vec, w_in, w_in, w_out, a)
    return xo, a, u, h


def ffn_bwd(dout, x, u, a, vec, w_in, w_out, weight):
    S = x.shape[0]
    tm = min(512, S)
    row = lambda i: (i, 0)
    half = lambda j: [pl.BlockSpec((2, tm, FSH), lambda i: (0, i, j)), _w3((8, D)),
                      pl.BlockSpec((None, D, FSH), lambda i: (j, 0, 0)),
                      pl.BlockSpec((None, D, FSH), lambda i: (j + 2, 0, 0)),
                      pl.BlockSpec((None, FSH, D), lambda i: (j, 0, 0))]
    half_out = lambda j: [pl.BlockSpec((tm, FSH), lambda i: (i, j)), pl.BlockSpec((2, tm, FSH), lambda i: (0, i, j))]
    half_shape = [jax.ShapeDtypeStruct((S, DFF), BF16), jax.ShapeDtypeStruct((2, S, DFF), BF16)]

    def hidden_bwd(du, a_ref, wg_ref, wu_ref, wo_ref, act_ref, da_ref):
        dact = _dot(du, wo_ref[...], NT)
        g = a_ref[0].astype(F32)
        up = a_ref[1].astype(F32)
        s = jax.nn.sigmoid(g)
        silu = g * s
        act_ref[...] = (silu * up).astype(BF16)
        dg = (dact * up * (s * (1.0 + g * (1.0 - s)))).astype(BF16)
        dup = (dact * silu).astype(BF16)
        da_ref[0] = dg
        da_ref[1] = dup
        return _dot(dg, wg_ref[...], NT) + _dot(dup, wu_ref[...], NT)

    def first(do_ref, u_ref, a_ref, vec_ref, wg_ref, wu_ref, wo_ref, du_ref, dh_ref, act_ref, da_ref, vg_ref):
        @pl.when(pl.program_id(0) == 0)
        def _():
            vg_ref[...] = jnp.zeros_like(vg_ref)

        du, dgate_rows, dgpost_rows = _postnorm_bwd(do_ref[...], u_ref[...], vec_ref, weight)
        vg_ref[2:3, :] += jnp.sum(dgate_rows, axis=0, keepdims=True)
        vg_ref[4:5, :] += jnp.sum(dgpost_rows, axis=0, keepdims=True)
        du = du.astype(BF16)
        du_ref[...] = du
        dh_ref[...] = hidden_bwd(du, a_ref, wg_ref, wu_ref, wo_ref, act_ref, da_ref)

    du, dh, act, da, vg_post = pl.pallas_call(
        first, name="ffn_bwd_first", grid=(S // tm,),
        in_specs=[pl.BlockSpec((tm, D), row), pl.BlockSpec((tm, D), row)] + half(0),
        out_specs=[pl.BlockSpec((tm, D), row), pl.BlockSpec((tm, D), row)] + half_out(0) + [_w3((8, D))],
        out_shape=[jax.ShapeDtypeStruct((S, D), BF16), jax.ShapeDtypeStruct((S, D), F32)] + half_shape
        + [jax.ShapeDtypeStruct((8, D), F32)],
        compiler_params=_params("arbitrary"),
    )(dout, u, a, vec, w_in, w_in, w_out)

    def second(do_ref, x_ref, du_ref, dh_ref, a_ref, vec_ref, wg_ref, wu_ref, wo_ref, act_in, da_in,
               dx_ref, act_ref, da_ref, vg_ref):
        @pl.when(pl.program_id(0) == 0)
        def _():
            vg_ref[...] = jnp.zeros_like(vg_ref)

        dh = dh_ref[...] + hidden_bwd(du_ref[...], a_ref, wg_ref, wu_ref, wo_ref, act_ref, da_ref)
        dx_ref[...] = do_ref[...] + _prenorm_bwd(dh, x_ref[...], vec_ref, vg_ref)

    dx, act, da, vg_pre = pl.pallas_call(
        second, name="ffn_bwd_second", grid=(S // tm,),
        in_specs=[pl.BlockSpec((tm, D), row), pl.BlockSpec((tm, D), row), pl.BlockSpec((tm, D), row),
                  pl.BlockSpec((tm, D), row)] + half(1) + [_ANY, _ANY],
        out_specs=[pl.BlockSpec((tm, D), row)] + half_out(1) + [_w3((8, D))],
        out_shape=[jax.ShapeDtypeStruct((S, D), F32)] + half_shape + [jax.ShapeDtypeStruct((8, D), F32)],
        input_output_aliases={9: 1, 10: 2},
        compiler_params=_params("arbitrary"),
    )(dout, x, du, dh, a, vec, w_in, w_in, w_out, act, da)
    return dx, du, act, da, vg_post + vg_pre


def dw_matmul(name, a, b, a_spec, b_spec, out_shape, out_spec, grid):
    def body(a_ref, b_ref, o_ref):
        @pl.when(pl.program_id(len(grid) - 1) == 0)
        def _():
            o_ref[...] = jnp.zeros_like(o_ref)

        o_ref[...] += _dot(a_ref[...], b_ref[...], TN)

    return pl.pallas_call(
        body, name=name, grid=grid, in_specs=[a_spec, b_spec], out_specs=out_spec,
        out_shape=jax.ShapeDtypeStruct(out_shape, F32),
        compiler_params=_params(*(["parallel"] * (len(grid) - 1) + ["arbitrary"])),
    )(a, b)


def ffn_dw(h, da, act, du):
    S = h.shape[0]
    tk = min(DW_TK, S)
    dw_in = dw_matmul("ffn_dw_in", h, da,
                      pl.BlockSpec((tk, D), lambda n, k: (k, 0)),
                      pl.BlockSpec((None, tk, FSH), lambda n, k: (n // 2, k, n % 2)),
                      (N_CHIP, D, FSH), pl.BlockSpec((None, D, FSH), lambda n, k: (n, 0, 0)),
                      (N_CHIP, S // tk))
    dw_out = dw_matmul("ffn_dw_out", act, du,
                       pl.BlockSpec((tk, FSH), lambda n, k: (k, n)),
                       pl.BlockSpec((tk, D), lambda n, k: (k, 0)),
                       (DFF, D), pl.BlockSpec((FSH, D), lambda n, k: (n, 0)),
                       (2, S // tk))
    return dw_in, dw_out


def _halo_specs(tm, S):
    nb = tm // HALO
    last = S // HALO - 1
    return [pl.BlockSpec((HALO, D), lambda i: (jnp.maximum(i * nb - 1, 0), 0)),
            pl.BlockSpec((tm, D), lambda i: (i, 0)),
            pl.BlockSpec((HALO, D), lambda i: (jnp.minimum((i + 1) * nb, last), 0))]


def _shift_rows(v, k):
    return pltpu.roll(v, k % v.shape[0], 0)


def _window_sum(v, g, forward):
    acc = v + _shift_rows(v, 1 if forward else -1)
    for step in (1, 2, 4)[:g]:
        acc = _shift_rows(acc, step) + _shift_rows(acc, -step)
    return acc


def _pool_count(t, w, S):
    return jnp.maximum(jnp.minimum(t + w // 2, S) - jnp.maximum(t - w // 2, 0), 1).astype(F32)


def pool_fwd(x, vec, pw, pvec):
    S = x.shape[0]
    tm = min(ROW_TILE, S)
    G = D // 4

    def body(xp_ref, x_ref, xn_ref, vec_ref, pw_ref, pv_ref, xo_ref, y_ref, z_ref):
        i = pl.program_id(0)
        xa = jnp.concatenate([xp_ref[...], x_ref[...], xn_ref[...]], axis=0)
        t = i * tm - HALO + lax.broadcasted_iota(jnp.int32, (tm + 2 * HALO, 1), 0)
        h, _, _ = _prenorm(xa, vec_ref)
        h = jnp.where((t >= 0) & (t < S), h, 0.0)
        tmain = t[HALO:HALO + tm]
        for g in range(4):
            hg = h[:, g * G:(g + 1) * G]
            pooled = _window_sum(hg, g, True)[HALO:HALO + tm] / _pool_count(tmain, POOL_WINDOWS[g], S)
            z = (pooled - hg[HALO:HALO + tm]).astype(BF16)
            z_ref[:, g * G:(g + 1) * G] = z
            y_ref[:, g * G:(g + 1) * G] = _dot(z, pw_ref[g]) + pv_ref[0:1, g * G:(g + 1) * G]
        u = y_ref[...] * pv_ref[1:2, :]
        uhat, _ = _rms(u)
        xo_ref[...] = x_ref[...] + (1.0 + vec_ref[4:5, :]) * (uhat * vec_ref[1:2, :])

    row = lambda i: (i, 0)
    full = lambda i: (0, 0)
    return pl.pallas_call(
        body, name="pool_fwd", grid=(S // tm,),
        in_specs=_halo_specs(tm, S) + [pl.BlockSpec((8, D), full), pl.BlockSpec((4, G, G), lambda i: (0, 0, 0)),
                                       pl.BlockSpec((8, D), full)],
        out_specs=[pl.BlockSpec((tm, D), row)] * 3,
        out_shape=[jax.ShapeDtypeStruct((S, D), F32), jax.ShapeDtypeStruct((S, D), F32),
                   jax.ShapeDtypeStruct((S, D), BF16)],
        compiler_params=_params("parallel"),
    )(x, x, x, vec, pw, pvec)


def pool_bwd(dout, x, y, z, vec, pw, pvec):
    S = x.shape[0]
    tm = min(ROW_TILE, S)
    G = D // 4
    R = G // N_CHIP

    def body(dop_ref, do_ref, don_ref, yp_ref, y_ref, yn_ref, x_ref, z_ref, vec_ref, pw_ref, pv_ref,
             dx_ref, vg_ref, pg_ref, dw_ref, dh_ref):
        i = pl.program_id(0)

        @pl.when(i == 0)
        def _():
            vg_ref[...] = jnp.zeros_like(vg_ref)
            pg_ref[...] = jnp.zeros_like(pg_ref)
            dw_ref[...] = jnp.zeros_like(dw_ref)

        doa = jnp.concatenate([dop_ref[...], do_ref[...], don_ref[...]], axis=0)
        ya = jnp.concatenate([yp_ref[...], y_ref[...], yn_ref[...]], axis=0)
        t = i * tm - HALO + lax.broadcasted_iota(jnp.int32, (tm + 2 * HALO, 1), 0)
        inside = (t >= 0) & (t < S)
        main = (t >= i * tm) & (t < (i + 1) * tm)
        du, dgate_rows, dgpost_rows = _postnorm_bwd(doa, ya * pv_ref[1:2, :], vec_ref, 1.0)
        du = jnp.where(inside, du, 0.0)
        vg_ref[2:3, :] += jnp.sum(jnp.where(main, dgate_rows, 0.0), axis=0, keepdims=True)
        vg_ref[4:5, :] += jnp.sum(jnp.where(main, dgpost_rows, 0.0), axis=0, keepdims=True)
        dy = du * pv_ref[1:2, :]
        pg_ref[0:1, :] += jnp.sum(jnp.where(main, dy, 0.0), axis=0, keepdims=True)
        pg_ref[1:2, :] += jnp.sum(jnp.where(main, du * ya, 0.0), axis=0, keepdims=True)
        for g in range(4):
            dyg = dy[:, g * G:(g + 1) * G].astype(BF16)
            dz = _dot(dyg, pw_ref[g], NT)
            e = dz / _pool_count(t, POOL_WINDOWS[g], S)
            dh_ref[:, g * G:(g + 1) * G] = (_window_sum(e, g, False) - dz)[HALO:HALO + tm]
            dwg = _dot(z_ref[:, g * G:(g + 1) * G], dyg[HALO:HALO + tm], TN)
            for q in range(N_CHIP):
                dw_ref[q, g] += dwg[q * R:(q + 1) * R, :]
        dx_ref[...] = do_ref[...] + _prenorm_bwd(dh_ref[...], x_ref[...], vec_ref, vg_ref)

    row = lambda i: (i, 0)
    full = lambda i: (0, 0)
    halo = _halo_specs(tm, S)
    return pl.pallas_call(
        body, name="pool_bwd", grid=(S // tm,),
        in_specs=halo + halo + [pl.BlockSpec((tm, D), row), pl.BlockSpec((tm, D), row), pl.BlockSpec((8, D), full),
                                pl.BlockSpec((4, G, G), lambda i: (0, 0, 0)), pl.BlockSpec((8, D), full)],
        out_specs=[pl.BlockSpec((tm, D), row), pl.BlockSpec((8, D), full), pl.BlockSpec((8, D), full),
                   pl.BlockSpec((N_CHIP, 4, R, G), lambda i: (0, 0, 0, 0))],
        out_shape=[jax.ShapeDtypeStruct((S, D), F32), jax.ShapeDtypeStruct((8, D), F32),
                   jax.ShapeDtypeStruct((8, D), F32), jax.ShapeDtypeStruct((N_CHIP, 4, R, G), F32)],
        scratch_shapes=[pltpu.VMEM((tm, D), F32)],
        compiler_params=_params("arbitrary"),
    )(dout, dout, dout, y, y, y, x, z, vec, pw, pvec)


N_PAIR = N_HEADS // 2
SLOTS = 128 // ROPE
ROPE_ALL = N_HEADS * ROPE
NOPE_ALL = N_HEADS * NOPE
LAT_ALL = N_HEADS * KVL
DLAT = QL + KVL + 2 * 128
DQ_ALL = NOPE_ALL + 2 * ROPE_ALL


def _w3(shape):
    return pl.BlockSpec(shape, lambda i: (0,) * len(shape))


def _slot_mask(hd, rows):
    lane = lax.broadcasted_iota(jnp.int32, (rows, 128), 1)
    return (lane // ROPE) == (hd % SLOTS)


MLA_WEIGHTS = ("wq", "wkv", "wkr4", "wkrs4", "qn", "kvn", "wn", "wr", "wrs", "bduk")


def _mla_weight_specs():
    return [_w3((D, QL)), _w3((D, KVL)), _w3((D, 128)), _w3((D, 128)), _w3((1, QL)), _w3((1, KVL)),
            _w3((QL, NOPE_ALL)), _w3((QL, ROPE_ALL)), _w3((QL, ROPE_ALL)), _w3((N_PAIR, 2 * NOPE, 2 * KVL))]


def mla_pre(x, vec, mw, tabs):
    S = x.shape[0]
    tm = min(ROW_TILE, S)

    def body(x_ref, vec_ref, cos_ref, sin_ref, wq_ref, wkv_ref, wkr_ref, wkrs_ref, qn_ref, kvn_ref,
             wn_ref, wr_ref, wrs_ref, bduk_ref,
             h_ref, cq_ref, ckv_ref, cqn_ref, qnope_ref, qcat_ref, kcat_ref, vcat_ref):
        h, _, _ = _prenorm(x_ref[...], vec_ref)
        hb = h.astype(BF16)
        h_ref[...] = hb
        cq_raw = _dot(hb, wq_ref[...])
        ckv_raw = _dot(hb, wkv_ref[...])
        cq_ref[...] = cq_raw
        ckv_ref[...] = ckv_raw
        cos, sin = cos_ref[...], sin_ref[...]
        ckv = (_rms(ckv_raw)[0] * kvn_ref[...]).astype(BF16)
        kcat_ref[:, 0:KVL] = ckv
        kcat_ref[:, KVL:] = (_dot(hb, wkr_ref[...]) * cos + _dot(hb, wkrs_ref[...]) * sin).astype(BF16)
        vcat_ref[:, 0:KVL] = ckv
        ones = lax.broadcasted_iota(jnp.int32, (tm, QPAD - KVL), 1) == 0
        vcat_ref[:, KVL:] = jnp.where(ones, 1.0, 0.0).astype(BF16)
        cqb = (_rms(cq_raw)[0] * qn_ref[...]).astype(BF16)
        cqn_ref[...] = cqb
        qn = _dot(cqb, wn_ref[...]).astype(BF16)
        qnope_ref[...] = qn
        cos4, sin4 = jnp.tile(cos, (1, SLOTS)), jnp.tile(sin, (1, SLOTS))
        qr = ((_dot(cqb, wr_ref[...]) * cos4 + _dot(cqb, wrs_ref[...]) * sin4) * ATTN_SCALE).astype(BF16)
        for j in range(N_PAIR):
            ql = (_dot(qn[:, 128 * j:128 * (j + 1)], bduk_ref[j]) * ATTN_SCALE).astype(BF16)
            for hd in (2 * j, 2 * j + 1):
                qcat_ref[hd, :, 0:KVL] = ql[:, KVL * (hd - 2 * j):KVL * (hd - 2 * j + 1)]
                group = qr[:, 128 * (hd // SLOTS):128 * (hd // SLOTS + 1)]
                qcat_ref[hd, :, KVL:] = jnp.where(_slot_mask(hd, tm), group, jnp.zeros_like(group))

    row = lambda i: (i, 0)
    hrow = lambda i: (0, i, 0)
    return pl.pallas_call(
        body, name="mla_pre", grid=(S // tm,),
        in_specs=[pl.BlockSpec((tm, D), row), _w3((8, D)), pl.BlockSpec((tm, 128), row), pl.BlockSpec((tm, 128), row)]
        + _mla_weight_specs(),
        out_specs=[pl.BlockSpec((tm, D), row), pl.BlockSpec((tm, QL), row), pl.BlockSpec((tm, KVL), row),
                   pl.BlockSpec((tm, QL), row), pl.BlockSpec((tm, NOPE_ALL), row),
                   pl.BlockSpec((N_HEADS, tm, QPAD), hrow), pl.BlockSpec((tm, QPAD), row),
                   pl.BlockSpec((tm, QPAD), row)],
        out_shape=[jax.ShapeDtypeStruct((S, D), BF16), jax.ShapeDtypeStruct((S, QL), F32),
                   jax.ShapeDtypeStruct((S, KVL), F32), jax.ShapeDtypeStruct((S, QL), BF16),
                   jax.ShapeDtypeStruct((S, NOPE_ALL), BF16), jax.ShapeDtypeStruct((N_HEADS, S, QPAD), BF16),
                   jax.ShapeDtypeStruct((S, QPAD), BF16), jax.ShapeDtypeStruct((S, QPAD), BF16)],
        compiler_params=_params("parallel"),
    )(x, vec, tabs[0], tabs[1], *[mw[k] for k in MLA_WEIGHTS])


def attn_fwd(qcat, kcat, vcat):
    S = kcat.shape[0]
    tq = min(ATTN_TQ, S)
    kc = min(ATTN_KC, S)

    def body(q_ref, k_ref, v_ref, o_ref, lse_ref):
        q = q_ref[...]
        m = jnp.full((tq, 1), -jnp.inf, F32)
        ov = jnp.zeros((tq, QPAD), F32)
        for c in range(S // kc):
            s = _dot(q, k_ref[c * kc:(c + 1) * kc, :], NT)
            m_new = jnp.maximum(m, jnp.max(s, axis=-1, keepdims=True))
            p = jnp.exp(s - m_new).astype(BF16)
            ov = ov * jnp.exp(m - m_new) + _dot(p, v_ref[c * kc:(c + 1) * kc, :])
            m = m_new
        l = ov[:, KVL:KVL + 1]
        o_ref[...] = (ov[:, 0:KVL] * (1.0 / l)).astype(BF16)
        lse_ref[...] = _as_row(m + jnp.log(l))

    return pl.pallas_call(
        body, name="attn_fwd", grid=(N_HEADS, S // tq),
        in_specs=[pl.BlockSpec((None, tq, QPAD), lambda h, i: (h, i, 0)),
                  pl.BlockSpec((S, QPAD), lambda h, i: (0, 0)),
                  pl.BlockSpec((S, QPAD), lambda h, i: (0, 0))],
        out_specs=[pl.BlockSpec((tq, KVL), lambda h, i: (i, h)),
                   pl.BlockSpec((None, 1, tq), lambda h, i: (h, 0, i))],
        out_shape=[jax.ShapeDtypeStruct((S, LAT_ALL), BF16), jax.ShapeDtypeStruct((N_HEADS, 1, S), F32)],
        compiler_params=_params("parallel", "parallel"),
    )(qcat, kcat, vcat)


def mla_post(olat, x, vec, bduv, wo):
    S = x.shape[0]
    tm = min(ROW_TILE, S)

    def body(o_ref, x_ref, vec_ref, bduv_ref, wo_ref, xo_ref, u_ref, ocat_ref):
        for j in range(N_PAIR):
            oc = _dot(o_ref[:, 2 * KVL * j:2 * KVL * (j + 1)], bduv_ref[j])
            ocat_ref[:, 2 * VH * j:2 * VH * (j + 1)] = oc.astype(BF16)
        u = _dot(ocat_ref[...], wo_ref[...])
        u_ref[...] = u
        uhat, _ = _rms(u)
        xo_ref[...] = x_ref[...] + (1.0 + vec_ref[4:5, :]) * (uhat * vec_ref[1:2, :])

    row = lambda i: (i, 0)
    return pl.pallas_call(
        body, name="mla_post", grid=(S // tm,),
        in_specs=[pl.BlockSpec((tm, LAT_ALL), row), pl.BlockSpec((tm, D), row), _w3((8, D)),
                  _w3((N_PAIR, 2 * KVL, 2 * VH)), _w3((D, D))],
        out_specs=[pl.BlockSpec((tm, D), row), pl.BlockSpec((tm, D), row), pl.BlockSpec((tm, D), row)],
        out_shape=[jax.ShapeDtypeStruct((S, D), F32), jax.ShapeDtypeStruct((S, D), F32),
                   jax.ShapeDtypeStruct((S, D), BF16)],
        compiler_params=_params("parallel"),
    )(olat, x, vec, bduv, wo)


def mla_post_bwd(dout, u, olat, vec, bduv, wo):
    S = u.shape[0]
    tm = min(ROW_TILE, S)

    def body(do_ref, u_ref, o_ref, vec_ref, bduv_ref, wo_ref, du_ref, docat_ref, dolat_ref, delta_ref, vg_ref):
        @pl.when(pl.program_id(0) == 0)
        def _():
            vg_ref[...] = jnp.zeros_like(vg_ref)

        du, dgate_rows, dgpost_rows = _postnorm_bwd(do_ref[...], u_ref[...], vec_ref, 1.0)
        vg_ref[2:3, :] += jnp.sum(dgate_rows, axis=0, keepdims=True)
        vg_ref[4:5, :] += jnp.sum(dgpost_rows, axis=0, keepdims=True)
        dub = du.astype(BF16)
        du_ref[...] = dub
        docat_ref[...] = _dot(dub, wo_ref[...], NT).astype(BF16)
        for j in range(N_PAIR):
            dol = _dot(docat_ref[:, 2 * VH * j:2 * VH * (j + 1)], bduv_ref[j], NT).astype(BF16)
            dolat_ref[:, 2 * KVL * j:2 * KVL * (j + 1)] = dol
            prod = dol.astype(F32) * o_ref[:, 2 * KVL * j:2 * KVL * (j + 1)].astype(F32)
            delta_ref[2 * j] = _as_row(jnp.sum(prod[:, 0:KVL], axis=-1, keepdims=True))
            delta_ref[2 * j + 1] = _as_row(jnp.sum(prod[:, KVL:], axis=-1, keepdims=True))

    row = lambda i: (i, 0)
    hrow = lambda i: (0, i, 0)
    return pl.pallas_call(
        body, name="mla_post_bwd", grid=(S // tm,),
        in_specs=[pl.BlockSpec((tm, D), row), pl.BlockSpec((tm, D), row), pl.BlockSpec((tm, LAT_ALL), row),
                  _w3((8, D)), _w3((N_PAIR, 2 * KVL, 2 * VH)), _w3((D, D))],
        out_specs=[pl.BlockSpec((tm, D), row), pl.BlockSpec((tm, D), row),
                   pl.BlockSpec((tm, LAT_ALL), row), pl.BlockSpec((N_HEADS, 1, tm), lambda i: (0, 0, i)), _w3((8, D))],
        out_shape=[jax.ShapeDtypeStruct((S, D), BF16), jax.ShapeDtypeStruct((S, D), BF16),
                   jax.ShapeDtypeStruct((S, LAT_ALL), BF16), jax.ShapeDtypeStruct((N_HEADS, 1, S), F32),
                   jax.ShapeDtypeStruct((8, D), F32)],
        compiler_params=_params("arbitrary"),
    )(dout, u, olat, vec, bduv, wo)


def attn_bwd(qcat, kcat, kcat_t, dolat, lse_row, delta_row):
    S = kcat.shape[0]
    tq = min(ATTN_TQ, S)
    kc = min(ATTN_KC, S)

    def body(q_ref, k_ref, kt_ref, do_ref, lse_ref, dl_ref, dq_ref, dk_ref, dv_ref):
        @pl.when((pl.program_id(0) == 0) & (pl.program_id(1) == 0))
        def _():
            dk_ref[...] = jnp.zeros_like(dk_ref)
            dv_ref[...] = jnp.zeros_like(dv_ref)

        q, do = q_ref[...], do_ref[...]
        lse, dl = lse_ref[...], dl_ref[...]
        dqt = jnp.zeros((QPAD, tq), F32)
        for c in range(S // kc):
            rows = slice(c * kc, (c + 1) * kc)
            st = _dot(k_ref[rows, :], q, NT)
            pt = jnp.exp(st - lse)
            dpt = _dot(k_ref[rows, 0:KVL], do, NT)
            dst = (pt * (dpt - dl)).astype(BF16)
            dv_ref[rows, :] += _dot(pt.astype(BF16), do)
            dk_ref[rows, :] += _dot(dst, q)
            dqt = dqt + _dot(kt_ref[:, rows], dst)
        dq_ref[...] = dqt.T

    return pl.pallas_call(
        body, name="attn_bwd", grid=(N_HEADS, S // tq),
        in_specs=[pl.BlockSpec((None, tq, QPAD), lambda h, i: (h, i, 0)),
                  pl.BlockSpec((S, QPAD), lambda h, i: (0, 0)),
                  pl.BlockSpec((QPAD, S), lambda h, i: (0, 0)),
                  pl.BlockSpec((tq, KVL), lambda h, i: (i, h)),
                  pl.BlockSpec((None, 1, tq), lambda h, i: (h, 0, i)),
                  pl.BlockSpec((None, 1, tq), lambda h, i: (h, 0, i))],
        out_specs=[pl.BlockSpec((None, tq, QPAD), lambda h, i: (h, i, 0)),
                   pl.BlockSpec((S, QPAD), lambda h, i: (0, 0)),
                   pl.BlockSpec((S, KVL), lambda h, i: (0, 0))],
        out_shape=[jax.ShapeDtypeStruct((N_HEADS, S, QPAD), F32), jax.ShapeDtypeStruct((S, QPAD), F32),
                   jax.ShapeDtypeStruct((S, KVL), F32)],
        compiler_params=_params("arbitrary", "arbitrary"),
    )(qcat, kcat, kcat_t, dolat, lse_row, delta_row)


def mla_pre_bwd(dout, dq, dk, dv, x, cq_raw, ckv_raw, vec, mw, tabs):
    S = x.shape[0]
    tm = min(ROW_TILE, S)

    def body(do_ref, dq_ref, dk_ref, dv_ref, x_ref, cq_ref, ckv_ref, vec_ref, cos_ref, sin_ref,
             wq_ref, wkv_ref, wkr_ref, wkrs_ref, qn_ref, kvn_ref, wn_ref, wr_ref, wrs_ref, bduk_ref,
             dx_ref, dlat_ref, dql_ref, dqcat_ref, vg_ref, ng_ref):
        @pl.when(pl.program_id(0) == 0)
        def _():
            vg_ref[...] = jnp.zeros_like(vg_ref)
            ng_ref[...] = jnp.zeros_like(ng_ref)

        cos, sin = cos_ref[...], sin_ref[...]
        for j in range(N_PAIR):
            dql = jnp.concatenate([dq_ref[2 * j, :, 0:KVL], dq_ref[2 * j + 1, :, 0:KVL]], axis=1) * ATTN_SCALE
            dql = dql.astype(BF16)
            dql_ref[:, 2 * KVL * j:2 * KVL * (j + 1)] = dql
            dqcat_ref[:, 2 * NOPE * j:2 * NOPE * (j + 1)] = _dot(dql, bduk_ref[j], NT).astype(BF16)
        groups = []
        for grp in range(N_HEADS // SLOTS):
            acc = jnp.zeros((tm, 128), F32)
            for hd in range(SLOTS * grp, SLOTS * (grp + 1)):
                acc = acc + jnp.where(_slot_mask(hd, tm), dq_ref[hd, :, KVL:], 0.0)
            groups.append(acc)
        dqr = jnp.concatenate(groups, axis=1) * ATTN_SCALE
        qa = (dqr * jnp.tile(cos, (1, SLOTS))).astype(BF16)
        qb = (dqr * jnp.tile(sin, (1, SLOTS))).astype(BF16)
        dqcat_ref[:, NOPE_ALL:NOPE_ALL + ROPE_ALL] = qa
        dqcat_ref[:, NOPE_ALL + ROPE_ALL:] = qb
        dcq = _dot(dqcat_ref[:, 0:NOPE_ALL], wn_ref[...], NT) + _dot(qa, wr_ref[...], NT) + _dot(qb, wrs_ref[...], NT)
        cqh, rq = _rms(cq_ref[...])
        ng_ref[0:1, :] += jnp.sum(dcq * cqh, axis=0, keepdims=True)
        dcq_raw = _rms_bwd(cqh, rq, dcq * qn_ref[...]).astype(BF16)
        dckv = dk_ref[:, 0:KVL] + dv_ref[...]
        ckvh, rk = _rms(ckv_ref[...])
        ng_ref[1:2, 0:KVL] += jnp.sum(dckv * ckvh, axis=0, keepdims=True)
        dckv_raw = _rms_bwd(ckvh, rk, dckv * kvn_ref[...]).astype(BF16)
        dkr = dk_ref[:, KVL:]
        ka = (dkr * cos).astype(BF16)
        kb = (dkr * sin).astype(BF16)
        dlat_ref[:, 0:QL] = dcq_raw
        dlat_ref[:, QL:QL + KVL] = dckv_raw
        dlat_ref[:, QL + KVL:QL + KVL + 128] = ka
        dlat_ref[:, QL + KVL + 128:] = kb
        dh = (_dot(dcq_raw, wq_ref[...], NT) + _dot(dckv_raw, wkv_ref[...], NT)
              + _dot(ka, wkr_ref[...], NT) + _dot(kb, wkrs_ref[...], NT))
        dx_ref[...] = do_ref[...] + _prenorm_bwd(dh, x_ref[...], vec_ref, vg_ref)

    row = lambda i: (i, 0)
    hrow = lambda i: (0, i, 0)
    return pl.pallas_call(
        body, name="mla_pre_bwd", grid=(S // tm,),
        in_specs=[pl.BlockSpec((tm, D), row), pl.BlockSpec((N_HEADS, tm, QPAD), hrow), pl.BlockSpec((tm, QPAD), row),
                  pl.BlockSpec((tm, KVL), row), pl.BlockSpec((tm, D), row), pl.BlockSpec((tm, QL), row),
                  pl.BlockSpec((tm, KVL), row), _w3((8, D)), pl.BlockSpec((tm, 128), row), pl.BlockSpec((tm, 128), row)]
        + _mla_weight_specs(),
        out_specs=[pl.BlockSpec((tm, D), row), pl.BlockSpec((tm, DLAT), row), pl.BlockSpec((tm, LAT_ALL), row),
                   pl.BlockSpec((tm, DQ_ALL), row), _w3((8, D)), _w3((8, QL))],
        out_shape=[jax.ShapeDtypeStruct((S, D), F32), jax.ShapeDtypeStruct((S, DLAT), BF16),
                   jax.ShapeDtypeStruct((S, LAT_ALL), BF16), jax.ShapeDtypeStruct((S, DQ_ALL), BF16),
                   jax.ShapeDtypeStruct((8, D), F32), jax.ShapeDtypeStruct((8, QL), F32)],
        compiler_params=_params("arbitrary"),
    )(dout, dq, dk, dv, x, cq_raw, ckv_raw, vec, tabs[0], tabs[1], *[mw[k] for k in MLA_WEIGHTS])


def mla_dw(h, dlat, cqn, dqcat, dql, qnope, olat, docat, ocat, du):
    S = h.shape[0]
    tk = min(DW_TK, S)
    nk = S // tk
    flat = lambda w: pl.BlockSpec((tk, w), lambda k: (k, 0))
    cols = lambda w: pl.BlockSpec((tk, w), lambda n, k: (k, n))
    pair_o = pl.BlockSpec((None, 2 * KVL, 128), lambda n, k: (n, 0, 0))
    g = {}
    g["in"] = dw_matmul("mla_dw_in", h, dlat, flat(D), flat(DLAT), (D, DLAT),
                        pl.BlockSpec((D, DLAT), lambda k: (0, 0)), (nk,))
    g["q"] = dw_matmul("mla_dw_q", cqn, dqcat, flat(QL), flat(DQ_ALL), (QL, DQ_ALL),
                       pl.BlockSpec((QL, DQ_ALL), lambda k: (0, 0)), (nk,))
    g["uk"] = dw_matmul("mla_dw_uk", dql, qnope, cols(2 * KVL), cols(2 * NOPE), (N_PAIR, 2 * KVL, 2 * NOPE), pair_o,
                        (N_PAIR, nk))
    g["uv"] = dw_matmul("mla_dw_uv", olat, docat, cols(2 * KVL), cols(2 * VH), (N_PAIR, 2 * KVL, 2 * VH), pair_o,
                        (N_PAIR, nk))
    g["o"] = dw_matmul("mla_dw_o", ocat, du, cols(256), pl.BlockSpec((tk, D), lambda n, k: (k, 0)), (D, D),
                       pl.BlockSpec((256, D), lambda n, k: (n, 0)), (D // 256, nk))
    return g


def loss_head(y, target):
    S = y.shape[0]
    tm = min(512, S)

    def body(y_ref, t_ref, loss_ref, dy_ref):
        @pl.when(pl.program_id(0) == 0)
        def _():
            loss_ref[...] = jnp.zeros_like(loss_ref)

        err = y_ref[...] - t_ref[...]
        dy_ref[...] = err * (1.0 / D)
        loss_ref[...] += 0.5 * jnp.sum(jnp.mean(err * err, axis=-1, keepdims=True), axis=0, keepdims=True)

    row = lambda i: (i, 0)
    return pl.pallas_call(
        body, name="loss_head", grid=(S // tm,),
        in_specs=[pl.BlockSpec((tm, D), row), pl.BlockSpec((tm, D), row)],
        out_specs=[pl.BlockSpec((1, 1), lambda i: (0, 0)), pl.BlockSpec((tm, D), row)],
        out_shape=[jax.ShapeDtypeStruct((1, 1), F32), jax.ShapeDtypeStruct((S, D), F32)],
        compiler_params=_params("arbitrary"),
    )(y, target)


MOD_COLS = 9 * D // N_CHIP


def mod_fwd(c_pad, ada_w, ada_b_loc):
    tn = MOD_COLS // 3

    def body(c_ref, w_ref, b_ref, o_ref):
        c = c_ref[...]
        sc = (c * jax.nn.sigmoid(c)).astype(BF16)
        o_ref[...] = _dot(sc, w_ref[...].astype(BF16)) + b_ref[...]

    return pl.pallas_call(
        body, name="mod_fwd", grid=(2, 3),
        in_specs=[pl.BlockSpec((16, D), lambda i, n: (0, 0)), pl.BlockSpec((None, D, tn), lambda i, n: (i, 0, n)),
                  pl.BlockSpec((None, 1, tn), lambda i, n: (i, 0, n))],
        out_specs=pl.BlockSpec((None, 16, tn), lambda i, n: (i, 0, n)),
        out_shape=jax.ShapeDtypeStruct((2, 16, MOD_COLS), F32),
        compiler_params=_params("parallel", "parallel"),
    )(c_pad, ada_w, ada_b_loc)


def _adamw_math(w, g, m, v):
    m = ADAM_B1 * m + (1.0 - ADAM_B1) * g
    v = ADAM_B2 * v + (1.0 - ADAM_B2) * (g * g)
    m_hat = m / (1.0 - ADAM_B1 ** ADAM_STEP)
    v_hat = v / (1.0 - ADAM_B2 ** ADAM_STEP)
    delta = -ADAM_LR * (m_hat / (jnp.sqrt(v_hat) + ADAM_EPS) + ADAM_WD * w)
    return delta, m, v


def adamw(name, w, g, m, v, part=None, prev=None):
    shape = w.shape
    if part is None and w.size * 4 <= (1 << 20):
        whole = pl.BlockSpec(shape, lambda i: (0,) * len(shape))

        def small_body(w_ref, g_ref, m_ref, v_ref, d_ref, mo_ref, vo_ref):
            d_ref[...], mo_ref[...], vo_ref[...] = _adamw_math(w_ref[...], g_ref[...], m_ref[...], v_ref[...])

        return pl.pallas_call(
            small_body, name=name, grid=(1,), in_specs=[whole] * 4, out_specs=[whole] * 3,
            out_shape=[jax.ShapeDtypeStruct(shape, F32)] * 3, compiler_params=_params("arbitrary"),
        )(w, g, m, v)
    cols = shape[-1]
    rows = w.size // cols
    per_entry = rows // shape[0] if part is not None else rows
    tr = per_entry
    budget_rows = (2 << 20) // (cols * 4)
    for cand in range(min(per_entry, budget_rows) // 8 * 8, 0, -8):
        if per_entry % cand == 0:
            tr = cand
            break
    first, count = part if part is not None else (0, 1)
    tiles = per_entry // tr

    def body(w_ref, g_ref, m_ref, v_ref, *rest):
        d_ref, mo_ref, vo_ref = rest[-3:]
        d_ref[...], mo_ref[...], vo_ref[...] = _adamw_math(w_ref[...], g_ref[...], m_ref[...], v_ref[...])

    spec = pl.BlockSpec((tr, cols), lambda i: (i + first * tiles, 0))
    operands = [a.reshape(rows, cols) for a in (w, g, m, v)]
    aliases = {}
    if prev is not None:
        operands += [p.reshape(rows, cols) for p in prev]
        aliases = {4: 0, 5: 1, 6: 2}
    outs = pl.pallas_call(
        body, name=name, grid=(count * tiles,), in_specs=[spec] * 4 + [_ANY] * (len(operands) - 4),
        out_specs=[spec] * 3, out_shape=[jax.ShapeDtypeStruct((rows, cols), F32)] * 3,
        input_output_aliases=aliases, compiler_params=_params("parallel"),
    )(*operands)
    return [o.reshape(shape) for o in outs]


def adamw_ada(c_pad, dmod, w, m, v):
    tr = 256

    def body(c_ref, dm_ref, w_ref, m_ref, v_ref, g_ref, d_ref, mo_ref, vo_ref):
        c = c_ref[...]
        sc = (c * jax.nn.sigmoid(c)).astype(BF16)
        g = _dot(sc, dm_ref[...].astype(BF16), TN)
        g_ref[...] = g
        d_ref[...], mo_ref[...], vo_ref[...] = _adamw_math(w_ref[...], g, m_ref[...], v_ref[...])

    wspec = pl.BlockSpec((None, tr, MOD_COLS), lambda i, r: (i, r, 0))
    return pl.pallas_call(
        body, name="adamw_ada", grid=(2, D // tr),
        in_specs=[pl.BlockSpec((16, tr), lambda i, r: (0, r)),
                  pl.BlockSpec((None, 16, MOD_COLS), lambda i, r: (i, 0, 0)), wspec, wspec, wspec],
        out_specs=[wspec] * 4,
        out_shape=[jax.ShapeDtypeStruct((2, D, MOD_COLS), F32)] * 4,
        compiler_params=_params("parallel", "parallel"),
    )(c_pad, dmod, w, m, v)


def sum_devices(name, a):
    _, R, C = a.shape
    tr = R
    for cand in (64, 32, 16, 8):
        if R % cand == 0:
            tr = cand
            break

    def body(a_ref, o_ref):
        acc = a_ref[0]
        for dev in range(1, N_DEV):
            acc = acc + a_ref[dev]
        o_ref[...] = acc

    return pl.pallas_call(
        body, name=name, grid=(R // tr,),
        in_specs=[pl.BlockSpec((N_DEV, tr, C), lambda i: (0, i, 0))],
        out_specs=pl.BlockSpec((tr, C), lambda i: (i, 0)),
        out_shape=jax.ShapeDtypeStruct((R, C), F32),
        compiler_params=_params("parallel"),
    )(a)


def _place():
    return lax.axis_index("x"), lax.axis_index("y"), lax.axis_index("c")


def _other_chips(x, y):
    return [(1 - x, y), (x, 1 - y), (1 - x, 1 - y)]


def gather_devices(name, a):
    m_per, n = a.shape

    def body(x_ref, out_ref, send_sems, recv_sems, local_sem):
        x, y, c = _place()
        me, sibling = (x, y, c), (x, y, 1 - c)
        chips = _other_chips(x, y)

        def rows(px, py, pc):
            return out_ref.at[pl.ds((4 * px + 2 * py + pc) * m_per, m_per), :]

        def copy(k, block, to, src=None):
            return pltpu.make_async_remote_copy(
                src_ref=rows(*block) if src is None else src, dst_ref=rows(*block),
                send_sem=send_sems.at[k], recv_sem=recv_sems.at[k], device_id=to, device_id_type=MESH)

        mine = pltpu.make_async_copy(x_ref, rows(*me), local_sem)
        mine.start()
        first = [copy(0, me, sibling, src=x_ref)]
        first += [copy(1 + j, me, (*chip, c), src=x_ref) for j, chip in enumerate(chips)]
        for cp in first:
            cp.start()
        passed = [copy(4 + j, (*chip, c), sibling) for j, chip in enumerate(chips)]
        for j, chip in enumerate(chips):
            copy(1 + j, (*chip, c), me).wait_recv()
            passed[j].start()
        copy(0, sibling, me).wait_recv()
        for j, chip in enumerate(chips):
            copy(4 + j, (*chip, 1 - c), me).wait_recv()
        for cp in first + passed:
            cp.wait_send()
        mine.wait()

    out = pl.pallas_call(
        body, name=name,
        out_shape=jax.ShapeDtypeStruct((N_DEV * m_per, n), a.dtype),
        in_specs=[pl.BlockSpec(memory_space=pltpu.VMEM)],
        out_specs=pl.BlockSpec(memory_space=pltpu.VMEM),
        scratch_shapes=[pltpu.SemaphoreType.DMA((7,)), pltpu.SemaphoreType.DMA((7,)), pltpu.SemaphoreType.DMA],
        compiler_params=pltpu.CompilerParams(vmem_limit_bytes=VMEM_LIMIT),
    )(a)
    return out.reshape(N_DEV, m_per, n)


_ANY = pl.BlockSpec(memory_space=pl.ANY)


def _hbm_ref(a):
    return jax.new_ref(a, memory_space=pltpu.MemorySpace.HBM)


def _hbm_empty(shape, dtype):
    return jax.empty_ref(jax.ShapeDtypeStruct(shape, dtype), memory_space=pltpu.MemorySpace.HBM)


ID_PAIR, ID_CHIPS, ID_SHARE, ID_UKV = 8, 9, 10, 11


def _sequencer(name, collective_id, n_sem, peers_of, program):
    sems = pltpu.SemaphoreType.DMA((n_sem,))

    @pl.kernel(mesh=plsc.ScalarSubcoreMesh(axis_name="seq", num_cores=1), name=name, scratch_types=[sems, sems],
               compiler_params=pltpu.CompilerParams(collective_id=collective_id))
    def launch(send_sem, recv_sem):
        x, y, c = _place()
        peers = peers_of(x, y, c)
        barrier = pltpu.get_barrier_semaphore()
        for peer in peers:
            pl.semaphore_signal(barrier, inc=1, device_id=peer, device_id_type=MESH)
        pl.semaphore_wait(barrier, len(peers))
        program(x, y, c, send_sem, recv_sem)

    launch()


def gather_weights(name, stage, arrays):
    n = len(arrays)
    refs = [_hbm_ref(a) for a in arrays]

    def program(x, y, c, send_sem, recv_sem):
        me = 2 * x + y
        chips = _other_chips(x, y)

        def ici(t, r, half):
            cx, cy = chips[r]
            mine = refs[t].at[me, half]
            return pltpu.make_async_remote_copy(
                src_ref=mine, dst_ref=mine, send_sem=send_sem.at[3 * t + r], recv_sem=recv_sem.at[3 * t + r],
                device_id=(cx, cy, c), device_id_type=MESH)

        def d2d(t, r, half):
            cx, cy = chips[r]
            there = refs[t].at[2 * cx + cy, half]
            k = 3 * n + 3 * t + r
            return pltpu.make_async_remote_copy(
                src_ref=there, dst_ref=there, send_sem=send_sem.at[k], recv_sem=recv_sem.at[k],
                device_id=(x, y, 1 - c), device_id_type=MESH)

        for t in range(n):
            for r in range(3):
                ici(t, r, c).start()
        for t in range(n):
            for r in range(3):
                ici(t, r, c).wait_recv()
                d2d(t, r, c).start()
        for t in range(n):
            for r in range(3):
                d2d(t, r, 1 - c).wait_recv()
        for t in range(n):
            for r in range(3):
                ici(t, r, c).wait_send()
                d2d(t, r, c).wait_send()

    _sequencer(name, stage, 6 * n, lambda x, y, c: [(x, y, 1 - c)] + [(cx, cy, c) for cx, cy in _other_chips(x, y)],
               program)
    return [r[...] for r in refs]


def cast_into_slots(name, chip, shards, after=None):
    steps = 2
    n = len(shards)

    def body(chip_ref, *refs):
        for src, dst in zip(refs[:n], refs[-n - 1:-1]):
            dst[...] = src[...].astype(BF16)
        refs[-1][...] = jnp.zeros_like(refs[-1])

    token_spec = pl.BlockSpec((8, 128), lambda h, i, chip_ref: (0, 0))

    def spec_in(a, prefix):
        R, C = a.shape[-2:]
        return pl.BlockSpec((None,) * (len(prefix) + 1) + (R // steps, C), lambda h, i, chip_ref: prefix + (h, i, 0))

    def spec_out(a):
        R, C = a.shape[-2:]
        return pl.BlockSpec((None, None, R // steps, C), lambda h, i, chip_ref: (chip_ref[0], h, i, 0))

    outs = pl.pallas_call(
        body, name=name,
        grid_spec=pltpu.PrefetchScalarGridSpec(
            num_scalar_prefetch=1, grid=(2, steps),
            in_specs=[spec_in(a, p) for a, p in shards] + ([token_spec] if after is not None else []),
            out_specs=[spec_out(a) for a, _ in shards] + [token_spec]),
        out_shape=[jax.ShapeDtypeStruct((N_CHIP, 2) + a.shape[-2:], BF16) for a, _ in shards]
        + [jax.ShapeDtypeStruct((8, 128), F32)],
        compiler_params=_params("arbitrary", "arbitrary"),
    )(chip, *[a for a, _ in shards], *([after] if after is not None else []))
    return outs[:-1], outs[-1]


def reduce_pair(name, grads):
    n = len(grads)
    src = [_hbm_ref(g) for g in grads]
    dst = [_hbm_empty((N_CHIP,) + g.shape[2:], g.dtype) for g in grads]

    def program(x, y, c, send_sem, recv_sem):
        cps = [pltpu.make_async_remote_copy(
            src_ref=src[t].at[:, 1 - c], dst_ref=dst[t], send_sem=send_sem.at[t], recv_sem=recv_sem.at[t],
            device_id=(x, y, 1 - c), device_id_type=MESH) for t in range(n)]
        for cp in cps:
            cp.start()
        for cp in cps:
            cp.wait()

    _sequencer(name, ID_PAIR, n, lambda x, y, c: [(x, y, 1 - c)], program)
    return [r[...] for r in src], [r[...] for r in dst]


def pair_add(name, core, g, got):
    _, _, R, C = g.shape

    def body(core_ref, g_ref, got_ref, o_ref):
        o_ref[...] = (g_ref[...] + got_ref[...]).astype(BF16)

    return pl.pallas_call(
        body, name=name,
        grid_spec=pltpu.PrefetchScalarGridSpec(
            num_scalar_prefetch=1, grid=(N_CHIP,),
            in_specs=[pl.BlockSpec((None, None, R, C), lambda q, core_ref: (q, core_ref[0], 0, 0)),
                      pl.BlockSpec((None, R, C), lambda q, core_ref: (q, 0, 0))],
            out_specs=pl.BlockSpec((None, R, C), lambda q, core_ref: (q, 0, 0))),
        out_shape=jax.ShapeDtypeStruct((N_CHIP, R, C), BF16),
        compiler_params=_params("parallel"),
    )(core, g, got)


def reduce_chips(name, sums):
    n = len(sums)
    src = [_hbm_ref(s) for s in sums]
    dst = [_hbm_empty((3,) + s.shape[1:], s.dtype) for s in sums]

    def program(x, y, c, send_sem, recv_sem):
        cps = []
        for t in range(n):
            for r, (cx, cy) in enumerate(_other_chips(x, y)):
                cps.append(pltpu.make_async_remote_copy(
                    src_ref=src[t].at[2 * cx + cy], dst_ref=dst[t].at[r],
                    send_sem=send_sem.at[3 * t + r], recv_sem=recv_sem.at[3 * t + r],
                    device_id=(cx, cy, c), device_id_type=MESH))
        for cp in cps:
            cp.start()
        for cp in cps:
            cp.wait()

    _sequencer(name, ID_CHIPS, 3 * n, lambda x, y, c: [(cx, cy, c) for cx, cy in _other_chips(x, y)], program)
    return [r[...] for r in src], [r[...] for r in dst]


def chip_add(name, place, s, got, k, n_slots, prev=None, after=None):
    _, R, C = s.shape

    def body(place_ref, s_ref, got_ref, *rest):
        o_ref = rest[-1]
        o_ref[...] = ((s_ref[...].astype(F32) + got_ref[0].astype(F32)) + got_ref[1].astype(F32)) + got_ref[2].astype(F32)

    in_specs = [pl.BlockSpec((None, R, C), lambda i, place_ref: (place_ref[0], 0, 0)),
                pl.BlockSpec((3, R, C), lambda i, place_ref: (0, 0, 0))]
    args = [place, s, got]
    aliases = {}
    if prev is not None:
        in_specs.append(_ANY)
        args.append(prev)
        aliases = {3: 0}
    if after is not None:
        in_specs.append(pl.BlockSpec((8, 128), lambda i, place_ref: (0, 0)))
        args.append(after)
    return pl.pallas_call(
        body, name=name,
        grid_spec=pltpu.PrefetchScalarGridSpec(
            num_scalar_prefetch=1, grid=(1,), in_specs=in_specs,
            out_specs=pl.BlockSpec((None, None, R, C), lambda i, place_ref: (k, place_ref[1], 0, 0))),
        out_shape=jax.ShapeDtypeStruct((n_slots, 2, R, C), F32),
        input_output_aliases=aliases,
        compiler_params=_params("arbitrary"),
    )(*args)


def share_halves(name, stacks, slots):
    n = len(stacks)
    dst = [_hbm_ref(s) for s in stacks]

    def program(x, y, c, send_sem, recv_sem):
        cps = [pltpu.make_async_remote_copy(
            src_ref=dst[t].at[slots[t], c], dst_ref=dst[t].at[slots[t], c],
            send_sem=send_sem.at[t], recv_sem=recv_sem.at[t],
            device_id=(x, y, 1 - c), device_id_type=MESH) for t in range(n)]
        for cp in cps:
            cp.start()
        for cp in cps:
            cp.wait()

    _sequencer(name, ID_SHARE, n, lambda x, y, c: [(x, y, 1 - c)], program)
    return [r[...] for r in dst]


def gather_blocks(name, slotted):
    out = _hbm_ref(slotted)

    def program(x, y, c, send_sem, recv_sem):
        sibling = (x, y, 1 - c)
        chips = _other_chips(x, y)

        def copy(k, px, py, pc, to):
            block = out.at[4 * px + 2 * py + pc]
            return pltpu.make_async_remote_copy(src_ref=block, dst_ref=block, send_sem=send_sem.at[k],
                                                recv_sem=recv_sem.at[k], device_id=to, device_id_type=MESH)

        first = [copy(0, x, y, c, sibling)] + [copy(1 + j, x, y, c, (cx, cy, c)) for j, (cx, cy) in enumerate(chips)]
        for cp in first:
            cp.start()
        passed = [copy(4 + j, cx, cy, c, sibling) for j, (cx, cy) in enumerate(chips)]
        for j, (cx, cy) in enumerate(chips):
            copy(1 + j, cx, cy, c, (x, y, c)).wait_recv()
            passed[j].start()
        copy(0, x, y, 1 - c, (x, y, c)).wait_recv()
        for j, (cx, cy) in enumerate(chips):
            copy(4 + j, cx, cy, 1 - c, (x, y, c)).wait_recv()
        for cp in first + passed:
            cp.wait_send()

    _sequencer(name, ID_UKV, 7, lambda x, y, c: [(x, y, 1 - c)] + [(cx, cy, c) for cx, cy in _other_chips(x, y)],
               program)
    return out[...]


def place_block(name, dev, a):
    M, N = a.shape
    tr = min(M, 64)

    def body(dev_ref, a_ref, o_ref):
        o_ref[...] = a_ref[...]

    return pl.pallas_call(
        body, name=name,
        grid_spec=pltpu.PrefetchScalarGridSpec(
            num_scalar_prefetch=1, grid=(M // tr,),
            in_specs=[pl.BlockSpec((tr, N), lambda i, dev_ref: (i, 0))],
            out_specs=pl.BlockSpec((None, tr, N), lambda i, dev_ref: (dev_ref[0], i, 0))),
        out_shape=jax.ShapeDtypeStruct((N_DEV, M, N), a.dtype),
        compiler_params=_params("parallel"),
    )(dev, a)


def _swap_rope(a):
    return jnp.concatenate([a[..., ROPE // 2:], a[..., :ROPE // 2]], axis=-1)


def _rope_tables(S):
    inv = 1.0 / (ROPE_THETA ** (jnp.arange(0, ROPE, 2, dtype=F32) / ROPE))
    ang = jnp.arange(S, dtype=F32)[:, None] * inv[None, :]
    cos, sin = jnp.cos(ang), jnp.sin(ang)
    return (jnp.tile(jnp.concatenate([cos, cos], axis=1), (1, SLOTS)),
            jnp.tile(jnp.concatenate([-sin, sin], axis=1), (1, SLOTS)))


def _vec(norm_g, mod, i, k):
    rows = [norm_g[i, 2 * k], norm_g[i, 2 * k + 1], mod[i, 3 * k], mod[i, 3 * k + 1], mod[i, 3 * k + 2]]
    return jnp.concatenate([jnp.stack(rows), jnp.zeros((3, D), F32)], axis=0)


def _unpack_weights(full, w_uk, w_uv, q_norm, kv_norm):
    G = D // 4
    ffn_in = [[full[2 * i + k].reshape(N_CHIP, D, FSH) for k in range(2)] for i in range(2)]
    ffn_out = [[full[4 + 2 * i + k].reshape(2, FSH, D) for k in range(2)] for i in range(2)]
    pw = full[8].reshape(N_CHIP, 4, G // N_CHIP, G).transpose(1, 0, 2, 3).reshape(4, G, G)
    w_in = full[9].reshape(D, QL + KVL + ROPE)
    w_uq = full[10].reshape(QL, N_HEADS, NOPE + ROPE)
    wkr = w_in[:, QL + KVL:]
    wr = w_uq[:, :, NOPE:]
    eye2 = jnp.eye(2, dtype=BF16)
    uk_t = jnp.transpose(w_uk, (1, 2, 0)).reshape(N_PAIR, 2, NOPE, KVL)
    bduk = jnp.einsum("janc,ab->janbc", uk_t, eye2).reshape(N_PAIR, 2 * NOPE, 2 * KVL)
    uv = jnp.transpose(w_uv, (1, 0, 2)).reshape(N_PAIR, 2, KVL, VH)
    bduv = jnp.einsum("jacn,ab->jacbn", uv, eye2).reshape(N_PAIR, 2 * KVL, 2 * VH)
    mw = dict(wq=w_in[:, :QL], wkv=w_in[:, QL:QL + KVL], wkr4=jnp.tile(wkr, (1, SLOTS)),
              wkrs4=jnp.tile(_swap_rope(wkr), (1, SLOTS)), qn=q_norm, kvn=kv_norm,
              wn=w_uq[:, :, :NOPE].reshape(QL, NOPE_ALL), wr=wr.reshape(QL, ROPE_ALL),
              wrs=_swap_rope(wr).reshape(QL, ROPE_ALL), bduk=bduk)
    return ffn_in, ffn_out, pw, mw, bduv, full[11].reshape(D, D)


def _example_step(x, target, mod, norm_g, pvec, ffn_in, ffn_out, pw, mw, bduv, wo, reducer):
    S = x.shape[0]
    tabs = _rope_tables(S)
    vec = [[_vec(norm_g, mod, i, k) for k in range(3)] for i in range(2)]
    saved = {}
    for i in range(2):
        xin = x
        x, a, u, h = ffn_fwd(xin, vec[i][0], ffn_in[i][0], ffn_out[i][0], 0.5)
        saved[i, 0] = (xin, a, u, h)
        xin = x
        if i == 0:
            x, y, z = pool_fwd(xin, vec[i][1], pw, pvec)
            saved[i, 1] = (xin, y, z)
        else:
            h_m, cq_raw, ckv_raw, cqn, qnope, qcat, kcat, vcat = mla_pre(xin, vec[i][1], mw, tabs)
            olat, lse = attn_fwd(qcat, kcat, vcat)
            x, u_m, ocat = mla_post(olat, xin, vec[i][1], bduv, wo)
            saved[i, 1] = (xin, h_m, cq_raw, ckv_raw, cqn, qnope, qcat, kcat, olat, lse, u_m, ocat)
        xin = x
        x, a, u, h = ffn_fwd(xin, vec[i][2], ffn_in[i][1], ffn_out[i][1], 0.5)
        saved[i, 2] = (xin, a, u, h)
    loss, dx = loss_head(x, target)

    vg = {}
    G = D // 4

    def ffn_grads(i, k, dw_in, dw_out):
        return [(0, 2 * i + k, 4, dw_in.reshape(N_CHIP, 2, D // 2, FSH)),
                (1, 2 * i + k, 4, dw_out.reshape(N_CHIP, 2, DFF // 8, D))]

    for i in (1, 0):
        xin, a, u, h = saved[i, 2]
        dx, du, act, da, vg[i, 2] = ffn_bwd(dx, xin, u, a, vec[i][2], ffn_in[i][1], ffn_out[i][1], 0.5)
        reducer.advance()
        reducer.add(f"f{i}1", ffn_grads(i, 1, *ffn_dw(h, da, act, du)))
        if i == 0:
            xin, y, z = saved[i, 1]
            dx, vg[i, 1], pgrad, g_pool = pool_bwd(dx, xin, y, z, vec[i][1], pw, pvec)
            reducer.advance()
        else:
            xin, h_m, cq_raw, ckv_raw, cqn, qnope, qcat, kcat, olat, lse, u_m, ocat = saved[i, 1]
            du, docat, dolat, delta, vg_post = mla_post_bwd(dx, u_m, olat, vec[i][1], bduv, wo)
            reducer.advance()
            dq, dk, dv = attn_bwd(qcat, kcat, kcat.T, dolat, lse, delta)
            reducer.advance()
            dx, dlat, dql, dqcat, vg_pre, ngrad = mla_pre_bwd(
                dx, dq, dk, dv, xin, cq_raw, ckv_raw, vec[i][1], mw, tabs)
            vg[i, 1] = vg_post + vg_pre
            g = mla_dw(h_m, dlat, cqn, dqcat, dql, qnope, olat, docat, ocat, du)
            slots = lambda a: a.reshape(D, SLOTS, ROPE).sum(axis=1)
            g_kr = slots(g["in"][:, QL + KVL:QL + KVL + 128]) + _swap_rope(slots(g["in"][:, QL + KVL + 128:]))
            g_in = jnp.concatenate([g["in"][:, :QL + KVL], g_kr], axis=1)
            g_r = g["q"][:, NOPE_ALL:NOPE_ALL + ROPE_ALL].reshape(QL, N_HEADS, ROPE)
            g_rs = g["q"][:, NOPE_ALL + ROPE_ALL:].reshape(QL, N_HEADS, ROPE)
            g_uq = jnp.concatenate([g["q"][:, :NOPE_ALL].reshape(QL, N_HEADS, NOPE), g_r + _swap_rope(g_rs)], axis=-1)

            def heads(pairs):
                blk = pairs.reshape(N_PAIR, 2, KVL, 2, NOPE)
                per_head = jnp.stack([blk[:, 0, :, 0, :], blk[:, 1, :, 1, :]], axis=1).reshape(N_HEADS, KVL, NOPE)
                return jnp.transpose(per_head, (1, 0, 2)).reshape(KVL, N_HEADS * NOPE)

            reducer.add("mla", [(3, 0, 1, g_in.reshape(N_CHIP, 2, D // 8, QL + KVL + ROPE)),
                                (4, 0, 1, g_uq.reshape(N_CHIP, 2, QL // 8, N_HEADS * (NOPE + ROPE))),
                                (5, 0, 1, g["o"].reshape(N_CHIP, 2, D // 8, D))])
            reducer.add_replicated(jnp.concatenate([heads(g["uk"]), heads(g["uv"])], axis=0))
        xin, a, u, h = saved[i, 0]
        dx, du, act, da, vg[i, 0] = ffn_bwd(dx, xin, u, a, vec[i][0], ffn_in[i][0], ffn_out[i][0], 0.5)
        if i == 1:
            reducer.advance()
        grads = ffn_grads(i, 0, *ffn_dw(h, da, act, du))
        if i == 0:
            grads.append((2, 0, 1, g_pool.reshape(N_CHIP, 2, 2 * G // N_CHIP, G)))
        reducer.add(f"f{i}0", grads)
    return loss, dx, vg, pgrad, ngrad


class _GradReducer:
    def __init__(self, core, place, dev):
        self.core, self.place, self.dev = core, place, dev
        self.stacks = {}
        self.live = []
        self.replicated = None

    def add(self, tag, items):
        gen = self._run(tag, items)
        next(gen)
        self.live.append(gen)

    def add_replicated(self, block):
        self.replicated = gather_blocks("gather_ukv", place_block("place_ukv", self.dev, block))

    def advance(self, after=None):
        self.after = after
        live = []
        for gen in self.live:
            try:
                next(gen)
                live.append(gen)
            except StopIteration:
                pass
        self.live = live

    def finish(self):
        while self.live:
            self.advance()
        return self.stacks, self.replicated

    def _run(self, tag, items):
        grads, from_pair = reduce_pair(f"reduce_pair_{tag}", [g for *_, g in items])
        yield
        sums = [pair_add(f"pair_add_{tag}_{j}", self.core, g, p) for j, (g, p) in enumerate(zip(grads, from_pair))]
        sums, from_chips = reduce_chips(f"reduce_chips_{tag}", sums)
        yield
        for j, ((o, k, n_slots, _), s, p) in enumerate(zip(items, sums, from_chips)):
            self.stacks[o] = chip_add(f"chip_add_{tag}_{j}", self.place, s, p, k, n_slots, self.stacks.get(o),
                                      self.after)
        shared = share_halves(f"share_halves_{tag}", [self.stacks[o] for o, *_ in items], [k for _, k, *_ in items])
        for (o, *_), v in zip(items, shared):
            self.stacks[o] = v


SMALL_IN = 8 * 640
SMALL_GRAD = 8 * 4224
SMALL_W = 8 * 2944


def _pack(parts, total):
    flat = jnp.concatenate([p.reshape(-1) for p in parts])
    return jnp.concatenate([flat, jnp.zeros((total - flat.shape[0],), F32)]).reshape(8, total // 8)


def kernel(x, c, ada_w, ada_b, norm_g, ffn_w_in, ffn_w_out, pool_w, pool_b, pool_scale, mla_w_in, mla_q_norm, mla_kv_norm, mla_w_uq, mla_w_uk, mla_w_uv, mla_w_o, loss_target, m_ada_w, m_ada_b, m_norm_g, m_ffn_w_in, m_ffn_w_out, m_pool_w, m_pool_b, m_pool_scale, m_mla_w_in, m_mla_q_norm, m_mla_kv_norm, m_mla_w_uq, m_mla_w_uk, m_mla_w_uv, m_mla_w_o, v_ada_w, v_ada_b, v_norm_g, v_ffn_w_in, v_ffn_w_out, v_pool_w, v_pool_b, v_pool_scale, v_mla_w_in, v_mla_q_norm, v_mla_kv_norm, v_mla_w_uq, v_mla_w_uk, v_mla_w_uv, v_mla_w_o):
    ix, iy, ic = _place()
    chip = 2 * ix + iy
    dev = 2 * chip + ic
    core_arr = ic.astype(jnp.int32).reshape(1)
    chip_arr = chip.astype(jnp.int32).reshape(1)
    S = x.shape[1]
    G = D // 4
    NG = D // N_CHIP

    def chip_cols(a, width, axis):
        return lax.dynamic_slice_in_dim(a, chip * width, width, axis)

    got = gather_devices("gather_small_in", _pack([c, norm_g, pool_b, mla_q_norm], SMALL_IN)).reshape(N_DEV, SMALL_IN)
    c_all = got[:, :D]
    parts = got[0::2]
    o = D
    norm_g_full = parts[:, o:o + 12 * NG].reshape(N_CHIP, 2, 6, NG).transpose(1, 2, 0, 3).reshape(2, 6, D)
    o += 12 * NG
    pool_b_full = parts[:, o:o + G].reshape(N_CHIP, 4, G // N_CHIP).transpose(1, 0, 2).reshape(1, D)
    o += G
    q_norm_full = parts[:, o:o + QL // N_CHIP].reshape(1, QL)
    pvec = jnp.concatenate([pool_b_full, pool_scale, jnp.zeros((6, D), F32)], axis=0)

    c_pad = jnp.concatenate([c_all, jnp.zeros((8, D), F32)], axis=0)
    mod_loc = mod_fwd(c_pad, ada_w, chip_cols(ada_b, MOD_COLS, 1).reshape(2, 1, MOD_COLS))
    got = gather_devices("gather_mod", mod_loc[:, :8].transpose(1, 0, 2).reshape(8, 2 * MOD_COLS))
    mine = lax.dynamic_index_in_dim(got[0::2].reshape(N_CHIP, 8, 2, MOD_COLS), dev, axis=1, keepdims=False)
    mod = mine.transpose(1, 0, 2).reshape(2, 9, D)

    bf = lambda a: a.astype(BF16)
    w_in_halves = ffn_w_in.reshape(2, 2, 2, D // 2, FSH)
    w_out_halves = ffn_w_out.reshape(2, 2, 2, DFF // 8, D)
    shards = [(w_in_halves, (i, k)) for i in range(2) for k in range(2)]
    shards += [(w_out_halves, (i, k)) for i in range(2) for k in range(2)]
    shards += [(pool_w.reshape(2, 2 * G // N_CHIP, G), ()), (mla_w_in.reshape(2, D // 8, QL + KVL + ROPE), ()),
               (mla_w_uq.reshape(2, QL // 8, N_HEADS * (NOPE + ROPE)), ()), (mla_w_o.reshape(2, D // 8, D), ())]
    full = [None] * len(shards)
    stages = [(0, 4, 8), (1, 5), (2, 6), (9, 10, 11), (3, 7)]
    first, token = cast_into_slots("cast_first", chip_arr, [shards[t] for t in stages[0]])
    slotted = dict(zip(stages[0], first))
    rest = [t for members in stages[1:] for t in members]
    for stage, members in enumerate(stages):
        got_w = gather_weights(f"gather_weights_{stage}", stage, [slotted[t] for t in members])
        for t, a in zip(members, got_w):
            full[t] = a
        if stage == 0:
            slotted.update(zip(rest, cast_into_slots("cast_rest", chip_arr, [shards[t] for t in rest], token)[0]))
    ffn_in, ffn_out, pw, mw, bduv, wo = _unpack_weights(full, bf(mla_w_uk[0]), bf(mla_w_uv[0]), q_norm_full,
                                                        mla_kv_norm)

    place_arr = jnp.stack([chip, ic]).astype(jnp.int32)
    reducer = _GradReducer(core_arr, place_arr, dev.astype(jnp.int32).reshape(1))
    loss_mine, grad_x, vg, pgrad, ngrad = _example_step(
        x[0], loss_target[0], mod, norm_g_full, pvec, ffn_in, ffn_out, pw, mw, bduv, wo, reducer)

    dmod = jnp.stack([jnp.concatenate([vg[i, k][0:3] for k in range(3)]) for i in range(2)])
    dnorm = jnp.stack([jnp.concatenate([vg[i, k][3:5] for k in range(3)]) for i in range(2)])
    small = _pack([dmod, dnorm, pgrad[0], pgrad[1], ngrad[0], ngrad[1, :KVL], loss_mine], SMALL_GRAD)
    got = gather_devices("gather_small_grad", small)
    tot = sum_devices("sum_small_grad", got).reshape(-1)
    n_mod = 2 * 9 * D
    g_ada_b = tot[:n_mod].reshape(ada_b.shape)
    o = n_mod
    g_norm = chip_cols(tot[o:o + 12 * D].reshape(2, 6, D), NG, 2)
    o += 12 * D
    g_pool_b = chip_cols(tot[o:o + D].reshape(1, 4, G), G // N_CHIP, 2)
    o += D
    g_pool_scale = tot[o:o + D].reshape(pool_scale.shape)
    o += D
    g_q_norm = chip_cols(tot[o:o + QL].reshape(1, QL), QL // N_CHIP, 1)
    o += QL
    g_kv_norm = tot[o:o + KVL].reshape(mla_kv_norm.shape)
    loss = tot[o + KVL]
    dmod_all = chip_cols(got.reshape(N_DEV, -1)[:, :n_mod].reshape(N_DEV, 2, 9 * D), MOD_COLS, 2)
    dmod_pad = jnp.concatenate([dmod_all.transpose(1, 0, 2), jnp.zeros((2, 8, MOD_COLS), F32)], axis=1)

    g_ada_w, d_ada_w, nm_ada_w, nv_ada_w = adamw_ada(c_pad, dmod_pad, ada_w, m_ada_w, v_ada_w)
    small_names = ["ada_b", "norm_g", "pool_b", "pool_scale", "mla_q_norm", "mla_kv_norm"]
    small_w = [ada_b, norm_g, pool_b, pool_scale, mla_q_norm, mla_kv_norm]
    small_g = [g_ada_b, g_norm, g_pool_b, g_pool_scale, g_q_norm, g_kv_norm]
    small_m = [m_ada_b, m_norm_g, m_pool_b, m_pool_scale, m_mla_q_norm, m_mla_kv_norm]
    small_v = [v_ada_b, v_norm_g, v_pool_b, v_pool_scale, v_mla_q_norm, v_mla_kv_norm]
    packed = adamw("adamw_small", *[_pack(p, SMALL_W) for p in (small_w, small_g, small_m, small_v)])
    upd = {}
    o = 0
    for name, w in zip(small_names, small_w):
        upd[name] = [p.reshape(-1)[o:o + w.size].reshape(w.shape) for p in packed]
        o += w.size
    upd["ada_w"] = [d_ada_w, nm_ada_w, nv_ada_w]

    reducer.advance(after=d_ada_w[0, :8, :128])
    ukv = sum_devices("sum_ukv", reducer.replicated)
    g_uk = ukv[:KVL].reshape(mla_w_uk.shape)
    g_uv = ukv[KVL:].reshape(mla_w_uv.shape)
    g_mla_in = reducer.stacks[3].reshape(mla_w_in.shape)
    g_uq = reducer.stacks[4].reshape(mla_w_uq.shape)
    g_wo = reducer.stacks[5].reshape(mla_w_o.shape)
    for name, w, g, m, v in [("mla_w_in", mla_w_in, g_mla_in, m_mla_w_in, v_mla_w_in),
                             ("mla_w_uq", mla_w_uq, g_uq, m_mla_w_uq, v_mla_w_uq),
                             ("mla_w_uk", mla_w_uk, g_uk, m_mla_w_uk, v_mla_w_uk),
                             ("mla_w_uv", mla_w_uv, g_uv, m_mla_w_uv, v_mla_w_uv),
                             ("mla_w_o", mla_w_o, g_wo, m_mla_w_o, v_mla_w_o)]:
        upd[name] = adamw("adamw_" + name, w, g, m, v)
    ffn = [("ffn_w_in", 0, ffn_w_in, m_ffn_w_in, v_ffn_w_in), ("ffn_w_out", 1, ffn_w_out, m_ffn_w_out, v_ffn_w_out)]
    slots = lambda a: a.reshape((4,) + a.shape[2:])
    early = {name: adamw(f"adamw_{name}_early", slots(w), slots(reducer.stacks[o].reshape(w.shape)), slots(m),
                         slots(v), part=(1, 3)) for name, o, w, m, v in ffn}

    reducer.advance(after=early["ffn_w_out"][0][1, :8, :128])
    stacks, _ = reducer.finish()
    g_ffn_in = stacks[0].reshape(ffn_w_in.shape)
    g_ffn_out = stacks[1].reshape(ffn_w_out.shape)
    g_pool_w = stacks[2].reshape(pool_w.shape)
    for name, o, w, m, v in ffn:
        done = adamw(f"adamw_{name}_last", slots(w), slots(stacks[o].reshape(w.shape)), slots(m), slots(v),
                     part=(0, 1), prev=early[name])
        upd[name] = [p.reshape(w.shape) for p in done]
    upd["pool_w"] = adamw("adamw_pool_w", pool_w, g_pool_w, m_pool_w, v_pool_w)

    order = ["ada_w", "ada_b", "norm_g", "ffn_w_in", "ffn_w_out", "pool_w", "pool_b", "pool_scale", "mla_w_in",
             "mla_q_norm", "mla_kv_norm", "mla_w_uq", "mla_w_uk", "mla_w_uv", "mla_w_o"]
    grad = dict(ada_w=g_ada_w, ada_b=g_ada_b, norm_g=g_norm, ffn_w_in=g_ffn_in, ffn_w_out=g_ffn_out, pool_w=g_pool_w,
                pool_b=g_pool_b, pool_scale=g_pool_scale, mla_w_in=g_mla_in, mla_q_norm=g_q_norm,
                mla_kv_norm=g_kv_norm, mla_w_uq=g_uq, mla_w_uk=g_uk, mla_w_uv=g_uv, mla_w_o=g_wo)
    return (loss, grad_x[None], *[grad[n] for n in order], *[upd[n][0] for n in order],
            *[upd[n][1] for n in order], *[upd[n][2] for n in order])
```

```python
import functools

import jax
import jax.numpy as jnp
from jax import lax
from jax.experimental import pallas as pl
from jax.experimental.pallas import tpu as pltpu
from jax.experimental.pallas import tpu_sc as plsc

F32 = jnp.float32
BF16 = jnp.bfloat16

D = 1024
DFF = 2816
FSH = 1408
N_CHIP = 4
N_DEV = 8
N_HEADS = 16
NOPE = 64
ROPE = 32
VH = 64
QL = 256
KVL = 128
QPAD = 256
EPS = 1e-6
ATTN_SCALE = (NOPE + ROPE) ** -0.5
ROPE_THETA = 10000.0
POOL_WINDOWS = (2, 4, 8, 16)
HALO = 8
ATTN_TQ = 1024
ATTN_KC = 512
ROW_TILE = 512
DW_TK = 2048

ADAM_LR, ADAM_B1, ADAM_B2, ADAM_EPS, ADAM_WD, ADAM_STEP = 0.001, 0.9, 0.999, 1e-08, 0.01, 10

VMEM_LIMIT = 60 * 1024 * 1024
MESH = pl.DeviceIdType.MESH

NT = (((1,), (1,)), ((), ()))
TN = (((0,), (0,)), ((), ()))


def _params(*sem):
    return pltpu.CompilerParams(dimension_semantics=sem, vmem_limit_bytes=VMEM_LIMIT)


def _dot(a, b, dims=None):
    if dims is None:
        return jnp.dot(a, b, preferred_element_type=F32)
    return lax.dot_general(a, b, dims, preferred_element_type=F32)


def _rms(x):
    r = lax.rsqrt(jnp.mean(x * x, axis=-1, keepdims=True) + EPS)
    return x * r, r


def _rms_bwd(xhat, r, dxhat):
    return r * (dxhat - xhat * jnp.mean(dxhat * xhat, axis=-1, keepdims=True))


def _as_row(col):
    return jnp.broadcast_to(col, (col.shape[0], 128)).T[0:1, :]


def _prenorm(x, vec_ref):
    xhat, r = _rms(x)
    h = xhat * vec_ref[0:1, :] * (1.0 + vec_ref[3:4, :]) + vec_ref[2:3, :]
    return h, xhat, r


def _postnorm_bwd(dout, u, vec_ref, weight):
    uhat, r = _rms(u)
    gt = weight * (1.0 + vec_ref[4:5, :])
    dy = dout * gt
    dgate_rows = (weight * dout) * (uhat * vec_ref[1:2, :])
    dgpost_rows = dy * uhat
    du = _rms_bwd(uhat, r, dy * vec_ref[1:2, :])
    return du, dgate_rows, dgpost_rows


def _prenorm_bwd(dh, x, vec_ref, vg_ref):
    xhat, r = _rms(x)
    sc1 = 1.0 + vec_ref[3:4, :]
    g = vec_ref[0:1, :]
    vg_ref[0:1, :] += jnp.sum(dh, axis=0, keepdims=True)
    vg_ref[1:2, :] += jnp.sum(dh * (xhat * g), axis=0, keepdims=True)
    vg_ref[3:4, :] += jnp.sum(dh * sc1 * xhat, axis=0, keepdims=True)
    return _rms_bwd(xhat, r, dh * g * sc1)


def ffn_fwd(x, vec, w_in, w_out, weight):
    S = x.shape[0]
    tm = min(512, S)
    row = lambda i: (i, 0)
    half = lambda j: [_w3((8, D)), pl.BlockSpec((None, D, FSH), lambda i: (j, 0, 0)),
                      pl.BlockSpec((None, D, FSH), lambda i: (j + 2, 0, 0)),
                      pl.BlockSpec((None, FSH, D), lambda i: (j, 0, 0))]
    a_spec = lambda j: pl.BlockSpec((2, tm, FSH), lambda i: (0, i, j))
    a_shape = jax.ShapeDtypeStruct((2, S, DFF), BF16)

    def hidden(hb, wg_ref, wu_ref, wo_ref, a_ref):
        g = _dot(hb, wg_ref[...])
        up = _dot(hb, wu_ref[...])
        a_ref[0] = g.astype(BF16)
        a_ref[1] = up.astype(BF16)
        act = (g * jax.nn.sigmoid(g)) * up
        return _dot(act.astype(BF16), wo_ref[...])

    def first(x_ref, vec_ref, wg_ref, wu_ref, wo_ref, h_ref, a_ref, u_ref):
        h, _, _ = _prenorm(x_ref[...], vec_ref)
        hb = h.astype(BF16)
        h_ref[...] = hb
        u_ref[...] = hidden(hb, wg_ref, wu_ref, wo_ref, a_ref)

    h, a, u_half = pl.pallas_call(
        first, name="ffn_fwd_first", grid=(S // tm,),
        in_specs=[pl.BlockSpec((tm, D), row)] + half(0),
        out_specs=[pl.BlockSpec((tm, D), row), a_spec(0), pl.BlockSpec((tm, D), row)],
        out_shape=[jax.ShapeDtypeStruct((S, D), BF16), a_shape, jax.ShapeDtypeStruct((S, D), F32)],
        compiler_params=_params("parallel"),
    )(x, vec, w_in, w_in, w_out)

    def second(x_ref, h_ref, uh_ref, vec_ref, wg_ref, wu_ref, wo_ref, a_in, xo_ref, a_ref, u_ref):
        u = uh_ref[...] + hidden(h_ref[...], wg_ref, wu_ref, wo_ref, a_ref)
        u_ref[...] = u
        uhat, _ = _rms(u)
        xo_ref[...] = x_ref[...] + (weight * (1.0 + vec_ref[4:5, :])) * (uhat * vec_ref[1:2, :])

    xo, a, u = pl.pallas_call(
        second, name="ffn_fwd_second", grid=(S // tm,),
        in_specs=[pl.BlockSpec((tm, D), row), pl.BlockSpec((tm, D), row), pl.BlockSpec((tm, D), row)] + half(1) + [_ANY],
        out_specs=[pl.BlockSpec((tm, D), row), a_spec(1), pl.BlockSpec((tm, D), row)],
        out_shape=[jax.ShapeDtypeStruct((S, D), F32), a_shape, jax.ShapeDtypeStruct((S, D), F32)],
        input_output_aliases={7: 1},
        compiler_params=_params("parallel"),
    )(x, h, u_half, vec, w_in, w_in, w_out, a)
    return xo, a, u, h


def ffn_bwd(dout, x, u, a, vec, w_in, w_out, weight):
    S = x.shape[0]
    tm = min(512, S)
    row = lambda i: (i, 0)
    half = lambda j: [pl.BlockSpec((2, tm, FSH), lambda i: (0, i, j)), _w3((8, D)),
                      pl.BlockSpec((None, D, FSH), lambda i: (j, 0, 0)),
                      pl.BlockSpec((None, D, FSH), lambda i: (j + 2, 0, 0)),
                      pl.BlockSpec((None, FSH, D), lambda i: (j, 0, 0))]
    half_out = lambda j: [pl.BlockSpec((tm, FSH), lambda i: (i, j)), pl.BlockSpec((2, tm, FSH), lambda i: (0, i, j))]
    half_shape = [jax.ShapeDtypeStruct((S, DFF), BF16), jax.ShapeDtypeStruct((2, S, DFF), BF16)]

    def hidden_bwd(du, a_ref, wg_ref, wu_ref, wo_ref, act_ref, da_ref):
        dact = _dot(du, wo_ref[...], NT)
        g = a_ref[0].astype(F32)
        up = a_ref[1].astype(F32)
        s = jax.nn.sigmoid(g)
        silu = g * s
        act_ref[...] = (silu * up).astype(BF16)
        dg = (dact * up * (s * (1.0 + g * (1.0 - s)))).astype(BF16)
        dup = (dact * silu).astype(BF16)
        da_ref[0] = dg
        da_ref[1] = dup
        return _dot(dg, wg_ref[...], NT) + _dot(dup, wu_ref[...], NT)

    def first(do_ref, u_ref, a_ref, vec_ref, wg_ref, wu_ref, wo_ref, du_ref, dh_ref, act_ref, da_ref, vg_ref):
        @pl.when(pl.program_id(0) == 0)
        def _():
            vg_ref[...] = jnp.zeros_like(vg_ref)

        du, dgate_rows, dgpost_rows = _postnorm_bwd(do_ref[...], u_ref[...], vec_ref, weight)
        vg_ref[2:3, :] += jnp.sum(dgate_rows, axis=0, keepdims=True)
        vg_ref[4:5, :] += jnp.sum(dgpost_rows, axis=0, keepdims=True)
        du = du.astype(BF16)
        du_ref[...] = du
        dh_ref[...] = hidden_bwd(du, a_ref, wg_ref, wu_ref, wo_ref, act_ref, da_ref)

    du, dh, act, da, vg_post = pl.pallas_call(
        first, name="ffn_bwd_first", grid=(S // tm,),
        in_specs=[pl.BlockSpec((tm, D), row), pl.BlockSpec((tm, D), row)] + half(0),
        out_specs=[pl.BlockSpec((tm, D), row), pl.BlockSpec((tm, D), row)] + half_out(0) + [_w3((8, D))],
        out_shape=[jax.ShapeDtypeStruct((S, D), BF16), jax.ShapeDtypeStruct((S, D), F32)] + half_shape
        + [jax.ShapeDtypeStruct((8, D), F32)],
        compiler_params=_params("arbitrary"),
    )(dout, u, a, vec, w_in, w_in, w_out)

    def second(do_ref, x_ref, du_ref, dh_ref, a_ref, vec_ref, wg_ref, wu_ref, wo_ref, act_in, da_in,
               dx_ref, act_ref, da_ref, vg_ref):
        @pl.when(pl.program_id(0) == 0)
        def _():
            vg_ref[...] = jnp.zeros_like(vg_ref)

        dh = dh_ref[...] + hidden_bwd(du_ref[...], a_ref, wg_ref, wu_ref, wo_ref, act_ref, da_ref)
        dx_ref[...] = do_ref[...] + _prenorm_bwd(dh, x_ref[...], vec_ref, vg_ref)

    dx, act, da, vg_pre = pl.pallas_call(
        second, name="ffn_bwd_second", grid=(S // tm,),
        in_specs=[pl.BlockSpec((tm, D), row), pl.BlockSpec((tm, D), row), pl.BlockSpec((tm, D), row),
                  pl.BlockSpec((tm, D), row)] + half(1) + [_ANY, _ANY],
        out_specs=[pl.BlockSpec((tm, D), row)] + half_out(1) + [_w3((8, D))],
        out_shape=[jax.ShapeDtypeStruct((S, D), F32)] + half_shape + [jax.ShapeDtypeStruct((8, D), F32)],
        input_output_aliases={9: 1, 10: 2},
        compiler_params=_params("arbitrary"),
    )(dout, x, du, dh, a, vec, w_in, w_in, w_out, act, da)
    return dx, du, act, da, vg_post + vg_pre


def dw_matmul(name, a, b, a_spec, b_spec, out_shape, out_spec, grid):
    def body(a_ref, b_ref, o_ref):
        @pl.when(pl.program_id(len(grid) - 1) == 0)
        def _():
            o_ref[...] = jnp.zeros_like(o_ref)

        o_ref[...] += _dot(a_ref[...], b_ref[...], TN)

    return pl.pallas_call(
        body, name=name, grid=grid, in_specs=[a_spec, b_spec], out_specs=out_spec,
        out_shape=jax.ShapeDtypeStruct(out_shape, F32),
        compiler_params=_params(*(["parallel"] * (len(grid) - 1) + ["arbitrary"])),
    )(a, b)


def ffn_dw(h, da, act, du):
    S = h.shape[0]
    tk = min(DW_TK, S)
    dw_in = dw_matmul("ffn_dw_in", h, da,
                      pl.BlockSpec((tk, D), lambda n, k: (k, 0)),
                      pl.BlockSpec((None, tk, FSH), lambda n, k: (n // 2, k, n % 2)),
                      (N_CHIP, D, FSH), pl.BlockSpec((None, D, FSH), lambda n, k: (n, 0, 0)),
                      (N_CHIP, S // tk))
    dw_out = dw_matmul("ffn_dw_out", act, du,
                       pl.BlockSpec((tk, FSH), lambda n, k: (k, n)),
                       pl.BlockSpec((tk, D), lambda n, k: (k, 0)),
                       (DFF, D), pl.BlockSpec((FSH, D), lambda n, k: (n, 0)),
                       (2, S // tk))
    return dw_in, dw_out


def _halo_specs(tm, S):
    nb = tm // HALO
    last = S // HALO - 1
    return [pl.BlockSpec((HALO, D), lambda i: (jnp.maximum(i * nb - 1, 0), 0)),
            pl.BlockSpec((tm, D), lambda i: (i, 0)),
            pl.BlockSpec((HALO, D), lambda i: (jnp.minimum((i + 1) * nb, last), 0))]


def _shift_rows(v, k):
    return pltpu.roll(v, k % v.shape[0], 0)


def _window_sum(v, g, forward):
    acc = v + _shift_rows(v, 1 if forward else -1)
    for step in (1, 2, 4)[:g]:
        acc = _shift_rows(acc, step) + _shift_rows(acc, -step)
    return acc


def _pool_count(t, w, S):
    return jnp.maximum(jnp.minimum(t + w // 2, S) - jnp.maximum(t - w // 2, 0), 1).astype(F32)


def pool_fwd(x, vec, pw, pvec):
    S = x.shape[0]
    tm = min(ROW_TILE, S)
    G = D // 4

    def body(xp_ref, x_ref, xn_ref, vec_ref, pw_ref, pv_ref, xo_ref, y_ref, z_ref):
        i = pl.program_id(0)
        xa = jnp.concatenate([xp_ref[...], x_ref[...], xn_ref[...]], axis=0)
        t = i * tm - HALO + lax.broadcasted_iota(jnp.int32, (tm + 2 * HALO, 1), 0)
        h, _, _ = _prenorm(xa, vec_ref)
        h = jnp.where((t >= 0) & (t < S), h, 0.0)
        tmain = t[HALO:HALO + tm]
        for g in range(4):
            hg = h[:, g * G:(g + 1) * G]
            pooled = _window_sum(hg, g, True)[HALO:HALO + tm] / _pool_count(tmain, POOL_WINDOWS[g], S)
            z = (pooled - hg[HALO:HALO + tm]).astype(BF16)
            z_ref[:, g * G:(g + 1) * G] = z
            y_ref[:, g * G:(g + 1) * G] = _dot(z, pw_ref[g]) + pv_ref[0:1, g * G:(g + 1) * G]
        u = y_ref[...] * pv_ref[1:2, :]
        uhat, _ = _rms(u)
        xo_ref[...] = x_ref[...] + (1.0 + vec_ref[4:5, :]) * (uhat * vec_ref[1:2, :])

    row = lambda i: (i, 0)
    full = lambda i: (0, 0)
    return pl.pallas_call(
        body, name="pool_fwd", grid=(S // tm,),
        in_specs=_halo_specs(tm, S) + [pl.BlockSpec((8, D), full), pl.BlockSpec((4, G, G), lambda i: (0, 0, 0)),
                                       pl.BlockSpec((8, D), full)],
        out_specs=[pl.BlockSpec((tm, D), row)] * 3,
        out_shape=[jax.ShapeDtypeStruct((S, D), F32), jax.ShapeDtypeStruct((S, D), F32),
                   jax.ShapeDtypeStruct((S, D), BF16)],
        compiler_params=_params("parallel"),
    )(x, x, x, vec, pw, pvec)


def pool_bwd(dout, x, y, z, vec, pw, pvec):
    S = x.shape[0]
    tm = min(ROW_TILE, S)
    G = D // 4
    R = G // N_CHIP

    def body(dop_ref, do_ref, don_ref, yp_ref, y_ref, yn_ref, x_ref, z_ref, vec_ref, pw_ref, pv_ref,
             dx_ref, vg_ref, pg_ref, dw_ref, dh_ref):
        i = pl.program_id(0)

        @pl.when(i == 0)
        def _():
            vg_ref[...] = jnp.zeros_like(vg_ref)
            pg_ref[...] = jnp.zeros_like(pg_ref)
            dw_ref[...] = jnp.zeros_like(dw_ref)

        doa = jnp.concatenate([dop_ref[...], do_ref[...], don_ref[...]], axis=0)
        ya = jnp.concatenate([yp_ref[...], y_ref[...], yn_ref[...]], axis=0)
        t = i * tm - HALO + lax.broadcasted_iota(jnp.int32, (tm + 2 * HALO, 1), 0)
        inside = (t >= 0) & (t < S)
        main = (t >= i * tm) & (t < (i + 1) * tm)
        du, dgate_rows, dgpost_rows = _postnorm_bwd(doa, ya * pv_ref[1:2, :], vec_ref, 1.0)
        du = jnp.where(inside, du, 0.0)
        vg_ref[2:3, :] += jnp.sum(jnp.where(main, dgate_rows, 0.0), axis=0, keepdims=True)
        vg_ref[4:5, :] += jnp.sum(jnp.where(main, dgpost_rows, 0.0), axis=0, keepdims=True)
        dy = du * pv_ref[1:2, :]
        pg_ref[0:1, :] += jnp.sum(jnp.where(main, dy, 0.0), axis=0, keepdims=True)
        pg_ref[1:2, :] += jnp.sum(jnp.where(main, du * ya, 0.0), axis=0, keepdims=True)
        for g in range(4):
            dyg = dy[:, g * G:(g + 1) * G].astype(BF16)
            dz = _dot(dyg, pw_ref[g], NT)
            e = dz / _pool_count(t, POOL_WINDOWS[g], S)
            dh_ref[:, g * G:(g + 1) * G] = (_window_sum(e, g, False) - dz)[HALO:HALO + tm]
            dwg = _dot(z_ref[:, g * G:(g + 1) * G], dyg[HALO:HALO + tm], TN)
            for q in range(N_CHIP):
                dw_ref[q, g] += dwg[q * R:(q + 1) * R, :]
        dx_ref[...] = do_ref[...] + _prenorm_bwd(dh_ref[...], x_ref[...], vec_ref, vg_ref)

    row = lambda i: (i, 0)
    full = lambda i: (0, 0)
    halo = _halo_specs(tm, S)
    return pl.pallas_call(
        body, name="pool_bwd", grid=(S // tm,),
        in_specs=halo + halo + [pl.BlockSpec((tm, D), row), pl.BlockSpec((tm, D), row), pl.BlockSpec((8, D), full),
                                pl.BlockSpec((4, G, G), lambda i: (0, 0, 0)), pl.BlockSpec((8, D), full)],
        out_specs=[pl.BlockSpec((tm, D), row), pl.BlockSpec((8, D), full), pl.BlockSpec((8, D), full),
                   pl.BlockSpec((N_CHIP, 4, R, G), lambda i: (0, 0, 0, 0))],
        out_shape=[jax.ShapeDtypeStruct((S, D), F32), jax.ShapeDtypeStruct((8, D), F32),
                   jax.ShapeDtypeStruct((8, D), F32), jax.ShapeDtypeStruct((N_CHIP, 4, R, G), F32)],
        scratch_shapes=[pltpu.VMEM((tm, D), F32)],
        compiler_params=_params("arbitrary"),
    )(dout, dout, dout, y, y, y, x, z, vec, pw, pvec)


N_PAIR = N_HEADS // 2
SLOTS = 128 // ROPE
ROPE_ALL = N_HEADS * ROPE
NOPE_ALL = N_HEADS * NOPE
LAT_ALL = N_HEADS * KVL
DLAT = QL + KVL + 2 * 128
DQ_ALL = NOPE_ALL + 2 * ROPE_ALL


def _w3(shape):
    return pl.BlockSpec(shape, lambda i: (0,) * len(shape))


def _slot_mask(hd, rows):
    lane = lax.broadcasted_iota(jnp.int32, (rows, 128), 1)
    return (lane // ROPE) == (hd % SLOTS)


MLA_WEIGHTS = ("wq", "wkv", "wkr4", "wkrs4", "qn", "kvn", "wn", "wr", "wrs", "bduk")


def _mla_weight_specs():
    return [_w3((D, QL)), _w3((D, KVL)), _w3((D, 128)), _w3((D, 128)), _w3((1, QL)), _w3((1, KVL)),
            _w3((QL, NOPE_ALL)), _w3((QL, ROPE_ALL)), _w3((QL, ROPE_ALL)), _w3((N_PAIR, 2 * NOPE, 2 * KVL))]


def mla_pre(x, vec, mw, tabs):
    S = x.shape[0]
    tm = min(ROW_TILE, S)

    def body(x_ref, vec_ref, cos_ref, sin_ref, wq_ref, wkv_ref, wkr_ref, wkrs_ref, qn_ref, kvn_ref,
             wn_ref, wr_ref, wrs_ref, bduk_ref,
             h_ref, cq_ref, ckv_ref, cqn_ref, qnope_ref, qcat_ref, kcat_ref, vcat_ref):
        h, _, _ = _prenorm(x_ref[...], vec_ref)
        hb = h.astype(BF16)
        h_ref[...] = hb
        cq_raw = _dot(hb, wq_ref[...])
        ckv_raw = _dot(hb, wkv_ref[...])
        cq_ref[...] = cq_raw
        ckv_ref[...] = ckv_raw
        cos, sin = cos_ref[...], sin_ref[...]
        ckv = (_rms(ckv_raw)[0] * kvn_ref[...]).astype(BF16)
        kcat_ref[:, 0:KVL] = ckv
        kcat_ref[:, KVL:] = (_dot(hb, wkr_ref[...]) * cos + _dot(hb, wkrs_ref[...]) * sin).astype(BF16)
        vcat_ref[:, 0:KVL] = ckv
        ones = lax.broadcasted_iota(jnp.int32, (tm, QPAD - KVL), 1) == 0
        vcat_ref[:, KVL:] = jnp.where(ones, 1.0, 0.0).astype(BF16)
        cqb = (_rms(cq_raw)[0] * qn_ref[...]).astype(BF16)
        cqn_ref[...] = cqb
        qn = _dot(cqb, wn_ref[...]).astype(BF16)
        qnope_ref[...] = qn
        cos4, sin4 = jnp.tile(cos, (1, SLOTS)), jnp.tile(sin, (1, SLOTS))
        qr = ((_dot(cqb, wr_ref[...]) * cos4 + _dot(cqb, wrs_ref[...]) * sin4) * ATTN_SCALE).astype(BF16)
        for j in range(N_PAIR):
            ql = (_dot(qn[:, 128 * j:128 * (j + 1)], bduk_ref[j]) * ATTN_SCALE).astype(BF16)
            for hd in (2 * j, 2 * j + 1):
                qcat_ref[hd, :, 0:KVL] = ql[:, KVL * (hd - 2 * j):KVL * (hd - 2 * j + 1)]
                group = qr[:, 128 * (hd // SLOTS):128 * (hd // SLOTS + 1)]
                qcat_ref[hd, :, KVL:] = jnp.where(_slot_mask(hd, tm), group, jnp.zeros_like(group))

    row = lambda i: (i, 0)
    hrow = lambda i: (0, i, 0)
    return pl.pallas_call(
        body, name="mla_pre", grid=(S // tm,),
        in_specs=[pl.BlockSpec((tm, D), row), _w3((8, D)), pl.BlockSpec((tm, 128), row), pl.BlockSpec((tm, 128), row)]
        + _mla_weight_specs(),
        out_specs=[pl.BlockSpec((tm, D), row), pl.BlockSpec((tm, QL), row), pl.BlockSpec((tm, KVL), row),
                   pl.BlockSpec((tm, QL), row), pl.BlockSpec((tm, NOPE_ALL), row),
                   pl.BlockSpec((N_HEADS, tm, QPAD), hrow), pl.BlockSpec((tm, QPAD), row),
                   pl.BlockSpec((tm, QPAD), row)],
        out_shape=[jax.ShapeDtypeStruct((S, D), BF16), jax.ShapeDtypeStruct((S, QL), F32),
                   jax.ShapeDtypeStruct((S, KVL), F32), jax.ShapeDtypeStruct((S, QL), BF16),
                   jax.ShapeDtypeStruct((S, NOPE_ALL), BF16), jax.ShapeDtypeStruct((N_HEADS, S, QPAD), BF16),
                   jax.ShapeDtypeStruct((S, QPAD), BF16), jax.ShapeDtypeStruct((S, QPAD), BF16)],
        compiler_params=_params("parallel"),
    )(x, vec, tabs[0], tabs[1], *[mw[k] for k in MLA_WEIGHTS])


def attn_fwd(qcat, kcat, vcat):
    S = kcat.shape[0]
    tq = min(ATTN_TQ, S)
    kc = min(ATTN_KC, S)

    def body(q_ref, k_ref, v_ref, o_ref, lse_ref):
        q = q_ref[...]
        m = jnp.full((tq, 1), -jnp.inf, F32)
        ov = jnp.zeros((tq, QPAD), F32)
        for c in range(S // kc):
            s = _dot(q, k_ref[c * kc:(c + 1) * kc, :], NT)
            m_new = jnp.maximum(m, jnp.max(s, axis=-1, keepdims=True))
            p = jnp.exp(s - m_new).astype(BF16)
            ov = ov * jnp.exp(m - m_new) + _dot(p, v_ref[c * kc:(c + 1) * kc, :])
            m = m_new
        l = ov[:, KVL:KVL + 1]
        o_ref[...] = (ov[:, 0:KVL] * (1.0 / l)).astype(BF16)
        lse_ref[...] = _as_row(m + jnp.log(l))

    return pl.pallas_call(
        body, name="attn_fwd", grid=(N_HEADS, S // tq),
        in_specs=[pl.BlockSpec((None, tq, QPAD), lambda h, i: (h, i, 0)),
                  pl.BlockSpec((S, QPAD), lambda h, i: (0, 0)),
                  pl.BlockSpec((S, QPAD), lambda h, i: (0, 0))],
        out_specs=[pl.BlockSpec((tq, KVL), lambda h, i: (i, h)),
                   pl.BlockSpec((None, 1, tq), lambda h, i: (h, 0, i))],
        out_shape=[jax.ShapeDtypeStruct((S, LAT_ALL), BF16), jax.ShapeDtypeStruct((N_HEADS, 1, S), F32)],
        compiler_params=_params("parallel", "parallel"),
    )(qcat, kcat, vcat)


def mla_post(olat, x, vec, bduv, wo):
    S = x.shape[0]
    tm = min(ROW_TILE, S)

    def body(o_ref, x_ref, vec_ref, bduv_ref, wo_ref, xo_ref, u_ref, ocat_ref):
        for j in range(N_PAIR):
            oc = _dot(o_ref[:, 2 * KVL * j:2 * KVL * (j + 1)], bduv_ref[j])
            ocat_ref[:, 2 * VH * j:2 * VH * (j + 1)] = oc.astype(BF16)
        u = _dot(ocat_ref[...], wo_ref[...])
        u_ref[...] = u
        uhat, _ = _rms(u)
        xo_ref[...] = x_ref[...] + (1.0 + vec_ref[4:5, :]) * (uhat * vec_ref[1:2, :])

    row = lambda i: (i, 0)
    return pl.pallas_call(
        body, name="mla_post", grid=(S // tm,),
        in_specs=[pl.BlockSpec((tm, LAT_ALL), row), pl.BlockSpec((tm, D), row), _w3((8, D)),
                  _w3((N_PAIR, 2 * KVL, 2 * VH)), _w3((D, D))],
        out_specs=[pl.BlockSpec((tm, D), row), pl.BlockSpec((tm, D), row), pl.BlockSpec((tm, D), row)],
        out_shape=[jax.ShapeDtypeStruct((S, D), F32), jax.ShapeDtypeStruct((S, D), F32),
                   jax.ShapeDtypeStruct((S, D), BF16)],
        compiler_params=_params("parallel"),
    )(olat, x, vec, bduv, wo)


def mla_post_bwd(dout, u, olat, vec, bduv, wo):
    S = u.shape[0]
    tm = min(ROW_TILE, S)

    def body(do_ref, u_ref, o_ref, vec_ref, bduv_ref, wo_ref, du_ref, docat_ref, dolat_ref, delta_ref, vg_ref):
        @pl.when(pl.program_id(0) == 0)
        def _():
            vg_ref[...] = jnp.zeros_like(vg_ref)

        du, dgate_rows, dgpost_rows = _postnorm_bwd(do_ref[...], u_ref[...], vec_ref, 1.0)
        vg_ref[2:3, :] += jnp.sum(dgate_rows, axis=0, keepdims=True)
        vg_ref[4:5, :] += jnp.sum(dgpost_rows, axis=0, keepdims=True)
        dub = du.astype(BF16)
        du_ref[...] = dub
        docat_ref[...] = _dot(dub, wo_ref[...], NT).astype(BF16)
        for j in range(N_PAIR):
            dol = _dot(docat_ref[:, 2 * VH * j:2 * VH * (j + 1)], bduv_ref[j], NT).astype(BF16)
            dolat_ref[:, 2 * KVL * j:2 * KVL * (j + 1)] = dol
            prod = dol.astype(F32) * o_ref[:, 2 * KVL * j:2 * KVL * (j + 1)].astype(F32)
            delta_ref[2 * j] = _as_row(jnp.sum(prod[:, 0:KVL], axis=-1, keepdims=True))
            delta_ref[2 * j + 1] = _as_row(jnp.sum(prod[:, KVL:], axis=-1, keepdims=True))

    row = lambda i: (i, 0)
    hrow = lambda i: (0, i, 0)
    return pl.pallas_call(
        body, name="mla_post_bwd", grid=(S // tm,),
        in_specs=[pl.BlockSpec((tm, D), row), pl.BlockSpec((tm, D), row), pl.BlockSpec((tm, LAT_ALL), row),
                  _w3((8, D)), _w3((N_PAIR, 2 * KVL, 2 * VH)), _w3((D, D))],
        out_specs=[pl.BlockSpec((tm, D), row), pl.BlockSpec((tm, D), row),
                   pl.BlockSpec((tm, LAT_ALL), row), pl.BlockSpec((N_HEADS, 1, tm), lambda i: (0, 0, i)), _w3((8, D))],
        out_shape=[jax.ShapeDtypeStruct((S, D), BF16), jax.ShapeDtypeStruct((S, D), BF16),
                   jax.ShapeDtypeStruct((S, LAT_ALL), BF16), jax.ShapeDtypeStruct((N_HEADS, 1, S), F32),
                   jax.ShapeDtypeStruct((8, D), F32)],
        compiler_params=_params("arbitrary"),
    )(dout, u, olat, vec, bduv, wo)


def attn_bwd(qcat, kcat, kcat_t, dolat, lse_row, delta_row):
    S = kcat.shape[0]
    tq = min(ATTN_TQ, S)
    kc = min(ATTN_KC, S)

    def body(q_ref, k_ref, kt_ref, do_ref, lse_ref, dl_ref, dq_ref, dk_ref, dv_ref):
        @pl.when((pl.program_id(0) == 0) & (pl.program_id(1) == 0))
        def _():
            dk_ref[...] = jnp.zeros_like(dk_ref)
            dv_ref[...] = jnp.zeros_like(dv_ref)

        q, do = q_ref[...], do_ref[...]
        lse, dl = lse_ref[...], dl_ref[...]
        dqt = jnp.zeros((QPAD, tq), F32)
        for c in range(S // kc):
            rows = slice(c * kc, (c + 1) * kc)
            st = _dot(k_ref[rows, :], q, NT)
            pt = jnp.exp(st - lse)
            dpt = _dot(k_ref[rows, 0:KVL], do, NT)
            dst = (pt * (dpt - dl)).astype(BF16)
            dv_ref[rows, :] += _dot(pt.astype(BF16), do)
            dk_ref[rows, :] += _dot(dst, q)
            dqt = dqt + _dot(kt_ref[:, rows], dst)
        dq_ref[...] = dqt.T

    return pl.pallas_call(
        body, name="attn_bwd", grid=(N_HEADS, S // tq),
        in_specs=[pl.BlockSpec((None, tq, QPAD), lambda h, i: (h, i, 0)),
                  pl.BlockSpec((S, QPAD), lambda h, i: (0, 0)),
                  pl.BlockSpec((QPAD, S), lambda h, i: (0, 0)),
                  pl.BlockSpec((tq, KVL), lambda h, i: (i, h)),
                  pl.BlockSpec((None, 1, tq), lambda h, i: (h, 0, i)),
                  pl.BlockSpec((None, 1, tq), lambda h, i: (h, 0, i))],
        out_specs=[pl.BlockSpec((None, tq, QPAD), lambda h, i: (h, i, 0)),
                   pl.BlockSpec((S, QPAD), lambda h, i: (0, 0)),
                   pl.BlockSpec((S, KVL), lambda h, i: (0, 0))],
        out_shape=[jax.ShapeDtypeStruct((N_HEADS, S, QPAD), F32), jax.ShapeDtypeStruct((S, QPAD), F32),
                   jax.ShapeDtypeStruct((S, KVL), F32)],
        compiler_params=_params("arbitrary", "arbitrary"),
    )(qcat, kcat, kcat_t, dolat, lse_row, delta_row)


def mla_pre_bwd(dout, dq, dk, dv, x, cq_raw, ckv_raw, vec, mw, tabs):
    S = x.shape[0]
    tm = min(ROW_TILE, S)

    def body(do_ref, dq_ref, dk_ref, dv_ref, x_ref, cq_ref, ckv_ref, vec_ref, cos_ref, sin_ref,
             wq_ref, wkv_ref, wkr_ref, wkrs_ref, qn_ref, kvn_ref, wn_ref, wr_ref, wrs_ref, bduk_ref,
             dx_ref, dlat_ref, dql_ref, dqcat_ref, vg_ref, ng_ref):
        @pl.when(pl.program_id(0) == 0)
        def _():
            vg_ref[...] = jnp.zeros_like(vg_ref)
            ng_ref[...] = jnp.zeros_like(ng_ref)

        cos, sin = cos_ref[...], sin_ref[...]
        for j in range(N_PAIR):
            dql = jnp.concatenate([dq_ref[2 * j, :, 0:KVL], dq_ref[2 * j + 1, :, 0:KVL]], axis=1) * ATTN_SCALE
            dql = dql.astype(BF16)
            dql_ref[:, 2 * KVL * j:2 * KVL * (j + 1)] = dql
            dqcat_ref[:, 2 * NOPE * j:2 * NOPE * (j + 1)] = _dot(dql, bduk_ref[j], NT).astype(BF16)
        groups = []
        for grp in range(N_HEADS // SLOTS):
            acc = jnp.zeros((tm, 128), F32)
            for hd in range(SLOTS * grp, SLOTS * (grp + 1)):
                acc = acc + jnp.where(_slot_mask(hd, tm), dq_ref[hd, :, KVL:], 0.0)
            groups.append(acc)
        dqr = jnp.concatenate(groups, axis=1) * ATTN_SCALE
        qa = (dqr * jnp.tile(cos, (1, SLOTS))).astype(BF16)
        qb = (dqr * jnp.tile(sin, (1, SLOTS))).astype(BF16)
        dqcat_ref[:, NOPE_ALL:NOPE_ALL + ROPE_ALL] = qa
        dqcat_ref[:, NOPE_ALL + ROPE_ALL:] = qb
        dcq = _dot(dqcat_ref[:, 0:NOPE_ALL], wn_ref[...], NT) + _dot(qa, wr_ref[...], NT) + _dot(qb, wrs_ref[...], NT)
        cqh, rq = _rms(cq_ref[...])
        ng_ref[0:1, :] += jnp.sum(dcq * cqh, axis=0, keepdims=True)
        dcq_raw = _rms_bwd(cqh, rq, dcq * qn_ref[...]).astype(BF16)
        dckv = dk_ref[:, 0:KVL] + dv_ref[...]
        ckvh, rk = _rms(ckv_ref[...])
        ng_ref[1:2, 0:KVL] += jnp.sum(dckv * ckvh, axis=0, keepdims=True)
        dckv_raw = _rms_bwd(ckvh, rk, dckv * kvn_ref[...]).astype(BF16)
        dkr = dk_ref[:, KVL:]
        ka = (dkr * cos).astype(BF16)
        kb = (dkr * sin).astype(BF16)
        dlat_ref[:, 0:QL] = dcq_raw
        dlat_ref[:, QL:QL + KVL] = dckv_raw
        dlat_ref[:, QL + KVL:QL + KVL + 128] = ka
        dlat_ref[:, QL + KVL + 128:] = kb
        dh = (_dot(dcq_raw, wq_ref[...], NT) + _dot(dckv_raw, wkv_ref[...], NT)
              + _dot(ka, wkr_ref[...], NT) + _dot(kb, wkrs_ref[...], NT))
        dx_ref[...] = do_ref[...] + _prenorm_bwd(dh, x_ref[...], vec_ref, vg_ref)

    row = lambda i: (i, 0)
    hrow = lambda i: (0, i, 0)
    return pl.pallas_call(
        body, name="mla_pre_bwd", grid=(S // tm,),
        in_specs=[pl.BlockSpec((tm, D), row), pl.BlockSpec((N_HEADS, tm, QPAD), hrow), pl.BlockSpec((tm, QPAD), row),
                  pl.BlockSpec((tm, KVL), row), pl.BlockSpec((tm, D), row), pl.BlockSpec((tm, QL), row),
                  pl.BlockSpec((tm, KVL), row), _w3((8, D)), pl.BlockSpec((tm, 128), row), pl.BlockSpec((tm, 128), row)]
        + _mla_weight_specs(),
        out_specs=[pl.BlockSpec((tm, D), row), pl.BlockSpec((tm, DLAT), row), pl.BlockSpec((tm, LAT_ALL), row),
                   pl.BlockSpec((tm, DQ_ALL), row), _w3((8, D)), _w3((8, QL))],
        out_shape=[jax.ShapeDtypeStruct((S, D), F32), jax.ShapeDtypeStruct((S, DLAT), BF16),
                   jax.ShapeDtypeStruct((S, LAT_ALL), BF16), jax.ShapeDtypeStruct((S, DQ_ALL), BF16),
                   jax.ShapeDtypeStruct((8, D), F32), jax.ShapeDtypeStruct((8, QL), F32)],
        compiler_params=_params("arbitrary"),
    )(dout, dq, dk, dv, x, cq_raw, ckv_raw, vec, tabs[0], tabs[1], *[mw[k] for k in MLA_WEIGHTS])


def mla_dw(h, dlat, cqn, dqcat, dql, qnope, olat, docat, ocat, du):
    S = h.shape[0]
    tk = min(DW_TK, S)
    nk = S // tk
    flat = lambda w: pl.BlockSpec((tk, w), lambda k: (k, 0))
    cols = lambda w: pl.BlockSpec((tk, w), lambda n, k: (k, n))
    pair_o = pl.BlockSpec((None, 2 * KVL, 128), lambda n, k: (n, 0, 0))
    g = {}
    g["in"] = dw_matmul("mla_dw_in", h, dlat, flat(D), flat(DLAT), (D, DLAT),
                        pl.BlockSpec((D, DLAT), lambda k: (0, 0)), (nk,))
    g["q"] = dw_matmul("mla_dw_q", cqn, dqcat, flat(QL), flat(DQ_ALL), (QL, DQ_ALL),
                       pl.BlockSpec((QL, DQ_ALL), lambda k: (0, 0)), (nk,))
    g["uk"] = dw_matmul("mla_dw_uk", dql, qnope, cols(2 * KVL), cols(2 * NOPE), (N_PAIR, 2 * KVL, 2 * NOPE), pair_o,
                        (N_PAIR, nk))
    g["uv"] = dw_matmul("mla_dw_uv", olat, docat, cols(2 * KVL), cols(2 * VH), (N_PAIR, 2 * KVL, 2 * VH), pair_o,
                        (N_PAIR, nk))
    g["o"] = dw_matmul("mla_dw_o", ocat, du, cols(256), pl.BlockSpec((tk, D), lambda n, k: (k, 0)), (D, D),
                       pl.BlockSpec((256, D), lambda n, k: (n, 0)), (D // 256, nk))
    return g


def loss_head(y, target):
    S = y.shape[0]
    tm = min(512, S)

    def body(y_ref, t_ref, loss_ref, dy_ref):
        @pl.when(pl.program_id(0) == 0)
        def _():
            loss_ref[...] = jnp.zeros_like(loss_ref)

        err = y_ref[...] - t_ref[...]
        dy_ref[...] = err * (1.0 / D)
        loss_ref[...] += 0.5 * jnp.sum(jnp.mean(err * err, axis=-1, keepdims=True), axis=0, keepdims=True)

    row = lambda i: (i, 0)
    return pl.pallas_call(
        body, name="loss_head", grid=(S // tm,),
        in_specs=[pl.BlockSpec((tm, D), row), pl.BlockSpec((tm, D), row)],
        out_specs=[pl.BlockSpec((1, 1), lambda i: (0, 0)), pl.BlockSpec((tm, D), row)],
        out_shape=[jax.ShapeDtypeStruct((1, 1), F32), jax.ShapeDtypeStruct((S, D), F32)],
        compiler_params=_params("arbitrary"),
    )(y, target)


MOD_COLS = 9 * D // N_CHIP


def mod_fwd(c_pad, ada_w, ada_b_loc):
    tn = MOD_COLS // 3

    def body(c_ref, w_ref, b_ref, o_ref):
        c = c_ref[...]
        sc = (c * jax.nn.sigmoid(c)).astype(BF16)
        o_ref[...] = _dot(sc, w_ref[...].astype(BF16)) + b_ref[...]

    return pl.pallas_call(
        body, name="mod_fwd", grid=(2, 3),
        in_specs=[pl.BlockSpec((16, D), lambda i, n: (0, 0)), pl.BlockSpec((None, D, tn), lambda i, n: (i, 0, n)),
                  pl.BlockSpec((None, 1, tn), lambda i, n: (i, 0, n))],
        out_specs=pl.BlockSpec((None, 16, tn), lambda i, n: (i, 0, n)),
        out_shape=jax.ShapeDtypeStruct((2, 16, MOD_COLS), F32),
        compiler_params=_params("parallel", "parallel"),
    )(c_pad, ada_w, ada_b_loc)


def _adamw_math(w, g, m, v):
    m = ADAM_B1 * m + (1.0 - ADAM_B1) * g
    v = ADAM_B2 * v + (1.0 - ADAM_B2) * (g * g)
    m_hat = m / (1.0 - ADAM_B1 ** ADAM_STEP)
    v_hat = v / (1.0 - ADAM_B2 ** ADAM_STEP)
    delta = -ADAM_LR * (m_hat / (jnp.sqrt(v_hat) + ADAM_EPS) + ADAM_WD * w)
    return delta, m, v


def adamw(name, w, g, m, v, part=None, prev=None, copy_grad=False):
    shape = w.shape
    if part is None and w.size * 4 <= (1 << 20):
        whole = pl.BlockSpec(shape, lambda i: (0,) * len(shape))

        def small_body(w_ref, g_ref, m_ref, v_ref, d_ref, mo_ref, vo_ref):
            d_ref[...], mo_ref[...], vo_ref[...] = _adamw_math(w_ref[...], g_ref[...], m_ref[...], v_ref[...])

        return pl.pallas_call(
            small_body, name=name, grid=(1,), in_specs=[whole] * 4, out_specs=[whole] * 3,
            out_shape=[jax.ShapeDtypeStruct(shape, F32)] * 3, compiler_params=_params("arbitrary"),
        )(w, g, m, v)
    cols = shape[-1]
    rows = w.size // cols
    per_entry = rows // shape[0] if part is not None else rows
    tr = per_entry
    budget_rows = (2 << 20) // (cols * 4)
    for cand in range(min(per_entry, budget_rows) // 8 * 8, 0, -8):
        if per_entry % cand == 0:
            tr = cand
            break
    first, count = part if part is not None else (0, 1)
    tiles = per_entry // tr

    n_out = 4 if copy_grad else 3

    def body(w_ref, g_ref, m_ref, v_ref, *rest):
        outs = rest[-n_out:]
        outs[0][...], outs[1][...], outs[2][...] = _adamw_math(w_ref[...], g_ref[...], m_ref[...], v_ref[...])
        if copy_grad:
            outs[3][...] = g_ref[...]

    spec = pl.BlockSpec((tr, cols), lambda i: (i + first * tiles, 0))
    operands = [a.reshape(rows, cols) for a in (w, g, m, v)]
    aliases = {}
    if prev is not None:
        operands += [p.reshape(rows, cols) for p in prev]
        aliases = {4 + t: t for t in range(n_out)}
    outs = pl.pallas_call(
        body, name=name, grid=(count * tiles,), in_specs=[spec] * 4 + [_ANY] * (len(operands) - 4),
        out_specs=[spec] * n_out, out_shape=[jax.ShapeDtypeStruct((rows, cols), F32)] * n_out,
        input_output_aliases=aliases, compiler_params=_params("parallel"),
    )(*operands)
    return [o.reshape(shape) for o in outs]


def adamw_ada(c_pad, dmod, w, m, v):
    tr = 256

    def body(c_ref, dm_ref, w_ref, m_ref, v_ref, g_ref, d_ref, mo_ref, vo_ref):
        c = c_ref[...]
        sc = (c * jax.nn.sigmoid(c)).astype(BF16)
        g = _dot(sc, dm_ref[...].astype(BF16), TN)
        g_ref[...] = g
        d_ref[...], mo_ref[...], vo_ref[...] = _adamw_math(w_ref[...], g, m_ref[...], v_ref[...])

    wspec = pl.BlockSpec((None, tr, MOD_COLS), lambda i, r: (i, r, 0))
    return pl.pallas_call(
        body, name="adamw_ada", grid=(2, D // tr),
        in_specs=[pl.BlockSpec((16, tr), lambda i, r: (0, r)),
                  pl.BlockSpec((None, 16, MOD_COLS), lambda i, r: (i, 0, 0)), wspec, wspec, wspec],
        out_specs=[wspec] * 4,
        out_shape=[jax.ShapeDtypeStruct((2, D, MOD_COLS), F32)] * 4,
        compiler_params=_params("parallel", "parallel"),
    )(c_pad, dmod, w, m, v)


def sum_devices(name, a):
    _, R, C = a.shape
    tr = R
    for cand in (64, 32, 16, 8):
        if R % cand == 0:
            tr = cand
            break

    def body(a_ref, o_ref):
        acc = a_ref[0]
        for dev in range(1, N_DEV):
            acc = acc + a_ref[dev]
        o_ref[...] = acc

    return pl.pallas_call(
        body, name=name, grid=(R // tr,),
        in_specs=[pl.BlockSpec((N_DEV, tr, C), lambda i: (0, i, 0))],
        out_specs=pl.BlockSpec((tr, C), lambda i: (i, 0)),
        out_shape=jax.ShapeDtypeStruct((R, C), F32),
        compiler_params=_params("parallel"),
    )(a)


def _place():
    return lax.axis_index("x"), lax.axis_index("y"), lax.axis_index("c")


def _other_chips(x, y):
    return [(1 - x, y), (x, 1 - y), (1 - x, 1 - y)]


def gather_devices(name, a):
    m_per, n = a.shape

    def body(x_ref, out_ref, send_sems, recv_sems, local_sem):
        x, y, c = _place()
        me, sibling = (x, y, c), (x, y, 1 - c)
        chips = _other_chips(x, y)

        def rows(px, py, pc):
            return out_ref.at[pl.ds((4 * px + 2 * py + pc) * m_per, m_per), :]

        def copy(k, block, to, src=None):
            return pltpu.make_async_remote_copy(
                src_ref=rows(*block) if src is None else src, dst_ref=rows(*block),
                send_sem=send_sems.at[k], recv_sem=recv_sems.at[k], device_id=to, device_id_type=MESH)

        mine = pltpu.make_async_copy(x_ref, rows(*me), local_sem)
        mine.start()
        first = [copy(0, me, sibling, src=x_ref)]
        first += [copy(1 + j, me, (*chip, c), src=x_ref) for j, chip in enumerate(chips)]
        for cp in first:
            cp.start()
        passed = [copy(4 + j, (*chip, c), sibling) for j, chip in enumerate(chips)]
        for j, chip in enumerate(chips):
            copy(1 + j, (*chip, c), me).wait_recv()
            passed[j].start()
        copy(0, sibling, me).wait_recv()
        for j, chip in enumerate(chips):
            copy(4 + j, (*chip, 1 - c), me).wait_recv()
        for cp in first + passed:
            cp.wait_send()
        mine.wait()

    out = pl.pallas_call(
        body, name=name,
        out_shape=jax.ShapeDtypeStruct((N_DEV * m_per, n), a.dtype),
        in_specs=[pl.BlockSpec(memory_space=pltpu.VMEM)],
        out_specs=pl.BlockSpec(memory_space=pltpu.VMEM),
        scratch_shapes=[pltpu.SemaphoreType.DMA((7,)), pltpu.SemaphoreType.DMA((7,)), pltpu.SemaphoreType.DMA],
        compiler_params=pltpu.CompilerParams(vmem_limit_bytes=VMEM_LIMIT),
    )(a)
    return out.reshape(N_DEV, m_per, n)


_ANY = pl.BlockSpec(memory_space=pl.ANY)


def _hbm_ref(a):
    return jax.new_ref(a, memory_space=pltpu.MemorySpace.HBM)


def _hbm_empty(shape, dtype):
    return jax.empty_ref(jax.ShapeDtypeStruct(shape, dtype), memory_space=pltpu.MemorySpace.HBM)


ID_PAIR, ID_CHIPS, ID_SHARE, ID_UKV = 8, 9, 10, 11


def _sequencer(name, collective_id, n_sem, peers_of, program):
    sems = pltpu.SemaphoreType.DMA((n_sem,))

    @pl.kernel(mesh=plsc.ScalarSubcoreMesh(axis_name="seq", num_cores=1), name=name, scratch_types=[sems, sems],
               compiler_params=pltpu.CompilerParams(collective_id=collective_id))
    def launch(send_sem, recv_sem):
        x, y, c = _place()
        peers = peers_of(x, y, c)
        barrier = pltpu.get_barrier_semaphore()
        for peer in peers:
            pl.semaphore_signal(barrier, inc=1, device_id=peer, device_id_type=MESH)
        pl.semaphore_wait(barrier, len(peers))
        program(x, y, c, send_sem, recv_sem)

    launch()


def gather_weights(name, stage, arrays):
    n = len(arrays)
    refs = [_hbm_ref(a) for a in arrays]

    def program(x, y, c, send_sem, recv_sem):
        me = 2 * x + y
        chips = _other_chips(x, y)

        def ici(t, r, half):
            cx, cy = chips[r]
            mine = refs[t].at[me, half]
            return pltpu.make_async_remote_copy(
                src_ref=mine, dst_ref=mine, send_sem=send_sem.at[3 * t + r], recv_sem=recv_sem.at[3 * t + r],
                device_id=(cx, cy, c), device_id_type=MESH)

        def d2d(t, r, half):
            cx, cy = chips[r]
            there = refs[t].at[2 * cx + cy, half]
            k = 3 * n + 3 * t + r
            return pltpu.make_async_remote_copy(
                src_ref=there, dst_ref=there, send_sem=send_sem.at[k], recv_sem=recv_sem.at[k],
                device_id=(x, y, 1 - c), device_id_type=MESH)

        for t in range(n):
            for r in range(3):
                ici(t, r, c).start()
        for t in range(n):
            for r in range(3):
                ici(t, r, c).wait_recv()
                d2d(t, r, c).start()
        for t in range(n):
            for r in range(3):
                d2d(t, r, 1 - c).wait_recv()
        for t in range(n):
            for r in range(3):
                ici(t, r, c).wait_send()
                d2d(t, r, c).wait_send()

    _sequencer(name, stage, 6 * n, lambda x, y, c: [(x, y, 1 - c)] + [(cx, cy, c) for cx, cy in _other_chips(x, y)],
               program)
    return [r[...] for r in refs]


def cast_into_slots(name, chip, shards, after=None):
    steps = 2
    n = len(shards)

    def body(chip_ref, *refs):
        for src, dst in zip(refs[:n], refs[-n - 1:-1]):
            dst[...] = src[...].astype(BF16)
        refs[-1][...] = jnp.zeros_like(refs[-1])

    token_spec = pl.BlockSpec((8, 128), lambda h, i, chip_ref: (0, 0))

    def spec_in(a, prefix):
        R, C = a.shape[-2:]
        return pl.BlockSpec((None,) * (len(prefix) + 1) + (R // steps, C), lambda h, i, chip_ref: prefix + (h, i, 0))

    def spec_out(a):
        R, C = a.shape[-2:]
        return pl.BlockSpec((None, None, R // steps, C), lambda h, i, chip_ref: (chip_ref[0], h, i, 0))

    outs = pl.pallas_call(
        body, name=name,
        grid_spec=pltpu.PrefetchScalarGridSpec(
            num_scalar_prefetch=1, grid=(2, steps),
            in_specs=[spec_in(a, p) for a, p in shards] + ([token_spec] if after is not None else []),
            out_specs=[spec_out(a) for a, _ in shards] + [token_spec]),
        out_shape=[jax.ShapeDtypeStruct((N_CHIP, 2) + a.shape[-2:], BF16) for a, _ in shards]
        + [jax.ShapeDtypeStruct((8, 128), F32)],
        compiler_params=_params("arbitrary", "arbitrary"),
    )(chip, *[a for a, _ in shards], *([after] if after is not None else []))
    return outs[:-1], outs[-1]


def reduce_pair(name, grads):
    n = len(grads)
    src = [_hbm_ref(g) for g in grads]
    dst = [_hbm_empty((N_CHIP,) + g.shape[2:], g.dtype) for g in grads]

    def program(x, y, c, send_sem, recv_sem):
        cps = [pltpu.make_async_remote_copy(
            src_ref=src[t].at[:, 1 - c], dst_ref=dst[t], send_sem=send_sem.at[t], recv_sem=recv_sem.at[t],
            device_id=(x, y, 1 - c), device_id_type=MESH) for t in range(n)]
        for cp in cps:
            cp.start()
        for cp in cps:
            cp.wait()

    _sequencer(name, ID_PAIR, n, lambda x, y, c: [(x, y, 1 - c)], program)
    return [r[...] for r in src], [r[...] for r in dst]


def pair_add(name, core, g, got):
    _, _, R, C = g.shape

    def body(core_ref, g_ref, got_ref, o_ref):
        o_ref[...] = (g_ref[...] + got_ref[...]).astype(BF16)

    return pl.pallas_call(
        body, name=name,
        grid_spec=pltpu.PrefetchScalarGridSpec(
            num_scalar_prefetch=1, grid=(N_CHIP,),
            in_specs=[pl.BlockSpec((None, None, R, C), lambda q, core_ref: (q, core_ref[0], 0, 0)),
                      pl.BlockSpec((None, R, C), lambda q, core_ref: (q, 0, 0))],
            out_specs=pl.BlockSpec((None, R, C), lambda q, core_ref: (q, 0, 0))),
        out_shape=jax.ShapeDtypeStruct((N_CHIP, R, C), BF16),
        compiler_params=_params("parallel"),
    )(core, g, got)


def reduce_chips(name, sums):
    n = len(sums)
    src = [_hbm_ref(s) for s in sums]
    dst = [_hbm_empty((3,) + s.shape[1:], s.dtype) for s in sums]

    def program(x, y, c, send_sem, recv_sem):
        cps = []
        for t in range(n):
            for r, (cx, cy) in enumerate(_other_chips(x, y)):
                cps.append(pltpu.make_async_remote_copy(
                    src_ref=src[t].at[2 * cx + cy], dst_ref=dst[t].at[r],
                    send_sem=send_sem.at[3 * t + r], recv_sem=recv_sem.at[3 * t + r],
                    device_id=(cx, cy, c), device_id_type=MESH))
        for cp in cps:
            cp.start()
        for cp in cps:
            cp.wait()

    _sequencer(name, ID_CHIPS, 3 * n, lambda x, y, c: [(cx, cy, c) for cx, cy in _other_chips(x, y)], program)
    return [r[...] for r in src], [r[...] for r in dst]


def chip_add(name, place, s, got, k, n_slots, prev=None, after=None):
    _, R, C = s.shape

    def body(place_ref, s_ref, got_ref, *rest):
        o_ref = rest[-1]
        o_ref[...] = ((s_ref[...].astype(F32) + got_ref[0].astype(F32)) + got_ref[1].astype(F32)) + got_ref[2].astype(F32)

    in_specs = [pl.BlockSpec((None, R, C), lambda i, place_ref: (place_ref[0], 0, 0)),
                pl.BlockSpec((3, R, C), lambda i, place_ref: (0, 0, 0))]
    args = [place, s, got]
    aliases = {}
    if prev is not None:
        in_specs.append(_ANY)
        args.append(prev)
        aliases = {3: 0}
    if after is not None:
        in_specs.append(pl.BlockSpec((8, 128), lambda i, place_ref: (0, 0)))
        args.append(after)
    return pl.pallas_call(
        body, name=name,
        grid_spec=pltpu.PrefetchScalarGridSpec(
            num_scalar_prefetch=1, grid=(1,), in_specs=in_specs,
            out_specs=pl.BlockSpec((None, None, R, C), lambda i, place_ref: (k, place_ref[1], 0, 0))),
        out_shape=jax.ShapeDtypeStruct((n_slots, 2, R, C), F32),
        input_output_aliases=aliases,
        compiler_params=_params("arbitrary"),
    )(*args)


def share_halves(name, stacks, slots):
    n = len(stacks)
    dst = [_hbm_ref(s) for s in stacks]

    def program(x, y, c, send_sem, recv_sem):
        cps = [pltpu.make_async_remote_copy(
            src_ref=dst[t].at[slots[t], c], dst_ref=dst[t].at[slots[t], c],
            send_sem=send_sem.at[t], recv_sem=recv_sem.at[t],
            device_id=(x, y, 1 - c), device_id_type=MESH) for t in range(n)]
        for cp in cps:
            cp.start()
        for cp in cps:
            cp.wait()

    _sequencer(name, ID_SHARE, n, lambda x, y, c: [(x, y, 1 - c)], program)
    return [r[...] for r in dst]


def gather_blocks(name, slotted):
    out = _hbm_ref(slotted)

    def program(x, y, c, send_sem, recv_sem):
        sibling = (x, y, 1 - c)
        chips = _other_chips(x, y)

        def copy(k, px, py, pc, to):
            block = out.at[4 * px + 2 * py + pc]
            return pltpu.make_async_remote_copy(src_ref=block, dst_ref=block, send_sem=send_sem.at[k],
                                                recv_sem=recv_sem.at[k], device_id=to, device_id_type=MESH)

        first = [copy(0, x, y, c, sibling)] + [copy(1 + j, x, y, c, (cx, cy, c)) for j, (cx, cy) in enumerate(chips)]
        for cp in first:
            cp.start()
        passed = [copy(4 + j, cx, cy, c, sibling) for j, (cx, cy) in enumerate(chips)]
        for j, (cx, cy) in enumerate(chips):
            copy(1 + j, cx, cy, c, (x, y, c)).wait_recv()
            passed[j].start()
        copy(0, x, y, 1 - c, (x, y, c)).wait_recv()
        for j, (cx, cy) in enumerate(chips):
            copy(4 + j, cx, cy, 1 - c, (x, y, c)).wait_recv()
        for cp in first + passed:
            cp.wait_send()

    _sequencer(name, ID_UKV, 7, lambda x, y, c: [(x, y, 1 - c)] + [(cx, cy, c) for cx, cy in _other_chips(x, y)],
               program)
    return out[...]


def place_block(name, dev, a):
    M, N = a.shape
    tr = min(M, 64)

    def body(dev_ref, a_ref, o_ref):
        o_ref[...] = a_ref[...]

    return pl.pallas_call(
        body, name=name,
        grid_spec=pltpu.PrefetchScalarGridSpec(
            num_scalar_prefetch=1, grid=(M // tr,),
            in_specs=[pl.BlockSpec((tr, N), lambda i, dev_ref: (i, 0))],
            out_specs=pl.BlockSpec((None, tr, N), lambda i, dev_ref: (dev_ref[0], i, 0))),
        out_shape=jax.ShapeDtypeStruct((N_DEV, M, N), a.dtype),
        compiler_params=_params("parallel"),
    )(dev, a)


def _swap_rope(a):
    return jnp.concatenate([a[..., ROPE // 2:], a[..., :ROPE // 2]], axis=-1)


def _rope_tables(S):
    inv = 1.0 / (ROPE_THETA ** (jnp.arange(0, ROPE, 2, dtype=F32) / ROPE))
    ang = jnp.arange(S, dtype=F32)[:, None] * inv[None, :]
    cos, sin = jnp.cos(ang), jnp.sin(ang)
    return (jnp.tile(jnp.concatenate([cos, cos], axis=1), (1, SLOTS)),
            jnp.tile(jnp.concatenate([-sin, sin], axis=1), (1, SLOTS)))


def _vec(norm_g, mod, i, k):
    rows = [norm_g[i, 2 * k], norm_g[i, 2 * k + 1], mod[i, 3 * k], mod[i, 3 * k + 1], mod[i, 3 * k + 2]]
    return jnp.concatenate([jnp.stack(rows), jnp.zeros((3, D), F32)], axis=0)


def _unpack_weights(full, w_uk, w_uv, q_norm, kv_norm):
    G = D // 4
    ffn_in = [[full[2 * i + k].reshape(N_CHIP, D, FSH) for k in range(2)] for i in range(2)]
    ffn_out = [[full[4 + 2 * i + k].reshape(2, FSH, D) for k in range(2)] for i in range(2)]
    pw = full[8].reshape(N_CHIP, 4, G // N_CHIP, G).transpose(1, 0, 2, 3).reshape(4, G, G)
    w_in = full[9].reshape(D, QL + KVL + ROPE)
    w_uq = full[10].reshape(QL, N_HEADS, NOPE + ROPE)
    wkr = w_in[:, QL + KVL:]
    wr = w_uq[:, :, NOPE:]
    eye2 = jnp.eye(2, dtype=BF16)
    uk_t = jnp.transpose(w_uk, (1, 2, 0)).reshape(N_PAIR, 2, NOPE, KVL)
    bduk = jnp.einsum("janc,ab->janbc", uk_t, eye2).reshape(N_PAIR, 2 * NOPE, 2 * KVL)
    uv = jnp.transpose(w_uv, (1, 0, 2)).reshape(N_PAIR, 2, KVL, VH)
    bduv = jnp.einsum("jacn,ab->jacbn", uv, eye2).reshape(N_PAIR, 2 * KVL, 2 * VH)
    mw = dict(wq=w_in[:, :QL], wkv=w_in[:, QL:QL + KVL], wkr4=jnp.tile(wkr, (1, SLOTS)),
              wkrs4=jnp.tile(_swap_rope(wkr), (1, SLOTS)), qn=q_norm, kvn=kv_norm,
              wn=w_uq[:, :, :NOPE].reshape(QL, NOPE_ALL), wr=wr.reshape(QL, ROPE_ALL),
              wrs=_swap_rope(wr).reshape(QL, ROPE_ALL), bduk=bduk)
    return ffn_in, ffn_out, pw, mw, bduv, full[11].reshape(D, D)


def _example_step(x, target, mod, norm_g, pvec, ffn_in, ffn_out, pw, mw, bduv, wo, reducer):
    S = x.shape[0]
    tabs = _rope_tables(S)
    vec = [[_vec(norm_g, mod, i, k) for k in range(3)] for i in range(2)]
    saved = {}
    for i in range(2):
        xin = x
        x, a, u, h = ffn_fwd(xin, vec[i][0], ffn_in[i][0], ffn_out[i][0], 0.5)
        saved[i, 0] = (xin, a, u, h)
        xin = x
        if i == 0:
            x, y, z = pool_fwd(xin, vec[i][1], pw, pvec)
            saved[i, 1] = (xin, y, z)
        else:
            h_m, cq_raw, ckv_raw, cqn, qnope, qcat, kcat, vcat = mla_pre(xin, vec[i][1], mw, tabs)
            olat, lse = attn_fwd(qcat, kcat, vcat)
            x, u_m, ocat = mla_post(olat, xin, vec[i][1], bduv, wo)
            saved[i, 1] = (xin, h_m, cq_raw, ckv_raw, cqn, qnope, qcat, kcat, olat, lse, u_m, ocat)
        xin = x
        x, a, u, h = ffn_fwd(xin, vec[i][2], ffn_in[i][1], ffn_out[i][1], 0.5)
        saved[i, 2] = (xin, a, u, h)
    loss, dx = loss_head(x, target)

    vg = {}
    G = D // 4

    def ffn_grads(i, k, dw_in, dw_out):
        return [(0, 2 * i + k, 4, dw_in.reshape(N_CHIP, 2, D // 2, FSH)),
                (1, 2 * i + k, 4, dw_out.reshape(N_CHIP, 2, DFF // 8, D))]

    for i in (1, 0):
        xin, a, u, h = saved[i, 2]
        dx, du, act, da, vg[i, 2] = ffn_bwd(dx, xin, u, a, vec[i][2], ffn_in[i][1], ffn_out[i][1], 0.5)
        reducer.advance()
        reducer.add(f"f{i}1", ffn_grads(i, 1, *ffn_dw(h, da, act, du)))
        if i == 0:
            xin, y, z = saved[i, 1]
            dx, vg[i, 1], pgrad, g_pool = pool_bwd(dx, xin, y, z, vec[i][1], pw, pvec)
            reducer.advance()
        else:
            xin, h_m, cq_raw, ckv_raw, cqn, qnope, qcat, kcat, olat, lse, u_m, ocat = saved[i, 1]
            du, docat, dolat, delta, vg_post = mla_post_bwd(dx, u_m, olat, vec[i][1], bduv, wo)
            reducer.advance()
            dq, dk, dv = attn_bwd(qcat, kcat, kcat.T, dolat, lse, delta)
            reducer.advance()
            dx, dlat, dql, dqcat, vg_pre, ngrad = mla_pre_bwd(
                dx, dq, dk, dv, xin, cq_raw, ckv_raw, vec[i][1], mw, tabs)
            vg[i, 1] = vg_post + vg_pre
            g = mla_dw(h_m, dlat, cqn, dqcat, dql, qnope, olat, docat, ocat, du)
            slots = lambda a: a.reshape(D, SLOTS, ROPE).sum(axis=1)
            g_kr = slots(g["in"][:, QL + KVL:QL + KVL + 128]) + _swap_rope(slots(g["in"][:, QL + KVL + 128:]))
            g_in = jnp.concatenate([g["in"][:, :QL + KVL], g_kr], axis=1)
            g_r = g["q"][:, NOPE_ALL:NOPE_ALL + ROPE_ALL].reshape(QL, N_HEADS, ROPE)
            g_rs = g["q"][:, NOPE_ALL + ROPE_ALL:].reshape(QL, N_HEADS, ROPE)
            g_uq = jnp.concatenate([g["q"][:, :NOPE_ALL].reshape(QL, N_HEADS, NOPE), g_r + _swap_rope(g_rs)], axis=-1)

            def heads(pairs):
                blk = pairs.reshape(N_PAIR, 2, KVL, 2, NOPE)
                per_head = jnp.stack([blk[:, 0, :, 0, :], blk[:, 1, :, 1, :]], axis=1).reshape(N_HEADS, KVL, NOPE)
                return jnp.transpose(per_head, (1, 0, 2)).reshape(KVL, N_HEADS * NOPE)

            reducer.add("mla", [(3, 0, 1, g_in.reshape(N_CHIP, 2, D // 8, QL + KVL + ROPE)),
                                (4, 0, 1, g_uq.reshape(N_CHIP, 2, QL // 8, N_HEADS * (NOPE + ROPE))),
                                (5, 0, 1, g["o"].reshape(N_CHIP, 2, D // 8, D))])
            reducer.add_replicated(jnp.concatenate([heads(g["uk"]), heads(g["uv"])], axis=0))
        xin, a, u, h = saved[i, 0]
        dx, du, act, da, vg[i, 0] = ffn_bwd(dx, xin, u, a, vec[i][0], ffn_in[i][0], ffn_out[i][0], 0.5)
        if i == 1:
            reducer.advance()
        grads = ffn_grads(i, 0, *ffn_dw(h, da, act, du))
        if i == 0:
            grads.append((2, 0, 1, g_pool.reshape(N_CHIP, 2, 2 * G // N_CHIP, G)))
        reducer.add(f"f{i}0", grads)
    return loss, dx, vg, pgrad, ngrad


class _GradReducer:
    def __init__(self, core, place, dev):
        self.core, self.place, self.dev = core, place, dev
        self.stacks = {}
        self.live = []
        self.replicated = None

    def add(self, tag, items):
        gen = self._run(tag, items)
        next(gen)
        self.live.append(gen)

    def add_replicated(self, block):
        self.replicated = gather_blocks("gather_ukv", place_block("place_ukv", self.dev, block))

    def advance(self, after=None):
        self.after = after
        live = []
        for gen in self.live:
            try:
                next(gen)
                live.append(gen)
            except StopIteration:
                pass
        self.live = live

    def finish(self):
        while self.live:
            self.advance()
        return self.stacks, self.replicated

    def _run(self, tag, items):
        grads, from_pair = reduce_pair(f"reduce_pair_{tag}", [g for *_, g in items])
        yield
        sums = [pair_add(f"pair_add_{tag}_{j}", self.core, g, p) for j, (g, p) in enumerate(zip(grads, from_pair))]
        sums, from_chips = reduce_chips(f"reduce_chips_{tag}", sums)
        yield
        for j, ((o, k, n_slots, _), s, p) in enumerate(zip(items, sums, from_chips)):
            self.stacks[o] = chip_add(f"chip_add_{tag}_{j}", self.place, s, p, k, n_slots, self.stacks.get(o),
                                      self.after)
        shared = share_halves(f"share_halves_{tag}", [self.stacks[o] for o, *_ in items], [k for _, k, *_ in items])
        for (o, *_), v in zip(items, shared):
            self.stacks[o] = v


SMALL_IN = 8 * 640
SMALL_GRAD = 8 * 4224
SMALL_W = 8 * 2944


def _pack(parts, total):
    flat = jnp.concatenate([p.reshape(-1) for p in parts])
    return jnp.concatenate([flat, jnp.zeros((total - flat.shape[0],), F32)]).reshape(8, total // 8)


def kernel(x, c, ada_w, ada_b, norm_g, ffn_w_in, ffn_w_out, pool_w, pool_b, pool_scale, mla_w_in, mla_q_norm, mla_kv_norm, mla_w_uq, mla_w_uk, mla_w_uv, mla_w_o, loss_target, m_ada_w, m_ada_b, m_norm_g, m_ffn_w_in, m_ffn_w_out, m_pool_w, m_pool_b, m_pool_scale, m_mla_w_in, m_mla_q_norm, m_mla_kv_norm, m_mla_w_uq, m_mla_w_uk, m_mla_w_uv, m_mla_w_o, v_ada_w, v_ada_b, v_norm_g, v_ffn_w_in, v_ffn_w_out, v_pool_w, v_pool_b, v_pool_scale, v_mla_w_in, v_mla_q_norm, v_mla_kv_norm, v_mla_w_uq, v_mla_w_uk, v_mla_w_uv, v_mla_w_o):
    ix, iy, ic = _place()
    chip = 2 * ix + iy
    dev = 2 * chip + ic
    core_arr = ic.astype(jnp.int32).reshape(1)
    chip_arr = chip.astype(jnp.int32).reshape(1)
    S = x.shape[1]
    G = D // 4
    NG = D // N_CHIP

    def chip_cols(a, width, axis):
        return lax.dynamic_slice_in_dim(a, chip * width, width, axis)

    got = gather_devices("gather_small_in", _pack([c, norm_g, pool_b, mla_q_norm], SMALL_IN)).reshape(N_DEV, SMALL_IN)
    c_all = got[:, :D]
    parts = got[0::2]
    o = D
    norm_g_full = parts[:, o:o + 12 * NG].reshape(N_CHIP, 2, 6, NG).transpose(1, 2, 0, 3).reshape(2, 6, D)
    o += 12 * NG
    pool_b_full = parts[:, o:o + G].reshape(N_CHIP, 4, G // N_CHIP).transpose(1, 0, 2).reshape(1, D)
    o += G
    q_norm_full = parts[:, o:o + QL // N_CHIP].reshape(1, QL)
    pvec = jnp.concatenate([pool_b_full, pool_scale, jnp.zeros((6, D), F32)], axis=0)

    c_pad = jnp.concatenate([c_all, jnp.zeros((8, D), F32)], axis=0)
    mod_loc = mod_fwd(c_pad, ada_w, chip_cols(ada_b, MOD_COLS, 1).reshape(2, 1, MOD_COLS))
    got = gather_devices("gather_mod", mod_loc[:, :8].transpose(1, 0, 2).reshape(8, 2 * MOD_COLS))
    mine = lax.dynamic_index_in_dim(got[0::2].reshape(N_CHIP, 8, 2, MOD_COLS), dev, axis=1, keepdims=False)
    mod = mine.transpose(1, 0, 2).reshape(2, 9, D)

    bf = lambda a: a.astype(BF16)
    w_in_halves = ffn_w_in.reshape(2, 2, 2, D // 2, FSH)
    w_out_halves = ffn_w_out.reshape(2, 2, 2, DFF // 8, D)
    shards = [(w_in_halves, (i, k)) for i in range(2) for k in range(2)]
    shards += [(w_out_halves, (i, k)) for i in range(2) for k in range(2)]
    shards += [(pool_w.reshape(2, 2 * G // N_CHIP, G), ()), (mla_w_in.reshape(2, D // 8, QL + KVL + ROPE), ()),
               (mla_w_uq.reshape(2, QL // 8, N_HEADS * (NOPE + ROPE)), ()), (mla_w_o.reshape(2, D // 8, D), ())]
    full = [None] * len(shards)
    stages = [(0, 4, 8), (1, 5), (2, 6), (9, 10, 11), (3, 7)]
    first, token = cast_into_slots("cast_first", chip_arr, [shards[t] for t in stages[0]])
    slotted = dict(zip(stages[0], first))
    rest = [t for members in stages[1:] for t in members]
    for stage, members in enumerate(stages):
        got_w = gather_weights(f"gather_weights_{stage}", stage, [slotted[t] for t in members])
        for t, a in zip(members, got_w):
            full[t] = a
        if stage == 0:
            slotted.update(zip(rest, cast_into_slots("cast_rest", chip_arr, [shards[t] for t in rest], token)[0]))
    ffn_in, ffn_out, pw, mw, bduv, wo = _unpack_weights(full, bf(mla_w_uk[0]), bf(mla_w_uv[0]), q_norm_full,
                                                        mla_kv_norm)

    place_arr = jnp.stack([chip, ic]).astype(jnp.int32)
    reducer = _GradReducer(core_arr, place_arr, dev.astype(jnp.int32).reshape(1))
    loss_mine, grad_x, vg, pgrad, ngrad = _example_step(
        x[0], loss_target[0], mod, norm_g_full, pvec, ffn_in, ffn_out, pw, mw, bduv, wo, reducer)

    dmod = jnp.stack([jnp.concatenate([vg[i, k][0:3] for k in range(3)]) for i in range(2)])
    dnorm = jnp.stack([jnp.concatenate([vg[i, k][3:5] for k in range(3)]) for i in range(2)])
    small = _pack([dmod, dnorm, pgrad[0], pgrad[1], ngrad[0], ngrad[1, :KVL], loss_mine], SMALL_GRAD)
    got = gather_devices("gather_small_grad", small)
    tot = sum_devices("sum_small_grad", got).reshape(-1)
    n_mod = 2 * 9 * D
    g_ada_b = tot[:n_mod].reshape(ada_b.shape)
    o = n_mod
    g_norm = chip_cols(tot[o:o + 12 * D].reshape(2, 6, D), NG, 2)
    o += 12 * D
    g_pool_b = chip_cols(tot[o:o + D].reshape(1, 4, G), G // N_CHIP, 2)
    o += D
    g_pool_scale = tot[o:o + D].reshape(pool_scale.shape)
    o += D
    g_q_norm = chip_cols(tot[o:o + QL].reshape(1, QL), QL // N_CHIP, 1)
    o += QL
    g_kv_norm = tot[o:o + KVL].reshape(mla_kv_norm.shape)
    loss = tot[o + KVL]
    dmod_all = chip_cols(got.reshape(N_DEV, -1)[:, :n_mod].reshape(N_DEV, 2, 9 * D), MOD_COLS, 2)
    dmod_pad = jnp.concatenate([dmod_all.transpose(1, 0, 2), jnp.zeros((2, 8, MOD_COLS), F32)], axis=1)

    g_ada_w, d_ada_w, nm_ada_w, nv_ada_w = adamw_ada(c_pad, dmod_pad, ada_w, m_ada_w, v_ada_w)
    small_names = ["ada_b", "norm_g", "pool_b", "pool_scale", "mla_q_norm", "mla_kv_norm"]
    small_w = [ada_b, norm_g, pool_b, pool_scale, mla_q_norm, mla_kv_norm]
    small_g = [g_ada_b, g_norm, g_pool_b, g_pool_scale, g_q_norm, g_kv_norm]
    small_m = [m_ada_b, m_norm_g, m_pool_b, m_pool_scale, m_mla_q_norm, m_mla_kv_norm]
    small_v = [v_ada_b, v_norm_g, v_pool_b, v_pool_scale, v_mla_q_norm, v_mla_kv_norm]
    packed = adamw("adamw_small", *[_pack(p, SMALL_W) for p in (small_w, small_g, small_m, small_v)])
    upd = {}
    o = 0
    for name, w in zip(small_names, small_w):
        upd[name] = [p.reshape(-1)[o:o + w.size].reshape(w.shape) for p in packed]
        o += w.size
    upd["ada_w"] = [d_ada_w, nm_ada_w, nv_ada_w]

    reducer.advance(after=d_ada_w[0, :8, :128])
    ffn = [("ffn_w_in", 0, ffn_w_in, m_ffn_w_in, v_ffn_w_in), ("ffn_w_out", 1, ffn_w_out, m_ffn_w_out, v_ffn_w_out)]
    slots = lambda a: a.reshape((4,) + a.shape[2:])
    early = {name: adamw(f"adamw_{name}_early", slots(w), slots(reducer.stacks[o].reshape(w.shape)), slots(m),
                         slots(v), part=(1, 3), copy_grad=True) for name, o, w, m, v in ffn}
    g_mla_in = reducer.stacks[3].reshape(mla_w_in.shape)
    g_uq = reducer.stacks[4].reshape(mla_w_uq.shape)
    g_wo = reducer.stacks[5].reshape(mla_w_o.shape)
    for name, w, g, m, v in [("mla_w_in", mla_w_in, g_mla_in, m_mla_w_in, v_mla_w_in),
                             ("mla_w_uq", mla_w_uq, g_uq, m_mla_w_uq, v_mla_w_uq),
                             ("mla_w_o", mla_w_o, g_wo, m_mla_w_o, v_mla_w_o)]:
        upd[name] = adamw("adamw_" + name, w, g, m, v)

    reducer.advance(after=early["ffn_w_in"][0][1, :8, :128])
    ukv = sum_devices("sum_ukv", reducer.replicated)
    g_uk = ukv[:KVL].reshape(mla_w_uk.shape)
    g_uv = ukv[KVL:].reshape(mla_w_uv.shape)
    upd["mla_w_uk"] = adamw("adamw_mla_w_uk", mla_w_uk, g_uk, m_mla_w_uk, v_mla_w_uk)
    upd["mla_w_uv"] = adamw("adamw_mla_w_uv", mla_w_uv, g_uv, m_mla_w_uv, v_mla_w_uv)
    stacks, _ = reducer.finish()
    g_pool_w = stacks[2].reshape(pool_w.shape)
    g_ffn = {}
    for name, o, w, m, v in ffn:
        done = adamw(f"adamw_{name}_last", slots(w), slots(stacks[o].reshape(w.shape)), slots(m), slots(v),
                     part=(0, 1), prev=early[name], copy_grad=True)
        upd[name] = [p.reshape(w.shape) for p in done[:3]]
        g_ffn[name] = done[3].reshape(w.shape)
    g_ffn_in, g_ffn_out = g_ffn["ffn_w_in"], g_ffn["ffn_w_out"]
    upd["pool_w"] = adamw("adamw_pool_w", pool_w, g_pool_w, m_pool_w, v_pool_w)

    order = ["ada_w", "ada_b", "norm_g", "ffn_w_in", "ffn_w_out", "pool_w", "pool_b", "pool_scale", "mla_w_in",
             "mla_q_norm", "mla_kv_norm", "mla_w_uq", "mla_w_uk", "mla_w_uv", "mla_w_o"]
    grad = dict(ada_w=g_ada_w, ada_b=g_ada_b, norm_g=g_norm, ffn_w_in=g_ffn_in, ffn_w_out=g_ffn_out, pool_w=g_pool_w,
                pool_b=g_pool_b, pool_scale=g_pool_scale, mla_w_in=g_mla_in, mla_q_norm=g_q_norm,
                mla_kv_norm=g_kv_norm, mla_w_uq=g_uq, mla_w_uk=g_uk, mla_w_uv=g_uv, mla_w_o=g_wo)
    return (loss, grad_x[None], *[grad[n] for n in order], *[upd[n][0] for n in order],
            *[upd[n][1] for n in order], *[upd[n][2] for n in order])
```

```python
import functools

import jax
import jax.numpy as jnp
from jax import lax
from jax.experimental import pallas as pl
from jax.experimental.pallas import tpu as pltpu
from jax.experimental.pallas import tpu_sc as plsc

F32 = jnp.float32
BF16 = jnp.bfloat16

D = 1024
DFF = 2816
FSH = 1408
N_CHIP = 4
N_DEV = 8
N_HEADS = 16
NOPE = 64
ROPE = 32
VH = 64
QL = 256
KVL = 128
QPAD = 256
EPS = 1e-6
ATTN_SCALE = (NOPE + ROPE) ** -0.5
ROPE_THETA = 10000.0
POOL_WINDOWS = (2, 4, 8, 16)
HALO = 8
ATTN_TQ = 1024
ATTN_KC = 512
ROW_TILE = 512
DW_TK = 2048

ADAM_LR, ADAM_B1, ADAM_B2, ADAM_EPS, ADAM_WD, ADAM_STEP = 0.001, 0.9, 0.999, 1e-08, 0.01, 10

VMEM_LIMIT = 60 * 1024 * 1024
MESH = pl.DeviceIdType.MESH

NT = (((1,), (1,)), ((), ()))
TN = (((0,), (0,)), ((), ()))


def _params(*sem):
    return pltpu.CompilerParams(dimension_semantics=sem, vmem_limit_bytes=VMEM_LIMIT)


def _dot(a, b, dims=None):
    if dims is None:
        return jnp.dot(a, b, preferred_element_type=F32)
    return lax.dot_general(a, b, dims, preferred_element_type=F32)


def _rms(x):
    r = lax.rsqrt(jnp.mean(x * x, axis=-1, keepdims=True) + EPS)
    return x * r, r


def _rms_bwd(xhat, r, dxhat):
    return r * (dxhat - xhat * jnp.mean(dxhat * xhat, axis=-1, keepdims=True))


def _as_row(col):
    return jnp.broadcast_to(col, (col.shape[0], 128)).T[0:1, :]


def _prenorm(x, vec_ref):
    xhat, r = _rms(x)
    h = xhat * vec_ref[0:1, :] * (1.0 + vec_ref[3:4, :]) + vec_ref[2:3, :]
    return h, xhat, r


def _postnorm_bwd(dout, u, vec_ref, weight):
    uhat, r = _rms(u)
    gt = weight * (1.0 + vec_ref[4:5, :])
    dy = dout * gt
    dgate_rows = (weight * dout) * (uhat * vec_ref[1:2, :])
    dgpost_rows = dy * uhat
    du = _rms_bwd(uhat, r, dy * vec_ref[1:2, :])
    return du, dgate_rows, dgpost_rows


def _prenorm_bwd(dh, x, vec_ref, vg_ref):
    xhat, r = _rms(x)
    sc1 = 1.0 + vec_ref[3:4, :]
    g = vec_ref[0:1, :]
    vg_ref[0:1, :] += jnp.sum(dh, axis=0, keepdims=True)
    vg_ref[1:2, :] += jnp.sum(dh * (xhat * g), axis=0, keepdims=True)
    vg_ref[3:4, :] += jnp.sum(dh * sc1 * xhat, axis=0, keepdims=True)
    return _rms_bwd(xhat, r, dh * g * sc1)


def ffn_fwd(x, vec, w_in, w_out, weight):
    S = x.shape[0]
    tm = min(512, S)
    row = lambda i: (i, 0)
    half = lambda j: [_w3((8, D)), pl.BlockSpec((None, D, FSH), lambda i: (j, 0, 0)),
                      pl.BlockSpec((None, D, FSH), lambda i: (j + 2, 0, 0)),
                      pl.BlockSpec((None, FSH, D), lambda i: (j, 0, 0))]
    a_spec = lambda j: pl.BlockSpec((2, tm, FSH), lambda i: (0, i, j))
    a_shape = jax.ShapeDtypeStruct((2, S, DFF), BF16)

    def hidden(hb, wg_ref, wu_ref, wo_ref, a_ref):
        g = _dot(hb, wg_ref[...])
        up = _dot(hb, wu_ref[...])
        a_ref[0] = g.astype(BF16)
        a_ref[1] = up.astype(BF16)
        act = (g * jax.nn.sigmoid(g)) * up
        return _dot(act.astype(BF16), wo_ref[...])

    def first(x_ref, vec_ref, wg_ref, wu_ref, wo_ref, h_ref, a_ref, u_ref):
        h, _, _ = _prenorm(x_ref[...], vec_ref)
        hb = h.astype(BF16)
        h_ref[...] = hb
        u_ref[...] = hidden(hb, wg_ref, wu_ref, wo_ref, a_ref)

    h, a, u_half = pl.pallas_call(
        first, name="ffn_fwd_first", grid=(S // tm,),
        in_specs=[pl.BlockSpec((tm, D), row)] + half(0),
        out_specs=[pl.BlockSpec((tm, D), row), a_spec(0), pl.BlockSpec((tm, D), row)],
        out_shape=[jax.ShapeDtypeStruct((S, D), BF16), a_shape, jax.ShapeDtypeStruct((S, D), F32)],
        compiler_params=_params("parallel"),
    )(x, vec, w_in, w_in, w_out)

    def second(x_ref, h_ref, uh_ref, vec_ref, wg_ref, wu_ref, wo_ref, a_in, xo_ref, a_ref, u_ref):
        u = uh_ref[...] + hidden(h_ref[...], wg_ref, wu_ref, wo_ref, a_ref)
        u_ref[...] = u
        uhat, _ = _rms(u)
        xo_ref[...] = x_ref[...] + (weight * (1.0 + vec_ref[4:5, :])) * (uhat * vec_ref[1:2, :])

    xo, a, u = pl.pallas_call(
        second, name="ffn_fwd_second", grid=(S // tm,),
        in_specs=[pl.BlockSpec((tm, D), row), pl.BlockSpec((tm, D), row), pl.BlockSpec((tm, D), row)] + half(1) + [_ANY],
        out_specs=[pl.BlockSpec((tm, D), row), a_spec(1), pl.BlockSpec((tm, D), row)],
        out_shape=[jax.ShapeDtypeStruct((S, D), F32), a_shape, jax.ShapeDtypeStruct((S, D), F32)],
        input_output_aliases={7: 1},
        compiler_params=_params("parallel"),
    )(x, h, u_half, vec, w_in, w_in, w_out, a)
    return xo, a, u, h


def ffn_bwd(dout, x, u, a, vec, w_in, w_out, weight):
    S = x.shape[0]
    tm = min(512, S)
    row = lambda i: (i, 0)
    half = lambda j: [pl.BlockSpec((2, tm, FSH), lambda i: (0, i, j)), _w3((8, D)),
                      pl.BlockSpec((None, D, FSH), lambda i: (j, 0, 0)),
                      pl.BlockSpec((None, D, FSH), lambda i: (j + 2, 0, 0)),
                      pl.BlockSpec((None, FSH, D), lambda i: (j, 0, 0))]
    half_out = lambda j: [pl.BlockSpec((tm, FSH), lambda i: (i, j)), pl.BlockSpec((2, tm, FSH), lambda i: (0, i, j))]
    half_shape = [jax.ShapeDtypeStruct((S, DFF), BF16), jax.ShapeDtypeStruct((2, S, DFF), BF16)]

    def hidden_bwd(du, a_ref, wg_ref, wu_ref, wo_ref, act_ref, da_ref):
        dact = _dot(du, wo_ref[...], NT)
        g = a_ref[0].astype(F32)
        up = a_ref[1].astype(F32)
        s = jax.nn.sigmoid(g)
        silu = g * s
        act_ref[...] = (silu * up).astype(BF16)
        dg = (dact * up * (s * (1.0 + g * (1.0 - s)))).astype(BF16)
        dup = (dact * silu).astype(BF16)
        da_ref[0] = dg
        da_ref[1] = dup
        return _dot(dg, wg_ref[...], NT) + _dot(dup, wu_ref[...], NT)

    def first(do_ref, u_ref, a_ref, vec_ref, wg_ref, wu_ref, wo_ref, du_ref, dh_ref, act_ref, da_ref, vg_ref):
        @pl.when(pl.program_id(0) == 0)
        def _():
            vg_ref[...] = jnp.zeros_like(vg_ref)

        du, dgate_rows, dgpost_rows = _postnorm_bwd(do_ref[...], u_ref[...], vec_ref, weight)
        vg_ref[2:3, :] += jnp.sum(dgate_rows, axis=0, keepdims=True)
        vg_ref[4:5, :] += jnp.sum(dgpost_rows, axis=0, keepdims=True)
        du = du.astype(BF16)
        du_ref[...] = du
        dh_ref[...] = hidden_bwd(du, a_ref, wg_ref, wu_ref, wo_ref, act_ref, da_ref)

    du, dh, act, da, vg_post = pl.pallas_call(
        first, name="ffn_bwd_first", grid=(S // tm,),
        in_specs=[pl.BlockSpec((tm, D), row), pl.BlockSpec((tm, D), row)] + half(0),
        out_specs=[pl.BlockSpec((tm, D), row), pl.BlockSpec((tm, D), row)] + half_out(0) + [_w3((8, D))],
        out_shape=[jax.ShapeDtypeStruct((S, D), BF16), jax.ShapeDtypeStruct((S, D), F32)] + half_shape
        + [jax.ShapeDtypeStruct((8, D), F32)],
        compiler_params=_params("arbitrary"),
    )(dout, u, a, vec, w_in, w_in, w_out)

    def second(do_ref, x_ref, du_ref, dh_ref, a_ref, vec_ref, wg_ref, wu_ref, wo_ref, act_in, da_in,
               dx_ref, act_ref, da_ref, vg_ref):
        @pl.when(pl.program_id(0) == 0)
        def _():
            vg_ref[...] = jnp.zeros_like(vg_ref)

        dh = dh_ref[...] + hidden_bwd(du_ref[...], a_ref, wg_ref, wu_ref, wo_ref, act_ref, da_ref)
        dx_ref[...] = do_ref[...] + _prenorm_bwd(dh, x_ref[...], vec_ref, vg_ref)

    dx, act, da, vg_pre = pl.pallas_call(
        second, name="ffn_bwd_second", grid=(S // tm,),
        in_specs=[pl.BlockSpec((tm, D), row), pl.BlockSpec((tm, D), row), pl.BlockSpec((tm, D), row),
                  pl.BlockSpec((tm, D), row)] + half(1) + [_ANY, _ANY],
        out_specs=[pl.BlockSpec((tm, D), row)] + half_out(1) + [_w3((8, D))],
        out_shape=[jax.ShapeDtypeStruct((S, D), F32)] + half_shape + [jax.ShapeDtypeStruct((8, D), F32)],
        input_output_aliases={9: 1, 10: 2},
        compiler_params=_params("arbitrary"),
    )(dout, x, du, dh, a, vec, w_in, w_in, w_out, act, da)
    return dx, du, act, da, vg_post + vg_pre


def dw_matmul(name, a, b, a_spec, b_spec, out_shape, out_spec, grid):
    def body(a_ref, b_ref, o_ref):
        @pl.when(pl.program_id(len(grid) - 1) == 0)
        def _():
            o_ref[...] = jnp.zeros_like(o_ref)

        o_ref[...] += _dot(a_ref[...], b_ref[...], TN)

    return pl.pallas_call(
        body, name=name, grid=grid, in_specs=[a_spec, b_spec], out_specs=out_spec,
        out_shape=jax.ShapeDtypeStruct(out_shape, F32),
        compiler_params=_params(*(["parallel"] * (len(grid) - 1) + ["arbitrary"])),
    )(a, b)


def ffn_dw(h, da, act, du):
    S = h.shape[0]
    tk = min(DW_TK, S)
    dw_in = dw_matmul("ffn_dw_in", h, da,
                      pl.BlockSpec((tk, D), lambda n, k: (k, 0)),
                      pl.BlockSpec((None, tk, FSH), lambda n, k: (n // 2, k, n % 2)),
                      (N_CHIP, D, FSH), pl.BlockSpec((None, D, FSH), lambda n, k: (n, 0, 0)),
                      (N_CHIP, S // tk))
    dw_out = dw_matmul("ffn_dw_out", act, du,
                       pl.BlockSpec((tk, FSH), lambda n, k: (k, n)),
                       pl.BlockSpec((tk, D), lambda n, k: (k, 0)),
                       (DFF, D), pl.BlockSpec((FSH, D), lambda n, k: (n, 0)),
                       (2, S // tk))
    return dw_in, dw_out


def _halo_specs(tm, S):
    nb = tm // HALO
    last = S // HALO - 1
    return [pl.BlockSpec((HALO, D), lambda i: (jnp.maximum(i * nb - 1, 0), 0)),
            pl.BlockSpec((tm, D), lambda i: (i, 0)),
            pl.BlockSpec((HALO, D), lambda i: (jnp.minimum((i + 1) * nb, last), 0))]


def _shift_rows(v, k):
    return pltpu.roll(v, k % v.shape[0], 0)


def _window_sum(v, g, forward):
    acc = v + _shift_rows(v, 1 if forward else -1)
    for step in (1, 2, 4)[:g]:
        acc = _shift_rows(acc, step) + _shift_rows(acc, -step)
    return acc


def _pool_count(t, w, S):
    return jnp.maximum(jnp.minimum(t + w // 2, S) - jnp.maximum(t - w // 2, 0), 1).astype(F32)


def pool_fwd(x, vec, pw, pvec):
    S = x.shape[0]
    tm = min(ROW_TILE, S)
    G = D // 4

    def body(xp_ref, x_ref, xn_ref, vec_ref, pw_ref, pv_ref, xo_ref, y_ref, z_ref):
        i = pl.program_id(0)
        xa = jnp.concatenate([xp_ref[...], x_ref[...], xn_ref[...]], axis=0)
        t = i * tm - HALO + lax.broadcasted_iota(jnp.int32, (tm + 2 * HALO, 1), 0)
        h, _, _ = _prenorm(xa, vec_ref)
        h = jnp.where((t >= 0) & (t < S), h, 0.0)
        tmain = t[HALO:HALO + tm]
        for g in range(4):
            hg = h[:, g * G:(g + 1) * G]
            pooled = _window_sum(hg, g, True)[HALO:HALO + tm] / _pool_count(tmain, POOL_WINDOWS[g], S)
            z = (pooled - hg[HALO:HALO + tm]).astype(BF16)
            z_ref[:, g * G:(g + 1) * G] = z
            y_ref[:, g * G:(g + 1) * G] = _dot(z, pw_ref[g]) + pv_ref[0:1, g * G:(g + 1) * G]
        u = y_ref[...] * pv_ref[1:2, :]
        uhat, _ = _rms(u)
        xo_ref[...] = x_ref[...] + (1.0 + vec_ref[4:5, :]) * (uhat * vec_ref[1:2, :])

    row = lambda i: (i, 0)
    full = lambda i: (0, 0)
    return pl.pallas_call(
        body, name="pool_fwd", grid=(S // tm,),
        in_specs=_halo_specs(tm, S) + [pl.BlockSpec((8, D), full), pl.BlockSpec((4, G, G), lambda i: (0, 0, 0)),
                                       pl.BlockSpec((8, D), full)],
        out_specs=[pl.BlockSpec((tm, D), row)] * 3,
        out_shape=[jax.ShapeDtypeStruct((S, D), F32), jax.ShapeDtypeStruct((S, D), F32),
                   jax.ShapeDtypeStruct((S, D), BF16)],
        compiler_params=_params("parallel"),
    )(x, x, x, vec, pw, pvec)


def pool_bwd(dout, x, y, z, vec, pw, pvec):
    S = x.shape[0]
    tm = min(ROW_TILE, S)
    G = D // 4
    R = G // N_CHIP

    def body(dop_ref, do_ref, don_ref, yp_ref, y_ref, yn_ref, x_ref, z_ref, vec_ref, pw_ref, pv_ref,
             dx_ref, vg_ref, pg_ref, dw_ref, dh_ref):
        i = pl.program_id(0)

        @pl.when(i == 0)
        def _():
            vg_ref[...] = jnp.zeros_like(vg_ref)
            pg_ref[...] = jnp.zeros_like(pg_ref)
            dw_ref[...] = jnp.zeros_like(dw_ref)

        doa = jnp.concatenate([dop_ref[...], do_ref[...], don_ref[...]], axis=0)
        ya = jnp.concatenate([yp_ref[...], y_ref[...], yn_ref[...]], axis=0)
        t = i * tm - HALO + lax.broadcasted_iota(jnp.int32, (tm + 2 * HALO, 1), 0)
        inside = (t >= 0) & (t < S)
        main = (t >= i * tm) & (t < (i + 1) * tm)
        du, dgate_rows, dgpost_rows = _postnorm_bwd(doa, ya * pv_ref[1:2, :], vec_ref, 1.0)
        du = jnp.where(inside, du, 0.0)
        vg_ref[2:3, :] += jnp.sum(jnp.where(main, dgate_rows, 0.0), axis=0, keepdims=True)
        vg_ref[4:5, :] += jnp.sum(jnp.where(main, dgpost_rows, 0.0), axis=0, keepdims=True)
        dy = du * pv_ref[1:2, :]
        pg_ref[0:1, :] += jnp.sum(jnp.where(main, dy, 0.0), axis=0, keepdims=True)
        pg_ref[1:2, :] += jnp.sum(jnp.where(main, du * ya, 0.0), axis=0, keepdims=True)
        for g in range(4):
            dyg = dy[:, g * G:(g + 1) * G].astype(BF16)
            dz = _dot(dyg, pw_ref[g], NT)
            e = dz / _pool_count(t, POOL_WINDOWS[g], S)
            dh_ref[:, g * G:(g + 1) * G] = (_window_sum(e, g, False) - dz)[HALO:HALO + tm]
            dwg = _dot(z_ref[:, g * G:(g + 1) * G], dyg[HALO:HALO + tm], TN)
            for q in range(N_CHIP):
                dw_ref[q, g] += dwg[q * R:(q + 1) * R, :]
        dx_ref[...] = do_ref[...] + _prenorm_bwd(dh_ref[...], x_ref[...], vec_ref, vg_ref)

    row = lambda i: (i, 0)
    full = lambda i: (0, 0)
    halo = _halo_specs(tm, S)
    return pl.pallas_call(
        body, name="pool_bwd", grid=(S // tm,),
        in_specs=halo + halo + [pl.BlockSpec((tm, D), row), pl.BlockSpec((tm, D), row), pl.BlockSpec((8, D), full),
                                pl.BlockSpec((4, G, G), lambda i: (0, 0, 0)), pl.BlockSpec((8, D), full)],
        out_specs=[pl.BlockSpec((tm, D), row), pl.BlockSpec((8, D), full), pl.BlockSpec((8, D), full),
                   pl.BlockSpec((N_CHIP, 4, R, G), lambda i: (0, 0, 0, 0))],
        out_shape=[jax.ShapeDtypeStruct((S, D), F32), jax.ShapeDtypeStruct((8, D), F32),
                   jax.ShapeDtypeStruct((8, D), F32), jax.ShapeDtypeStruct((N_CHIP, 4, R, G), F32)],
        scratch_shapes=[pltpu.VMEM((tm, D), F32)],
        compiler_params=_params("arbitrary"),
    )(dout, dout, dout, y, y, y, x, z, vec, pw, pvec)


N_PAIR = N_HEADS // 2
SLOTS = 128 // ROPE
ROPE_ALL = N_HEADS * ROPE
NOPE_ALL = N_HEADS * NOPE
LAT_ALL = N_HEADS * KVL
DLAT = QL + KVL + 2 * 128
DQ_ALL = NOPE_ALL + 2 * ROPE_ALL


def _w3(shape):
    return pl.BlockSpec(shape, lambda i: (0,) * len(shape))


def _slot_mask(hd, rows):
    lane = lax.broadcasted_iota(jnp.int32, (rows, 128), 1)
    return (lane // ROPE) == (hd % SLOTS)


MLA_WEIGHTS = ("wq", "wkv", "wkr4", "wkrs4", "qn", "kvn", "wn", "wr", "wrs", "bduk")


def _mla_weight_specs():
    return [_w3((D, QL)), _w3((D, KVL)), _w3((D, 128)), _w3((D, 128)), _w3((1, QL)), _w3((1, KVL)),
            _w3((QL, NOPE_ALL)), _w3((QL, ROPE_ALL)), _w3((QL, ROPE_ALL)), _w3((N_PAIR, 2 * NOPE, 2 * KVL))]


def mla_pre(x, vec, mw, tabs):
    S = x.shape[0]
    tm = min(ROW_TILE, S)

    def body(x_ref, vec_ref, cos_ref, sin_ref, wq_ref, wkv_ref, wkr_ref, wkrs_ref, qn_ref, kvn_ref,
             wn_ref, wr_ref, wrs_ref, bduk_ref,
             h_ref, cq_ref, ckv_ref, cqn_ref, qnope_ref, qcat_ref, kcat_ref, vcat_ref):
        h, _, _ = _prenorm(x_ref[...], vec_ref)
        hb = h.astype(BF16)
        h_ref[...] = hb
        cq_raw = _dot(hb, wq_ref[...])
        ckv_raw = _dot(hb, wkv_ref[...])
        cq_ref[...] = cq_raw
        ckv_ref[...] = ckv_raw
        cos, sin = cos_ref[...], sin_ref[...]
        ckv = (_rms(ckv_raw)[0] * kvn_ref[...]).astype(BF16)
        kcat_ref[:, 0:KVL] = ckv
        kcat_ref[:, KVL:] = (_dot(hb, wkr_ref[...]) * cos + _dot(hb, wkrs_ref[...]) * sin).astype(BF16)
        vcat_ref[:, 0:KVL] = ckv
        ones = lax.broadcasted_iota(jnp.int32, (tm, QPAD - KVL), 1) == 0
        vcat_ref[:, KVL:] = jnp.where(ones, 1.0, 0.0).astype(BF16)
        cqb = (_rms(cq_raw)[0] * qn_ref[...]).astype(BF16)
        cqn_ref[...] = cqb
        qn = _dot(cqb, wn_ref[...]).astype(BF16)
        qnope_ref[...] = qn
        cos4, sin4 = jnp.tile(cos, (1, SLOTS)), jnp.tile(sin, (1, SLOTS))
        qr = ((_dot(cqb, wr_ref[...]) * cos4 + _dot(cqb, wrs_ref[...]) * sin4) * ATTN_SCALE).astype(BF16)
        for j in range(N_PAIR):
            ql = (_dot(qn[:, 128 * j:128 * (j + 1)], bduk_ref[j]) * ATTN_SCALE).astype(BF16)
            for hd in (2 * j, 2 * j + 1):
                qcat_ref[hd, :, 0:KVL] = ql[:, KVL * (hd - 2 * j):KVL * (hd - 2 * j + 1)]
                group = qr[:, 128 * (hd // SLOTS):128 * (hd // SLOTS + 1)]
                qcat_ref[hd, :, KVL:] = jnp.where(_slot_mask(hd, tm), group, jnp.zeros_like(group))

    row = lambda i: (i, 0)
    hrow = lambda i: (0, i, 0)
    return pl.pallas_call(
        body, name="mla_pre", grid=(S // tm,),
        in_specs=[pl.BlockSpec((tm, D), row), _w3((8, D)), pl.BlockSpec((tm, 128), row), pl.BlockSpec((tm, 128), row)]
        + _mla_weight_specs(),
        out_specs=[pl.BlockSpec((tm, D), row), pl.BlockSpec((tm, QL), row), pl.BlockSpec((tm, KVL), row),
                   pl.BlockSpec((tm, QL), row), pl.BlockSpec((tm, NOPE_ALL), row),
                   pl.BlockSpec((N_HEADS, tm, QPAD), hrow), pl.BlockSpec((tm, QPAD), row),
                   pl.BlockSpec((tm, QPAD), row)],
        out_shape=[jax.ShapeDtypeStruct((S, D), BF16), jax.ShapeDtypeStruct((S, QL), F32),
                   jax.ShapeDtypeStruct((S, KVL), F32), jax.ShapeDtypeStruct((S, QL), BF16),
                   jax.ShapeDtypeStruct((S, NOPE_ALL), BF16), jax.ShapeDtypeStruct((N_HEADS, S, QPAD), BF16),
                   jax.ShapeDtypeStruct((S, QPAD), BF16), jax.ShapeDtypeStruct((S, QPAD), BF16)],
        compiler_params=_params("parallel"),
    )(x, vec, tabs[0], tabs[1], *[mw[k] for k in MLA_WEIGHTS])


def attn_fwd(qcat, kcat, vcat):
    S = kcat.shape[0]
    tq = min(ATTN_TQ, S)
    kc = min(ATTN_KC, S)

    def body(q_ref, k_ref, v_ref, o_ref, lse_ref):
        q = q_ref[...]
        m = jnp.full((tq, 1), -jnp.inf, F32)
        ov = jnp.zeros((tq, QPAD), F32)
        for c in range(S // kc):
            s = _dot(q, k_ref[c * kc:(c + 1) * kc, :], NT)
            m_new = jnp.maximum(m, jnp.max(s, axis=-1, keepdims=True))
            p = jnp.exp(s - m_new).astype(BF16)
            ov = ov * jnp.exp(m - m_new) + _dot(p, v_ref[c * kc:(c + 1) * kc, :])
            m = m_new
        l = ov[:, KVL:KVL + 1]
        o_ref[...] = (ov[:, 0:KVL] * (1.0 / l)).astype(BF16)
        lse_ref[...] = _as_row(m + jnp.log(l))

    return pl.pallas_call(
        body, name="attn_fwd", grid=(N_HEADS, S // tq),
        in_specs=[pl.BlockSpec((None, tq, QPAD), lambda h, i: (h, i, 0)),
                  pl.BlockSpec((S, QPAD), lambda h, i: (0, 0)),
                  pl.BlockSpec((S, QPAD), lambda h, i: (0, 0))],
        out_specs=[pl.BlockSpec((tq, KVL), lambda h, i: (i, h)),
                   pl.BlockSpec((None, 1, tq), lambda h, i: (h, 0, i))],
        out_shape=[jax.ShapeDtypeStruct((S, LAT_ALL), BF16), jax.ShapeDtypeStruct((N_HEADS, 1, S), F32)],
        compiler_params=_params("parallel", "parallel"),
    )(qcat, kcat, vcat)


def mla_post(olat, x, vec, bduv, wo):
    S = x.shape[0]
    tm = min(ROW_TILE, S)

    def body(o_ref, x_ref, vec_ref, bduv_ref, wo_ref, xo_ref, u_ref, ocat_ref):
        for j in range(N_PAIR):
            oc = _dot(o_ref[:, 2 * KVL * j:2 * KVL * (j + 1)], bduv_ref[j])
            ocat_ref[:, 2 * VH * j:2 * VH * (j + 1)] = oc.astype(BF16)
        u = _dot(ocat_ref[...], wo_ref[...])
        u_ref[...] = u
        uhat, _ = _rms(u)
        xo_ref[...] = x_ref[...] + (1.0 + vec_ref[4:5, :]) * (uhat * vec_ref[1:2, :])

    row = lambda i: (i, 0)
    return pl.pallas_call(
        body, name="mla_post", grid=(S // tm,),
        in_specs=[pl.BlockSpec((tm, LAT_ALL), row), pl.BlockSpec((tm, D), row), _w3((8, D)),
                  _w3((N_PAIR, 2 * KVL, 2 * VH)), _w3((D, D))],
        out_specs=[pl.BlockSpec((tm, D), row), pl.BlockSpec((tm, D), row), pl.BlockSpec((tm, D), row)],
        out_shape=[jax.ShapeDtypeStruct((S, D), F32), jax.ShapeDtypeStruct((S, D), F32),
                   jax.ShapeDtypeStruct((S, D), BF16)],
        compiler_params=_params("parallel"),
    )(olat, x, vec, bduv, wo)


def mla_post_bwd(dout, u, olat, vec, bduv, wo):
    S = u.shape[0]
    tm = min(ROW_TILE, S)

    def body(do_ref, u_ref, o_ref, vec_ref, bduv_ref, wo_ref, du_ref, docat_ref, dolat_ref, delta_ref, vg_ref):
        @pl.when(pl.program_id(0) == 0)
        def _():
            vg_ref[...] = jnp.zeros_like(vg_ref)

        du, dgate_rows, dgpost_rows = _postnorm_bwd(do_ref[...], u_ref[...], vec_ref, 1.0)
        vg_ref[2:3, :] += jnp.sum(dgate_rows, axis=0, keepdims=True)
        vg_ref[4:5, :] += jnp.sum(dgpost_rows, axis=0, keepdims=True)
        dub = du.astype(BF16)
        du_ref[...] = dub
        docat_ref[...] = _dot(dub, wo_ref[...], NT).astype(BF16)
        for j in range(N_PAIR):
            dol = _dot(docat_ref[:, 2 * VH * j:2 * VH * (j + 1)], bduv_ref[j], NT).astype(BF16)
            dolat_ref[:, 2 * KVL * j:2 * KVL * (j + 1)] = dol
            prod = dol.astype(F32) * o_ref[:, 2 * KVL * j:2 * KVL * (j + 1)].astype(F32)
            delta_ref[2 * j] = _as_row(jnp.sum(prod[:, 0:KVL], axis=-1, keepdims=True))
            delta_ref[2 * j + 1] = _as_row(jnp.sum(prod[:, KVL:], axis=-1, keepdims=True))

    row = lambda i: (i, 0)
    hrow = lambda i: (0, i, 0)
    return pl.pallas_call(
        body, name="mla_post_bwd", grid=(S // tm,),
        in_specs=[pl.BlockSpec((tm, D), row), pl.BlockSpec((tm, D), row), pl.BlockSpec((tm, LAT_ALL), row),
                  _w3((8, D)), _w3((N_PAIR, 2 * KVL, 2 * VH)), _w3((D, D))],
        out_specs=[pl.BlockSpec((tm, D), row), pl.BlockSpec((tm, D), row),
                   pl.BlockSpec((tm, LAT_ALL), row), pl.BlockSpec((N_HEADS, 1, tm), lambda i: (0, 0, i)), _w3((8, D))],
        out_shape=[jax.ShapeDtypeStruct((S, D), BF16), jax.ShapeDtypeStruct((S, D), BF16),
                   jax.ShapeDtypeStruct((S, LAT_ALL), BF16), jax.ShapeDtypeStruct((N_HEADS, 1, S), F32),
                   jax.ShapeDtypeStruct((8, D), F32)],
        compiler_params=_params("arbitrary"),
    )(dout, u, olat, vec, bduv, wo)


def attn_bwd(qcat, kcat, kcat_t, dolat, lse_row, delta_row):
    S = kcat.shape[0]
    tq = min(ATTN_TQ, S)
    kc = min(ATTN_KC, S)

    def body(q_ref, k_ref, kt_ref, do_ref, lse_ref, dl_ref, dq_ref, dk_ref, dv_ref):
        @pl.when((pl.program_id(0) == 0) & (pl.program_id(1) == 0))
        def _():
            dk_ref[...] = jnp.zeros_like(dk_ref)
            dv_ref[...] = jnp.zeros_like(dv_ref)

        q, do = q_ref[...], do_ref[...]
        lse, dl = lse_ref[...], dl_ref[...]
        dqt = jnp.zeros((QPAD, tq), F32)
        for c in range(S // kc):
            rows = slice(c * kc, (c + 1) * kc)
            st = _dot(k_ref[rows, :], q, NT)
            pt = jnp.exp(st - lse)
            dpt = _dot(k_ref[rows, 0:KVL], do, NT)
            dst = (pt * (dpt - dl)).astype(BF16)
            dv_ref[rows, :] += _dot(pt.astype(BF16), do)
            dk_ref[rows, :] += _dot(dst, q)
            dqt = dqt + _dot(kt_ref[:, rows], dst)
        dq_ref[...] = (dqt.T * ATTN_SCALE).astype(BF16)

    return pl.pallas_call(
        body, name="attn_bwd", grid=(N_HEADS, S // tq),
        in_specs=[pl.BlockSpec((None, tq, QPAD), lambda h, i: (h, i, 0)),
                  pl.BlockSpec((S, QPAD), lambda h, i: (0, 0)),
                  pl.BlockSpec((QPAD, S), lambda h, i: (0, 0)),
                  pl.BlockSpec((tq, KVL), lambda h, i: (i, h)),
                  pl.BlockSpec((None, 1, tq), lambda h, i: (h, 0, i)),
                  pl.BlockSpec((None, 1, tq), lambda h, i: (h, 0, i))],
        out_specs=[pl.BlockSpec((None, tq, QPAD), lambda h, i: (h, i, 0)),
                   pl.BlockSpec((S, QPAD), lambda h, i: (0, 0)),
                   pl.BlockSpec((S, KVL), lambda h, i: (0, 0))],
        out_shape=[jax.ShapeDtypeStruct((N_HEADS, S, QPAD), BF16), jax.ShapeDtypeStruct((S, QPAD), F32),
                   jax.ShapeDtypeStruct((S, KVL), F32)],
        compiler_params=_params("arbitrary", "arbitrary"),
    )(qcat, kcat, kcat_t, dolat, lse_row, delta_row)


def mla_pre_bwd(dout, dq, dk, dv, x, cq_raw, ckv_raw, vec, mw, tabs):
    S = x.shape[0]
    tm = min(ROW_TILE, S)

    def body(do_ref, dq_ref, dk_ref, dv_ref, x_ref, cq_ref, ckv_ref, vec_ref, cos_ref, sin_ref,
             wq_ref, wkv_ref, wkr_ref, wkrs_ref, qn_ref, kvn_ref, wn_ref, wr_ref, wrs_ref, bduk_ref,
             dx_ref, dlat_ref, dql_ref, dqcat_ref, vg_ref, ng_ref):
        @pl.when(pl.program_id(0) == 0)
        def _():
            vg_ref[...] = jnp.zeros_like(vg_ref)
            ng_ref[...] = jnp.zeros_like(ng_ref)

        cos, sin = cos_ref[...], sin_ref[...]
        for j in range(N_PAIR):
            dql = jnp.concatenate([dq_ref[2 * j, :, 0:KVL], dq_ref[2 * j + 1, :, 0:KVL]], axis=1)
            dql_ref[:, 2 * KVL * j:2 * KVL * (j + 1)] = dql
            dqcat_ref[:, 2 * NOPE * j:2 * NOPE * (j + 1)] = _dot(dql, bduk_ref[j], NT).astype(BF16)
        groups = []
        for grp in range(N_HEADS // SLOTS):
            acc = jnp.zeros((tm, 128), F32)
            for hd in range(SLOTS * grp, SLOTS * (grp + 1)):
                acc = acc + jnp.where(_slot_mask(hd, tm), dq_ref[hd, :, KVL:].astype(F32), 0.0)
            groups.append(acc)
        dqr = jnp.concatenate(groups, axis=1)
        qa = (dqr * jnp.tile(cos, (1, SLOTS))).astype(BF16)
        qb = (dqr * jnp.tile(sin, (1, SLOTS))).astype(BF16)
        dqcat_ref[:, NOPE_ALL:NOPE_ALL + ROPE_ALL] = qa
        dqcat_ref[:, NOPE_ALL + ROPE_ALL:] = qb
        dcq = _dot(dqcat_ref[:, 0:NOPE_ALL], wn_ref[...], NT) + _dot(qa, wr_ref[...], NT) + _dot(qb, wrs_ref[...], NT)
        cqh, rq = _rms(cq_ref[...])
        ng_ref[0:1, :] += jnp.sum(dcq * cqh, axis=0, keepdims=True)
        dcq_raw = _rms_bwd(cqh, rq, dcq * qn_ref[...]).astype(BF16)
        dckv = dk_ref[:, 0:KVL] + dv_ref[...]
        ckvh, rk = _rms(ckv_ref[...])
        ng_ref[1:2, 0:KVL] += jnp.sum(dckv * ckvh, axis=0, keepdims=True)
        dckv_raw = _rms_bwd(ckvh, rk, dckv * kvn_ref[...]).astype(BF16)
        dkr = dk_ref[:, KVL:]
        ka = (dkr * cos).astype(BF16)
        kb = (dkr * sin).astype(BF16)
        dlat_ref[:, 0:QL] = dcq_raw
        dlat_ref[:, QL:QL + KVL] = dckv_raw
        dlat_ref[:, QL + KVL:QL + KVL + 128] = ka
        dlat_ref[:, QL + KVL + 128:] = kb
        dh = (_dot(dcq_raw, wq_ref[...], NT) + _dot(dckv_raw, wkv_ref[...], NT)
              + _dot(ka, wkr_ref[...], NT) + _dot(kb, wkrs_ref[...], NT))
        dx_ref[...] = do_ref[...] + _prenorm_bwd(dh, x_ref[...], vec_ref, vg_ref)

    row = lambda i: (i, 0)
    hrow = lambda i: (0, i, 0)
    return pl.pallas_call(
        body, name="mla_pre_bwd", grid=(S // tm,),
        in_specs=[pl.BlockSpec((tm, D), row), pl.BlockSpec((N_HEADS, tm, QPAD), hrow), pl.BlockSpec((tm, QPAD), row),
                  pl.BlockSpec((tm, KVL), row), pl.BlockSpec((tm, D), row), pl.BlockSpec((tm, QL), row),
                  pl.BlockSpec((tm, KVL), row), _w3((8, D)), pl.BlockSpec((tm, 128), row), pl.BlockSpec((tm, 128), row)]
        + _mla_weight_specs(),
        out_specs=[pl.BlockSpec((tm, D), row), pl.BlockSpec((tm, DLAT), row), pl.BlockSpec((tm, LAT_ALL), row),
                   pl.BlockSpec((tm, DQ_ALL), row), _w3((8, D)), _w3((8, QL))],
        out_shape=[jax.ShapeDtypeStruct((S, D), F32), jax.ShapeDtypeStruct((S, DLAT), BF16),
                   jax.ShapeDtypeStruct((S, LAT_ALL), BF16), jax.ShapeDtypeStruct((S, DQ_ALL), BF16),
                   jax.ShapeDtypeStruct((8, D), F32), jax.ShapeDtypeStruct((8, QL), F32)],
        compiler_params=_params("arbitrary"),
    )(dout, dq, dk, dv, x, cq_raw, ckv_raw, vec, tabs[0], tabs[1], *[mw[k] for k in MLA_WEIGHTS])


def mla_dw(h, dlat, cqn, dqcat, dql, qnope, olat, docat, ocat, du):
    S = h.shape[0]
    tk = min(DW_TK, S)
    nk = S // tk
    flat = lambda w: pl.BlockSpec((tk, w), lambda k: (k, 0))
    cols = lambda w: pl.BlockSpec((tk, w), lambda n, k: (k, n))
    pair_o = pl.BlockSpec((None, 2 * KVL, 128), lambda n, k: (n, 0, 0))
    g = {}
    g["in"] = dw_matmul("mla_dw_in", h, dlat, flat(D), flat(DLAT), (D, DLAT),
                        pl.BlockSpec((D, DLAT), lambda k: (0, 0)), (nk,))
    g["q"] = dw_matmul("mla_dw_q", cqn, dqcat, flat(QL), flat(DQ_ALL), (QL, DQ_ALL),
                       pl.BlockSpec((QL, DQ_ALL), lambda k: (0, 0)), (nk,))
    g["uk"] = dw_matmul("mla_dw_uk", dql, qnope, cols(2 * KVL), cols(2 * NOPE), (N_PAIR, 2 * KVL, 2 * NOPE), pair_o,
                        (N_PAIR, nk))
    g["uv"] = dw_matmul("mla_dw_uv", olat, docat, cols(2 * KVL), cols(2 * VH), (N_PAIR, 2 * KVL, 2 * VH), pair_o,
                        (N_PAIR, nk))
    g["o"] = dw_matmul("mla_dw_o", ocat, du, cols(256), pl.BlockSpec((tk, D), lambda n, k: (k, 0)), (D, D),
                       pl.BlockSpec((256, D), lambda n, k: (n, 0)), (D // 256, nk))
    return g


def loss_head(y, target):
    S = y.shape[0]
    tm = min(512, S)

    def body(y_ref, t_ref, loss_ref, dy_ref):
        @pl.when(pl.program_id(0) == 0)
        def _():
            loss_ref[...] = jnp.zeros_like(loss_ref)

        err = y_ref[...] - t_ref[...]
        dy_ref[...] = err * (1.0 / D)
        loss_ref[...] += 0.5 * jnp.sum(jnp.mean(err * err, axis=-1, keepdims=True), axis=0, keepdims=True)

    row = lambda i: (i, 0)
    return pl.pallas_call(
        body, name="loss_head", grid=(S // tm,),
        in_specs=[pl.BlockSpec((tm, D), row), pl.BlockSpec((tm, D), row)],
        out_specs=[pl.BlockSpec((1, 1), lambda i: (0, 0)), pl.BlockSpec((tm, D), row)],
        out_shape=[jax.ShapeDtypeStruct((1, 1), F32), jax.ShapeDtypeStruct((S, D), F32)],
        compiler_params=_params("arbitrary"),
    )(y, target)


MOD_COLS = 9 * D // N_CHIP


def mod_fwd(c_pad, ada_w, ada_b_loc):
    tn = MOD_COLS // 3

    def body(c_ref, w_ref, b_ref, o_ref):
        c = c_ref[...]
        sc = (c * jax.nn.sigmoid(c)).astype(BF16)
        o_ref[...] = _dot(sc, w_ref[...].astype(BF16)) + b_ref[...]

    return pl.pallas_call(
        body, name="mod_fwd", grid=(2, 3),
        in_specs=[pl.BlockSpec((16, D), lambda i, n: (0, 0)), pl.BlockSpec((None, D, tn), lambda i, n: (i, 0, n)),
                  pl.BlockSpec((None, 1, tn), lambda i, n: (i, 0, n))],
        out_specs=pl.BlockSpec((None, 16, tn), lambda i, n: (i, 0, n)),
        out_shape=jax.ShapeDtypeStruct((2, 16, MOD_COLS), F32),
        compiler_params=_params("parallel", "parallel"),
    )(c_pad, ada_w, ada_b_loc)


def _adamw_math(w, g, m, v):
    m = ADAM_B1 * m + (1.0 - ADAM_B1) * g
    v = ADAM_B2 * v + (1.0 - ADAM_B2) * (g * g)
    m_hat = m / (1.0 - ADAM_B1 ** ADAM_STEP)
    v_hat = v / (1.0 - ADAM_B2 ** ADAM_STEP)
    delta = -ADAM_LR * (m_hat / (jnp.sqrt(v_hat) + ADAM_EPS) + ADAM_WD * w)
    return delta, m, v


def adamw(name, w, g, m, v, part=None, prev=None, copy_grad=False):
    shape = w.shape
    if part is None and w.size * 4 <= (1 << 20):
        whole = pl.BlockSpec(shape, lambda i: (0,) * len(shape))

        def small_body(w_ref, g_ref, m_ref, v_ref, d_ref, mo_ref, vo_ref):
            d_ref[...], mo_ref[...], vo_ref[...] = _adamw_math(w_ref[...], g_ref[...], m_ref[...], v_ref[...])

        return pl.pallas_call(
            small_body, name=name, grid=(1,), in_specs=[whole] * 4, out_specs=[whole] * 3,
            out_shape=[jax.ShapeDtypeStruct(shape, F32)] * 3, compiler_params=_params("arbitrary"),
        )(w, g, m, v)
    cols = shape[-1]
    rows = w.size // cols
    per_entry = rows // shape[0] if part is not None else rows
    tr = per_entry
    budget_rows = (2 << 20) // (cols * 4)
    for cand in range(min(per_entry, budget_rows) // 8 * 8, 0, -8):
        if per_entry % cand == 0:
            tr = cand
            break
    first, count = part if part is not None else (0, 1)
    tiles = per_entry // tr

    n_out = 4 if copy_grad else 3

    def body(w_ref, g_ref, m_ref, v_ref, *rest):
        outs = rest[-n_out:]
        outs[0][...], outs[1][...], outs[2][...] = _adamw_math(w_ref[...], g_ref[...], m_ref[...], v_ref[...])
        if copy_grad:
            outs[3][...] = g_ref[...]

    spec = pl.BlockSpec((tr, cols), lambda i: (i + first * tiles, 0))
    operands = [a.reshape(rows, cols) for a in (w, g, m, v)]
    aliases = {}
    if prev is not None:
        operands += [p.reshape(rows, cols) for p in prev]
        aliases = {4 + t: t for t in range(n_out)}
    outs = pl.pallas_call(
        body, name=name, grid=(count * tiles,), in_specs=[spec] * 4 + [_ANY] * (len(operands) - 4),
        out_specs=[spec] * n_out, out_shape=[jax.ShapeDtypeStruct((rows, cols), F32)] * n_out,
        input_output_aliases=aliases, compiler_params=_params("parallel"),
    )(*operands)
    return [o.reshape(shape) for o in outs]


def adamw_ada(c_pad, dmod, w, m, v):
    tr = 256

    def body(c_ref, dm_ref, w_ref, m_ref, v_ref, g_ref, d_ref, mo_ref, vo_ref):
        c = c_ref[...]
        sc = (c * jax.nn.sigmoid(c)).astype(BF16)
        g = _dot(sc, dm_ref[...].astype(BF16), TN)
        g_ref[...] = g
        d_ref[...], mo_ref[...], vo_ref[...] = _adamw_math(w_ref[...], g, m_ref[...], v_ref[...])

    wspec = pl.BlockSpec((None, tr, MOD_COLS), lambda i, r: (i, r, 0))
    return pl.pallas_call(
        body, name="adamw_ada", grid=(2, D // tr),
        in_specs=[pl.BlockSpec((16, tr), lambda i, r: (0, r)),
                  pl.BlockSpec((None, 16, MOD_COLS), lambda i, r: (i, 0, 0)), wspec, wspec, wspec],
        out_specs=[wspec] * 4,
        out_shape=[jax.ShapeDtypeStruct((2, D, MOD_COLS), F32)] * 4,
        compiler_params=_params("parallel", "parallel"),
    )(c_pad, dmod, w, m, v)


def sum_devices(name, a):
    _, R, C = a.shape
    tr = R
    for cand in (64, 32, 16, 8):
        if R % cand == 0:
            tr = cand
            break

    def body(a_ref, o_ref):
        acc = a_ref[0]
        for dev in range(1, N_DEV):
            acc = acc + a_ref[dev]
        o_ref[...] = acc

    return pl.pallas_call(
        body, name=name, grid=(R // tr,),
        in_specs=[pl.BlockSpec((N_DEV, tr, C), lambda i: (0, i, 0))],
        out_specs=pl.BlockSpec((tr, C), lambda i: (i, 0)),
        out_shape=jax.ShapeDtypeStruct((R, C), F32),
        compiler_params=_params("parallel"),
    )(a)


def _place():
    return lax.axis_index("x"), lax.axis_index("y"), lax.axis_index("c")


def _other_chips(x, y):
    return [(1 - x, y), (x, 1 - y), (1 - x, 1 - y)]


def gather_devices(name, a):
    m_per, n = a.shape

    def body(x_ref, out_ref, send_sems, recv_sems, local_sem):
        x, y, c = _place()
        me, sibling = (x, y, c), (x, y, 1 - c)
        chips = _other_chips(x, y)

        def rows(px, py, pc):
            return out_ref.at[pl.ds((4 * px + 2 * py + pc) * m_per, m_per), :]

        def copy(k, block, to, src=None):
            return pltpu.make_async_remote_copy(
                src_ref=rows(*block) if src is None else src, dst_ref=rows(*block),
                send_sem=send_sems.at[k], recv_sem=recv_sems.at[k], device_id=to, device_id_type=MESH)

        mine = pltpu.make_async_copy(x_ref, rows(*me), local_sem)
        mine.start()
        first = [copy(0, me, sibling, src=x_ref)]
        first += [copy(1 + j, me, (*chip, c), src=x_ref) for j, chip in enumerate(chips)]
        for cp in first:
            cp.start()
        passed = [copy(4 + j, (*chip, c), sibling) for j, chip in enumerate(chips)]
        for j, chip in enumerate(chips):
            copy(1 + j, (*chip, c), me).wait_recv()
            passed[j].start()
        copy(0, sibling, me).wait_recv()
        for j, chip in enumerate(chips):
            copy(4 + j, (*chip, 1 - c), me).wait_recv()
        for cp in first + passed:
            cp.wait_send()
        mine.wait()

    out = pl.pallas_call(
        body, name=name,
        out_shape=jax.ShapeDtypeStruct((N_DEV * m_per, n), a.dtype),
        in_specs=[pl.BlockSpec(memory_space=pltpu.VMEM)],
        out_specs=pl.BlockSpec(memory_space=pltpu.VMEM),
        scratch_shapes=[pltpu.SemaphoreType.DMA((7,)), pltpu.SemaphoreType.DMA((7,)), pltpu.SemaphoreType.DMA],
        compiler_params=pltpu.CompilerParams(vmem_limit_bytes=VMEM_LIMIT),
    )(a)
    return out.reshape(N_DEV, m_per, n)


_ANY = pl.BlockSpec(memory_space=pl.ANY)


def _hbm_ref(a):
    return jax.new_ref(a, memory_space=pltpu.MemorySpace.HBM)


def _hbm_empty(shape, dtype):
    return jax.empty_ref(jax.ShapeDtypeStruct(shape, dtype), memory_space=pltpu.MemorySpace.HBM)


ID_PAIR, ID_CHIPS, ID_SHARE, ID_UKV = 8, 9, 10, 11


def _sequencer(name, collective_id, n_sem, peers_of, program):
    sems = pltpu.SemaphoreType.DMA((n_sem,))

    @pl.kernel(mesh=plsc.ScalarSubcoreMesh(axis_name="seq", num_cores=1), name=name, scratch_types=[sems, sems],
               compiler_params=pltpu.CompilerParams(collective_id=collective_id))
    def launch(send_sem, recv_sem):
        x, y, c = _place()
        peers = peers_of(x, y, c)
        barrier = pltpu.get_barrier_semaphore()
        for peer in peers:
            pl.semaphore_signal(barrier, inc=1, device_id=peer, device_id_type=MESH)
        pl.semaphore_wait(barrier, len(peers))
        program(x, y, c, send_sem, recv_sem)

    launch()


def gather_weights(name, stage, arrays):
    n = len(arrays)
    refs = [_hbm_ref(a) for a in arrays]

    def program(x, y, c, send_sem, recv_sem):
        me = 2 * x + y
        chips = _other_chips(x, y)

        def ici(t, r, half):
            cx, cy = chips[r]
            mine = refs[t].at[me, half]
            return pltpu.make_async_remote_copy(
                src_ref=mine, dst_ref=mine, send_sem=send_sem.at[3 * t + r], recv_sem=recv_sem.at[3 * t + r],
                device_id=(cx, cy, c), device_id_type=MESH)

        def d2d(t, r, half):
            cx, cy = chips[r]
            there = refs[t].at[2 * cx + cy, half]
            k = 3 * n + 3 * t + r
            return pltpu.make_async_remote_copy(
                src_ref=there, dst_ref=there, send_sem=send_sem.at[k], recv_sem=recv_sem.at[k],
                device_id=(x, y, 1 - c), device_id_type=MESH)

        for t in range(n):
            for r in range(3):
                ici(t, r, c).start()
        for t in range(n):
            for r in range(3):
                ici(t, r, c).wait_recv()
                d2d(t, r, c).start()
        for t in range(n):
            for r in range(3):
                d2d(t, r, 1 - c).wait_recv()
        for t in range(n):
            for r in range(3):
                ici(t, r, c).wait_send()
                d2d(t, r, c).wait_send()

    _sequencer(name, stage, 6 * n, lambda x, y, c: [(x, y, 1 - c)] + [(cx, cy, c) for cx, cy in _other_chips(x, y)],
               program)
    return [r[...] for r in refs]


def cast_into_slots(name, chip, shards, after=None):
    steps = 2
    n = len(shards)

    def body(chip_ref, *refs):
        for src, dst in zip(refs[:n], refs[-n - 1:-1]):
            dst[...] = src[...].astype(BF16)
        refs[-1][...] = jnp.zeros_like(refs[-1])

    token_spec = pl.BlockSpec((8, 128), lambda h, i, chip_ref: (0, 0))

    def spec_in(a, prefix):
        R, C = a.shape[-2:]
        return pl.BlockSpec((None,) * (len(prefix) + 1) + (R // steps, C), lambda h, i, chip_ref: prefix + (h, i, 0))

    def spec_out(a):
        R, C = a.shape[-2:]
        return pl.BlockSpec((None, None, R // steps, C), lambda h, i, chip_ref: (chip_ref[0], h, i, 0))

    outs = pl.pallas_call(
        body, name=name,
        grid_spec=pltpu.PrefetchScalarGridSpec(
            num_scalar_prefetch=1, grid=(2, steps),
            in_specs=[spec_in(a, p) for a, p in shards] + ([token_spec] if after is not None else []),
            out_specs=[spec_out(a) for a, _ in shards] + [token_spec]),
        out_shape=[jax.ShapeDtypeStruct((N_CHIP, 2) + a.shape[-2:], BF16) for a, _ in shards]
        + [jax.ShapeDtypeStruct((8, 128), F32)],
        compiler_params=_params("arbitrary", "arbitrary"),
    )(chip, *[a for a, _ in shards], *([after] if after is not None else []))
    return outs[:-1], outs[-1]


def reduce_pair(name, grads):
    n = len(grads)
    src = [_hbm_ref(g) for g in grads]
    dst = [_hbm_empty((N_CHIP,) + g.shape[2:], g.dtype) for g in grads]

    def program(x, y, c, send_sem, recv_sem):
        cps = [pltpu.make_async_remote_copy(
            src_ref=src[t].at[:, 1 - c], dst_ref=dst[t], send_sem=send_sem.at[t], recv_sem=recv_sem.at[t],
            device_id=(x, y, 1 - c), device_id_type=MESH) for t in range(n)]
        for cp in cps:
            cp.start()
        for cp in cps:
            cp.wait()

    _sequencer(name, ID_PAIR, n, lambda x, y, c: [(x, y, 1 - c)], program)
    return [r[...] for r in src], [r[...] for r in dst]


def pair_add(name, core, g, got):
    _, _, R, C = g.shape

    def body(core_ref, g_ref, got_ref, o_ref):
        o_ref[...] = (g_ref[...] + got_ref[...]).astype(BF16)

    return pl.pallas_call(
        body, name=name,
        grid_spec=pltpu.PrefetchScalarGridSpec(
            num_scalar_prefetch=1, grid=(N_CHIP,),
            in_specs=[pl.BlockSpec((None, None, R, C), lambda q, core_ref: (q, core_ref[0], 0, 0)),
                      pl.BlockSpec((None, R, C), lambda q, core_ref: (q, 0, 0))],
            out_specs=pl.BlockSpec((None, R, C), lambda q, core_ref: (q, 0, 0))),
        out_shape=jax.ShapeDtypeStruct((N_CHIP, R, C), BF16),
        compiler_params=_params("parallel"),
    )(core, g, got)


def reduce_chips(name, sums):
    n = len(sums)
    src = [_hbm_ref(s) for s in sums]
    dst = [_hbm_empty((3,) + s.shape[1:], s.dtype) for s in sums]

    def program(x, y, c, send_sem, recv_sem):
        cps = []
        for t in range(n):
            for r, (cx, cy) in enumerate(_other_chips(x, y)):
                cps.append(pltpu.make_async_remote_copy(
                    src_ref=src[t].at[2 * cx + cy], dst_ref=dst[t].at[r],
                    send_sem=send_sem.at[3 * t + r], recv_sem=recv_sem.at[3 * t + r],
                    device_id=(cx, cy, c), device_id_type=MESH))
        for cp in cps:
            cp.start()
        for cp in cps:
            cp.wait()

    _sequencer(name, ID_CHIPS, 3 * n, lambda x, y, c: [(cx, cy, c) for cx, cy in _other_chips(x, y)], program)
    return [r[...] for r in src], [r[...] for r in dst]


def chip_add(name, place, s, got, k, n_slots, prev=None, after=None):
    _, R, C = s.shape

    def body(place_ref, s_ref, got_ref, *rest):
        o_ref = rest[-1]
        o_ref[...] = ((s_ref[...].astype(F32) + got_ref[0].astype(F32)) + got_ref[1].astype(F32)) + got_ref[2].astype(F32)

    in_specs = [pl.BlockSpec((None, R, C), lambda i, place_ref: (place_ref[0], 0, 0)),
                pl.BlockSpec((3, R, C), lambda i, place_ref: (0, 0, 0))]
    args = [place, s, got]
    aliases = {}
    if prev is not None:
        in_specs.append(_ANY)
        args.append(prev)
        aliases = {3: 0}
    for piece in after or ():
        in_specs.append(pl.BlockSpec((8, 128), lambda i, place_ref: (0, 0)))
        args.append(piece)
    return pl.pallas_call(
        body, name=name,
        grid_spec=pltpu.PrefetchScalarGridSpec(
            num_scalar_prefetch=1, grid=(1,), in_specs=in_specs,
            out_specs=pl.BlockSpec((None, None, R, C), lambda i, place_ref: (k, place_ref[1], 0, 0))),
        out_shape=jax.ShapeDtypeStruct((n_slots, 2, R, C), F32),
        input_output_aliases=aliases,
        compiler_params=_params("arbitrary"),
    )(*args)


def share_halves(name, stacks, slots):
    n = len(stacks)
    dst = [_hbm_ref(s) for s in stacks]

    def program(x, y, c, send_sem, recv_sem):
        cps = [pltpu.make_async_remote_copy(
            src_ref=dst[t].at[slots[t], c], dst_ref=dst[t].at[slots[t], c],
            send_sem=send_sem.at[t], recv_sem=recv_sem.at[t],
            device_id=(x, y, 1 - c), device_id_type=MESH) for t in range(n)]
        for cp in cps:
            cp.start()
        for cp in cps:
            cp.wait()

    _sequencer(name, ID_SHARE, n, lambda x, y, c: [(x, y, 1 - c)], program)
    return [r[...] for r in dst]


def gather_blocks(name, slotted):
    out = _hbm_ref(slotted)

    def program(x, y, c, send_sem, recv_sem):
        sibling = (x, y, 1 - c)
        chips = _other_chips(x, y)

        def copy(k, px, py, pc, to):
            block = out.at[4 * px + 2 * py + pc]
            return pltpu.make_async_remote_copy(src_ref=block, dst_ref=block, send_sem=send_sem.at[k],
                                                recv_sem=recv_sem.at[k], device_id=to, device_id_type=MESH)

        first = [copy(0, x, y, c, sibling)] + [copy(1 + j, x, y, c, (cx, cy, c)) for j, (cx, cy) in enumerate(chips)]
        for cp in first:
            cp.start()
        passed = [copy(4 + j, cx, cy, c, sibling) for j, (cx, cy) in enumerate(chips)]
        for j, (cx, cy) in enumerate(chips):
            copy(1 + j, cx, cy, c, (x, y, c)).wait_recv()
            passed[j].start()
        copy(0, x, y, 1 - c, (x, y, c)).wait_recv()
        for j, (cx, cy) in enumerate(chips):
            copy(4 + j, cx, cy, 1 - c, (x, y, c)).wait_recv()
        for cp in first + passed:
            cp.wait_send()

    _sequencer(name, ID_UKV, 7, lambda x, y, c: [(x, y, 1 - c)] + [(cx, cy, c) for cx, cy in _other_chips(x, y)],
               program)
    return out[...]


def place_block(name, dev, a):
    M, N = a.shape
    tr = min(M, 64)

    def body(dev_ref, a_ref, o_ref):
        o_ref[...] = a_ref[...]

    return pl.pallas_call(
        body, name=name,
        grid_spec=pltpu.PrefetchScalarGridSpec(
            num_scalar_prefetch=1, grid=(M // tr,),
            in_specs=[pl.BlockSpec((tr, N), lambda i, dev_ref: (i, 0))],
            out_specs=pl.BlockSpec((None, tr, N), lambda i, dev_ref: (dev_ref[0], i, 0))),
        out_shape=jax.ShapeDtypeStruct((N_DEV, M, N), a.dtype),
        compiler_params=_params("parallel"),
    )(dev, a)


def _swap_rope(a):
    return jnp.concatenate([a[..., ROPE // 2:], a[..., :ROPE // 2]], axis=-1)


def _rope_tables(S):
    inv = 1.0 / (ROPE_THETA ** (jnp.arange(0, ROPE, 2, dtype=F32) / ROPE))
    ang = jnp.arange(S, dtype=F32)[:, None] * inv[None, :]
    cos, sin = jnp.cos(ang), jnp.sin(ang)
    return (jnp.tile(jnp.concatenate([cos, cos], axis=1), (1, SLOTS)),
            jnp.tile(jnp.concatenate([-sin, sin], axis=1), (1, SLOTS)))


def _vec(norm_g, mod, i, k):
    rows = [norm_g[i, 2 * k], norm_g[i, 2 * k + 1], mod[i, 3 * k], mod[i, 3 * k + 1], mod[i, 3 * k + 2]]
    return jnp.concatenate([jnp.stack(rows), jnp.zeros((3, D), F32)], axis=0)


def _unpack_weights(full, w_uk, w_uv, q_norm, kv_norm):
    G = D // 4
    ffn_in = [[full[2 * i + k].reshape(N_CHIP, D, FSH) for k in range(2)] for i in range(2)]
    ffn_out = [[full[4 + 2 * i + k].reshape(2, FSH, D) for k in range(2)] for i in range(2)]
    pw = full[8].reshape(N_CHIP, 4, G // N_CHIP, G).transpose(1, 0, 2, 3).reshape(4, G, G)
    w_in = full[9].reshape(D, QL + KVL + ROPE)
    w_uq = full[10].reshape(QL, N_HEADS, NOPE + ROPE)
    wkr = w_in[:, QL + KVL:]
    wr = w_uq[:, :, NOPE:]
    eye2 = jnp.eye(2, dtype=BF16)
    uk_t = jnp.transpose(w_uk, (1, 2, 0)).reshape(N_PAIR, 2, NOPE, KVL)
    bduk = jnp.einsum("janc,ab->janbc", uk_t, eye2).reshape(N_PAIR, 2 * NOPE, 2 * KVL)
    uv = jnp.transpose(w_uv, (1, 0, 2)).reshape(N_PAIR, 2, KVL, VH)
    bduv = jnp.einsum("jacn,ab->jacbn", uv, eye2).reshape(N_PAIR, 2 * KVL, 2 * VH)
    mw = dict(wq=w_in[:, :QL], wkv=w_in[:, QL:QL + KVL], wkr4=jnp.tile(wkr, (1, SLOTS)),
              wkrs4=jnp.tile(_swap_rope(wkr), (1, SLOTS)), qn=q_norm, kvn=kv_norm,
              wn=w_uq[:, :, :NOPE].reshape(QL, NOPE_ALL), wr=wr.reshape(QL, ROPE_ALL),
              wrs=_swap_rope(wr).reshape(QL, ROPE_ALL), bduk=bduk)
    return ffn_in, ffn_out, pw, mw, bduv, full[11].reshape(D, D)


def _example_step(x, target, mod, norm_g, pvec, ffn_in, ffn_out, pw, mw, bduv, wo, reducer):
    S = x.shape[0]
    tabs = _rope_tables(S)
    vec = [[_vec(norm_g, mod, i, k) for k in range(3)] for i in range(2)]
    saved = {}
    for i in range(2):
        xin = x
        x, a, u, h = ffn_fwd(xin, vec[i][0], ffn_in[i][0], ffn_out[i][0], 0.5)
        saved[i, 0] = (xin, a, u, h)
        xin = x
        if i == 0:
            x, y, z = pool_fwd(xin, vec[i][1], pw, pvec)
            saved[i, 1] = (xin, y, z)
        else:
            h_m, cq_raw, ckv_raw, cqn, qnope, qcat, kcat, vcat = mla_pre(xin, vec[i][1], mw, tabs)
            olat, lse = attn_fwd(qcat, kcat, vcat)
            x, u_m, ocat = mla_post(olat, xin, vec[i][1], bduv, wo)
            saved[i, 1] = (xin, h_m, cq_raw, ckv_raw, cqn, qnope, qcat, kcat, olat, lse, u_m, ocat)
        xin = x
        x, a, u, h = ffn_fwd(xin, vec[i][2], ffn_in[i][1], ffn_out[i][1], 0.5)
        saved[i, 2] = (xin, a, u, h)
    loss, dx = loss_head(x, target)

    vg = {}
    G = D // 4

    def ffn_grads(i, k, dw_in, dw_out):
        return [(0, 2 * i + k, 4, dw_in.reshape(N_CHIP, 2, D // 2, FSH)),
                (1, 2 * i + k, 4, dw_out.reshape(N_CHIP, 2, DFF // 8, D))]

    for i in (1, 0):
        xin, a, u, h = saved[i, 2]
        dx, du, act, da, vg[i, 2] = ffn_bwd(dx, xin, u, a, vec[i][2], ffn_in[i][1], ffn_out[i][1], 0.5)
        reducer.advance()
        reducer.add(f"f{i}1", ffn_grads(i, 1, *ffn_dw(h, da, act, du)))
        if i == 0:
            xin, y, z = saved[i, 1]
            dx, vg[i, 1], pgrad, g_pool = pool_bwd(dx, xin, y, z, vec[i][1], pw, pvec)
            reducer.advance()
        else:
            xin, h_m, cq_raw, ckv_raw, cqn, qnope, qcat, kcat, olat, lse, u_m, ocat = saved[i, 1]
            du, docat, dolat, delta, vg_post = mla_post_bwd(dx, u_m, olat, vec[i][1], bduv, wo)
            reducer.advance()
            dq, dk, dv = attn_bwd(qcat, kcat, kcat.T, dolat, lse, delta)
            reducer.advance()
            dx, dlat, dql, dqcat, vg_pre, ngrad = mla_pre_bwd(
                dx, dq, dk, dv, xin, cq_raw, ckv_raw, vec[i][1], mw, tabs)
            vg[i, 1] = vg_post + vg_pre
            g = mla_dw(h_m, dlat, cqn, dqcat, dql, qnope, olat, docat, ocat, du)
            slots = lambda a: a.reshape(D, SLOTS, ROPE).sum(axis=1)
            g_kr = slots(g["in"][:, QL + KVL:QL + KVL + 128]) + _swap_rope(slots(g["in"][:, QL + KVL + 128:]))
            g_in = jnp.concatenate([g["in"][:, :QL + KVL], g_kr], axis=1)
            g_r = g["q"][:, NOPE_ALL:NOPE_ALL + ROPE_ALL].reshape(QL, N_HEADS, ROPE)
            g_rs = g["q"][:, NOPE_ALL + ROPE_ALL:].reshape(QL, N_HEADS, ROPE)
            g_uq = jnp.concatenate([g["q"][:, :NOPE_ALL].reshape(QL, N_HEADS, NOPE), g_r + _swap_rope(g_rs)], axis=-1)

            def heads(pairs):
                blk = pairs.reshape(N_PAIR, 2, KVL, 2, NOPE)
                per_head = jnp.stack([blk[:, 0, :, 0, :], blk[:, 1, :, 1, :]], axis=1).reshape(N_HEADS, KVL, NOPE)
                return jnp.transpose(per_head, (1, 0, 2)).reshape(KVL, N_HEADS * NOPE)

            reducer.add("mla", [(3, 0, 1, g_in.reshape(N_CHIP, 2, D // 8, QL + KVL + ROPE)),
                                (4, 0, 1, g_uq.reshape(N_CHIP, 2, QL // 8, N_HEADS * (NOPE + ROPE))),
                                (5, 0, 1, g["o"].reshape(N_CHIP, 2, D // 8, D))])
            reducer.add_replicated(jnp.concatenate([heads(g["uk"]), heads(g["uv"])], axis=0))
        xin, a, u, h = saved[i, 0]
        dx, du, act, da, vg[i, 0] = ffn_bwd(dx, xin, u, a, vec[i][0], ffn_in[i][0], ffn_out[i][0], 0.5)
        if i == 1:
            reducer.advance()
        grads = ffn_grads(i, 0, *ffn_dw(h, da, act, du))
        if i == 0:
            grads.append((2, 0, 1, g_pool.reshape(N_CHIP, 2, 2 * G // N_CHIP, G)))
        reducer.add(f"f{i}0", grads)
    return loss, dx, vg, pgrad, ngrad


class _GradReducer:
    def __init__(self, core, place, dev):
        self.core, self.place, self.dev = core, place, dev
        self.stacks = {}
        self.live = []
        self.replicated = None

    def add(self, tag, items):
        gen = self._run(tag, items)
        next(gen)
        self.live.append(gen)

    def add_replicated(self, block):
        self.replicated = gather_blocks("gather_ukv", place_block("place_ukv", self.dev, block))

    def advance(self, after=None):
        self.after = after
        live = []
        for gen in self.live:
            try:
                next(gen)
                live.append(gen)
            except StopIteration:
                pass
        self.live = live

    def finish(self):
        while self.live:
            self.advance()
        return self.stacks, self.replicated

    def _run(self, tag, items):
        grads, from_pair = reduce_pair(f"reduce_pair_{tag}", [g for *_, g in items])
        yield
        sums = [pair_add(f"pair_add_{tag}_{j}", self.core, g, p) for j, (g, p) in enumerate(zip(grads, from_pair))]
        sums, from_chips = reduce_chips(f"reduce_chips_{tag}", sums)
        yield
        for j, ((o, k, n_slots, _), s, p) in enumerate(zip(items, sums, from_chips)):
            self.stacks[o] = chip_add(f"chip_add_{tag}_{j}", self.place, s, p, k, n_slots, self.stacks.get(o),
                                      self.after)
        shared = share_halves(f"share_halves_{tag}", [self.stacks[o] for o, *_ in items], [k for _, k, *_ in items])
        for (o, *_), v in zip(items, shared):
            self.stacks[o] = v


SMALL_IN = 8 * 640
SMALL_GRAD = 8 * 4224
SMALL_W = 8 * 2944


def _pack(parts, total):
    flat = jnp.concatenate([p.reshape(-1) for p in parts])
    return jnp.concatenate([flat, jnp.zeros((total - flat.shape[0],), F32)]).reshape(8, total // 8)


def kernel(x, c, ada_w, ada_b, norm_g, ffn_w_in, ffn_w_out, pool_w, pool_b, pool_scale, mla_w_in, mla_q_norm, mla_kv_norm, mla_w_uq, mla_w_uk, mla_w_uv, mla_w_o, loss_target, m_ada_w, m_ada_b, m_norm_g, m_ffn_w_in, m_ffn_w_out, m_pool_w, m_pool_b, m_pool_scale, m_mla_w_in, m_mla_q_norm, m_mla_kv_norm, m_mla_w_uq, m_mla_w_uk, m_mla_w_uv, m_mla_w_o, v_ada_w, v_ada_b, v_norm_g, v_ffn_w_in, v_ffn_w_out, v_pool_w, v_pool_b, v_pool_scale, v_mla_w_in, v_mla_q_norm, v_mla_kv_norm, v_mla_w_uq, v_mla_w_uk, v_mla_w_uv, v_mla_w_o):
    ix, iy, ic = _place()
    chip = 2 * ix + iy
    dev = 2 * chip + ic
    core_arr = ic.astype(jnp.int32).reshape(1)
    chip_arr = chip.astype(jnp.int32).reshape(1)
    S = x.shape[1]
    G = D // 4
    NG = D // N_CHIP

    def chip_cols(a, width, axis):
        return lax.dynamic_slice_in_dim(a, chip * width, width, axis)

    got = gather_devices("gather_small_in", _pack([c, norm_g, pool_b, mla_q_norm], SMALL_IN)).reshape(N_DEV, SMALL_IN)
    c_all = got[:, :D]
    parts = got[0::2]
    o = D
    norm_g_full = parts[:, o:o + 12 * NG].reshape(N_CHIP, 2, 6, NG).transpose(1, 2, 0, 3).reshape(2, 6, D)
    o += 12 * NG
    pool_b_full = parts[:, o:o + G].reshape(N_CHIP, 4, G // N_CHIP).transpose(1, 0, 2).reshape(1, D)
    o += G
    q_norm_full = parts[:, o:o + QL // N_CHIP].reshape(1, QL)
    pvec = jnp.concatenate([pool_b_full, pool_scale, jnp.zeros((6, D), F32)], axis=0)

    c_pad = jnp.concatenate([c_all, jnp.zeros((8, D), F32)], axis=0)
    mod_loc = mod_fwd(c_pad, ada_w, chip_cols(ada_b, MOD_COLS, 1).reshape(2, 1, MOD_COLS))
    got = gather_devices("gather_mod", mod_loc[:, :8].transpose(1, 0, 2).reshape(8, 2 * MOD_COLS))
    mine = lax.dynamic_index_in_dim(got[0::2].reshape(N_CHIP, 8, 2, MOD_COLS), dev, axis=1, keepdims=False)
    mod = mine.transpose(1, 0, 2).reshape(2, 9, D)

    bf = lambda a: a.astype(BF16)
    w_in_halves = ffn_w_in.reshape(2, 2, 2, D // 2, FSH)
    w_out_halves = ffn_w_out.reshape(2, 2, 2, DFF // 8, D)
    shards = [(w_in_halves, (i, k)) for i in range(2) for k in range(2)]
    shards += [(w_out_halves, (i, k)) for i in range(2) for k in range(2)]
    shards += [(pool_w.reshape(2, 2 * G // N_CHIP, G), ()), (mla_w_in.reshape(2, D // 8, QL + KVL + ROPE), ()),
               (mla_w_uq.reshape(2, QL // 8, N_HEADS * (NOPE + ROPE)), ()), (mla_w_o.reshape(2, D // 8, D), ())]
    full = [None] * len(shards)
    stages = [(0, 4, 8), (1, 5), (2, 6), (9, 10, 11), (3, 7)]
    first, token = cast_into_slots("cast_first", chip_arr, [shards[t] for t in stages[0]])
    slotted = dict(zip(stages[0], first))
    rest = [t for members in stages[1:] for t in members]
    for stage, members in enumerate(stages):
        got_w = gather_weights(f"gather_weights_{stage}", stage, [slotted[t] for t in members])
        for t, a in zip(members, got_w):
            full[t] = a
        if stage == 0:
            slotted.update(zip(rest, cast_into_slots("cast_rest", chip_arr, [shards[t] for t in rest], token)[0]))
    ffn_in, ffn_out, pw, mw, bduv, wo = _unpack_weights(full, bf(mla_w_uk[0]), bf(mla_w_uv[0]), q_norm_full,
                                                        mla_kv_norm)

    place_arr = jnp.stack([chip, ic]).astype(jnp.int32)
    reducer = _GradReducer(core_arr, place_arr, dev.astype(jnp.int32).reshape(1))
    loss_mine, grad_x, vg, pgrad, ngrad = _example_step(
        x[0], loss_target[0], mod, norm_g_full, pvec, ffn_in, ffn_out, pw, mw, bduv, wo, reducer)

    dmod = jnp.stack([jnp.concatenate([vg[i, k][0:3] for k in range(3)]) for i in range(2)])
    dnorm = jnp.stack([jnp.concatenate([vg[i, k][3:5] for k in range(3)]) for i in range(2)])
    small = _pack([dmod, dnorm, pgrad[0], pgrad[1], ngrad[0], ngrad[1, :KVL], loss_mine], SMALL_GRAD)
    got = gather_devices("gather_small_grad", small)
    tot = sum_devices("sum_small_grad", got).reshape(-1)
    n_mod = 2 * 9 * D
    g_ada_b = tot[:n_mod].reshape(ada_b.shape)
    o = n_mod
    g_norm = chip_cols(tot[o:o + 12 * D].reshape(2, 6, D), NG, 2)
    o += 12 * D
    g_pool_b = chip_cols(tot[o:o + D].reshape(1, 4, G), G // N_CHIP, 2)
    o += D
    g_pool_scale = tot[o:o + D].reshape(pool_scale.shape)
    o += D
    g_q_norm = chip_cols(tot[o:o + QL].reshape(1, QL), QL // N_CHIP, 1)
    o += QL
    g_kv_norm = tot[o:o + KVL].reshape(mla_kv_norm.shape)
    loss = tot[o + KVL]
    dmod_all = chip_cols(got.reshape(N_DEV, -1)[:, :n_mod].reshape(N_DEV, 2, 9 * D), MOD_COLS, 2)
    dmod_pad = jnp.concatenate([dmod_all.transpose(1, 0, 2), jnp.zeros((2, 8, MOD_COLS), F32)], axis=1)

    g_ada_w, d_ada_w, nm_ada_w, nv_ada_w = adamw_ada(c_pad, dmod_pad, ada_w, m_ada_w, v_ada_w)
    small_names = ["ada_b", "norm_g", "pool_b", "pool_scale", "mla_q_norm", "mla_kv_norm"]
    small_w = [ada_b, norm_g, pool_b, pool_scale, mla_q_norm, mla_kv_norm]
    small_g = [g_ada_b, g_norm, g_pool_b, g_pool_scale, g_q_norm, g_kv_norm]
    small_m = [m_ada_b, m_norm_g, m_pool_b, m_pool_scale, m_mla_q_norm, m_mla_kv_norm]
    small_v = [v_ada_b, v_norm_g, v_pool_b, v_pool_scale, v_mla_q_norm, v_mla_kv_norm]
    packed = adamw("adamw_small", *[_pack(p, SMALL_W) for p in (small_w, small_g, small_m, small_v)])
    upd = {}
    o = 0
    for name, w in zip(small_names, small_w):
        upd[name] = [p.reshape(-1)[o:o + w.size].reshape(w.shape) for p in packed]
        o += w.size
    upd["ada_w"] = [d_ada_w, nm_ada_w, nv_ada_w]

    reducer.advance(after=(d_ada_w[0, :8, :128],))
    ffn = [("ffn_w_in", 0, ffn_w_in, m_ffn_w_in, v_ffn_w_in), ("ffn_w_out", 1, ffn_w_out, m_ffn_w_out, v_ffn_w_out)]
    slots = lambda a: a.reshape((4,) + a.shape[2:])
    early = {name: adamw(f"adamw_{name}_early", slots(w), slots(reducer.stacks[o].reshape(w.shape)), slots(m),
                         slots(v), part=(1, 3), copy_grad=True) for name, o, w, m, v in ffn}
    g_mla_in = reducer.stacks[3].reshape(mla_w_in.shape)
    g_uq = reducer.stacks[4].reshape(mla_w_uq.shape)
    g_wo = reducer.stacks[5].reshape(mla_w_o.shape)
    for name, w, g, m, v in [("mla_w_in", mla_w_in, g_mla_in, m_mla_w_in, v_mla_w_in),
                             ("mla_w_uq", mla_w_uq, g_uq, m_mla_w_uq, v_mla_w_uq),
                             ("mla_w_o", mla_w_o, g_wo, m_mla_w_o, v_mla_w_o)]:
        upd[name] = adamw("adamw_" + name, w, g, m, v)

    reducer.advance(after=(early["ffn_w_in"][0][1, :8, :128], early["ffn_w_out"][0][1, :8, :128],
                           upd["mla_w_o"][0][0, :8, :128], upd["mla_w_in"][0][0, :8, :128]))
    ukv = sum_devices("sum_ukv", reducer.replicated)
    g_uk = ukv[:KVL].reshape(mla_w_uk.shape)
    g_uv = ukv[KVL:].reshape(mla_w_uv.shape)
    upd["mla_w_uk"] = adamw("adamw_mla_w_uk", mla_w_uk, g_uk, m_mla_w_uk, v_mla_w_uk)
    upd["mla_w_uv"] = adamw("adamw_mla_w_uv", mla_w_uv, g_uv, m_mla_w_uv, v_mla_w_uv)
    stacks, _ = reducer.finish()
    g_pool_w = stacks[2].reshape(pool_w.shape)
    g_ffn = {}
    for name, o, w, m, v in ffn:
        done = adamw(f"adamw_{name}_last", slots(w), slots(stacks[o].reshape(w.shape)), slots(m), slots(v),
                     part=(0, 1), prev=early[name], copy_grad=True)
        upd[name] = [p.reshape(w.shape) for p in done[:3]]
        g_ffn[name] = done[3].reshape(w.shape)
    g_ffn_in, g_ffn_out = g_ffn["ffn_w_in"], g_ffn["ffn_w_out"]
    upd["pool_w"] = adamw("adamw_pool_w", pool_w, g_pool_w, m_pool_w, v_pool_w)

    order = ["ada_w", "ada_b", "norm_g", "ffn_w_in", "ffn_w_out", "pool_w", "pool_b", "pool_scale", "mla_w_in",
             "mla_q_norm", "mla_kv_norm", "mla_w_uq", "mla_w_uk", "mla_w_uv", "mla_w_o"]
    grad = dict(ada_w=g_ada_w, ada_b=g_ada_b, norm_g=g_norm, ffn_w_in=g_ffn_in, ffn_w_out=g_ffn_out, pool_w=g_pool_w,
                pool_b=g_pool_b, pool_scale=g_pool_scale, mla_w_in=g_mla_in, mla_q_norm=g_q_norm,
                mla_kv_norm=g_kv_norm, mla_w_uq=g_uq, mla_w_uk=g_uk, mla_w_uv=g_uv, mla_w_o=g_wo)
    return (loss, grad_x[None], *[grad[n] for n in order], *[upd[n][0] for n in order],
            *[upd[n][1] for n in order], *[upd[n][2] for n in order])
```

```python
import functools

import jax
import jax.numpy as jnp
from jax import lax
from jax.experimental import pallas as pl
from jax.experimental.pallas import tpu as pltpu
from jax.experimental.pallas import tpu_sc as plsc

F32 = jnp.float32
BF16 = jnp.bfloat16

D = 1024
DFF = 2816
FSH = 1408
N_CHIP = 4
N_DEV = 8
N_HEADS = 16
NOPE = 64
ROPE = 32
VH = 64
QL = 256
KVL = 128
QPAD = 256
EPS = 1e-6
ATTN_SCALE = (NOPE + ROPE) ** -0.5
ROPE_THETA = 10000.0
POOL_WINDOWS = (2, 4, 8, 16)
HALO = 8
ATTN_TQ = 1024
ATTN_KC = 512
ROW_TILE = 512
DW_TK = 2048

ADAM_LR, ADAM_B1, ADAM_B2, ADAM_EPS, ADAM_WD, ADAM_STEP = 0.001, 0.9, 0.999, 1e-08, 0.01, 10

VMEM_LIMIT = 60 * 1024 * 1024
MESH = pl.DeviceIdType.MESH

NT = (((1,), (1,)), ((), ()))
TN = (((0,), (0,)), ((), ()))


def _params(*sem):
    return pltpu.CompilerParams(dimension_semantics=sem, vmem_limit_bytes=VMEM_LIMIT)


def _dot(a, b, dims=None):
    if dims is None:
        return jnp.dot(a, b, preferred_element_type=F32)
    return lax.dot_general(a, b, dims, preferred_element_type=F32)


def _rms(x):
    r = lax.rsqrt(jnp.mean(x * x, axis=-1, keepdims=True) + EPS)
    return x * r, r


def _rms_bwd(xhat, r, dxhat):
    return r * (dxhat - xhat * jnp.mean(dxhat * xhat, axis=-1, keepdims=True))


def _as_row(col):
    return jnp.broadcast_to(col, (col.shape[0], 128)).T[0:1, :]


def _prenorm(x, vec_ref):
    xhat, r = _rms(x)
    h = xhat * vec_ref[0:1, :] * (1.0 + vec_ref[3:4, :]) + vec_ref[2:3, :]
    return h, xhat, r


def _postnorm_bwd(dout, u, vec_ref, weight):
    uhat, r = _rms(u)
    gt = weight * (1.0 + vec_ref[4:5, :])
    dy = dout * gt
    dgate_rows = (weight * dout) * (uhat * vec_ref[1:2, :])
    dgpost_rows = dy * uhat
    du = _rms_bwd(uhat, r, dy * vec_ref[1:2, :])
    return du, dgate_rows, dgpost_rows


def _prenorm_bwd(dh, x, vec_ref, vg_ref):
    xhat, r = _rms(x)
    sc1 = 1.0 + vec_ref[3:4, :]
    g = vec_ref[0:1, :]
    vg_ref[0:1, :] += jnp.sum(dh, axis=0, keepdims=True)
    vg_ref[1:2, :] += jnp.sum(dh * (xhat * g), axis=0, keepdims=True)
    vg_ref[3:4, :] += jnp.sum(dh * sc1 * xhat, axis=0, keepdims=True)
    return _rms_bwd(xhat, r, dh * g * sc1)


def ffn_fwd(x, vec, w_in, w_out, weight):
    S = x.shape[0]
    tm = min(512, S)
    row = lambda i: (i, 0)
    half = lambda j: [_w3((8, D)), pl.BlockSpec((None, D, FSH), lambda i: (j, 0, 0)),
                      pl.BlockSpec((None, D, FSH), lambda i: (j + 2, 0, 0)),
                      pl.BlockSpec((None, FSH, D), lambda i: (j, 0, 0))]
    a_spec = lambda j: pl.BlockSpec((2, tm, FSH), lambda i: (0, i, j))
    a_shape = jax.ShapeDtypeStruct((2, S, DFF), BF16)

    def hidden(hb, wg_ref, wu_ref, wo_ref, a_ref):
        g = _dot(hb, wg_ref[...])
        up = _dot(hb, wu_ref[...])
        a_ref[0] = g.astype(BF16)
        a_ref[1] = up.astype(BF16)
        act = (g * jax.nn.sigmoid(g)) * up
        return _dot(act.astype(BF16), wo_ref[...])

    def first(x_ref, vec_ref, wg_ref, wu_ref, wo_ref, h_ref, a_ref, u_ref):
        h, _, _ = _prenorm(x_ref[...], vec_ref)
        hb = h.astype(BF16)
        h_ref[...] = hb
        u_ref[...] = hidden(hb, wg_ref, wu_ref, wo_ref, a_ref)

    h, a, u_half = pl.pallas_call(
        first, name="ffn_fwd_first", grid=(S // tm,),
        in_specs=[pl.BlockSpec((tm, D), row)] + half(0),
        out_specs=[pl.BlockSpec((tm, D), row), a_spec(0), pl.BlockSpec((tm, D), row)],
        out_shape=[jax.ShapeDtypeStruct((S, D), BF16), a_shape, jax.ShapeDtypeStruct((S, D), F32)],
        compiler_params=_params("parallel"),
    )(x, vec, w_in, w_in, w_out)

    def second(x_ref, h_ref, uh_ref, vec_ref, wg_ref, wu_ref, wo_ref, a_in, xo_ref, a_ref, u_ref):
        u = uh_ref[...] + hidden(h_ref[...], wg_ref, wu_ref, wo_ref, a_ref)
        u_ref[...] = u
        uhat, _ = _rms(u)
        xo_ref[...] = x_ref[...] + (weight * (1.0 + vec_ref[4:5, :])) * (uhat * vec_ref[1:2, :])

    xo, a, u = pl.pallas_call(
        second, name="ffn_fwd_second", grid=(S // tm,),
        in_specs=[pl.BlockSpec((tm, D), row), pl.BlockSpec((tm, D), row), pl.BlockSpec((tm, D), row)] + half(1) + [_ANY],
        out_specs=[pl.BlockSpec((tm, D), row), a_spec(1), pl.BlockSpec((tm, D), row)],
        out_shape=[jax.ShapeDtypeStruct((S, D), F32), a_shape, jax.ShapeDtypeStruct((S, D), F32)],
        input_output_aliases={7: 1},
        compiler_params=_params("parallel"),
    )(x, h, u_half, vec, w_in, w_in, w_out, a)
    return xo, a, u, h


def ffn_bwd(dout, x, u, a, vec, w_in, w_out, weight):
    S = x.shape[0]
    tm = min(512, S)
    row = lambda i: (i, 0)
    half = lambda j: [pl.BlockSpec((2, tm, FSH), lambda i: (0, i, j)), _w3((8, D)),
                      pl.BlockSpec((None, D, FSH), lambda i: (j, 0, 0)),
                      pl.BlockSpec((None, D, FSH), lambda i: (j + 2, 0, 0)),
                      pl.BlockSpec((None, FSH, D), lambda i: (j, 0, 0))]
    half_out = lambda j: [pl.BlockSpec((tm, FSH), lambda i: (i, j)), pl.BlockSpec((2, tm, FSH), lambda i: (0, i, j))]
    half_shape = [jax.ShapeDtypeStruct((S, DFF), BF16), jax.ShapeDtypeStruct((2, S, DFF), BF16)]

    def hidden_bwd(du, a_ref, wg_ref, wu_ref, wo_ref, act_ref, da_ref):
        dact = _dot(du, wo_ref[...], NT)
        g = a_ref[0].astype(F32)
        up = a_ref[1].astype(F32)
        s = jax.nn.sigmoid(g)
        silu = g * s
        act_ref[...] = (silu * up).astype(BF16)
        dg = (dact * up * (s * (1.0 + g * (1.0 - s)))).astype(BF16)
        dup = (dact * silu).astype(BF16)
        da_ref[0] = dg
        da_ref[1] = dup
        return _dot(dg, wg_ref[...], NT) + _dot(dup, wu_ref[...], NT)

    def first(do_ref, u_ref, a_ref, vec_ref, wg_ref, wu_ref, wo_ref, du_ref, dh_ref, act_ref, da_ref, vg_ref):
        @pl.when(pl.program_id(0) == 0)
        def _():
            vg_ref[...] = jnp.zeros_like(vg_ref)

        du, dgate_rows, dgpost_rows = _postnorm_bwd(do_ref[...], u_ref[...], vec_ref, weight)
        vg_ref[2:3, :] += jnp.sum(dgate_rows, axis=0, keepdims=True)
        vg_ref[4:5, :] += jnp.sum(dgpost_rows, axis=0, keepdims=True)
        du = du.astype(BF16)
        du_ref[...] = du
        dh_ref[...] = hidden_bwd(du, a_ref, wg_ref, wu_ref, wo_ref, act_ref, da_ref)

    du, dh, act, da, vg_post = pl.pallas_call(
        first, name="ffn_bwd_first", grid=(S // tm,),
        in_specs=[pl.BlockSpec((tm, D), row), pl.BlockSpec((tm, D), row)] + half(0),
        out_specs=[pl.BlockSpec((tm, D), row), pl.BlockSpec((tm, D), row)] + half_out(0) + [_w3((8, D))],
        out_shape=[jax.ShapeDtypeStruct((S, D), BF16), jax.ShapeDtypeStruct((S, D), F32)] + half_shape
        + [jax.ShapeDtypeStruct((8, D), F32)],
        compiler_params=_params("arbitrary"),
    )(dout, u, a, vec, w_in, w_in, w_out)

    def second(do_ref, x_ref, du_ref, dh_ref, a_ref, vec_ref, wg_ref, wu_ref, wo_ref, act_in, da_in,
               dx_ref, act_ref, da_ref, vg_ref):
        @pl.when(pl.program_id(0) == 0)
        def _():
            vg_ref[...] = jnp.zeros_like(vg_ref)

        dh = dh_ref[...] + hidden_bwd(du_ref[...], a_ref, wg_ref, wu_ref, wo_ref, act_ref, da_ref)
        dx_ref[...] = do_ref[...] + _prenorm_bwd(dh, x_ref[...], vec_ref, vg_ref)

    dx, act, da, vg_pre = pl.pallas_call(
        second, name="ffn_bwd_second", grid=(S // tm,),
        in_specs=[pl.BlockSpec((tm, D), row), pl.BlockSpec((tm, D), row), pl.BlockSpec((tm, D), row),
                  pl.BlockSpec((tm, D), row)] + half(1) + [_ANY, _ANY],
        out_specs=[pl.BlockSpec((tm, D), row)] + half_out(1) + [_w3((8, D))],
        out_shape=[jax.ShapeDtypeStruct((S, D), F32)] + half_shape + [jax.ShapeDtypeStruct((8, D), F32)],
        input_output_aliases={9: 1, 10: 2},
        compiler_params=_params("arbitrary"),
    )(dout, x, du, dh, a, vec, w_in, w_in, w_out, act, da)
    return dx, du, act, da, vg_post + vg_pre


def dw_matmul(name, a, b, a_spec, b_spec, out_shape, out_spec, grid):
    def body(a_ref, b_ref, o_ref):
        @pl.when(pl.program_id(len(grid) - 1) == 0)
        def _():
            o_ref[...] = jnp.zeros_like(o_ref)

        o_ref[...] += _dot(a_ref[...], b_ref[...], TN)

    return pl.pallas_call(
        body, name=name, grid=grid, in_specs=[a_spec, b_spec], out_specs=out_spec,
        out_shape=jax.ShapeDtypeStruct(out_shape, F32),
        compiler_params=_params(*(["parallel"] * (len(grid) - 1) + ["arbitrary"])),
    )(a, b)


def ffn_dw(h, da, act, du):
    S = h.shape[0]
    tk = min(DW_TK, S)
    dw_in = dw_matmul("ffn_dw_in", h, da,
                      pl.BlockSpec((tk, D), lambda n, k: (k, 0)),
                      pl.BlockSpec((None, tk, FSH), lambda n, k: (n // 2, k, n % 2)),
                      (N_CHIP, D, FSH), pl.BlockSpec((None, D, FSH), lambda n, k: (n, 0, 0)),
                      (N_CHIP, S // tk))
    dw_out = dw_matmul("ffn_dw_out", act, du,
                       pl.BlockSpec((tk, FSH), lambda n, k: (k, n)),
                       pl.BlockSpec((tk, D), lambda n, k: (k, 0)),
                       (DFF, D), pl.BlockSpec((FSH, D), lambda n, k: (n, 0)),
                       (2, S // tk))
    return dw_in, dw_out


def _halo_specs(tm, S):
    nb = tm // HALO
    last = S // HALO - 1
    return [pl.BlockSpec((HALO, D), lambda i: (jnp.maximum(i * nb - 1, 0), 0)),
            pl.BlockSpec((tm, D), lambda i: (i, 0)),
            pl.BlockSpec((HALO, D), lambda i: (jnp.minimum((i + 1) * nb, last), 0))]


def _shift_rows(v, k):
    return pltpu.roll(v, k % v.shape[0], 0)


def _window_sum(v, g, forward):
    acc = v + _shift_rows(v, 1 if forward else -1)
    for step in (1, 2, 4)[:g]:
        acc = _shift_rows(acc, step) + _shift_rows(acc, -step)
    return acc


def _pool_count(t, w, S):
    return jnp.maximum(jnp.minimum(t + w // 2, S) - jnp.maximum(t - w // 2, 0), 1).astype(F32)


def pool_fwd(x, vec, pw, pvec):
    S = x.shape[0]
    tm = min(ROW_TILE, S)
    G = D // 4

    def body(xp_ref, x_ref, xn_ref, vec_ref, pw_ref, pv_ref, xo_ref, y_ref, z_ref):
        i = pl.program_id(0)
        xa = jnp.concatenate([xp_ref[...], x_ref[...], xn_ref[...]], axis=0)
        t = i * tm - HALO + lax.broadcasted_iota(jnp.int32, (tm + 2 * HALO, 1), 0)
        h, _, _ = _prenorm(xa, vec_ref)
        h = jnp.where((t >= 0) & (t < S), h, 0.0)
        tmain = t[HALO:HALO + tm]
        for g in range(4):
            hg = h[:, g * G:(g + 1) * G]
            pooled = _window_sum(hg, g, True)[HALO:HALO + tm] / _pool_count(tmain, POOL_WINDOWS[g], S)
            z = (pooled - hg[HALO:HALO + tm]).astype(BF16)
            z_ref[:, g * G:(g + 1) * G] = z
            y_ref[:, g * G:(g + 1) * G] = _dot(z, pw_ref[g]) + pv_ref[0:1, g * G:(g + 1) * G]
        u = y_ref[...] * pv_ref[1:2, :]
        uhat, _ = _rms(u)
        xo_ref[...] = x_ref[...] + (1.0 + vec_ref[4:5, :]) * (uhat * vec_ref[1:2, :])

    row = lambda i: (i, 0)
    full = lambda i: (0, 0)
    return pl.pallas_call(
        body, name="pool_fwd", grid=(S // tm,),
        in_specs=_halo_specs(tm, S) + [pl.BlockSpec((8, D), full), pl.BlockSpec((4, G, G), lambda i: (0, 0, 0)),
                                       pl.BlockSpec((8, D), full)],
        out_specs=[pl.BlockSpec((tm, D), row)] * 3,
        out_shape=[jax.ShapeDtypeStruct((S, D), F32), jax.ShapeDtypeStruct((S, D), F32),
                   jax.ShapeDtypeStruct((S, D), BF16)],
        compiler_params=_params("parallel"),
    )(x, x, x, vec, pw, pvec)


def pool_bwd(dout, x, y, z, vec, pw, pvec):
    S = x.shape[0]
    tm = min(ROW_TILE, S)
    G = D // 4
    R = G // N_CHIP

    def body(dop_ref, do_ref, don_ref, yp_ref, y_ref, yn_ref, x_ref, z_ref, vec_ref, pw_ref, pv_ref,
             dx_ref, vg_ref, pg_ref, dw_ref, dh_ref):
        i = pl.program_id(0)

        @pl.when(i == 0)
        def _():
            vg_ref[...] = jnp.zeros_like(vg_ref)
            pg_ref[...] = jnp.zeros_like(pg_ref)
            dw_ref[...] = jnp.zeros_like(dw_ref)

        doa = jnp.concatenate([dop_ref[...], do_ref[...], don_ref[...]], axis=0)
        ya = jnp.concatenate([yp_ref[...], y_ref[...], yn_ref[...]], axis=0)
        t = i * tm - HALO + lax.broadcasted_iota(jnp.int32, (tm + 2 * HALO, 1), 0)
        inside = (t >= 0) & (t < S)
        main = (t >= i * tm) & (t < (i + 1) * tm)
        du, dgate_rows, dgpost_rows = _postnorm_bwd(doa, ya * pv_ref[1:2, :], vec_ref, 1.0)
        du = jnp.where(inside, du, 0.0)
        vg_ref[2:3, :] += jnp.sum(jnp.where(main, dgate_rows, 0.0), axis=0, keepdims=True)
        vg_ref[4:5, :] += jnp.sum(jnp.where(main, dgpost_rows, 0.0), axis=0, keepdims=True)
        dy = du * pv_ref[1:2, :]
        pg_ref[0:1, :] += jnp.sum(jnp.where(main, dy, 0.0), axis=0, keepdims=True)
        pg_ref[1:2, :] += jnp.sum(jnp.where(main, du * ya, 0.0), axis=0, keepdims=True)
        for g in range(4):
            dyg = dy[:, g * G:(g + 1) * G].astype(BF16)
            dz = _dot(dyg, pw_ref[g], NT)
            e = dz / _pool_count(t, POOL_WINDOWS[g], S)
            dh_ref[:, g * G:(g + 1) * G] = (_window_sum(e, g, False) - dz)[HALO:HALO + tm]
            dwg = _dot(z_ref[:, g * G:(g + 1) * G], dyg[HALO:HALO + tm], TN)
            for q in range(N_CHIP):
                dw_ref[q, g] += dwg[q * R:(q + 1) * R, :]
        dx_ref[...] = do_ref[...] + _prenorm_bwd(dh_ref[...], x_ref[...], vec_ref, vg_ref)

    row = lambda i: (i, 0)
    full = lambda i: (0, 0)
    halo = _halo_specs(tm, S)
    return pl.pallas_call(
        body, name="pool_bwd", grid=(S // tm,),
        in_specs=halo + halo + [pl.BlockSpec((tm, D), row), pl.BlockSpec((tm, D), row), pl.BlockSpec((8, D), full),
                                pl.BlockSpec((4, G, G), lambda i: (0, 0, 0)), pl.BlockSpec((8, D), full)],
        out_specs=[pl.BlockSpec((tm, D), row), pl.BlockSpec((8, D), full), pl.BlockSpec((8, D), full),
                   pl.BlockSpec((N_CHIP, 4, R, G), lambda i: (0, 0, 0, 0))],
        out_shape=[jax.ShapeDtypeStruct((S, D), F32), jax.ShapeDtypeStruct((8, D), F32),
                   jax.ShapeDtypeStruct((8, D), F32), jax.ShapeDtypeStruct((N_CHIP, 4, R, G), F32)],
        scratch_shapes=[pltpu.VMEM((tm, D), F32)],
        compiler_params=_params("arbitrary"),
    )(dout, dout, dout, y, y, y, x, z, vec, pw, pvec)


N_PAIR = N_HEADS // 2
SLOTS = 128 // ROPE
ROPE_ALL = N_HEADS * ROPE
NOPE_ALL = N_HEADS * NOPE
LAT_ALL = N_HEADS * KVL
DLAT = QL + KVL + 2 * 128
DQ_ALL = NOPE_ALL + 2 * ROPE_ALL


def _w3(shape):
    return pl.BlockSpec(shape, lambda i: (0,) * len(shape))


def _slot_mask(hd, rows):
    lane = lax.broadcasted_iota(jnp.int32, (rows, 128), 1)
    return (lane // ROPE) == (hd % SLOTS)


MLA_WEIGHTS = ("wq", "wkv", "wkr4", "wkrs4", "qn", "kvn", "wn", "wr", "wrs", "bduk")


def _mla_weight_specs():
    return [_w3((D, QL)), _w3((D, KVL)), _w3((D, 128)), _w3((D, 128)), _w3((1, QL)), _w3((1, KVL)),
            _w3((QL, NOPE_ALL)), _w3((QL, ROPE_ALL)), _w3((QL, ROPE_ALL)), _w3((N_PAIR, 2 * NOPE, 2 * KVL))]


def mla_pre(x, vec, mw, tabs):
    S = x.shape[0]
    tm = min(ROW_TILE, S)

    def body(x_ref, vec_ref, cos_ref, sin_ref, wq_ref, wkv_ref, wkr_ref, wkrs_ref, qn_ref, kvn_ref,
             wn_ref, wr_ref, wrs_ref, bduk_ref,
             h_ref, cq_ref, ckv_ref, cqn_ref, qnope_ref, qcat_ref, kcat_ref, vcat_ref):
        h, _, _ = _prenorm(x_ref[...], vec_ref)
        hb = h.astype(BF16)
        h_ref[...] = hb
        cq_raw = _dot(hb, wq_ref[...])
        ckv_raw = _dot(hb, wkv_ref[...])
        cq_ref[...] = cq_raw
        ckv_ref[...] = ckv_raw
        cos, sin = cos_ref[...], sin_ref[...]
        ckv = (_rms(ckv_raw)[0] * kvn_ref[...]).astype(BF16)
        kcat_ref[:, 0:KVL] = ckv
        kcat_ref[:, KVL:] = (_dot(hb, wkr_ref[...]) * cos + _dot(hb, wkrs_ref[...]) * sin).astype(BF16)
        vcat_ref[:, 0:KVL] = ckv
        ones = lax.broadcasted_iota(jnp.int32, (tm, QPAD - KVL), 1) == 0
        vcat_ref[:, KVL:] = jnp.where(ones, 1.0, 0.0).astype(BF16)
        cqb = (_rms(cq_raw)[0] * qn_ref[...]).astype(BF16)
        cqn_ref[...] = cqb
        qn = _dot(cqb, wn_ref[...]).astype(BF16)
        qnope_ref[...] = qn
        cos4, sin4 = jnp.tile(cos, (1, SLOTS)), jnp.tile(sin, (1, SLOTS))
        qr = ((_dot(cqb, wr_ref[...]) * cos4 + _dot(cqb, wrs_ref[...]) * sin4) * ATTN_SCALE).astype(BF16)
        for j in range(N_PAIR):
            ql = (_dot(qn[:, 128 * j:128 * (j + 1)], bduk_ref[j]) * ATTN_SCALE).astype(BF16)
            for hd in (2 * j, 2 * j + 1):
                qcat_ref[hd, :, 0:KVL] = ql[:, KVL * (hd - 2 * j):KVL * (hd - 2 * j + 1)]
                group = qr[:, 128 * (hd // SLOTS):128 * (hd // SLOTS + 1)]
                qcat_ref[hd, :, KVL:] = jnp.where(_slot_mask(hd, tm), group, jnp.zeros_like(group))

    row = lambda i: (i, 0)
    hrow = lambda i: (0, i, 0)
    return pl.pallas_call(
        body, name="mla_pre", grid=(S // tm,),
        in_specs=[pl.BlockSpec((tm, D), row), _w3((8, D)), pl.BlockSpec((tm, 128), row), pl.BlockSpec((tm, 128), row)]
        + _mla_weight_specs(),
        out_specs=[pl.BlockSpec((tm, D), row), pl.BlockSpec((tm, QL), row), pl.BlockSpec((tm, KVL), row),
                   pl.BlockSpec((tm, QL), row), pl.BlockSpec((tm, NOPE_ALL), row),
                   pl.BlockSpec((N_HEADS, tm, QPAD), hrow), pl.BlockSpec((tm, QPAD), row),
                   pl.BlockSpec((tm, QPAD), row)],
        out_shape=[jax.ShapeDtypeStruct((S, D), BF16), jax.ShapeDtypeStruct((S, QL), F32),
                   jax.ShapeDtypeStruct((S, KVL), F32), jax.ShapeDtypeStruct((S, QL), BF16),
                   jax.ShapeDtypeStruct((S, NOPE_ALL), BF16), jax.ShapeDtypeStruct((N_HEADS, S, QPAD), BF16),
                   jax.ShapeDtypeStruct((S, QPAD), BF16), jax.ShapeDtypeStruct((S, QPAD), BF16)],
        compiler_params=_params("parallel"),
    )(x, vec, tabs[0], tabs[1], *[mw[k] for k in MLA_WEIGHTS])


def attn_fwd(qcat, kcat, vcat):
    S = kcat.shape[0]
    tq = min(ATTN_TQ, S)
    kc = min(ATTN_KC, S)

    def body(q_ref, k_ref, v_ref, o_ref, lse_ref):
        q = q_ref[...]
        m = jnp.full((tq, 1), -jnp.inf, F32)
        ov = jnp.zeros((tq, QPAD), F32)
        for c in range(S // kc):
            s = _dot(q, k_ref[c * kc:(c + 1) * kc, :], NT)
            m_new = jnp.maximum(m, jnp.max(s, axis=-1, keepdims=True))
            p = jnp.exp(s - m_new).astype(BF16)
            ov = ov * jnp.exp(m - m_new) + _dot(p, v_ref[c * kc:(c + 1) * kc, :])
            m = m_new
        l = ov[:, KVL:KVL + 1]
        o_ref[...] = (ov[:, 0:KVL] * (1.0 / l)).astype(BF16)
        lse_ref[...] = _as_row(m + jnp.log(l))

    return pl.pallas_call(
        body, name="attn_fwd", grid=(N_HEADS, S // tq),
        in_specs=[pl.BlockSpec((None, tq, QPAD), lambda h, i: (h, i, 0)),
                  pl.BlockSpec((S, QPAD), lambda h, i: (0, 0)),
                  pl.BlockSpec((S, QPAD), lambda h, i: (0, 0))],
        out_specs=[pl.BlockSpec((tq, KVL), lambda h, i: (i, h)),
                   pl.BlockSpec((None, 1, tq), lambda h, i: (h, 0, i))],
        out_shape=[jax.ShapeDtypeStruct((S, LAT_ALL), BF16), jax.ShapeDtypeStruct((N_HEADS, 1, S), F32)],
        compiler_params=_params("parallel", "parallel"),
    )(qcat, kcat, vcat)


def mla_post(olat, x, vec, bduv, wo):
    S = x.shape[0]
    tm = min(ROW_TILE, S)

    def body(o_ref, x_ref, vec_ref, bduv_ref, wo_ref, xo_ref, u_ref, ocat_ref):
        for j in range(N_PAIR):
            oc = _dot(o_ref[:, 2 * KVL * j:2 * KVL * (j + 1)], bduv_ref[j])
            ocat_ref[:, 2 * VH * j:2 * VH * (j + 1)] = oc.astype(BF16)
        u = _dot(ocat_ref[...], wo_ref[...])
        u_ref[...] = u
        uhat, _ = _rms(u)
        xo_ref[...] = x_ref[...] + (1.0 + vec_ref[4:5, :]) * (uhat * vec_ref[1:2, :])

    row = lambda i: (i, 0)
    return pl.pallas_call(
        body, name="mla_post", grid=(S // tm,),
        in_specs=[pl.BlockSpec((tm, LAT_ALL), row), pl.BlockSpec((tm, D), row), _w3((8, D)),
                  _w3((N_PAIR, 2 * KVL, 2 * VH)), _w3((D, D))],
        out_specs=[pl.BlockSpec((tm, D), row), pl.BlockSpec((tm, D), row), pl.BlockSpec((tm, D), row)],
        out_shape=[jax.ShapeDtypeStruct((S, D), F32), jax.ShapeDtypeStruct((S, D), F32),
                   jax.ShapeDtypeStruct((S, D), BF16)],
        compiler_params=_params("parallel"),
    )(olat, x, vec, bduv, wo)


def mla_post_bwd(dout, u, olat, vec, bduv, wo):
    S = u.shape[0]
    tm = min(ROW_TILE, S)

    def body(do_ref, u_ref, o_ref, vec_ref, bduv_ref, wo_ref, du_ref, docat_ref, dolat_ref, delta_ref, vg_ref):
        @pl.when(pl.program_id(0) == 0)
        def _():
            vg_ref[...] = jnp.zeros_like(vg_ref)

        du, dgate_rows, dgpost_rows = _postnorm_bwd(do_ref[...], u_ref[...], vec_ref, 1.0)
        vg_ref[2:3, :] += jnp.sum(dgate_rows, axis=0, keepdims=True)
        vg_ref[4:5, :] += jnp.sum(dgpost_rows, axis=0, keepdims=True)
        dub = du.astype(BF16)
        du_ref[...] = dub
        docat_ref[...] = _dot(dub, wo_ref[...], NT).astype(BF16)
        for j in range(N_PAIR):
            dol = _dot(docat_ref[:, 2 * VH * j:2 * VH * (j + 1)], bduv_ref[j], NT).astype(BF16)
            dolat_ref[:, 2 * KVL * j:2 * KVL * (j + 1)] = dol
            prod = dol.astype(F32) * o_ref[:, 2 * KVL * j:2 * KVL * (j + 1)].astype(F32)
            delta_ref[2 * j] = _as_row(jnp.sum(prod[:, 0:KVL], axis=-1, keepdims=True))
            delta_ref[2 * j + 1] = _as_row(jnp.sum(prod[:, KVL:], axis=-1, keepdims=True))

    row = lambda i: (i, 0)
    hrow = lambda i: (0, i, 0)
    return pl.pallas_call(
        body, name="mla_post_bwd", grid=(S // tm,),
        in_specs=[pl.BlockSpec((tm, D), row), pl.BlockSpec((tm, D), row), pl.BlockSpec((tm, LAT_ALL), row),
                  _w3((8, D)), _w3((N_PAIR, 2 * KVL, 2 * VH)), _w3((D, D))],
        out_specs=[pl.BlockSpec((tm, D), row), pl.BlockSpec((tm, D), row),
                   pl.BlockSpec((tm, LAT_ALL), row), pl.BlockSpec((N_HEADS, 1, tm), lambda i: (0, 0, i)), _w3((8, D))],
        out_shape=[jax.ShapeDtypeStruct((S, D), BF16), jax.ShapeDtypeStruct((S, D), BF16),
                   jax.ShapeDtypeStruct((S, LAT_ALL), BF16), jax.ShapeDtypeStruct((N_HEADS, 1, S), F32),
                   jax.ShapeDtypeStruct((8, D), F32)],
        compiler_params=_params("arbitrary"),
    )(dout, u, olat, vec, bduv, wo)


def attn_bwd(qcat, kcat, kcat_t, dolat, lse_row, delta_row):
    S = kcat.shape[0]
    tq = min(ATTN_TQ, S)
    kc = min(ATTN_KC, S)

    def body(q_ref, k_ref, kt_ref, do_ref, lse_ref, dl_ref, dq_ref, dk_ref, dv_ref):
        @pl.when((pl.program_id(0) == 0) & (pl.program_id(1) == 0))
        def _():
            dk_ref[...] = jnp.zeros_like(dk_ref)
            dv_ref[...] = jnp.zeros_like(dv_ref)

        q, do = q_ref[...], do_ref[...]
        lse, dl = lse_ref[...], dl_ref[...]
        dqt = jnp.zeros((QPAD, tq), F32)
        for c in range(S // kc):
            rows = slice(c * kc, (c + 1) * kc)
            st = _dot(k_ref[rows, :], q, NT)
            pt = jnp.exp(st - lse)
            dpt = _dot(k_ref[rows, 0:KVL], do, NT)
            dst = (pt * (dpt - dl)).astype(BF16)
            dv_ref[rows, :] += _dot(pt.astype(BF16), do)
            dk_ref[rows, :] += _dot(dst, q)
            dqt = dqt + _dot(kt_ref[:, rows], dst)
        dq_ref[...] = (dqt.T * ATTN_SCALE).astype(BF16)

    return pl.pallas_call(
        body, name="attn_bwd", grid=(N_HEADS, S // tq),
        in_specs=[pl.BlockSpec((None, tq, QPAD), lambda h, i: (h, i, 0)),
                  pl.BlockSpec((S, QPAD), lambda h, i: (0, 0)),
                  pl.BlockSpec((QPAD, S), lambda h, i: (0, 0)),
                  pl.BlockSpec((tq, KVL), lambda h, i: (i, h)),
                  pl.BlockSpec((None, 1, tq), lambda h, i: (h, 0, i)),
                  pl.BlockSpec((None, 1, tq), lambda h, i: (h, 0, i))],
        out_specs=[pl.BlockSpec((None, tq, QPAD), lambda h, i: (h, i, 0)),
                   pl.BlockSpec((S, QPAD), lambda h, i: (0, 0)),
                   pl.BlockSpec((S, KVL), lambda h, i: (0, 0))],
        out_shape=[jax.ShapeDtypeStruct((N_HEADS, S, QPAD), BF16), jax.ShapeDtypeStruct((S, QPAD), F32),
                   jax.ShapeDtypeStruct((S, KVL), F32)],
        compiler_params=_params("arbitrary", "arbitrary"),
    )(qcat, kcat, kcat_t, dolat, lse_row, delta_row)


def mla_pre_bwd(dout, dq, dk, dv, x, cq_raw, ckv_raw, vec, mw, tabs):
    S = x.shape[0]
    tm = min(ROW_TILE, S)

    def body(do_ref, dq_ref, dk_ref, dv_ref, x_ref, cq_ref, ckv_ref, vec_ref, cos_ref, sin_ref,
             wq_ref, wkv_ref, wkr_ref, wkrs_ref, qn_ref, kvn_ref, wn_ref, wr_ref, wrs_ref, bduk_ref,
             dx_ref, dlat_ref, dql_ref, dqcat_ref, vg_ref, ng_ref):
        @pl.when(pl.program_id(0) == 0)
        def _():
            vg_ref[...] = jnp.zeros_like(vg_ref)
            ng_ref[...] = jnp.zeros_like(ng_ref)

        cos, sin = cos_ref[...], sin_ref[...]
        for j in range(N_PAIR):
            dql = jnp.concatenate([dq_ref[2 * j, :, 0:KVL], dq_ref[2 * j + 1, :, 0:KVL]], axis=1)
            dql_ref[:, 2 * KVL * j:2 * KVL * (j + 1)] = dql
            dqcat_ref[:, 2 * NOPE * j:2 * NOPE * (j + 1)] = _dot(dql, bduk_ref[j], NT).astype(BF16)
        groups = []
        for grp in range(N_HEADS // SLOTS):
            acc = jnp.zeros((tm, 128), F32)
            for hd in range(SLOTS * grp, SLOTS * (grp + 1)):
                acc = acc + jnp.where(_slot_mask(hd, tm), dq_ref[hd, :, KVL:].astype(F32), 0.0)
            groups.append(acc)
        dqr = jnp.concatenate(groups, axis=1)
        qa = (dqr * jnp.tile(cos, (1, SLOTS))).astype(BF16)
        qb = (dqr * jnp.tile(sin, (1, SLOTS))).astype(BF16)
        dqcat_ref[:, NOPE_ALL:NOPE_ALL + ROPE_ALL] = qa
        dqcat_ref[:, NOPE_ALL + ROPE_ALL:] = qb
        dcq = _dot(dqcat_ref[:, 0:NOPE_ALL], wn_ref[...], NT) + _dot(qa, wr_ref[...], NT) + _dot(qb, wrs_ref[...], NT)
        cqh, rq = _rms(cq_ref[...])
        ng_ref[0:1, :] += jnp.sum(dcq * cqh, axis=0, keepdims=True)
        dcq_raw = _rms_bwd(cqh, rq, dcq * qn_ref[...]).astype(BF16)
        dckv = dk_ref[:, 0:KVL] + dv_ref[...]
        ckvh, rk = _rms(ckv_ref[...])
        ng_ref[1:2, 0:KVL] += jnp.sum(dckv * ckvh, axis=0, keepdims=True)
        dckv_raw = _rms_bwd(ckvh, rk, dckv * kvn_ref[...]).astype(BF16)
        dkr = dk_ref[:, KVL:]
        ka = (dkr * cos).astype(BF16)
        kb = (dkr * sin).astype(BF16)
        dlat_ref[:, 0:QL] = dcq_raw
        dlat_ref[:, QL:QL + KVL] = dckv_raw
        dlat_ref[:, QL + KVL:QL + KVL + 128] = ka
        dlat_ref[:, QL + KVL + 128:] = kb
        dh = (_dot(dcq_raw, wq_ref[...], NT) + _dot(dckv_raw, wkv_ref[...], NT)
              + _dot(ka, wkr_ref[...], NT) + _dot(kb, wkrs_ref[...], NT))
        dx_ref[...] = do_ref[...] + _prenorm_bwd(dh, x_ref[...], vec_ref, vg_ref)

    row = lambda i: (i, 0)
    hrow = lambda i: (0, i, 0)
    return pl.pallas_call(
        body, name="mla_pre_bwd", grid=(S // tm,),
        in_specs=[pl.BlockSpec((tm, D), row), pl.BlockSpec((N_HEADS, tm, QPAD), hrow), pl.BlockSpec((tm, QPAD), row),
                  pl.BlockSpec((tm, KVL), row), pl.BlockSpec((tm, D), row), pl.BlockSpec((tm, QL), row),
                  pl.BlockSpec((tm, KVL), row), _w3((8, D)), pl.BlockSpec((tm, 128), row), pl.BlockSpec((tm, 128), row)]
        + _mla_weight_specs(),
        out_specs=[pl.BlockSpec((tm, D), row), pl.BlockSpec((tm, DLAT), row), pl.BlockSpec((tm, LAT_ALL), row),
                   pl.BlockSpec((tm, DQ_ALL), row), _w3((8, D)), _w3((8, QL))],
        out_shape=[jax.ShapeDtypeStruct((S, D), F32), jax.ShapeDtypeStruct((S, DLAT), BF16),
                   jax.ShapeDtypeStruct((S, LAT_ALL), BF16), jax.ShapeDtypeStruct((S, DQ_ALL), BF16),
                   jax.ShapeDtypeStruct((8, D), F32), jax.ShapeDtypeStruct((8, QL), F32)],
        compiler_params=_params("arbitrary"),
    )(dout, dq, dk, dv, x, cq_raw, ckv_raw, vec, tabs[0], tabs[1], *[mw[k] for k in MLA_WEIGHTS])


def mla_dw(h, dlat, cqn, dqcat, dql, qnope, olat, docat, ocat, du):
    S = h.shape[0]
    tk = min(DW_TK, S)
    nk = S // tk
    flat = lambda w: pl.BlockSpec((tk, w), lambda k: (k, 0))
    cols = lambda w: pl.BlockSpec((tk, w), lambda n, k: (k, n))
    pair_o = pl.BlockSpec((None, 2 * KVL, 128), lambda n, k: (n, 0, 0))
    g = {}
    g["in"] = dw_matmul("mla_dw_in", h, dlat, flat(D), flat(DLAT), (D, DLAT),
                        pl.BlockSpec((D, DLAT), lambda k: (0, 0)), (nk,))
    g["q"] = dw_matmul("mla_dw_q", cqn, dqcat, flat(QL), flat(DQ_ALL), (QL, DQ_ALL),
                       pl.BlockSpec((QL, DQ_ALL), lambda k: (0, 0)), (nk,))
    g["uk"] = dw_matmul("mla_dw_uk", dql, qnope, cols(2 * KVL), cols(2 * NOPE), (N_PAIR, 2 * KVL, 2 * NOPE), pair_o,
                        (N_PAIR, nk))
    g["uv"] = dw_matmul("mla_dw_uv", olat, docat, cols(2 * KVL), cols(2 * VH), (N_PAIR, 2 * KVL, 2 * VH), pair_o,
                        (N_PAIR, nk))
    g["o"] = dw_matmul("mla_dw_o", ocat, du, cols(256), pl.BlockSpec((tk, D), lambda n, k: (k, 0)), (D, D),
                       pl.BlockSpec((256, D), lambda n, k: (n, 0)), (D // 256, nk))
    return g


def loss_head(y, target):
    S = y.shape[0]
    tm = min(512, S)

    def body(y_ref, t_ref, loss_ref, dy_ref):
        @pl.when(pl.program_id(0) == 0)
        def _():
            loss_ref[...] = jnp.zeros_like(loss_ref)

        err = y_ref[...] - t_ref[...]
        dy_ref[...] = err * (1.0 / D)
        loss_ref[...] += 0.5 * jnp.sum(jnp.mean(err * err, axis=-1, keepdims=True), axis=0, keepdims=True)

    row = lambda i: (i, 0)
    return pl.pallas_call(
        body, name="loss_head", grid=(S // tm,),
        in_specs=[pl.BlockSpec((tm, D), row), pl.BlockSpec((tm, D), row)],
        out_specs=[pl.BlockSpec((1, 1), lambda i: (0, 0)), pl.BlockSpec((tm, D), row)],
        out_shape=[jax.ShapeDtypeStruct((1, 1), F32), jax.ShapeDtypeStruct((S, D), F32)],
        compiler_params=_params("arbitrary"),
    )(y, target)


MOD_COLS = 9 * D // N_CHIP


def mod_fwd(c_pad, ada_w, ada_b_loc):
    tn = MOD_COLS // 3

    def body(c_ref, w_ref, b_ref, o_ref):
        c = c_ref[...]
        sc = (c * jax.nn.sigmoid(c)).astype(BF16)
        o_ref[...] = _dot(sc, w_ref[...].astype(BF16)) + b_ref[...]

    return pl.pallas_call(
        body, name="mod_fwd", grid=(2, 3),
        in_specs=[pl.BlockSpec((16, D), lambda i, n: (0, 0)), pl.BlockSpec((None, D, tn), lambda i, n: (i, 0, n)),
                  pl.BlockSpec((None, 1, tn), lambda i, n: (i, 0, n))],
        out_specs=pl.BlockSpec((None, 16, tn), lambda i, n: (i, 0, n)),
        out_shape=jax.ShapeDtypeStruct((2, 16, MOD_COLS), F32),
        compiler_params=_params("parallel", "parallel"),
    )(c_pad, ada_w, ada_b_loc)


def _adamw_math(w, g, m, v):
    m = ADAM_B1 * m + (1.0 - ADAM_B1) * g
    v = ADAM_B2 * v + (1.0 - ADAM_B2) * (g * g)
    m_hat = m / (1.0 - ADAM_B1 ** ADAM_STEP)
    v_hat = v / (1.0 - ADAM_B2 ** ADAM_STEP)
    delta = -ADAM_LR * (m_hat / (jnp.sqrt(v_hat) + ADAM_EPS) + ADAM_WD * w)
    return delta, m, v


def adamw(name, w, g, m, v, part=None, prev=None, copy_grad=False):
    shape = w.shape
    if part is None and w.size * 4 <= (1 << 20):
        whole = pl.BlockSpec(shape, lambda i: (0,) * len(shape))

        def small_body(w_ref, g_ref, m_ref, v_ref, d_ref, mo_ref, vo_ref):
            d_ref[...], mo_ref[...], vo_ref[...] = _adamw_math(w_ref[...], g_ref[...], m_ref[...], v_ref[...])

        return pl.pallas_call(
            small_body, name=name, grid=(1,), in_specs=[whole] * 4, out_specs=[whole] * 3,
            out_shape=[jax.ShapeDtypeStruct(shape, F32)] * 3, compiler_params=_params("arbitrary"),
        )(w, g, m, v)
    cols = shape[-1]
    rows = w.size // cols
    per_entry = rows // shape[0] if part is not None else rows
    tr = per_entry
    budget_rows = (2 << 20) // (cols * 4)
    for cand in range(min(per_entry, budget_rows) // 8 * 8, 0, -8):
        if per_entry % cand == 0:
            tr = cand
            break
    first, count = part if part is not None else (0, 1)
    tiles = per_entry // tr

    n_out = 4 if copy_grad else 3

    def body(w_ref, g_ref, m_ref, v_ref, *rest):
        outs = rest[-n_out:]
        outs[0][...], outs[1][...], outs[2][...] = _adamw_math(w_ref[...], g_ref[...], m_ref[...], v_ref[...])
        if copy_grad:
            outs[3][...] = g_ref[...]

    spec = pl.BlockSpec((tr, cols), lambda i: (i + first * tiles, 0))
    operands = [a.reshape(rows, cols) for a in (w, g, m, v)]
    aliases = {}
    if prev is not None:
        operands += [p.reshape(rows, cols) for p in prev]
        aliases = {4 + t: t for t in range(n_out)}
    outs = pl.pallas_call(
        body, name=name, grid=(count * tiles,), in_specs=[spec] * 4 + [_ANY] * (len(operands) - 4),
        out_specs=[spec] * n_out, out_shape=[jax.ShapeDtypeStruct((rows, cols), F32)] * n_out,
        input_output_aliases=aliases, compiler_params=_params("parallel"),
    )(*operands)
    return [o.reshape(shape) for o in outs]


def adamw_ada(c_pad, dmod, w, m, v):
    tr = 256

    def body(c_ref, dm_ref, w_ref, m_ref, v_ref, g_ref, d_ref, mo_ref, vo_ref):
        c = c_ref[...]
        sc = (c * jax.nn.sigmoid(c)).astype(BF16)
        g = _dot(sc, dm_ref[...].astype(BF16), TN)
        g_ref[...] = g
        d_ref[...], mo_ref[...], vo_ref[...] = _adamw_math(w_ref[...], g, m_ref[...], v_ref[...])

    wspec = pl.BlockSpec((None, tr, MOD_COLS), lambda i, r: (i, r, 0))
    return pl.pallas_call(
        body, name="adamw_ada", grid=(2, D // tr),
        in_specs=[pl.BlockSpec((16, tr), lambda i, r: (0, r)),
                  pl.BlockSpec((None, 16, MOD_COLS), lambda i, r: (i, 0, 0)), wspec, wspec, wspec],
        out_specs=[wspec] * 4,
        out_shape=[jax.ShapeDtypeStruct((2, D, MOD_COLS), F32)] * 4,
        compiler_params=_params("parallel", "parallel"),
    )(c_pad, dmod, w, m, v)


def sum_devices(name, a):
    _, R, C = a.shape
    tr = R
    for cand in (64, 32, 16, 8):
        if R % cand == 0:
            tr = cand
            break

    def body(a_ref, o_ref):
        acc = a_ref[0]
        for dev in range(1, N_DEV):
            acc = acc + a_ref[dev]
        o_ref[...] = acc

    return pl.pallas_call(
        body, name=name, grid=(R // tr,),
        in_specs=[pl.BlockSpec((N_DEV, tr, C), lambda i: (0, i, 0))],
        out_specs=pl.BlockSpec((tr, C), lambda i: (i, 0)),
        out_shape=jax.ShapeDtypeStruct((R, C), F32),
        compiler_params=_params("parallel"),
    )(a)


def _place():
    return lax.axis_index("x"), lax.axis_index("y"), lax.axis_index("c")


def _other_chips(x, y):
    return [(1 - x, y), (x, 1 - y), (1 - x, 1 - y)]


def gather_devices(name, a):
    m_per, n = a.shape

    def body(x_ref, out_ref, send_sems, recv_sems, local_sem):
        x, y, c = _place()
        me, sibling = (x, y, c), (x, y, 1 - c)
        chips = _other_chips(x, y)

        def rows(px, py, pc):
            return out_ref.at[pl.ds((4 * px + 2 * py + pc) * m_per, m_per), :]

        def copy(k, block, to, src=None):
            return pltpu.make_async_remote_copy(
                src_ref=rows(*block) if src is None else src, dst_ref=rows(*block),
                send_sem=send_sems.at[k], recv_sem=recv_sems.at[k], device_id=to, device_id_type=MESH)

        mine = pltpu.make_async_copy(x_ref, rows(*me), local_sem)
        mine.start()
        first = [copy(0, me, sibling, src=x_ref)]
        first += [copy(1 + j, me, (*chip, c), src=x_ref) for j, chip in enumerate(chips)]
        for cp in first:
            cp.start()
        passed = [copy(4 + j, (*chip, c), sibling) for j, chip in enumerate(chips)]
        for j, chip in enumerate(chips):
            copy(1 + j, (*chip, c), me).wait_recv()
            passed[j].start()
        copy(0, sibling, me).wait_recv()
        for j, chip in enumerate(chips):
            copy(4 + j, (*chip, 1 - c), me).wait_recv()
        for cp in first + passed:
            cp.wait_send()
        mine.wait()

    out = pl.pallas_call(
        body, name=name,
        out_shape=jax.ShapeDtypeStruct((N_DEV * m_per, n), a.dtype),
        in_specs=[pl.BlockSpec(memory_space=pltpu.VMEM)],
        out_specs=pl.BlockSpec(memory_space=pltpu.VMEM),
        scratch_shapes=[pltpu.SemaphoreType.DMA((7,)), pltpu.SemaphoreType.DMA((7,)), pltpu.SemaphoreType.DMA],
        compiler_params=pltpu.CompilerParams(vmem_limit_bytes=VMEM_LIMIT),
    )(a)
    return out.reshape(N_DEV, m_per, n)


_ANY = pl.BlockSpec(memory_space=pl.ANY)


def _hbm_ref(a):
    return jax.new_ref(a, memory_space=pltpu.MemorySpace.HBM)


def _hbm_empty(shape, dtype):
    return jax.empty_ref(jax.ShapeDtypeStruct(shape, dtype), memory_space=pltpu.MemorySpace.HBM)


ID_PAIR, ID_CHIPS, ID_SHARE, ID_UKV = 8, 9, 10, 11


def _sequencer(name, collective_id, n_sem, peers_of, program):
    sems = pltpu.SemaphoreType.DMA((n_sem,))

    @pl.kernel(mesh=plsc.ScalarSubcoreMesh(axis_name="seq", num_cores=1), name=name, scratch_types=[sems, sems],
               compiler_params=pltpu.CompilerParams(collective_id=collective_id))
    def launch(send_sem, recv_sem):
        x, y, c = _place()
        peers = peers_of(x, y, c)
        barrier = pltpu.get_barrier_semaphore()
        for peer in peers:
            pl.semaphore_signal(barrier, inc=1, device_id=peer, device_id_type=MESH)
        pl.semaphore_wait(barrier, len(peers))
        program(x, y, c, send_sem, recv_sem)

    launch()


def gather_weights(name, stage, arrays):
    n = len(arrays)
    refs = [_hbm_ref(a) for a in arrays]

    def program(x, y, c, send_sem, recv_sem):
        me = 2 * x + y
        chips = _other_chips(x, y)

        def ici(t, r, half):
            cx, cy = chips[r]
            mine = refs[t].at[me, half]
            return pltpu.make_async_remote_copy(
                src_ref=mine, dst_ref=mine, send_sem=send_sem.at[3 * t + r], recv_sem=recv_sem.at[3 * t + r],
                device_id=(cx, cy, c), device_id_type=MESH)

        def d2d(t, r, half):
            cx, cy = chips[r]
            there = refs[t].at[2 * cx + cy, half]
            k = 3 * n + 3 * t + r
            return pltpu.make_async_remote_copy(
                src_ref=there, dst_ref=there, send_sem=send_sem.at[k], recv_sem=recv_sem.at[k],
                device_id=(x, y, 1 - c), device_id_type=MESH)

        for t in range(n):
            for r in range(3):
                ici(t, r, c).start()
        for t in range(n):
            for r in range(3):
                ici(t, r, c).wait_recv()
                d2d(t, r, c).start()
        for t in range(n):
            for r in range(3):
                d2d(t, r, 1 - c).wait_recv()
        for t in range(n):
            for r in range(3):
                ici(t, r, c).wait_send()
                d2d(t, r, c).wait_send()

    _sequencer(name, stage, 6 * n, lambda x, y, c: [(x, y, 1 - c)] + [(cx, cy, c) for cx, cy in _other_chips(x, y)],
               program)
    return [r[...] for r in refs]


def cast_into_slots(name, chip, shards, after=None):
    steps = 2
    n = len(shards)

    def body(chip_ref, *refs):
        for src, dst in zip(refs[:n], refs[-n - 1:-1]):
            dst[...] = src[...].astype(BF16)
        refs[-1][...] = jnp.zeros_like(refs[-1])

    token_spec = pl.BlockSpec((8, 128), lambda h, i, chip_ref: (0, 0))

    def spec_in(a, prefix):
        R, C = a.shape[-2:]
        return pl.BlockSpec((None,) * (len(prefix) + 1) + (R // steps, C), lambda h, i, chip_ref: prefix + (h, i, 0))

    def spec_out(a):
        R, C = a.shape[-2:]
        return pl.BlockSpec((None, None, R // steps, C), lambda h, i, chip_ref: (chip_ref[0], h, i, 0))

    outs = pl.pallas_call(
        body, name=name,
        grid_spec=pltpu.PrefetchScalarGridSpec(
            num_scalar_prefetch=1, grid=(2, steps),
            in_specs=[spec_in(a, p) for a, p in shards] + ([token_spec] if after is not None else []),
            out_specs=[spec_out(a) for a, _ in shards] + [token_spec]),
        out_shape=[jax.ShapeDtypeStruct((N_CHIP, 2) + a.shape[-2:], BF16) for a, _ in shards]
        + [jax.ShapeDtypeStruct((8, 128), F32)],
        compiler_params=_params("arbitrary", "arbitrary"),
    )(chip, *[a for a, _ in shards], *([after] if after is not None else []))
    return outs[:-1], outs[-1]


def reduce_pair(name, grads):
    n = len(grads)
    src = [_hbm_ref(g) for g in grads]
    dst = [_hbm_empty((N_CHIP,) + g.shape[2:], g.dtype) for g in grads]

    def program(x, y, c, send_sem, recv_sem):
        cps = [pltpu.make_async_remote_copy(
            src_ref=src[t].at[:, 1 - c], dst_ref=dst[t], send_sem=send_sem.at[t], recv_sem=recv_sem.at[t],
            device_id=(x, y, 1 - c), device_id_type=MESH) for t in range(n)]
        for cp in cps:
            cp.start()
        for cp in cps:
            cp.wait()

    _sequencer(name, ID_PAIR, n, lambda x, y, c: [(x, y, 1 - c)], program)
    return [r[...] for r in src], [r[...] for r in dst]


def pair_add(name, core, g, got):
    _, _, R, C = g.shape

    def body(core_ref, g_ref, got_ref, o_ref):
        o_ref[...] = (g_ref[...] + got_ref[...]).astype(BF16)

    return pl.pallas_call(
        body, name=name,
        grid_spec=pltpu.PrefetchScalarGridSpec(
            num_scalar_prefetch=1, grid=(N_CHIP,),
            in_specs=[pl.BlockSpec((None, None, R, C), lambda q, core_ref: (q, core_ref[0], 0, 0)),
                      pl.BlockSpec((None, R, C), lambda q, core_ref: (q, 0, 0))],
            out_specs=pl.BlockSpec((None, R, C), lambda q, core_ref: (q, 0, 0))),
        out_shape=jax.ShapeDtypeStruct((N_CHIP, R, C), BF16),
        compiler_params=_params("parallel"),
    )(core, g, got)


def reduce_chips(name, sums):
    n = len(sums)
    src = [_hbm_ref(s) for s in sums]
    dst = [_hbm_empty((3,) + s.shape[1:], s.dtype) for s in sums]

    def program(x, y, c, send_sem, recv_sem):
        cps = []
        for t in range(n):
            for r, (cx, cy) in enumerate(_other_chips(x, y)):
                cps.append(pltpu.make_async_remote_copy(
                    src_ref=src[t].at[2 * cx + cy], dst_ref=dst[t].at[r],
                    send_sem=send_sem.at[3 * t + r], recv_sem=recv_sem.at[3 * t + r],
                    device_id=(cx, cy, c), device_id_type=MESH))
        for cp in cps:
            cp.start()
        for cp in cps:
            cp.wait()

    _sequencer(name, ID_CHIPS, 3 * n, lambda x, y, c: [(cx, cy, c) for cx, cy in _other_chips(x, y)], program)
    return [r[...] for r in src], [r[...] for r in dst]


def chip_add(name, place, s, got, k, n_slots, prev=None, after=None):
    _, R, C = s.shape

    def body(place_ref, s_ref, got_ref, *rest):
        o_ref = rest[-1]
        o_ref[...] = ((s_ref[...].astype(F32) + got_ref[0].astype(F32)) + got_ref[1].astype(F32)) + got_ref[2].astype(F32)

    in_specs = [pl.BlockSpec((None, R, C), lambda i, place_ref: (place_ref[0], 0, 0)),
                pl.BlockSpec((3, R, C), lambda i, place_ref: (0, 0, 0))]
    args = [place, s, got]
    aliases = {}
    if prev is not None:
        in_specs.append(_ANY)
        args.append(prev)
        aliases = {3: 0}
    for piece in after or ():
        in_specs.append(pl.BlockSpec((8, 128), lambda i, place_ref: (0, 0)))
        args.append(piece)
    return pl.pallas_call(
        body, name=name,
        grid_spec=pltpu.PrefetchScalarGridSpec(
            num_scalar_prefetch=1, grid=(1,), in_specs=in_specs,
            out_specs=pl.BlockSpec((None, None, R, C), lambda i, place_ref: (k, place_ref[1], 0, 0))),
        out_shape=jax.ShapeDtypeStruct((n_slots, 2, R, C), F32),
        input_output_aliases=aliases,
        compiler_params=_params("arbitrary"),
    )(*args)


def share_halves(name, stacks, slots):
    n = len(stacks)
    dst = [_hbm_ref(s) for s in stacks]

    def program(x, y, c, send_sem, recv_sem):
        cps = [pltpu.make_async_remote_copy(
            src_ref=dst[t].at[slots[t], c], dst_ref=dst[t].at[slots[t], c],
            send_sem=send_sem.at[t], recv_sem=recv_sem.at[t],
            device_id=(x, y, 1 - c), device_id_type=MESH) for t in range(n)]
        for cp in cps:
            cp.start()
        for cp in cps:
            cp.wait()

    _sequencer(name, ID_SHARE, n, lambda x, y, c: [(x, y, 1 - c)], program)
    return [r[...] for r in dst]


def gather_blocks(name, slotted):
    out = _hbm_ref(slotted)

    def program(x, y, c, send_sem, recv_sem):
        sibling = (x, y, 1 - c)
        chips = _other_chips(x, y)

        def copy(k, px, py, pc, to):
            block = out.at[4 * px + 2 * py + pc]
            return pltpu.make_async_remote_copy(src_ref=block, dst_ref=block, send_sem=send_sem.at[k],
                                                recv_sem=recv_sem.at[k], device_id=to, device_id_type=MESH)

        first = [copy(0, x, y, c, sibling)] + [copy(1 + j, x, y, c, (cx, cy, c)) for j, (cx, cy) in enumerate(chips)]
        for cp in first:
            cp.start()
        passed = [copy(4 + j, cx, cy, c, sibling) for j, (cx, cy) in enumerate(chips)]
        for j, (cx, cy) in enumerate(chips):
            copy(1 + j, cx, cy, c, (x, y, c)).wait_recv()
            passed[j].start()
        copy(0, x, y, 1 - c, (x, y, c)).wait_recv()
        for j, (cx, cy) in enumerate(chips):
            copy(4 + j, cx, cy, 1 - c, (x, y, c)).wait_recv()
        for cp in first + passed:
            cp.wait_send()

    _sequencer(name, ID_UKV, 7, lambda x, y, c: [(x, y, 1 - c)] + [(cx, cy, c) for cx, cy in _other_chips(x, y)],
               program)
    return out[...]


def place_block(name, dev, a):
    M, N = a.shape
    tr = min(M, 64)

    def body(dev_ref, a_ref, o_ref):
        o_ref[...] = a_ref[...]

    return pl.pallas_call(
        body, name=name,
        grid_spec=pltpu.PrefetchScalarGridSpec(
            num_scalar_prefetch=1, grid=(M // tr,),
            in_specs=[pl.BlockSpec((tr, N), lambda i, dev_ref: (i, 0))],
            out_specs=pl.BlockSpec((None, tr, N), lambda i, dev_ref: (dev_ref[0], i, 0))),
        out_shape=jax.ShapeDtypeStruct((N_DEV, M, N), a.dtype),
        compiler_params=_params("parallel"),
    )(dev, a)


def _swap_rope(a):
    return jnp.concatenate([a[..., ROPE // 2:], a[..., :ROPE // 2]], axis=-1)


def _rope_tables(S):
    inv = 1.0 / (ROPE_THETA ** (jnp.arange(0, ROPE, 2, dtype=F32) / ROPE))
    ang = jnp.arange(S, dtype=F32)[:, None] * inv[None, :]
    cos, sin = jnp.cos(ang), jnp.sin(ang)
    return (jnp.tile(jnp.concatenate([cos, cos], axis=1), (1, SLOTS)),
            jnp.tile(jnp.concatenate([-sin, sin], axis=1), (1, SLOTS)))


def _vec(norm_g, mod, i, k):
    rows = [norm_g[i, 2 * k], norm_g[i, 2 * k + 1], mod[i, 3 * k], mod[i, 3 * k + 1], mod[i, 3 * k + 2]]
    return jnp.concatenate([jnp.stack(rows), jnp.zeros((3, D), F32)], axis=0)


def _unpack_weights(full, w_uk, w_uv, q_norm, kv_norm):
    G = D // 4
    ffn_in = [[full[2 * i + k].reshape(N_CHIP, D, FSH) for k in range(2)] for i in range(2)]
    ffn_out = [[full[4 + 2 * i + k].reshape(2, FSH, D) for k in range(2)] for i in range(2)]
    pw = full[8].reshape(N_CHIP, 4, G // N_CHIP, G).transpose(1, 0, 2, 3).reshape(4, G, G)
    w_in = full[9].reshape(D, QL + KVL + ROPE)
    w_uq = full[10].reshape(QL, N_HEADS, NOPE + ROPE)
    wkr = w_in[:, QL + KVL:]
    wr = w_uq[:, :, NOPE:]
    eye2 = jnp.eye(2, dtype=BF16)
    uk_t = jnp.transpose(w_uk, (1, 2, 0)).reshape(N_PAIR, 2, NOPE, KVL)
    bduk = jnp.einsum("janc,ab->janbc", uk_t, eye2).reshape(N_PAIR, 2 * NOPE, 2 * KVL)
    uv = jnp.transpose(w_uv, (1, 0, 2)).reshape(N_PAIR, 2, KVL, VH)
    bduv = jnp.einsum("jacn,ab->jacbn", uv, eye2).reshape(N_PAIR, 2 * KVL, 2 * VH)
    mw = dict(wq=w_in[:, :QL], wkv=w_in[:, QL:QL + KVL], wkr4=jnp.tile(wkr, (1, SLOTS)),
              wkrs4=jnp.tile(_swap_rope(wkr), (1, SLOTS)), qn=q_norm, kvn=kv_norm,
              wn=w_uq[:, :, :NOPE].reshape(QL, NOPE_ALL), wr=wr.reshape(QL, ROPE_ALL),
              wrs=_swap_rope(wr).reshape(QL, ROPE_ALL), bduk=bduk)
    return ffn_in, ffn_out, pw, mw, bduv, full[11].reshape(D, D)


def _example_step(x, target, mod, norm_g, pvec, ffn_in, ffn_out, pw, mw, bduv, wo, reducer):
    S = x.shape[0]
    tabs = _rope_tables(S)
    vec = [[_vec(norm_g, mod, i, k) for k in range(3)] for i in range(2)]
    saved = {}
    for i in range(2):
        xin = x
        x, a, u, h = ffn_fwd(xin, vec[i][0], ffn_in[i][0], ffn_out[i][0], 0.5)
        saved[i, 0] = (xin, a, u, h)
        xin = x
        if i == 0:
            x, y, z = pool_fwd(xin, vec[i][1], pw, pvec)
            saved[i, 1] = (xin, y, z)
        else:
            h_m, cq_raw, ckv_raw, cqn, qnope, qcat, kcat, vcat = mla_pre(xin, vec[i][1], mw, tabs)
            olat, lse = attn_fwd(qcat, kcat, vcat)
            x, u_m, ocat = mla_post(olat, xin, vec[i][1], bduv, wo)
            saved[i, 1] = (xin, h_m, cq_raw, ckv_raw, cqn, qnope, qcat, kcat, olat, lse, u_m, ocat)
        xin = x
        x, a, u, h = ffn_fwd(xin, vec[i][2], ffn_in[i][1], ffn_out[i][1], 0.5)
        saved[i, 2] = (xin, a, u, h)
    loss, dx = loss_head(x, target)

    vg = {}
    G = D // 4

    def ffn_grads(i, k, dw_in, dw_out):
        return [(0, 2 * i + k, 4, dw_in.reshape(N_CHIP, 2, D // 2, FSH)),
                (1, 2 * i + k, 4, dw_out.reshape(N_CHIP, 2, DFF // 8, D))]

    piece = lambda t: t[:8, :128]
    for i in (1, 0):
        xin, a, u, h = saved[i, 2]
        dx, du, act, da, vg[i, 2] = ffn_bwd(dx, xin, u, a, vec[i][2], ffn_in[i][1], ffn_out[i][1], 0.5)
        reducer.advance(after=(piece(dx),))
        reducer.add(f"f{i}1", ffn_grads(i, 1, *ffn_dw(h, da, act, du)))
        if i == 0:
            xin, y, z = saved[i, 1]
            dx, vg[i, 1], pgrad, g_pool = pool_bwd(dx, xin, y, z, vec[i][1], pw, pvec)
            reducer.advance(after=(piece(dx),))
        else:
            xin, h_m, cq_raw, ckv_raw, cqn, qnope, qcat, kcat, olat, lse, u_m, ocat = saved[i, 1]
            du, docat, dolat, delta, vg_post = mla_post_bwd(dx, u_m, olat, vec[i][1], bduv, wo)
            reducer.advance()
            dq, dk, dv = attn_bwd(qcat, kcat, kcat.T, dolat, lse, delta)
            reducer.advance(after=(piece(dk),))
            dx, dlat, dql, dqcat, vg_pre, ngrad = mla_pre_bwd(
                dx, dq, dk, dv, xin, cq_raw, ckv_raw, vec[i][1], mw, tabs)
            vg[i, 1] = vg_post + vg_pre
            g = mla_dw(h_m, dlat, cqn, dqcat, dql, qnope, olat, docat, ocat, du)
            slots = lambda a: a.reshape(D, SLOTS, ROPE).sum(axis=1)
            g_kr = slots(g["in"][:, QL + KVL:QL + KVL + 128]) + _swap_rope(slots(g["in"][:, QL + KVL + 128:]))
            g_in = jnp.concatenate([g["in"][:, :QL + KVL], g_kr], axis=1)
            g_r = g["q"][:, NOPE_ALL:NOPE_ALL + ROPE_ALL].reshape(QL, N_HEADS, ROPE)
            g_rs = g["q"][:, NOPE_ALL + ROPE_ALL:].reshape(QL, N_HEADS, ROPE)
            g_uq = jnp.concatenate([g["q"][:, :NOPE_ALL].reshape(QL, N_HEADS, NOPE), g_r + _swap_rope(g_rs)], axis=-1)

            def heads(pairs):
                blk = pairs.reshape(N_PAIR, 2, KVL, 2, NOPE)
                per_head = jnp.stack([blk[:, 0, :, 0, :], blk[:, 1, :, 1, :]], axis=1).reshape(N_HEADS, KVL, NOPE)
                return jnp.transpose(per_head, (1, 0, 2)).reshape(KVL, N_HEADS * NOPE)

            reducer.add("mla", [(3, 0, 1, g_in.reshape(N_CHIP, 2, D // 8, QL + KVL + ROPE)),
                                (4, 0, 1, g_uq.reshape(N_CHIP, 2, QL // 8, N_HEADS * (NOPE + ROPE))),
                                (5, 0, 1, g["o"].reshape(N_CHIP, 2, D // 8, D))])
            reducer.add_replicated(jnp.concatenate([heads(g["uk"]), heads(g["uv"])], axis=0))
        xin, a, u, h = saved[i, 0]
        dx, du, act, da, vg[i, 0] = ffn_bwd(dx, xin, u, a, vec[i][0], ffn_in[i][0], ffn_out[i][0], 0.5)
        if i == 1:
            reducer.advance(after=(piece(dx),))
        grads = ffn_grads(i, 0, *ffn_dw(h, da, act, du))
        if i == 0:
            grads.append((2, 0, 1, g_pool.reshape(N_CHIP, 2, 2 * G // N_CHIP, G)))
        reducer.add(f"f{i}0", grads)
    return loss, dx, vg, pgrad, ngrad


class _GradReducer:
    def __init__(self, core, place, dev):
        self.core, self.place, self.dev = core, place, dev
        self.stacks = {}
        self.live = []
        self.replicated = None

    def add(self, tag, items):
        gen = self._run(tag, items)
        next(gen)
        self.live.append(gen)

    def add_replicated(self, block):
        self.replicated = gather_blocks("gather_ukv", place_block("place_ukv", self.dev, block))

    def advance(self, after=None):
        self.after = after
        live = []
        for gen in self.live:
            try:
                next(gen)
                live.append(gen)
            except StopIteration:
                pass
        self.live = live

    def finish(self):
        while self.live:
            self.advance()
        return self.stacks, self.replicated

    def _run(self, tag, items):
        grads, from_pair = reduce_pair(f"reduce_pair_{tag}", [g for *_, g in items])
        yield
        sums = [pair_add(f"pair_add_{tag}_{j}", self.core, g, p) for j, (g, p) in enumerate(zip(grads, from_pair))]
        sums, from_chips = reduce_chips(f"reduce_chips_{tag}", sums)
        yield
        for j, ((o, k, n_slots, _), s, p) in enumerate(zip(items, sums, from_chips)):
            self.stacks[o] = chip_add(f"chip_add_{tag}_{j}", self.place, s, p, k, n_slots, self.stacks.get(o),
                                      self.after)
        shared = share_halves(f"share_halves_{tag}", [self.stacks[o] for o, *_ in items], [k for _, k, *_ in items])
        for (o, *_), v in zip(items, shared):
            self.stacks[o] = v


SMALL_IN = 8 * 640
SMALL_GRAD = 8 * 4224
SMALL_W = 8 * 2944


def _pack(parts, total):
    flat = jnp.concatenate([p.reshape(-1) for p in parts])
    return jnp.concatenate([flat, jnp.zeros((total - flat.shape[0],), F32)]).reshape(8, total // 8)


def kernel(x, c, ada_w, ada_b, norm_g, ffn_w_in, ffn_w_out, pool_w, pool_b, pool_scale, mla_w_in, mla_q_norm, mla_kv_norm, mla_w_uq, mla_w_uk, mla_w_uv, mla_w_o, loss_target, m_ada_w, m_ada_b, m_norm_g, m_ffn_w_in, m_ffn_w_out, m_pool_w, m_pool_b, m_pool_scale, m_mla_w_in, m_mla_q_norm, m_mla_kv_norm, m_mla_w_uq, m_mla_w_uk, m_mla_w_uv, m_mla_w_o, v_ada_w, v_ada_b, v_norm_g, v_ffn_w_in, v_ffn_w_out, v_pool_w, v_pool_b, v_pool_scale, v_mla_w_in, v_mla_q_norm, v_mla_kv_norm, v_mla_w_uq, v_mla_w_uk, v_mla_w_uv, v_mla_w_o):
    ix, iy, ic = _place()
    chip = 2 * ix + iy
    dev = 2 * chip + ic
    core_arr = ic.astype(jnp.int32).reshape(1)
    chip_arr = chip.astype(jnp.int32).reshape(1)
    S = x.shape[1]
    G = D // 4
    NG = D // N_CHIP

    def chip_cols(a, width, axis):
        return lax.dynamic_slice_in_dim(a, chip * width, width, axis)

    got = gather_devices("gather_small_in", _pack([c, norm_g, pool_b, mla_q_norm], SMALL_IN)).reshape(N_DEV, SMALL_IN)
    c_all = got[:, :D]
    parts = got[0::2]
    o = D
    norm_g_full = parts[:, o:o + 12 * NG].reshape(N_CHIP, 2, 6, NG).transpose(1, 2, 0, 3).reshape(2, 6, D)
    o += 12 * NG
    pool_b_full = parts[:, o:o + G].reshape(N_CHIP, 4, G // N_CHIP).transpose(1, 0, 2).reshape(1, D)
    o += G
    q_norm_full = parts[:, o:o + QL // N_CHIP].reshape(1, QL)
    pvec = jnp.concatenate([pool_b_full, pool_scale, jnp.zeros((6, D), F32)], axis=0)

    c_pad = jnp.concatenate([c_all, jnp.zeros((8, D), F32)], axis=0)
    mod_loc = mod_fwd(c_pad, ada_w, chip_cols(ada_b, MOD_COLS, 1).reshape(2, 1, MOD_COLS))
    got = gather_devices("gather_mod", mod_loc[:, :8].transpose(1, 0, 2).reshape(8, 2 * MOD_COLS))
    mine = lax.dynamic_index_in_dim(got[0::2].reshape(N_CHIP, 8, 2, MOD_COLS), dev, axis=1, keepdims=False)
    mod = mine.transpose(1, 0, 2).reshape(2, 9, D)

    bf = lambda a: a.astype(BF16)
    w_in_halves = ffn_w_in.reshape(2, 2, 2, D // 2, FSH)
    w_out_halves = ffn_w_out.reshape(2, 2, 2, DFF // 8, D)
    shards = [(w_in_halves, (i, k)) for i in range(2) for k in range(2)]
    shards += [(w_out_halves, (i, k)) for i in range(2) for k in range(2)]
    shards += [(pool_w.reshape(2, 2 * G // N_CHIP, G), ()), (mla_w_in.reshape(2, D // 8, QL + KVL + ROPE), ()),
               (mla_w_uq.reshape(2, QL // 8, N_HEADS * (NOPE + ROPE)), ()), (mla_w_o.reshape(2, D // 8, D), ())]
    full = [None] * len(shards)
    stages = [(0, 4, 8), (1, 5), (2, 6), (9, 10, 11), (3, 7)]
    first, token = cast_into_slots("cast_first", chip_arr, [shards[t] for t in stages[0]])
    slotted = dict(zip(stages[0], first))
    rest = [t for members in stages[1:] for t in members]
    for stage, members in enumerate(stages):
        got_w = gather_weights(f"gather_weights_{stage}", stage, [slotted[t] for t in members])
        for t, a in zip(members, got_w):
            full[t] = a
        if stage == 0:
            slotted.update(zip(rest, cast_into_slots("cast_rest", chip_arr, [shards[t] for t in rest], token)[0]))
    ffn_in, ffn_out, pw, mw, bduv, wo = _unpack_weights(full, bf(mla_w_uk[0]), bf(mla_w_uv[0]), q_norm_full,
                                                        mla_kv_norm)

    place_arr = jnp.stack([chip, ic]).astype(jnp.int32)
    reducer = _GradReducer(core_arr, place_arr, dev.astype(jnp.int32).reshape(1))
    loss_mine, grad_x, vg, pgrad, ngrad = _example_step(
        x[0], loss_target[0], mod, norm_g_full, pvec, ffn_in, ffn_out, pw, mw, bduv, wo, reducer)

    dmod = jnp.stack([jnp.concatenate([vg[i, k][0:3] for k in range(3)]) for i in range(2)])
    dnorm = jnp.stack([jnp.concatenate([vg[i, k][3:5] for k in range(3)]) for i in range(2)])
    small = _pack([dmod, dnorm, pgrad[0], pgrad[1], ngrad[0], ngrad[1, :KVL], loss_mine], SMALL_GRAD)
    got = gather_devices("gather_small_grad", small)
    tot = sum_devices("sum_small_grad", got).reshape(-1)
    n_mod = 2 * 9 * D
    g_ada_b = tot[:n_mod].reshape(ada_b.shape)
    o = n_mod
    g_norm = chip_cols(tot[o:o + 12 * D].reshape(2, 6, D), NG, 2)
    o += 12 * D
    g_pool_b = chip_cols(tot[o:o + D].reshape(1, 4, G), G // N_CHIP, 2)
    o += D
    g_pool_scale = tot[o:o + D].reshape(pool_scale.shape)
    o += D
    g_q_norm = chip_cols(tot[o:o + QL].reshape(1, QL), QL // N_CHIP, 1)
    o += QL
    g_kv_norm = tot[o:o + KVL].reshape(mla_kv_norm.shape)
    loss = tot[o + KVL]
    dmod_all = chip_cols(got.reshape(N_DEV, -1)[:, :n_mod].reshape(N_DEV, 2, 9 * D), MOD_COLS, 2)
    dmod_pad = jnp.concatenate([dmod_all.transpose(1, 0, 2), jnp.zeros((2, 8, MOD_COLS), F32)], axis=1)

    g_ada_w, d_ada_w, nm_ada_w, nv_ada_w = adamw_ada(c_pad, dmod_pad, ada_w, m_ada_w, v_ada_w)
    small_names = ["ada_b", "norm_g", "pool_b", "pool_scale", "mla_q_norm", "mla_kv_norm"]
    small_w = [ada_b, norm_g, pool_b, pool_scale, mla_q_norm, mla_kv_norm]
    small_g = [g_ada_b, g_norm, g_pool_b, g_pool_scale, g_q_norm, g_kv_norm]
    small_m = [m_ada_b, m_norm_g, m_pool_b, m_pool_scale, m_mla_q_norm, m_mla_kv_norm]
    small_v = [v_ada_b, v_norm_g, v_pool_b, v_pool_scale, v_mla_q_norm, v_mla_kv_norm]
    packed = adamw("adamw_small", *[_pack(p, SMALL_W) for p in (small_w, small_g, small_m, small_v)])
    upd = {}
    o = 0
    for name, w in zip(small_names, small_w):
        upd[name] = [p.reshape(-1)[o:o + w.size].reshape(w.shape) for p in packed]
        o += w.size
    upd["ada_w"] = [d_ada_w, nm_ada_w, nv_ada_w]

    reducer.advance(after=(d_ada_w[0, :8, :128],))
    ffn = [("ffn_w_in", 0, ffn_w_in, m_ffn_w_in, v_ffn_w_in), ("ffn_w_out", 1, ffn_w_out, m_ffn_w_out, v_ffn_w_out)]
    slots = lambda a: a.reshape((4,) + a.shape[2:])
    early = {name: adamw(f"adamw_{name}_early", slots(w), slots(reducer.stacks[o].reshape(w.shape)), slots(m),
                         slots(v), part=(1, 3), copy_grad=True) for name, o, w, m, v in ffn}
    g_mla_in = reducer.stacks[3].reshape(mla_w_in.shape)
    g_uq = reducer.stacks[4].reshape(mla_w_uq.shape)
    g_wo = reducer.stacks[5].reshape(mla_w_o.shape)
    for name, w, g, m, v in [("mla_w_in", mla_w_in, g_mla_in, m_mla_w_in, v_mla_w_in),
                             ("mla_w_uq", mla_w_uq, g_uq, m_mla_w_uq, v_mla_w_uq),
                             ("mla_w_o", mla_w_o, g_wo, m_mla_w_o, v_mla_w_o)]:
        upd[name] = adamw("adamw_" + name, w, g, m, v)

    reducer.advance(after=(early["ffn_w_in"][0][1, :8, :128], early["ffn_w_out"][0][1, :8, :128],
                           upd["mla_w_o"][0][0, :8, :128], upd["mla_w_in"][0][0, :8, :128]))
    ukv = sum_devices("sum_ukv", reducer.replicated)
    g_uk = ukv[:KVL].reshape(mla_w_uk.shape)
    g_uv = ukv[KVL:].reshape(mla_w_uv.shape)
    upd["mla_w_uk"] = adamw("adamw_mla_w_uk", mla_w_uk, g_uk, m_mla_w_uk, v_mla_w_uk)
    upd["mla_w_uv"] = adamw("adamw_mla_w_uv", mla_w_uv, g_uv, m_mla_w_uv, v_mla_w_uv)
    stacks, _ = reducer.finish()
    g_pool_w = stacks[2].reshape(pool_w.shape)
    g_ffn = {}
    for name, o, w, m, v in ffn:
        done = adamw(f"adamw_{name}_last", slots(w), slots(stacks[o].reshape(w.shape)), slots(m), slots(v),
                     part=(0, 1), prev=early[name], copy_grad=True)
        upd[name] = [p.reshape(w.shape) for p in done[:3]]
        g_ffn[name] = done[3].reshape(w.shape)
    g_ffn_in, g_ffn_out = g_ffn["ffn_w_in"], g_ffn["ffn_w_out"]
    upd["pool_w"] = adamw("adamw_pool_w", pool_w, g_pool_w, m_pool_w, v_pool_w)

    order = ["ada_w", "ada_b", "norm_g", "ffn_w_in", "ffn_w_out", "pool_w", "pool_b", "pool_scale", "mla_w_in",
             "mla_q_norm", "mla_kv_norm", "mla_w_uq", "mla_w_uk", "mla_w_uv", "mla_w_o"]
    grad = dict(ada_w=g_ada_w, ada_b=g_ada_b, norm_g=g_norm, ffn_w_in=g_ffn_in, ffn_w_out=g_ffn_out, pool_w=g_pool_w,
                pool_b=g_pool_b, pool_scale=g_pool_scale, mla_w_in=g_mla_in, mla_q_norm=g_q_norm,
                mla_kv_norm=g_kv_norm, mla_w_uq=g_uq, mla_w_uk=g_uk, mla_w_uv=g_uv, mla_w_o=g_wo)
    return (loss, grad_x[None], *[grad[n] for n in order], *[upd[n][0] for n in order],
            *[upd[n][1] for n in order], *[upd[n][2] for n in order])
```

```python
import functools

import jax
import jax.numpy as jnp
from jax import lax
from jax.experimental import pallas as pl
from jax.experimental.pallas import tpu as pltpu
from jax.experimental.pallas import tpu_sc as plsc

F32 = jnp.float32
BF16 = jnp.bfloat16

D = 1024
DFF = 2816
FSH = 1408
N_CHIP = 4
N_DEV = 8
N_HEADS = 16
NOPE = 64
ROPE = 32
VH = 64
QL = 256
KVL = 128
LANES = 128
SUBLANES = 8
QPAD = 256
EPS = 1e-6
ATTN_SCALE = (NOPE + ROPE) ** -0.5
ROPE_THETA = 10000.0
POOL_WINDOWS = (2, 4, 8, 16)
HALO = 8
ATTN_TQ = 1024
ATTN_KC = 512
ROW_TILE = 512
DW_TK = 2048

ADAM_LR, ADAM_B1, ADAM_B2, ADAM_EPS, ADAM_WD, ADAM_STEP = 0.001, 0.9, 0.999, 1e-08, 0.01, 10

VMEM_LIMIT = 60 * 1024 * 1024
MESH = pl.DeviceIdType.MESH

NT = (((1,), (1,)), ((), ()))
TN = (((0,), (0,)), ((), ()))


def _params(*sem):
    return pltpu.CompilerParams(dimension_semantics=sem, vmem_limit_bytes=VMEM_LIMIT)


def _dot(a, b, dims=None):
    if dims is None:
        return jnp.dot(a, b, preferred_element_type=F32)
    return lax.dot_general(a, b, dims, preferred_element_type=F32)


def _rms(x):
    r = lax.rsqrt(jnp.mean(x * x, axis=-1, keepdims=True) + EPS)
    return x * r, r


def _rms_bwd(xhat, r, dxhat):
    return r * (dxhat - xhat * jnp.mean(dxhat * xhat, axis=-1, keepdims=True))


def _as_row(col):
    return jnp.broadcast_to(col, (col.shape[0], LANES)).T[0:1, :]


def _prenorm(x, vec_ref):
    xhat, r = _rms(x)
    h = xhat * vec_ref[0:1, :] * (1.0 + vec_ref[3:4, :]) + vec_ref[2:3, :]
    return h, xhat, r


def _postnorm_bwd(dout, u, vec_ref, weight):
    uhat, r = _rms(u)
    gt = weight * (1.0 + vec_ref[4:5, :])
    dy = dout * gt
    dgate_rows = (weight * dout) * (uhat * vec_ref[1:2, :])
    dgpost_rows = dy * uhat
    du = _rms_bwd(uhat, r, dy * vec_ref[1:2, :])
    return du, dgate_rows, dgpost_rows


def _prenorm_bwd(dh, x, vec_ref, vg_ref):
    xhat, r = _rms(x)
    sc1 = 1.0 + vec_ref[3:4, :]
    g = vec_ref[0:1, :]
    vg_ref[0:1, :] += jnp.sum(dh, axis=0, keepdims=True)
    vg_ref[1:2, :] += jnp.sum(dh * (xhat * g), axis=0, keepdims=True)
    vg_ref[3:4, :] += jnp.sum(dh * sc1 * xhat, axis=0, keepdims=True)
    return _rms_bwd(xhat, r, dh * g * sc1)


def ffn_fwd(x, vec, w_in, w_out, weight):
    S = x.shape[0]
    tm = min(512, S)
    row = lambda i: (i, 0)
    half = lambda j: [_w3((8, D)), pl.BlockSpec((None, D, FSH), lambda i: (j, 0, 0)),
                      pl.BlockSpec((None, D, FSH), lambda i: (j + 2, 0, 0)),
                      pl.BlockSpec((None, FSH, D), lambda i: (j, 0, 0))]
    a_spec = lambda j: pl.BlockSpec((2, tm, FSH), lambda i: (0, i, j))
    a_shape = jax.ShapeDtypeStruct((2, S, DFF), BF16)

    def hidden(hb, wg_ref, wu_ref, wo_ref, a_ref):
        g = _dot(hb, wg_ref[...])
        up = _dot(hb, wu_ref[...])
        a_ref[0] = g.astype(BF16)
        a_ref[1] = up.astype(BF16)
        act = (g * jax.nn.sigmoid(g)) * up
        return _dot(act.astype(BF16), wo_ref[...])

    def first(x_ref, vec_ref, wg_ref, wu_ref, wo_ref, h_ref, a_ref, u_ref):
        h, _, _ = _prenorm(x_ref[...], vec_ref)
        hb = h.astype(BF16)
        h_ref[...] = hb
        u_ref[...] = hidden(hb, wg_ref, wu_ref, wo_ref, a_ref)

    h, a, u_half = pl.pallas_call(
        first, name="ffn_fwd_first", grid=(S // tm,),
        in_specs=[pl.BlockSpec((tm, D), row)] + half(0),
        out_specs=[pl.BlockSpec((tm, D), row), a_spec(0), pl.BlockSpec((tm, D), row)],
        out_shape=[jax.ShapeDtypeStruct((S, D), BF16), a_shape, jax.ShapeDtypeStruct((S, D), F32)],
        compiler_params=_params("parallel"),
    )(x, vec, w_in, w_in, w_out)

    def second(x_ref, h_ref, uh_ref, vec_ref, wg_ref, wu_ref, wo_ref, a_in, xo_ref, a_ref, u_ref):
        u = uh_ref[...] + hidden(h_ref[...], wg_ref, wu_ref, wo_ref, a_ref)
        u_ref[...] = u
        uhat, _ = _rms(u)
        xo_ref[...] = x_ref[...] + (weight * (1.0 + vec_ref[4:5, :])) * (uhat * vec_ref[1:2, :])

    xo, a, u = pl.pallas_call(
        second, name="ffn_fwd_second", grid=(S // tm,),
        in_specs=[pl.BlockSpec((tm, D), row), pl.BlockSpec((tm, D), row), pl.BlockSpec((tm, D), row)] + half(1) + [_ANY],
        out_specs=[pl.BlockSpec((tm, D), row), a_spec(1), pl.BlockSpec((tm, D), row)],
        out_shape=[jax.ShapeDtypeStruct((S, D), F32), a_shape, jax.ShapeDtypeStruct((S, D), F32)],
        input_output_aliases={7: 1},
        compiler_params=_params("parallel"),
    )(x, h, u_half, vec, w_in, w_in, w_out, a)
    return xo, a, u, h


def ffn_bwd(dout, x, u, a, vec, w_in, w_out, weight):
    S = x.shape[0]
    tm = min(512, S)
    row = lambda i: (i, 0)
    half = lambda j: [pl.BlockSpec((2, tm, FSH), lambda i: (0, i, j)), _w3((8, D)),
                      pl.BlockSpec((None, D, FSH), lambda i: (j, 0, 0)),
                      pl.BlockSpec((None, D, FSH), lambda i: (j + 2, 0, 0)),
                      pl.BlockSpec((None, FSH, D), lambda i: (j, 0, 0))]
    half_out = lambda j: [pl.BlockSpec((tm, FSH), lambda i: (i, j)), pl.BlockSpec((2, tm, FSH), lambda i: (0, i, j))]
    half_shape = [jax.ShapeDtypeStruct((S, DFF), BF16), jax.ShapeDtypeStruct((2, S, DFF), BF16)]

    def hidden_bwd(du, a_ref, wg_ref, wu_ref, wo_ref, act_ref, da_ref):
        dact = _dot(du, wo_ref[...], NT)
        g = a_ref[0].astype(F32)
        up = a_ref[1].astype(F32)
        s = jax.nn.sigmoid(g)
        silu = g * s
        act_ref[...] = (silu * up).astype(BF16)
        dg = (dact * up * (s * (1.0 + g * (1.0 - s)))).astype(BF16)
        dup = (dact * silu).astype(BF16)
        da_ref[0] = dg
        da_ref[1] = dup
        return _dot(dg, wg_ref[...], NT) + _dot(dup, wu_ref[...], NT)

    def first(do_ref, u_ref, a_ref, vec_ref, wg_ref, wu_ref, wo_ref, du_ref, dh_ref, act_ref, da_ref, vg_ref):
        @pl.when(pl.program_id(0) == 0)
        def _():
            vg_ref[...] = jnp.zeros_like(vg_ref)

        du, dgate_rows, dgpost_rows = _postnorm_bwd(do_ref[...], u_ref[...], vec_ref, weight)
        vg_ref[2:3, :] += jnp.sum(dgate_rows, axis=0, keepdims=True)
        vg_ref[4:5, :] += jnp.sum(dgpost_rows, axis=0, keepdims=True)
        du = du.astype(BF16)
        du_ref[...] = du
        dh_ref[...] = hidden_bwd(du, a_ref, wg_ref, wu_ref, wo_ref, act_ref, da_ref)

    du, dh, act, da, vg_post = pl.pallas_call(
        first, name="ffn_bwd_first", grid=(S // tm,),
        in_specs=[pl.BlockSpec((tm, D), row), pl.BlockSpec((tm, D), row)] + half(0),
        out_specs=[pl.BlockSpec((tm, D), row), pl.BlockSpec((tm, D), row)] + half_out(0) + [_w3((8, D))],
        out_shape=[jax.ShapeDtypeStruct((S, D), BF16), jax.ShapeDtypeStruct((S, D), F32)] + half_shape
        + [jax.ShapeDtypeStruct((8, D), F32)],
        compiler_params=_params("arbitrary"),
    )(dout, u, a, vec, w_in, w_in, w_out)

    def second(do_ref, x_ref, du_ref, dh_ref, a_ref, vec_ref, wg_ref, wu_ref, wo_ref, act_in, da_in,
               dx_ref, act_ref, da_ref, vg_ref):
        @pl.when(pl.program_id(0) == 0)
        def _():
            vg_ref[...] = jnp.zeros_like(vg_ref)

        dh = dh_ref[...] + hidden_bwd(du_ref[...], a_ref, wg_ref, wu_ref, wo_ref, act_ref, da_ref)
        dx_ref[...] = do_ref[...] + _prenorm_bwd(dh, x_ref[...], vec_ref, vg_ref)

    dx, act, da, vg_pre = pl.pallas_call(
        second, name="ffn_bwd_second", grid=(S // tm,),
        in_specs=[pl.BlockSpec((tm, D), row), pl.BlockSpec((tm, D), row), pl.BlockSpec((tm, D), row),
                  pl.BlockSpec((tm, D), row)] + half(1) + [_ANY, _ANY],
        out_specs=[pl.BlockSpec((tm, D), row)] + half_out(1) + [_w3((8, D))],
        out_shape=[jax.ShapeDtypeStruct((S, D), F32)] + half_shape + [jax.ShapeDtypeStruct((8, D), F32)],
        input_output_aliases={9: 1, 10: 2},
        compiler_params=_params("arbitrary"),
    )(dout, x, du, dh, a, vec, w_in, w_in, w_out, act, da)
    return dx, du, act, da, vg_post + vg_pre


def dw_matmul(name, a, b, a_spec, b_spec, out_shape, out_spec, grid):
    def body(a_ref, b_ref, o_ref):
        @pl.when(pl.program_id(len(grid) - 1) == 0)
        def _():
            o_ref[...] = jnp.zeros_like(o_ref)

        o_ref[...] += _dot(a_ref[...], b_ref[...], TN)

    return pl.pallas_call(
        body, name=name, grid=grid, in_specs=[a_spec, b_spec], out_specs=out_spec,
        out_shape=jax.ShapeDtypeStruct(out_shape, F32),
        compiler_params=_params(*(["parallel"] * (len(grid) - 1) + ["arbitrary"])),
    )(a, b)


def ffn_dw(h, da, act, du):
    S = h.shape[0]
    tk = min(DW_TK, S)
    dw_in = dw_matmul("ffn_dw_in", h, da,
                      pl.BlockSpec((tk, D), lambda n, k: (k, 0)),
                      pl.BlockSpec((None, tk, FSH), lambda n, k: (n // 2, k, n % 2)),
                      (N_CHIP, D, FSH), pl.BlockSpec((None, D, FSH), lambda n, k: (n, 0, 0)),
                      (N_CHIP, S // tk))
    dw_out = dw_matmul("ffn_dw_out", act, du,
                       pl.BlockSpec((tk, FSH), lambda n, k: (k, n)),
                       pl.BlockSpec((tk, D), lambda n, k: (k, 0)),
                       (DFF, D), pl.BlockSpec((FSH, D), lambda n, k: (n, 0)),
                       (2, S // tk))
    return dw_in, dw_out


def _halo_specs(tm, S):
    nb = tm // HALO
    last = S // HALO - 1
    return [pl.BlockSpec((HALO, D), lambda i: (jnp.maximum(i * nb - 1, 0), 0)),
            pl.BlockSpec((tm, D), lambda i: (i, 0)),
            pl.BlockSpec((HALO, D), lambda i: (jnp.minimum((i + 1) * nb, last), 0))]


def _shift_rows(v, k):
    return pltpu.roll(v, k % v.shape[0], 0)


def _window_sum(v, g, forward):
    acc = v + _shift_rows(v, 1 if forward else -1)
    for step in (1, 2, 4)[:g]:
        acc = _shift_rows(acc, step) + _shift_rows(acc, -step)
    return acc


def _pool_count(t, w, S):
    return jnp.maximum(jnp.minimum(t + w // 2, S) - jnp.maximum(t - w // 2, 0), 1).astype(F32)


def pool_fwd(x, vec, pw, pvec):
    S = x.shape[0]
    tm = min(ROW_TILE, S)
    G = D // 4

    def body(xp_ref, x_ref, xn_ref, vec_ref, pw_ref, pv_ref, xo_ref, y_ref, z_ref):
        i = pl.program_id(0)
        xa = jnp.concatenate([xp_ref[...], x_ref[...], xn_ref[...]], axis=0)
        t = i * tm - HALO + lax.broadcasted_iota(jnp.int32, (tm + 2 * HALO, 1), 0)
        h, _, _ = _prenorm(xa, vec_ref)
        h = jnp.where((t >= 0) & (t < S), h, 0.0)
        tmain = t[HALO:HALO + tm]
        for g in range(4):
            hg = h[:, g * G:(g + 1) * G]
            pooled = _window_sum(hg, g, True)[HALO:HALO + tm] / _pool_count(tmain, POOL_WINDOWS[g], S)
            z = (pooled - hg[HALO:HALO + tm]).astype(BF16)
            z_ref[:, g * G:(g + 1) * G] = z
            y_ref[:, g * G:(g + 1) * G] = _dot(z, pw_ref[g]) + pv_ref[0:1, g * G:(g + 1) * G]
        u = y_ref[...] * pv_ref[1:2, :]
        uhat, _ = _rms(u)
        xo_ref[...] = x_ref[...] + (1.0 + vec_ref[4:5, :]) * (uhat * vec_ref[1:2, :])

    row = lambda i: (i, 0)
    full = lambda i: (0, 0)
    return pl.pallas_call(
        body, name="pool_fwd", grid=(S // tm,),
        in_specs=_halo_specs(tm, S) + [pl.BlockSpec((8, D), full), pl.BlockSpec((4, G, G), lambda i: (0, 0, 0)),
                                       pl.BlockSpec((8, D), full)],
        out_specs=[pl.BlockSpec((tm, D), row)] * 3,
        out_shape=[jax.ShapeDtypeStruct((S, D), F32), jax.ShapeDtypeStruct((S, D), F32),
                   jax.ShapeDtypeStruct((S, D), BF16)],
        compiler_params=_params("parallel"),
    )(x, x, x, vec, pw, pvec)


def pool_bwd(dout, x, y, z, vec, pw, pvec):
    S = x.shape[0]
    tm = min(ROW_TILE, S)
    G = D // 4
    R = G // N_CHIP

    def body(dop_ref, do_ref, don_ref, yp_ref, y_ref, yn_ref, x_ref, z_ref, vec_ref, pw_ref, pv_ref,
             dx_ref, vg_ref, pg_ref, dw_ref, dh_ref):
        i = pl.program_id(0)

        @pl.when(i == 0)
        def _():
            vg_ref[...] = jnp.zeros_like(vg_ref)
            pg_ref[...] = jnp.zeros_like(pg_ref)
            dw_ref[...] = jnp.zeros_like(dw_ref)

        doa = jnp.concatenate([dop_ref[...], do_ref[...], don_ref[...]], axis=0)
        ya = jnp.concatenate([yp_ref[...], y_ref[...], yn_ref[...]], axis=0)
        t = i * tm - HALO + lax.broadcasted_iota(jnp.int32, (tm + 2 * HALO, 1), 0)
        inside = (t >= 0) & (t < S)
        main = (t >= i * tm) & (t < (i + 1) * tm)
        du, dgate_rows, dgpost_rows = _postnorm_bwd(doa, ya * pv_ref[1:2, :], vec_ref, 1.0)
        du = jnp.where(inside, du, 0.0)
        vg_ref[2:3, :] += jnp.sum(jnp.where(main, dgate_rows, 0.0), axis=0, keepdims=True)
        vg_ref[4:5, :] += jnp.sum(jnp.where(main, dgpost_rows, 0.0), axis=0, keepdims=True)
        dy = du * pv_ref[1:2, :]
        pg_ref[0:1, :] += jnp.sum(jnp.where(main, dy, 0.0), axis=0, keepdims=True)
        pg_ref[1:2, :] += jnp.sum(jnp.where(main, du * ya, 0.0), axis=0, keepdims=True)
        for g in range(4):
            dyg = dy[:, g * G:(g + 1) * G].astype(BF16)
            dz = _dot(dyg, pw_ref[g], NT)
            e = dz / _pool_count(t, POOL_WINDOWS[g], S)
            dh_ref[:, g * G:(g + 1) * G] = (_window_sum(e, g, False) - dz)[HALO:HALO + tm]
            dwg = _dot(z_ref[:, g * G:(g + 1) * G], dyg[HALO:HALO + tm], TN)
            for q in range(N_CHIP):
                dw_ref[q, g] += dwg[q * R:(q + 1) * R, :]
        dx_ref[...] = do_ref[...] + _prenorm_bwd(dh_ref[...], x_ref[...], vec_ref, vg_ref)

    row = lambda i: (i, 0)
    full = lambda i: (0, 0)
    halo = _halo_specs(tm, S)
    return pl.pallas_call(
        body, name="pool_bwd", grid=(S // tm,),
        in_specs=halo + halo + [pl.BlockSpec((tm, D), row), pl.BlockSpec((tm, D), row), pl.BlockSpec((8, D), full),
                                pl.BlockSpec((4, G, G), lambda i: (0, 0, 0)), pl.BlockSpec((8, D), full)],
        out_specs=[pl.BlockSpec((tm, D), row), pl.BlockSpec((8, D), full), pl.BlockSpec((8, D), full),
                   pl.BlockSpec((N_CHIP, 4, R, G), lambda i: (0, 0, 0, 0))],
        out_shape=[jax.ShapeDtypeStruct((S, D), F32), jax.ShapeDtypeStruct((8, D), F32),
                   jax.ShapeDtypeStruct((8, D), F32), jax.ShapeDtypeStruct((N_CHIP, 4, R, G), F32)],
        scratch_shapes=[pltpu.VMEM((tm, D), F32)],
        compiler_params=_params("arbitrary"),
    )(dout, dout, dout, y, y, y, x, z, vec, pw, pvec)


N_PAIR = N_HEADS // 2
SLOTS = LANES // ROPE
ROPE_ALL = N_HEADS * ROPE
NOPE_ALL = N_HEADS * NOPE
LAT_ALL = N_HEADS * KVL
DLAT = QL + KVL + 2 * LANES
DQ_ALL = NOPE_ALL + 2 * ROPE_ALL


def _w3(shape):
    return pl.BlockSpec(shape, lambda i: (0,) * len(shape))


def _slot_mask(hd, rows):
    lane = lax.broadcasted_iota(jnp.int32, (rows, LANES), 1)
    return (lane // ROPE) == (hd % SLOTS)


MLA_WEIGHTS = ("wq", "wkv", "wkr4", "wkrs4", "qn", "kvn", "wn", "wr", "wrs", "bduk")


def _mla_weight_specs():
    return [_w3((D, QL)), _w3((D, KVL)), _w3((D, LANES)), _w3((D, LANES)), _w3((1, QL)), _w3((1, KVL)),
            _w3((QL, NOPE_ALL)), _w3((QL, ROPE_ALL)), _w3((QL, ROPE_ALL)), _w3((N_PAIR, 2 * NOPE, 2 * KVL))]


def mla_pre(x, vec, mw, tabs):
    S = x.shape[0]
    tm = min(ROW_TILE, S)

    def body(x_ref, vec_ref, cos_ref, sin_ref, wq_ref, wkv_ref, wkr_ref, wkrs_ref, qn_ref, kvn_ref,
             wn_ref, wr_ref, wrs_ref, bduk_ref,
             h_ref, cq_ref, ckv_ref, cqn_ref, qnope_ref, qcat_ref, kcat_ref, vcat_ref):
        h, _, _ = _prenorm(x_ref[...], vec_ref)
        hb = h.astype(BF16)
        h_ref[...] = hb
        cq_raw = _dot(hb, wq_ref[...])
        ckv_raw = _dot(hb, wkv_ref[...])
        cq_ref[...] = cq_raw
        ckv_ref[...] = ckv_raw
        cos, sin = cos_ref[...], sin_ref[...]
        ckv = (_rms(ckv_raw)[0] * kvn_ref[...]).astype(BF16)
        kcat_ref[:, 0:KVL] = ckv
        kcat_ref[:, KVL:] = (_dot(hb, wkr_ref[...]) * cos + _dot(hb, wkrs_ref[...]) * sin).astype(BF16)
        vcat_ref[:, 0:KVL] = ckv
        ones = lax.broadcasted_iota(jnp.int32, (tm, QPAD - KVL), 1) == 0
        vcat_ref[:, KVL:] = jnp.where(ones, 1.0, 0.0).astype(BF16)
        cqb = (_rms(cq_raw)[0] * qn_ref[...]).astype(BF16)
        cqn_ref[...] = cqb
        qn = _dot(cqb, wn_ref[...]).astype(BF16)
        qnope_ref[...] = qn
        cos4, sin4 = jnp.tile(cos, (1, SLOTS)), jnp.tile(sin, (1, SLOTS))
        qr = ((_dot(cqb, wr_ref[...]) * cos4 + _dot(cqb, wrs_ref[...]) * sin4) * ATTN_SCALE).astype(BF16)
        for j in range(N_PAIR):
            ql = (_dot(qn[:, 2 * NOPE * j:2 * NOPE * (j + 1)], bduk_ref[j]) * ATTN_SCALE).astype(BF16)
            for hd in (2 * j, 2 * j + 1):
                qcat_ref[hd, :, 0:KVL] = ql[:, KVL * (hd - 2 * j):KVL * (hd - 2 * j + 1)]
                group = qr[:, LANES * (hd // SLOTS):LANES * (hd // SLOTS + 1)]
                qcat_ref[hd, :, KVL:] = jnp.where(_slot_mask(hd, tm), group, jnp.zeros_like(group))

    row = lambda i: (i, 0)
    hrow = lambda i: (0, i, 0)
    return pl.pallas_call(
        body, name="mla_pre", grid=(S // tm,),
        in_specs=[pl.BlockSpec((tm, D), row), _w3((8, D)), pl.BlockSpec((tm, LANES), row), pl.BlockSpec((tm, LANES), row)]
        + _mla_weight_specs(),
        out_specs=[pl.BlockSpec((tm, D), row), pl.BlockSpec((tm, QL), row), pl.BlockSpec((tm, KVL), row),
                   pl.BlockSpec((tm, QL), row), pl.BlockSpec((tm, NOPE_ALL), row),
                   pl.BlockSpec((N_HEADS, tm, QPAD), hrow), pl.BlockSpec((tm, QPAD), row),
                   pl.BlockSpec((tm, QPAD), row)],
        out_shape=[jax.ShapeDtypeStruct((S, D), BF16), jax.ShapeDtypeStruct((S, QL), F32),
                   jax.ShapeDtypeStruct((S, KVL), F32), jax.ShapeDtypeStruct((S, QL), BF16),
                   jax.ShapeDtypeStruct((S, NOPE_ALL), BF16), jax.ShapeDtypeStruct((N_HEADS, S, QPAD), BF16),
                   jax.ShapeDtypeStruct((S, QPAD), BF16), jax.ShapeDtypeStruct((S, QPAD), BF16)],
        compiler_params=_params("parallel"),
    )(x, vec, tabs[0], tabs[1], *[mw[k] for k in MLA_WEIGHTS])


def attn_fwd(qcat, kcat, vcat):
    S = kcat.shape[0]
    tq = min(ATTN_TQ, S)
    kc = min(ATTN_KC, S)

    def body(q_ref, k_ref, v_ref, o_ref, lse_ref):
        q = q_ref[...]
        m = jnp.full((tq, 1), -jnp.inf, F32)
        ov = jnp.zeros((tq, QPAD), F32)
        for c in range(S // kc):
            s = _dot(q, k_ref[c * kc:(c + 1) * kc, :], NT)
            m_new = jnp.maximum(m, jnp.max(s, axis=-1, keepdims=True))
            p = jnp.exp(s - m_new).astype(BF16)
            ov = ov * jnp.exp(m - m_new) + _dot(p, v_ref[c * kc:(c + 1) * kc, :])
            m = m_new
        l = ov[:, KVL:KVL + 1]
        o_ref[...] = (ov[:, 0:KVL] * (1.0 / l)).astype(BF16)
        lse_ref[...] = _as_row(m + jnp.log(l))

    return pl.pallas_call(
        body, name="attn_fwd", grid=(N_HEADS, S // tq),
        in_specs=[pl.BlockSpec((None, tq, QPAD), lambda h, i: (h, i, 0)),
                  pl.BlockSpec((S, QPAD), lambda h, i: (0, 0)),
                  pl.BlockSpec((S, QPAD), lambda h, i: (0, 0))],
        out_specs=[pl.BlockSpec((tq, KVL), lambda h, i: (i, h)),
                   pl.BlockSpec((None, 1, tq), lambda h, i: (h, 0, i))],
        out_shape=[jax.ShapeDtypeStruct((S, LAT_ALL), BF16), jax.ShapeDtypeStruct((N_HEADS, 1, S), F32)],
        compiler_params=_params("parallel", "parallel"),
    )(qcat, kcat, vcat)


def mla_post(olat, x, vec, bduv, wo):
    S = x.shape[0]
    tm = min(ROW_TILE, S)

    def body(o_ref, x_ref, vec_ref, bduv_ref, wo_ref, xo_ref, u_ref, ocat_ref):
        for j in range(N_PAIR):
            oc = _dot(o_ref[:, 2 * KVL * j:2 * KVL * (j + 1)], bduv_ref[j])
            ocat_ref[:, 2 * VH * j:2 * VH * (j + 1)] = oc.astype(BF16)
        u = _dot(ocat_ref[...], wo_ref[...])
        u_ref[...] = u
        uhat, _ = _rms(u)
        xo_ref[...] = x_ref[...] + (1.0 + vec_ref[4:5, :]) * (uhat * vec_ref[1:2, :])

    row = lambda i: (i, 0)
    return pl.pallas_call(
        body, name="mla_post", grid=(S // tm,),
        in_specs=[pl.BlockSpec((tm, LAT_ALL), row), pl.BlockSpec((tm, D), row), _w3((8, D)),
                  _w3((N_PAIR, 2 * KVL, 2 * VH)), _w3((D, D))],
        out_specs=[pl.BlockSpec((tm, D), row), pl.BlockSpec((tm, D), row), pl.BlockSpec((tm, D), row)],
        out_shape=[jax.ShapeDtypeStruct((S, D), F32), jax.ShapeDtypeStruct((S, D), F32),
                   jax.ShapeDtypeStruct((S, D), BF16)],
        compiler_params=_params("parallel"),
    )(olat, x, vec, bduv, wo)


def mla_post_bwd(dout, u, olat, vec, bduv, wo):
    S = u.shape[0]
    tm = min(ROW_TILE, S)

    def body(do_ref, u_ref, o_ref, vec_ref, bduv_ref, wo_ref, du_ref, docat_ref, dolat_ref, delta_ref, vg_ref):
        @pl.when(pl.program_id(0) == 0)
        def _():
            vg_ref[...] = jnp.zeros_like(vg_ref)

        du, dgate_rows, dgpost_rows = _postnorm_bwd(do_ref[...], u_ref[...], vec_ref, 1.0)
        vg_ref[2:3, :] += jnp.sum(dgate_rows, axis=0, keepdims=True)
        vg_ref[4:5, :] += jnp.sum(dgpost_rows, axis=0, keepdims=True)
        dub = du.astype(BF16)
        du_ref[...] = dub
        docat_ref[...] = _dot(dub, wo_ref[...], NT).astype(BF16)
        for j in range(N_PAIR):
            dol = _dot(docat_ref[:, 2 * VH * j:2 * VH * (j + 1)], bduv_ref[j], NT).astype(BF16)
            dolat_ref[:, 2 * KVL * j:2 * KVL * (j + 1)] = dol
            prod = dol.astype(F32) * o_ref[:, 2 * KVL * j:2 * KVL * (j + 1)].astype(F32)
            delta_ref[2 * j] = _as_row(jnp.sum(prod[:, 0:KVL], axis=-1, keepdims=True))
            delta_ref[2 * j + 1] = _as_row(jnp.sum(prod[:, KVL:], axis=-1, keepdims=True))

    row = lambda i: (i, 0)
    hrow = lambda i: (0, i, 0)
    return pl.pallas_call(
        body, name="mla_post_bwd", grid=(S // tm,),
        in_specs=[pl.BlockSpec((tm, D), row), pl.BlockSpec((tm, D), row), pl.BlockSpec((tm, LAT_ALL), row),
                  _w3((8, D)), _w3((N_PAIR, 2 * KVL, 2 * VH)), _w3((D, D))],
        out_specs=[pl.BlockSpec((tm, D), row), pl.BlockSpec((tm, D), row),
                   pl.BlockSpec((tm, LAT_ALL), row), pl.BlockSpec((N_HEADS, 1, tm), lambda i: (0, 0, i)), _w3((8, D))],
        out_shape=[jax.ShapeDtypeStruct((S, D), BF16), jax.ShapeDtypeStruct((S, D), BF16),
                   jax.ShapeDtypeStruct((S, LAT_ALL), BF16), jax.ShapeDtypeStruct((N_HEADS, 1, S), F32),
                   jax.ShapeDtypeStruct((8, D), F32)],
        compiler_params=_params("arbitrary"),
    )(dout, u, olat, vec, bduv, wo)


def attn_bwd(qcat, kcat, kcat_t, dolat, lse_row, delta_row):
    S = kcat.shape[0]
    tq = min(ATTN_TQ, S)
    kc = min(ATTN_KC, S)

    def body(q_ref, k_ref, kt_ref, do_ref, lse_ref, dl_ref, dq_ref, dk_ref, dv_ref):
        @pl.when((pl.program_id(0) == 0) & (pl.program_id(1) == 0))
        def _():
            dk_ref[...] = jnp.zeros_like(dk_ref)
            dv_ref[...] = jnp.zeros_like(dv_ref)

        q, do = q_ref[...], do_ref[...]
        lse, dl = lse_ref[...], dl_ref[...]
        dqt = jnp.zeros((QPAD, tq), F32)
        for c in range(S // kc):
            rows = slice(c * kc, (c + 1) * kc)
            st = _dot(k_ref[rows, :], q, NT)
            pt = jnp.exp(st - lse)
            dpt = _dot(k_ref[rows, 0:KVL], do, NT)
            dst = (pt * (dpt - dl)).astype(BF16)
            dv_ref[rows, :] += _dot(pt.astype(BF16), do)
            dk_ref[rows, :] += _dot(dst, q)
            dqt = dqt + _dot(kt_ref[:, rows], dst)
        dq_ref[...] = (dqt.T * ATTN_SCALE).astype(BF16)

    return pl.pallas_call(
        body, name="attn_bwd", grid=(N_HEADS, S // tq),
        in_specs=[pl.BlockSpec((None, tq, QPAD), lambda h, i: (h, i, 0)),
                  pl.BlockSpec((S, QPAD), lambda h, i: (0, 0)),
                  pl.BlockSpec((QPAD, S), lambda h, i: (0, 0)),
                  pl.BlockSpec((tq, KVL), lambda h, i: (i, h)),
                  pl.BlockSpec((None, 1, tq), lambda h, i: (h, 0, i)),
                  pl.BlockSpec((None, 1, tq), lambda h, i: (h, 0, i))],
        out_specs=[pl.BlockSpec((None, tq, QPAD), lambda h, i: (h, i, 0)),
                   pl.BlockSpec((S, QPAD), lambda h, i: (0, 0)),
                   pl.BlockSpec((S, KVL), lambda h, i: (0, 0))],
        out_shape=[jax.ShapeDtypeStruct((N_HEADS, S, QPAD), BF16), jax.ShapeDtypeStruct((S, QPAD), F32),
                   jax.ShapeDtypeStruct((S, KVL), F32)],
        compiler_params=_params("arbitrary", "arbitrary"),
    )(qcat, kcat, kcat_t, dolat, lse_row, delta_row)


def mla_pre_bwd(dout, dq, dk, dv, x, cq_raw, ckv_raw, vec, mw, tabs):
    S = x.shape[0]
    tm = min(ROW_TILE, S)

    def body(do_ref, dq_ref, dk_ref, dv_ref, x_ref, cq_ref, ckv_ref, vec_ref, cos_ref, sin_ref,
             wq_ref, wkv_ref, wkr_ref, wkrs_ref, qn_ref, kvn_ref, wn_ref, wr_ref, wrs_ref, bduk_ref,
             dx_ref, dlat_ref, dql_ref, dqcat_ref, vg_ref, ng_ref):
        @pl.when(pl.program_id(0) == 0)
        def _():
            vg_ref[...] = jnp.zeros_like(vg_ref)
            ng_ref[...] = jnp.zeros_like(ng_ref)

        cos, sin = cos_ref[...], sin_ref[...]
        for j in range(N_PAIR):
            dql = jnp.concatenate([dq_ref[2 * j, :, 0:KVL], dq_ref[2 * j + 1, :, 0:KVL]], axis=1)
            dql_ref[:, 2 * KVL * j:2 * KVL * (j + 1)] = dql
            dqcat_ref[:, 2 * NOPE * j:2 * NOPE * (j + 1)] = _dot(dql, bduk_ref[j], NT).astype(BF16)
        groups = []
        for grp in range(N_HEADS // SLOTS):
            acc = jnp.zeros((tm, LANES), F32)
            for hd in range(SLOTS * grp, SLOTS * (grp + 1)):
                acc = acc + jnp.where(_slot_mask(hd, tm), dq_ref[hd, :, KVL:].astype(F32), 0.0)
            groups.append(acc)
        dqr = jnp.concatenate(groups, axis=1)
        qa = (dqr * jnp.tile(cos, (1, SLOTS))).astype(BF16)
        qb = (dqr * jnp.tile(sin, (1, SLOTS))).astype(BF16)
        dqcat_ref[:, NOPE_ALL:NOPE_ALL + ROPE_ALL] = qa
        dqcat_ref[:, NOPE_ALL + ROPE_ALL:] = qb
        dcq = _dot(dqcat_ref[:, 0:NOPE_ALL], wn_ref[...], NT) + _dot(qa, wr_ref[...], NT) + _dot(qb, wrs_ref[...], NT)
        cqh, rq = _rms(cq_ref[...])
        ng_ref[0:1, :] += jnp.sum(dcq * cqh, axis=0, keepdims=True)
        dcq_raw = _rms_bwd(cqh, rq, dcq * qn_ref[...]).astype(BF16)
        dckv = dk_ref[:, 0:KVL] + dv_ref[...]
        ckvh, rk = _rms(ckv_ref[...])
        ng_ref[1:2, 0:KVL] += jnp.sum(dckv * ckvh, axis=0, keepdims=True)
        dckv_raw = _rms_bwd(ckvh, rk, dckv * kvn_ref[...]).astype(BF16)
        dkr = dk_ref[:, KVL:]
        ka = (dkr * cos).astype(BF16)
        kb = (dkr * sin).astype(BF16)
        dlat_ref[:, 0:QL] = dcq_raw
        dlat_ref[:, QL:QL + KVL] = dckv_raw
        dlat_ref[:, QL + KVL:QL + KVL + LANES] = ka
        dlat_ref[:, QL + KVL + LANES:] = kb
        dh = (_dot(dcq_raw, wq_ref[...], NT) + _dot(dckv_raw, wkv_ref[...], NT)
              + _dot(ka, wkr_ref[...], NT) + _dot(kb, wkrs_ref[...], NT))
        dx_ref[...] = do_ref[...] + _prenorm_bwd(dh, x_ref[...], vec_ref, vg_ref)

    row = lambda i: (i, 0)
    hrow = lambda i: (0, i, 0)
    return pl.pallas_call(
        body, name="mla_pre_bwd", grid=(S // tm,),
        in_specs=[pl.BlockSpec((tm, D), row), pl.BlockSpec((N_HEADS, tm, QPAD), hrow), pl.BlockSpec((tm, QPAD), row),
                  pl.BlockSpec((tm, KVL), row), pl.BlockSpec((tm, D), row), pl.BlockSpec((tm, QL), row),
                  pl.BlockSpec((tm, KVL), row), _w3((8, D)), pl.BlockSpec((tm, LANES), row), pl.BlockSpec((tm, LANES), row)]
        + _mla_weight_specs(),
        out_specs=[pl.BlockSpec((tm, D), row), pl.BlockSpec((tm, DLAT), row), pl.BlockSpec((tm, LAT_ALL), row),
                   pl.BlockSpec((tm, DQ_ALL), row), _w3((8, D)), _w3((8, QL))],
        out_shape=[jax.ShapeDtypeStruct((S, D), F32), jax.ShapeDtypeStruct((S, DLAT), BF16),
                   jax.ShapeDtypeStruct((S, LAT_ALL), BF16), jax.ShapeDtypeStruct((S, DQ_ALL), BF16),
                   jax.ShapeDtypeStruct((8, D), F32), jax.ShapeDtypeStruct((8, QL), F32)],
        compiler_params=_params("arbitrary"),
    )(dout, dq, dk, dv, x, cq_raw, ckv_raw, vec, tabs[0], tabs[1], *[mw[k] for k in MLA_WEIGHTS])


def mla_dw(h, dlat, cqn, dqcat, dql, qnope, olat, docat, ocat, du):
    S = h.shape[0]
    tk = min(DW_TK, S)
    nk = S // tk
    flat = lambda w: pl.BlockSpec((tk, w), lambda k: (k, 0))
    cols = lambda w: pl.BlockSpec((tk, w), lambda n, k: (k, n))
    pair_o = pl.BlockSpec((None, 2 * KVL, 2 * NOPE), lambda n, k: (n, 0, 0))
    g = {}
    g["in"] = dw_matmul("mla_dw_in", h, dlat, flat(D), flat(DLAT), (D, DLAT),
                        pl.BlockSpec((D, DLAT), lambda k: (0, 0)), (nk,))
    g["q"] = dw_matmul("mla_dw_q", cqn, dqcat, flat(QL), flat(DQ_ALL), (QL, DQ_ALL),
                       pl.BlockSpec((QL, DQ_ALL), lambda k: (0, 0)), (nk,))
    g["uk"] = dw_matmul("mla_dw_uk", dql, qnope, cols(2 * KVL), cols(2 * NOPE), (N_PAIR, 2 * KVL, 2 * NOPE), pair_o,
                        (N_PAIR, nk))
    g["uv"] = dw_matmul("mla_dw_uv", olat, docat, cols(2 * KVL), cols(2 * VH), (N_PAIR, 2 * KVL, 2 * VH), pair_o,
                        (N_PAIR, nk))
    g["o"] = dw_matmul("mla_dw_o", ocat, du, cols(256), pl.BlockSpec((tk, D), lambda n, k: (k, 0)), (D, D),
                       pl.BlockSpec((256, D), lambda n, k: (n, 0)), (D // 256, nk))
    return g


def loss_head(y, target):
    S = y.shape[0]
    tm = min(2 * ROW_TILE, S)

    def body(y_ref, t_ref, loss_ref, dy_ref):
        @pl.when(pl.program_id(0) == 0)
        def _():
            loss_ref[...] = jnp.zeros_like(loss_ref)

        err = y_ref[...] - t_ref[...]
        dy_ref[...] = err * (1.0 / D)
        loss_ref[...] += 0.5 * jnp.sum(jnp.mean(err * err, axis=-1, keepdims=True), axis=0, keepdims=True)

    row = lambda i: (i, 0)
    return pl.pallas_call(
        body, name="loss_head", grid=(S // tm,),
        in_specs=[pl.BlockSpec((tm, D), row), pl.BlockSpec((tm, D), row)],
        out_specs=[pl.BlockSpec((1, 1), lambda i: (0, 0)), pl.BlockSpec((tm, D), row)],
        out_shape=[jax.ShapeDtypeStruct((1, 1), F32), jax.ShapeDtypeStruct((S, D), F32)],
        compiler_params=_params("arbitrary"),
    )(y, target)


MOD_COLS = 9 * D // N_CHIP


def mod_fwd(c_pad, ada_w, ada_b_loc):
    tn = MOD_COLS // 3

    def body(c_ref, w_ref, b_ref, o_ref):
        c = c_ref[...]
        sc = (c * jax.nn.sigmoid(c)).astype(BF16)
        o_ref[...] = _dot(sc, w_ref[...].astype(BF16)) + b_ref[...]

    return pl.pallas_call(
        body, name="mod_fwd", grid=(2, 3),
        in_specs=[pl.BlockSpec((16, D), lambda i, n: (0, 0)), pl.BlockSpec((None, D, tn), lambda i, n: (i, 0, n)),
                  pl.BlockSpec((None, 1, tn), lambda i, n: (i, 0, n))],
        out_specs=pl.BlockSpec((None, 16, tn), lambda i, n: (i, 0, n)),
        out_shape=jax.ShapeDtypeStruct((2, 16, MOD_COLS), F32),
        compiler_params=_params("parallel", "parallel"),
    )(c_pad, ada_w, ada_b_loc)


def _adamw_math(w, g, m, v):
    m = ADAM_B1 * m + (1.0 - ADAM_B1) * g
    v = ADAM_B2 * v + (1.0 - ADAM_B2) * (g * g)
    m_hat = m / (1.0 - ADAM_B1 ** ADAM_STEP)
    v_hat = v / (1.0 - ADAM_B2 ** ADAM_STEP)
    delta = -ADAM_LR * (m_hat / (jnp.sqrt(v_hat) + ADAM_EPS) + ADAM_WD * w)
    return delta, m, v


def adamw(name, w, g, m, v, part=None, prev=None, copy_grad=False):
    shape = w.shape
    if part is None and w.size * 4 <= (1 << 20):
        whole = pl.BlockSpec(shape, lambda i: (0,) * len(shape))

        def small_body(w_ref, g_ref, m_ref, v_ref, d_ref, mo_ref, vo_ref):
            d_ref[...], mo_ref[...], vo_ref[...] = _adamw_math(w_ref[...], g_ref[...], m_ref[...], v_ref[...])

        return pl.pallas_call(
            small_body, name=name, grid=(1,), in_specs=[whole] * 4, out_specs=[whole] * 3,
            out_shape=[jax.ShapeDtypeStruct(shape, F32)] * 3, compiler_params=_params("arbitrary"),
        )(w, g, m, v)
    cols = shape[-1]
    rows = w.size // cols
    per_entry = rows // shape[0] if part is not None else rows
    tr = per_entry
    budget_rows = (2 << 20) // (cols * 4)
    for cand in range(min(per_entry, budget_rows) // 8 * 8, 0, -8):
        if per_entry % cand == 0:
            tr = cand
            break
    first, count = part if part is not None else (0, 1)
    tiles = per_entry // tr

    n_out = 4 if copy_grad else 3

    def body(w_ref, g_ref, m_ref, v_ref, *rest):
        outs = rest[-n_out:]
        outs[0][...], outs[1][...], outs[2][...] = _adamw_math(w_ref[...], g_ref[...], m_ref[...], v_ref[...])
        if copy_grad:
            outs[3][...] = g_ref[...]

    spec = pl.BlockSpec((tr, cols), lambda i: (i + first * tiles, 0))
    operands = [a.reshape(rows, cols) for a in (w, g, m, v)]
    aliases = {}
    if prev is not None:
        operands += [p.reshape(rows, cols) for p in prev]
        aliases = {4 + t: t for t in range(n_out)}
    outs = pl.pallas_call(
        body, name=name, grid=(count * tiles,), in_specs=[spec] * 4 + [_ANY] * (len(operands) - 4),
        out_specs=[spec] * n_out, out_shape=[jax.ShapeDtypeStruct((rows, cols), F32)] * n_out,
        input_output_aliases=aliases, compiler_params=_params("parallel"),
    )(*operands)
    return [o.reshape(shape) for o in outs]


def adamw_ada(c_pad, dmod, w, m, v):
    tr = 256

    def body(c_ref, dm_ref, w_ref, m_ref, v_ref, g_ref, d_ref, mo_ref, vo_ref):
        c = c_ref[...]
        sc = (c * jax.nn.sigmoid(c)).astype(BF16)
        g = _dot(sc, dm_ref[...].astype(BF16), TN)
        g_ref[...] = g
        d_ref[...], mo_ref[...], vo_ref[...] = _adamw_math(w_ref[...], g, m_ref[...], v_ref[...])

    wspec = pl.BlockSpec((None, tr, MOD_COLS), lambda i, r: (i, r, 0))
    return pl.pallas_call(
        body, name="adamw_ada", grid=(2, D // tr),
        in_specs=[pl.BlockSpec((16, tr), lambda i, r: (0, r)),
                  pl.BlockSpec((None, 16, MOD_COLS), lambda i, r: (i, 0, 0)), wspec, wspec, wspec],
        out_specs=[wspec] * 4,
        out_shape=[jax.ShapeDtypeStruct((2, D, MOD_COLS), F32)] * 4,
        compiler_params=_params("parallel", "parallel"),
    )(c_pad, dmod, w, m, v)


def sum_devices(name, a):
    _, R, C = a.shape
    tr = R
    for cand in (64, 32, 16, 8):
        if R % cand == 0:
            tr = cand
            break

    def body(a_ref, o_ref):
        acc = a_ref[0]
        for dev in range(1, N_DEV):
            acc = acc + a_ref[dev]
        o_ref[...] = acc

    return pl.pallas_call(
        body, name=name, grid=(R // tr,),
        in_specs=[pl.BlockSpec((N_DEV, tr, C), lambda i: (0, i, 0))],
        out_specs=pl.BlockSpec((tr, C), lambda i: (i, 0)),
        out_shape=jax.ShapeDtypeStruct((R, C), F32),
        compiler_params=_params("parallel"),
    )(a)


def _place():
    return lax.axis_index("x"), lax.axis_index("y"), lax.axis_index("c")


def _other_chips(x, y):
    return [(1 - x, y), (x, 1 - y), (1 - x, 1 - y)]


def gather_devices(name, a):
    m_per, n = a.shape

    def body(x_ref, out_ref, send_sems, recv_sems, local_sem):
        x, y, c = _place()
        me, sibling = (x, y, c), (x, y, 1 - c)
        chips = _other_chips(x, y)

        def rows(px, py, pc):
            return out_ref.at[pl.ds((4 * px + 2 * py + pc) * m_per, m_per), :]

        def copy(k, block, to, src=None):
            return pltpu.make_async_remote_copy(
                src_ref=rows(*block) if src is None else src, dst_ref=rows(*block),
                send_sem=send_sems.at[k], recv_sem=recv_sems.at[k], device_id=to, device_id_type=MESH)

        mine = pltpu.make_async_copy(x_ref, rows(*me), local_sem)
        mine.start()
        first = [copy(0, me, sibling, src=x_ref)]
        first += [copy(1 + j, me, (*chip, c), src=x_ref) for j, chip in enumerate(chips)]
        for cp in first:
            cp.start()
        passed = [copy(4 + j, (*chip, c), sibling) for j, chip in enumerate(chips)]
        for j, chip in enumerate(chips):
            copy(1 + j, (*chip, c), me).wait_recv()
            passed[j].start()
        copy(0, sibling, me).wait_recv()
        for j, chip in enumerate(chips):
            copy(4 + j, (*chip, 1 - c), me).wait_recv()
        for cp in first + passed:
            cp.wait_send()
        mine.wait()

    out = pl.pallas_call(
        body, name=name,
        out_shape=jax.ShapeDtypeStruct((N_DEV * m_per, n), a.dtype),
        in_specs=[pl.BlockSpec(memory_space=pltpu.VMEM)],
        out_specs=pl.BlockSpec(memory_space=pltpu.VMEM),
        scratch_shapes=[pltpu.SemaphoreType.DMA((7,)), pltpu.SemaphoreType.DMA((7,)), pltpu.SemaphoreType.DMA],
        compiler_params=pltpu.CompilerParams(vmem_limit_bytes=VMEM_LIMIT),
    )(a)
    return out.reshape(N_DEV, m_per, n)


_ANY = pl.BlockSpec(memory_space=pl.ANY)


def _hbm_ref(a):
    return jax.new_ref(a, memory_space=pltpu.MemorySpace.HBM)


def _hbm_empty(shape, dtype):
    return jax.empty_ref(jax.ShapeDtypeStruct(shape, dtype), memory_space=pltpu.MemorySpace.HBM)


ID_PAIR, ID_CHIPS, ID_SHARE, ID_UKV = 8, 9, 10, 11


def _sequencer(name, collective_id, n_sem, peers_of, program):
    sems = pltpu.SemaphoreType.DMA((n_sem,))

    @pl.kernel(mesh=plsc.ScalarSubcoreMesh(axis_name="seq", num_cores=1), name=name, scratch_types=[sems, sems],
               compiler_params=pltpu.CompilerParams(collective_id=collective_id))
    def launch(send_sem, recv_sem):
        x, y, c = _place()
        peers = peers_of(x, y, c)
        barrier = pltpu.get_barrier_semaphore()
        for peer in peers:
            pl.semaphore_signal(barrier, inc=1, device_id=peer, device_id_type=MESH)
        pl.semaphore_wait(barrier, len(peers))
        program(x, y, c, send_sem, recv_sem)

    launch()


def gather_weights(name, stage, arrays):
    n = len(arrays)
    refs = [_hbm_ref(a) for a in arrays]

    def program(x, y, c, send_sem, recv_sem):
        me = 2 * x + y
        chips = _other_chips(x, y)

        def ici(t, r, half):
            cx, cy = chips[r]
            mine = refs[t].at[me, half]
            return pltpu.make_async_remote_copy(
                src_ref=mine, dst_ref=mine, send_sem=send_sem.at[3 * t + r], recv_sem=recv_sem.at[3 * t + r],
                device_id=(cx, cy, c), device_id_type=MESH)

        def d2d(t, r, half):
            cx, cy = chips[r]
            there = refs[t].at[2 * cx + cy, half]
            k = 3 * n + 3 * t + r
            return pltpu.make_async_remote_copy(
                src_ref=there, dst_ref=there, send_sem=send_sem.at[k], recv_sem=recv_sem.at[k],
                device_id=(x, y, 1 - c), device_id_type=MESH)

        for t in range(n):
            for r in range(3):
                ici(t, r, c).start()
        for t in range(n):
            for r in range(3):
                ici(t, r, c).wait_recv()
                d2d(t, r, c).start()
        for t in range(n):
            for r in range(3):
                d2d(t, r, 1 - c).wait_recv()
        for t in range(n):
            for r in range(3):
                ici(t, r, c).wait_send()
                d2d(t, r, c).wait_send()

    _sequencer(name, stage, 6 * n, lambda x, y, c: [(x, y, 1 - c)] + [(cx, cy, c) for cx, cy in _other_chips(x, y)],
               program)
    return [r[...] for r in refs]


def cast_into_slots(name, chip, shards, after=None):
    steps = 2
    n = len(shards)

    def body(chip_ref, *refs):
        for src, dst in zip(refs[:n], refs[-n - 1:-1]):
            dst[...] = src[...].astype(BF16)
        refs[-1][...] = jnp.zeros_like(refs[-1])

    token_spec = pl.BlockSpec((SUBLANES, LANES), lambda h, i, chip_ref: (0, 0))

    def spec_in(a, prefix):
        R, C = a.shape[-2:]
        return pl.BlockSpec((None,) * (len(prefix) + 1) + (R // steps, C), lambda h, i, chip_ref: prefix + (h, i, 0))

    def spec_out(a):
        R, C = a.shape[-2:]
        return pl.BlockSpec((None, None, R // steps, C), lambda h, i, chip_ref: (chip_ref[0], h, i, 0))

    outs = pl.pallas_call(
        body, name=name,
        grid_spec=pltpu.PrefetchScalarGridSpec(
            num_scalar_prefetch=1, grid=(2, steps),
            in_specs=[spec_in(a, p) for a, p in shards] + ([token_spec] if after is not None else []),
            out_specs=[spec_out(a) for a, _ in shards] + [token_spec]),
        out_shape=[jax.ShapeDtypeStruct((N_CHIP, 2) + a.shape[-2:], BF16) for a, _ in shards]
        + [jax.ShapeDtypeStruct((SUBLANES, LANES), F32)],
        compiler_params=_params("arbitrary", "arbitrary"),
    )(chip, *[a for a, _ in shards], *([after] if after is not None else []))
    return outs[:-1], outs[-1]


def reduce_pair(name, grads):
    n = len(grads)
    src = [_hbm_ref(g) for g in grads]
    dst = [_hbm_empty((N_CHIP,) + g.shape[2:], g.dtype) for g in grads]

    def program(x, y, c, send_sem, recv_sem):
        cps = [pltpu.make_async_remote_copy(
            src_ref=src[t].at[:, 1 - c], dst_ref=dst[t], send_sem=send_sem.at[t], recv_sem=recv_sem.at[t],
            device_id=(x, y, 1 - c), device_id_type=MESH) for t in range(n)]
        for cp in cps:
            cp.start()
        for cp in cps:
            cp.wait()

    _sequencer(name, ID_PAIR, n, lambda x, y, c: [(x, y, 1 - c)], program)
    return [r[...] for r in src], [r[...] for r in dst]


def pair_add(name, core, g, got):
    _, _, R, C = g.shape

    def body(core_ref, g_ref, got_ref, o_ref):
        o_ref[...] = (g_ref[...] + got_ref[...]).astype(BF16)

    return pl.pallas_call(
        body, name=name,
        grid_spec=pltpu.PrefetchScalarGridSpec(
            num_scalar_prefetch=1, grid=(N_CHIP,),
            in_specs=[pl.BlockSpec((None, None, R, C), lambda q, core_ref: (q, core_ref[0], 0, 0)),
                      pl.BlockSpec((None, R, C), lambda q, core_ref: (q, 0, 0))],
            out_specs=pl.BlockSpec((None, R, C), lambda q, core_ref: (q, 0, 0))),
        out_shape=jax.ShapeDtypeStruct((N_CHIP, R, C), BF16),
        compiler_params=_params("parallel"),
    )(core, g, got)


def reduce_chips(name, sums):
    n = len(sums)
    src = [_hbm_ref(s) for s in sums]
    dst = [_hbm_empty((3,) + s.shape[1:], s.dtype) for s in sums]

    def program(x, y, c, send_sem, recv_sem):
        cps = []
        for t in range(n):
            for r, (cx, cy) in enumerate(_other_chips(x, y)):
                cps.append(pltpu.make_async_remote_copy(
                    src_ref=src[t].at[2 * cx + cy], dst_ref=dst[t].at[r],
                    send_sem=send_sem.at[3 * t + r], recv_sem=recv_sem.at[3 * t + r],
                    device_id=(cx, cy, c), device_id_type=MESH))
        for cp in cps:
            cp.start()
        for cp in cps:
            cp.wait()

    _sequencer(name, ID_CHIPS, 3 * n, lambda x, y, c: [(cx, cy, c) for cx, cy in _other_chips(x, y)], program)
    return [r[...] for r in src], [r[...] for r in dst]


def chip_add(name, place, s, got, k, n_slots, prev=None, after=None):
    _, R, C = s.shape

    def body(place_ref, s_ref, got_ref, *rest):
        o_ref = rest[-1]
        o_ref[...] = ((s_ref[...].astype(F32) + got_ref[0].astype(F32)) + got_ref[1].astype(F32)) + got_ref[2].astype(F32)

    in_specs = [pl.BlockSpec((None, R, C), lambda i, place_ref: (place_ref[0], 0, 0)),
                pl.BlockSpec((3, R, C), lambda i, place_ref: (0, 0, 0))]
    args = [place, s, got]
    aliases = {}
    if prev is not None:
        in_specs.append(_ANY)
        args.append(prev)
        aliases = {3: 0}
    for piece in after or ():
        in_specs.append(pl.BlockSpec((SUBLANES, LANES), lambda i, place_ref: (0, 0)))
        args.append(piece)
    return pl.pallas_call(
        body, name=name,
        grid_spec=pltpu.PrefetchScalarGridSpec(
            num_scalar_prefetch=1, grid=(1,), in_specs=in_specs,
            out_specs=pl.BlockSpec((None, None, R, C), lambda i, place_ref: (k, place_ref[1], 0, 0))),
        out_shape=jax.ShapeDtypeStruct((n_slots, 2, R, C), F32),
        input_output_aliases=aliases,
        compiler_params=_params("arbitrary"),
    )(*args)


def share_halves(name, stacks, slots):
    n = len(stacks)
    dst = [_hbm_ref(s) for s in stacks]

    def program(x, y, c, send_sem, recv_sem):
        cps = [pltpu.make_async_remote_copy(
            src_ref=dst[t].at[slots[t], c], dst_ref=dst[t].at[slots[t], c],
            send_sem=send_sem.at[t], recv_sem=recv_sem.at[t],
            device_id=(x, y, 1 - c), device_id_type=MESH) for t in range(n)]
        for cp in cps:
            cp.start()
        for cp in cps:
            cp.wait()

    _sequencer(name, ID_SHARE, n, lambda x, y, c: [(x, y, 1 - c)], program)
    return [r[...] for r in dst]


def gather_blocks(name, slotted):
    out = _hbm_ref(slotted)

    def program(x, y, c, send_sem, recv_sem):
        sibling = (x, y, 1 - c)
        chips = _other_chips(x, y)

        def copy(k, px, py, pc, to):
            block = out.at[4 * px + 2 * py + pc]
            return pltpu.make_async_remote_copy(src_ref=block, dst_ref=block, send_sem=send_sem.at[k],
                                                recv_sem=recv_sem.at[k], device_id=to, device_id_type=MESH)

        first = [copy(0, x, y, c, sibling)] + [copy(1 + j, x, y, c, (cx, cy, c)) for j, (cx, cy) in enumerate(chips)]
        for cp in first:
            cp.start()
        passed = [copy(4 + j, cx, cy, c, sibling) for j, (cx, cy) in enumerate(chips)]
        for j, (cx, cy) in enumerate(chips):
            copy(1 + j, cx, cy, c, (x, y, c)).wait_recv()
            passed[j].start()
        copy(0, x, y, 1 - c, (x, y, c)).wait_recv()
        for j, (cx, cy) in enumerate(chips):
            copy(4 + j, cx, cy, 1 - c, (x, y, c)).wait_recv()
        for cp in first + passed:
            cp.wait_send()

    _sequencer(name, ID_UKV, 7, lambda x, y, c: [(x, y, 1 - c)] + [(cx, cy, c) for cx, cy in _other_chips(x, y)],
               program)
    return out[...]


def place_block(name, dev, a):
    M, N = a.shape
    tr = min(M, 64)

    def body(dev_ref, a_ref, o_ref):
        o_ref[...] = a_ref[...]

    return pl.pallas_call(
        body, name=name,
        grid_spec=pltpu.PrefetchScalarGridSpec(
            num_scalar_prefetch=1, grid=(M // tr,),
            in_specs=[pl.BlockSpec((tr, N), lambda i, dev_ref: (i, 0))],
            out_specs=pl.BlockSpec((None, tr, N), lambda i, dev_ref: (dev_ref[0], i, 0))),
        out_shape=jax.ShapeDtypeStruct((N_DEV, M, N), a.dtype),
        compiler_params=_params("parallel"),
    )(dev, a)


def _swap_rope(a):
    return jnp.concatenate([a[..., ROPE // 2:], a[..., :ROPE // 2]], axis=-1)


def _rope_tables(S):
    inv = 1.0 / (ROPE_THETA ** (jnp.arange(0, ROPE, 2, dtype=F32) / ROPE))
    ang = jnp.arange(S, dtype=F32)[:, None] * inv[None, :]
    cos, sin = jnp.cos(ang), jnp.sin(ang)
    return (jnp.tile(jnp.concatenate([cos, cos], axis=1), (1, SLOTS)),
            jnp.tile(jnp.concatenate([-sin, sin], axis=1), (1, SLOTS)))


def _vec(norm_g, mod, i, k):
    rows = [norm_g[i, 2 * k], norm_g[i, 2 * k + 1], mod[i, 3 * k], mod[i, 3 * k + 1], mod[i, 3 * k + 2]]
    return jnp.concatenate([jnp.stack(rows), jnp.zeros((3, D), F32)], axis=0)


def _unpack_weights(full, w_uk, w_uv, q_norm, kv_norm):
    G = D // 4
    ffn_in = [[full[2 * i + k].reshape(N_CHIP, D, FSH) for k in range(2)] for i in range(2)]
    ffn_out = [[full[4 + 2 * i + k].reshape(2, FSH, D) for k in range(2)] for i in range(2)]
    pw = full[8].reshape(N_CHIP, 4, G // N_CHIP, G).transpose(1, 0, 2, 3).reshape(4, G, G)
    w_in = full[9].reshape(D, QL + KVL + ROPE)
    w_uq = full[10].reshape(QL, N_HEADS, NOPE + ROPE)
    wkr = w_in[:, QL + KVL:]
    wr = w_uq[:, :, NOPE:]
    eye2 = jnp.eye(2, dtype=BF16)
    uk_t = jnp.transpose(w_uk, (1, 2, 0)).reshape(N_PAIR, 2, NOPE, KVL)
    bduk = jnp.einsum("janc,ab->janbc", uk_t, eye2).reshape(N_PAIR, 2 * NOPE, 2 * KVL)
    uv = jnp.transpose(w_uv, (1, 0, 2)).reshape(N_PAIR, 2, KVL, VH)
    bduv = jnp.einsum("jacn,ab->jacbn", uv, eye2).reshape(N_PAIR, 2 * KVL, 2 * VH)
    mw = dict(wq=w_in[:, :QL], wkv=w_in[:, QL:QL + KVL], wkr4=jnp.tile(wkr, (1, SLOTS)),
              wkrs4=jnp.tile(_swap_rope(wkr), (1, SLOTS)), qn=q_norm, kvn=kv_norm,
              wn=w_uq[:, :, :NOPE].reshape(QL, NOPE_ALL), wr=wr.reshape(QL, ROPE_ALL),
              wrs=_swap_rope(wr).reshape(QL, ROPE_ALL), bduk=bduk)
    return ffn_in, ffn_out, pw, mw, bduv, full[11].reshape(D, D)


def _example_step(x, target, mod, norm_g, pvec, ffn_in, ffn_out, pw, mw, bduv, wo, reducer):
    S = x.shape[0]
    tabs = _rope_tables(S)
    vec = [[_vec(norm_g, mod, i, k) for k in range(3)] for i in range(2)]
    saved = {}
    for i in range(2):
        xin = x
        x, a, u, h = ffn_fwd(xin, vec[i][0], ffn_in[i][0], ffn_out[i][0], 0.5)
        saved[i, 0] = (xin, a, u, h)
        xin = x
        if i == 0:
            x, y, z = pool_fwd(xin, vec[i][1], pw, pvec)
            saved[i, 1] = (xin, y, z)
        else:
            h_m, cq_raw, ckv_raw, cqn, qnope, qcat, kcat, vcat = mla_pre(xin, vec[i][1], mw, tabs)
            olat, lse = attn_fwd(qcat, kcat, vcat)
            x, u_m, ocat = mla_post(olat, xin, vec[i][1], bduv, wo)
            saved[i, 1] = (xin, h_m, cq_raw, ckv_raw, cqn, qnope, qcat, kcat, olat, lse, u_m, ocat)
        xin = x
        x, a, u, h = ffn_fwd(xin, vec[i][2], ffn_in[i][1], ffn_out[i][1], 0.5)
        saved[i, 2] = (xin, a, u, h)
    loss, dx = loss_head(x, target)

    vg = {}
    G = D // 4

    def ffn_grads(i, k, dw_in, dw_out):
        return [(0, 2 * i + k, 4, dw_in.reshape(N_CHIP, 2, D // 2, FSH)),
                (1, 2 * i + k, 4, dw_out.reshape(N_CHIP, 2, DFF // 8, D))]

    piece = lambda t: t[:SUBLANES, :LANES]
    for i in (1, 0):
        xin, a, u, h = saved[i, 2]
        dx, du, act, da, vg[i, 2] = ffn_bwd(dx, xin, u, a, vec[i][2], ffn_in[i][1], ffn_out[i][1], 0.5)
        reducer.advance(after=(piece(dx),))
        reducer.add(f"f{i}1", ffn_grads(i, 1, *ffn_dw(h, da, act, du)))
        if i == 0:
            xin, y, z = saved[i, 1]
            dx, vg[i, 1], pgrad, g_pool = pool_bwd(dx, xin, y, z, vec[i][1], pw, pvec)
            reducer.advance(after=(piece(dx),))
        else:
            xin, h_m, cq_raw, ckv_raw, cqn, qnope, qcat, kcat, olat, lse, u_m, ocat = saved[i, 1]
            du, docat, dolat, delta, vg_post = mla_post_bwd(dx, u_m, olat, vec[i][1], bduv, wo)
            reducer.advance()
            dq, dk, dv = attn_bwd(qcat, kcat, kcat.T, dolat, lse, delta)
            reducer.advance(after=(piece(dk),))
            dx, dlat, dql, dqcat, vg_pre, ngrad = mla_pre_bwd(
                dx, dq, dk, dv, xin, cq_raw, ckv_raw, vec[i][1], mw, tabs)
            vg[i, 1] = vg_post + vg_pre
            g = mla_dw(h_m, dlat, cqn, dqcat, dql, qnope, olat, docat, ocat, du)
            slots = lambda a: a.reshape(D, SLOTS, ROPE).sum(axis=1)
            g_kr = slots(g["in"][:, QL + KVL:QL + KVL + LANES]) + _swap_rope(slots(g["in"][:, QL + KVL + LANES:]))
            g_in = jnp.concatenate([g["in"][:, :QL + KVL], g_kr], axis=1)
            g_r = g["q"][:, NOPE_ALL:NOPE_ALL + ROPE_ALL].reshape(QL, N_HEADS, ROPE)
            g_rs = g["q"][:, NOPE_ALL + ROPE_ALL:].reshape(QL, N_HEADS, ROPE)
            g_uq = jnp.concatenate([g["q"][:, :NOPE_ALL].reshape(QL, N_HEADS, NOPE), g_r + _swap_rope(g_rs)], axis=-1)

            def heads(pairs):
                blk = pairs.reshape(N_PAIR, 2, KVL, 2, NOPE)
                per_head = jnp.stack([blk[:, 0, :, 0, :], blk[:, 1, :, 1, :]], axis=1).reshape(N_HEADS, KVL, NOPE)
                return jnp.transpose(per_head, (1, 0, 2)).reshape(KVL, N_HEADS * NOPE)

            reducer.add("mla", [(3, 0, 1, g_in.reshape(N_CHIP, 2, D // 8, QL + KVL + ROPE)),
                                (4, 0, 1, g_uq.reshape(N_CHIP, 2, QL // 8, N_HEADS * (NOPE + ROPE))),
                                (5, 0, 1, g["o"].reshape(N_CHIP, 2, D // 8, D))])
            reducer.add_replicated(jnp.concatenate([heads(g["uk"]), heads(g["uv"])], axis=0))
        xin, a, u, h = saved[i, 0]
        dx, du, act, da, vg[i, 0] = ffn_bwd(dx, xin, u, a, vec[i][0], ffn_in[i][0], ffn_out[i][0], 0.5)
        if i == 1:
            reducer.advance(after=(piece(dx),))
        grads = ffn_grads(i, 0, *ffn_dw(h, da, act, du))
        if i == 0:
            grads.append((2, 0, 1, g_pool.reshape(N_CHIP, 2, 2 * G // N_CHIP, G)))
        reducer.add(f"f{i}0", grads)
    return loss, dx, vg, pgrad, ngrad


class _GradReducer:
    def __init__(self, core, place, dev):
        self.core, self.place, self.dev = core, place, dev
        self.stacks = {}
        self.live = []
        self.replicated = None

    def add(self, tag, items):
        gen = self._run(tag, items)
        next(gen)
        self.live.append(gen)

    def add_replicated(self, block):
        self.replicated = gather_blocks("gather_ukv", place_block("place_ukv", self.dev, block))

    def advance(self, after=None):
        self.after = after
        live = []
        for gen in self.live:
            try:
                next(gen)
                live.append(gen)
            except StopIteration:
                pass
        self.live = live

    def finish(self):
        while self.live:
            self.advance()
        return self.stacks, self.replicated

    def _run(self, tag, items):
        grads, from_pair = reduce_pair(f"reduce_pair_{tag}", [g for *_, g in items])
        yield
        sums = [pair_add(f"pair_add_{tag}_{j}", self.core, g, p) for j, (g, p) in enumerate(zip(grads, from_pair))]
        sums, from_chips = reduce_chips(f"reduce_chips_{tag}", sums)
        yield
        for j, ((o, k, n_slots, _), s, p) in enumerate(zip(items, sums, from_chips)):
            self.stacks[o] = chip_add(f"chip_add_{tag}_{j}", self.place, s, p, k, n_slots, self.stacks.get(o),
                                      self.after)
        shared = share_halves(f"share_halves_{tag}", [self.stacks[o] for o, *_ in items], [k for _, k, *_ in items])
        for (o, *_), v in zip(items, shared):
            self.stacks[o] = v


SMALL_IN = 8 * 640
SMALL_GRAD = 8 * 4224
SMALL_W = 8 * 2944


def _pack(parts, total):
    flat = jnp.concatenate([p.reshape(-1) for p in parts])
    return jnp.concatenate([flat, jnp.zeros((total - flat.shape[0],), F32)]).reshape(8, total // 8)


def kernel(x, c, ada_w, ada_b, norm_g, ffn_w_in, ffn_w_out, pool_w, pool_b, pool_scale, mla_w_in, mla_q_norm, mla_kv_norm, mla_w_uq, mla_w_uk, mla_w_uv, mla_w_o, loss_target, m_ada_w, m_ada_b, m_norm_g, m_ffn_w_in, m_ffn_w_out, m_pool_w, m_pool_b, m_pool_scale, m_mla_w_in, m_mla_q_norm, m_mla_kv_norm, m_mla_w_uq, m_mla_w_uk, m_mla_w_uv, m_mla_w_o, v_ada_w, v_ada_b, v_norm_g, v_ffn_w_in, v_ffn_w_out, v_pool_w, v_pool_b, v_pool_scale, v_mla_w_in, v_mla_q_norm, v_mla_kv_norm, v_mla_w_uq, v_mla_w_uk, v_mla_w_uv, v_mla_w_o):
    ix, iy, ic = _place()
    chip = 2 * ix + iy
    dev = 2 * chip + ic
    core_arr = ic.astype(jnp.int32).reshape(1)
    chip_arr = chip.astype(jnp.int32).reshape(1)
    S = x.shape[1]
    G = D // 4
    NG = D // N_CHIP

    def chip_cols(a, width, axis):
        return lax.dynamic_slice_in_dim(a, chip * width, width, axis)

    got = gather_devices("gather_small_in", _pack([c, norm_g, pool_b, mla_q_norm], SMALL_IN)).reshape(N_DEV, SMALL_IN)
    c_all = got[:, :D]
    parts = got[0::2]
    o = D
    norm_g_full = parts[:, o:o + 12 * NG].reshape(N_CHIP, 2, 6, NG).transpose(1, 2, 0, 3).reshape(2, 6, D)
    o += 12 * NG
    pool_b_full = parts[:, o:o + G].reshape(N_CHIP, 4, G // N_CHIP).transpose(1, 0, 2).reshape(1, D)
    o += G
    q_norm_full = parts[:, o:o + QL // N_CHIP].reshape(1, QL)
    pvec = jnp.concatenate([pool_b_full, pool_scale, jnp.zeros((6, D), F32)], axis=0)

    c_pad = jnp.concatenate([c_all, jnp.zeros((8, D), F32)], axis=0)
    mod_loc = mod_fwd(c_pad, ada_w, chip_cols(ada_b, MOD_COLS, 1).reshape(2, 1, MOD_COLS))
    got = gather_devices("gather_mod", mod_loc[:, :8].transpose(1, 0, 2).reshape(8, 2 * MOD_COLS))
    mine = lax.dynamic_index_in_dim(got[0::2].reshape(N_CHIP, 8, 2, MOD_COLS), dev, axis=1, keepdims=False)
    mod = mine.transpose(1, 0, 2).reshape(2, 9, D)

    bf = lambda a: a.astype(BF16)
    w_in_halves = ffn_w_in.reshape(2, 2, 2, D // 2, FSH)
    w_out_halves = ffn_w_out.reshape(2, 2, 2, DFF // 8, D)
    shards = [(w_in_halves, (i, k)) for i in range(2) for k in range(2)]
    shards += [(w_out_halves, (i, k)) for i in range(2) for k in range(2)]
    shards += [(pool_w.reshape(2, 2 * G // N_CHIP, G), ()), (mla_w_in.reshape(2, D // 8, QL + KVL + ROPE), ()),
               (mla_w_uq.reshape(2, QL // 8, N_HEADS * (NOPE + ROPE)), ()), (mla_w_o.reshape(2, D // 8, D), ())]
    full = [None] * len(shards)
    stages = [(0, 4, 8), (1, 5), (2, 6), (9, 10, 11), (3, 7)]
    first, token = cast_into_slots("cast_first", chip_arr, [shards[t] for t in stages[0]])
    slotted = dict(zip(stages[0], first))
    rest = [t for members in stages[1:] for t in members]
    for stage, members in enumerate(stages):
        got_w = gather_weights(f"gather_weights_{stage}", stage, [slotted[t] for t in members])
        for t, a in zip(members, got_w):
            full[t] = a
        if stage == 0:
            slotted.update(zip(rest, cast_into_slots("cast_rest", chip_arr, [shards[t] for t in rest], token)[0]))
    ffn_in, ffn_out, pw, mw, bduv, wo = _unpack_weights(full, bf(mla_w_uk[0]), bf(mla_w_uv[0]), q_norm_full,
                                                        mla_kv_norm)

    place_arr = jnp.stack([chip, ic]).astype(jnp.int32)
    reducer = _GradReducer(core_arr, place_arr, dev.astype(jnp.int32).reshape(1))
    loss_mine, grad_x, vg, pgrad, ngrad = _example_step(
        x[0], loss_target[0], mod, norm_g_full, pvec, ffn_in, ffn_out, pw, mw, bduv, wo, reducer)

    dmod = jnp.stack([jnp.concatenate([vg[i, k][0:3] for k in range(3)]) for i in range(2)])
    dnorm = jnp.stack([jnp.concatenate([vg[i, k][3:5] for k in range(3)]) for i in range(2)])
    small = _pack([dmod, dnorm, pgrad[0], pgrad[1], ngrad[0], ngrad[1, :KVL], loss_mine], SMALL_GRAD)
    got = gather_devices("gather_small_grad", small)
    tot = sum_devices("sum_small_grad", got).reshape(-1)
    n_mod = 2 * 9 * D
    g_ada_b = tot[:n_mod].reshape(ada_b.shape)
    o = n_mod
    g_norm = chip_cols(tot[o:o + 12 * D].reshape(2, 6, D), NG, 2)
    o += 12 * D
    g_pool_b = chip_cols(tot[o:o + D].reshape(1, 4, G), G // N_CHIP, 2)
    o += D
    g_pool_scale = tot[o:o + D].reshape(pool_scale.shape)
    o += D
    g_q_norm = chip_cols(tot[o:o + QL].reshape(1, QL), QL // N_CHIP, 1)
    o += QL
    g_kv_norm = tot[o:o + KVL].reshape(mla_kv_norm.shape)
    loss = tot[o + KVL]
    dmod_all = chip_cols(got.reshape(N_DEV, -1)[:, :n_mod].reshape(N_DEV, 2, 9 * D), MOD_COLS, 2)
    dmod_pad = jnp.concatenate([dmod_all.transpose(1, 0, 2), jnp.zeros((2, 8, MOD_COLS), F32)], axis=1)

    g_ada_w, d_ada_w, nm_ada_w, nv_ada_w = adamw_ada(c_pad, dmod_pad, ada_w, m_ada_w, v_ada_w)
    small_names = ["ada_b", "norm_g", "pool_b", "pool_scale", "mla_q_norm", "mla_kv_norm"]
    small_w = [ada_b, norm_g, pool_b, pool_scale, mla_q_norm, mla_kv_norm]
    small_g = [g_ada_b, g_norm, g_pool_b, g_pool_scale, g_q_norm, g_kv_norm]
    small_m = [m_ada_b, m_norm_g, m_pool_b, m_pool_scale, m_mla_q_norm, m_mla_kv_norm]
    small_v = [v_ada_b, v_norm_g, v_pool_b, v_pool_scale, v_mla_q_norm, v_mla_kv_norm]
    packed = adamw("adamw_small", *[_pack(p, SMALL_W) for p in (small_w, small_g, small_m, small_v)])
    upd = {}
    o = 0
    for name, w in zip(small_names, small_w):
        upd[name] = [p.reshape(-1)[o:o + w.size].reshape(w.shape) for p in packed]
        o += w.size
    upd["ada_w"] = [d_ada_w, nm_ada_w, nv_ada_w]

    reducer.advance(after=(d_ada_w[0, :SUBLANES, :LANES],))
    ffn = [("ffn_w_in", 0, ffn_w_in, m_ffn_w_in, v_ffn_w_in), ("ffn_w_out", 1, ffn_w_out, m_ffn_w_out, v_ffn_w_out)]
    slots = lambda a: a.reshape((4,) + a.shape[2:])
    early = {name: adamw(f"adamw_{name}_early", slots(w), slots(reducer.stacks[o].reshape(w.shape)), slots(m),
                         slots(v), part=(1, 3), copy_grad=True) for name, o, w, m, v in ffn}
    g_mla_in = reducer.stacks[3].reshape(mla_w_in.shape)
    g_uq = reducer.stacks[4].reshape(mla_w_uq.shape)
    g_wo = reducer.stacks[5].reshape(mla_w_o.shape)
    for name, w, g, m, v in [("mla_w_in", mla_w_in, g_mla_in, m_mla_w_in, v_mla_w_in),
                             ("mla_w_uq", mla_w_uq, g_uq, m_mla_w_uq, v_mla_w_uq),
                             ("mla_w_o", mla_w_o, g_wo, m_mla_w_o, v_mla_w_o)]:
        upd[name] = adamw("adamw_" + name, w, g, m, v)

    reducer.advance(after=(early["ffn_w_in"][0][1, :SUBLANES, :LANES], early["ffn_w_out"][0][1, :SUBLANES, :LANES],
                           upd["mla_w_o"][0][0, :SUBLANES, :LANES], upd["mla_w_in"][0][0, :SUBLANES, :LANES]))
    ukv = sum_devices("sum_ukv", reducer.replicated)
    g_uk = ukv[:KVL].reshape(mla_w_uk.shape)
    g_uv = ukv[KVL:].reshape(mla_w_uv.shape)
    upd["mla_w_uk"] = adamw("adamw_mla_w_uk", mla_w_uk, g_uk, m_mla_w_uk, v_mla_w_uk)
    upd["mla_w_uv"] = adamw("adamw_mla_w_uv", mla_w_uv, g_uv, m_mla_w_uv, v_mla_w_uv)
    stacks, _ = reducer.finish()
    g_pool_w = stacks[2].reshape(pool_w.shape)
    g_ffn = {}
    for name, o, w, m, v in ffn:
        done = adamw(f"adamw_{name}_last", slots(w), slots(stacks[o].reshape(w.shape)), slots(m), slots(v),
                     part=(0, 1), prev=early[name], copy_grad=True)
        upd[name] = [p.reshape(w.shape) for p in done[:3]]
        g_ffn[name] = done[3].reshape(w.shape)
    g_ffn_in, g_ffn_out = g_ffn["ffn_w_in"], g_ffn["ffn_w_out"]
    upd["pool_w"] = adamw("adamw_pool_w", pool_w, g_pool_w, m_pool_w, v_pool_w)

    order = ["ada_w", "ada_b", "norm_g", "ffn_w_in", "ffn_w_out", "pool_w", "pool_b", "pool_scale", "mla_w_in",
             "mla_q_norm", "mla_kv_norm", "mla_w_uq", "mla_w_uk", "mla_w_uv", "mla_w_o"]
    grad = dict(ada_w=g_ada_w, ada_b=g_ada_b, norm_g=g_norm, ffn_w_in=g_ffn_in, ffn_w_out=g_ffn_out, pool_w=g_pool_w,
                pool_b=g_pool_b, pool_scale=g_pool_scale, mla_w_in=g_mla_in, mla_q_norm=g_q_norm,
                mla_kv_norm=g_kv_norm, mla_w_uq=g_uq, mla_w_uk=g_uk, mla_w_uv=g_uv, mla_w_o=g_wo)
    return (loss, grad_x[None], *[grad[n] for n in order], *[upd[n][0] for n in order],
            *[upd[n][1] for n in order], *[upd[n][2] for n in order])
```

```python
import functools

import jax
import jax.numpy as jnp
from jax import lax
from jax.experimental import pallas as pl
from jax.experimental.pallas import tpu as pltpu
from jax.experimental.pallas import tpu_sc as plsc

F32 = jnp.float32
BF16 = jnp.bfloat16

D = 1024
DFF = 2816
FSH = 1408
N_CHIP = 4
N_DEV = 8
N_HEADS = 16
NOPE = 64
ROPE = 32
VH = 64
QL = 256
KVL = 128
LANES = 128
SUBLANES = 8
QPAD = 256
EPS = 1e-6
ATTN_SCALE = (NOPE + ROPE) ** -0.5
ROPE_THETA = 10000.0
POOL_WINDOWS = (2, 4, 8, 16)
HALO = 8
ATTN_TQ = 1024
ATTN_KC = 512
ROW_TILE = 512
DW_TK = 4096

ADAM_LR, ADAM_B1, ADAM_B2, ADAM_EPS, ADAM_WD, ADAM_STEP = 0.001, 0.9, 0.999, 1e-08, 0.01, 10

VMEM_LIMIT = 60 * 1024 * 1024
MESH = pl.DeviceIdType.MESH

NT = (((1,), (1,)), ((), ()))
TN = (((0,), (0,)), ((), ()))


def _params(*sem):
    return pltpu.CompilerParams(dimension_semantics=sem, vmem_limit_bytes=VMEM_LIMIT)


def _dot(a, b, dims=None):
    if dims is None:
        return jnp.dot(a, b, preferred_element_type=F32)
    return lax.dot_general(a, b, dims, preferred_element_type=F32)


def _rms(x):
    r = lax.rsqrt(jnp.mean(x * x, axis=-1, keepdims=True) + EPS)
    return x * r, r


def _rms_bwd(xhat, r, dxhat):
    return r * (dxhat - xhat * jnp.mean(dxhat * xhat, axis=-1, keepdims=True))


def _as_row(col):
    return jnp.broadcast_to(col, (col.shape[0], LANES)).T[0:1, :]


def _prenorm(x, vec_ref):
    xhat, r = _rms(x)
    h = xhat * vec_ref[0:1, :] * (1.0 + vec_ref[3:4, :]) + vec_ref[2:3, :]
    return h, xhat, r


def _postnorm_bwd(dout, u, vec_ref, weight):
    uhat, r = _rms(u)
    gt = weight * (1.0 + vec_ref[4:5, :])
    dy = dout * gt
    dgate_rows = (weight * dout) * (uhat * vec_ref[1:2, :])
    dgpost_rows = dy * uhat
    du = _rms_bwd(uhat, r, dy * vec_ref[1:2, :])
    return du, dgate_rows, dgpost_rows


def _prenorm_bwd(dh, x, vec_ref, vg_ref):
    xhat, r = _rms(x)
    sc1 = 1.0 + vec_ref[3:4, :]
    g = vec_ref[0:1, :]
    vg_ref[0:1, :] += jnp.sum(dh, axis=0, keepdims=True)
    vg_ref[1:2, :] += jnp.sum(dh * (xhat * g), axis=0, keepdims=True)
    vg_ref[3:4, :] += jnp.sum(dh * sc1 * xhat, axis=0, keepdims=True)
    return _rms_bwd(xhat, r, dh * g * sc1)


def ffn_fwd(x, vec, w_in, w_out, weight):
    S = x.shape[0]
    tm = min(512, S)
    row = lambda i: (i, 0)
    half = lambda j: [_w3((8, D)), pl.BlockSpec((None, D, FSH), lambda i: (j, 0, 0)),
                      pl.BlockSpec((None, D, FSH), lambda i: (j + 2, 0, 0)),
                      pl.BlockSpec((None, FSH, D), lambda i: (j, 0, 0))]
    a_spec = lambda j: pl.BlockSpec((2, tm, FSH), lambda i: (0, i, j))
    a_shape = jax.ShapeDtypeStruct((2, S, DFF), BF16)

    def hidden(hb, wg_ref, wu_ref, wo_ref, a_ref):
        g = _dot(hb, wg_ref[...])
        up = _dot(hb, wu_ref[...])
        a_ref[0] = g.astype(BF16)
        a_ref[1] = up.astype(BF16)
        act = (g * jax.nn.sigmoid(g)) * up
        return _dot(act.astype(BF16), wo_ref[...])

    def first(x_ref, vec_ref, wg_ref, wu_ref, wo_ref, h_ref, a_ref, u_ref):
        h, _, _ = _prenorm(x_ref[...], vec_ref)
        hb = h.astype(BF16)
        h_ref[...] = hb
        u_ref[...] = hidden(hb, wg_ref, wu_ref, wo_ref, a_ref)

    h, a, u_half = pl.pallas_call(
        first, name="ffn_fwd_first", grid=(S // tm,),
        in_specs=[pl.BlockSpec((tm, D), row)] + half(0),
        out_specs=[pl.BlockSpec((tm, D), row), a_spec(0), pl.BlockSpec((tm, D), row)],
        out_shape=[jax.ShapeDtypeStruct((S, D), BF16), a_shape, jax.ShapeDtypeStruct((S, D), F32)],
        compiler_params=_params("parallel"),
    )(x, vec, w_in, w_in, w_out)

    def second(x_ref, h_ref, uh_ref, vec_ref, wg_ref, wu_ref, wo_ref, a_in, xo_ref, a_ref, u_ref):
        u = uh_ref[...] + hidden(h_ref[...], wg_ref, wu_ref, wo_ref, a_ref)
        u_ref[...] = u
        uhat, _ = _rms(u)
        xo_ref[...] = x_ref[...] + (weight * (1.0 + vec_ref[4:5, :])) * (uhat * vec_ref[1:2, :])

    xo, a, u = pl.pallas_call(
        second, name="ffn_fwd_second", grid=(S // tm,),
        in_specs=[pl.BlockSpec((tm, D), row), pl.BlockSpec((tm, D), row), pl.BlockSpec((tm, D), row)] + half(1) + [_ANY],
        out_specs=[pl.BlockSpec((tm, D), row), a_spec(1), pl.BlockSpec((tm, D), row)],
        out_shape=[jax.ShapeDtypeStruct((S, D), F32), a_shape, jax.ShapeDtypeStruct((S, D), F32)],
        input_output_aliases={7: 1},
        compiler_params=_params("parallel"),
    )(x, h, u_half, vec, w_in, w_in, w_out, a)
    return xo, a, u, h


def ffn_bwd(dout, x, u, a, vec, w_in, w_out, weight):
    S = x.shape[0]
    tm = min(512, S)
    row = lambda i: (i, 0)
    half = lambda j: [pl.BlockSpec((2, tm, FSH), lambda i: (0, i, j)), _w3((8, D)),
                      pl.BlockSpec((None, D, FSH), lambda i: (j, 0, 0)),
                      pl.BlockSpec((None, D, FSH), lambda i: (j + 2, 0, 0)),
                      pl.BlockSpec((None, FSH, D), lambda i: (j, 0, 0))]
    half_out = lambda j: [pl.BlockSpec((tm, FSH), lambda i: (i, j)), pl.BlockSpec((2, tm, FSH), lambda i: (0, i, j))]
    half_shape = [jax.ShapeDtypeStruct((S, DFF), BF16), jax.ShapeDtypeStruct((2, S, DFF), BF16)]

    def hidden_bwd(du, a_ref, wg_ref, wu_ref, wo_ref, act_ref, da_ref):
        dact = _dot(du, wo_ref[...], NT)
        g = a_ref[0].astype(F32)
        up = a_ref[1].astype(F32)
        s = jax.nn.sigmoid(g)
        silu = g * s
        act_ref[...] = (silu * up).astype(BF16)
        dg = (dact * up * (s * (1.0 + g * (1.0 - s)))).astype(BF16)
        dup = (dact * silu).astype(BF16)
        da_ref[0] = dg
        da_ref[1] = dup
        return _dot(dg, wg_ref[...], NT) + _dot(dup, wu_ref[...], NT)

    def first(do_ref, u_ref, a_ref, vec_ref, wg_ref, wu_ref, wo_ref, du_ref, dh_ref, act_ref, da_ref, vg_ref):
        @pl.when(pl.program_id(0) == 0)
        def _():
            vg_ref[...] = jnp.zeros_like(vg_ref)

        du, dgate_rows, dgpost_rows = _postnorm_bwd(do_ref[...], u_ref[...], vec_ref, weight)
        vg_ref[2:3, :] += jnp.sum(dgate_rows, axis=0, keepdims=True)
        vg_ref[4:5, :] += jnp.sum(dgpost_rows, axis=0, keepdims=True)
        du = du.astype(BF16)
        du_ref[...] = du
        dh_ref[...] = hidden_bwd(du, a_ref, wg_ref, wu_ref, wo_ref, act_ref, da_ref)

    du, dh, act, da, vg_post = pl.pallas_call(
        first, name="ffn_bwd_first", grid=(S // tm,),
        in_specs=[pl.BlockSpec((tm, D), row), pl.BlockSpec((tm, D), row)] + half(0),
        out_specs=[pl.BlockSpec((tm, D), row), pl.BlockSpec((tm, D), row)] + half_out(0) + [_w3((8, D))],
        out_shape=[jax.ShapeDtypeStruct((S, D), BF16), jax.ShapeDtypeStruct((S, D), F32)] + half_shape
        + [jax.ShapeDtypeStruct((8, D), F32)],
        compiler_params=_params("arbitrary"),
    )(dout, u, a, vec, w_in, w_in, w_out)

    def second(do_ref, x_ref, du_ref, dh_ref, a_ref, vec_ref, wg_ref, wu_ref, wo_ref, act_in, da_in,
               dx_ref, act_ref, da_ref, vg_ref):
        @pl.when(pl.program_id(0) == 0)
        def _():
            vg_ref[...] = jnp.zeros_like(vg_ref)

        dh = dh_ref[...] + hidden_bwd(du_ref[...], a_ref, wg_ref, wu_ref, wo_ref, act_ref, da_ref)
        dx_ref[...] = do_ref[...] + _prenorm_bwd(dh, x_ref[...], vec_ref, vg_ref)

    dx, act, da, vg_pre = pl.pallas_call(
        second, name="ffn_bwd_second", grid=(S // tm,),
        in_specs=[pl.BlockSpec((tm, D), row), pl.BlockSpec((tm, D), row), pl.BlockSpec((tm, D), row),
                  pl.BlockSpec((tm, D), row)] + half(1) + [_ANY, _ANY],
        out_specs=[pl.BlockSpec((tm, D), row)] + half_out(1) + [_w3((8, D))],
        out_shape=[jax.ShapeDtypeStruct((S, D), F32)] + half_shape + [jax.ShapeDtypeStruct((8, D), F32)],
        input_output_aliases={9: 1, 10: 2},
        compiler_params=_params("arbitrary"),
    )(dout, x, du, dh, a, vec, w_in, w_in, w_out, act, da)
    return dx, du, act, da, vg_post + vg_pre


def dw_matmul(name, a, b, a_spec, b_spec, out_shape, out_spec, grid):
    def body(a_ref, b_ref, o_ref):
        @pl.when(pl.program_id(len(grid) - 1) == 0)
        def _():
            o_ref[...] = jnp.zeros_like(o_ref)

        o_ref[...] += _dot(a_ref[...], b_ref[...], TN)

    return pl.pallas_call(
        body, name=name, grid=grid, in_specs=[a_spec, b_spec], out_specs=out_spec,
        out_shape=jax.ShapeDtypeStruct(out_shape, F32),
        compiler_params=_params(*(["parallel"] * (len(grid) - 1) + ["arbitrary"])),
    )(a, b)


def ffn_dw(h, da, act, du):
    S = h.shape[0]
    tk = min(DW_TK, S)
    dw_in = dw_matmul("ffn_dw_in", h, da,
                      pl.BlockSpec((tk, D), lambda n, k: (k, 0)),
                      pl.BlockSpec((None, tk, FSH), lambda n, k: (n // 2, k, n % 2)),
                      (N_CHIP, D, FSH), pl.BlockSpec((None, D, FSH), lambda n, k: (n, 0, 0)),
                      (N_CHIP, S // tk))
    dw_out = dw_matmul("ffn_dw_out", act, du,
                       pl.BlockSpec((tk, FSH), lambda n, k: (k, n)),
                       pl.BlockSpec((tk, D), lambda n, k: (k, 0)),
                       (DFF, D), pl.BlockSpec((FSH, D), lambda n, k: (n, 0)),
                       (2, S // tk))
    return dw_in, dw_out


def _halo_specs(tm, S):
    nb = tm // HALO
    last = S // HALO - 1
    return [pl.BlockSpec((HALO, D), lambda i: (jnp.maximum(i * nb - 1, 0), 0)),
            pl.BlockSpec((tm, D), lambda i: (i, 0)),
            pl.BlockSpec((HALO, D), lambda i: (jnp.minimum((i + 1) * nb, last), 0))]


def _shift_rows(v, k):
    return pltpu.roll(v, k % v.shape[0], 0)


def _window_sum(v, g, forward):
    acc = v + _shift_rows(v, 1 if forward else -1)
    for step in (1, 2, 4)[:g]:
        acc = _shift_rows(acc, step) + _shift_rows(acc, -step)
    return acc


def _pool_count(t, w, S):
    return jnp.maximum(jnp.minimum(t + w // 2, S) - jnp.maximum(t - w // 2, 0), 1).astype(F32)


def pool_fwd(x, vec, pw, pvec):
    S = x.shape[0]
    tm = min(ROW_TILE, S)
    G = D // 4

    def body(xp_ref, x_ref, xn_ref, vec_ref, pw_ref, pv_ref, xo_ref, y_ref, z_ref):
        i = pl.program_id(0)
        xa = jnp.concatenate([xp_ref[...], x_ref[...], xn_ref[...]], axis=0)
        t = i * tm - HALO + lax.broadcasted_iota(jnp.int32, (tm + 2 * HALO, 1), 0)
        h, _, _ = _prenorm(xa, vec_ref)
        h = jnp.where((t >= 0) & (t < S), h, 0.0)
        tmain = t[HALO:HALO + tm]
        for g in range(4):
            hg = h[:, g * G:(g + 1) * G]
            pooled = _window_sum(hg, g, True)[HALO:HALO + tm] / _pool_count(tmain, POOL_WINDOWS[g], S)
            z = (pooled - hg[HALO:HALO + tm]).astype(BF16)
            z_ref[:, g * G:(g + 1) * G] = z
            y_ref[:, g * G:(g + 1) * G] = _dot(z, pw_ref[g]) + pv_ref[0:1, g * G:(g + 1) * G]
        u = y_ref[...] * pv_ref[1:2, :]
        uhat, _ = _rms(u)
        xo_ref[...] = x_ref[...] + (1.0 + vec_ref[4:5, :]) * (uhat * vec_ref[1:2, :])

    row = lambda i: (i, 0)
    full = lambda i: (0, 0)
    return pl.pallas_call(
        body, name="pool_fwd", grid=(S // tm,),
        in_specs=_halo_specs(tm, S) + [pl.BlockSpec((8, D), full), pl.BlockSpec((4, G, G), lambda i: (0, 0, 0)),
                                       pl.BlockSpec((8, D), full)],
        out_specs=[pl.BlockSpec((tm, D), row)] * 3,
        out_shape=[jax.ShapeDtypeStruct((S, D), F32), jax.ShapeDtypeStruct((S, D), F32),
                   jax.ShapeDtypeStruct((S, D), BF16)],
        compiler_params=_params("parallel"),
    )(x, x, x, vec, pw, pvec)


def pool_bwd(dout, x, y, z, vec, pw, pvec):
    S = x.shape[0]
    tm = min(ROW_TILE, S)
    G = D // 4
    R = G // N_CHIP

    def body(dop_ref, do_ref, don_ref, yp_ref, y_ref, yn_ref, x_ref, z_ref, vec_ref, pw_ref, pv_ref,
             dx_ref, vg_ref, pg_ref, dw_ref, dh_ref):
        i = pl.program_id(0)

        @pl.when(i == 0)
        def _():
            vg_ref[...] = jnp.zeros_like(vg_ref)
            pg_ref[...] = jnp.zeros_like(pg_ref)
            dw_ref[...] = jnp.zeros_like(dw_ref)

        doa = jnp.concatenate([dop_ref[...], do_ref[...], don_ref[...]], axis=0)
        ya = jnp.concatenate([yp_ref[...], y_ref[...], yn_ref[...]], axis=0)
        t = i * tm - HALO + lax.broadcasted_iota(jnp.int32, (tm + 2 * HALO, 1), 0)
        inside = (t >= 0) & (t < S)
        main = (t >= i * tm) & (t < (i + 1) * tm)
        du, dgate_rows, dgpost_rows = _postnorm_bwd(doa, ya * pv_ref[1:2, :], vec_ref, 1.0)
        du = jnp.where(inside, du, 0.0)
        vg_ref[2:3, :] += jnp.sum(jnp.where(main, dgate_rows, 0.0), axis=0, keepdims=True)
        vg_ref[4:5, :] += jnp.sum(jnp.where(main, dgpost_rows, 0.0), axis=0, keepdims=True)
        dy = du * pv_ref[1:2, :]
        pg_ref[0:1, :] += jnp.sum(jnp.where(main, dy, 0.0), axis=0, keepdims=True)
        pg_ref[1:2, :] += jnp.sum(jnp.where(main, du * ya, 0.0), axis=0, keepdims=True)
        for g in range(4):
            dyg = dy[:, g * G:(g + 1) * G].astype(BF16)
            dz = _dot(dyg, pw_ref[g], NT)
            e = dz / _pool_count(t, POOL_WINDOWS[g], S)
            dh_ref[:, g * G:(g + 1) * G] = (_window_sum(e, g, False) - dz)[HALO:HALO + tm]
            dwg = _dot(z_ref[:, g * G:(g + 1) * G], dyg[HALO:HALO + tm], TN)
            for q in range(N_CHIP):
                dw_ref[q, g] += dwg[q * R:(q + 1) * R, :]
        dx_ref[...] = do_ref[...] + _prenorm_bwd(dh_ref[...], x_ref[...], vec_ref, vg_ref)

    row = lambda i: (i, 0)
    full = lambda i: (0, 0)
    halo = _halo_specs(tm, S)
    return pl.pallas_call(
        body, name="pool_bwd", grid=(S // tm,),
        in_specs=halo + halo + [pl.BlockSpec((tm, D), row), pl.BlockSpec((tm, D), row), pl.BlockSpec((8, D), full),
                                pl.BlockSpec((4, G, G), lambda i: (0, 0, 0)), pl.BlockSpec((8, D), full)],
        out_specs=[pl.BlockSpec((tm, D), row), pl.BlockSpec((8, D), full), pl.BlockSpec((8, D), full),
                   pl.BlockSpec((N_CHIP, 4, R, G), lambda i: (0, 0, 0, 0))],
        out_shape=[jax.ShapeDtypeStruct((S, D), F32), jax.ShapeDtypeStruct((8, D), F32),
                   jax.ShapeDtypeStruct((8, D), F32), jax.ShapeDtypeStruct((N_CHIP, 4, R, G), F32)],
        scratch_shapes=[pltpu.VMEM((tm, D), F32)],
        compiler_params=_params("arbitrary"),
    )(dout, dout, dout, y, y, y, x, z, vec, pw, pvec)


N_PAIR = N_HEADS // 2
SLOTS = LANES // ROPE
ROPE_ALL = N_HEADS * ROPE
NOPE_ALL = N_HEADS * NOPE
LAT_ALL = N_HEADS * KVL
DLAT = QL + KVL + 2 * LANES
DQ_ALL = NOPE_ALL + 2 * ROPE_ALL


def _w3(shape):
    return pl.BlockSpec(shape, lambda i: (0,) * len(shape))


def _slot_mask(hd, rows):
    lane = lax.broadcasted_iota(jnp.int32, (rows, LANES), 1)
    return (lane // ROPE) == (hd % SLOTS)


MLA_WEIGHTS = ("wq", "wkv", "wkr4", "wkrs4", "qn", "kvn", "wn", "wr", "wrs", "bduk")


def _mla_weight_specs():
    return [_w3((D, QL)), _w3((D, KVL)), _w3((D, LANES)), _w3((D, LANES)), _w3((1, QL)), _w3((1, KVL)),
            _w3((QL, NOPE_ALL)), _w3((QL, ROPE_ALL)), _w3((QL, ROPE_ALL)), _w3((N_PAIR, 2 * NOPE, 2 * KVL))]


def mla_pre(x, vec, mw, tabs):
    S = x.shape[0]
    tm = min(ROW_TILE, S)

    def body(x_ref, vec_ref, cos_ref, sin_ref, wq_ref, wkv_ref, wkr_ref, wkrs_ref, qn_ref, kvn_ref,
             wn_ref, wr_ref, wrs_ref, bduk_ref,
             h_ref, cq_ref, ckv_ref, cqn_ref, qnope_ref, qcat_ref, kcat_ref, vcat_ref):
        h, _, _ = _prenorm(x_ref[...], vec_ref)
        hb = h.astype(BF16)
        h_ref[...] = hb
        cq_raw = _dot(hb, wq_ref[...])
        ckv_raw = _dot(hb, wkv_ref[...])
        cq_ref[...] = cq_raw
        ckv_ref[...] = ckv_raw
        cos, sin = cos_ref[...], sin_ref[...]
        ckv = (_rms(ckv_raw)[0] * kvn_ref[...]).astype(BF16)
        kcat_ref[:, 0:KVL] = ckv
        kcat_ref[:, KVL:] = (_dot(hb, wkr_ref[...]) * cos + _dot(hb, wkrs_ref[...]) * sin).astype(BF16)
        vcat_ref[:, 0:KVL] = ckv
        ones = lax.broadcasted_iota(jnp.int32, (tm, QPAD - KVL), 1) == 0
        vcat_ref[:, KVL:] = jnp.where(ones, 1.0, 0.0).astype(BF16)
        cqb = (_rms(cq_raw)[0] * qn_ref[...]).astype(BF16)
        cqn_ref[...] = cqb
        qn = _dot(cqb, wn_ref[...]).astype(BF16)
        qnope_ref[...] = qn
        cos4, sin4 = jnp.tile(cos, (1, SLOTS)), jnp.tile(sin, (1, SLOTS))
        qr = ((_dot(cqb, wr_ref[...]) * cos4 + _dot(cqb, wrs_ref[...]) * sin4) * ATTN_SCALE).astype(BF16)
        for j in range(N_PAIR):
            ql = (_dot(qn[:, 2 * NOPE * j:2 * NOPE * (j + 1)], bduk_ref[j]) * ATTN_SCALE).astype(BF16)
            for hd in (2 * j, 2 * j + 1):
                qcat_ref[hd, :, 0:KVL] = ql[:, KVL * (hd - 2 * j):KVL * (hd - 2 * j + 1)]
                group = qr[:, LANES * (hd // SLOTS):LANES * (hd // SLOTS + 1)]
                qcat_ref[hd, :, KVL:] = jnp.where(_slot_mask(hd, tm), group, jnp.zeros_like(group))

    row = lambda i: (i, 0)
    hrow = lambda i: (0, i, 0)
    return pl.pallas_call(
        body, name="mla_pre", grid=(S // tm,),
        in_specs=[pl.BlockSpec((tm, D), row), _w3((8, D)), pl.BlockSpec((tm, LANES), row), pl.BlockSpec((tm, LANES), row)]
        + _mla_weight_specs(),
        out_specs=[pl.BlockSpec((tm, D), row), pl.BlockSpec((tm, QL), row), pl.BlockSpec((tm, KVL), row),
                   pl.BlockSpec((tm, QL), row), pl.BlockSpec((tm, NOPE_ALL), row),
                   pl.BlockSpec((N_HEADS, tm, QPAD), hrow), pl.BlockSpec((tm, QPAD), row),
                   pl.BlockSpec((tm, QPAD), row)],
        out_shape=[jax.ShapeDtypeStruct((S, D), BF16), jax.ShapeDtypeStruct((S, QL), F32),
                   jax.ShapeDtypeStruct((S, KVL), F32), jax.ShapeDtypeStruct((S, QL), BF16),
                   jax.ShapeDtypeStruct((S, NOPE_ALL), BF16), jax.ShapeDtypeStruct((N_HEADS, S, QPAD), BF16),
                   jax.ShapeDtypeStruct((S, QPAD), BF16), jax.ShapeDtypeStruct((S, QPAD), BF16)],
        compiler_params=_params("parallel"),
    )(x, vec, tabs[0], tabs[1], *[mw[k] for k in MLA_WEIGHTS])


def attn_fwd(qcat, kcat, vcat):
    S = kcat.shape[0]
    tq = min(ATTN_TQ, S)
    kc = min(ATTN_KC, S)

    def body(q_ref, k_ref, v_ref, o_ref, lse_ref):
        q = q_ref[...]
        m = jnp.full((tq, 1), -jnp.inf, F32)
        ov = jnp.zeros((tq, QPAD), F32)
        for c in range(S // kc):
            s = _dot(q, k_ref[c * kc:(c + 1) * kc, :], NT)
            m_new = jnp.maximum(m, jnp.max(s, axis=-1, keepdims=True))
            p = jnp.exp(s - m_new).astype(BF16)
            ov = ov * jnp.exp(m - m_new) + _dot(p, v_ref[c * kc:(c + 1) * kc, :])
            m = m_new
        l = ov[:, KVL:KVL + 1]
        o_ref[...] = (ov[:, 0:KVL] * (1.0 / l)).astype(BF16)
        lse_ref[...] = _as_row(m + jnp.log(l))

    return pl.pallas_call(
        body, name="attn_fwd", grid=(N_HEADS, S // tq),
        in_specs=[pl.BlockSpec((None, tq, QPAD), lambda h, i: (h, i, 0)),
                  pl.BlockSpec((S, QPAD), lambda h, i: (0, 0)),
                  pl.BlockSpec((S, QPAD), lambda h, i: (0, 0))],
        out_specs=[pl.BlockSpec((tq, KVL), lambda h, i: (i, h)),
                   pl.BlockSpec((None, 1, tq), lambda h, i: (h, 0, i))],
        out_shape=[jax.ShapeDtypeStruct((S, LAT_ALL), BF16), jax.ShapeDtypeStruct((N_HEADS, 1, S), F32)],
        compiler_params=_params("parallel", "parallel"),
    )(qcat, kcat, vcat)


def mla_post(olat, x, vec, bduv, wo):
    S = x.shape[0]
    tm = min(ROW_TILE, S)

    def body(o_ref, x_ref, vec_ref, bduv_ref, wo_ref, xo_ref, u_ref, ocat_ref):
        for j in range(N_PAIR):
            oc = _dot(o_ref[:, 2 * KVL * j:2 * KVL * (j + 1)], bduv_ref[j])
            ocat_ref[:, 2 * VH * j:2 * VH * (j + 1)] = oc.astype(BF16)
        u = _dot(ocat_ref[...], wo_ref[...])
        u_ref[...] = u
        uhat, _ = _rms(u)
        xo_ref[...] = x_ref[...] + (1.0 + vec_ref[4:5, :]) * (uhat * vec_ref[1:2, :])

    row = lambda i: (i, 0)
    return pl.pallas_call(
        body, name="mla_post", grid=(S // tm,),
        in_specs=[pl.BlockSpec((tm, LAT_ALL), row), pl.BlockSpec((tm, D), row), _w3((8, D)),
                  _w3((N_PAIR, 2 * KVL, 2 * VH)), _w3((D, D))],
        out_specs=[pl.BlockSpec((tm, D), row), pl.BlockSpec((tm, D), row), pl.BlockSpec((tm, D), row)],
        out_shape=[jax.ShapeDtypeStruct((S, D), F32), jax.ShapeDtypeStruct((S, D), F32),
                   jax.ShapeDtypeStruct((S, D), BF16)],
        compiler_params=_params("parallel"),
    )(olat, x, vec, bduv, wo)


def mla_post_bwd(dout, u, olat, vec, bduv, wo):
    S = u.shape[0]
    tm = min(ROW_TILE, S)

    def body(do_ref, u_ref, o_ref, vec_ref, bduv_ref, wo_ref, du_ref, docat_ref, dolat_ref, delta_ref, vg_ref):
        @pl.when(pl.program_id(0) == 0)
        def _():
            vg_ref[...] = jnp.zeros_like(vg_ref)

        du, dgate_rows, dgpost_rows = _postnorm_bwd(do_ref[...], u_ref[...], vec_ref, 1.0)
        vg_ref[2:3, :] += jnp.sum(dgate_rows, axis=0, keepdims=True)
        vg_ref[4:5, :] += jnp.sum(dgpost_rows, axis=0, keepdims=True)
        dub = du.astype(BF16)
        du_ref[...] = dub
        docat_ref[...] = _dot(dub, wo_ref[...], NT).astype(BF16)
        for j in range(N_PAIR):
            dol = _dot(docat_ref[:, 2 * VH * j:2 * VH * (j + 1)], bduv_ref[j], NT).astype(BF16)
            dolat_ref[:, 2 * KVL * j:2 * KVL * (j + 1)] = dol
            prod = dol.astype(F32) * o_ref[:, 2 * KVL * j:2 * KVL * (j + 1)].astype(F32)
            delta_ref[2 * j] = _as_row(jnp.sum(prod[:, 0:KVL], axis=-1, keepdims=True))
            delta_ref[2 * j + 1] = _as_row(jnp.sum(prod[:, KVL:], axis=-1, keepdims=True))

    row = lambda i: (i, 0)
    hrow = lambda i: (0, i, 0)
    return pl.pallas_call(
        body, name="mla_post_bwd", grid=(S // tm,),
        in_specs=[pl.BlockSpec((tm, D), row), pl.BlockSpec((tm, D), row), pl.BlockSpec((tm, LAT_ALL), row),
                  _w3((8, D)), _w3((N_PAIR, 2 * KVL, 2 * VH)), _w3((D, D))],
        out_specs=[pl.BlockSpec((tm, D), row), pl.BlockSpec((tm, D), row),
                   pl.BlockSpec((tm, LAT_ALL), row), pl.BlockSpec((N_HEADS, 1, tm), lambda i: (0, 0, i)), _w3((8, D))],
        out_shape=[jax.ShapeDtypeStruct((S, D), BF16), jax.ShapeDtypeStruct((S, D), BF16),
                   jax.ShapeDtypeStruct((S, LAT_ALL), BF16), jax.ShapeDtypeStruct((N_HEADS, 1, S), F32),
                   jax.ShapeDtypeStruct((8, D), F32)],
        compiler_params=_params("arbitrary"),
    )(dout, u, olat, vec, bduv, wo)


def attn_bwd(qcat, kcat, kcat_t, dolat, lse_row, delta_row):
    S = kcat.shape[0]
    tq = min(ATTN_TQ, S)
    kc = min(ATTN_KC, S)

    def body(q_ref, k_ref, kt_ref, do_ref, lse_ref, dl_ref, dq_ref, dk_ref, dv_ref):
        @pl.when((pl.program_id(0) == 0) & (pl.program_id(1) == 0))
        def _():
            dk_ref[...] = jnp.zeros_like(dk_ref)
            dv_ref[...] = jnp.zeros_like(dv_ref)

        q, do = q_ref[...], do_ref[...]
        lse, dl = lse_ref[...], dl_ref[...]
        dqt = jnp.zeros((QPAD, tq), F32)
        for c in range(S // kc):
            rows = slice(c * kc, (c + 1) * kc)
            st = _dot(k_ref[rows, :], q, NT)
            pt = jnp.exp(st - lse)
            dpt = _dot(k_ref[rows, 0:KVL], do, NT)
            dst = (pt * (dpt - dl)).astype(BF16)
            dv_ref[rows, :] += _dot(pt.astype(BF16), do)
            dk_ref[rows, :] += _dot(dst, q)
            dqt = dqt + _dot(kt_ref[:, rows], dst)
        dq_ref[...] = (dqt.T * ATTN_SCALE).astype(BF16)

    return pl.pallas_call(
        body, name="attn_bwd", grid=(N_HEADS, S // tq),
        in_specs=[pl.BlockSpec((None, tq, QPAD), lambda h, i: (h, i, 0)),
                  pl.BlockSpec((S, QPAD), lambda h, i: (0, 0)),
                  pl.BlockSpec((QPAD, S), lambda h, i: (0, 0)),
                  pl.BlockSpec((tq, KVL), lambda h, i: (i, h)),
                  pl.BlockSpec((None, 1, tq), lambda h, i: (h, 0, i)),
                  pl.BlockSpec((None, 1, tq), lambda h, i: (h, 0, i))],
        out_specs=[pl.BlockSpec((None, tq, QPAD), lambda h, i: (h, i, 0)),
                   pl.BlockSpec((S, QPAD), lambda h, i: (0, 0)),
                   pl.BlockSpec((S, KVL), lambda h, i: (0, 0))],
        out_shape=[jax.ShapeDtypeStruct((N_HEADS, S, QPAD), BF16), jax.ShapeDtypeStruct((S, QPAD), F32),
                   jax.ShapeDtypeStruct((S, KVL), F32)],
        compiler_params=_params("arbitrary", "arbitrary"),
    )(qcat, kcat, kcat_t, dolat, lse_row, delta_row)


def mla_pre_bwd(dout, dq, dk, dv, x, cq_raw, ckv_raw, vec, mw, tabs):
    S = x.shape[0]
    tm = min(ROW_TILE, S)

    def body(do_ref, dq_ref, dk_ref, dv_ref, x_ref, cq_ref, ckv_ref, vec_ref, cos_ref, sin_ref,
             wq_ref, wkv_ref, wkr_ref, wkrs_ref, qn_ref, kvn_ref, wn_ref, wr_ref, wrs_ref, bduk_ref,
             dx_ref, dlat_ref, dql_ref, dqcat_ref, vg_ref, ng_ref):
        @pl.when(pl.program_id(0) == 0)
        def _():
            vg_ref[...] = jnp.zeros_like(vg_ref)
            ng_ref[...] = jnp.zeros_like(ng_ref)

        cos, sin = cos_ref[...], sin_ref[...]
        for j in range(N_PAIR):
            dql = jnp.concatenate([dq_ref[2 * j, :, 0:KVL], dq_ref[2 * j + 1, :, 0:KVL]], axis=1)
            dql_ref[:, 2 * KVL * j:2 * KVL * (j + 1)] = dql
            dqcat_ref[:, 2 * NOPE * j:2 * NOPE * (j + 1)] = _dot(dql, bduk_ref[j], NT).astype(BF16)
        groups = []
        for grp in range(N_HEADS // SLOTS):
            acc = jnp.zeros((tm, LANES), F32)
            for hd in range(SLOTS * grp, SLOTS * (grp + 1)):
                acc = acc + jnp.where(_slot_mask(hd, tm), dq_ref[hd, :, KVL:].astype(F32), 0.0)
            groups.append(acc)
        dqr = jnp.concatenate(groups, axis=1)
        qa = (dqr * jnp.tile(cos, (1, SLOTS))).astype(BF16)
        qb = (dqr * jnp.tile(sin, (1, SLOTS))).astype(BF16)
        dqcat_ref[:, NOPE_ALL:NOPE_ALL + ROPE_ALL] = qa
        dqcat_ref[:, NOPE_ALL + ROPE_ALL:] = qb
        dcq = _dot(dqcat_ref[:, 0:NOPE_ALL], wn_ref[...], NT) + _dot(qa, wr_ref[...], NT) + _dot(qb, wrs_ref[...], NT)
        cqh, rq = _rms(cq_ref[...])
        ng_ref[0:1, :] += jnp.sum(dcq * cqh, axis=0, keepdims=True)
        dcq_raw = _rms_bwd(cqh, rq, dcq * qn_ref[...]).astype(BF16)
        dckv = dk_ref[:, 0:KVL] + dv_ref[...]
        ckvh, rk = _rms(ckv_ref[...])
        ng_ref[1:2, 0:KVL] += jnp.sum(dckv * ckvh, axis=0, keepdims=True)
        dckv_raw = _rms_bwd(ckvh, rk, dckv * kvn_ref[...]).astype(BF16)
        dkr = dk_ref[:, KVL:]
        ka = (dkr * cos).astype(BF16)
        kb = (dkr * sin).astype(BF16)
        dlat_ref[:, 0:QL] = dcq_raw
        dlat_ref[:, QL:QL + KVL] = dckv_raw
        dlat_ref[:, QL + KVL:QL + KVL + LANES] = ka
        dlat_ref[:, QL + KVL + LANES:] = kb
        dh = (_dot(dcq_raw, wq_ref[...], NT) + _dot(dckv_raw, wkv_ref[...], NT)
              + _dot(ka, wkr_ref[...], NT) + _dot(kb, wkrs_ref[...], NT))
        dx_ref[...] = do_ref[...] + _prenorm_bwd(dh, x_ref[...], vec_ref, vg_ref)

    row = lambda i: (i, 0)
    hrow = lambda i: (0, i, 0)
    return pl.pallas_call(
        body, name="mla_pre_bwd", grid=(S // tm,),
        in_specs=[pl.BlockSpec((tm, D), row), pl.BlockSpec((N_HEADS, tm, QPAD), hrow), pl.BlockSpec((tm, QPAD), row),
                  pl.BlockSpec((tm, KVL), row), pl.BlockSpec((tm, D), row), pl.BlockSpec((tm, QL), row),
                  pl.BlockSpec((tm, KVL), row), _w3((8, D)), pl.BlockSpec((tm, LANES), row), pl.BlockSpec((tm, LANES), row)]
        + _mla_weight_specs(),
        out_specs=[pl.BlockSpec((tm, D), row), pl.BlockSpec((tm, DLAT), row), pl.BlockSpec((tm, LAT_ALL), row),
                   pl.BlockSpec((tm, DQ_ALL), row), _w3((8, D)), _w3((8, QL))],
        out_shape=[jax.ShapeDtypeStruct((S, D), F32), jax.ShapeDtypeStruct((S, DLAT), BF16),
                   jax.ShapeDtypeStruct((S, LAT_ALL), BF16), jax.ShapeDtypeStruct((S, DQ_ALL), BF16),
                   jax.ShapeDtypeStruct((8, D), F32), jax.ShapeDtypeStruct((8, QL), F32)],
        compiler_params=_params("arbitrary"),
    )(dout, dq, dk, dv, x, cq_raw, ckv_raw, vec, tabs[0], tabs[1], *[mw[k] for k in MLA_WEIGHTS])


def mla_dw(h, dlat, cqn, dqcat, dql, qnope, olat, docat, ocat, du):
    S = h.shape[0]
    tk = min(DW_TK, S)
    nk = S // tk
    flat = lambda w: pl.BlockSpec((tk, w), lambda k: (k, 0))
    cols = lambda w: pl.BlockSpec((tk, w), lambda n, k: (k, n))
    pair_o = pl.BlockSpec((None, 2 * KVL, 2 * NOPE), lambda n, k: (n, 0, 0))
    g = {}
    g["in"] = dw_matmul("mla_dw_in", h, dlat, flat(D), flat(DLAT), (D, DLAT),
                        pl.BlockSpec((D, DLAT), lambda k: (0, 0)), (nk,))
    g["q"] = dw_matmul("mla_dw_q", cqn, dqcat, flat(QL), flat(DQ_ALL), (QL, DQ_ALL),
                       pl.BlockSpec((QL, DQ_ALL), lambda k: (0, 0)), (nk,))
    g["uk"] = dw_matmul("mla_dw_uk", dql, qnope, cols(2 * KVL), cols(2 * NOPE), (N_PAIR, 2 * KVL, 2 * NOPE), pair_o,
                        (N_PAIR, nk))
    g["uv"] = dw_matmul("mla_dw_uv", olat, docat, cols(2 * KVL), cols(2 * VH), (N_PAIR, 2 * KVL, 2 * VH), pair_o,
                        (N_PAIR, nk))
    g["o"] = dw_matmul("mla_dw_o", ocat, du, cols(256), pl.BlockSpec((tk, D), lambda n, k: (k, 0)), (D, D),
                       pl.BlockSpec((256, D), lambda n, k: (n, 0)), (D // 256, nk))
    return g


def loss_head(y, target):
    S = y.shape[0]
    tm = min(2 * ROW_TILE, S)

    def body(y_ref, t_ref, loss_ref, dy_ref):
        @pl.when(pl.program_id(0) == 0)
        def _():
            loss_ref[...] = jnp.zeros_like(loss_ref)

        err = y_ref[...] - t_ref[...]
        dy_ref[...] = err * (1.0 / D)
        loss_ref[...] += 0.5 * jnp.sum(jnp.mean(err * err, axis=-1, keepdims=True), axis=0, keepdims=True)

    row = lambda i: (i, 0)
    return pl.pallas_call(
        body, name="loss_head", grid=(S // tm,),
        in_specs=[pl.BlockSpec((tm, D), row), pl.BlockSpec((tm, D), row)],
        out_specs=[pl.BlockSpec((1, 1), lambda i: (0, 0)), pl.BlockSpec((tm, D), row)],
        out_shape=[jax.ShapeDtypeStruct((1, 1), F32), jax.ShapeDtypeStruct((S, D), F32)],
        compiler_params=_params("arbitrary"),
    )(y, target)


MOD_COLS = 9 * D // N_CHIP


def mod_fwd(c_pad, ada_w, ada_b_loc):
    tn = MOD_COLS // 3

    def body(c_ref, w_ref, b_ref, o_ref):
        c = c_ref[...]
        sc = (c * jax.nn.sigmoid(c)).astype(BF16)
        o_ref[...] = _dot(sc, w_ref[...].astype(BF16)) + b_ref[...]

    return pl.pallas_call(
        body, name="mod_fwd", grid=(2, 3),
        in_specs=[pl.BlockSpec((16, D), lambda i, n: (0, 0)), pl.BlockSpec((None, D, tn), lambda i, n: (i, 0, n)),
                  pl.BlockSpec((None, 1, tn), lambda i, n: (i, 0, n))],
        out_specs=pl.BlockSpec((None, 16, tn), lambda i, n: (i, 0, n)),
        out_shape=jax.ShapeDtypeStruct((2, 16, MOD_COLS), F32),
        compiler_params=_params("parallel", "parallel"),
    )(c_pad, ada_w, ada_b_loc)


def _adamw_math(w, g, m, v):
    m = ADAM_B1 * m + (1.0 - ADAM_B1) * g
    v = ADAM_B2 * v + (1.0 - ADAM_B2) * (g * g)
    m_hat = m / (1.0 - ADAM_B1 ** ADAM_STEP)
    v_hat = v / (1.0 - ADAM_B2 ** ADAM_STEP)
    delta = -ADAM_LR * (m_hat / (jnp.sqrt(v_hat) + ADAM_EPS) + ADAM_WD * w)
    return delta, m, v


def adamw(name, w, g, m, v, part=None, prev=None, copy_grad=False):
    shape = w.shape
    if part is None and w.size * 4 <= (1 << 20):
        whole = pl.BlockSpec(shape, lambda i: (0,) * len(shape))

        def small_body(w_ref, g_ref, m_ref, v_ref, d_ref, mo_ref, vo_ref):
            d_ref[...], mo_ref[...], vo_ref[...] = _adamw_math(w_ref[...], g_ref[...], m_ref[...], v_ref[...])

        return pl.pallas_call(
            small_body, name=name, grid=(1,), in_specs=[whole] * 4, out_specs=[whole] * 3,
            out_shape=[jax.ShapeDtypeStruct(shape, F32)] * 3, compiler_params=_params("arbitrary"),
        )(w, g, m, v)
    cols = shape[-1]
    rows = w.size // cols
    per_entry = rows // shape[0] if part is not None else rows
    tr = per_entry
    budget_rows = (2 << 20) // (cols * 4)
    for cand in range(min(per_entry, budget_rows) // 8 * 8, 0, -8):
        if per_entry % cand == 0:
            tr = cand
            break
    first, count = part if part is not None else (0, 1)
    tiles = per_entry // tr

    n_out = 4 if copy_grad else 3

    def body(w_ref, g_ref, m_ref, v_ref, *rest):
        outs = rest[-n_out:]
        outs[0][...], outs[1][...], outs[2][...] = _adamw_math(w_ref[...], g_ref[...], m_ref[...], v_ref[...])
        if copy_grad:
            outs[3][...] = g_ref[...]

    spec = pl.BlockSpec((tr, cols), lambda i: (i + first * tiles, 0))
    operands = [a.reshape(rows, cols) for a in (w, g, m, v)]
    aliases = {}
    if prev is not None:
        operands += [p.reshape(rows, cols) for p in prev]
        aliases = {4 + t: t for t in range(n_out)}
    outs = pl.pallas_call(
        body, name=name, grid=(count * tiles,), in_specs=[spec] * 4 + [_ANY] * (len(operands) - 4),
        out_specs=[spec] * n_out, out_shape=[jax.ShapeDtypeStruct((rows, cols), F32)] * n_out,
        input_output_aliases=aliases, compiler_params=_params("parallel"),
    )(*operands)
    return [o.reshape(shape) for o in outs]


def adamw_ada(c_pad, dmod, w, m, v):
    tr = 256

    def body(c_ref, dm_ref, w_ref, m_ref, v_ref, g_ref, d_ref, mo_ref, vo_ref):
        c = c_ref[...]
        sc = (c * jax.nn.sigmoid(c)).astype(BF16)
        g = _dot(sc, dm_ref[...].astype(BF16), TN)
        g_ref[...] = g
        d_ref[...], mo_ref[...], vo_ref[...] = _adamw_math(w_ref[...], g, m_ref[...], v_ref[...])

    wspec = pl.BlockSpec((None, tr, MOD_COLS), lambda i, r: (i, r, 0))
    return pl.pallas_call(
        body, name="adamw_ada", grid=(2, D // tr),
        in_specs=[pl.BlockSpec((16, tr), lambda i, r: (0, r)),
                  pl.BlockSpec((None, 16, MOD_COLS), lambda i, r: (i, 0, 0)), wspec, wspec, wspec],
        out_specs=[wspec] * 4,
        out_shape=[jax.ShapeDtypeStruct((2, D, MOD_COLS), F32)] * 4,
        compiler_params=_params("parallel", "parallel"),
    )(c_pad, dmod, w, m, v)


def sum_devices(name, a):
    _, R, C = a.shape
    tr = R
    for cand in (64, 32, 16, 8):
        if R % cand == 0:
            tr = cand
            break

    def body(a_ref, o_ref):
        acc = a_ref[0]
        for dev in range(1, N_DEV):
            acc = acc + a_ref[dev]
        o_ref[...] = acc

    return pl.pallas_call(
        body, name=name, grid=(R // tr,),
        in_specs=[pl.BlockSpec((N_DEV, tr, C), lambda i: (0, i, 0))],
        out_specs=pl.BlockSpec((tr, C), lambda i: (i, 0)),
        out_shape=jax.ShapeDtypeStruct((R, C), F32),
        compiler_params=_params("parallel"),
    )(a)


def _place():
    return lax.axis_index("x"), lax.axis_index("y"), lax.axis_index("c")


def _other_chips(x, y):
    return [(1 - x, y), (x, 1 - y), (1 - x, 1 - y)]


def gather_devices(name, a):
    m_per, n = a.shape

    def body(x_ref, out_ref, send_sems, recv_sems, local_sem):
        x, y, c = _place()
        me, sibling = (x, y, c), (x, y, 1 - c)
        chips = _other_chips(x, y)

        def rows(px, py, pc):
            return out_ref.at[pl.ds((4 * px + 2 * py + pc) * m_per, m_per), :]

        def copy(k, block, to, src=None):
            return pltpu.make_async_remote_copy(
                src_ref=rows(*block) if src is None else src, dst_ref=rows(*block),
                send_sem=send_sems.at[k], recv_sem=recv_sems.at[k], device_id=to, device_id_type=MESH)

        mine = pltpu.make_async_copy(x_ref, rows(*me), local_sem)
        mine.start()
        first = [copy(0, me, sibling, src=x_ref)]
        first += [copy(1 + j, me, (*chip, c), src=x_ref) for j, chip in enumerate(chips)]
        for cp in first:
            cp.start()
        passed = [copy(4 + j, (*chip, c), sibling) for j, chip in enumerate(chips)]
        for j, chip in enumerate(chips):
            copy(1 + j, (*chip, c), me).wait_recv()
            passed[j].start()
        copy(0, sibling, me).wait_recv()
        for j, chip in enumerate(chips):
            copy(4 + j, (*chip, 1 - c), me).wait_recv()
        for cp in first + passed:
            cp.wait_send()
        mine.wait()

    out = pl.pallas_call(
        body, name=name,
        out_shape=jax.ShapeDtypeStruct((N_DEV * m_per, n), a.dtype),
        in_specs=[pl.BlockSpec(memory_space=pltpu.VMEM)],
        out_specs=pl.BlockSpec(memory_space=pltpu.VMEM),
        scratch_shapes=[pltpu.SemaphoreType.DMA((7,)), pltpu.SemaphoreType.DMA((7,)), pltpu.SemaphoreType.DMA],
        compiler_params=pltpu.CompilerParams(vmem_limit_bytes=VMEM_LIMIT),
    )(a)
    return out.reshape(N_DEV, m_per, n)


_ANY = pl.BlockSpec(memory_space=pl.ANY)


def _hbm_ref(a):
    return jax.new_ref(a, memory_space=pltpu.MemorySpace.HBM)


def _hbm_empty(shape, dtype):
    return jax.empty_ref(jax.ShapeDtypeStruct(shape, dtype), memory_space=pltpu.MemorySpace.HBM)


ID_PAIR, ID_CHIPS, ID_SHARE, ID_UKV = 8, 9, 10, 11


def _sequencer(name, collective_id, n_sem, peers_of, program):
    sems = pltpu.SemaphoreType.DMA((n_sem,))

    @pl.kernel(mesh=plsc.ScalarSubcoreMesh(axis_name="seq", num_cores=1), name=name, scratch_types=[sems, sems],
               compiler_params=pltpu.CompilerParams(collective_id=collective_id))
    def launch(send_sem, recv_sem):
        x, y, c = _place()
        peers = peers_of(x, y, c)
        barrier = pltpu.get_barrier_semaphore()
        for peer in peers:
            pl.semaphore_signal(barrier, inc=1, device_id=peer, device_id_type=MESH)
        pl.semaphore_wait(barrier, len(peers))
        program(x, y, c, send_sem, recv_sem)

    launch()


def gather_weights(name, stage, arrays):
    n = len(arrays)
    refs = [_hbm_ref(a) for a in arrays]

    def program(x, y, c, send_sem, recv_sem):
        me = 2 * x + y
        chips = _other_chips(x, y)

        def ici(t, r, half):
            cx, cy = chips[r]
            mine = refs[t].at[me, half]
            return pltpu.make_async_remote_copy(
                src_ref=mine, dst_ref=mine, send_sem=send_sem.at[3 * t + r], recv_sem=recv_sem.at[3 * t + r],
                device_id=(cx, cy, c), device_id_type=MESH)

        def d2d(t, r, half):
            cx, cy = chips[r]
            there = refs[t].at[2 * cx + cy, half]
            k = 3 * n + 3 * t + r
            return pltpu.make_async_remote_copy(
                src_ref=there, dst_ref=there, send_sem=send_sem.at[k], recv_sem=recv_sem.at[k],
                device_id=(x, y, 1 - c), device_id_type=MESH)

        for t in range(n):
            for r in range(3):
                ici(t, r, c).start()
        for t in range(n):
            for r in range(3):
                ici(t, r, c).wait_recv()
                d2d(t, r, c).start()
        for t in range(n):
            for r in range(3):
                d2d(t, r, 1 - c).wait_recv()
        for t in range(n):
            for r in range(3):
                ici(t, r, c).wait_send()
                d2d(t, r, c).wait_send()

    _sequencer(name, stage, 6 * n, lambda x, y, c: [(x, y, 1 - c)] + [(cx, cy, c) for cx, cy in _other_chips(x, y)],
               program)
    return [r[...] for r in refs]


def cast_into_slots(name, chip, shards, after=None):
    steps = 2
    n = len(shards)

    def body(chip_ref, *refs):
        for src, dst in zip(refs[:n], refs[-n - 1:-1]):
            dst[...] = src[...].astype(BF16)
        refs[-1][...] = jnp.zeros_like(refs[-1])

    token_spec = pl.BlockSpec((SUBLANES, LANES), lambda h, i, chip_ref: (0, 0))

    def spec_in(a, prefix):
        R, C = a.shape[-2:]
        return pl.BlockSpec((None,) * (len(prefix) + 1) + (R // steps, C), lambda h, i, chip_ref: prefix + (h, i, 0))

    def spec_out(a):
        R, C = a.shape[-2:]
        return pl.BlockSpec((None, None, R // steps, C), lambda h, i, chip_ref: (chip_ref[0], h, i, 0))

    outs = pl.pallas_call(
        body, name=name,
        grid_spec=pltpu.PrefetchScalarGridSpec(
            num_scalar_prefetch=1, grid=(2, steps),
            in_specs=[spec_in(a, p) for a, p in shards] + ([token_spec] if after is not None else []),
            out_specs=[spec_out(a) for a, _ in shards] + [token_spec]),
        out_shape=[jax.ShapeDtypeStruct((N_CHIP, 2) + a.shape[-2:], BF16) for a, _ in shards]
        + [jax.ShapeDtypeStruct((SUBLANES, LANES), F32)],
        compiler_params=_params("arbitrary", "arbitrary"),
    )(chip, *[a for a, _ in shards], *([after] if after is not None else []))
    return outs[:-1], outs[-1]


def reduce_pair(name, grads):
    n = len(grads)
    src = [_hbm_ref(g) for g in grads]
    dst = [_hbm_empty((N_CHIP,) + g.shape[2:], g.dtype) for g in grads]

    def program(x, y, c, send_sem, recv_sem):
        cps = [pltpu.make_async_remote_copy(
            src_ref=src[t].at[:, 1 - c], dst_ref=dst[t], send_sem=send_sem.at[t], recv_sem=recv_sem.at[t],
            device_id=(x, y, 1 - c), device_id_type=MESH) for t in range(n)]
        for cp in cps:
            cp.start()
        for cp in cps:
            cp.wait()

    _sequencer(name, ID_PAIR, n, lambda x, y, c: [(x, y, 1 - c)], program)
    return [r[...] for r in src], [r[...] for r in dst]


def pair_add(name, core, g, got):
    _, _, R, C = g.shape

    def body(core_ref, g_ref, got_ref, o_ref):
        o_ref[...] = (g_ref[...] + got_ref[...]).astype(BF16)

    return pl.pallas_call(
        body, name=name,
        grid_spec=pltpu.PrefetchScalarGridSpec(
            num_scalar_prefetch=1, grid=(N_CHIP,),
            in_specs=[pl.BlockSpec((None, None, R, C), lambda q, core_ref: (q, core_ref[0], 0, 0)),
                      pl.BlockSpec((None, R, C), lambda q, core_ref: (q, 0, 0))],
            out_specs=pl.BlockSpec((None, R, C), lambda q, core_ref: (q, 0, 0))),
        out_shape=jax.ShapeDtypeStruct((N_CHIP, R, C), BF16),
        compiler_params=_params("parallel"),
    )(core, g, got)


def reduce_chips(name, sums):
    n = len(sums)
    src = [_hbm_ref(s) for s in sums]
    dst = [_hbm_empty((3,) + s.shape[1:], s.dtype) for s in sums]

    def program(x, y, c, send_sem, recv_sem):
        cps = []
        for t in range(n):
            for r, (cx, cy) in enumerate(_other_chips(x, y)):
                cps.append(pltpu.make_async_remote_copy(
                    src_ref=src[t].at[2 * cx + cy], dst_ref=dst[t].at[r],
                    send_sem=send_sem.at[3 * t + r], recv_sem=recv_sem.at[3 * t + r],
                    device_id=(cx, cy, c), device_id_type=MESH))
        for cp in cps:
            cp.start()
        for cp in cps:
            cp.wait()

    _sequencer(name, ID_CHIPS, 3 * n, lambda x, y, c: [(cx, cy, c) for cx, cy in _other_chips(x, y)], program)
    return [r[...] for r in src], [r[...] for r in dst]


def chip_add(name, place, s, got, k, n_slots, prev=None, after=None):
    _, R, C = s.shape

    def body(place_ref, s_ref, got_ref, *rest):
        o_ref = rest[-1]
        o_ref[...] = ((s_ref[...].astype(F32) + got_ref[0].astype(F32)) + got_ref[1].astype(F32)) + got_ref[2].astype(F32)

    in_specs = [pl.BlockSpec((None, R, C), lambda i, place_ref: (place_ref[0], 0, 0)),
                pl.BlockSpec((3, R, C), lambda i, place_ref: (0, 0, 0))]
    args = [place, s, got]
    aliases = {}
    if prev is not None:
        in_specs.append(_ANY)
        args.append(prev)
        aliases = {3: 0}
    for piece in after or ():
        in_specs.append(pl.BlockSpec((SUBLANES, LANES), lambda i, place_ref: (0, 0)))
        args.append(piece)
    return pl.pallas_call(
        body, name=name,
        grid_spec=pltpu.PrefetchScalarGridSpec(
            num_scalar_prefetch=1, grid=(1,), in_specs=in_specs,
            out_specs=pl.BlockSpec((None, None, R, C), lambda i, place_ref: (k, place_ref[1], 0, 0))),
        out_shape=jax.ShapeDtypeStruct((n_slots, 2, R, C), F32),
        input_output_aliases=aliases,
        compiler_params=_params("arbitrary"),
    )(*args)


def share_halves(name, stacks, slots):
    n = len(stacks)
    dst = [_hbm_ref(s) for s in stacks]

    def program(x, y, c, send_sem, recv_sem):
        cps = [pltpu.make_async_remote_copy(
            src_ref=dst[t].at[slots[t], c], dst_ref=dst[t].at[slots[t], c],
            send_sem=send_sem.at[t], recv_sem=recv_sem.at[t],
            device_id=(x, y, 1 - c), device_id_type=MESH) for t in range(n)]
        for cp in cps:
            cp.start()
        for cp in cps:
            cp.wait()

    _sequencer(name, ID_SHARE, n, lambda x, y, c: [(x, y, 1 - c)], program)
    return [r[...] for r in dst]


def gather_blocks(name, slotted):
    out = _hbm_ref(slotted)

    def program(x, y, c, send_sem, recv_sem):
        sibling = (x, y, 1 - c)
        chips = _other_chips(x, y)

        def copy(k, px, py, pc, to):
            block = out.at[4 * px + 2 * py + pc]
            return pltpu.make_async_remote_copy(src_ref=block, dst_ref=block, send_sem=send_sem.at[k],
                                                recv_sem=recv_sem.at[k], device_id=to, device_id_type=MESH)

        first = [copy(0, x, y, c, sibling)] + [copy(1 + j, x, y, c, (cx, cy, c)) for j, (cx, cy) in enumerate(chips)]
        for cp in first:
            cp.start()
        passed = [copy(4 + j, cx, cy, c, sibling) for j, (cx, cy) in enumerate(chips)]
        for j, (cx, cy) in enumerate(chips):
            copy(1 + j, cx, cy, c, (x, y, c)).wait_recv()
            passed[j].start()
        copy(0, x, y, 1 - c, (x, y, c)).wait_recv()
        for j, (cx, cy) in enumerate(chips):
            copy(4 + j, cx, cy, 1 - c, (x, y, c)).wait_recv()
        for cp in first + passed:
            cp.wait_send()

    _sequencer(name, ID_UKV, 7, lambda x, y, c: [(x, y, 1 - c)] + [(cx, cy, c) for cx, cy in _other_chips(x, y)],
               program)
    return out[...]


def place_block(name, dev, a):
    M, N = a.shape
    tr = min(M, 64)

    def body(dev_ref, a_ref, o_ref):
        o_ref[...] = a_ref[...]

    return pl.pallas_call(
        body, name=name,
        grid_spec=pltpu.PrefetchScalarGridSpec(
            num_scalar_prefetch=1, grid=(M // tr,),
            in_specs=[pl.BlockSpec((tr, N), lambda i, dev_ref: (i, 0))],
            out_specs=pl.BlockSpec((None, tr, N), lambda i, dev_ref: (dev_ref[0], i, 0))),
        out_shape=jax.ShapeDtypeStruct((N_DEV, M, N), a.dtype),
        compiler_params=_params("parallel"),
    )(dev, a)


def _swap_rope(a):
    return jnp.concatenate([a[..., ROPE // 2:], a[..., :ROPE // 2]], axis=-1)


def _rope_tables(S):
    inv = 1.0 / (ROPE_THETA ** (jnp.arange(0, ROPE, 2, dtype=F32) / ROPE))
    ang = jnp.arange(S, dtype=F32)[:, None] * inv[None, :]
    cos, sin = jnp.cos(ang), jnp.sin(ang)
    return (jnp.tile(jnp.concatenate([cos, cos], axis=1), (1, SLOTS)),
            jnp.tile(jnp.concatenate([-sin, sin], axis=1), (1, SLOTS)))


def _vec(norm_g, mod, i, k):
    rows = [norm_g[i, 2 * k], norm_g[i, 2 * k + 1], mod[i, 3 * k], mod[i, 3 * k + 1], mod[i, 3 * k + 2]]
    return jnp.concatenate([jnp.stack(rows), jnp.zeros((3, D), F32)], axis=0)


def _unpack_weights(full, w_uk, w_uv, q_norm, kv_norm):
    G = D // 4
    ffn_in = [[full[2 * i + k].reshape(N_CHIP, D, FSH) for k in range(2)] for i in range(2)]
    ffn_out = [[full[4 + 2 * i + k].reshape(2, FSH, D) for k in range(2)] for i in range(2)]
    pw = full[8].reshape(N_CHIP, 4, G // N_CHIP, G).transpose(1, 0, 2, 3).reshape(4, G, G)
    w_in = full[9].reshape(D, QL + KVL + ROPE)
    w_uq = full[10].reshape(QL, N_HEADS, NOPE + ROPE)
    wkr = w_in[:, QL + KVL:]
    wr = w_uq[:, :, NOPE:]
    eye2 = jnp.eye(2, dtype=BF16)
    uk_t = jnp.transpose(w_uk, (1, 2, 0)).reshape(N_PAIR, 2, NOPE, KVL)
    bduk = jnp.einsum("janc,ab->janbc", uk_t, eye2).reshape(N_PAIR, 2 * NOPE, 2 * KVL)
    uv = jnp.transpose(w_uv, (1, 0, 2)).reshape(N_PAIR, 2, KVL, VH)
    bduv = jnp.einsum("jacn,ab->jacbn", uv, eye2).reshape(N_PAIR, 2 * KVL, 2 * VH)
    mw = dict(wq=w_in[:, :QL], wkv=w_in[:, QL:QL + KVL], wkr4=jnp.tile(wkr, (1, SLOTS)),
              wkrs4=jnp.tile(_swap_rope(wkr), (1, SLOTS)), qn=q_norm, kvn=kv_norm,
              wn=w_uq[:, :, :NOPE].reshape(QL, NOPE_ALL), wr=wr.reshape(QL, ROPE_ALL),
              wrs=_swap_rope(wr).reshape(QL, ROPE_ALL), bduk=bduk)
    return ffn_in, ffn_out, pw, mw, bduv, full[11].reshape(D, D)


def _example_step(x, target, mod, norm_g, pvec, ffn_in, ffn_out, pw, mw, bduv, wo, reducer):
    S = x.shape[0]
    tabs = _rope_tables(S)
    vec = [[_vec(norm_g, mod, i, k) for k in range(3)] for i in range(2)]
    saved = {}
    for i in range(2):
        xin = x
        x, a, u, h = ffn_fwd(xin, vec[i][0], ffn_in[i][0], ffn_out[i][0], 0.5)
        saved[i, 0] = (xin, a, u, h)
        xin = x
        if i == 0:
            x, y, z = pool_fwd(xin, vec[i][1], pw, pvec)
            saved[i, 1] = (xin, y, z)
        else:
            h_m, cq_raw, ckv_raw, cqn, qnope, qcat, kcat, vcat = mla_pre(xin, vec[i][1], mw, tabs)
            olat, lse = attn_fwd(qcat, kcat, vcat)
            x, u_m, ocat = mla_post(olat, xin, vec[i][1], bduv, wo)
            saved[i, 1] = (xin, h_m, cq_raw, ckv_raw, cqn, qnope, qcat, kcat, olat, lse, u_m, ocat)
        xin = x
        x, a, u, h = ffn_fwd(xin, vec[i][2], ffn_in[i][1], ffn_out[i][1], 0.5)
        saved[i, 2] = (xin, a, u, h)
    loss, dx = loss_head(x, target)

    vg = {}
    G = D // 4

    def ffn_grads(i, k, dw_in, dw_out):
        return [(0, 2 * i + k, 4, dw_in.reshape(N_CHIP, 2, D // 2, FSH)),
                (1, 2 * i + k, 4, dw_out.reshape(N_CHIP, 2, DFF // 8, D))]

    piece = lambda t: t[:SUBLANES, :LANES]
    for i in (1, 0):
        xin, a, u, h = saved[i, 2]
        dx, du, act, da, vg[i, 2] = ffn_bwd(dx, xin, u, a, vec[i][2], ffn_in[i][1], ffn_out[i][1], 0.5)
        reducer.advance(after=(piece(dx),))
        reducer.add(f"f{i}1", ffn_grads(i, 1, *ffn_dw(h, da, act, du)))
        if i == 0:
            xin, y, z = saved[i, 1]
            dx, vg[i, 1], pgrad, g_pool = pool_bwd(dx, xin, y, z, vec[i][1], pw, pvec)
            reducer.advance(after=(piece(dx),))
        else:
            xin, h_m, cq_raw, ckv_raw, cqn, qnope, qcat, kcat, olat, lse, u_m, ocat = saved[i, 1]
            du, docat, dolat, delta, vg_post = mla_post_bwd(dx, u_m, olat, vec[i][1], bduv, wo)
            reducer.advance()
            dq, dk, dv = attn_bwd(qcat, kcat, kcat.T, dolat, lse, delta)
            reducer.advance(after=(piece(dk),))
            dx, dlat, dql, dqcat, vg_pre, ngrad = mla_pre_bwd(
                dx, dq, dk, dv, xin, cq_raw, ckv_raw, vec[i][1], mw, tabs)
            vg[i, 1] = vg_post + vg_pre
            g = mla_dw(h_m, dlat, cqn, dqcat, dql, qnope, olat, docat, ocat, du)
            slots = lambda a: a.reshape(D, SLOTS, ROPE).sum(axis=1)
            g_kr = slots(g["in"][:, QL + KVL:QL + KVL + LANES]) + _swap_rope(slots(g["in"][:, QL + KVL + LANES:]))
            g_in = jnp.concatenate([g["in"][:, :QL + KVL], g_kr], axis=1)
            g_r = g["q"][:, NOPE_ALL:NOPE_ALL + ROPE_ALL].reshape(QL, N_HEADS, ROPE)
            g_rs = g["q"][:, NOPE_ALL + ROPE_ALL:].reshape(QL, N_HEADS, ROPE)
            g_uq = jnp.concatenate([g["q"][:, :NOPE_ALL].reshape(QL, N_HEADS, NOPE), g_r + _swap_rope(g_rs)], axis=-1)

            def heads(pairs):
                blk = pairs.reshape(N_PAIR, 2, KVL, 2, NOPE)
                per_head = jnp.stack([blk[:, 0, :, 0, :], blk[:, 1, :, 1, :]], axis=1).reshape(N_HEADS, KVL, NOPE)
                return jnp.transpose(per_head, (1, 0, 2)).reshape(KVL, N_HEADS * NOPE)

            reducer.add("mla", [(3, 0, 1, g_in.reshape(N_CHIP, 2, D // 8, QL + KVL + ROPE)),
                                (4, 0, 1, g_uq.reshape(N_CHIP, 2, QL // 8, N_HEADS * (NOPE + ROPE))),
                                (5, 0, 1, g["o"].reshape(N_CHIP, 2, D // 8, D))])
            reducer.add_replicated(jnp.concatenate([heads(g["uk"]), heads(g["uv"])], axis=0))
        xin, a, u, h = saved[i, 0]
        dx, du, act, da, vg[i, 0] = ffn_bwd(dx, xin, u, a, vec[i][0], ffn_in[i][0], ffn_out[i][0], 0.5)
        if i == 1:
            reducer.advance(after=(piece(dx),))
        grads = ffn_grads(i, 0, *ffn_dw(h, da, act, du))
        if i == 0:
            grads.append((2, 0, 1, g_pool.reshape(N_CHIP, 2, 2 * G // N_CHIP, G)))
        reducer.add(f"f{i}0", grads)
    return loss, dx, vg, pgrad, ngrad


class _GradReducer:
    def __init__(self, core, place, dev):
        self.core, self.place, self.dev = core, place, dev
        self.stacks = {}
        self.live = []
        self.replicated = None

    def add(self, tag, items):
        gen = self._run(tag, items)
        next(gen)
        self.live.append(gen)

    def add_replicated(self, block):
        self.replicated = gather_blocks("gather_ukv", place_block("place_ukv", self.dev, block))

    def advance(self, after=None):
        self.after = after
        live = []
        for gen in self.live:
            try:
                next(gen)
                live.append(gen)
            except StopIteration:
                pass
        self.live = live

    def finish(self):
        while self.live:
            self.advance()
        return self.stacks, self.replicated

    def _run(self, tag, items):
        grads, from_pair = reduce_pair(f"reduce_pair_{tag}", [g for *_, g in items])
        yield
        sums = [pair_add(f"pair_add_{tag}_{j}", self.core, g, p) for j, (g, p) in enumerate(zip(grads, from_pair))]
        sums, from_chips = reduce_chips(f"reduce_chips_{tag}", sums)
        yield
        for j, ((o, k, n_slots, _), s, p) in enumerate(zip(items, sums, from_chips)):
            self.stacks[o] = chip_add(f"chip_add_{tag}_{j}", self.place, s, p, k, n_slots, self.stacks.get(o),
                                      self.after)
        shared = share_halves(f"share_halves_{tag}", [self.stacks[o] for o, *_ in items], [k for _, k, *_ in items])
        for (o, *_), v in zip(items, shared):
            self.stacks[o] = v


SMALL_IN = 8 * 640
SMALL_GRAD = 8 * 4224
SMALL_W = 8 * 2944


def _pack(parts, total):
    flat = jnp.concatenate([p.reshape(-1) for p in parts])
    return jnp.concatenate([flat, jnp.zeros((total - flat.shape[0],), F32)]).reshape(8, total // 8)


def kernel(x, c, ada_w, ada_b, norm_g, ffn_w_in, ffn_w_out, pool_w, pool_b, pool_scale, mla_w_in, mla_q_norm, mla_kv_norm, mla_w_uq, mla_w_uk, mla_w_uv, mla_w_o, loss_target, m_ada_w, m_ada_b, m_norm_g, m_ffn_w_in, m_ffn_w_out, m_pool_w, m_pool_b, m_pool_scale, m_mla_w_in, m_mla_q_norm, m_mla_kv_norm, m_mla_w_uq, m_mla_w_uk, m_mla_w_uv, m_mla_w_o, v_ada_w, v_ada_b, v_norm_g, v_ffn_w_in, v_ffn_w_out, v_pool_w, v_pool_b, v_pool_scale, v_mla_w_in, v_mla_q_norm, v_mla_kv_norm, v_mla_w_uq, v_mla_w_uk, v_mla_w_uv, v_mla_w_o):
    ix, iy, ic = _place()
    chip = 2 * ix + iy
    dev = 2 * chip + ic
    core_arr = ic.astype(jnp.int32).reshape(1)
    chip_arr = chip.astype(jnp.int32).reshape(1)
    S = x.shape[1]
    G = D // 4
    NG = D // N_CHIP

    def chip_cols(a, width, axis):
        return lax.dynamic_slice_in_dim(a, chip * width, width, axis)

    got = gather_devices("gather_small_in", _pack([c, norm_g, pool_b, mla_q_norm], SMALL_IN)).reshape(N_DEV, SMALL_IN)
    c_all = got[:, :D]
    parts = got[0::2]
    o = D
    norm_g_full = parts[:, o:o + 12 * NG].reshape(N_CHIP, 2, 6, NG).transpose(1, 2, 0, 3).reshape(2, 6, D)
    o += 12 * NG
    pool_b_full = parts[:, o:o + G].reshape(N_CHIP, 4, G // N_CHIP).transpose(1, 0, 2).reshape(1, D)
    o += G
    q_norm_full = parts[:, o:o + QL // N_CHIP].reshape(1, QL)
    pvec = jnp.concatenate([pool_b_full, pool_scale, jnp.zeros((6, D), F32)], axis=0)

    c_pad = jnp.concatenate([c_all, jnp.zeros((8, D), F32)], axis=0)
    mod_loc = mod_fwd(c_pad, ada_w, chip_cols(ada_b, MOD_COLS, 1).reshape(2, 1, MOD_COLS))
    got = gather_devices("gather_mod", mod_loc[:, :8].transpose(1, 0, 2).reshape(8, 2 * MOD_COLS))
    mine = lax.dynamic_index_in_dim(got[0::2].reshape(N_CHIP, 8, 2, MOD_COLS), dev, axis=1, keepdims=False)
    mod = mine.transpose(1, 0, 2).reshape(2, 9, D)

    bf = lambda a: a.astype(BF16)
    w_in_halves = ffn_w_in.reshape(2, 2, 2, D // 2, FSH)
    w_out_halves = ffn_w_out.reshape(2, 2, 2, DFF // 8, D)
    shards = [(w_in_halves, (i, k)) for i in range(2) for k in range(2)]
    shards += [(w_out_halves, (i, k)) for i in range(2) for k in range(2)]
    shards += [(pool_w.reshape(2, 2 * G // N_CHIP, G), ()), (mla_w_in.reshape(2, D // 8, QL + KVL + ROPE), ()),
               (mla_w_uq.reshape(2, QL // 8, N_HEADS * (NOPE + ROPE)), ()), (mla_w_o.reshape(2, D // 8, D), ())]
    full = [None] * len(shards)
    stages = [(0, 4, 8), (1, 5), (2, 6), (9, 10, 11), (3, 7)]
    first, token = cast_into_slots("cast_first", chip_arr, [shards[t] for t in stages[0]])
    slotted = dict(zip(stages[0], first))
    rest = [t for members in stages[1:] for t in members]
    for stage, members in enumerate(stages):
        got_w = gather_weights(f"gather_weights_{stage}", stage, [slotted[t] for t in members])
        for t, a in zip(members, got_w):
            full[t] = a
        if stage == 0:
            slotted.update(zip(rest, cast_into_slots("cast_rest", chip_arr, [shards[t] for t in rest], token)[0]))
    ffn_in, ffn_out, pw, mw, bduv, wo = _unpack_weights(full, bf(mla_w_uk[0]), bf(mla_w_uv[0]), q_norm_full,
                                                        mla_kv_norm)

    place_arr = jnp.stack([chip, ic]).astype(jnp.int32)
    reducer = _GradReducer(core_arr, place_arr, dev.astype(jnp.int32).reshape(1))
    loss_mine, grad_x, vg, pgrad, ngrad = _example_step(
        x[0], loss_target[0], mod, norm_g_full, pvec, ffn_in, ffn_out, pw, mw, bduv, wo, reducer)

    dmod = jnp.stack([jnp.concatenate([vg[i, k][0:3] for k in range(3)]) for i in range(2)])
    dnorm = jnp.stack([jnp.concatenate([vg[i, k][3:5] for k in range(3)]) for i in range(2)])
    small = _pack([dmod, dnorm, pgrad[0], pgrad[1], ngrad[0], ngrad[1, :KVL], loss_mine], SMALL_GRAD)
    got = gather_devices("gather_small_grad", small)
    tot = sum_devices("sum_small_grad", got).reshape(-1)
    n_mod = 2 * 9 * D
    g_ada_b = tot[:n_mod].reshape(ada_b.shape)
    o = n_mod
    g_norm = chip_cols(tot[o:o + 12 * D].reshape(2, 6, D), NG, 2)
    o += 12 * D
    g_pool_b = chip_cols(tot[o:o + D].reshape(1, 4, G), G // N_CHIP, 2)
    o += D
    g_pool_scale = tot[o:o + D].reshape(pool_scale.shape)
    o += D
    g_q_norm = chip_cols(tot[o:o + QL].reshape(1, QL), QL // N_CHIP, 1)
    o += QL
    g_kv_norm = tot[o:o + KVL].reshape(mla_kv_norm.shape)
    loss = tot[o + KVL]
    dmod_all = chip_cols(got.reshape(N_DEV, -1)[:, :n_mod].reshape(N_DEV, 2, 9 * D), MOD_COLS, 2)
    dmod_pad = jnp.concatenate([dmod_all.transpose(1, 0, 2), jnp.zeros((2, 8, MOD_COLS), F32)], axis=1)

    g_ada_w, d_ada_w, nm_ada_w, nv_ada_w = adamw_ada(c_pad, dmod_pad, ada_w, m_ada_w, v_ada_w)
    small_names = ["ada_b", "norm_g", "pool_b", "pool_scale", "mla_q_norm", "mla_kv_norm"]
    small_w = [ada_b, norm_g, pool_b, pool_scale, mla_q_norm, mla_kv_norm]
    small_g = [g_ada_b, g_norm, g_pool_b, g_pool_scale, g_q_norm, g_kv_norm]
    small_m = [m_ada_b, m_norm_g, m_pool_b, m_pool_scale, m_mla_q_norm, m_mla_kv_norm]
    small_v = [v_ada_b, v_norm_g, v_pool_b, v_pool_scale, v_mla_q_norm, v_mla_kv_norm]
    packed = adamw("adamw_small", *[_pack(p, SMALL_W) for p in (small_w, small_g, small_m, small_v)])
    upd = {}
    o = 0
    for name, w in zip(small_names, small_w):
        upd[name] = [p.reshape(-1)[o:o + w.size].reshape(w.shape) for p in packed]
        o += w.size
    upd["ada_w"] = [d_ada_w, nm_ada_w, nv_ada_w]

    reducer.advance(after=(d_ada_w[0, :SUBLANES, :LANES],))
    ffn = [("ffn_w_in", 0, ffn_w_in, m_ffn_w_in, v_ffn_w_in), ("ffn_w_out", 1, ffn_w_out, m_ffn_w_out, v_ffn_w_out)]
    slots = lambda a: a.reshape((4,) + a.shape[2:])
    early = {name: adamw(f"adamw_{name}_early", slots(w), slots(reducer.stacks[o].reshape(w.shape)), slots(m),
                         slots(v), part=(1, 3), copy_grad=True) for name, o, w, m, v in ffn}
    g_mla_in = reducer.stacks[3].reshape(mla_w_in.shape)
    g_uq = reducer.stacks[4].reshape(mla_w_uq.shape)
    g_wo = reducer.stacks[5].reshape(mla_w_o.shape)
    for name, w, g, m, v in [("mla_w_in", mla_w_in, g_mla_in, m_mla_w_in, v_mla_w_in),
                             ("mla_w_uq", mla_w_uq, g_uq, m_mla_w_uq, v_mla_w_uq),
                             ("mla_w_o", mla_w_o, g_wo, m_mla_w_o, v_mla_w_o)]:
        upd[name] = adamw("adamw_" + name, w, g, m, v)

    reducer.advance(after=(early["ffn_w_in"][0][1, :SUBLANES, :LANES], early["ffn_w_out"][0][1, :SUBLANES, :LANES],
                           upd["mla_w_o"][0][0, :SUBLANES, :LANES], upd["mla_w_in"][0][0, :SUBLANES, :LANES]))
    ukv = sum_devices("sum_ukv", reducer.replicated)
    g_uk = ukv[:KVL].reshape(mla_w_uk.shape)
    g_uv = ukv[KVL:].reshape(mla_w_uv.shape)
    upd["mla_w_uk"] = adamw("adamw_mla_w_uk", mla_w_uk, g_uk, m_mla_w_uk, v_mla_w_uk)
    upd["mla_w_uv"] = adamw("adamw_mla_w_uv", mla_w_uv, g_uv, m_mla_w_uv, v_mla_w_uv)
    stacks, _ = reducer.finish()
    g_pool_w = stacks[2].reshape(pool_w.shape)
    g_ffn = {}
    for name, o, w, m, v in ffn:
        done = adamw(f"adamw_{name}_last", slots(w), slots(stacks[o].reshape(w.shape)), slots(m), slots(v),
                     part=(0, 1), prev=early[name], copy_grad=True)
        upd[name] = [p.reshape(w.shape) for p in done[:3]]
        g_ffn[name] = done[3].reshape(w.shape)
    g_ffn_in, g_ffn_out = g_ffn["ffn_w_in"], g_ffn["ffn_w_out"]
    upd["pool_w"] = adamw("adamw_pool_w", pool_w, g_pool_w, m_pool_w, v_pool_w)

    order = ["ada_w", "ada_b", "norm_g", "ffn_w_in", "ffn_w_out", "pool_w", "pool_b", "pool_scale", "mla_w_in",
             "mla_q_norm", "mla_kv_norm", "mla_w_uq", "mla_w_uk", "mla_w_uv", "mla_w_o"]
    grad = dict(ada_w=g_ada_w, ada_b=g_ada_b, norm_g=g_norm, ffn_w_in=g_ffn_in, ffn_w_out=g_ffn_out, pool_w=g_pool_w,
                pool_b=g_pool_b, pool_scale=g_pool_scale, mla_w_in=g_mla_in, mla_q_norm=g_q_norm,
                mla_kv_norm=g_kv_norm, mla_w_uq=g_uq, mla_w_uk=g_uk, mla_w_uv=g_uv, mla_w_o=g_wo)
    return (loss, grad_x[None], *[grad[n] for n in order], *[upd[n][0] for n in order],
            *[upd[n][1] for n in order], *[upd[n][2] for n in order])
```

```python
import functools

import jax
import jax.numpy as jnp
from jax import lax
from jax.experimental import pallas as pl
from jax.experimental.pallas import tpu as pltpu
from jax.experimental.pallas import tpu_sc as plsc

F32 = jnp.float32
BF16 = jnp.bfloat16

D = 1024
DFF = 2816
FSH = 1408
N_CHIP = 4
N_DEV = 8
N_HEADS = 16
NOPE = 64
ROPE = 32
VH = 64
QL = 256
KVL = 128
LANES = 128
SUBLANES = 8
QPAD = 256
EPS = 1e-6
ATTN_SCALE = (NOPE + ROPE) ** -0.5
ROPE_THETA = 10000.0
POOL_WINDOWS = (2, 4, 8, 16)
HALO = 8
ATTN_TQ = 1024
ATTN_KC = 512
ROW_TILE = 512
DW_TK = 2048

ADAM_LR, ADAM_B1, ADAM_B2, ADAM_EPS, ADAM_WD, ADAM_STEP = 0.001, 0.9, 0.999, 1e-08, 0.01, 10

VMEM_LIMIT = 60 * 1024 * 1024
MESH = pl.DeviceIdType.MESH

NT = (((1,), (1,)), ((), ()))
TN = (((0,), (0,)), ((), ()))


def _params(*sem):
    return pltpu.CompilerParams(dimension_semantics=sem, vmem_limit_bytes=VMEM_LIMIT)


def _dot(a, b, dims=None):
    if dims is None:
        return jnp.dot(a, b, preferred_element_type=F32)
    return lax.dot_general(a, b, dims, preferred_element_type=F32)


def _rms(x):
    r = lax.rsqrt(jnp.mean(x * x, axis=-1, keepdims=True) + EPS)
    return x * r, r


def _rms_bwd(xhat, r, dxhat):
    return r * (dxhat - xhat * jnp.mean(dxhat * xhat, axis=-1, keepdims=True))


def _as_row(col):
    return jnp.broadcast_to(col, (col.shape[0], LANES)).T[0:1, :]


def _prenorm(x, vec_ref):
    xhat, r = _rms(x)
    h = xhat * vec_ref[0:1, :] * (1.0 + vec_ref[3:4, :]) + vec_ref[2:3, :]
    return h, xhat, r


def _postnorm_bwd(dout, u, vec_ref, weight):
    uhat, r = _rms(u)
    gt = weight * (1.0 + vec_ref[4:5, :])
    dy = dout * gt
    dgate_rows = (weight * dout) * (uhat * vec_ref[1:2, :])
    dgpost_rows = dy * uhat
    du = _rms_bwd(uhat, r, dy * vec_ref[1:2, :])
    return du, dgate_rows, dgpost_rows


def _prenorm_bwd(dh, x, vec_ref, vg_ref):
    xhat, r = _rms(x)
    sc1 = 1.0 + vec_ref[3:4, :]
    g = vec_ref[0:1, :]
    vg_ref[0:1, :] += jnp.sum(dh, axis=0, keepdims=True)
    vg_ref[1:2, :] += jnp.sum(dh * (xhat * g), axis=0, keepdims=True)
    vg_ref[3:4, :] += jnp.sum(dh * sc1 * xhat, axis=0, keepdims=True)
    return _rms_bwd(xhat, r, dh * g * sc1)


def ffn_fwd(x, vec, w_in, w_out, weight):
    S = x.shape[0]
    tm = min(512, S)
    row = lambda i: (i, 0)
    half = lambda j: [_w3((8, D)), pl.BlockSpec((None, D, FSH), lambda i: (j, 0, 0)),
                      pl.BlockSpec((None, D, FSH), lambda i: (j + 2, 0, 0)),
                      pl.BlockSpec((None, FSH, D), lambda i: (j, 0, 0))]
    a_spec = lambda j: pl.BlockSpec((2, tm, FSH), lambda i: (0, i, j))
    a_shape = jax.ShapeDtypeStruct((2, S, DFF), BF16)

    def hidden(hb, wg_ref, wu_ref, wo_ref, a_ref):
        g = _dot(hb, wg_ref[...])
        up = _dot(hb, wu_ref[...])
        a_ref[0] = g.astype(BF16)
        a_ref[1] = up.astype(BF16)
        act = (g * jax.nn.sigmoid(g)) * up
        return _dot(act.astype(BF16), wo_ref[...])

    def first(x_ref, vec_ref, wg_ref, wu_ref, wo_ref, h_ref, a_ref, u_ref):
        h, _, _ = _prenorm(x_ref[...], vec_ref)
        hb = h.astype(BF16)
        h_ref[...] = hb
        u_ref[...] = hidden(hb, wg_ref, wu_ref, wo_ref, a_ref)

    h, a, u_half = pl.pallas_call(
        first, name="ffn_fwd_first", grid=(S // tm,),
        in_specs=[pl.BlockSpec((tm, D), row)] + half(0),
        out_specs=[pl.BlockSpec((tm, D), row), a_spec(0), pl.BlockSpec((tm, D), row)],
        out_shape=[jax.ShapeDtypeStruct((S, D), BF16), a_shape, jax.ShapeDtypeStruct((S, D), F32)],
        compiler_params=_params("parallel"),
    )(x, vec, w_in, w_in, w_out)

    def second(x_ref, h_ref, uh_ref, vec_ref, wg_ref, wu_ref, wo_ref, a_in, xo_ref, a_ref, u_ref):
        u = uh_ref[...] + hidden(h_ref[...], wg_ref, wu_ref, wo_ref, a_ref)
        u_ref[...] = u
        uhat, _ = _rms(u)
        xo_ref[...] = x_ref[...] + (weight * (1.0 + vec_ref[4:5, :])) * (uhat * vec_ref[1:2, :])

    xo, a, u = pl.pallas_call(
        second, name="ffn_fwd_second", grid=(S // tm,),
        in_specs=[pl.BlockSpec((tm, D), row), pl.BlockSpec((tm, D), row), pl.BlockSpec((tm, D), row)] + half(1) + [_ANY],
        out_specs=[pl.BlockSpec((tm, D), row), a_spec(1), pl.BlockSpec((tm, D), row)],
        out_shape=[jax.ShapeDtypeStruct((S, D), F32), a_shape, jax.ShapeDtypeStruct((S, D), F32)],
        input_output_aliases={7: 1},
        compiler_params=_params("parallel"),
    )(x, h, u_half, vec, w_in, w_in, w_out, a)
    return xo, a, u, h


def ffn_bwd(dout, x, u, a, vec, w_in, w_out, weight):
    S = x.shape[0]
    tm = min(512, S)
    row = lambda i: (i, 0)
    half = lambda j: [pl.BlockSpec((2, tm, FSH), lambda i: (0, i, j)), _w3((8, D)),
                      pl.BlockSpec((None, D, FSH), lambda i: (j, 0, 0)),
                      pl.BlockSpec((None, D, FSH), lambda i: (j + 2, 0, 0)),
                      pl.BlockSpec((None, FSH, D), lambda i: (j, 0, 0))]
    half_out = lambda j: [pl.BlockSpec((tm, FSH), lambda i: (i, j)), pl.BlockSpec((2, tm, FSH), lambda i: (0, i, j))]
    half_shape = [jax.ShapeDtypeStruct((S, DFF), BF16), jax.ShapeDtypeStruct((2, S, DFF), BF16)]

    def hidden_bwd(du, a_ref, wg_ref, wu_ref, wo_ref, act_ref, da_ref):
        dact = _dot(du, wo_ref[...], NT)
        g = a_ref[0].astype(F32)
        up = a_ref[1].astype(F32)
        s = jax.nn.sigmoid(g)
        silu = g * s
        act_ref[...] = (silu * up).astype(BF16)
        dg = (dact * up * (s * (1.0 + g * (1.0 - s)))).astype(BF16)
        dup = (dact * silu).astype(BF16)
        da_ref[0] = dg
        da_ref[1] = dup
        return _dot(dg, wg_ref[...], NT) + _dot(dup, wu_ref[...], NT)

    def first(do_ref, u_ref, a_ref, vec_ref, wg_ref, wu_ref, wo_ref, du_ref, dh_ref, act_ref, da_ref, vg_ref):
        @pl.when(pl.program_id(0) == 0)
        def _():
            vg_ref[...] = jnp.zeros_like(vg_ref)

        du, dgate_rows, dgpost_rows = _postnorm_bwd(do_ref[...], u_ref[...], vec_ref, weight)
        vg_ref[2:3, :] += jnp.sum(dgate_rows, axis=0, keepdims=True)
        vg_ref[4:5, :] += jnp.sum(dgpost_rows, axis=0, keepdims=True)
        du = du.astype(BF16)
        du_ref[...] = du
        dh_ref[...] = hidden_bwd(du, a_ref, wg_ref, wu_ref, wo_ref, act_ref, da_ref)

    du, dh, act, da, vg_post = pl.pallas_call(
        first, name="ffn_bwd_first", grid=(S // tm,),
        in_specs=[pl.BlockSpec((tm, D), row), pl.BlockSpec((tm, D), row)] + half(0),
        out_specs=[pl.BlockSpec((tm, D), row), pl.BlockSpec((tm, D), row)] + half_out(0) + [_w3((8, D))],
        out_shape=[jax.ShapeDtypeStruct((S, D), BF16), jax.ShapeDtypeStruct((S, D), F32)] + half_shape
        + [jax.ShapeDtypeStruct((8, D), F32)],
        compiler_params=_params("arbitrary"),
    )(dout, u, a, vec, w_in, w_in, w_out)

    def second(do_ref, x_ref, du_ref, dh_ref, a_ref, vec_ref, wg_ref, wu_ref, wo_ref, act_in, da_in,
               dx_ref, act_ref, da_ref, vg_ref):
        @pl.when(pl.program_id(0) == 0)
        def _():
            vg_ref[...] = jnp.zeros_like(vg_ref)

        dh = dh_ref[...] + hidden_bwd(du_ref[...], a_ref, wg_ref, wu_ref, wo_ref, act_ref, da_ref)
        dx_ref[...] = do_ref[...] + _prenorm_bwd(dh, x_ref[...], vec_ref, vg_ref)

    dx, act, da, vg_pre = pl.pallas_call(
        second, name="ffn_bwd_second", grid=(S // tm,),
        in_specs=[pl.BlockSpec((tm, D), row), pl.BlockSpec((tm, D), row), pl.BlockSpec((tm, D), row),
                  pl.BlockSpec((tm, D), row)] + half(1) + [_ANY, _ANY],
        out_specs=[pl.BlockSpec((tm, D), row)] + half_out(1) + [_w3((8, D))],
        out_shape=[jax.ShapeDtypeStruct((S, D), F32)] + half_shape + [jax.ShapeDtypeStruct((8, D), F32)],
        input_output_aliases={9: 1, 10: 2},
        compiler_params=_params("arbitrary"),
    )(dout, x, du, dh, a, vec, w_in, w_in, w_out, act, da)
    return dx, du, act, da, vg_post + vg_pre


def dw_matmul(name, a, b, a_spec, b_spec, out_shape, out_spec, grid):
    def body(a_ref, b_ref, o_ref):
        @pl.when(pl.program_id(len(grid) - 1) == 0)
        def _():
            o_ref[...] = jnp.zeros_like(o_ref)

        o_ref[...] += _dot(a_ref[...], b_ref[...], TN)

    return pl.pallas_call(
        body, name=name, grid=grid, in_specs=[a_spec, b_spec], out_specs=out_spec,
        out_shape=jax.ShapeDtypeStruct(out_shape, F32),
        compiler_params=_params(*(["parallel"] * (len(grid) - 1) + ["arbitrary"])),
    )(a, b)


def ffn_dw(h, da, act, du):
    S = h.shape[0]
    tk = min(DW_TK, S)
    dw_in = dw_matmul("ffn_dw_in", h, da,
                      pl.BlockSpec((tk, D), lambda n, k: (k, 0)),
                      pl.BlockSpec((None, tk, FSH), lambda n, k: (n // 2, k, n % 2)),
                      (N_CHIP, D, FSH), pl.BlockSpec((None, D, FSH), lambda n, k: (n, 0, 0)),
                      (N_CHIP, S // tk))
    dw_out = dw_matmul("ffn_dw_out", act, du,
                       pl.BlockSpec((tk, FSH), lambda n, k: (k, n)),
                       pl.BlockSpec((tk, D), lambda n, k: (k, 0)),
                       (DFF, D), pl.BlockSpec((FSH, D), lambda n, k: (n, 0)),
                       (2, S // tk))
    return dw_in, dw_out


def _halo_specs(tm, S):
    nb = tm // HALO
    last = S // HALO - 1
    return [pl.BlockSpec((HALO, D), lambda i: (jnp.maximum(i * nb - 1, 0), 0)),
            pl.BlockSpec((tm, D), lambda i: (i, 0)),
            pl.BlockSpec((HALO, D), lambda i: (jnp.minimum((i + 1) * nb, last), 0))]


def _shift_rows(v, k):
    return pltpu.roll(v, k % v.shape[0], 0)


def _window_sum(v, g, forward):
    acc = v + _shift_rows(v, 1 if forward else -1)
    for step in (1, 2, 4)[:g]:
        acc = _shift_rows(acc, step) + _shift_rows(acc, -step)
    return acc


def _pool_count(t, w, S):
    return jnp.maximum(jnp.minimum(t + w // 2, S) - jnp.maximum(t - w // 2, 0), 1).astype(F32)


def pool_fwd(x, vec, pw, pvec):
    S = x.shape[0]
    tm = min(ROW_TILE, S)
    G = D // 4

    def body(xp_ref, x_ref, xn_ref, vec_ref, pw_ref, pv_ref, xo_ref, y_ref, z_ref):
        i = pl.program_id(0)
        xa = jnp.concatenate([xp_ref[...], x_ref[...], xn_ref[...]], axis=0)
        t = i * tm - HALO + lax.broadcasted_iota(jnp.int32, (tm + 2 * HALO, 1), 0)
        h, _, _ = _prenorm(xa, vec_ref)
        h = jnp.where((t >= 0) & (t < S), h, 0.0)
        tmain = t[HALO:HALO + tm]
        for g in range(4):
            hg = h[:, g * G:(g + 1) * G]
            pooled = _window_sum(hg, g, True)[HALO:HALO + tm] / _pool_count(tmain, POOL_WINDOWS[g], S)
            z = (pooled - hg[HALO:HALO + tm]).astype(BF16)
            z_ref[:, g * G:(g + 1) * G] = z
            y_ref[:, g * G:(g + 1) * G] = _dot(z, pw_ref[g]) + pv_ref[0:1, g * G:(g + 1) * G]
        u = y_ref[...] * pv_ref[1:2, :]
        uhat, _ = _rms(u)
        xo_ref[...] = x_ref[...] + (1.0 + vec_ref[4:5, :]) * (uhat * vec_ref[1:2, :])

    row = lambda i: (i, 0)
    full = lambda i: (0, 0)
    return pl.pallas_call(
        body, name="pool_fwd", grid=(S // tm,),
        in_specs=_halo_specs(tm, S) + [pl.BlockSpec((8, D), full), pl.BlockSpec((4, G, G), lambda i: (0, 0, 0)),
                                       pl.BlockSpec((8, D), full)],
        out_specs=[pl.BlockSpec((tm, D), row)] * 3,
        out_shape=[jax.ShapeDtypeStruct((S, D), F32), jax.ShapeDtypeStruct((S, D), F32),
                   jax.ShapeDtypeStruct((S, D), BF16)],
        compiler_params=_params("parallel"),
    )(x, x, x, vec, pw, pvec)


def pool_bwd(dout, x, y, z, vec, pw, pvec):
    S = x.shape[0]
    tm = min(ROW_TILE, S)
    G = D // 4
    R = G // N_CHIP

    def body(dop_ref, do_ref, don_ref, yp_ref, y_ref, yn_ref, x_ref, z_ref, vec_ref, pw_ref, pv_ref,
             dx_ref, vg_ref, pg_ref, dw_ref, dh_ref):
        i = pl.program_id(0)

        @pl.when(i == 0)
        def _():
            vg_ref[...] = jnp.zeros_like(vg_ref)
            pg_ref[...] = jnp.zeros_like(pg_ref)
            dw_ref[...] = jnp.zeros_like(dw_ref)

        doa = jnp.concatenate([dop_ref[...], do_ref[...], don_ref[...]], axis=0)
        ya = jnp.concatenate([yp_ref[...], y_ref[...], yn_ref[...]], axis=0)
        t = i * tm - HALO + lax.broadcasted_iota(jnp.int32, (tm + 2 * HALO, 1), 0)
        inside = (t >= 0) & (t < S)
        main = (t >= i * tm) & (t < (i + 1) * tm)
        du, dgate_rows, dgpost_rows = _postnorm_bwd(doa, ya * pv_ref[1:2, :], vec_ref, 1.0)
        du = jnp.where(inside, du, 0.0)
        vg_ref[2:3, :] += jnp.sum(jnp.where(main, dgate_rows, 0.0), axis=0, keepdims=True)
        vg_ref[4:5, :] += jnp.sum(jnp.where(main, dgpost_rows, 0.0), axis=0, keepdims=True)
        dy = du * pv_ref[1:2, :]
        pg_ref[0:1, :] += jnp.sum(jnp.where(main, dy, 0.0), axis=0, keepdims=True)
        pg_ref[1:2, :] += jnp.sum(jnp.where(main, du * ya, 0.0), axis=0, keepdims=True)
        for g in range(4):
            dyg = dy[:, g * G:(g + 1) * G].astype(BF16)
            dz = _dot(dyg, pw_ref[g], NT)
            e = dz / _pool_count(t, POOL_WINDOWS[g], S)
            dh_ref[:, g * G:(g + 1) * G] = (_window_sum(e, g, False) - dz)[HALO:HALO + tm]
            dwg = _dot(z_ref[:, g * G:(g + 1) * G], dyg[HALO:HALO + tm], TN)
            for q in range(N_CHIP):
                dw_ref[q, g] += dwg[q * R:(q + 1) * R, :]
        dx_ref[...] = do_ref[...] + _prenorm_bwd(dh_ref[...], x_ref[...], vec_ref, vg_ref)

    row = lambda i: (i, 0)
    full = lambda i: (0, 0)
    halo = _halo_specs(tm, S)
    return pl.pallas_call(
        body, name="pool_bwd", grid=(S // tm,),
        in_specs=halo + halo + [pl.BlockSpec((tm, D), row), pl.BlockSpec((tm, D), row), pl.BlockSpec((8, D), full),
                                pl.BlockSpec((4, G, G), lambda i: (0, 0, 0)), pl.BlockSpec((8, D), full)],
        out_specs=[pl.BlockSpec((tm, D), row), pl.BlockSpec((8, D), full), pl.BlockSpec((8, D), full),
                   pl.BlockSpec((N_CHIP, 4, R, G), lambda i: (0, 0, 0, 0))],
        out_shape=[jax.ShapeDtypeStruct((S, D), F32), jax.ShapeDtypeStruct((8, D), F32),
                   jax.ShapeDtypeStruct((8, D), F32), jax.ShapeDtypeStruct((N_CHIP, 4, R, G), F32)],
        scratch_shapes=[pltpu.VMEM((tm, D), F32)],
        compiler_params=_params("arbitrary"),
    )(dout, dout, dout, y, y, y, x, z, vec, pw, pvec)


N_PAIR = N_HEADS // 2
SLOTS = LANES // ROPE
ROPE_ALL = N_HEADS * ROPE
NOPE_ALL = N_HEADS * NOPE
LAT_ALL = N_HEADS * KVL
DLAT = QL + KVL + 2 * LANES
DQ_ALL = NOPE_ALL + 2 * ROPE_ALL


def _w3(shape):
    return pl.BlockSpec(shape, lambda i: (0,) * len(shape))


def _slot_mask(hd, rows):
    lane = lax.broadcasted_iota(jnp.int32, (rows, LANES), 1)
    return (lane // ROPE) == (hd % SLOTS)


MLA_WEIGHTS = ("wq", "wkv", "wkr4", "wkrs4", "qn", "kvn", "wn", "wr", "wrs", "bduk")


def _mla_weight_specs():
    return [_w3((D, QL)), _w3((D, KVL)), _w3((D, LANES)), _w3((D, LANES)), _w3((1, QL)), _w3((1, KVL)),
            _w3((QL, NOPE_ALL)), _w3((QL, ROPE_ALL)), _w3((QL, ROPE_ALL)), _w3((N_PAIR, 2 * NOPE, 2 * KVL))]


def mla_pre(x, vec, mw, tabs):
    S = x.shape[0]
    tm = min(ROW_TILE, S)

    def body(x_ref, vec_ref, cos_ref, sin_ref, wq_ref, wkv_ref, wkr_ref, wkrs_ref, qn_ref, kvn_ref,
             wn_ref, wr_ref, wrs_ref, bduk_ref,
             h_ref, cq_ref, ckv_ref, cqn_ref, qnope_ref, qcat_ref, kcat_ref, vcat_ref):
        h, _, _ = _prenorm(x_ref[...], vec_ref)
        hb = h.astype(BF16)
        h_ref[...] = hb
        cq_raw = _dot(hb, wq_ref[...])
        ckv_raw = _dot(hb, wkv_ref[...])
        cq_ref[...] = cq_raw
        ckv_ref[...] = ckv_raw
        cos, sin = cos_ref[...], sin_ref[...]
        ckv = (_rms(ckv_raw)[0] * kvn_ref[...]).astype(BF16)
        kcat_ref[:, 0:KVL] = ckv
        kcat_ref[:, KVL:] = (_dot(hb, wkr_ref[...]) * cos + _dot(hb, wkrs_ref[...]) * sin).astype(BF16)
        vcat_ref[:, 0:KVL] = ckv
        ones = lax.broadcasted_iota(jnp.int32, (tm, QPAD - KVL), 1) == 0
        vcat_ref[:, KVL:] = jnp.where(ones, 1.0, 0.0).astype(BF16)
        cqb = (_rms(cq_raw)[0] * qn_ref[...]).astype(BF16)
        cqn_ref[...] = cqb
        qn = _dot(cqb, wn_ref[...]).astype(BF16)
        qnope_ref[...] = qn
        cos4, sin4 = jnp.tile(cos, (1, SLOTS)), jnp.tile(sin, (1, SLOTS))
        qr = ((_dot(cqb, wr_ref[...]) * cos4 + _dot(cqb, wrs_ref[...]) * sin4) * ATTN_SCALE).astype(BF16)
        for j in range(N_PAIR):
            ql = (_dot(qn[:, 2 * NOPE * j:2 * NOPE * (j + 1)], bduk_ref[j]) * ATTN_SCALE).astype(BF16)
            for hd in (2 * j, 2 * j + 1):
                qcat_ref[hd, :, 0:KVL] = ql[:, KVL * (hd - 2 * j):KVL * (hd - 2 * j + 1)]
                group = qr[:, LANES * (hd // SLOTS):LANES * (hd // SLOTS + 1)]
                qcat_ref[hd, :, KVL:] = jnp.where(_slot_mask(hd, tm), group, jnp.zeros_like(group))

    row = lambda i: (i, 0)
    hrow = lambda i: (0, i, 0)
    return pl.pallas_call(
        body, name="mla_pre", grid=(S // tm,),
        in_specs=[pl.BlockSpec((tm, D), row), _w3((8, D)), pl.BlockSpec((tm, LANES), row), pl.BlockSpec((tm, LANES), row)]
        + _mla_weight_specs(),
        out_specs=[pl.BlockSpec((tm, D), row), pl.BlockSpec((tm, QL), row), pl.BlockSpec((tm, KVL), row),
                   pl.BlockSpec((tm, QL), row), pl.BlockSpec((tm, NOPE_ALL), row),
                   pl.BlockSpec((N_HEADS, tm, QPAD), hrow), pl.BlockSpec((tm, QPAD), row),
                   pl.BlockSpec((tm, QPAD), row)],
        out_shape=[jax.ShapeDtypeStruct((S, D), BF16), jax.ShapeDtypeStruct((S, QL), F32),
                   jax.ShapeDtypeStruct((S, KVL), F32), jax.ShapeDtypeStruct((S, QL), BF16),
                   jax.ShapeDtypeStruct((S, NOPE_ALL), BF16), jax.ShapeDtypeStruct((N_HEADS, S, QPAD), BF16),
                   jax.ShapeDtypeStruct((S, QPAD), BF16), jax.ShapeDtypeStruct((S, QPAD), BF16)],
        compiler_params=_params("parallel"),
    )(x, vec, tabs[0], tabs[1], *[mw[k] for k in MLA_WEIGHTS])


def attn_fwd(qcat, kcat, vcat):
    S = kcat.shape[0]
    tq = min(ATTN_TQ, S)
    kc = min(ATTN_KC, S)

    def body(q_ref, k_ref, v_ref, o_ref, lse_ref):
        q = q_ref[...]
        m = jnp.full((tq, 1), -jnp.inf, F32)
        ov = jnp.zeros((tq, QPAD), F32)
        for c in range(S // kc):
            s = _dot(q, k_ref[c * kc:(c + 1) * kc, :], NT)
            m_new = jnp.maximum(m, jnp.max(s, axis=-1, keepdims=True))
            p = jnp.exp(s - m_new).astype(BF16)
            ov = ov * jnp.exp(m - m_new) + _dot(p, v_ref[c * kc:(c + 1) * kc, :])
            m = m_new
        l = ov[:, KVL:KVL + 1]
        o_ref[...] = (ov[:, 0:KVL] * (1.0 / l)).astype(BF16)
        lse_ref[...] = _as_row(m + jnp.log(l))

    return pl.pallas_call(
        body, name="attn_fwd", grid=(N_HEADS, S // tq),
        in_specs=[pl.BlockSpec((None, tq, QPAD), lambda h, i: (h, i, 0)),
                  pl.BlockSpec((S, QPAD), lambda h, i: (0, 0)),
                  pl.BlockSpec((S, QPAD), lambda h, i: (0, 0))],
        out_specs=[pl.BlockSpec((tq, KVL), lambda h, i: (i, h)),
                   pl.BlockSpec((None, 1, tq), lambda h, i: (h, 0, i))],
        out_shape=[jax.ShapeDtypeStruct((S, LAT_ALL), BF16), jax.ShapeDtypeStruct((N_HEADS, 1, S), F32)],
        compiler_params=_params("parallel", "parallel"),
    )(qcat, kcat, vcat)


def mla_post(olat, x, vec, bduv, wo):
    S = x.shape[0]
    tm = min(ROW_TILE, S)

    def body(o_ref, x_ref, vec_ref, bduv_ref, wo_ref, xo_ref, u_ref, ocat_ref):
        for j in range(N_PAIR):
            oc = _dot(o_ref[:, 2 * KVL * j:2 * KVL * (j + 1)], bduv_ref[j])
            ocat_ref[:, 2 * VH * j:2 * VH * (j + 1)] = oc.astype(BF16)
        u = _dot(ocat_ref[...], wo_ref[...])
        u_ref[...] = u
        uhat, _ = _rms(u)
        xo_ref[...] = x_ref[...] + (1.0 + vec_ref[4:5, :]) * (uhat * vec_ref[1:2, :])

    row = lambda i: (i, 0)
    return pl.pallas_call(
        body, name="mla_post", grid=(S // tm,),
        in_specs=[pl.BlockSpec((tm, LAT_ALL), row), pl.BlockSpec((tm, D), row), _w3((8, D)),
                  _w3((N_PAIR, 2 * KVL, 2 * VH)), _w3((D, D))],
        out_specs=[pl.BlockSpec((tm, D), row), pl.BlockSpec((tm, D), row), pl.BlockSpec((tm, D), row)],
        out_shape=[jax.ShapeDtypeStruct((S, D), F32), jax.ShapeDtypeStruct((S, D), F32),
                   jax.ShapeDtypeStruct((S, D), BF16)],
        compiler_params=_params("parallel"),
    )(olat, x, vec, bduv, wo)


def mla_post_bwd(dout, u, olat, vec, bduv, wo):
    S = u.shape[0]
    tm = min(ROW_TILE, S)

    def body(do_ref, u_ref, o_ref, vec_ref, bduv_ref, wo_ref, du_ref, docat_ref, dolat_ref, delta_ref, vg_ref):
        @pl.when(pl.program_id(0) == 0)
        def _():
            vg_ref[...] = jnp.zeros_like(vg_ref)

        du, dgate_rows, dgpost_rows = _postnorm_bwd(do_ref[...], u_ref[...], vec_ref, 1.0)
        vg_ref[2:3, :] += jnp.sum(dgate_rows, axis=0, keepdims=True)
        vg_ref[4:5, :] += jnp.sum(dgpost_rows, axis=0, keepdims=True)
        dub = du.astype(BF16)
        du_ref[...] = dub
        docat_ref[...] = _dot(dub, wo_ref[...], NT).astype(BF16)
        for j in range(N_PAIR):
            dol = _dot(docat_ref[:, 2 * VH * j:2 * VH * (j + 1)], bduv_ref[j], NT).astype(BF16)
            dolat_ref[:, 2 * KVL * j:2 * KVL * (j + 1)] = dol
            prod = dol.astype(F32) * o_ref[:, 2 * KVL * j:2 * KVL * (j + 1)].astype(F32)
            delta_ref[2 * j] = _as_row(jnp.sum(prod[:, 0:KVL], axis=-1, keepdims=True))
            delta_ref[2 * j + 1] = _as_row(jnp.sum(prod[:, KVL:], axis=-1, keepdims=True))

    row = lambda i: (i, 0)
    hrow = lambda i: (0, i, 0)
    return pl.pallas_call(
        body, name="mla_post_bwd", grid=(S // tm,),
        in_specs=[pl.BlockSpec((tm, D), row), pl.BlockSpec((tm, D), row), pl.BlockSpec((tm, LAT_ALL), row),
                  _w3((8, D)), _w3((N_PAIR, 2 * KVL, 2 * VH)), _w3((D, D))],
        out_specs=[pl.BlockSpec((tm, D), row), pl.BlockSpec((tm, D), row),
                   pl.BlockSpec((tm, LAT_ALL), row), pl.BlockSpec((N_HEADS, 1, tm), lambda i: (0, 0, i)), _w3((8, D))],
        out_shape=[jax.ShapeDtypeStruct((S, D), BF16), jax.ShapeDtypeStruct((S, D), BF16),
                   jax.ShapeDtypeStruct((S, LAT_ALL), BF16), jax.ShapeDtypeStruct((N_HEADS, 1, S), F32),
                   jax.ShapeDtypeStruct((8, D), F32)],
        compiler_params=_params("arbitrary"),
    )(dout, u, olat, vec, bduv, wo)


def attn_bwd(qcat, kcat, kcat_t, dolat, lse_row, delta_row):
    S = kcat.shape[0]
    tq = min(ATTN_TQ, S)
    kc = min(ATTN_KC, S)

    def body(q_ref, k_ref, kt_ref, do_ref, lse_ref, dl_ref, dq_ref, dk_ref, dv_ref):
        @pl.when((pl.program_id(0) == 0) & (pl.program_id(1) == 0))
        def _():
            dk_ref[...] = jnp.zeros_like(dk_ref)
            dv_ref[...] = jnp.zeros_like(dv_ref)

        q, do = q_ref[...], do_ref[...]
        lse, dl = lse_ref[...], dl_ref[...]
        dqt = jnp.zeros((QPAD, tq), F32)
        for c in range(S // kc):
            rows = slice(c * kc, (c + 1) * kc)
            st = _dot(k_ref[rows, :], q, NT)
            pt = jnp.exp(st - lse)
            dpt = _dot(k_ref[rows, 0:KVL], do, NT)
            dst = (pt * (dpt - dl)).astype(BF16)
            dv_ref[rows, :] += _dot(pt.astype(BF16), do)
            dk_ref[rows, :] += _dot(dst, q)
            dqt = dqt + _dot(kt_ref[:, rows], dst)
        dq_ref[...] = (dqt.T * ATTN_SCALE).astype(BF16)

    return pl.pallas_call(
        body, name="attn_bwd", grid=(N_HEADS, S // tq),
        in_specs=[pl.BlockSpec((None, tq, QPAD), lambda h, i: (h, i, 0)),
                  pl.BlockSpec((S, QPAD), lambda h, i: (0, 0)),
                  pl.BlockSpec((QPAD, S), lambda h, i: (0, 0)),
                  pl.BlockSpec((tq, KVL), lambda h, i: (i, h)),
                  pl.BlockSpec((None, 1, tq), lambda h, i: (h, 0, i)),
                  pl.BlockSpec((None, 1, tq), lambda h, i: (h, 0, i))],
        out_specs=[pl.BlockSpec((None, tq, QPAD), lambda h, i: (h, i, 0)),
                   pl.BlockSpec((S, QPAD), lambda h, i: (0, 0)),
                   pl.BlockSpec((S, KVL), lambda h, i: (0, 0))],
        out_shape=[jax.ShapeDtypeStruct((N_HEADS, S, QPAD), BF16), jax.ShapeDtypeStruct((S, QPAD), F32),
                   jax.ShapeDtypeStruct((S, KVL), F32)],
        compiler_params=_params("arbitrary", "arbitrary"),
    )(qcat, kcat, kcat_t, dolat, lse_row, delta_row)


def mla_pre_bwd(dout, dq, dk, dv, x, cq_raw, ckv_raw, vec, mw, tabs):
    S = x.shape[0]
    tm = min(ROW_TILE, S)

    def body(do_ref, dq_ref, dk_ref, dv_ref, x_ref, cq_ref, ckv_ref, vec_ref, cos_ref, sin_ref,
             wq_ref, wkv_ref, wkr_ref, wkrs_ref, qn_ref, kvn_ref, wn_ref, wr_ref, wrs_ref, bduk_ref,
             dx_ref, dlat_ref, dql_ref, dqcat_ref, vg_ref, ng_ref):
        @pl.when(pl.program_id(0) == 0)
        def _():
            vg_ref[...] = jnp.zeros_like(vg_ref)
            ng_ref[...] = jnp.zeros_like(ng_ref)

        cos, sin = cos_ref[...], sin_ref[...]
        for j in range(N_PAIR):
            dql = jnp.concatenate([dq_ref[2 * j, :, 0:KVL], dq_ref[2 * j + 1, :, 0:KVL]], axis=1)
            dql_ref[:, 2 * KVL * j:2 * KVL * (j + 1)] = dql
            dqcat_ref[:, 2 * NOPE * j:2 * NOPE * (j + 1)] = _dot(dql, bduk_ref[j], NT).astype(BF16)
        groups = []
        for grp in range(N_HEADS // SLOTS):
            acc = jnp.zeros((tm, LANES), F32)
            for hd in range(SLOTS * grp, SLOTS * (grp + 1)):
                acc = acc + jnp.where(_slot_mask(hd, tm), dq_ref[hd, :, KVL:].astype(F32), 0.0)
            groups.append(acc)
        dqr = jnp.concatenate(groups, axis=1)
        qa = (dqr * jnp.tile(cos, (1, SLOTS))).astype(BF16)
        qb = (dqr * jnp.tile(sin, (1, SLOTS))).astype(BF16)
        dqcat_ref[:, NOPE_ALL:NOPE_ALL + ROPE_ALL] = qa
        dqcat_ref[:, NOPE_ALL + ROPE_ALL:] = qb
        dcq = _dot(dqcat_ref[:, 0:NOPE_ALL], wn_ref[...], NT) + _dot(qa, wr_ref[...], NT) + _dot(qb, wrs_ref[...], NT)
        cqh, rq = _rms(cq_ref[...])
        ng_ref[0:1, :] += jnp.sum(dcq * cqh, axis=0, keepdims=True)
        dcq_raw = _rms_bwd(cqh, rq, dcq * qn_ref[...]).astype(BF16)
        dckv = dk_ref[:, 0:KVL] + dv_ref[...]
        ckvh, rk = _rms(ckv_ref[...])
        ng_ref[1:2, 0:KVL] += jnp.sum(dckv * ckvh, axis=0, keepdims=True)
        dckv_raw = _rms_bwd(ckvh, rk, dckv * kvn_ref[...]).astype(BF16)
        dkr = dk_ref[:, KVL:]
        ka = (dkr * cos).astype(BF16)
        kb = (dkr * sin).astype(BF16)
        dlat_ref[:, 0:QL] = dcq_raw
        dlat_ref[:, QL:QL + KVL] = dckv_raw
        dlat_ref[:, QL + KVL:QL + KVL + LANES] = ka
        dlat_ref[:, QL + KVL + LANES:] = kb
        dh = (_dot(dcq_raw, wq_ref[...], NT) + _dot(dckv_raw, wkv_ref[...], NT)
              + _dot(ka, wkr_ref[...], NT) + _dot(kb, wkrs_ref[...], NT))
        dx_ref[...] = do_ref[...] + _prenorm_bwd(dh, x_ref[...], vec_ref, vg_ref)

    row = lambda i: (i, 0)
    hrow = lambda i: (0, i, 0)
    return pl.pallas_call(
        body, name="mla_pre_bwd", grid=(S // tm,),
        in_specs=[pl.BlockSpec((tm, D), row), pl.BlockSpec((N_HEADS, tm, QPAD), hrow), pl.BlockSpec((tm, QPAD), row),
                  pl.BlockSpec((tm, KVL), row), pl.BlockSpec((tm, D), row), pl.BlockSpec((tm, QL), row),
                  pl.BlockSpec((tm, KVL), row), _w3((8, D)), pl.BlockSpec((tm, LANES), row), pl.BlockSpec((tm, LANES), row)]
        + _mla_weight_specs(),
        out_specs=[pl.BlockSpec((tm, D), row), pl.BlockSpec((tm, DLAT), row), pl.BlockSpec((tm, LAT_ALL), row),
                   pl.BlockSpec((tm, DQ_ALL), row), _w3((8, D)), _w3((8, QL))],
        out_shape=[jax.ShapeDtypeStruct((S, D), F32), jax.ShapeDtypeStruct((S, DLAT), BF16),
                   jax.ShapeDtypeStruct((S, LAT_ALL), BF16), jax.ShapeDtypeStruct((S, DQ_ALL), BF16),
                   jax.ShapeDtypeStruct((8, D), F32), jax.ShapeDtypeStruct((8, QL), F32)],
        compiler_params=_params("arbitrary"),
    )(dout, dq, dk, dv, x, cq_raw, ckv_raw, vec, tabs[0], tabs[1], *[mw[k] for k in MLA_WEIGHTS])


def mla_dw(h, dlat, cqn, dqcat, dql, qnope, olat, docat, ocat, du):
    S = h.shape[0]
    tk = min(DW_TK, S)
    nk = S // tk
    flat = lambda w: pl.BlockSpec((tk, w), lambda k: (k, 0))
    cols = lambda w: pl.BlockSpec((tk, w), lambda n, k: (k, n))
    pair_o = pl.BlockSpec((None, 2 * KVL, 2 * NOPE), lambda n, k: (n, 0, 0))
    g = {}
    g["in"] = dw_matmul("mla_dw_in", h, dlat, flat(D), flat(DLAT), (D, DLAT),
                        pl.BlockSpec((D, DLAT), lambda k: (0, 0)), (nk,))
    g["q"] = dw_matmul("mla_dw_q", cqn, dqcat, flat(QL), flat(DQ_ALL), (QL, DQ_ALL),
                       pl.BlockSpec((QL, DQ_ALL), lambda k: (0, 0)), (nk,))
    g["uk"] = dw_matmul("mla_dw_uk", dql, qnope, cols(2 * KVL), cols(2 * NOPE), (N_PAIR, 2 * KVL, 2 * NOPE), pair_o,
                        (N_PAIR, nk))
    g["uv"] = dw_matmul("mla_dw_uv", olat, docat, cols(2 * KVL), cols(2 * VH), (N_PAIR, 2 * KVL, 2 * VH), pair_o,
                        (N_PAIR, nk))
    g["o"] = dw_matmul("mla_dw_o", ocat, du, cols(256), pl.BlockSpec((tk, D), lambda n, k: (k, 0)), (D, D),
                       pl.BlockSpec((256, D), lambda n, k: (n, 0)), (D // 256, nk))
    return g


def loss_head(y, target):
    S = y.shape[0]
    tm = min(2 * ROW_TILE, S)

    def body(y_ref, t_ref, loss_ref, dy_ref):
        @pl.when(pl.program_id(0) == 0)
        def _():
            loss_ref[...] = jnp.zeros_like(loss_ref)

        err = y_ref[...] - t_ref[...]
        dy_ref[...] = err * (1.0 / D)
        loss_ref[...] += 0.5 * jnp.sum(jnp.mean(err * err, axis=-1, keepdims=True), axis=0, keepdims=True)

    row = lambda i: (i, 0)
    return pl.pallas_call(
        body, name="loss_head", grid=(S // tm,),
        in_specs=[pl.BlockSpec((tm, D), row), pl.BlockSpec((tm, D), row)],
        out_specs=[pl.BlockSpec((1, 1), lambda i: (0, 0)), pl.BlockSpec((tm, D), row)],
        out_shape=[jax.ShapeDtypeStruct((1, 1), F32), jax.ShapeDtypeStruct((S, D), F32)],
        compiler_params=_params("arbitrary"),
    )(y, target)


MOD_COLS = 9 * D // N_CHIP


def mod_fwd(c_pad, ada_w, ada_b_loc):
    tn = MOD_COLS // 3

    def body(c_ref, w_ref, b_ref, o_ref):
        c = c_ref[...]
        sc = (c * jax.nn.sigmoid(c)).astype(BF16)
        o_ref[...] = _dot(sc, w_ref[...].astype(BF16)) + b_ref[...]

    return pl.pallas_call(
        body, name="mod_fwd", grid=(2, 3),
        in_specs=[pl.BlockSpec((16, D), lambda i, n: (0, 0)), pl.BlockSpec((None, D, tn), lambda i, n: (i, 0, n)),
                  pl.BlockSpec((None, 1, tn), lambda i, n: (i, 0, n))],
        out_specs=pl.BlockSpec((None, 16, tn), lambda i, n: (i, 0, n)),
        out_shape=jax.ShapeDtypeStruct((2, 16, MOD_COLS), F32),
        compiler_params=_params("parallel", "parallel"),
    )(c_pad, ada_w, ada_b_loc)


def _adamw_math(w, g, m, v):
    m = ADAM_B1 * m + (1.0 - ADAM_B1) * g
    v = ADAM_B2 * v + (1.0 - ADAM_B2) * (g * g)
    m_hat = m / (1.0 - ADAM_B1 ** ADAM_STEP)
    v_hat = v / (1.0 - ADAM_B2 ** ADAM_STEP)
    delta = -ADAM_LR * (m_hat / (jnp.sqrt(v_hat) + ADAM_EPS) + ADAM_WD * w)
    return delta, m, v


def adamw(name, w, g, m, v, part=None, prev=None, copy_grad=False):
    shape = w.shape
    if part is None and w.size * 4 <= (1 << 20):
        whole = pl.BlockSpec(shape, lambda i: (0,) * len(shape))

        def small_body(w_ref, g_ref, m_ref, v_ref, d_ref, mo_ref, vo_ref):
            d_ref[...], mo_ref[...], vo_ref[...] = _adamw_math(w_ref[...], g_ref[...], m_ref[...], v_ref[...])

        return pl.pallas_call(
            small_body, name=name, grid=(1,), in_specs=[whole] * 4, out_specs=[whole] * 3,
            out_shape=[jax.ShapeDtypeStruct(shape, F32)] * 3, compiler_params=_params("arbitrary"),
        )(w, g, m, v)
    cols = shape[-1]
    rows = w.size // cols
    per_entry = rows // shape[0] if part is not None else rows
    tr = per_entry
    budget_rows = (2 << 20) // (cols * 4)
    for cand in range(min(per_entry, budget_rows) // 8 * 8, 0, -8):
        if per_entry % cand == 0:
            tr = cand
            break
    first, count = part if part is not None else (0, 1)
    tiles = per_entry // tr

    n_out = 4 if copy_grad else 3

    def body(w_ref, g_ref, m_ref, v_ref, *rest):
        outs = rest[-n_out:]
        outs[0][...], outs[1][...], outs[2][...] = _adamw_math(w_ref[...], g_ref[...], m_ref[...], v_ref[...])
        if copy_grad:
            outs[3][...] = g_ref[...]

    spec = pl.BlockSpec((tr, cols), lambda i: (i + first * tiles, 0))
    operands = [a.reshape(rows, cols) for a in (w, g, m, v)]
    aliases = {}
    if prev is not None:
        operands += [p.reshape(rows, cols) for p in prev]
        aliases = {4 + t: t for t in range(n_out)}
    outs = pl.pallas_call(
        body, name=name, grid=(count * tiles,), in_specs=[spec] * 4 + [_ANY] * (len(operands) - 4),
        out_specs=[spec] * n_out, out_shape=[jax.ShapeDtypeStruct((rows, cols), F32)] * n_out,
        input_output_aliases=aliases, compiler_params=_params("parallel"),
    )(*operands)
    return [o.reshape(shape) for o in outs]


def adamw_ada(c_pad, dmod, w, m, v):
    tr = 256

    def body(c_ref, dm_ref, w_ref, m_ref, v_ref, g_ref, d_ref, mo_ref, vo_ref):
        c = c_ref[...]
        sc = (c * jax.nn.sigmoid(c)).astype(BF16)
        g = _dot(sc, dm_ref[...].astype(BF16), TN)
        g_ref[...] = g
        d_ref[...], mo_ref[...], vo_ref[...] = _adamw_math(w_ref[...], g, m_ref[...], v_ref[...])

    wspec = pl.BlockSpec((None, tr, MOD_COLS), lambda i, r: (i, r, 0))
    return pl.pallas_call(
        body, name="adamw_ada", grid=(2, D // tr),
        in_specs=[pl.BlockSpec((16, tr), lambda i, r: (0, r)),
                  pl.BlockSpec((None, 16, MOD_COLS), lambda i, r: (i, 0, 0)), wspec, wspec, wspec],
        out_specs=[wspec] * 4,
        out_shape=[jax.ShapeDtypeStruct((2, D, MOD_COLS), F32)] * 4,
        compiler_params=_params("parallel", "parallel"),
    )(c_pad, dmod, w, m, v)


def sum_devices(name, a):
    _, R, C = a.shape
    tr = R
    for cand in (64, 32, 16, 8):
        if R % cand == 0:
            tr = cand
            break

    def body(a_ref, o_ref):
        acc = a_ref[0]
        for dev in range(1, N_DEV):
            acc = acc + a_ref[dev]
        o_ref[...] = acc

    return pl.pallas_call(
        body, name=name, grid=(R // tr,),
        in_specs=[pl.BlockSpec((N_DEV, tr, C), lambda i: (0, i, 0))],
        out_specs=pl.BlockSpec((tr, C), lambda i: (i, 0)),
        out_shape=jax.ShapeDtypeStruct((R, C), F32),
        compiler_params=_params("parallel"),
    )(a)


def _place():
    return lax.axis_index("x"), lax.axis_index("y"), lax.axis_index("c")


def _other_chips(x, y):
    return [(1 - x, y), (x, 1 - y), (1 - x, 1 - y)]


def gather_devices(name, a):
    m_per, n = a.shape

    def body(x_ref, out_ref, send_sems, recv_sems, local_sem):
        x, y, c = _place()
        me, sibling = (x, y, c), (x, y, 1 - c)
        chips = _other_chips(x, y)

        def rows(px, py, pc):
            return out_ref.at[pl.ds((4 * px + 2 * py + pc) * m_per, m_per), :]

        def copy(k, block, to, src=None):
            return pltpu.make_async_remote_copy(
                src_ref=rows(*block) if src is None else src, dst_ref=rows(*block),
                send_sem=send_sems.at[k], recv_sem=recv_sems.at[k], device_id=to, device_id_type=MESH)

        mine = pltpu.make_async_copy(x_ref, rows(*me), local_sem)
        mine.start()
        first = [copy(0, me, sibling, src=x_ref)]
        first += [copy(1 + j, me, (*chip, c), src=x_ref) for j, chip in enumerate(chips)]
        for cp in first:
            cp.start()
        passed = [copy(4 + j, (*chip, c), sibling) for j, chip in enumerate(chips)]
        for j, chip in enumerate(chips):
            copy(1 + j, (*chip, c), me).wait_recv()
            passed[j].start()
        copy(0, sibling, me).wait_recv()
        for j, chip in enumerate(chips):
            copy(4 + j, (*chip, 1 - c), me).wait_recv()
        for cp in first + passed:
            cp.wait_send()
        mine.wait()

    out = pl.pallas_call(
        body, name=name,
        out_shape=jax.ShapeDtypeStruct((N_DEV * m_per, n), a.dtype),
        in_specs=[pl.BlockSpec(memory_space=pltpu.VMEM)],
        out_specs=pl.BlockSpec(memory_space=pltpu.VMEM),
        scratch_shapes=[pltpu.SemaphoreType.DMA((7,)), pltpu.SemaphoreType.DMA((7,)), pltpu.SemaphoreType.DMA],
        compiler_params=pltpu.CompilerParams(vmem_limit_bytes=VMEM_LIMIT),
    )(a)
    return out.reshape(N_DEV, m_per, n)


_ANY = pl.BlockSpec(memory_space=pl.ANY)


def _hbm_ref(a):
    return jax.new_ref(a, memory_space=pltpu.MemorySpace.HBM)


def _hbm_empty(shape, dtype):
    return jax.empty_ref(jax.ShapeDtypeStruct(shape, dtype), memory_space=pltpu.MemorySpace.HBM)


ID_PAIR, ID_CHIPS, ID_SHARE, ID_UKV = 8, 9, 10, 11


def _sequencer(name, collective_id, n_sem, peers_of, program):
    sems = pltpu.SemaphoreType.DMA((n_sem,))

    @pl.kernel(mesh=plsc.ScalarSubcoreMesh(axis_name="seq", num_cores=1), name=name, scratch_types=[sems, sems],
               compiler_params=pltpu.CompilerParams(collective_id=collective_id))
    def launch(send_sem, recv_sem):
        x, y, c = _place()
        peers = peers_of(x, y, c)
        barrier = pltpu.get_barrier_semaphore()
        for peer in peers:
            pl.semaphore_signal(barrier, inc=1, device_id=peer, device_id_type=MESH)
        pl.semaphore_wait(barrier, len(peers))
        program(x, y, c, send_sem, recv_sem)

    launch()


def gather_weights(name, stage, arrays):
    n = len(arrays)
    refs = [_hbm_ref(a) for a in arrays]

    def program(x, y, c, send_sem, recv_sem):
        me = 2 * x + y
        chips = _other_chips(x, y)

        def ici(t, r, half):
            cx, cy = chips[r]
            mine = refs[t].at[me, half]
            return pltpu.make_async_remote_copy(
                src_ref=mine, dst_ref=mine, send_sem=send_sem.at[3 * t + r], recv_sem=recv_sem.at[3 * t + r],
                device_id=(cx, cy, c), device_id_type=MESH)

        def d2d(t, r, half):
            cx, cy = chips[r]
            there = refs[t].at[2 * cx + cy, half]
            k = 3 * n + 3 * t + r
            return pltpu.make_async_remote_copy(
                src_ref=there, dst_ref=there, send_sem=send_sem.at[k], recv_sem=recv_sem.at[k],
                device_id=(x, y, 1 - c), device_id_type=MESH)

        for t in range(n):
            for r in range(3):
                ici(t, r, c).start()
        for t in range(n):
            for r in range(3):
                ici(t, r, c).wait_recv()
                d2d(t, r, c).start()
        for t in range(n):
            for r in range(3):
                d2d(t, r, 1 - c).wait_recv()
        for t in range(n):
            for r in range(3):
                ici(t, r, c).wait_send()
                d2d(t, r, c).wait_send()

    _sequencer(name, stage, 6 * n, lambda x, y, c: [(x, y, 1 - c)] + [(cx, cy, c) for cx, cy in _other_chips(x, y)],
               program)
    return [r[...] for r in refs]


def cast_into_slots(name, chip, shards, after=None):
    steps = 2
    n = len(shards)

    def body(chip_ref, *refs):
        for src, dst in zip(refs[:n], refs[-n - 1:-1]):
            dst[...] = src[...].astype(BF16)
        refs[-1][...] = jnp.zeros_like(refs[-1])

    token_spec = pl.BlockSpec((SUBLANES, LANES), lambda h, i, chip_ref: (0, 0))

    def spec_in(a, prefix):
        R, C = a.shape[-2:]
        return pl.BlockSpec((None,) * (len(prefix) + 1) + (R // steps, C), lambda h, i, chip_ref: prefix + (h, i, 0))

    def spec_out(a):
        R, C = a.shape[-2:]
        return pl.BlockSpec((None, None, R // steps, C), lambda h, i, chip_ref: (chip_ref[0], h, i, 0))

    outs = pl.pallas_call(
        body, name=name,
        grid_spec=pltpu.PrefetchScalarGridSpec(
            num_scalar_prefetch=1, grid=(2, steps),
            in_specs=[spec_in(a, p) for a, p in shards] + ([token_spec] if after is not None else []),
            out_specs=[spec_out(a) for a, _ in shards] + [token_spec]),
        out_shape=[jax.ShapeDtypeStruct((N_CHIP, 2) + a.shape[-2:], BF16) for a, _ in shards]
        + [jax.ShapeDtypeStruct((SUBLANES, LANES), F32)],
        compiler_params=_params("arbitrary", "arbitrary"),
    )(chip, *[a for a, _ in shards], *([after] if after is not None else []))
    return outs[:-1], outs[-1]


def reduce_pair(name, grads):
    n = len(grads)
    src = [_hbm_ref(g) for g in grads]
    dst = [_hbm_empty((N_CHIP,) + g.shape[2:], g.dtype) for g in grads]

    def program(x, y, c, send_sem, recv_sem):
        cps = [pltpu.make_async_remote_copy(
            src_ref=src[t].at[:, 1 - c], dst_ref=dst[t], send_sem=send_sem.at[t], recv_sem=recv_sem.at[t],
            device_id=(x, y, 1 - c), device_id_type=MESH) for t in range(n)]
        for cp in cps:
            cp.start()
        for cp in cps:
            cp.wait()

    _sequencer(name, ID_PAIR, n, lambda x, y, c: [(x, y, 1 - c)], program)
    return [r[...] for r in src], [r[...] for r in dst]


def pair_add(name, core, gs, gots):
    n = len(gs)

    def body(core_ref, *refs):
        for t in range(n):
            refs[2 * n + t][...] = (refs[2 * t][...] + refs[2 * t + 1][...]).astype(BF16)

    in_specs, out_specs, out_shape = [], [], []
    for g in gs:
        _, _, R, C = g.shape
        in_specs += [pl.BlockSpec((None, None, R, C), lambda q, core_ref: (q, core_ref[0], 0, 0)),
                     pl.BlockSpec((None, R, C), lambda q, core_ref: (q, 0, 0))]
        out_specs.append(pl.BlockSpec((None, R, C), lambda q, core_ref: (q, 0, 0)))
        out_shape.append(jax.ShapeDtypeStruct((N_CHIP, R, C), BF16))
    return pl.pallas_call(
        body, name=name,
        grid_spec=pltpu.PrefetchScalarGridSpec(num_scalar_prefetch=1, grid=(N_CHIP,), in_specs=in_specs,
                                               out_specs=out_specs),
        out_shape=out_shape, compiler_params=_params("parallel"),
    )(core, *[a for pair in zip(gs, gots) for a in pair])


def reduce_chips(name, sums):
    n = len(sums)
    src = [_hbm_ref(s) for s in sums]
    dst = [_hbm_empty((3,) + s.shape[1:], s.dtype) for s in sums]

    def program(x, y, c, send_sem, recv_sem):
        cps = []
        for t in range(n):
            for r, (cx, cy) in enumerate(_other_chips(x, y)):
                cps.append(pltpu.make_async_remote_copy(
                    src_ref=src[t].at[2 * cx + cy], dst_ref=dst[t].at[r],
                    send_sem=send_sem.at[3 * t + r], recv_sem=recv_sem.at[3 * t + r],
                    device_id=(cx, cy, c), device_id_type=MESH))
        for cp in cps:
            cp.start()
        for cp in cps:
            cp.wait()

    _sequencer(name, ID_CHIPS, 3 * n, lambda x, y, c: [(cx, cy, c) for cx, cy in _other_chips(x, y)], program)
    return [r[...] for r in src], [r[...] for r in dst]


def chip_add(name, place, items, after=None):
    n = len(items)

    def body(place_ref, *refs):
        for t in range(n):
            s_ref, got_ref, o_ref = refs[2 * t], refs[2 * t + 1], refs[len(refs) - n + t]
            o_ref[...] = ((s_ref[...].astype(F32) + got_ref[0].astype(F32)) + got_ref[1].astype(F32)) + got_ref[2].astype(F32)

    in_specs, args, out_specs, out_shape, aliases = [], [place], [], [], {}
    for s, got, k, n_slots, _ in items:
        _, R, C = s.shape
        in_specs += [pl.BlockSpec((None, R, C), lambda i, place_ref: (place_ref[0], 0, 0)),
                     pl.BlockSpec((3, R, C), lambda i, place_ref: (0, 0, 0))]
        args += [s, got]
        out_specs.append(pl.BlockSpec((None, None, R, C), lambda i, place_ref, k=k: (k, place_ref[1], 0, 0)))
        out_shape.append(jax.ShapeDtypeStruct((n_slots, 2, R, C), F32))
    for t, (*_, prev) in enumerate(items):
        if prev is not None:
            aliases[len(args)] = t
            in_specs.append(_ANY)
            args.append(prev)
    for piece in after or ():
        in_specs.append(pl.BlockSpec((SUBLANES, LANES), lambda i, place_ref: (0, 0)))
        args.append(piece)
    return pl.pallas_call(
        body, name=name,
        grid_spec=pltpu.PrefetchScalarGridSpec(num_scalar_prefetch=1, grid=(1,), in_specs=in_specs,
                                               out_specs=out_specs),
        out_shape=out_shape, input_output_aliases=aliases, compiler_params=_params("arbitrary"),
    )(*args)


def share_halves(name, stacks, slots):
    n = len(stacks)
    dst = [_hbm_ref(s) for s in stacks]

    def program(x, y, c, send_sem, recv_sem):
        cps = [pltpu.make_async_remote_copy(
            src_ref=dst[t].at[slots[t], c], dst_ref=dst[t].at[slots[t], c],
            send_sem=send_sem.at[t], recv_sem=recv_sem.at[t],
            device_id=(x, y, 1 - c), device_id_type=MESH) for t in range(n)]
        for cp in cps:
            cp.start()
        for cp in cps:
            cp.wait()

    _sequencer(name, ID_SHARE, n, lambda x, y, c: [(x, y, 1 - c)], program)
    return [r[...] for r in dst]


def gather_blocks(name, slotted):
    out = _hbm_ref(slotted)

    def program(x, y, c, send_sem, recv_sem):
        sibling = (x, y, 1 - c)
        chips = _other_chips(x, y)

        def copy(k, px, py, pc, to):
            block = out.at[4 * px + 2 * py + pc]
            return pltpu.make_async_remote_copy(src_ref=block, dst_ref=block, send_sem=send_sem.at[k],
                                                recv_sem=recv_sem.at[k], device_id=to, device_id_type=MESH)

        first = [copy(0, x, y, c, sibling)] + [copy(1 + j, x, y, c, (cx, cy, c)) for j, (cx, cy) in enumerate(chips)]
        for cp in first:
            cp.start()
        passed = [copy(4 + j, cx, cy, c, sibling) for j, (cx, cy) in enumerate(chips)]
        for j, (cx, cy) in enumerate(chips):
            copy(1 + j, cx, cy, c, (x, y, c)).wait_recv()
            passed[j].start()
        copy(0, x, y, 1 - c, (x, y, c)).wait_recv()
        for j, (cx, cy) in enumerate(chips):
            copy(4 + j, cx, cy, 1 - c, (x, y, c)).wait_recv()
        for cp in first + passed:
            cp.wait_send()

    _sequencer(name, ID_UKV, 7, lambda x, y, c: [(x, y, 1 - c)] + [(cx, cy, c) for cx, cy in _other_chips(x, y)],
               program)
    return out[...]


def place_block(name, dev, a):
    M, N = a.shape
    tr = min(M, 64)

    def body(dev_ref, a_ref, o_ref):
        o_ref[...] = a_ref[...]

    return pl.pallas_call(
        body, name=name,
        grid_spec=pltpu.PrefetchScalarGridSpec(
            num_scalar_prefetch=1, grid=(M // tr,),
            in_specs=[pl.BlockSpec((tr, N), lambda i, dev_ref: (i, 0))],
            out_specs=pl.BlockSpec((None, tr, N), lambda i, dev_ref: (dev_ref[0], i, 0))),
        out_shape=jax.ShapeDtypeStruct((N_DEV, M, N), a.dtype),
        compiler_params=_params("parallel"),
    )(dev, a)


def _swap_rope(a):
    return jnp.concatenate([a[..., ROPE // 2:], a[..., :ROPE // 2]], axis=-1)


def _rope_tables(S):
    inv = 1.0 / (ROPE_THETA ** (jnp.arange(0, ROPE, 2, dtype=F32) / ROPE))
    ang = jnp.arange(S, dtype=F32)[:, None] * inv[None, :]
    cos, sin = jnp.cos(ang), jnp.sin(ang)
    return (jnp.tile(jnp.concatenate([cos, cos], axis=1), (1, SLOTS)),
            jnp.tile(jnp.concatenate([-sin, sin], axis=1), (1, SLOTS)))


def _vec(norm_g, mod, i, k):
    rows = [norm_g[i, 2 * k], norm_g[i, 2 * k + 1], mod[i, 3 * k], mod[i, 3 * k + 1], mod[i, 3 * k + 2]]
    return jnp.concatenate([jnp.stack(rows), jnp.zeros((3, D), F32)], axis=0)


def _unpack_weights(full, w_uk, w_uv, q_norm, kv_norm):
    G = D // 4
    ffn_in = [[full[2 * i + k].reshape(N_CHIP, D, FSH) for k in range(2)] for i in range(2)]
    ffn_out = [[full[4 + 2 * i + k].reshape(2, FSH, D) for k in range(2)] for i in range(2)]
    pw = full[8].reshape(N_CHIP, 4, G // N_CHIP, G).transpose(1, 0, 2, 3).reshape(4, G, G)
    w_in = full[9].reshape(D, QL + KVL + ROPE)
    w_uq = full[10].reshape(QL, N_HEADS, NOPE + ROPE)
    wkr = w_in[:, QL + KVL:]
    wr = w_uq[:, :, NOPE:]
    eye2 = jnp.eye(2, dtype=BF16)
    uk_t = jnp.transpose(w_uk, (1, 2, 0)).reshape(N_PAIR, 2, NOPE, KVL)
    bduk = jnp.einsum("janc,ab->janbc", uk_t, eye2).reshape(N_PAIR, 2 * NOPE, 2 * KVL)
    uv = jnp.transpose(w_uv, (1, 0, 2)).reshape(N_PAIR, 2, KVL, VH)
    bduv = jnp.einsum("jacn,ab->jacbn", uv, eye2).reshape(N_PAIR, 2 * KVL, 2 * VH)
    mw = dict(wq=w_in[:, :QL], wkv=w_in[:, QL:QL + KVL], wkr4=jnp.tile(wkr, (1, SLOTS)),
              wkrs4=jnp.tile(_swap_rope(wkr), (1, SLOTS)), qn=q_norm, kvn=kv_norm,
              wn=w_uq[:, :, :NOPE].reshape(QL, NOPE_ALL), wr=wr.reshape(QL, ROPE_ALL),
              wrs=_swap_rope(wr).reshape(QL, ROPE_ALL), bduk=bduk)
    return ffn_in, ffn_out, pw, mw, bduv, full[11].reshape(D, D)


def _example_step(x, target, mod, norm_g, pvec, ffn_in, ffn_out, pw, mw, bduv, wo, reducer):
    S = x.shape[0]
    tabs = _rope_tables(S)
    vec = [[_vec(norm_g, mod, i, k) for k in range(3)] for i in range(2)]
    saved = {}
    for i in range(2):
        xin = x
        x, a, u, h = ffn_fwd(xin, vec[i][0], ffn_in[i][0], ffn_out[i][0], 0.5)
        saved[i, 0] = (xin, a, u, h)
        xin = x
        if i == 0:
            x, y, z = pool_fwd(xin, vec[i][1], pw, pvec)
            saved[i, 1] = (xin, y, z)
        else:
            h_m, cq_raw, ckv_raw, cqn, qnope, qcat, kcat, vcat = mla_pre(xin, vec[i][1], mw, tabs)
            olat, lse = attn_fwd(qcat, kcat, vcat)
            x, u_m, ocat = mla_post(olat, xin, vec[i][1], bduv, wo)
            saved[i, 1] = (xin, h_m, cq_raw, ckv_raw, cqn, qnope, qcat, kcat, olat, lse, u_m, ocat)
        xin = x
        x, a, u, h = ffn_fwd(xin, vec[i][2], ffn_in[i][1], ffn_out[i][1], 0.5)
        saved[i, 2] = (xin, a, u, h)
    loss, dx = loss_head(x, target)

    vg = {}
    G = D // 4

    def ffn_grads(i, k, dw_in, dw_out):
        return [(0, 2 * i + k, 4, dw_in.reshape(N_CHIP, 2, D // 2, FSH)),
                (1, 2 * i + k, 4, dw_out.reshape(N_CHIP, 2, DFF // 8, D))]

    piece = lambda t: t[:SUBLANES, :LANES]
    for i in (1, 0):
        xin, a, u, h = saved[i, 2]
        dx, du, act, da, vg[i, 2] = ffn_bwd(dx, xin, u, a, vec[i][2], ffn_in[i][1], ffn_out[i][1], 0.5)
        reducer.advance(after=(piece(dx),))
        reducer.add(f"f{i}1", ffn_grads(i, 1, *ffn_dw(h, da, act, du)))
        if i == 0:
            xin, y, z = saved[i, 1]
            dx, vg[i, 1], pgrad, g_pool = pool_bwd(dx, xin, y, z, vec[i][1], pw, pvec)
            reducer.advance(after=(piece(dx),))
        else:
            xin, h_m, cq_raw, ckv_raw, cqn, qnope, qcat, kcat, olat, lse, u_m, ocat = saved[i, 1]
            du, docat, dolat, delta, vg_post = mla_post_bwd(dx, u_m, olat, vec[i][1], bduv, wo)
            reducer.advance()
            dq, dk, dv = attn_bwd(qcat, kcat, kcat.T, dolat, lse, delta)
            reducer.advance(after=(piece(dk),))
            dx, dlat, dql, dqcat, vg_pre, ngrad = mla_pre_bwd(
                dx, dq, dk, dv, xin, cq_raw, ckv_raw, vec[i][1], mw, tabs)
            vg[i, 1] = vg_post + vg_pre
            g = mla_dw(h_m, dlat, cqn, dqcat, dql, qnope, olat, docat, ocat, du)
            slots = lambda a: a.reshape(D, SLOTS, ROPE).sum(axis=1)
            g_kr = slots(g["in"][:, QL + KVL:QL + KVL + LANES]) + _swap_rope(slots(g["in"][:, QL + KVL + LANES:]))
            g_in = jnp.concatenate([g["in"][:, :QL + KVL], g_kr], axis=1)
            g_r = g["q"][:, NOPE_ALL:NOPE_ALL + ROPE_ALL].reshape(QL, N_HEADS, ROPE)
            g_rs = g["q"][:, NOPE_ALL + ROPE_ALL:].reshape(QL, N_HEADS, ROPE)
            g_uq = jnp.concatenate([g["q"][:, :NOPE_ALL].reshape(QL, N_HEADS, NOPE), g_r + _swap_rope(g_rs)], axis=-1)

            def heads(pairs):
                blk = pairs.reshape(N_PAIR, 2, KVL, 2, NOPE)
                per_head = jnp.stack([blk[:, 0, :, 0, :], blk[:, 1, :, 1, :]], axis=1).reshape(N_HEADS, KVL, NOPE)
                return jnp.transpose(per_head, (1, 0, 2)).reshape(KVL, N_HEADS * NOPE)

            reducer.add("mla", [(3, 0, 1, g_in.reshape(N_CHIP, 2, D // 8, QL + KVL + ROPE)),
                                (4, 0, 1, g_uq.reshape(N_CHIP, 2, QL // 8, N_HEADS * (NOPE + ROPE))),
                                (5, 0, 1, g["o"].reshape(N_CHIP, 2, D // 8, D))])
            reducer.add_replicated(jnp.concatenate([heads(g["uk"]), heads(g["uv"])], axis=0))
        xin, a, u, h = saved[i, 0]
        dx, du, act, da, vg[i, 0] = ffn_bwd(dx, xin, u, a, vec[i][0], ffn_in[i][0], ffn_out[i][0], 0.5)
        if i == 1:
            reducer.advance(after=(piece(dx),))
        grads = ffn_grads(i, 0, *ffn_dw(h, da, act, du))
        if i == 0:
            grads.append((2, 0, 1, g_pool.reshape(N_CHIP, 2, 2 * G // N_CHIP, G)))
        reducer.add(f"f{i}0", grads)
    return loss, dx, vg, pgrad, ngrad


class _GradReducer:
    def __init__(self, core, place, dev):
        self.core, self.place, self.dev = core, place, dev
        self.stacks = {}
        self.live = []
        self.replicated = None

    def add(self, tag, items):
        gen = self._run(tag, items)
        next(gen)
        self.live.append(gen)

    def add_replicated(self, block):
        self.replicated = gather_blocks("gather_ukv", place_block("place_ukv", self.dev, block))

    def advance(self, after=None):
        self.after = after
        live = []
        for gen in self.live:
            try:
                next(gen)
                live.append(gen)
            except StopIteration:
                pass
        self.live = live

    def finish(self):
        while self.live:
            self.advance()
        return self.stacks, self.replicated

    def _run(self, tag, items):
        grads, from_pair = reduce_pair(f"reduce_pair_{tag}", [g for *_, g in items])
        yield
        sums = pair_add(f"pair_add_{tag}", self.core, grads, from_pair)
        sums, from_chips = reduce_chips(f"reduce_chips_{tag}", sums)
        yield
        stacks = chip_add(f"chip_add_{tag}", self.place,
                          [(s, p, k, n_slots, self.stacks.get(o)) for (o, k, n_slots, _), s, p
                           in zip(items, sums, from_chips)], self.after)
        for (o, *_), stack in zip(items, stacks):
            self.stacks[o] = stack
        shared = share_halves(f"share_halves_{tag}", [self.stacks[o] for o, *_ in items], [k for _, k, *_ in items])
        for (o, *_), v in zip(items, shared):
            self.stacks[o] = v


SMALL_IN = 8 * 640
SMALL_GRAD = 8 * 4224
SMALL_W = 8 * 2944


def _pack(parts, total):
    flat = jnp.concatenate([p.reshape(-1) for p in parts])
    return jnp.concatenate([flat, jnp.zeros((total - flat.shape[0],), F32)]).reshape(8, total // 8)


def kernel(x, c, ada_w, ada_b, norm_g, ffn_w_in, ffn_w_out, pool_w, pool_b, pool_scale, mla_w_in, mla_q_norm, mla_kv_norm, mla_w_uq, mla_w_uk, mla_w_uv, mla_w_o, loss_target, m_ada_w, m_ada_b, m_norm_g, m_ffn_w_in, m_ffn_w_out, m_pool_w, m_pool_b, m_pool_scale, m_mla_w_in, m_mla_q_norm, m_mla_kv_norm, m_mla_w_uq, m_mla_w_uk, m_mla_w_uv, m_mla_w_o, v_ada_w, v_ada_b, v_norm_g, v_ffn_w_in, v_ffn_w_out, v_pool_w, v_pool_b, v_pool_scale, v_mla_w_in, v_mla_q_norm, v_mla_kv_norm, v_mla_w_uq, v_mla_w_uk, v_mla_w_uv, v_mla_w_o):
    ix, iy, ic = _place()
    chip = 2 * ix + iy
    dev = 2 * chip + ic
    core_arr = ic.astype(jnp.int32).reshape(1)
    chip_arr = chip.astype(jnp.int32).reshape(1)
    S = x.shape[1]
    G = D // 4
    NG = D // N_CHIP

    def chip_cols(a, width, axis):
        return lax.dynamic_slice_in_dim(a, chip * width, width, axis)

    got = gather_devices("gather_small_in", _pack([c, norm_g, pool_b, mla_q_norm], SMALL_IN)).reshape(N_DEV, SMALL_IN)
    c_all = got[:, :D]
    parts = got[0::2]
    o = D
    norm_g_full = parts[:, o:o + 12 * NG].reshape(N_CHIP, 2, 6, NG).transpose(1, 2, 0, 3).reshape(2, 6, D)
    o += 12 * NG
    pool_b_full = parts[:, o:o + G].reshape(N_CHIP, 4, G // N_CHIP).transpose(1, 0, 2).reshape(1, D)
    o += G
    q_norm_full = parts[:, o:o + QL // N_CHIP].reshape(1, QL)
    pvec = jnp.concatenate([pool_b_full, pool_scale, jnp.zeros((6, D), F32)], axis=0)

    c_pad = jnp.concatenate([c_all, jnp.zeros((8, D), F32)], axis=0)
    mod_loc = mod_fwd(c_pad, ada_w, chip_cols(ada_b, MOD_COLS, 1).reshape(2, 1, MOD_COLS))
    got = gather_devices("gather_mod", mod_loc[:, :8].transpose(1, 0, 2).reshape(8, 2 * MOD_COLS))
    mine = lax.dynamic_index_in_dim(got[0::2].reshape(N_CHIP, 8, 2, MOD_COLS), dev, axis=1, keepdims=False)
    mod = mine.transpose(1, 0, 2).reshape(2, 9, D)

    bf = lambda a: a.astype(BF16)
    w_in_halves = ffn_w_in.reshape(2, 2, 2, D // 2, FSH)
    w_out_halves = ffn_w_out.reshape(2, 2, 2, DFF // 8, D)
    shards = [(w_in_halves, (i, k)) for i in range(2) for k in range(2)]
    shards += [(w_out_halves, (i, k)) for i in range(2) for k in range(2)]
    shards += [(pool_w.reshape(2, 2 * G // N_CHIP, G), ()), (mla_w_in.reshape(2, D // 8, QL + KVL + ROPE), ()),
               (mla_w_uq.reshape(2, QL // 8, N_HEADS * (NOPE + ROPE)), ()), (mla_w_o.reshape(2, D // 8, D), ())]
    full = [None] * len(shards)
    stages = [(0, 4, 8), (1, 5), (2, 6), (9, 10, 11), (3, 7)]
    first, token = cast_into_slots("cast_first", chip_arr, [shards[t] for t in stages[0]])
    slotted = dict(zip(stages[0], first))
    rest = [t for members in stages[1:] for t in members]
    for stage, members in enumerate(stages):
        got_w = gather_weights(f"gather_weights_{stage}", stage, [slotted[t] for t in members])
        for t, a in zip(members, got_w):
            full[t] = a
        if stage == 0:
            slotted.update(zip(rest, cast_into_slots("cast_rest", chip_arr, [shards[t] for t in rest], token)[0]))
    ffn_in, ffn_out, pw, mw, bduv, wo = _unpack_weights(full, bf(mla_w_uk[0]), bf(mla_w_uv[0]), q_norm_full,
                                                        mla_kv_norm)

    place_arr = jnp.stack([chip, ic]).astype(jnp.int32)
    reducer = _GradReducer(core_arr, place_arr, dev.astype(jnp.int32).reshape(1))
    loss_mine, grad_x, vg, pgrad, ngrad = _example_step(
        x[0], loss_target[0], mod, norm_g_full, pvec, ffn_in, ffn_out, pw, mw, bduv, wo, reducer)

    dmod = jnp.stack([jnp.concatenate([vg[i, k][0:3] for k in range(3)]) for i in range(2)])
    dnorm = jnp.stack([jnp.concatenate([vg[i, k][3:5] for k in range(3)]) for i in range(2)])
    small = _pack([dmod, dnorm, pgrad[0], pgrad[1], ngrad[0], ngrad[1, :KVL], loss_mine], SMALL_GRAD)
    got = gather_devices("gather_small_grad", small)
    tot = sum_devices("sum_small_grad", got).reshape(-1)
    n_mod = 2 * 9 * D
    g_ada_b = tot[:n_mod].reshape(ada_b.shape)
    o = n_mod
    g_norm = chip_cols(tot[o:o + 12 * D].reshape(2, 6, D), NG, 2)
    o += 12 * D
    g_pool_b = chip_cols(tot[o:o + D].reshape(1, 4, G), G // N_CHIP, 2)
    o += D
    g_pool_scale = tot[o:o + D].reshape(pool_scale.shape)
    o += D
    g_q_norm = chip_cols(tot[o:o + QL].reshape(1, QL), QL // N_CHIP, 1)
    o += QL
    g_kv_norm = tot[o:o + KVL].reshape(mla_kv_norm.shape)
    loss = tot[o + KVL]
    dmod_all = chip_cols(got.reshape(N_DEV, -1)[:, :n_mod].reshape(N_DEV, 2, 9 * D), MOD_COLS, 2)
    dmod_pad = jnp.concatenate([dmod_all.transpose(1, 0, 2), jnp.zeros((2, 8, MOD_COLS), F32)], axis=1)

    g_ada_w, d_ada_w, nm_ada_w, nv_ada_w = adamw_ada(c_pad, dmod_pad, ada_w, m_ada_w, v_ada_w)
    small_names = ["ada_b", "norm_g", "pool_b", "pool_scale", "mla_q_norm", "mla_kv_norm"]
    small_w = [ada_b, norm_g, pool_b, pool_scale, mla_q_norm, mla_kv_norm]
    small_g = [g_ada_b, g_norm, g_pool_b, g_pool_scale, g_q_norm, g_kv_norm]
    small_m = [m_ada_b, m_norm_g, m_pool_b, m_pool_scale, m_mla_q_norm, m_mla_kv_norm]
    small_v = [v_ada_b, v_norm_g, v_pool_b, v_pool_scale, v_mla_q_norm, v_mla_kv_norm]
    packed = adamw("adamw_small", *[_pack(p, SMALL_W) for p in (small_w, small_g, small_m, small_v)])
    upd = {}
    o = 0
    for name, w in zip(small_names, small_w):
        upd[name] = [p.reshape(-1)[o:o + w.size].reshape(w.shape) for p in packed]
        o += w.size
    upd["ada_w"] = [d_ada_w, nm_ada_w, nv_ada_w]

    reducer.advance(after=(d_ada_w[0, :SUBLANES, :LANES],))
    ffn = [("ffn_w_in", 0, ffn_w_in, m_ffn_w_in, v_ffn_w_in), ("ffn_w_out", 1, ffn_w_out, m_ffn_w_out, v_ffn_w_out)]
    slots = lambda a: a.reshape((4,) + a.shape[2:])
    early = {name: adamw(f"adamw_{name}_early", slots(w), slots(reducer.stacks[o].reshape(w.shape)), slots(m),
                         slots(v), part=(1, 3), copy_grad=True) for name, o, w, m, v in ffn}
    g_mla_in = reducer.stacks[3].reshape(mla_w_in.shape)
    g_uq = reducer.stacks[4].reshape(mla_w_uq.shape)
    g_wo = reducer.stacks[5].reshape(mla_w_o.shape)
    for name, w, g, m, v in [("mla_w_in", mla_w_in, g_mla_in, m_mla_w_in, v_mla_w_in),
                             ("mla_w_uq", mla_w_uq, g_uq, m_mla_w_uq, v_mla_w_uq),
                             ("mla_w_o", mla_w_o, g_wo, m_mla_w_o, v_mla_w_o)]:
        upd[name] = adamw("adamw_" + name, w, g, m, v)

    reducer.advance(after=(early["ffn_w_in"][0][1, :SUBLANES, :LANES], early["ffn_w_out"][0][1, :SUBLANES, :LANES],
                           upd["mla_w_o"][0][0, :SUBLANES, :LANES], upd["mla_w_in"][0][0, :SUBLANES, :LANES]))
    ukv = sum_devices("sum_ukv", reducer.replicated)
    g_uk = ukv[:KVL].reshape(mla_w_uk.shape)
    g_uv = ukv[KVL:].reshape(mla_w_uv.shape)
    upd["mla_w_uk"] = adamw("adamw_mla_w_uk", mla_w_uk, g_uk, m_mla_w_uk, v_mla_w_uk)
    upd["mla_w_uv"] = adamw("adamw_mla_w_uv", mla_w_uv, g_uv, m_mla_w_uv, v_mla_w_uv)
    stacks, _ = reducer.finish()
    g_pool_w = stacks[2].reshape(pool_w.shape)
    g_ffn = {}
    for name, o, w, m, v in ffn:
        done = adamw(f"adamw_{name}_last", slots(w), slots(stacks[o].reshape(w.shape)), slots(m), slots(v),
                     part=(0, 1), prev=early[name], copy_grad=True)
        upd[name] = [p.reshape(w.shape) for p in done[:3]]
        g_ffn[name] = done[3].reshape(w.shape)
    g_ffn_in, g_ffn_out = g_ffn["ffn_w_in"], g_ffn["ffn_w_out"]
    upd["pool_w"] = adamw("adamw_pool_w", pool_w, g_pool_w, m_pool_w, v_pool_w)

    order = ["ada_w", "ada_b", "norm_g", "ffn_w_in", "ffn_w_out", "pool_w", "pool_b", "pool_scale", "mla_w_in",
             "mla_q_norm", "mla_kv_norm", "mla_w_uq", "mla_w_uk", "mla_w_uv", "mla_w_o"]
    grad = dict(ada_w=g_ada_w, ada_b=g_ada_b, norm_g=g_norm, ffn_w_in=g_ffn_in, ffn_w_out=g_ffn_out, pool_w=g_pool_w,
                pool_b=g_pool_b, pool_scale=g_pool_scale, mla_w_in=g_mla_in, mla_q_norm=g_q_norm,
                mla_kv_norm=g_kv_norm, mla_w_uq=g_uq, mla_w_uk=g_uk, mla_w_uv=g_uv, mla_w_o=g_wo)
    return (loss, grad_x[None], *[grad[n] for n in order], *[upd[n][0] for n in order],
            *[upd[n][1] for n in order], *[upd[n][2] for n in order])
```

```python
import functools

import jax
import jax.numpy as jnp
from jax import lax
from jax.experimental import pallas as pl
from jax.experimental.pallas import tpu as pltpu
from jax.experimental.pallas import tpu_sc as plsc

F32 = jnp.float32
BF16 = jnp.bfloat16

D = 1024
DFF = 2816
FSH = 1408
N_CHIP = 4
N_DEV = 8
N_HEADS = 16
NOPE = 64
ROPE = 32
VH = 64
QL = 256
KVL = 128
LANES = 128
SUBLANES = 8
QPAD = 256
EPS = 1e-6
ATTN_SCALE = (NOPE + ROPE) ** -0.5
ROPE_THETA = 10000.0
POOL_WINDOWS = (2, 4, 8, 16)
HALO = 8
ATTN_TQ = 1024
ATTN_KC = 512
ROW_TILE = 512
DW_TK = 2048

ADAM_LR, ADAM_B1, ADAM_B2, ADAM_EPS, ADAM_WD, ADAM_STEP = 0.001, 0.9, 0.999, 1e-08, 0.01, 10

VMEM_LIMIT = 60 * 1024 * 1024
MESH = pl.DeviceIdType.MESH

NT = (((1,), (1,)), ((), ()))
TN = (((0,), (0,)), ((), ()))


def _params(*sem):
    return pltpu.CompilerParams(dimension_semantics=sem, vmem_limit_bytes=VMEM_LIMIT)


def _dot(a, b, dims=None):
    if dims is None:
        return jnp.dot(a, b, preferred_element_type=F32)
    return lax.dot_general(a, b, dims, preferred_element_type=F32)


def _rms(x):
    r = lax.rsqrt(jnp.mean(x * x, axis=-1, keepdims=True) + EPS)
    return x * r, r


def _rms_bwd(xhat, r, dxhat):
    return r * (dxhat - xhat * jnp.mean(dxhat * xhat, axis=-1, keepdims=True))


def _as_row(col):
    return jnp.broadcast_to(col, (col.shape[0], LANES)).T[0:1, :]


def _prenorm(x, vec_ref):
    xhat, r = _rms(x)
    h = xhat * vec_ref[0:1, :] * (1.0 + vec_ref[3:4, :]) + vec_ref[2:3, :]
    return h, xhat, r


def _postnorm_bwd(dout, u, vec_ref, weight):
    uhat, r = _rms(u)
    gt = weight * (1.0 + vec_ref[4:5, :])
    dy = dout * gt
    dgate_rows = (weight * dout) * (uhat * vec_ref[1:2, :])
    dgpost_rows = dy * uhat
    du = _rms_bwd(uhat, r, dy * vec_ref[1:2, :])
    return du, dgate_rows, dgpost_rows


def _prenorm_bwd(dh, x, vec_ref, vg_ref):
    xhat, r = _rms(x)
    sc1 = 1.0 + vec_ref[3:4, :]
    g = vec_ref[0:1, :]
    vg_ref[0:1, :] += jnp.sum(dh, axis=0, keepdims=True)
    vg_ref[1:2, :] += jnp.sum(dh * (xhat * g), axis=0, keepdims=True)
    vg_ref[3:4, :] += jnp.sum(dh * sc1 * xhat, axis=0, keepdims=True)
    return _rms_bwd(xhat, r, dh * g * sc1)


def ffn_fwd(x, vec, w_in, w_out, weight):
    S = x.shape[0]
    tm = min(512, S)
    row = lambda i: (i, 0)
    half = lambda j: [_w3((8, D)), pl.BlockSpec((None, D, FSH), lambda i: (j, 0, 0)),
                      pl.BlockSpec((None, D, FSH), lambda i: (j + 2, 0, 0)),
                      pl.BlockSpec((None, FSH, D), lambda i: (j, 0, 0))]
    a_spec = lambda j: pl.BlockSpec((2, tm, FSH), lambda i: (0, i, j))
    a_shape = jax.ShapeDtypeStruct((2, S, DFF), BF16)

    def hidden(hb, wg_ref, wu_ref, wo_ref, a_ref):
        g = _dot(hb, wg_ref[...])
        up = _dot(hb, wu_ref[...])
        a_ref[0] = g.astype(BF16)
        a_ref[1] = up.astype(BF16)
        act = (g * jax.nn.sigmoid(g)) * up
        return _dot(act.astype(BF16), wo_ref[...])

    def first(x_ref, vec_ref, wg_ref, wu_ref, wo_ref, h_ref, a_ref, u_ref):
        h, _, _ = _prenorm(x_ref[...], vec_ref)
        hb = h.astype(BF16)
        h_ref[...] = hb
        u_ref[...] = hidden(hb, wg_ref, wu_ref, wo_ref, a_ref)

    h, a, u_half = pl.pallas_call(
        first, name="ffn_fwd_first", grid=(S // tm,),
        in_specs=[pl.BlockSpec((tm, D), row)] + half(0),
        out_specs=[pl.BlockSpec((tm, D), row), a_spec(0), pl.BlockSpec((tm, D), row)],
        out_shape=[jax.ShapeDtypeStruct((S, D), BF16), a_shape, jax.ShapeDtypeStruct((S, D), F32)],
        compiler_params=_params("parallel"),
    )(x, vec, w_in, w_in, w_out)

    def second(x_ref, h_ref, uh_ref, vec_ref, wg_ref, wu_ref, wo_ref, a_in, xo_ref, a_ref, u_ref):
        u = uh_ref[...] + hidden(h_ref[...], wg_ref, wu_ref, wo_ref, a_ref)
        u_ref[...] = u
        uhat, _ = _rms(u)
        xo_ref[...] = x_ref[...] + (weight * (1.0 + vec_ref[4:5, :])) * (uhat * vec_ref[1:2, :])

    xo, a, u = pl.pallas_call(
        second, name="ffn_fwd_second", grid=(S // tm,),
        in_specs=[pl.BlockSpec((tm, D), row), pl.BlockSpec((tm, D), row), pl.BlockSpec((tm, D), row)] + half(1) + [_ANY],
        out_specs=[pl.BlockSpec((tm, D), row), a_spec(1), pl.BlockSpec((tm, D), row)],
        out_shape=[jax.ShapeDtypeStruct((S, D), F32), a_shape, jax.ShapeDtypeStruct((S, D), F32)],
        input_output_aliases={7: 1},
        compiler_params=_params("parallel"),
    )(x, h, u_half, vec, w_in, w_in, w_out, a)
    return xo, a, u, h


def ffn_bwd(dout, x, u, a, vec, w_in, w_out, weight):
    S = x.shape[0]
    tm = min(512, S)
    row = lambda i: (i, 0)
    half = lambda j: [pl.BlockSpec((2, tm, FSH), lambda i: (0, i, j)), _w3((8, D)),
                      pl.BlockSpec((None, D, FSH), lambda i: (j, 0, 0)),
                      pl.BlockSpec((None, D, FSH), lambda i: (j + 2, 0, 0)),
                      pl.BlockSpec((None, FSH, D), lambda i: (j, 0, 0))]
    half_out = lambda j: [pl.BlockSpec((tm, FSH), lambda i: (i, j)), pl.BlockSpec((2, tm, FSH), lambda i: (0, i, j))]
    half_shape = [jax.ShapeDtypeStruct((S, DFF), BF16), jax.ShapeDtypeStruct((2, S, DFF), BF16)]

    def hidden_bwd(du, a_ref, wg_ref, wu_ref, wo_ref, act_ref, da_ref):
        dact = _dot(du, wo_ref[...], NT)
        g = a_ref[0].astype(F32)
        up = a_ref[1].astype(F32)
        s = jax.nn.sigmoid(g)
        silu = g * s
        act_ref[...] = (silu * up).astype(BF16)
        dg = (dact * up * (s * (1.0 + g * (1.0 - s)))).astype(BF16)
        dup = (dact * silu).astype(BF16)
        da_ref[0] = dg
        da_ref[1] = dup
        return _dot(dg, wg_ref[...], NT) + _dot(dup, wu_ref[...], NT)

    def first(do_ref, u_ref, a_ref, vec_ref, wg_ref, wu_ref, wo_ref, du_ref, dh_ref, act_ref, da_ref, vg_ref):
        @pl.when(pl.program_id(0) == 0)
        def _():
            vg_ref[...] = jnp.zeros_like(vg_ref)

        du, dgate_rows, dgpost_rows = _postnorm_bwd(do_ref[...], u_ref[...], vec_ref, weight)
        vg_ref[2:3, :] += jnp.sum(dgate_rows, axis=0, keepdims=True)
        vg_ref[4:5, :] += jnp.sum(dgpost_rows, axis=0, keepdims=True)
        du = du.astype(BF16)
        du_ref[...] = du
        dh_ref[...] = hidden_bwd(du, a_ref, wg_ref, wu_ref, wo_ref, act_ref, da_ref)

    du, dh, act, da, vg_post = pl.pallas_call(
        first, name="ffn_bwd_first", grid=(S // tm,),
        in_specs=[pl.BlockSpec((tm, D), row), pl.BlockSpec((tm, D), row)] + half(0),
        out_specs=[pl.BlockSpec((tm, D), row), pl.BlockSpec((tm, D), row)] + half_out(0) + [_w3((8, D))],
        out_shape=[jax.ShapeDtypeStruct((S, D), BF16), jax.ShapeDtypeStruct((S, D), F32)] + half_shape
        + [jax.ShapeDtypeStruct((8, D), F32)],
        compiler_params=_params("arbitrary"),
    )(dout, u, a, vec, w_in, w_in, w_out)

    def second(do_ref, x_ref, du_ref, dh_ref, a_ref, vec_ref, wg_ref, wu_ref, wo_ref, act_in, da_in,
               dx_ref, act_ref, da_ref, vg_ref):
        @pl.when(pl.program_id(0) == 0)
        def _():
            vg_ref[...] = jnp.zeros_like(vg_ref)

        dh = dh_ref[...] + hidden_bwd(du_ref[...], a_ref, wg_ref, wu_ref, wo_ref, act_ref, da_ref)
        dx_ref[...] = do_ref[...] + _prenorm_bwd(dh, x_ref[...], vec_ref, vg_ref)

    dx, act, da, vg_pre = pl.pallas_call(
        second, name="ffn_bwd_second", grid=(S // tm,),
        in_specs=[pl.BlockSpec((tm, D), row), pl.BlockSpec((tm, D), row), pl.BlockSpec((tm, D), row),
                  pl.BlockSpec((tm, D), row)] + half(1) + [_ANY, _ANY],
        out_specs=[pl.BlockSpec((tm, D), row)] + half_out(1) + [_w3((8, D))],
        out_shape=[jax.ShapeDtypeStruct((S, D), F32)] + half_shape + [jax.ShapeDtypeStruct((8, D), F32)],
        input_output_aliases={9: 1, 10: 2},
        compiler_params=_params("arbitrary"),
    )(dout, x, du, dh, a, vec, w_in, w_in, w_out, act, da)
    return dx, du, act, da, vg_post + vg_pre


def dw_matmul(name, a, b, a_spec, b_spec, out_shape, out_spec, grid):
    def body(a_ref, b_ref, o_ref):
        @pl.when(pl.program_id(len(grid) - 1) == 0)
        def _():
            o_ref[...] = jnp.zeros_like(o_ref)

        o_ref[...] += _dot(a_ref[...], b_ref[...], TN)

    return pl.pallas_call(
        body, name=name, grid=grid, in_specs=[a_spec, b_spec], out_specs=out_spec,
        out_shape=jax.ShapeDtypeStruct(out_shape, F32),
        compiler_params=_params(*(["parallel"] * (len(grid) - 1) + ["arbitrary"])),
    )(a, b)


def ffn_dw(h, da, act, du):
    S = h.shape[0]
    tk = min(DW_TK, S)
    dw_in = dw_matmul("ffn_dw_in", h, da,
                      pl.BlockSpec((tk, D), lambda n, k: (k, 0)),
                      pl.BlockSpec((None, tk, FSH), lambda n, k: (n // 2, k, n % 2)),
                      (N_CHIP, D, FSH), pl.BlockSpec((None, D, FSH), lambda n, k: (n, 0, 0)),
                      (N_CHIP, S // tk))
    dw_out = dw_matmul("ffn_dw_out", act, du,
                       pl.BlockSpec((tk, FSH), lambda n, k: (k, n)),
                       pl.BlockSpec((tk, D), lambda n, k: (k, 0)),
                       (DFF, D), pl.BlockSpec((FSH, D), lambda n, k: (n, 0)),
                       (2, S // tk))
    return dw_in, dw_out


def _halo_specs(tm, S):
    nb = tm // HALO
    last = S // HALO - 1
    return [pl.BlockSpec((HALO, D), lambda i: (jnp.maximum(i * nb - 1, 0), 0)),
            pl.BlockSpec((tm, D), lambda i: (i, 0)),
            pl.BlockSpec((HALO, D), lambda i: (jnp.minimum((i + 1) * nb, last), 0))]


def _shift_rows(v, k):
    return pltpu.roll(v, k % v.shape[0], 0)


def _window_sum(v, g, forward):
    acc = v + _shift_rows(v, 1 if forward else -1)
    for step in (1, 2, 4)[:g]:
        acc = _shift_rows(acc, step) + _shift_rows(acc, -step)
    return acc


def _pool_count(t, w, S):
    return jnp.maximum(jnp.minimum(t + w // 2, S) - jnp.maximum(t - w // 2, 0), 1).astype(F32)


def pool_fwd(x, vec, pw, pvec):
    S = x.shape[0]
    tm = min(ROW_TILE, S)
    G = D // 4

    def body(xp_ref, x_ref, xn_ref, vec_ref, pw_ref, pv_ref, xo_ref, y_ref, z_ref):
        i = pl.program_id(0)
        xa = jnp.concatenate([xp_ref[...], x_ref[...], xn_ref[...]], axis=0)
        t = i * tm - HALO + lax.broadcasted_iota(jnp.int32, (tm + 2 * HALO, 1), 0)
        h, _, _ = _prenorm(xa, vec_ref)
        h = jnp.where((t >= 0) & (t < S), h, 0.0)
        tmain = t[HALO:HALO + tm]
        for g in range(4):
            hg = h[:, g * G:(g + 1) * G]
            pooled = _window_sum(hg, g, True)[HALO:HALO + tm] / _pool_count(tmain, POOL_WINDOWS[g], S)
            z = (pooled - hg[HALO:HALO + tm]).astype(BF16)
            z_ref[:, g * G:(g + 1) * G] = z
            y_ref[:, g * G:(g + 1) * G] = _dot(z, pw_ref[g]) + pv_ref[0:1, g * G:(g + 1) * G]
        u = y_ref[...] * pv_ref[1:2, :]
        uhat, _ = _rms(u)
        xo_ref[...] = x_ref[...] + (1.0 + vec_ref[4:5, :]) * (uhat * vec_ref[1:2, :])

    row = lambda i: (i, 0)
    full = lambda i: (0, 0)
    return pl.pallas_call(
        body, name="pool_fwd", grid=(S // tm,),
        in_specs=_halo_specs(tm, S) + [pl.BlockSpec((8, D), full), pl.BlockSpec((4, G, G), lambda i: (0, 0, 0)),
                                       pl.BlockSpec((8, D), full)],
        out_specs=[pl.BlockSpec((tm, D), row)] * 3,
        out_shape=[jax.ShapeDtypeStruct((S, D), F32), jax.ShapeDtypeStruct((S, D), F32),
                   jax.ShapeDtypeStruct((S, D), BF16)],
        compiler_params=_params("parallel"),
    )(x, x, x, vec, pw, pvec)


def pool_bwd(dout, x, y, z, vec, pw, pvec):
    S = x.shape[0]
    tm = min(ROW_TILE, S)
    G = D // 4
    R = G // N_CHIP

    def body(dop_ref, do_ref, don_ref, yp_ref, y_ref, yn_ref, x_ref, z_ref, vec_ref, pw_ref, pv_ref,
             dx_ref, vg_ref, pg_ref, dw_ref, dh_ref):
        i = pl.program_id(0)

        @pl.when(i == 0)
        def _():
            vg_ref[...] = jnp.zeros_like(vg_ref)
            pg_ref[...] = jnp.zeros_like(pg_ref)
            dw_ref[...] = jnp.zeros_like(dw_ref)

        doa = jnp.concatenate([dop_ref[...], do_ref[...], don_ref[...]], axis=0)
        ya = jnp.concatenate([yp_ref[...], y_ref[...], yn_ref[...]], axis=0)
        t = i * tm - HALO + lax.broadcasted_iota(jnp.int32, (tm + 2 * HALO, 1), 0)
        inside = (t >= 0) & (t < S)
        main = (t >= i * tm) & (t < (i + 1) * tm)
        du, dgate_rows, dgpost_rows = _postnorm_bwd(doa, ya * pv_ref[1:2, :], vec_ref, 1.0)
        du = jnp.where(inside, du, 0.0)
        vg_ref[2:3, :] += jnp.sum(jnp.where(main, dgate_rows, 0.0), axis=0, keepdims=True)
        vg_ref[4:5, :] += jnp.sum(jnp.where(main, dgpost_rows, 0.0), axis=0, keepdims=True)
        dy = du * pv_ref[1:2, :]
        pg_ref[0:1, :] += jnp.sum(jnp.where(main, dy, 0.0), axis=0, keepdims=True)
        pg_ref[1:2, :] += jnp.sum(jnp.where(main, du * ya, 0.0), axis=0, keepdims=True)
        for g in range(4):
            dyg = dy[:, g * G:(g + 1) * G].astype(BF16)
            dz = _dot(dyg, pw_ref[g], NT)
            e = dz / _pool_count(t, POOL_WINDOWS[g], S)
            dh_ref[:, g * G:(g + 1) * G] = (_window_sum(e, g, False) - dz)[HALO:HALO + tm]
            dwg = _dot(z_ref[:, g * G:(g + 1) * G], dyg[HALO:HALO + tm], TN)
            for q in range(N_CHIP):
                dw_ref[q, g] += dwg[q * R:(q + 1) * R, :]
        dx_ref[...] = do_ref[...] + _prenorm_bwd(dh_ref[...], x_ref[...], vec_ref, vg_ref)

    row = lambda i: (i, 0)
    full = lambda i: (0, 0)
    halo = _halo_specs(tm, S)
    return pl.pallas_call(
        body, name="pool_bwd", grid=(S // tm,),
        in_specs=halo + halo + [pl.BlockSpec((tm, D), row), pl.BlockSpec((tm, D), row), pl.BlockSpec((8, D), full),
                                pl.BlockSpec((4, G, G), lambda i: (0, 0, 0)), pl.BlockSpec((8, D), full)],
        out_specs=[pl.BlockSpec((tm, D), row), pl.BlockSpec((8, D), full), pl.BlockSpec((8, D), full),
                   pl.BlockSpec((N_CHIP, 4, R, G), lambda i: (0, 0, 0, 0))],
        out_shape=[jax.ShapeDtypeStruct((S, D), F32), jax.ShapeDtypeStruct((8, D), F32),
                   jax.ShapeDtypeStruct((8, D), F32), jax.ShapeDtypeStruct((N_CHIP, 4, R, G), F32)],
        scratch_shapes=[pltpu.VMEM((tm, D), F32)],
        compiler_params=_params("arbitrary"),
    )(dout, dout, dout, y, y, y, x, z, vec, pw, pvec)


N_PAIR = N_HEADS // 2
SLOTS = LANES // ROPE
ROPE_ALL = N_HEADS * ROPE
NOPE_ALL = N_HEADS * NOPE
LAT_ALL = N_HEADS * KVL
DLAT = QL + KVL + 2 * LANES
DQ_ALL = NOPE_ALL + 2 * ROPE_ALL


def _w3(shape):
    return pl.BlockSpec(shape, lambda i: (0,) * len(shape))


def _slot_mask(hd, rows):
    lane = lax.broadcasted_iota(jnp.int32, (rows, LANES), 1)
    return (lane // ROPE) == (hd % SLOTS)


MLA_WEIGHTS = ("wq", "wkv", "wkr4", "wkrs4", "qn", "kvn", "wn", "wr", "wrs", "bduk")


def _mla_weight_specs():
    return [_w3((D, QL)), _w3((D, KVL)), _w3((D, LANES)), _w3((D, LANES)), _w3((1, QL)), _w3((1, KVL)),
            _w3((QL, NOPE_ALL)), _w3((QL, ROPE_ALL)), _w3((QL, ROPE_ALL)), _w3((N_PAIR, 2 * NOPE, 2 * KVL))]


def mla_pre(x, vec, mw, tabs):
    S = x.shape[0]
    tm = min(ROW_TILE, S)

    def body(x_ref, vec_ref, cos_ref, sin_ref, wq_ref, wkv_ref, wkr_ref, wkrs_ref, qn_ref, kvn_ref,
             wn_ref, wr_ref, wrs_ref, bduk_ref,
             h_ref, cq_ref, ckv_ref, cqn_ref, qnope_ref, qcat_ref, kcat_ref, vcat_ref):
        h, _, _ = _prenorm(x_ref[...], vec_ref)
        hb = h.astype(BF16)
        h_ref[...] = hb
        cq_raw = _dot(hb, wq_ref[...])
        ckv_raw = _dot(hb, wkv_ref[...])
        cq_ref[...] = cq_raw
        ckv_ref[...] = ckv_raw
        cos, sin = cos_ref[...], sin_ref[...]
        ckv = (_rms(ckv_raw)[0] * kvn_ref[...]).astype(BF16)
        kcat_ref[:, 0:KVL] = ckv
        kcat_ref[:, KVL:] = (_dot(hb, wkr_ref[...]) * cos + _dot(hb, wkrs_ref[...]) * sin).astype(BF16)
        vcat_ref[:, 0:KVL] = ckv
        ones = lax.broadcasted_iota(jnp.int32, (tm, QPAD - KVL), 1) == 0
        vcat_ref[:, KVL:] = jnp.where(ones, 1.0, 0.0).astype(BF16)
        cqb = (_rms(cq_raw)[0] * qn_ref[...]).astype(BF16)
        cqn_ref[...] = cqb
        qn = _dot(cqb, wn_ref[...]).astype(BF16)
        qnope_ref[...] = qn
        cos4, sin4 = jnp.tile(cos, (1, SLOTS)), jnp.tile(sin, (1, SLOTS))
        qr = ((_dot(cqb, wr_ref[...]) * cos4 + _dot(cqb, wrs_ref[...]) * sin4) * ATTN_SCALE).astype(BF16)
        for j in range(N_PAIR):
            ql = (_dot(qn[:, 2 * NOPE * j:2 * NOPE * (j + 1)], bduk_ref[j]) * ATTN_SCALE).astype(BF16)
            for hd in (2 * j, 2 * j + 1):
                qcat_ref[hd, :, 0:KVL] = ql[:, KVL * (hd - 2 * j):KVL * (hd - 2 * j + 1)]
                group = qr[:, LANES * (hd // SLOTS):LANES * (hd // SLOTS + 1)]
                qcat_ref[hd, :, KVL:] = jnp.where(_slot_mask(hd, tm), group, jnp.zeros_like(group))

    row = lambda i: (i, 0)
    hrow = lambda i: (0, i, 0)
    return pl.pallas_call(
        body, name="mla_pre", grid=(S // tm,),
        in_specs=[pl.BlockSpec((tm, D), row), _w3((8, D)), pl.BlockSpec((tm, LANES), row), pl.BlockSpec((tm, LANES), row)]
        + _mla_weight_specs(),
        out_specs=[pl.BlockSpec((tm, D), row), pl.BlockSpec((tm, QL), row), pl.BlockSpec((tm, KVL), row),
                   pl.BlockSpec((tm, QL), row), pl.BlockSpec((tm, NOPE_ALL), row),
                   pl.BlockSpec((N_HEADS, tm, QPAD), hrow), pl.BlockSpec((tm, QPAD), row),
                   pl.BlockSpec((tm, QPAD), row)],
        out_shape=[jax.ShapeDtypeStruct((S, D), BF16), jax.ShapeDtypeStruct((S, QL), F32),
                   jax.ShapeDtypeStruct((S, KVL), F32), jax.ShapeDtypeStruct((S, QL), BF16),
                   jax.ShapeDtypeStruct((S, NOPE_ALL), BF16), jax.ShapeDtypeStruct((N_HEADS, S, QPAD), BF16),
                   jax.ShapeDtypeStruct((S, QPAD), BF16), jax.ShapeDtypeStruct((S, QPAD), BF16)],
        compiler_params=_params("parallel"),
    )(x, vec, tabs[0], tabs[1], *[mw[k] for k in MLA_WEIGHTS])


def attn_fwd(qcat, kcat, vcat):
    S = kcat.shape[0]
    tq = min(ATTN_TQ, S)
    kc = min(ATTN_KC, S)

    def body(q_ref, k_ref, v_ref, o_ref, lse_ref):
        q = q_ref[...]
        m = jnp.full((tq, 1), -jnp.inf, F32)
        ov = jnp.zeros((tq, QPAD), F32)
        for c in range(S // kc):
            s = _dot(q, k_ref[c * kc:(c + 1) * kc, :], NT)
            m_new = jnp.maximum(m, jnp.max(s, axis=-1, keepdims=True))
            p = jnp.exp(s - m_new).astype(BF16)
            ov = ov * jnp.exp(m - m_new) + _dot(p, v_ref[c * kc:(c + 1) * kc, :])
            m = m_new
        l = ov[:, KVL:KVL + 1]
        o_ref[...] = (ov[:, 0:KVL] * (1.0 / l)).astype(BF16)
        lse_ref[...] = _as_row(m + jnp.log(l))

    return pl.pallas_call(
        body, name="attn_fwd", grid=(N_HEADS, S // tq),
        in_specs=[pl.BlockSpec((None, tq, QPAD), lambda h, i: (h, i, 0)),
                  pl.BlockSpec((S, QPAD), lambda h, i: (0, 0)),
                  pl.BlockSpec((S, QPAD), lambda h, i: (0, 0))],
        out_specs=[pl.BlockSpec((tq, KVL), lambda h, i: (i, h)),
                   pl.BlockSpec((None, 1, tq), lambda h, i: (h, 0, i))],
        out_shape=[jax.ShapeDtypeStruct((S, LAT_ALL), BF16), jax.ShapeDtypeStruct((N_HEADS, 1, S), F32)],
        compiler_params=_params("parallel", "parallel"),
    )(qcat, kcat, vcat)


def mla_post(olat, x, vec, bduv, wo):
    S = x.shape[0]
    tm = min(ROW_TILE, S)

    def body(o_ref, x_ref, vec_ref, bduv_ref, wo_ref, xo_ref, u_ref, ocat_ref):
        for j in range(N_PAIR):
            oc = _dot(o_ref[:, 2 * KVL * j:2 * KVL * (j + 1)], bduv_ref[j])
            ocat_ref[:, 2 * VH * j:2 * VH * (j + 1)] = oc.astype(BF16)
        u = _dot(ocat_ref[...], wo_ref[...])
        u_ref[...] = u
        uhat, _ = _rms(u)
        xo_ref[...] = x_ref[...] + (1.0 + vec_ref[4:5, :]) * (uhat * vec_ref[1:2, :])

    row = lambda i: (i, 0)
    return pl.pallas_call(
        body, name="mla_post", grid=(S // tm,),
        in_specs=[pl.BlockSpec((tm, LAT_ALL), row), pl.BlockSpec((tm, D), row), _w3((8, D)),
                  _w3((N_PAIR, 2 * KVL, 2 * VH)), _w3((D, D))],
        out_specs=[pl.BlockSpec((tm, D), row), pl.BlockSpec((tm, D), row), pl.BlockSpec((tm, D), row)],
        out_shape=[jax.ShapeDtypeStruct((S, D), F32), jax.ShapeDtypeStruct((S, D), F32),
                   jax.ShapeDtypeStruct((S, D), BF16)],
        compiler_params=_params("parallel"),
    )(olat, x, vec, bduv, wo)


def mla_post_bwd(dout, u, olat, vec, bduv, wo):
    S = u.shape[0]
    tm = min(ROW_TILE, S)

    def body(do_ref, u_ref, o_ref, vec_ref, bduv_ref, wo_ref, du_ref, docat_ref, dolat_ref, delta_ref, vg_ref):
        @pl.when(pl.program_id(0) == 0)
        def _():
            vg_ref[...] = jnp.zeros_like(vg_ref)

        du, dgate_rows, dgpost_rows = _postnorm_bwd(do_ref[...], u_ref[...], vec_ref, 1.0)
        vg_ref[2:3, :] += jnp.sum(dgate_rows, axis=0, keepdims=True)
        vg_ref[4:5, :] += jnp.sum(dgpost_rows, axis=0, keepdims=True)
        dub = du.astype(BF16)
        du_ref[...] = dub
        docat_ref[...] = _dot(dub, wo_ref[...], NT).astype(BF16)
        for j in range(N_PAIR):
            dol = _dot(docat_ref[:, 2 * VH * j:2 * VH * (j + 1)], bduv_ref[j], NT).astype(BF16)
            dolat_ref[:, 2 * KVL * j:2 * KVL * (j + 1)] = dol
            prod = dol.astype(F32) * o_ref[:, 2 * KVL * j:2 * KVL * (j + 1)].astype(F32)
            delta_ref[2 * j] = _as_row(jnp.sum(prod[:, 0:KVL], axis=-1, keepdims=True))
            delta_ref[2 * j + 1] = _as_row(jnp.sum(prod[:, KVL:], axis=-1, keepdims=True))

    row = lambda i: (i, 0)
    hrow = lambda i: (0, i, 0)
    return pl.pallas_call(
        body, name="mla_post_bwd", grid=(S // tm,),
        in_specs=[pl.BlockSpec((tm, D), row), pl.BlockSpec((tm, D), row), pl.BlockSpec((tm, LAT_ALL), row),
                  _w3((8, D)), _w3((N_PAIR, 2 * KVL, 2 * VH)), _w3((D, D))],
        out_specs=[pl.BlockSpec((tm, D), row), pl.BlockSpec((tm, D), row),
                   pl.BlockSpec((tm, LAT_ALL), row), pl.BlockSpec((N_HEADS, 1, tm), lambda i: (0, 0, i)), _w3((8, D))],
        out_shape=[jax.ShapeDtypeStruct((S, D), BF16), jax.ShapeDtypeStruct((S, D), BF16),
                   jax.ShapeDtypeStruct((S, LAT_ALL), BF16), jax.ShapeDtypeStruct((N_HEADS, 1, S), F32),
                   jax.ShapeDtypeStruct((8, D), F32)],
        compiler_params=_params("arbitrary"),
    )(dout, u, olat, vec, bduv, wo)


def attn_bwd(qcat, kcat, kcat_t, dolat, lse_row, delta_row):
    S = kcat.shape[0]
    tq = min(ATTN_TQ, S)
    kc = min(ATTN_KC, S)

    def body(q_ref, k_ref, kt_ref, do_ref, lse_ref, dl_ref, dq_ref, dk_ref, dv_ref):
        @pl.when((pl.program_id(0) == 0) & (pl.program_id(1) == 0))
        def _():
            dk_ref[...] = jnp.zeros_like(dk_ref)
            dv_ref[...] = jnp.zeros_like(dv_ref)

        q, do = q_ref[...], do_ref[...]
        lse, dl = lse_ref[...], dl_ref[...]
        dqt = jnp.zeros((QPAD, tq), F32)
        for c in range(S // kc):
            rows = slice(c * kc, (c + 1) * kc)
            st = _dot(k_ref[rows, :], q, NT)
            pt = jnp.exp(st - lse)
            dpt = _dot(k_ref[rows, 0:KVL], do, NT)
            dst = (pt * (dpt - dl)).astype(BF16)
            dv_ref[rows, :] += _dot(pt.astype(BF16), do)
            dk_ref[rows, :] += _dot(dst, q)
            dqt = dqt + _dot(kt_ref[:, rows], dst)
        dq_ref[...] = (dqt.T * ATTN_SCALE).astype(BF16)

    return pl.pallas_call(
        body, name="attn_bwd", grid=(N_HEADS, S // tq),
        in_specs=[pl.BlockSpec((None, tq, QPAD), lambda h, i: (h, i, 0)),
                  pl.BlockSpec((S, QPAD), lambda h, i: (0, 0)),
                  pl.BlockSpec((QPAD, S), lambda h, i: (0, 0)),
                  pl.BlockSpec((tq, KVL), lambda h, i: (i, h)),
                  pl.BlockSpec((None, 1, tq), lambda h, i: (h, 0, i)),
                  pl.BlockSpec((None, 1, tq), lambda h, i: (h, 0, i))],
        out_specs=[pl.BlockSpec((None, tq, QPAD), lambda h, i: (h, i, 0)),
                   pl.BlockSpec((S, QPAD), lambda h, i: (0, 0)),
                   pl.BlockSpec((S, KVL), lambda h, i: (0, 0))],
        out_shape=[jax.ShapeDtypeStruct((N_HEADS, S, QPAD), BF16), jax.ShapeDtypeStruct((S, QPAD), F32),
                   jax.ShapeDtypeStruct((S, KVL), F32)],
        compiler_params=_params("arbitrary", "arbitrary"),
    )(qcat, kcat, kcat_t, dolat, lse_row, delta_row)


def mla_pre_bwd(dout, dq, dk, dv, x, cq_raw, ckv_raw, vec, mw, tabs):
    S = x.shape[0]
    tm = min(ROW_TILE, S)

    def body(do_ref, dq_ref, dk_ref, dv_ref, x_ref, cq_ref, ckv_ref, vec_ref, cos_ref, sin_ref,
             wq_ref, wkv_ref, wkr_ref, wkrs_ref, qn_ref, kvn_ref, wn_ref, wr_ref, wrs_ref, bduk_ref,
             dx_ref, dlat_ref, dql_ref, dqcat_ref, vg_ref, ng_ref):
        @pl.when(pl.program_id(0) == 0)
        def _():
            vg_ref[...] = jnp.zeros_like(vg_ref)
            ng_ref[...] = jnp.zeros_like(ng_ref)

        cos, sin = cos_ref[...], sin_ref[...]
        for j in range(N_PAIR):
            dql = jnp.concatenate([dq_ref[2 * j, :, 0:KVL], dq_ref[2 * j + 1, :, 0:KVL]], axis=1)
            dql_ref[:, 2 * KVL * j:2 * KVL * (j + 1)] = dql
            dqcat_ref[:, 2 * NOPE * j:2 * NOPE * (j + 1)] = _dot(dql, bduk_ref[j], NT).astype(BF16)
        groups = []
        for grp in range(N_HEADS // SLOTS):
            acc = jnp.zeros((tm, LANES), F32)
            for hd in range(SLOTS * grp, SLOTS * (grp + 1)):
                acc = acc + jnp.where(_slot_mask(hd, tm), dq_ref[hd, :, KVL:].astype(F32), 0.0)
            groups.append(acc)
        dqr = jnp.concatenate(groups, axis=1)
        qa = (dqr * jnp.tile(cos, (1, SLOTS))).astype(BF16)
        qb = (dqr * jnp.tile(sin, (1, SLOTS))).astype(BF16)
        dqcat_ref[:, NOPE_ALL:NOPE_ALL + ROPE_ALL] = qa
        dqcat_ref[:, NOPE_ALL + ROPE_ALL:] = qb
        dcq = _dot(dqcat_ref[:, 0:NOPE_ALL], wn_ref[...], NT) + _dot(qa, wr_ref[...], NT) + _dot(qb, wrs_ref[...], NT)
        cqh, rq = _rms(cq_ref[...])
        ng_ref[0:1, :] += jnp.sum(dcq * cqh, axis=0, keepdims=True)
        dcq_raw = _rms_bwd(cqh, rq, dcq * qn_ref[...]).astype(BF16)
        dckv = dk_ref[:, 0:KVL] + dv_ref[...]
        ckvh, rk = _rms(ckv_ref[...])
        ng_ref[1:2, 0:KVL] += jnp.sum(dckv * ckvh, axis=0, keepdims=True)
        dckv_raw = _rms_bwd(ckvh, rk, dckv * kvn_ref[...]).astype(BF16)
        dkr = dk_ref[:, KVL:]
        ka = (dkr * cos).astype(BF16)
        kb = (dkr * sin).astype(BF16)
        dlat_ref[:, 0:QL] = dcq_raw
        dlat_ref[:, QL:QL + KVL] = dckv_raw
        dlat_ref[:, QL + KVL:QL + KVL + LANES] = ka
        dlat_ref[:, QL + KVL + LANES:] = kb
        dh = (_dot(dcq_raw, wq_ref[...], NT) + _dot(dckv_raw, wkv_ref[...], NT)
              + _dot(ka, wkr_ref[...], NT) + _dot(kb, wkrs_ref[...], NT))
        dx_ref[...] = do_ref[...] + _prenorm_bwd(dh, x_ref[...], vec_ref, vg_ref)

    row = lambda i: (i, 0)
    hrow = lambda i: (0, i, 0)
    return pl.pallas_call(
        body, name="mla_pre_bwd", grid=(S // tm,),
        in_specs=[pl.BlockSpec((tm, D), row), pl.BlockSpec((N_HEADS, tm, QPAD), hrow), pl.BlockSpec((tm, QPAD), row),
                  pl.BlockSpec((tm, KVL), row), pl.BlockSpec((tm, D), row), pl.BlockSpec((tm, QL), row),
                  pl.BlockSpec((tm, KVL), row), _w3((8, D)), pl.BlockSpec((tm, LANES), row), pl.BlockSpec((tm, LANES), row)]
        + _mla_weight_specs(),
        out_specs=[pl.BlockSpec((tm, D), row), pl.BlockSpec((tm, DLAT), row), pl.BlockSpec((tm, LAT_ALL), row),
                   pl.BlockSpec((tm, DQ_ALL), row), _w3((8, D)), _w3((8, QL))],
        out_shape=[jax.ShapeDtypeStruct((S, D), F32), jax.ShapeDtypeStruct((S, DLAT), BF16),
                   jax.ShapeDtypeStruct((S, LAT_ALL), BF16), jax.ShapeDtypeStruct((S, DQ_ALL), BF16),
                   jax.ShapeDtypeStruct((8, D), F32), jax.ShapeDtypeStruct((8, QL), F32)],
        compiler_params=_params("arbitrary"),
    )(dout, dq, dk, dv, x, cq_raw, ckv_raw, vec, tabs[0], tabs[1], *[mw[k] for k in MLA_WEIGHTS])


def mla_dw(h, dlat, cqn, dqcat, dql, qnope, olat, docat, ocat, du):
    S = h.shape[0]
    tk = min(DW_TK, S)
    nk = S // tk
    flat = lambda w: pl.BlockSpec((tk, w), lambda k: (k, 0))
    cols = lambda w: pl.BlockSpec((tk, w), lambda n, k: (k, n))
    pair_o = pl.BlockSpec((None, 2 * KVL, 2 * NOPE), lambda n, k: (n, 0, 0))
    g = {}
    g["in"] = dw_matmul("mla_dw_in", h, dlat, flat(D), flat(DLAT), (D, DLAT),
                        pl.BlockSpec((D, DLAT), lambda k: (0, 0)), (nk,))
    g["q"] = dw_matmul("mla_dw_q", cqn, dqcat, flat(QL), flat(DQ_ALL), (QL, DQ_ALL),
                       pl.BlockSpec((QL, DQ_ALL), lambda k: (0, 0)), (nk,))
    g["uk"] = dw_matmul("mla_dw_uk", dql, qnope, cols(2 * KVL), cols(2 * NOPE), (N_PAIR, 2 * KVL, 2 * NOPE), pair_o,
                        (N_PAIR, nk))
    g["uv"] = dw_matmul("mla_dw_uv", olat, docat, cols(2 * KVL), cols(2 * VH), (N_PAIR, 2 * KVL, 2 * VH), pair_o,
                        (N_PAIR, nk))
    g["o"] = dw_matmul("mla_dw_o", ocat, du, cols(256), pl.BlockSpec((tk, D), lambda n, k: (k, 0)), (D, D),
                       pl.BlockSpec((256, D), lambda n, k: (n, 0)), (D // 256, nk))
    return g


def loss_head(y, target):
    S = y.shape[0]
    tm = min(2 * ROW_TILE, S)

    def body(y_ref, t_ref, loss_ref, dy_ref):
        @pl.when(pl.program_id(0) == 0)
        def _():
            loss_ref[...] = jnp.zeros_like(loss_ref)

        err = y_ref[...] - t_ref[...]
        dy_ref[...] = err * (1.0 / D)
        loss_ref[...] += 0.5 * jnp.sum(jnp.mean(err * err, axis=-1, keepdims=True), axis=0, keepdims=True)

    row = lambda i: (i, 0)
    return pl.pallas_call(
        body, name="loss_head", grid=(S // tm,),
        in_specs=[pl.BlockSpec((tm, D), row), pl.BlockSpec((tm, D), row)],
        out_specs=[pl.BlockSpec((1, 1), lambda i: (0, 0)), pl.BlockSpec((tm, D), row)],
        out_shape=[jax.ShapeDtypeStruct((1, 1), F32), jax.ShapeDtypeStruct((S, D), F32)],
        compiler_params=_params("arbitrary"),
    )(y, target)


MOD_COLS = 9 * D // N_CHIP


def mod_fwd(c_pad, ada_w, ada_b_loc):
    tn = MOD_COLS // 3

    def body(c_ref, w_ref, b_ref, o_ref):
        c = c_ref[...]
        sc = (c * jax.nn.sigmoid(c)).astype(BF16)
        o_ref[...] = _dot(sc, w_ref[...].astype(BF16)) + b_ref[...]

    return pl.pallas_call(
        body, name="mod_fwd", grid=(2, 3),
        in_specs=[pl.BlockSpec((16, D), lambda i, n: (0, 0)), pl.BlockSpec((None, D, tn), lambda i, n: (i, 0, n)),
                  pl.BlockSpec((None, 1, tn), lambda i, n: (i, 0, n))],
        out_specs=pl.BlockSpec((None, 16, tn), lambda i, n: (i, 0, n)),
        out_shape=jax.ShapeDtypeStruct((2, 16, MOD_COLS), F32),
        compiler_params=_params("parallel", "parallel"),
    )(c_pad, ada_w, ada_b_loc)


def _adamw_math(w, g, m, v):
    m = ADAM_B1 * m + (1.0 - ADAM_B1) * g
    v = ADAM_B2 * v + (1.0 - ADAM_B2) * (g * g)
    m_hat = m / (1.0 - ADAM_B1 ** ADAM_STEP)
    v_hat = v / (1.0 - ADAM_B2 ** ADAM_STEP)
    delta = -ADAM_LR * (m_hat / (jnp.sqrt(v_hat) + ADAM_EPS) + ADAM_WD * w)
    return delta, m, v


def adamw(name, w, g, m, v, part=None, prev=None, copy_grad=False):
    shape = w.shape
    if part is None and w.size * 4 <= (1 << 20):
        whole = pl.BlockSpec(shape, lambda i: (0,) * len(shape))

        def small_body(w_ref, g_ref, m_ref, v_ref, d_ref, mo_ref, vo_ref):
            d_ref[...], mo_ref[...], vo_ref[...] = _adamw_math(w_ref[...], g_ref[...], m_ref[...], v_ref[...])

        return pl.pallas_call(
            small_body, name=name, grid=(1,), in_specs=[whole] * 4, out_specs=[whole] * 3,
            out_shape=[jax.ShapeDtypeStruct(shape, F32)] * 3, compiler_params=_params("arbitrary"),
        )(w, g, m, v)
    cols = shape[-1]
    rows = w.size // cols
    per_entry = rows // shape[0] if part is not None else rows
    tr = per_entry
    budget_rows = (2 << 20) // (cols * 4)
    for cand in range(min(per_entry, budget_rows) // 8 * 8, 0, -8):
        if per_entry % cand == 0:
            tr = cand
            break
    first, count = part if part is not None else (0, 1)
    tiles = per_entry // tr

    n_out = 4 if copy_grad else 3

    def body(w_ref, g_ref, m_ref, v_ref, *rest):
        outs = rest[-n_out:]
        outs[0][...], outs[1][...], outs[2][...] = _adamw_math(w_ref[...], g_ref[...], m_ref[...], v_ref[...])
        if copy_grad:
            outs[3][...] = g_ref[...]

    spec = pl.BlockSpec((tr, cols), lambda i: (i + first * tiles, 0))
    operands = [a.reshape(rows, cols) for a in (w, g, m, v)]
    aliases = {}
    if prev is not None:
        operands += [p.reshape(rows, cols) for p in prev]
        aliases = {4 + t: t for t in range(n_out)}
    outs = pl.pallas_call(
        body, name=name, grid=(count * tiles,), in_specs=[spec] * 4 + [_ANY] * (len(operands) - 4),
        out_specs=[spec] * n_out, out_shape=[jax.ShapeDtypeStruct((rows, cols), F32)] * n_out,
        input_output_aliases=aliases, compiler_params=_params("parallel"),
    )(*operands)
    return [o.reshape(shape) for o in outs]


def adamw_ada(c_pad, dmod, w, m, v):
    tr = 256

    def body(c_ref, dm_ref, w_ref, m_ref, v_ref, g_ref, d_ref, mo_ref, vo_ref):
        c = c_ref[...]
        sc = (c * jax.nn.sigmoid(c)).astype(BF16)
        g = _dot(sc, dm_ref[...].astype(BF16), TN)
        g_ref[...] = g
        d_ref[...], mo_ref[...], vo_ref[...] = _adamw_math(w_ref[...], g, m_ref[...], v_ref[...])

    wspec = pl.BlockSpec((None, tr, MOD_COLS), lambda i, r: (i, r, 0))
    return pl.pallas_call(
        body, name="adamw_ada", grid=(2, D // tr),
        in_specs=[pl.BlockSpec((16, tr), lambda i, r: (0, r)),
                  pl.BlockSpec((None, 16, MOD_COLS), lambda i, r: (i, 0, 0)), wspec, wspec, wspec],
        out_specs=[wspec] * 4,
        out_shape=[jax.ShapeDtypeStruct((2, D, MOD_COLS), F32)] * 4,
        compiler_params=_params("parallel", "parallel"),
    )(c_pad, dmod, w, m, v)


def sum_devices(name, a):
    _, R, C = a.shape
    tr = R
    for cand in (64, 32, 16, 8):
        if R % cand == 0:
            tr = cand
            break

    def body(a_ref, o_ref):
        acc = a_ref[0]
        for dev in range(1, N_DEV):
            acc = acc + a_ref[dev]
        o_ref[...] = acc

    return pl.pallas_call(
        body, name=name, grid=(R // tr,),
        in_specs=[pl.BlockSpec((N_DEV, tr, C), lambda i: (0, i, 0))],
        out_specs=pl.BlockSpec((tr, C), lambda i: (i, 0)),
        out_shape=jax.ShapeDtypeStruct((R, C), F32),
        compiler_params=_params("parallel"),
    )(a)


def _place():
    return lax.axis_index("x"), lax.axis_index("y"), lax.axis_index("c")


def _other_chips(x, y):
    return [(1 - x, y), (x, 1 - y), (1 - x, 1 - y)]


def gather_devices(name, a):
    m_per, n = a.shape

    def body(x_ref, out_ref, send_sems, recv_sems, local_sem):
        x, y, c = _place()
        me, sibling = (x, y, c), (x, y, 1 - c)
        chips = _other_chips(x, y)

        def rows(px, py, pc):
            return out_ref.at[pl.ds((4 * px + 2 * py + pc) * m_per, m_per), :]

        def copy(k, block, to, src=None):
            return pltpu.make_async_remote_copy(
                src_ref=rows(*block) if src is None else src, dst_ref=rows(*block),
                send_sem=send_sems.at[k], recv_sem=recv_sems.at[k], device_id=to, device_id_type=MESH)

        mine = pltpu.make_async_copy(x_ref, rows(*me), local_sem)
        mine.start()
        first = [copy(0, me, sibling, src=x_ref)]
        first += [copy(1 + j, me, (*chip, c), src=x_ref) for j, chip in enumerate(chips)]
        for cp in first:
            cp.start()
        passed = [copy(4 + j, (*chip, c), sibling) for j, chip in enumerate(chips)]
        for j, chip in enumerate(chips):
            copy(1 + j, (*chip, c), me).wait_recv()
            passed[j].start()
        copy(0, sibling, me).wait_recv()
        for j, chip in enumerate(chips):
            copy(4 + j, (*chip, 1 - c), me).wait_recv()
        for cp in first + passed:
            cp.wait_send()
        mine.wait()

    out = pl.pallas_call(
        body, name=name,
        out_shape=jax.ShapeDtypeStruct((N_DEV * m_per, n), a.dtype),
        in_specs=[pl.BlockSpec(memory_space=pltpu.VMEM)],
        out_specs=pl.BlockSpec(memory_space=pltpu.VMEM),
        scratch_shapes=[pltpu.SemaphoreType.DMA((7,)), pltpu.SemaphoreType.DMA((7,)), pltpu.SemaphoreType.DMA],
        compiler_params=pltpu.CompilerParams(vmem_limit_bytes=VMEM_LIMIT),
    )(a)
    return out.reshape(N_DEV, m_per, n)


_ANY = pl.BlockSpec(memory_space=pl.ANY)


def _hbm_ref(a):
    return jax.new_ref(a, memory_space=pltpu.MemorySpace.HBM)


def _hbm_empty(shape, dtype):
    return jax.empty_ref(jax.ShapeDtypeStruct(shape, dtype), memory_space=pltpu.MemorySpace.HBM)


ID_PAIR, ID_CHIPS, ID_SHARE, ID_UKV = 8, 9, 10, 11


def _sequencer(name, collective_id, n_sem, peers_of, program):
    sems = pltpu.SemaphoreType.DMA((n_sem,))

    @pl.kernel(mesh=plsc.ScalarSubcoreMesh(axis_name="seq", num_cores=1), name=name, scratch_types=[sems, sems],
               compiler_params=pltpu.CompilerParams(collective_id=collective_id))
    def launch(send_sem, recv_sem):
        x, y, c = _place()
        peers = peers_of(x, y, c)
        barrier = pltpu.get_barrier_semaphore()
        for peer in peers:
            pl.semaphore_signal(barrier, inc=1, device_id=peer, device_id_type=MESH)
        pl.semaphore_wait(barrier, len(peers))
        program(x, y, c, send_sem, recv_sem)

    launch()


def gather_weights(name, stage, arrays):
    n = len(arrays)
    refs = [_hbm_ref(a) for a in arrays]

    def program(x, y, c, send_sem, recv_sem):
        me = 2 * x + y
        chips = _other_chips(x, y)

        def ici(t, r, half):
            cx, cy = chips[r]
            mine = refs[t].at[me, half]
            return pltpu.make_async_remote_copy(
                src_ref=mine, dst_ref=mine, send_sem=send_sem.at[3 * t + r], recv_sem=recv_sem.at[3 * t + r],
                device_id=(cx, cy, c), device_id_type=MESH)

        def d2d(t, r, half):
            cx, cy = chips[r]
            there = refs[t].at[2 * cx + cy, half]
            k = 3 * n + 3 * t + r
            return pltpu.make_async_remote_copy(
                src_ref=there, dst_ref=there, send_sem=send_sem.at[k], recv_sem=recv_sem.at[k],
                device_id=(x, y, 1 - c), device_id_type=MESH)

        for t in range(n):
            for r in range(3):
                ici(t, r, c).start()
        for t in range(n):
            for r in range(3):
                ici(t, r, c).wait_recv()
                d2d(t, r, c).start()
        for t in range(n):
            for r in range(3):
                d2d(t, r, 1 - c).wait_recv()
        for t in range(n):
            for r in range(3):
                ici(t, r, c).wait_send()
                d2d(t, r, c).wait_send()

    _sequencer(name, stage, 6 * n, lambda x, y, c: [(x, y, 1 - c)] + [(cx, cy, c) for cx, cy in _other_chips(x, y)],
               program)
    return [r[...] for r in refs]


def cast_into_slots(name, chip, shards, after=None):
    steps = 2
    n = len(shards)

    def body(chip_ref, *refs):
        for src, dst in zip(refs[:n], refs[-n - 1:-1]):
            dst[...] = src[...].astype(BF16)
        refs[-1][...] = jnp.zeros_like(refs[-1])

    token_spec = pl.BlockSpec((SUBLANES, LANES), lambda h, i, chip_ref: (0, 0))

    def spec_in(a, prefix):
        R, C = a.shape[-2:]
        return pl.BlockSpec((None,) * (len(prefix) + 1) + (R // steps, C), lambda h, i, chip_ref: prefix + (h, i, 0))

    def spec_out(a):
        R, C = a.shape[-2:]
        return pl.BlockSpec((None, None, R // steps, C), lambda h, i, chip_ref: (chip_ref[0], h, i, 0))

    outs = pl.pallas_call(
        body, name=name,
        grid_spec=pltpu.PrefetchScalarGridSpec(
            num_scalar_prefetch=1, grid=(2, steps),
            in_specs=[spec_in(a, p) for a, p in shards] + ([token_spec] if after is not None else []),
            out_specs=[spec_out(a) for a, _ in shards] + [token_spec]),
        out_shape=[jax.ShapeDtypeStruct((N_CHIP, 2) + a.shape[-2:], BF16) for a, _ in shards]
        + [jax.ShapeDtypeStruct((SUBLANES, LANES), F32)],
        compiler_params=_params("arbitrary", "arbitrary"),
    )(chip, *[a for a, _ in shards], *([after] if after is not None else []))
    return outs[:-1], outs[-1]


def reduce_pair(name, grads):
    n = len(grads)
    src = [_hbm_ref(g) for g in grads]
    dst = [_hbm_empty((N_CHIP,) + g.shape[2:], g.dtype) for g in grads]

    def program(x, y, c, send_sem, recv_sem):
        cps = [pltpu.make_async_remote_copy(
            src_ref=src[t].at[:, 1 - c], dst_ref=dst[t], send_sem=send_sem.at[t], recv_sem=recv_sem.at[t],
            device_id=(x, y, 1 - c), device_id_type=MESH) for t in range(n)]
        for cp in cps:
            cp.start()
        for cp in cps:
            cp.wait()

    _sequencer(name, ID_PAIR, n, lambda x, y, c: [(x, y, 1 - c)], program)
    return [r[...] for r in src], [r[...] for r in dst]


def pair_add(name, core, gs, gots):
    n = len(gs)

    def body(core_ref, *refs):
        for t in range(n):
            refs[2 * n + t][...] = (refs[2 * t][...] + refs[2 * t + 1][...]).astype(BF16)

    in_specs, out_specs, out_shape = [], [], []
    for g in gs:
        _, _, R, C = g.shape
        in_specs += [pl.BlockSpec((None, None, R, C), lambda q, core_ref: (q, core_ref[0], 0, 0)),
                     pl.BlockSpec((None, R, C), lambda q, core_ref: (q, 0, 0))]
        out_specs.append(pl.BlockSpec((None, R, C), lambda q, core_ref: (q, 0, 0)))
        out_shape.append(jax.ShapeDtypeStruct((N_CHIP, R, C), BF16))
    return pl.pallas_call(
        body, name=name,
        grid_spec=pltpu.PrefetchScalarGridSpec(num_scalar_prefetch=1, grid=(N_CHIP,), in_specs=in_specs,
                                               out_specs=out_specs),
        out_shape=out_shape, compiler_params=_params("parallel"),
    )(core, *[a for pair in zip(gs, gots) for a in pair])


def reduce_chips(name, sums):
    n = len(sums)
    src = [_hbm_ref(s) for s in sums]
    dst = [_hbm_empty((3,) + s.shape[1:], s.dtype) for s in sums]

    def program(x, y, c, send_sem, recv_sem):
        cps = []
        for t in range(n):
            for r, (cx, cy) in enumerate(_other_chips(x, y)):
                cps.append(pltpu.make_async_remote_copy(
                    src_ref=src[t].at[2 * cx + cy], dst_ref=dst[t].at[r],
                    send_sem=send_sem.at[3 * t + r], recv_sem=recv_sem.at[3 * t + r],
                    device_id=(cx, cy, c), device_id_type=MESH))
        for cp in cps:
            cp.start()
        for cp in cps:
            cp.wait()

    _sequencer(name, ID_CHIPS, 3 * n, lambda x, y, c: [(cx, cy, c) for cx, cy in _other_chips(x, y)], program)
    return [r[...] for r in src], [r[...] for r in dst]


def chip_add(name, place, items, after=None):
    n = len(items)

    def body(place_ref, *refs):
        for t in range(n):
            s_ref, got_ref, o_ref = refs[2 * t], refs[2 * t + 1], refs[len(refs) - n + t]
            o_ref[...] = ((s_ref[...].astype(F32) + got_ref[0].astype(F32)) + got_ref[1].astype(F32)) + got_ref[2].astype(F32)

    in_specs, args, out_specs, out_shape, aliases = [], [place], [], [], {}
    for s, got, k, n_slots, _ in items:
        _, R, C = s.shape
        in_specs += [pl.BlockSpec((None, R, C), lambda i, place_ref: (place_ref[0], 0, 0)),
                     pl.BlockSpec((3, R, C), lambda i, place_ref: (0, 0, 0))]
        args += [s, got]
        out_specs.append(pl.BlockSpec((None, None, R, C), lambda i, place_ref, k=k: (k, place_ref[1], 0, 0)))
        out_shape.append(jax.ShapeDtypeStruct((n_slots, 2, R, C), F32))
    for t, (*_, prev) in enumerate(items):
        if prev is not None:
            aliases[len(args)] = t
            in_specs.append(_ANY)
            args.append(prev)
    for piece in after or ():
        in_specs.append(pl.BlockSpec((SUBLANES, LANES), lambda i, place_ref: (0, 0)))
        args.append(piece)
    return pl.pallas_call(
        body, name=name,
        grid_spec=pltpu.PrefetchScalarGridSpec(num_scalar_prefetch=1, grid=(1,), in_specs=in_specs,
                                               out_specs=out_specs),
        out_shape=out_shape, input_output_aliases=aliases, compiler_params=_params("arbitrary"),
    )(*args)


def share_halves(name, stacks, slots):
    n = len(stacks)
    dst = [_hbm_ref(s) for s in stacks]

    def program(x, y, c, send_sem, recv_sem):
        cps = [pltpu.make_async_remote_copy(
            src_ref=dst[t].at[slots[t], c], dst_ref=dst[t].at[slots[t], c],
            send_sem=send_sem.at[t], recv_sem=recv_sem.at[t],
            device_id=(x, y, 1 - c), device_id_type=MESH) for t in range(n)]
        for cp in cps:
            cp.start()
        for cp in cps:
            cp.wait()

    _sequencer(name, ID_SHARE, n, lambda x, y, c: [(x, y, 1 - c)], program)
    return [r[...] for r in dst]


def gather_blocks(name, slotted):
    out = _hbm_ref(slotted)

    def program(x, y, c, send_sem, recv_sem):
        sibling = (x, y, 1 - c)
        chips = _other_chips(x, y)

        def copy(k, px, py, pc, to):
            block = out.at[4 * px + 2 * py + pc]
            return pltpu.make_async_remote_copy(src_ref=block, dst_ref=block, send_sem=send_sem.at[k],
                                                recv_sem=recv_sem.at[k], device_id=to, device_id_type=MESH)

        first = [copy(0, x, y, c, sibling)] + [copy(1 + j, x, y, c, (cx, cy, c)) for j, (cx, cy) in enumerate(chips)]
        for cp in first:
            cp.start()
        passed = [copy(4 + j, cx, cy, c, sibling) for j, (cx, cy) in enumerate(chips)]
        for j, (cx, cy) in enumerate(chips):
            copy(1 + j, cx, cy, c, (x, y, c)).wait_recv()
            passed[j].start()
        copy(0, x, y, 1 - c, (x, y, c)).wait_recv()
        for j, (cx, cy) in enumerate(chips):
            copy(4 + j, cx, cy, 1 - c, (x, y, c)).wait_recv()
        for cp in first + passed:
            cp.wait_send()

    _sequencer(name, ID_UKV, 7, lambda x, y, c: [(x, y, 1 - c)] + [(cx, cy, c) for cx, cy in _other_chips(x, y)],
               program)
    return out[...]


def place_block(name, dev, a):
    M, N = a.shape
    tr = min(M, 64)

    def body(dev_ref, a_ref, o_ref):
        o_ref[...] = a_ref[...]

    return pl.pallas_call(
        body, name=name,
        grid_spec=pltpu.PrefetchScalarGridSpec(
            num_scalar_prefetch=1, grid=(M // tr,),
            in_specs=[pl.BlockSpec((tr, N), lambda i, dev_ref: (i, 0))],
            out_specs=pl.BlockSpec((None, tr, N), lambda i, dev_ref: (dev_ref[0], i, 0))),
        out_shape=jax.ShapeDtypeStruct((N_DEV, M, N), a.dtype),
        compiler_params=_params("parallel"),
    )(dev, a)


def _swap_rope(a):
    return jnp.concatenate([a[..., ROPE // 2:], a[..., :ROPE // 2]], axis=-1)


def _rope_tables(S):
    inv = 1.0 / (ROPE_THETA ** (jnp.arange(0, ROPE, 2, dtype=F32) / ROPE))
    ang = jnp.arange(S, dtype=F32)[:, None] * inv[None, :]
    cos, sin = jnp.cos(ang), jnp.sin(ang)
    return (jnp.tile(jnp.concatenate([cos, cos], axis=1), (1, SLOTS)),
            jnp.tile(jnp.concatenate([-sin, sin], axis=1), (1, SLOTS)))


def _vec(norm_g, mod, i, k):
    rows = [norm_g[i, 2 * k], norm_g[i, 2 * k + 1], mod[i, 3 * k], mod[i, 3 * k + 1], mod[i, 3 * k + 2]]
    return jnp.concatenate([jnp.stack(rows), jnp.zeros((3, D), F32)], axis=0)


def _unpack_weights(full, w_uk, w_uv, q_norm, kv_norm):
    G = D // 4
    ffn_in = [[full[2 * i + k].reshape(N_CHIP, D, FSH) for k in range(2)] for i in range(2)]
    ffn_out = [[full[4 + 2 * i + k].reshape(2, FSH, D) for k in range(2)] for i in range(2)]
    pw = full[8].reshape(N_CHIP, 4, G // N_CHIP, G).transpose(1, 0, 2, 3).reshape(4, G, G)
    w_in = full[9].reshape(D, QL + KVL + ROPE)
    w_uq = full[10].reshape(QL, N_HEADS, NOPE + ROPE)
    wkr = w_in[:, QL + KVL:]
    wr = w_uq[:, :, NOPE:]
    eye2 = jnp.eye(2, dtype=BF16)
    uk_t = jnp.transpose(w_uk, (1, 2, 0)).reshape(N_PAIR, 2, NOPE, KVL)
    bduk = jnp.einsum("janc,ab->janbc", uk_t, eye2).reshape(N_PAIR, 2 * NOPE, 2 * KVL)
    uv = jnp.transpose(w_uv, (1, 0, 2)).reshape(N_PAIR, 2, KVL, VH)
    bduv = jnp.einsum("jacn,ab->jacbn", uv, eye2).reshape(N_PAIR, 2 * KVL, 2 * VH)
    mw = dict(wq=w_in[:, :QL], wkv=w_in[:, QL:QL + KVL], wkr4=jnp.tile(wkr, (1, SLOTS)),
              wkrs4=jnp.tile(_swap_rope(wkr), (1, SLOTS)), qn=q_norm, kvn=kv_norm,
              wn=w_uq[:, :, :NOPE].reshape(QL, NOPE_ALL), wr=wr.reshape(QL, ROPE_ALL),
              wrs=_swap_rope(wr).reshape(QL, ROPE_ALL), bduk=bduk)
    return ffn_in, ffn_out, pw, mw, bduv, full[11].reshape(D, D)


def _example_step(x, target, mod, norm_g, pvec, ffn_in, ffn_out, pw, mw, bduv, wo, reducer):
    S = x.shape[0]
    tabs = _rope_tables(S)
    vec = [[_vec(norm_g, mod, i, k) for k in range(3)] for i in range(2)]
    saved = {}
    for i in range(2):
        xin = x
        x, a, u, h = ffn_fwd(xin, vec[i][0], ffn_in[i][0], ffn_out[i][0], 0.5)
        saved[i, 0] = (xin, a, u, h)
        xin = x
        if i == 0:
            x, y, z = pool_fwd(xin, vec[i][1], pw, pvec)
            saved[i, 1] = (xin, y, z)
        else:
            h_m, cq_raw, ckv_raw, cqn, qnope, qcat, kcat, vcat = mla_pre(xin, vec[i][1], mw, tabs)
            olat, lse = attn_fwd(qcat, kcat, vcat)
            x, u_m, ocat = mla_post(olat, xin, vec[i][1], bduv, wo)
            saved[i, 1] = (xin, h_m, cq_raw, ckv_raw, cqn, qnope, qcat, kcat, olat, lse, u_m, ocat)
        xin = x
        x, a, u, h = ffn_fwd(xin, vec[i][2], ffn_in[i][1], ffn_out[i][1], 0.5)
        saved[i, 2] = (xin, a, u, h)
    loss, dx = loss_head(x, target)

    vg = {}
    G = D // 4

    def ffn_grads(i, k, dw_in, dw_out):
        return [(0, 2 * i + k, 4, dw_in.reshape(N_CHIP, 2, D // 2, FSH)),
                (1, 2 * i + k, 4, dw_out.reshape(N_CHIP, 2, DFF // 8, D))]

    piece = lambda t: t[:SUBLANES, :LANES]
    for i in (1, 0):
        xin, a, u, h = saved[i, 2]
        dx, du, act, da, vg[i, 2] = ffn_bwd(dx, xin, u, a, vec[i][2], ffn_in[i][1], ffn_out[i][1], 0.5)
        reducer.advance(after=(piece(dx),))
        reducer.add(f"f{i}1", ffn_grads(i, 1, *ffn_dw(h, da, act, du)))
        if i == 0:
            xin, y, z = saved[i, 1]
            dx, vg[i, 1], pgrad, g_pool = pool_bwd(dx, xin, y, z, vec[i][1], pw, pvec)
        else:
            xin, h_m, cq_raw, ckv_raw, cqn, qnope, qcat, kcat, olat, lse, u_m, ocat = saved[i, 1]
            du, docat, dolat, delta, vg_post = mla_post_bwd(dx, u_m, olat, vec[i][1], bduv, wo)
            reducer.advance()
            dq, dk, dv = attn_bwd(qcat, kcat, kcat.T, dolat, lse, delta)
            reducer.advance(after=(piece(dk),))
            dx, dlat, dql, dqcat, vg_pre, ngrad = mla_pre_bwd(
                dx, dq, dk, dv, xin, cq_raw, ckv_raw, vec[i][1], mw, tabs)
            vg[i, 1] = vg_post + vg_pre
            g = mla_dw(h_m, dlat, cqn, dqcat, dql, qnope, olat, docat, ocat, du)
            slots = lambda a: a.reshape(D, SLOTS, ROPE).sum(axis=1)
            g_kr = slots(g["in"][:, QL + KVL:QL + KVL + LANES]) + _swap_rope(slots(g["in"][:, QL + KVL + LANES:]))
            g_in = jnp.concatenate([g["in"][:, :QL + KVL], g_kr], axis=1)
            g_r = g["q"][:, NOPE_ALL:NOPE_ALL + ROPE_ALL].reshape(QL, N_HEADS, ROPE)
            g_rs = g["q"][:, NOPE_ALL + ROPE_ALL:].reshape(QL, N_HEADS, ROPE)
            g_uq = jnp.concatenate([g["q"][:, :NOPE_ALL].reshape(QL, N_HEADS, NOPE), g_r + _swap_rope(g_rs)], axis=-1)

            def heads(pairs):
                blk = pairs.reshape(N_PAIR, 2, KVL, 2, NOPE)
                per_head = jnp.stack([blk[:, 0, :, 0, :], blk[:, 1, :, 1, :]], axis=1).reshape(N_HEADS, KVL, NOPE)
                return jnp.transpose(per_head, (1, 0, 2)).reshape(KVL, N_HEADS * NOPE)

            reducer.add("mla", [(3, 0, 1, g_in.reshape(N_CHIP, 2, D // 8, QL + KVL + ROPE)),
                                (4, 0, 1, g_uq.reshape(N_CHIP, 2, QL // 8, N_HEADS * (NOPE + ROPE))),
                                (5, 0, 1, g["o"].reshape(N_CHIP, 2, D // 8, D))])
            reducer.add_replicated(jnp.concatenate([heads(g["uk"]), heads(g["uv"])], axis=0))
        xin, a, u, h = saved[i, 0]
        dx, du, act, da, vg[i, 0] = ffn_bwd(dx, xin, u, a, vec[i][0], ffn_in[i][0], ffn_out[i][0], 0.5)
        reducer.advance(after=(piece(dx),))
        grads = ffn_grads(i, 0, *ffn_dw(h, da, act, du))
        if i == 0:
            grads.append((2, 0, 1, g_pool.reshape(N_CHIP, 2, 2 * G // N_CHIP, G)))
        reducer.add(f"f{i}0", grads)
    return loss, dx, vg, pgrad, ngrad


class _GradReducer:
    def __init__(self, core, place, dev):
        self.core, self.place, self.dev = core, place, dev
        self.stacks = {}
        self.live = []
        self.replicated = None

    def add(self, tag, items):
        gen = self._run(tag, items)
        next(gen)
        self.live.append(gen)

    def add_replicated(self, block):
        self.replicated = gather_blocks("gather_ukv", place_block("place_ukv", self.dev, block))

    def advance(self, after=None):
        self.after = after
        live = []
        for gen in self.live:
            try:
                next(gen)
                live.append(gen)
            except StopIteration:
                pass
        self.live = live

    def finish(self):
        while self.live:
            self.advance()
        return self.stacks, self.replicated

    def _run(self, tag, items):
        grads, from_pair = reduce_pair(f"reduce_pair_{tag}", [g for *_, g in items])
        yield
        sums = pair_add(f"pair_add_{tag}", self.core, grads, from_pair)
        sums, from_chips = reduce_chips(f"reduce_chips_{tag}", sums)
        yield
        stacks = chip_add(f"chip_add_{tag}", self.place,
                          [(s, p, k, n_slots, self.stacks.get(o)) for (o, k, n_slots, _), s, p
                           in zip(items, sums, from_chips)], self.after)
        for (o, *_), stack in zip(items, stacks):
            self.stacks[o] = stack
        shared = share_halves(f"share_halves_{tag}", [self.stacks[o] for o, *_ in items], [k for _, k, *_ in items])
        for (o, *_), v in zip(items, shared):
            self.stacks[o] = v


SMALL_IN = 8 * 640
SMALL_GRAD = 8 * 4224
SMALL_W = 8 * 2944


def _pack(parts, total):
    flat = jnp.concatenate([p.reshape(-1) for p in parts])
    return jnp.concatenate([flat, jnp.zeros((total - flat.shape[0],), F32)]).reshape(8, total // 8)


def kernel(x, c, ada_w, ada_b, norm_g, ffn_w_in, ffn_w_out, pool_w, pool_b, pool_scale, mla_w_in, mla_q_norm, mla_kv_norm, mla_w_uq, mla_w_uk, mla_w_uv, mla_w_o, loss_target, m_ada_w, m_ada_b, m_norm_g, m_ffn_w_in, m_ffn_w_out, m_pool_w, m_pool_b, m_pool_scale, m_mla_w_in, m_mla_q_norm, m_mla_kv_norm, m_mla_w_uq, m_mla_w_uk, m_mla_w_uv, m_mla_w_o, v_ada_w, v_ada_b, v_norm_g, v_ffn_w_in, v_ffn_w_out, v_pool_w, v_pool_b, v_pool_scale, v_mla_w_in, v_mla_q_norm, v_mla_kv_norm, v_mla_w_uq, v_mla_w_uk, v_mla_w_uv, v_mla_w_o):
    ix, iy, ic = _place()
    chip = 2 * ix + iy
    dev = 2 * chip + ic
    core_arr = ic.astype(jnp.int32).reshape(1)
    chip_arr = chip.astype(jnp.int32).reshape(1)
    S = x.shape[1]
    G = D // 4
    NG = D // N_CHIP

    def chip_cols(a, width, axis):
        return lax.dynamic_slice_in_dim(a, chip * width, width, axis)

    got = gather_devices("gather_small_in", _pack([c, norm_g, pool_b, mla_q_norm], SMALL_IN)).reshape(N_DEV, SMALL_IN)
    c_all = got[:, :D]
    parts = got[0::2]
    o = D
    norm_g_full = parts[:, o:o + 12 * NG].reshape(N_CHIP, 2, 6, NG).transpose(1, 2, 0, 3).reshape(2, 6, D)
    o += 12 * NG
    pool_b_full = parts[:, o:o + G].reshape(N_CHIP, 4, G // N_CHIP).transpose(1, 0, 2).reshape(1, D)
    o += G
    q_norm_full = parts[:, o:o + QL // N_CHIP].reshape(1, QL)
    pvec = jnp.concatenate([pool_b_full, pool_scale, jnp.zeros((6, D), F32)], axis=0)

    c_pad = jnp.concatenate([c_all, jnp.zeros((8, D), F32)], axis=0)
    mod_loc = mod_fwd(c_pad, ada_w, chip_cols(ada_b, MOD_COLS, 1).reshape(2, 1, MOD_COLS))
    got = gather_devices("gather_mod", mod_loc[:, :8].transpose(1, 0, 2).reshape(8, 2 * MOD_COLS))
    mine = lax.dynamic_index_in_dim(got[0::2].reshape(N_CHIP, 8, 2, MOD_COLS), dev, axis=1, keepdims=False)
    mod = mine.transpose(1, 0, 2).reshape(2, 9, D)

    bf = lambda a: a.astype(BF16)
    w_in_halves = ffn_w_in.reshape(2, 2, 2, D // 2, FSH)
    w_out_halves = ffn_w_out.reshape(2, 2, 2, DFF // 8, D)
    shards = [(w_in_halves, (i, k)) for i in range(2) for k in range(2)]
    shards += [(w_out_halves, (i, k)) for i in range(2) for k in range(2)]
    shards += [(pool_w.reshape(2, 2 * G // N_CHIP, G), ()), (mla_w_in.reshape(2, D // 8, QL + KVL + ROPE), ()),
               (mla_w_uq.reshape(2, QL // 8, N_HEADS * (NOPE + ROPE)), ()), (mla_w_o.reshape(2, D // 8, D), ())]
    full = [None] * len(shards)
    stages = [(0, 4, 8), (1, 5), (2, 6), (9, 10, 11), (3, 7)]
    first, token = cast_into_slots("cast_first", chip_arr, [shards[t] for t in stages[0]])
    slotted = dict(zip(stages[0], first))
    rest = [t for members in stages[1:] for t in members]
    for stage, members in enumerate(stages):
        got_w = gather_weights(f"gather_weights_{stage}", stage, [slotted[t] for t in members])
        for t, a in zip(members, got_w):
            full[t] = a
        if stage == 0:
            slotted.update(zip(rest, cast_into_slots("cast_rest", chip_arr, [shards[t] for t in rest], token)[0]))
    ffn_in, ffn_out, pw, mw, bduv, wo = _unpack_weights(full, bf(mla_w_uk[0]), bf(mla_w_uv[0]), q_norm_full,
                                                        mla_kv_norm)

    place_arr = jnp.stack([chip, ic]).astype(jnp.int32)
    reducer = _GradReducer(core_arr, place_arr, dev.astype(jnp.int32).reshape(1))
    loss_mine, grad_x, vg, pgrad, ngrad = _example_step(
        x[0], loss_target[0], mod, norm_g_full, pvec, ffn_in, ffn_out, pw, mw, bduv, wo, reducer)

    dmod = jnp.stack([jnp.concatenate([vg[i, k][0:3] for k in range(3)]) for i in range(2)])
    dnorm = jnp.stack([jnp.concatenate([vg[i, k][3:5] for k in range(3)]) for i in range(2)])
    small = _pack([dmod, dnorm, pgrad[0], pgrad[1], ngrad[0], ngrad[1, :KVL], loss_mine], SMALL_GRAD)
    got = gather_devices("gather_small_grad", small)
    tot = sum_devices("sum_small_grad", got).reshape(-1)
    n_mod = 2 * 9 * D
    g_ada_b = tot[:n_mod].reshape(ada_b.shape)
    o = n_mod
    g_norm = chip_cols(tot[o:o + 12 * D].reshape(2, 6, D), NG, 2)
    o += 12 * D
    g_pool_b = chip_cols(tot[o:o + D].reshape(1, 4, G), G // N_CHIP, 2)
    o += D
    g_pool_scale = tot[o:o + D].reshape(pool_scale.shape)
    o += D
    g_q_norm = chip_cols(tot[o:o + QL].reshape(1, QL), QL // N_CHIP, 1)
    o += QL
    g_kv_norm = tot[o:o + KVL].reshape(mla_kv_norm.shape)
    loss = tot[o + KVL]
    dmod_all = chip_cols(got.reshape(N_DEV, -1)[:, :n_mod].reshape(N_DEV, 2, 9 * D), MOD_COLS, 2)
    dmod_pad = jnp.concatenate([dmod_all.transpose(1, 0, 2), jnp.zeros((2, 8, MOD_COLS), F32)], axis=1)

    g_ada_w, d_ada_w, nm_ada_w, nv_ada_w = adamw_ada(c_pad, dmod_pad, ada_w, m_ada_w, v_ada_w)
    small_names = ["ada_b", "norm_g", "pool_b", "pool_scale", "mla_q_norm", "mla_kv_norm"]
    small_w = [ada_b, norm_g, pool_b, pool_scale, mla_q_norm, mla_kv_norm]
    small_g = [g_ada_b, g_norm, g_pool_b, g_pool_scale, g_q_norm, g_kv_norm]
    small_m = [m_ada_b, m_norm_g, m_pool_b, m_pool_scale, m_mla_q_norm, m_mla_kv_norm]
    small_v = [v_ada_b, v_norm_g, v_pool_b, v_pool_scale, v_mla_q_norm, v_mla_kv_norm]
    packed = adamw("adamw_small", *[_pack(p, SMALL_W) for p in (small_w, small_g, small_m, small_v)])
    upd = {}
    o = 0
    for name, w in zip(small_names, small_w):
        upd[name] = [p.reshape(-1)[o:o + w.size].reshape(w.shape) for p in packed]
        o += w.size
    upd["ada_w"] = [d_ada_w, nm_ada_w, nv_ada_w]

    reducer.advance(after=(d_ada_w[0, :SUBLANES, :LANES],))
    ffn = [("ffn_w_in", 0, ffn_w_in, m_ffn_w_in, v_ffn_w_in), ("ffn_w_out", 1, ffn_w_out, m_ffn_w_out, v_ffn_w_out)]
    slots = lambda a: a.reshape((4,) + a.shape[2:])
    early = {name: adamw(f"adamw_{name}_early", slots(w), slots(reducer.stacks[o].reshape(w.shape)), slots(m),
                         slots(v), part=(1, 3), copy_grad=True) for name, o, w, m, v in ffn}
    g_mla_in = reducer.stacks[3].reshape(mla_w_in.shape)
    g_uq = reducer.stacks[4].reshape(mla_w_uq.shape)
    g_wo = reducer.stacks[5].reshape(mla_w_o.shape)
    for name, w, g, m, v in [("mla_w_in", mla_w_in, g_mla_in, m_mla_w_in, v_mla_w_in),
                             ("mla_w_uq", mla_w_uq, g_uq, m_mla_w_uq, v_mla_w_uq),
                             ("mla_w_o", mla_w_o, g_wo, m_mla_w_o, v_mla_w_o)]:
        upd[name] = adamw("adamw_" + name, w, g, m, v)

    reducer.advance(after=(early["ffn_w_in"][0][1, :SUBLANES, :LANES], early["ffn_w_out"][0][1, :SUBLANES, :LANES],
                           upd["mla_w_o"][0][0, :SUBLANES, :LANES], upd["mla_w_in"][0][0, :SUBLANES, :LANES]))
    ukv = sum_devices("sum_ukv", reducer.replicated)
    g_uk = ukv[:KVL].reshape(mla_w_uk.shape)
    g_uv = ukv[KVL:].reshape(mla_w_uv.shape)
    upd["mla_w_uk"] = adamw("adamw_mla_w_uk", mla_w_uk, g_uk, m_mla_w_uk, v_mla_w_uk)
    upd["mla_w_uv"] = adamw("adamw_mla_w_uv", mla_w_uv, g_uv, m_mla_w_uv, v_mla_w_uv)
    stacks, _ = reducer.finish()
    g_pool_w = stacks[2].reshape(pool_w.shape)
    g_ffn = {}
    for name, o, w, m, v in ffn:
        done = adamw(f"adamw_{name}_last", slots(w), slots(stacks[o].reshape(w.shape)), slots(m), slots(v),
                     part=(0, 1), prev=early[name], copy_grad=True)
        upd[name] = [p.reshape(w.shape) for p in done[:3]]
        g_ffn[name] = done[3].reshape(w.shape)
    g_ffn_in, g_ffn_out = g_ffn["ffn_w_in"], g_ffn["ffn_w_out"]
    upd["pool_w"] = adamw("adamw_pool_w", pool_w, g_pool_w, m_pool_w, v_pool_w)

    order = ["ada_w", "ada_b", "norm_g", "ffn_w_in", "ffn_w_out", "pool_w", "pool_b", "pool_scale", "mla_w_in",
             "mla_q_norm", "mla_kv_norm", "mla_w_uq", "mla_w_uk", "mla_w_uv", "mla_w_o"]
    grad = dict(ada_w=g_ada_w, ada_b=g_ada_b, norm_g=g_norm, ffn_w_in=g_ffn_in, ffn_w_out=g_ffn_out, pool_w=g_pool_w,
                pool_b=g_pool_b, pool_scale=g_pool_scale, mla_w_in=g_mla_in, mla_q_norm=g_q_norm,
                mla_kv_norm=g_kv_norm, mla_w_uq=g_uq, mla_w_uk=g_uk, mla_w_uv=g_uv, mla_w_o=g_wo)
    return (loss, grad_x[None], *[grad[n] for n in order], *[upd[n][0] for n in order],
            *[upd[n][1] for n in order], *[upd[n][2] for n in order])
```

```python
import functools

import jax
import jax.numpy as jnp
from jax import lax
from jax.experimental import pallas as pl
from jax.experimental.pallas import tpu as pltpu
from jax.experimental.pallas import tpu_sc as plsc

F32 = jnp.float32
BF16 = jnp.bfloat16

D = 1024
DFF = 2816
FSH = 1408
N_CHIP = 4
N_DEV = 8
N_HEADS = 16
NOPE = 64
ROPE = 32
VH = 64
QL = 256
KVL = 128
LANES = 128
SUBLANES = 8
QPAD = 256
EPS = 1e-6
ATTN_SCALE = (NOPE + ROPE) ** -0.5
ROPE_THETA = 10000.0
POOL_WINDOWS = (2, 4, 8, 16)
HALO = 8
ATTN_TQ = 1024
ATTN_KC = 512
ROW_TILE = 512
DW_TK = 2048

ADAM_LR, ADAM_B1, ADAM_B2, ADAM_EPS, ADAM_WD, ADAM_STEP = 0.001, 0.9, 0.999, 1e-08, 0.01, 10

VMEM_LIMIT = 60 * 1024 * 1024
MESH = pl.DeviceIdType.MESH

NT = (((1,), (1,)), ((), ()))
TN = (((0,), (0,)), ((), ()))


def _params(*sem):
    return pltpu.CompilerParams(dimension_semantics=sem, vmem_limit_bytes=VMEM_LIMIT)


def _dot(a, b, dims=None):
    if dims is None:
        return jnp.dot(a, b, preferred_element_type=F32)
    return lax.dot_general(a, b, dims, preferred_element_type=F32)


def _rms(x):
    r = lax.rsqrt(jnp.mean(x * x, axis=-1, keepdims=True) + EPS)
    return x * r, r


def _rms_bwd(xhat, r, dxhat):
    return r * (dxhat - xhat * jnp.mean(dxhat * xhat, axis=-1, keepdims=True))


def _as_row(col):
    return jnp.broadcast_to(col, (col.shape[0], LANES)).T[0:1, :]


def _prenorm(x, vec_ref):
    xhat, r = _rms(x)
    h = xhat * vec_ref[0:1, :] * (1.0 + vec_ref[3:4, :]) + vec_ref[2:3, :]
    return h, xhat, r


def _postnorm_bwd(dout, u, vec_ref, weight):
    uhat, r = _rms(u)
    gt = weight * (1.0 + vec_ref[4:5, :])
    dy = dout * gt
    dgate_rows = (weight * dout) * (uhat * vec_ref[1:2, :])
    dgpost_rows = dy * uhat
    du = _rms_bwd(uhat, r, dy * vec_ref[1:2, :])
    return du, dgate_rows, dgpost_rows


def _prenorm_bwd(dh, x, vec_ref, vg_ref):
    xhat, r = _rms(x)
    sc1 = 1.0 + vec_ref[3:4, :]
    g = vec_ref[0:1, :]
    vg_ref[0:1, :] += jnp.sum(dh, axis=0, keepdims=True)
    vg_ref[1:2, :] += jnp.sum(dh * (xhat * g), axis=0, keepdims=True)
    vg_ref[3:4, :] += jnp.sum(dh * sc1 * xhat, axis=0, keepdims=True)
    return _rms_bwd(xhat, r, dh * g * sc1)


def ffn_fwd(x, vec, w_in, w_out, weight, target=None):
    S = x.shape[0]
    tm = min(512, S)
    row = lambda i: (i, 0)
    half = lambda j: [_w3((8, D)), pl.BlockSpec((None, D, FSH), lambda i: (j, 0, 0)),
                      pl.BlockSpec((None, D, FSH), lambda i: (j + 2, 0, 0)),
                      pl.BlockSpec((None, FSH, D), lambda i: (j, 0, 0))]
    a_spec = lambda j: pl.BlockSpec((2, tm, FSH), lambda i: (0, i, j))
    a_shape = jax.ShapeDtypeStruct((2, S, DFF), BF16)

    def hidden(hb, wg_ref, wu_ref, wo_ref, a_ref):
        g = _dot(hb, wg_ref[...])
        up = _dot(hb, wu_ref[...])
        a_ref[0] = g.astype(BF16)
        a_ref[1] = up.astype(BF16)
        act = (g * jax.nn.sigmoid(g)) * up
        return _dot(act.astype(BF16), wo_ref[...])

    def first(x_ref, vec_ref, wg_ref, wu_ref, wo_ref, h_ref, a_ref, u_ref):
        h, _, _ = _prenorm(x_ref[...], vec_ref)
        hb = h.astype(BF16)
        h_ref[...] = hb
        u_ref[...] = hidden(hb, wg_ref, wu_ref, wo_ref, a_ref)

    h, a, u_half = pl.pallas_call(
        first, name="ffn_fwd_first", grid=(S // tm,),
        in_specs=[pl.BlockSpec((tm, D), row)] + half(0),
        out_specs=[pl.BlockSpec((tm, D), row), a_spec(0), pl.BlockSpec((tm, D), row)],
        out_shape=[jax.ShapeDtypeStruct((S, D), BF16), a_shape, jax.ShapeDtypeStruct((S, D), F32)],
        compiler_params=_params("parallel"),
    )(x, vec, w_in, w_in, w_out)

    def second(x_ref, h_ref, uh_ref, vec_ref, wg_ref, wu_ref, wo_ref, a_in, xo_ref, a_ref, u_ref):
        u = uh_ref[...] + hidden(h_ref[...], wg_ref, wu_ref, wo_ref, a_ref)
        u_ref[...] = u
        uhat, _ = _rms(u)
        xo_ref[...] = x_ref[...] + (weight * (1.0 + vec_ref[4:5, :])) * (uhat * vec_ref[1:2, :])

    def second_with_loss(x_ref, h_ref, uh_ref, vec_ref, wg_ref, wu_ref, wo_ref, t_ref, a_in,
                         dy_ref, a_ref, u_ref, loss_ref):
        @pl.when(pl.program_id(0) == 0)
        def _():
            loss_ref[...] = jnp.zeros_like(loss_ref)

        second(x_ref, h_ref, uh_ref, vec_ref, wg_ref, wu_ref, wo_ref, a_in, dy_ref, a_ref, u_ref)
        err = dy_ref[...] - t_ref[...]
        dy_ref[...] = err * (1.0 / D)
        loss_ref[...] += 0.5 * jnp.sum(jnp.mean(err * err, axis=-1, keepdims=True), axis=0, keepdims=True)

    rows3 = [pl.BlockSpec((tm, D), row)] * 3
    if target is None:
        xo, a, u = pl.pallas_call(
            second, name="ffn_fwd_second", grid=(S // tm,),
            in_specs=rows3 + half(1) + [_ANY],
            out_specs=[pl.BlockSpec((tm, D), row), a_spec(1), pl.BlockSpec((tm, D), row)],
            out_shape=[jax.ShapeDtypeStruct((S, D), F32), a_shape, jax.ShapeDtypeStruct((S, D), F32)],
            input_output_aliases={7: 1},
            compiler_params=_params("parallel"),
        )(x, h, u_half, vec, w_in, w_in, w_out, a)
        return xo, a, u, h
    dy, a, u, loss = pl.pallas_call(
        second_with_loss, name="ffn_fwd_last", grid=(S // tm,),
        in_specs=rows3 + half(1) + [pl.BlockSpec((tm, D), row), _ANY],
        out_specs=[pl.BlockSpec((tm, D), row), a_spec(1), pl.BlockSpec((tm, D), row), _w3((1, 1))],
        out_shape=[jax.ShapeDtypeStruct((S, D), F32), a_shape, jax.ShapeDtypeStruct((S, D), F32),
                   jax.ShapeDtypeStruct((1, 1), F32)],
        input_output_aliases={8: 1},
        compiler_params=_params("arbitrary"),
    )(x, h, u_half, vec, w_in, w_in, w_out, target, a)
    return dy, a, u, h, loss


def ffn_bwd(dout, x, u, a, vec, w_in, w_out, weight):
    S = x.shape[0]
    tm = min(512, S)
    row = lambda i: (i, 0)
    half = lambda j: [pl.BlockSpec((2, tm, FSH), lambda i: (0, i, j)), _w3((8, D)),
                      pl.BlockSpec((None, D, FSH), lambda i: (j, 0, 0)),
                      pl.BlockSpec((None, D, FSH), lambda i: (j + 2, 0, 0)),
                      pl.BlockSpec((None, FSH, D), lambda i: (j, 0, 0))]
    half_out = lambda j: [pl.BlockSpec((tm, FSH), lambda i: (i, j)), pl.BlockSpec((2, tm, FSH), lambda i: (0, i, j))]
    half_shape = [jax.ShapeDtypeStruct((S, DFF), BF16), jax.ShapeDtypeStruct((2, S, DFF), BF16)]

    def hidden_bwd(du, a_ref, wg_ref, wu_ref, wo_ref, act_ref, da_ref):
        dact = _dot(du, wo_ref[...], NT)
        g = a_ref[0].astype(F32)
        up = a_ref[1].astype(F32)
        s = jax.nn.sigmoid(g)
        silu = g * s
        act_ref[...] = (silu * up).astype(BF16)
        dg = (dact * up * (s * (1.0 + g * (1.0 - s)))).astype(BF16)
        dup = (dact * silu).astype(BF16)
        da_ref[0] = dg
        da_ref[1] = dup
        return _dot(dg, wg_ref[...], NT) + _dot(dup, wu_ref[...], NT)

    def first(do_ref, u_ref, a_ref, vec_ref, wg_ref, wu_ref, wo_ref, du_ref, dh_ref, act_ref, da_ref, vg_ref):
        @pl.when(pl.program_id(0) == 0)
        def _():
            vg_ref[...] = jnp.zeros_like(vg_ref)

        du, dgate_rows, dgpost_rows = _postnorm_bwd(do_ref[...], u_ref[...], vec_ref, weight)
        vg_ref[2:3, :] += jnp.sum(dgate_rows, axis=0, keepdims=True)
        vg_ref[4:5, :] += jnp.sum(dgpost_rows, axis=0, keepdims=True)
        du = du.astype(BF16)
        du_ref[...] = du
        dh_ref[...] = hidden_bwd(du, a_ref, wg_ref, wu_ref, wo_ref, act_ref, da_ref)

    du, dh, act, da, vg_post = pl.pallas_call(
        first, name="ffn_bwd_first", grid=(S // tm,),
        in_specs=[pl.BlockSpec((tm, D), row), pl.BlockSpec((tm, D), row)] + half(0),
        out_specs=[pl.BlockSpec((tm, D), row), pl.BlockSpec((tm, D), row)] + half_out(0) + [_w3((8, D))],
        out_shape=[jax.ShapeDtypeStruct((S, D), BF16), jax.ShapeDtypeStruct((S, D), F32)] + half_shape
        + [jax.ShapeDtypeStruct((8, D), F32)],
        compiler_params=_params("arbitrary"),
    )(dout, u, a, vec, w_in, w_in, w_out)

    def second(do_ref, x_ref, du_ref, dh_ref, a_ref, vec_ref, wg_ref, wu_ref, wo_ref, act_in, da_in,
               dx_ref, act_ref, da_ref, vg_ref):
        @pl.when(pl.program_id(0) == 0)
        def _():
            vg_ref[...] = jnp.zeros_like(vg_ref)

        dh = dh_ref[...] + hidden_bwd(du_ref[...], a_ref, wg_ref, wu_ref, wo_ref, act_ref, da_ref)
        dx_ref[...] = do_ref[...] + _prenorm_bwd(dh, x_ref[...], vec_ref, vg_ref)

    dx, act, da, vg_pre = pl.pallas_call(
        second, name="ffn_bwd_second", grid=(S // tm,),
        in_specs=[pl.BlockSpec((tm, D), row), pl.BlockSpec((tm, D), row), pl.BlockSpec((tm, D), row),
                  pl.BlockSpec((tm, D), row)] + half(1) + [_ANY, _ANY],
        out_specs=[pl.BlockSpec((tm, D), row)] + half_out(1) + [_w3((8, D))],
        out_shape=[jax.ShapeDtypeStruct((S, D), F32)] + half_shape + [jax.ShapeDtypeStruct((8, D), F32)],
        input_output_aliases={9: 1, 10: 2},
        compiler_params=_params("arbitrary"),
    )(dout, x, du, dh, a, vec, w_in, w_in, w_out, act, da)
    return dx, du, act, da, vg_post + vg_pre


def dw_matmul(name, a, b, a_spec, b_spec, out_shape, out_spec, grid):
    def body(a_ref, b_ref, o_ref):
        @pl.when(pl.program_id(len(grid) - 1) == 0)
        def _():
            o_ref[...] = jnp.zeros_like(o_ref)

        o_ref[...] += _dot(a_ref[...], b_ref[...], TN)

    return pl.pallas_call(
        body, name=name, grid=grid, in_specs=[a_spec, b_spec], out_specs=out_spec,
        out_shape=jax.ShapeDtypeStruct(out_shape, F32),
        compiler_params=_params(*(["parallel"] * (len(grid) - 1) + ["arbitrary"])),
    )(a, b)


def ffn_dw(h, da, act, du):
    S = h.shape[0]
    tk = min(DW_TK, S)
    dw_in = dw_matmul("ffn_dw_in", h, da,
                      pl.BlockSpec((tk, D), lambda n, k: (k, 0)),
                      pl.BlockSpec((None, tk, FSH), lambda n, k: (n // 2, k, n % 2)),
                      (N_CHIP, D, FSH), pl.BlockSpec((None, D, FSH), lambda n, k: (n, 0, 0)),
                      (N_CHIP, S // tk))
    dw_out = dw_matmul("ffn_dw_out", act, du,
                       pl.BlockSpec((tk, FSH), lambda n, k: (k, n)),
                       pl.BlockSpec((tk, D), lambda n, k: (k, 0)),
                       (DFF, D), pl.BlockSpec((FSH, D), lambda n, k: (n, 0)),
                       (2, S // tk))
    return dw_in, dw_out


def _halo_specs(tm, S):
    nb = tm // HALO
    last = S // HALO - 1
    return [pl.BlockSpec((HALO, D), lambda i: (jnp.maximum(i * nb - 1, 0), 0)),
            pl.BlockSpec((tm, D), lambda i: (i, 0)),
            pl.BlockSpec((HALO, D), lambda i: (jnp.minimum((i + 1) * nb, last), 0))]


def _shift_rows(v, k):
    return pltpu.roll(v, k % v.shape[0], 0)


def _window_sum(v, g, forward):
    acc = v + _shift_rows(v, 1 if forward else -1)
    for step in (1, 2, 4)[:g]:
        acc = _shift_rows(acc, step) + _shift_rows(acc, -step)
    return acc


def _pool_count(t, w, S):
    return jnp.maximum(jnp.minimum(t + w // 2, S) - jnp.maximum(t - w // 2, 0), 1).astype(F32)


def pool_fwd(x, vec, pw, pvec):
    S = x.shape[0]
    tm = min(ROW_TILE, S)
    G = D // 4

    def body(xp_ref, x_ref, xn_ref, vec_ref, pw_ref, pv_ref, xo_ref, y_ref, z_ref):
        i = pl.program_id(0)
        xa = jnp.concatenate([xp_ref[...], x_ref[...], xn_ref[...]], axis=0)
        t = i * tm - HALO + lax.broadcasted_iota(jnp.int32, (tm + 2 * HALO, 1), 0)
        h, _, _ = _prenorm(xa, vec_ref)
        h = jnp.where((t >= 0) & (t < S), h, 0.0)
        tmain = t[HALO:HALO + tm]
        for g in range(4):
            hg = h[:, g * G:(g + 1) * G]
            pooled = _window_sum(hg, g, True)[HALO:HALO + tm] / _pool_count(tmain, POOL_WINDOWS[g], S)
            z = (pooled - hg[HALO:HALO + tm]).astype(BF16)
            z_ref[:, g * G:(g + 1) * G] = z
            y_ref[:, g * G:(g + 1) * G] = _dot(z, pw_ref[g]) + pv_ref[0:1, g * G:(g + 1) * G]
        u = y_ref[...] * pv_ref[1:2, :]
        uhat, _ = _rms(u)
        xo_ref[...] = x_ref[...] + (1.0 + vec_ref[4:5, :]) * (uhat * vec_ref[1:2, :])

    row = lambda i: (i, 0)
    full = lambda i: (0, 0)
    return pl.pallas_call(
        body, name="pool_fwd", grid=(S // tm,),
        in_specs=_halo_specs(tm, S) + [pl.BlockSpec((8, D), full), pl.BlockSpec((4, G, G), lambda i: (0, 0, 0)),
                                       pl.BlockSpec((8, D), full)],
        out_specs=[pl.BlockSpec((tm, D), row)] * 3,
        out_shape=[jax.ShapeDtypeStruct((S, D), F32), jax.ShapeDtypeStruct((S, D), F32),
                   jax.ShapeDtypeStruct((S, D), BF16)],
        compiler_params=_params("parallel"),
    )(x, x, x, vec, pw, pvec)


def pool_bwd(dout, x, y, z, vec, pw, pvec):
    S = x.shape[0]
    tm = min(ROW_TILE, S)
    G = D // 4
    R = G // N_CHIP

    def body(dop_ref, do_ref, don_ref, yp_ref, y_ref, yn_ref, x_ref, z_ref, vec_ref, pw_ref, pv_ref,
             dx_ref, vg_ref, pg_ref, dw_ref, dh_ref):
        i = pl.program_id(0)

        @pl.when(i == 0)
        def _():
            vg_ref[...] = jnp.zeros_like(vg_ref)
            pg_ref[...] = jnp.zeros_like(pg_ref)
            dw_ref[...] = jnp.zeros_like(dw_ref)

        doa = jnp.concatenate([dop_ref[...], do_ref[...], don_ref[...]], axis=0)
        ya = jnp.concatenate([yp_ref[...], y_ref[...], yn_ref[...]], axis=0)
        t = i * tm - HALO + lax.broadcasted_iota(jnp.int32, (tm + 2 * HALO, 1), 0)
        inside = (t >= 0) & (t < S)
        main = (t >= i * tm) & (t < (i + 1) * tm)
        du, dgate_rows, dgpost_rows = _postnorm_bwd(doa, ya * pv_ref[1:2, :], vec_ref, 1.0)
        du = jnp.where(inside, du, 0.0)
        vg_ref[2:3, :] += jnp.sum(jnp.where(main, dgate_rows, 0.0), axis=0, keepdims=True)
        vg_ref[4:5, :] += jnp.sum(jnp.where(main, dgpost_rows, 0.0), axis=0, keepdims=True)
        dy = du * pv_ref[1:2, :]
        pg_ref[0:1, :] += jnp.sum(jnp.where(main, dy, 0.0), axis=0, keepdims=True)
        pg_ref[1:2, :] += jnp.sum(jnp.where(main, du * ya, 0.0), axis=0, keepdims=True)
        for g in range(4):
            dyg = dy[:, g * G:(g + 1) * G].astype(BF16)
            dz = _dot(dyg, pw_ref[g], NT)
            e = dz / _pool_count(t, POOL_WINDOWS[g], S)
            dh_ref[:, g * G:(g + 1) * G] = (_window_sum(e, g, False) - dz)[HALO:HALO + tm]
            dwg = _dot(z_ref[:, g * G:(g + 1) * G], dyg[HALO:HALO + tm], TN)
            for q in range(N_CHIP):
                dw_ref[q, g] += dwg[q * R:(q + 1) * R, :]
        dx_ref[...] = do_ref[...] + _prenorm_bwd(dh_ref[...], x_ref[...], vec_ref, vg_ref)

    row = lambda i: (i, 0)
    full = lambda i: (0, 0)
    halo = _halo_specs(tm, S)
    return pl.pallas_call(
        body, name="pool_bwd", grid=(S // tm,),
        in_specs=halo + halo + [pl.BlockSpec((tm, D), row), pl.BlockSpec((tm, D), row), pl.BlockSpec((8, D), full),
                                pl.BlockSpec((4, G, G), lambda i: (0, 0, 0)), pl.BlockSpec((8, D), full)],
        out_specs=[pl.BlockSpec((tm, D), row), pl.BlockSpec((8, D), full), pl.BlockSpec((8, D), full),
                   pl.BlockSpec((N_CHIP, 4, R, G), lambda i: (0, 0, 0, 0))],
        out_shape=[jax.ShapeDtypeStruct((S, D), F32), jax.ShapeDtypeStruct((8, D), F32),
                   jax.ShapeDtypeStruct((8, D), F32), jax.ShapeDtypeStruct((N_CHIP, 4, R, G), F32)],
        scratch_shapes=[pltpu.VMEM((tm, D), F32)],
        compiler_params=_params("arbitrary"),
    )(dout, dout, dout, y, y, y, x, z, vec, pw, pvec)


N_PAIR = N_HEADS // 2
SLOTS = LANES // ROPE
ROPE_ALL = N_HEADS * ROPE
NOPE_ALL = N_HEADS * NOPE
LAT_ALL = N_HEADS * KVL
DLAT = QL + KVL + 2 * LANES
DQ_ALL = NOPE_ALL + 2 * ROPE_ALL


def _w3(shape):
    return pl.BlockSpec(shape, lambda i: (0,) * len(shape))


def _slot_mask(hd, rows):
    lane = lax.broadcasted_iota(jnp.int32, (rows, LANES), 1)
    return (lane // ROPE) == (hd % SLOTS)


MLA_WEIGHTS = ("wq", "wkv", "wkr4", "wkrs4", "qn", "kvn", "wn", "wr", "wrs", "bduk")


def _mla_weight_specs():
    return [_w3((D, QL)), _w3((D, KVL)), _w3((D, LANES)), _w3((D, LANES)), _w3((1, QL)), _w3((1, KVL)),
            _w3((QL, NOPE_ALL)), _w3((QL, ROPE_ALL)), _w3((QL, ROPE_ALL)), _w3((N_PAIR, 2 * NOPE, 2 * KVL))]


def mla_pre(x, vec, mw, tabs):
    S = x.shape[0]
    tm = min(ROW_TILE, S)

    def body(x_ref, vec_ref, cos_ref, sin_ref, wq_ref, wkv_ref, wkr_ref, wkrs_ref, qn_ref, kvn_ref,
             wn_ref, wr_ref, wrs_ref, bduk_ref,
             h_ref, cq_ref, ckv_ref, cqn_ref, qnope_ref, qcat_ref, kcat_ref, vcat_ref):
        h, _, _ = _prenorm(x_ref[...], vec_ref)
        hb = h.astype(BF16)
        h_ref[...] = hb
        cq_raw = _dot(hb, wq_ref[...])
        ckv_raw = _dot(hb, wkv_ref[...])
        cq_ref[...] = cq_raw
        ckv_ref[...] = ckv_raw
        cos, sin = cos_ref[...], sin_ref[...]
        ckv = (_rms(ckv_raw)[0] * kvn_ref[...]).astype(BF16)
        kcat_ref[:, 0:KVL] = ckv
        kcat_ref[:, KVL:] = (_dot(hb, wkr_ref[...]) * cos + _dot(hb, wkrs_ref[...]) * sin).astype(BF16)
        vcat_ref[:, 0:KVL] = ckv
        ones = lax.broadcasted_iota(jnp.int32, (tm, QPAD - KVL), 1) == 0
        vcat_ref[:, KVL:] = jnp.where(ones, 1.0, 0.0).astype(BF16)
        cqb = (_rms(cq_raw)[0] * qn_ref[...]).astype(BF16)
        cqn_ref[...] = cqb
        qn = _dot(cqb, wn_ref[...]).astype(BF16)
        qnope_ref[...] = qn
        cos4, sin4 = jnp.tile(cos, (1, SLOTS)), jnp.tile(sin, (1, SLOTS))
        qr = ((_dot(cqb, wr_ref[...]) * cos4 + _dot(cqb, wrs_ref[...]) * sin4) * ATTN_SCALE).astype(BF16)
        for j in range(N_PAIR):
            ql = (_dot(qn[:, 2 * NOPE * j:2 * NOPE * (j + 1)], bduk_ref[j]) * ATTN_SCALE).astype(BF16)
            for hd in (2 * j, 2 * j + 1):
                qcat_ref[hd, :, 0:KVL] = ql[:, KVL * (hd - 2 * j):KVL * (hd - 2 * j + 1)]
                group = qr[:, LANES * (hd // SLOTS):LANES * (hd // SLOTS + 1)]
                qcat_ref[hd, :, KVL:] = jnp.where(_slot_mask(hd, tm), group, jnp.zeros_like(group))

    row = lambda i: (i, 0)
    hrow = lambda i: (0, i, 0)
    return pl.pallas_call(
        body, name="mla_pre", grid=(S // tm,),
        in_specs=[pl.BlockSpec((tm, D), row), _w3((8, D)), pl.BlockSpec((tm, LANES), row), pl.BlockSpec((tm, LANES), row)]
        + _mla_weight_specs(),
        out_specs=[pl.BlockSpec((tm, D), row), pl.BlockSpec((tm, QL), row), pl.BlockSpec((tm, KVL), row),
                   pl.BlockSpec((tm, QL), row), pl.BlockSpec((tm, NOPE_ALL), row),
                   pl.BlockSpec((N_HEADS, tm, QPAD), hrow), pl.BlockSpec((tm, QPAD), row),
                   pl.BlockSpec((tm, QPAD), row)],
        out_shape=[jax.ShapeDtypeStruct((S, D), BF16), jax.ShapeDtypeStruct((S, QL), F32),
                   jax.ShapeDtypeStruct((S, KVL), F32), jax.ShapeDtypeStruct((S, QL), BF16),
                   jax.ShapeDtypeStruct((S, NOPE_ALL), BF16), jax.ShapeDtypeStruct((N_HEADS, S, QPAD), BF16),
                   jax.ShapeDtypeStruct((S, QPAD), BF16), jax.ShapeDtypeStruct((S, QPAD), BF16)],
        compiler_params=_params("parallel"),
    )(x, vec, tabs[0], tabs[1], *[mw[k] for k in MLA_WEIGHTS])


def attn_fwd(qcat, kcat, vcat):
    S = kcat.shape[0]
    tq = min(ATTN_TQ, S)
    kc = min(ATTN_KC, S)

    def body(q_ref, k_ref, v_ref, o_ref, lse_ref):
        q = q_ref[...]
        m = jnp.full((tq, 1), -jnp.inf, F32)
        ov = jnp.zeros((tq, QPAD), F32)
        for c in range(S // kc):
            s = _dot(q, k_ref[c * kc:(c + 1) * kc, :], NT)
            m_new = jnp.maximum(m, jnp.max(s, axis=-1, keepdims=True))
            p = jnp.exp(s - m_new).astype(BF16)
            ov = ov * jnp.exp(m - m_new) + _dot(p, v_ref[c * kc:(c + 1) * kc, :])
            m = m_new
        l = ov[:, KVL:KVL + 1]
        o_ref[...] = (ov[:, 0:KVL] * (1.0 / l)).astype(BF16)
        lse_ref[...] = _as_row(m + jnp.log(l))

    return pl.pallas_call(
        body, name="attn_fwd", grid=(N_HEADS, S // tq),
        in_specs=[pl.BlockSpec((None, tq, QPAD), lambda h, i: (h, i, 0)),
                  pl.BlockSpec((S, QPAD), lambda h, i: (0, 0)),
                  pl.BlockSpec((S, QPAD), lambda h, i: (0, 0))],
        out_specs=[pl.BlockSpec((tq, KVL), lambda h, i: (i, h)),
                   pl.BlockSpec((None, 1, tq), lambda h, i: (h, 0, i))],
        out_shape=[jax.ShapeDtypeStruct((S, LAT_ALL), BF16), jax.ShapeDtypeStruct((N_HEADS, 1, S), F32)],
        compiler_params=_params("parallel", "parallel"),
    )(qcat, kcat, vcat)


def mla_post(olat, x, vec, bduv, wo):
    S = x.shape[0]
    tm = min(ROW_TILE, S)

    def body(o_ref, x_ref, vec_ref, bduv_ref, wo_ref, xo_ref, u_ref, ocat_ref):
        for j in range(N_PAIR):
            oc = _dot(o_ref[:, 2 * KVL * j:2 * KVL * (j + 1)], bduv_ref[j])
            ocat_ref[:, 2 * VH * j:2 * VH * (j + 1)] = oc.astype(BF16)
        u = _dot(ocat_ref[...], wo_ref[...])
        u_ref[...] = u
        uhat, _ = _rms(u)
        xo_ref[...] = x_ref[...] + (1.0 + vec_ref[4:5, :]) * (uhat * vec_ref[1:2, :])

    row = lambda i: (i, 0)
    return pl.pallas_call(
        body, name="mla_post", grid=(S // tm,),
        in_specs=[pl.BlockSpec((tm, LAT_ALL), row), pl.BlockSpec((tm, D), row), _w3((8, D)),
                  _w3((N_PAIR, 2 * KVL, 2 * VH)), _w3((D, D))],
        out_specs=[pl.BlockSpec((tm, D), row), pl.BlockSpec((tm, D), row), pl.BlockSpec((tm, D), row)],
        out_shape=[jax.ShapeDtypeStruct((S, D), F32), jax.ShapeDtypeStruct((S, D), F32),
                   jax.ShapeDtypeStruct((S, D), BF16)],
        compiler_params=_params("parallel"),
    )(olat, x, vec, bduv, wo)


def mla_post_bwd(dout, u, olat, vec, bduv, wo):
    S = u.shape[0]
    tm = min(ROW_TILE, S)

    def body(do_ref, u_ref, o_ref, vec_ref, bduv_ref, wo_ref, du_ref, docat_ref, dolat_ref, delta_ref, vg_ref):
        @pl.when(pl.program_id(0) == 0)
        def _():
            vg_ref[...] = jnp.zeros_like(vg_ref)

        du, dgate_rows, dgpost_rows = _postnorm_bwd(do_ref[...], u_ref[...], vec_ref, 1.0)
        vg_ref[2:3, :] += jnp.sum(dgate_rows, axis=0, keepdims=True)
        vg_ref[4:5, :] += jnp.sum(dgpost_rows, axis=0, keepdims=True)
        dub = du.astype(BF16)
        du_ref[...] = dub
        docat_ref[...] = _dot(dub, wo_ref[...], NT).astype(BF16)
        for j in range(N_PAIR):
            dol = _dot(docat_ref[:, 2 * VH * j:2 * VH * (j + 1)], bduv_ref[j], NT).astype(BF16)
            dolat_ref[:, 2 * KVL * j:2 * KVL * (j + 1)] = dol
            prod = dol.astype(F32) * o_ref[:, 2 * KVL * j:2 * KVL * (j + 1)].astype(F32)
            delta_ref[2 * j] = _as_row(jnp.sum(prod[:, 0:KVL], axis=-1, keepdims=True))
            delta_ref[2 * j + 1] = _as_row(jnp.sum(prod[:, KVL:], axis=-1, keepdims=True))

    row = lambda i: (i, 0)
    hrow = lambda i: (0, i, 0)
    return pl.pallas_call(
        body, name="mla_post_bwd", grid=(S // tm,),
        in_specs=[pl.BlockSpec((tm, D), row), pl.BlockSpec((tm, D), row), pl.BlockSpec((tm, LAT_ALL), row),
                  _w3((8, D)), _w3((N_PAIR, 2 * KVL, 2 * VH)), _w3((D, D))],
        out_specs=[pl.BlockSpec((tm, D), row), pl.BlockSpec((tm, D), row),
                   pl.BlockSpec((tm, LAT_ALL), row), pl.BlockSpec((N_HEADS, 1, tm), lambda i: (0, 0, i)), _w3((8, D))],
        out_shape=[jax.ShapeDtypeStruct((S, D), BF16), jax.ShapeDtypeStruct((S, D), BF16),
                   jax.ShapeDtypeStruct((S, LAT_ALL), BF16), jax.ShapeDtypeStruct((N_HEADS, 1, S), F32),
                   jax.ShapeDtypeStruct((8, D), F32)],
        compiler_params=_params("arbitrary"),
    )(dout, u, olat, vec, bduv, wo)


def attn_bwd(qcat, kcat, kcat_t, dolat, lse_row, delta_row):
    S = kcat.shape[0]
    tq = min(ATTN_TQ, S)
    kc = min(ATTN_KC, S)

    def body(q_ref, k_ref, kt_ref, do_ref, lse_ref, dl_ref, dq_ref, dk_ref, dv_ref):
        @pl.when((pl.program_id(0) == 0) & (pl.program_id(1) == 0))
        def _():
            dk_ref[...] = jnp.zeros_like(dk_ref)
            dv_ref[...] = jnp.zeros_like(dv_ref)

        q, do = q_ref[...], do_ref[...]
        lse, dl = lse_ref[...], dl_ref[...]
        dqt = jnp.zeros((QPAD, tq), F32)
        for c in range(S // kc):
            rows = slice(c * kc, (c + 1) * kc)
            st = _dot(k_ref[rows, :], q, NT)
            pt = jnp.exp(st - lse)
            dpt = _dot(k_ref[rows, 0:KVL], do, NT)
            dst = (pt * (dpt - dl)).astype(BF16)
            dv_ref[rows, :] += _dot(pt.astype(BF16), do)
            dk_ref[rows, :] += _dot(dst, q)
            dqt = dqt + _dot(kt_ref[:, rows], dst)
        dq_ref[...] = (dqt.T * ATTN_SCALE).astype(BF16)

    return pl.pallas_call(
        body, name="attn_bwd", grid=(N_HEADS, S // tq),
        in_specs=[pl.BlockSpec((None, tq, QPAD), lambda h, i: (h, i, 0)),
                  pl.BlockSpec((S, QPAD), lambda h, i: (0, 0)),
                  pl.BlockSpec((QPAD, S), lambda h, i: (0, 0)),
                  pl.BlockSpec((tq, KVL), lambda h, i: (i, h)),
                  pl.BlockSpec((None, 1, tq), lambda h, i: (h, 0, i)),
                  pl.BlockSpec((None, 1, tq), lambda h, i: (h, 0, i))],
        out_specs=[pl.BlockSpec((None, tq, QPAD), lambda h, i: (h, i, 0)),
                   pl.BlockSpec((S, QPAD), lambda h, i: (0, 0)),
                   pl.BlockSpec((S, KVL), lambda h, i: (0, 0))],
        out_shape=[jax.ShapeDtypeStruct((N_HEADS, S, QPAD), BF16), jax.ShapeDtypeStruct((S, QPAD), F32),
                   jax.ShapeDtypeStruct((S, KVL), F32)],
        compiler_params=_params("arbitrary", "arbitrary"),
    )(qcat, kcat, kcat_t, dolat, lse_row, delta_row)


def mla_pre_bwd(dout, dq, dk, dv, x, cq_raw, ckv_raw, vec, mw, tabs):
    S = x.shape[0]
    tm = min(ROW_TILE, S)

    def body(do_ref, dq_ref, dk_ref, dv_ref, x_ref, cq_ref, ckv_ref, vec_ref, cos_ref, sin_ref,
             wq_ref, wkv_ref, wkr_ref, wkrs_ref, qn_ref, kvn_ref, wn_ref, wr_ref, wrs_ref, bduk_ref,
             dx_ref, dlat_ref, dql_ref, dqcat_ref, vg_ref, ng_ref):
        @pl.when(pl.program_id(0) == 0)
        def _():
            vg_ref[...] = jnp.zeros_like(vg_ref)
            ng_ref[...] = jnp.zeros_like(ng_ref)

        cos, sin = cos_ref[...], sin_ref[...]
        for j in range(N_PAIR):
            dql = jnp.concatenate([dq_ref[2 * j, :, 0:KVL], dq_ref[2 * j + 1, :, 0:KVL]], axis=1)
            dql_ref[:, 2 * KVL * j:2 * KVL * (j + 1)] = dql
            dqcat_ref[:, 2 * NOPE * j:2 * NOPE * (j + 1)] = _dot(dql, bduk_ref[j], NT).astype(BF16)
        groups = []
        for grp in range(N_HEADS // SLOTS):
            acc = jnp.zeros((tm, LANES), F32)
            for hd in range(SLOTS * grp, SLOTS * (grp + 1)):
                acc = acc + jnp.where(_slot_mask(hd, tm), dq_ref[hd, :, KVL:].astype(F32), 0.0)
            groups.append(acc)
        dqr = jnp.concatenate(groups, axis=1)
        qa = (dqr * jnp.tile(cos, (1, SLOTS))).astype(BF16)
        qb = (dqr * jnp.tile(sin, (1, SLOTS))).astype(BF16)
        dqcat_ref[:, NOPE_ALL:NOPE_ALL + ROPE_ALL] = qa
        dqcat_ref[:, NOPE_ALL + ROPE_ALL:] = qb
        dcq = _dot(dqcat_ref[:, 0:NOPE_ALL], wn_ref[...], NT) + _dot(qa, wr_ref[...], NT) + _dot(qb, wrs_ref[...], NT)
        cqh, rq = _rms(cq_ref[...])
        ng_ref[0:1, :] += jnp.sum(dcq * cqh, axis=0, keepdims=True)
        dcq_raw = _rms_bwd(cqh, rq, dcq * qn_ref[...]).astype(BF16)
        dckv = dk_ref[:, 0:KVL] + dv_ref[...]
        ckvh, rk = _rms(ckv_ref[...])
        ng_ref[1:2, 0:KVL] += jnp.sum(dckv * ckvh, axis=0, keepdims=True)
        dckv_raw = _rms_bwd(ckvh, rk, dckv * kvn_ref[...]).astype(BF16)
        dkr = dk_ref[:, KVL:]
        ka = (dkr * cos).astype(BF16)
        kb = (dkr * sin).astype(BF16)
        dlat_ref[:, 0:QL] = dcq_raw
        dlat_ref[:, QL:QL + KVL] = dckv_raw
        dlat_ref[:, QL + KVL:QL + KVL + LANES] = ka
        dlat_ref[:, QL + KVL + LANES:] = kb
        dh = (_dot(dcq_raw, wq_ref[...], NT) + _dot(dckv_raw, wkv_ref[...], NT)
              + _dot(ka, wkr_ref[...], NT) + _dot(kb, wkrs_ref[...], NT))
        dx_ref[...] = do_ref[...] + _prenorm_bwd(dh, x_ref[...], vec_ref, vg_ref)

    row = lambda i: (i, 0)
    hrow = lambda i: (0, i, 0)
    return pl.pallas_call(
        body, name="mla_pre_bwd", grid=(S // tm,),
        in_specs=[pl.BlockSpec((tm, D), row), pl.BlockSpec((N_HEADS, tm, QPAD), hrow), pl.BlockSpec((tm, QPAD), row),
                  pl.BlockSpec((tm, KVL), row), pl.BlockSpec((tm, D), row), pl.BlockSpec((tm, QL), row),
                  pl.BlockSpec((tm, KVL), row), _w3((8, D)), pl.BlockSpec((tm, LANES), row), pl.BlockSpec((tm, LANES), row)]
        + _mla_weight_specs(),
        out_specs=[pl.BlockSpec((tm, D), row), pl.BlockSpec((tm, DLAT), row), pl.BlockSpec((tm, LAT_ALL), row),
                   pl.BlockSpec((tm, DQ_ALL), row), _w3((8, D)), _w3((8, QL))],
        out_shape=[jax.ShapeDtypeStruct((S, D), F32), jax.ShapeDtypeStruct((S, DLAT), BF16),
                   jax.ShapeDtypeStruct((S, LAT_ALL), BF16), jax.ShapeDtypeStruct((S, DQ_ALL), BF16),
                   jax.ShapeDtypeStruct((8, D), F32), jax.ShapeDtypeStruct((8, QL), F32)],
        compiler_params=_params("arbitrary"),
    )(dout, dq, dk, dv, x, cq_raw, ckv_raw, vec, tabs[0], tabs[1], *[mw[k] for k in MLA_WEIGHTS])


def mla_dw(h, dlat, cqn, dqcat, dql, qnope, olat, docat, ocat, du):
    S = h.shape[0]
    tk = min(DW_TK, S)
    nk = S // tk
    flat = lambda w: pl.BlockSpec((tk, w), lambda k: (k, 0))
    cols = lambda w: pl.BlockSpec((tk, w), lambda n, k: (k, n))
    pair_o = pl.BlockSpec((None, 2 * KVL, 2 * NOPE), lambda n, k: (n, 0, 0))
    g = {}
    g["in"] = dw_matmul("mla_dw_in", h, dlat, flat(D), flat(DLAT), (D, DLAT),
                        pl.BlockSpec((D, DLAT), lambda k: (0, 0)), (nk,))
    g["q"] = dw_matmul("mla_dw_q", cqn, dqcat, flat(QL), flat(DQ_ALL), (QL, DQ_ALL),
                       pl.BlockSpec((QL, DQ_ALL), lambda k: (0, 0)), (nk,))
    g["uk"] = dw_matmul("mla_dw_uk", dql, qnope, cols(2 * KVL), cols(2 * NOPE), (N_PAIR, 2 * KVL, 2 * NOPE), pair_o,
                        (N_PAIR, nk))
    g["uv"] = dw_matmul("mla_dw_uv", olat, docat, cols(2 * KVL), cols(2 * VH), (N_PAIR, 2 * KVL, 2 * VH), pair_o,
                        (N_PAIR, nk))
    g["o"] = dw_matmul("mla_dw_o", ocat, du, cols(256), pl.BlockSpec((tk, D), lambda n, k: (k, 0)), (D, D),
                       pl.BlockSpec((256, D), lambda n, k: (n, 0)), (D // 256, nk))
    return g


MOD_COLS = 9 * D // N_CHIP


def mod_fwd(c_pad, ada_w, ada_b_loc):
    tn = MOD_COLS // 3

    def body(c_ref, w_ref, b_ref, o_ref):
        c = c_ref[...]
        sc = (c * jax.nn.sigmoid(c)).astype(BF16)
        o_ref[...] = _dot(sc, w_ref[...].astype(BF16)) + b_ref[...]

    return pl.pallas_call(
        body, name="mod_fwd", grid=(2, 3),
        in_specs=[pl.BlockSpec((16, D), lambda i, n: (0, 0)), pl.BlockSpec((None, D, tn), lambda i, n: (i, 0, n)),
                  pl.BlockSpec((None, 1, tn), lambda i, n: (i, 0, n))],
        out_specs=pl.BlockSpec((None, 16, tn), lambda i, n: (i, 0, n)),
        out_shape=jax.ShapeDtypeStruct((2, 16, MOD_COLS), F32),
        compiler_params=_params("parallel", "parallel"),
    )(c_pad, ada_w, ada_b_loc)


def _adamw_math(w, g, m, v):
    m = ADAM_B1 * m + (1.0 - ADAM_B1) * g
    v = ADAM_B2 * v + (1.0 - ADAM_B2) * (g * g)
    m_hat = m / (1.0 - ADAM_B1 ** ADAM_STEP)
    v_hat = v / (1.0 - ADAM_B2 ** ADAM_STEP)
    delta = -ADAM_LR * (m_hat / (jnp.sqrt(v_hat) + ADAM_EPS) + ADAM_WD * w)
    return delta, m, v


def adamw_whole(name, items):
    n = len(items)

    def body(*refs):
        for t in range(n):
            w_ref, g_ref, m_ref, v_ref = refs[4 * t:4 * t + 4]
            d_ref, mo_ref, vo_ref = refs[4 * n + 3 * t:4 * n + 3 * t + 3]
            d_ref[...], mo_ref[...], vo_ref[...] = _adamw_math(w_ref[...], g_ref[...], m_ref[...], v_ref[...])

    whole = lambda shape: pl.BlockSpec(shape, lambda i: (0,) * len(shape))
    outs = pl.pallas_call(
        body, name=name, grid=(1,),
        in_specs=[whole(a.shape) for item in items for a in item],
        out_specs=[whole(item[0].shape) for item in items for _ in range(3)],
        out_shape=[jax.ShapeDtypeStruct(item[0].shape, F32) for item in items for _ in range(3)],
        compiler_params=_params("arbitrary"),
    )(*[a for item in items for a in item])
    return [list(outs[3 * t:3 * t + 3]) for t in range(n)]


def adamw(name, w, g, m, v, part=None, prev=None, copy_grad=False):
    shape = w.shape
    if part is None and w.size * 4 <= (1 << 20):
        return adamw_whole(name, [(w, g, m, v)])[0]
    cols = shape[-1]
    rows = w.size // cols
    per_entry = rows // shape[0] if part is not None else rows
    tr = per_entry
    budget_rows = (2 << 20) // (cols * 4)
    for cand in range(min(per_entry, budget_rows) // 8 * 8, 0, -8):
        if per_entry % cand == 0:
            tr = cand
            break
    first, count = part if part is not None else (0, 1)
    tiles = per_entry // tr

    n_out = 4 if copy_grad else 3

    def body(w_ref, g_ref, m_ref, v_ref, *rest):
        outs = rest[-n_out:]
        outs[0][...], outs[1][...], outs[2][...] = _adamw_math(w_ref[...], g_ref[...], m_ref[...], v_ref[...])
        if copy_grad:
            outs[3][...] = g_ref[...]

    spec = pl.BlockSpec((tr, cols), lambda i: (i + first * tiles, 0))
    operands = [a.reshape(rows, cols) for a in (w, g, m, v)]
    aliases = {}
    if prev is not None:
        operands += [p.reshape(rows, cols) for p in prev]
        aliases = {4 + t: t for t in range(n_out)}
    outs = pl.pallas_call(
        body, name=name, grid=(count * tiles,), in_specs=[spec] * 4 + [_ANY] * (len(operands) - 4),
        out_specs=[spec] * n_out, out_shape=[jax.ShapeDtypeStruct((rows, cols), F32)] * n_out,
        input_output_aliases=aliases, compiler_params=_params("parallel"),
    )(*operands)
    return [o.reshape(shape) for o in outs]


def adamw_ada(c_pad, dmod, w, m, v):
    tr = 256

    def body(c_ref, dm_ref, w_ref, m_ref, v_ref, g_ref, d_ref, mo_ref, vo_ref):
        c = c_ref[...]
        sc = (c * jax.nn.sigmoid(c)).astype(BF16)
        g = _dot(sc, dm_ref[...].astype(BF16), TN)
        g_ref[...] = g
        d_ref[...], mo_ref[...], vo_ref[...] = _adamw_math(w_ref[...], g, m_ref[...], v_ref[...])

    wspec = pl.BlockSpec((None, tr, MOD_COLS), lambda i, r: (i, r, 0))
    return pl.pallas_call(
        body, name="adamw_ada", grid=(2, D // tr),
        in_specs=[pl.BlockSpec((16, tr), lambda i, r: (0, r)),
                  pl.BlockSpec((None, 16, MOD_COLS), lambda i, r: (i, 0, 0)), wspec, wspec, wspec],
        out_specs=[wspec] * 4,
        out_shape=[jax.ShapeDtypeStruct((2, D, MOD_COLS), F32)] * 4,
        compiler_params=_params("parallel", "parallel"),
    )(c_pad, dmod, w, m, v)


def sum_devices(name, a):
    _, R, C = a.shape
    tr = R
    for cand in (64, 32, 16, 8):
        if R % cand == 0:
            tr = cand
            break

    def body(a_ref, o_ref):
        acc = a_ref[0]
        for dev in range(1, N_DEV):
            acc = acc + a_ref[dev]
        o_ref[...] = acc

    return pl.pallas_call(
        body, name=name, grid=(R // tr,),
        in_specs=[pl.BlockSpec((N_DEV, tr, C), lambda i: (0, i, 0))],
        out_specs=pl.BlockSpec((tr, C), lambda i: (i, 0)),
        out_shape=jax.ShapeDtypeStruct((R, C), F32),
        compiler_params=_params("parallel"),
    )(a)


def _place():
    return lax.axis_index("x"), lax.axis_index("y"), lax.axis_index("c")


def _other_chips(x, y):
    return [(1 - x, y), (x, 1 - y), (1 - x, 1 - y)]


def gather_devices(name, a):
    m_per, n = a.shape

    def body(x_ref, out_ref, send_sems, recv_sems, local_sem):
        x, y, c = _place()
        me, sibling = (x, y, c), (x, y, 1 - c)
        chips = _other_chips(x, y)

        def rows(px, py, pc):
            return out_ref.at[pl.ds((4 * px + 2 * py + pc) * m_per, m_per), :]

        def copy(k, block, to, src=None):
            return pltpu.make_async_remote_copy(
                src_ref=rows(*block) if src is None else src, dst_ref=rows(*block),
                send_sem=send_sems.at[k], recv_sem=recv_sems.at[k], device_id=to, device_id_type=MESH)

        mine = pltpu.make_async_copy(x_ref, rows(*me), local_sem)
        mine.start()
        first = [copy(0, me, sibling, src=x_ref)]
        first += [copy(1 + j, me, (*chip, c), src=x_ref) for j, chip in enumerate(chips)]
        for cp in first:
            cp.start()
        passed = [copy(4 + j, (*chip, c), sibling) for j, chip in enumerate(chips)]
        for j, chip in enumerate(chips):
            copy(1 + j, (*chip, c), me).wait_recv()
            passed[j].start()
        copy(0, sibling, me).wait_recv()
        for j, chip in enumerate(chips):
            copy(4 + j, (*chip, 1 - c), me).wait_recv()
        for cp in first + passed:
            cp.wait_send()
        mine.wait()

    out = pl.pallas_call(
        body, name=name,
        out_shape=jax.ShapeDtypeStruct((N_DEV * m_per, n), a.dtype),
        in_specs=[pl.BlockSpec(memory_space=pltpu.VMEM)],
        out_specs=pl.BlockSpec(memory_space=pltpu.VMEM),
        scratch_shapes=[pltpu.SemaphoreType.DMA((7,)), pltpu.SemaphoreType.DMA((7,)), pltpu.SemaphoreType.DMA],
        compiler_params=pltpu.CompilerParams(vmem_limit_bytes=VMEM_LIMIT),
    )(a)
    return out.reshape(N_DEV, m_per, n)


_ANY = pl.BlockSpec(memory_space=pl.ANY)


def _hbm_ref(a):
    return jax.new_ref(a, memory_space=pltpu.MemorySpace.HBM)


def _hbm_empty(shape, dtype):
    return jax.empty_ref(jax.ShapeDtypeStruct(shape, dtype), memory_space=pltpu.MemorySpace.HBM)


ID_PAIR, ID_CHIPS, ID_SHARE, ID_UKV = 8, 9, 10, 11


def _sequencer(name, collective_id, n_sem, peers_of, program):
    sems = pltpu.SemaphoreType.DMA((n_sem,))

    @pl.kernel(mesh=plsc.ScalarSubcoreMesh(axis_name="seq", num_cores=1), name=name, scratch_types=[sems, sems],
               compiler_params=pltpu.CompilerParams(collective_id=collective_id))
    def launch(send_sem, recv_sem):
        x, y, c = _place()
        peers = peers_of(x, y, c)
        barrier = pltpu.get_barrier_semaphore()
        for peer in peers:
            pl.semaphore_signal(barrier, inc=1, device_id=peer, device_id_type=MESH)
        pl.semaphore_wait(barrier, len(peers))
        program(x, y, c, send_sem, recv_sem)

    launch()


def gather_weights(name, stage, arrays):
    n = len(arrays)
    refs = [_hbm_ref(a) for a in arrays]

    def program(x, y, c, send_sem, recv_sem):
        me = 2 * x + y
        chips = _other_chips(x, y)

        def ici(t, r, half):
            cx, cy = chips[r]
            mine = refs[t].at[me, half]
            return pltpu.make_async_remote_copy(
                src_ref=mine, dst_ref=mine, send_sem=send_sem.at[3 * t + r], recv_sem=recv_sem.at[3 * t + r],
                device_id=(cx, cy, c), device_id_type=MESH)

        def d2d(t, r, half):
            cx, cy = chips[r]
            there = refs[t].at[2 * cx + cy, half]
            k = 3 * n + 3 * t + r
            return pltpu.make_async_remote_copy(
                src_ref=there, dst_ref=there, send_sem=send_sem.at[k], recv_sem=recv_sem.at[k],
                device_id=(x, y, 1 - c), device_id_type=MESH)

        for t in range(n):
            for r in range(3):
                ici(t, r, c).start()
        for t in range(n):
            for r in range(3):
                ici(t, r, c).wait_recv()
                d2d(t, r, c).start()
        for t in range(n):
            for r in range(3):
                d2d(t, r, 1 - c).wait_recv()
        for t in range(n):
            for r in range(3):
                ici(t, r, c).wait_send()
                d2d(t, r, c).wait_send()

    _sequencer(name, stage, 6 * n, lambda x, y, c: [(x, y, 1 - c)] + [(cx, cy, c) for cx, cy in _other_chips(x, y)],
               program)
    return [r[...] for r in refs]


def cast_into_slots(name, chip, shards, after=None):
    steps = 2
    n = len(shards)

    def body(chip_ref, *refs):
        for src, dst in zip(refs[:n], refs[-n - 1:-1]):
            dst[...] = src[...].astype(BF16)
        refs[-1][...] = jnp.zeros_like(refs[-1])

    token_spec = pl.BlockSpec((SUBLANES, LANES), lambda h, i, chip_ref: (0, 0))

    def spec_in(a, prefix):
        R, C = a.shape[-2:]
        return pl.BlockSpec((None,) * (len(prefix) + 1) + (R // steps, C), lambda h, i, chip_ref: prefix + (h, i, 0))

    def spec_out(a):
        R, C = a.shape[-2:]
        return pl.BlockSpec((None, None, R // steps, C), lambda h, i, chip_ref: (chip_ref[0], h, i, 0))

    outs = pl.pallas_call(
        body, name=name,
        grid_spec=pltpu.PrefetchScalarGridSpec(
            num_scalar_prefetch=1, grid=(2, steps),
            in_specs=[spec_in(a, p) for a, p in shards] + ([token_spec] if after is not None else []),
            out_specs=[spec_out(a) for a, _ in shards] + [token_spec]),
        out_shape=[jax.ShapeDtypeStruct((N_CHIP, 2) + a.shape[-2:], BF16) for a, _ in shards]
        + [jax.ShapeDtypeStruct((SUBLANES, LANES), F32)],
        compiler_params=_params("arbitrary", "arbitrary"),
    )(chip, *[a for a, _ in shards], *([after] if after is not None else []))
    return outs[:-1], outs[-1]


def reduce_pair(name, grads):
    n = len(grads)
    src = [_hbm_ref(g) for g in grads]
    dst = [_hbm_empty((N_CHIP,) + g.shape[2:], g.dtype) for g in grads]

    def program(x, y, c, send_sem, recv_sem):
        cps = [pltpu.make_async_remote_copy(
            src_ref=src[t].at[:, 1 - c], dst_ref=dst[t], send_sem=send_sem.at[t], recv_sem=recv_sem.at[t],
            device_id=(x, y, 1 - c), device_id_type=MESH) for t in range(n)]
        for cp in cps:
            cp.start()
        for cp in cps:
            cp.wait()

    _sequencer(name, ID_PAIR, n, lambda x, y, c: [(x, y, 1 - c)], program)
    return [r[...] for r in src], [r[...] for r in dst]


def pair_add(name, core, gs, gots):
    n = len(gs)

    def body(core_ref, *refs):
        for t in range(n):
            refs[2 * n + t][...] = (refs[2 * t][...] + refs[2 * t + 1][...]).astype(BF16)

    in_specs, out_specs, out_shape = [], [], []
    for g in gs:
        _, _, R, C = g.shape
        in_specs += [pl.BlockSpec((None, None, R, C), lambda q, core_ref: (q, core_ref[0], 0, 0)),
                     pl.BlockSpec((None, R, C), lambda q, core_ref: (q, 0, 0))]
        out_specs.append(pl.BlockSpec((None, R, C), lambda q, core_ref: (q, 0, 0)))
        out_shape.append(jax.ShapeDtypeStruct((N_CHIP, R, C), BF16))
    return pl.pallas_call(
        body, name=name,
        grid_spec=pltpu.PrefetchScalarGridSpec(num_scalar_prefetch=1, grid=(N_CHIP,), in_specs=in_specs,
                                               out_specs=out_specs),
        out_shape=out_shape, compiler_params=_params("parallel"),
    )(core, *[a for pair in zip(gs, gots) for a in pair])


def reduce_chips(name, sums):
    n = len(sums)
    src = [_hbm_ref(s) for s in sums]
    dst = [_hbm_empty((3,) + s.shape[1:], s.dtype) for s in sums]

    def program(x, y, c, send_sem, recv_sem):
        cps = []
        for t in range(n):
            for r, (cx, cy) in enumerate(_other_chips(x, y)):
                cps.append(pltpu.make_async_remote_copy(
                    src_ref=src[t].at[2 * cx + cy], dst_ref=dst[t].at[r],
                    send_sem=send_sem.at[3 * t + r], recv_sem=recv_sem.at[3 * t + r],
                    device_id=(cx, cy, c), device_id_type=MESH))
        for cp in cps:
            cp.start()
        for cp in cps:
            cp.wait()

    _sequencer(name, ID_CHIPS, 3 * n, lambda x, y, c: [(cx, cy, c) for cx, cy in _other_chips(x, y)], program)
    return [r[...] for r in src], [r[...] for r in dst]


def chip_add(name, place, items, after=None):
    n = len(items)

    def body(place_ref, *refs):
        for t in range(n):
            s_ref, got_ref, o_ref = refs[2 * t], refs[2 * t + 1], refs[len(refs) - n + t]
            o_ref[...] = ((s_ref[...].astype(F32) + got_ref[0].astype(F32)) + got_ref[1].astype(F32)) + got_ref[2].astype(F32)

    in_specs, args, out_specs, out_shape, aliases = [], [place], [], [], {}
    for s, got, k, n_slots, _ in items:
        _, R, C = s.shape
        in_specs += [pl.BlockSpec((None, R, C), lambda i, place_ref: (place_ref[0], 0, 0)),
                     pl.BlockSpec((3, R, C), lambda i, place_ref: (0, 0, 0))]
        args += [s, got]
        out_specs.append(pl.BlockSpec((None, None, R, C), lambda i, place_ref, k=k: (k, place_ref[1], 0, 0)))
        out_shape.append(jax.ShapeDtypeStruct((n_slots, 2, R, C), F32))
    for t, (*_, prev) in enumerate(items):
        if prev is not None:
            aliases[len(args)] = t
            in_specs.append(_ANY)
            args.append(prev)
    for piece in after or ():
        in_specs.append(pl.BlockSpec((SUBLANES, LANES), lambda i, place_ref: (0, 0)))
        args.append(piece)
    return pl.pallas_call(
        body, name=name,
        grid_spec=pltpu.PrefetchScalarGridSpec(num_scalar_prefetch=1, grid=(1,), in_specs=in_specs,
                                               out_specs=out_specs),
        out_shape=out_shape, input_output_aliases=aliases, compiler_params=_params("arbitrary"),
    )(*args)


def share_halves(name, stacks, slots):
    n = len(stacks)
    dst = [_hbm_ref(s) for s in stacks]

    def program(x, y, c, send_sem, recv_sem):
        cps = [pltpu.make_async_remote_copy(
            src_ref=dst[t].at[slots[t], c], dst_ref=dst[t].at[slots[t], c],
            send_sem=send_sem.at[t], recv_sem=recv_sem.at[t],
            device_id=(x, y, 1 - c), device_id_type=MESH) for t in range(n)]
        for cp in cps:
            cp.start()
        for cp in cps:
            cp.wait()

    _sequencer(name, ID_SHARE, n, lambda x, y, c: [(x, y, 1 - c)], program)
    return [r[...] for r in dst]


def gather_blocks(name, slotted):
    out = _hbm_ref(slotted)

    def program(x, y, c, send_sem, recv_sem):
        sibling = (x, y, 1 - c)
        chips = _other_chips(x, y)

        def copy(k, px, py, pc, to):
            block = out.at[4 * px + 2 * py + pc]
            return pltpu.make_async_remote_copy(src_ref=block, dst_ref=block, send_sem=send_sem.at[k],
                                                recv_sem=recv_sem.at[k], device_id=to, device_id_type=MESH)

        first = [copy(0, x, y, c, sibling)] + [copy(1 + j, x, y, c, (cx, cy, c)) for j, (cx, cy) in enumerate(chips)]
        for cp in first:
            cp.start()
        passed = [copy(4 + j, cx, cy, c, sibling) for j, (cx, cy) in enumerate(chips)]
        for j, (cx, cy) in enumerate(chips):
            copy(1 + j, cx, cy, c, (x, y, c)).wait_recv()
            passed[j].start()
        copy(0, x, y, 1 - c, (x, y, c)).wait_recv()
        for j, (cx, cy) in enumerate(chips):
            copy(4 + j, cx, cy, 1 - c, (x, y, c)).wait_recv()
        for cp in first + passed:
            cp.wait_send()

    _sequencer(name, ID_UKV, 7, lambda x, y, c: [(x, y, 1 - c)] + [(cx, cy, c) for cx, cy in _other_chips(x, y)],
               program)
    return out[...]


def place_block(name, dev, a):
    M, N = a.shape
    tr = min(M, 64)

    def body(dev_ref, a_ref, o_ref):
        o_ref[...] = a_ref[...]

    return pl.pallas_call(
        body, name=name,
        grid_spec=pltpu.PrefetchScalarGridSpec(
            num_scalar_prefetch=1, grid=(M // tr,),
            in_specs=[pl.BlockSpec((tr, N), lambda i, dev_ref: (i, 0))],
            out_specs=pl.BlockSpec((None, tr, N), lambda i, dev_ref: (dev_ref[0], i, 0))),
        out_shape=jax.ShapeDtypeStruct((N_DEV, M, N), a.dtype),
        compiler_params=_params("parallel"),
    )(dev, a)


def _swap_rope(a):
    return jnp.concatenate([a[..., ROPE // 2:], a[..., :ROPE // 2]], axis=-1)


def _rope_tables(S):
    inv = 1.0 / (ROPE_THETA ** (jnp.arange(0, ROPE, 2, dtype=F32) / ROPE))
    ang = jnp.arange(S, dtype=F32)[:, None] * inv[None, :]
    cos, sin = jnp.cos(ang), jnp.sin(ang)
    return (jnp.tile(jnp.concatenate([cos, cos], axis=1), (1, SLOTS)),
            jnp.tile(jnp.concatenate([-sin, sin], axis=1), (1, SLOTS)))


def _vec(norm_g, mod, i, k):
    rows = [norm_g[i, 2 * k], norm_g[i, 2 * k + 1], mod[i, 3 * k], mod[i, 3 * k + 1], mod[i, 3 * k + 2]]
    return jnp.concatenate([jnp.stack(rows), jnp.zeros((3, D), F32)], axis=0)


def _unpack_weights(full, w_uk, w_uv, q_norm, kv_norm):
    G = D // 4
    ffn_in = [[full[2 * i + k].reshape(N_CHIP, D, FSH) for k in range(2)] for i in range(2)]
    ffn_out = [[full[4 + 2 * i + k].reshape(2, FSH, D) for k in range(2)] for i in range(2)]
    pw = full[8].reshape(N_CHIP, 4, G // N_CHIP, G).transpose(1, 0, 2, 3).reshape(4, G, G)
    w_in = full[9].reshape(D, QL + KVL + ROPE)
    w_uq = full[10].reshape(QL, N_HEADS, NOPE + ROPE)
    wkr = w_in[:, QL + KVL:]
    wr = w_uq[:, :, NOPE:]
    eye2 = jnp.eye(2, dtype=BF16)
    uk_t = jnp.transpose(w_uk, (1, 2, 0)).reshape(N_PAIR, 2, NOPE, KVL)
    bduk = jnp.einsum("janc,ab->janbc", uk_t, eye2).reshape(N_PAIR, 2 * NOPE, 2 * KVL)
    uv = jnp.transpose(w_uv, (1, 0, 2)).reshape(N_PAIR, 2, KVL, VH)
    bduv = jnp.einsum("jacn,ab->jacbn", uv, eye2).reshape(N_PAIR, 2 * KVL, 2 * VH)
    mw = dict(wq=w_in[:, :QL], wkv=w_in[:, QL:QL + KVL], wkr4=jnp.tile(wkr, (1, SLOTS)),
              wkrs4=jnp.tile(_swap_rope(wkr), (1, SLOTS)), qn=q_norm, kvn=kv_norm,
              wn=w_uq[:, :, :NOPE].reshape(QL, NOPE_ALL), wr=wr.reshape(QL, ROPE_ALL),
              wrs=_swap_rope(wr).reshape(QL, ROPE_ALL), bduk=bduk)
    return ffn_in, ffn_out, pw, mw, bduv, full[11].reshape(D, D)


def _example_step(x, target, mod, norm_g, pvec, ffn_in, ffn_out, pw, mw, bduv, wo, reducer):
    S = x.shape[0]
    tabs = _rope_tables(S)
    vec = [[_vec(norm_g, mod, i, k) for k in range(3)] for i in range(2)]
    saved = {}
    for i in range(2):
        xin = x
        x, a, u, h = ffn_fwd(xin, vec[i][0], ffn_in[i][0], ffn_out[i][0], 0.5)
        saved[i, 0] = (xin, a, u, h)
        xin = x
        if i == 0:
            x, y, z = pool_fwd(xin, vec[i][1], pw, pvec)
            saved[i, 1] = (xin, y, z)
        else:
            h_m, cq_raw, ckv_raw, cqn, qnope, qcat, kcat, vcat = mla_pre(xin, vec[i][1], mw, tabs)
            olat, lse = attn_fwd(qcat, kcat, vcat)
            x, u_m, ocat = mla_post(olat, xin, vec[i][1], bduv, wo)
            saved[i, 1] = (xin, h_m, cq_raw, ckv_raw, cqn, qnope, qcat, kcat, olat, lse, u_m, ocat)
        xin = x
        if i == 0:
            x, a, u, h = ffn_fwd(xin, vec[i][2], ffn_in[i][1], ffn_out[i][1], 0.5)
        else:
            dx, a, u, h, loss = ffn_fwd(xin, vec[i][2], ffn_in[i][1], ffn_out[i][1], 0.5, target)
        saved[i, 2] = (xin, a, u, h)

    vg = {}
    G = D // 4

    def ffn_grads(i, k, dw_in, dw_out):
        return [(0, 2 * i + k, 4, dw_in.reshape(N_CHIP, 2, D // 2, FSH)),
                (1, 2 * i + k, 4, dw_out.reshape(N_CHIP, 2, DFF // 8, D))]

    piece = lambda t: t[:SUBLANES, :LANES]
    for i in (1, 0):
        xin, a, u, h = saved[i, 2]
        dx, du, act, da, vg[i, 2] = ffn_bwd(dx, xin, u, a, vec[i][2], ffn_in[i][1], ffn_out[i][1], 0.5)
        reducer.advance(after=(piece(dx),))
        reducer.add(f"f{i}1", ffn_grads(i, 1, *ffn_dw(h, da, act, du)))
        if i == 0:
            xin, y, z = saved[i, 1]
            dx, vg[i, 1], pgrad, g_pool = pool_bwd(dx, xin, y, z, vec[i][1], pw, pvec)
        else:
            xin, h_m, cq_raw, ckv_raw, cqn, qnope, qcat, kcat, olat, lse, u_m, ocat = saved[i, 1]
            du, docat, dolat, delta, vg_post = mla_post_bwd(dx, u_m, olat, vec[i][1], bduv, wo)
            reducer.advance()
            dq, dk, dv = attn_bwd(qcat, kcat, kcat.T, dolat, lse, delta)
            reducer.advance(after=(piece(dk),))
            dx, dlat, dql, dqcat, vg_pre, ngrad = mla_pre_bwd(
                dx, dq, dk, dv, xin, cq_raw, ckv_raw, vec[i][1], mw, tabs)
            vg[i, 1] = vg_post + vg_pre
            g = mla_dw(h_m, dlat, cqn, dqcat, dql, qnope, olat, docat, ocat, du)
            slots = lambda a: a.reshape(D, SLOTS, ROPE).sum(axis=1)
            g_kr = slots(g["in"][:, QL + KVL:QL + KVL + LANES]) + _swap_rope(slots(g["in"][:, QL + KVL + LANES:]))
            g_in = jnp.concatenate([g["in"][:, :QL + KVL], g_kr], axis=1)
            g_r = g["q"][:, NOPE_ALL:NOPE_ALL + ROPE_ALL].reshape(QL, N_HEADS, ROPE)
            g_rs = g["q"][:, NOPE_ALL + ROPE_ALL:].reshape(QL, N_HEADS, ROPE)
            g_uq = jnp.concatenate([g["q"][:, :NOPE_ALL].reshape(QL, N_HEADS, NOPE), g_r + _swap_rope(g_rs)], axis=-1)

            def heads(pairs):
                blk = pairs.reshape(N_PAIR, 2, KVL, 2, NOPE)
                per_head = jnp.stack([blk[:, 0, :, 0, :], blk[:, 1, :, 1, :]], axis=1).reshape(N_HEADS, KVL, NOPE)
                return jnp.transpose(per_head, (1, 0, 2)).reshape(KVL, N_HEADS * NOPE)

            reducer.add("mla", [(3, 0, 1, g_in.reshape(N_CHIP, 2, D // 8, QL + KVL + ROPE)),
                                (4, 0, 1, g_uq.reshape(N_CHIP, 2, QL // 8, N_HEADS * (NOPE + ROPE))),
                                (5, 0, 1, g["o"].reshape(N_CHIP, 2, D // 8, D))])
            reducer.add_replicated(jnp.concatenate([heads(g["uk"]), heads(g["uv"])], axis=0))
        xin, a, u, h = saved[i, 0]
        dx, du, act, da, vg[i, 0] = ffn_bwd(dx, xin, u, a, vec[i][0], ffn_in[i][0], ffn_out[i][0], 0.5)
        reducer.advance(after=(piece(dx),))
        grads = ffn_grads(i, 0, *ffn_dw(h, da, act, du))
        if i == 0:
            grads.append((2, 0, 1, g_pool.reshape(N_CHIP, 2, 2 * G // N_CHIP, G)))
        reducer.add(f"f{i}0", grads)
    return loss, dx, vg, pgrad, ngrad


class _GradReducer:
    def __init__(self, core, place, dev):
        self.core, self.place, self.dev = core, place, dev
        self.stacks = {}
        self.live = []
        self.replicated = None

    def add(self, tag, items):
        gen = self._run(tag, items)
        next(gen)
        self.live.append(gen)

    def add_replicated(self, block):
        self.replicated = gather_blocks("gather_ukv", place_block("place_ukv", self.dev, block))

    def advance(self, after=None):
        self.after = after
        live = []
        for gen in self.live:
            try:
                next(gen)
                live.append(gen)
            except StopIteration:
                pass
        self.live = live

    def finish(self):
        while self.live:
            self.advance()
        return self.stacks, self.replicated

    def _run(self, tag, items):
        grads, from_pair = reduce_pair(f"reduce_pair_{tag}", [g for *_, g in items])
        yield
        sums = pair_add(f"pair_add_{tag}", self.core, grads, from_pair)
        sums, from_chips = reduce_chips(f"reduce_chips_{tag}", sums)
        yield
        stacks = chip_add(f"chip_add_{tag}", self.place,
                          [(s, p, k, n_slots, self.stacks.get(o)) for (o, k, n_slots, _), s, p
                           in zip(items, sums, from_chips)], self.after)
        for (o, *_), stack in zip(items, stacks):
            self.stacks[o] = stack
        shared = share_halves(f"share_halves_{tag}", [self.stacks[o] for o, *_ in items], [k for _, k, *_ in items])
        for (o, *_), v in zip(items, shared):
            self.stacks[o] = v


SMALL_IN = 8 * 640
SMALL_GRAD = 8 * 4224
SMALL_W = 8 * 2944


def _pack(parts, total):
    flat = jnp.concatenate([p.reshape(-1) for p in parts])
    return jnp.concatenate([flat, jnp.zeros((total - flat.shape[0],), F32)]).reshape(8, total // 8)


def kernel(x, c, ada_w, ada_b, norm_g, ffn_w_in, ffn_w_out, pool_w, pool_b, pool_scale, mla_w_in, mla_q_norm, mla_kv_norm, mla_w_uq, mla_w_uk, mla_w_uv, mla_w_o, loss_target, m_ada_w, m_ada_b, m_norm_g, m_ffn_w_in, m_ffn_w_out, m_pool_w, m_pool_b, m_pool_scale, m_mla_w_in, m_mla_q_norm, m_mla_kv_norm, m_mla_w_uq, m_mla_w_uk, m_mla_w_uv, m_mla_w_o, v_ada_w, v_ada_b, v_norm_g, v_ffn_w_in, v_ffn_w_out, v_pool_w, v_pool_b, v_pool_scale, v_mla_w_in, v_mla_q_norm, v_mla_kv_norm, v_mla_w_uq, v_mla_w_uk, v_mla_w_uv, v_mla_w_o):
    ix, iy, ic = _place()
    chip = 2 * ix + iy
    dev = 2 * chip + ic
    core_arr = ic.astype(jnp.int32).reshape(1)
    chip_arr = chip.astype(jnp.int32).reshape(1)
    S = x.shape[1]
    G = D // 4
    NG = D // N_CHIP

    def chip_cols(a, width, axis):
        return lax.dynamic_slice_in_dim(a, chip * width, width, axis)

    got = gather_devices("gather_small_in", _pack([c, norm_g, pool_b, mla_q_norm], SMALL_IN)).reshape(N_DEV, SMALL_IN)
    c_all = got[:, :D]
    parts = got[0::2]
    o = D
    norm_g_full = parts[:, o:o + 12 * NG].reshape(N_CHIP, 2, 6, NG).transpose(1, 2, 0, 3).reshape(2, 6, D)
    o += 12 * NG
    pool_b_full = parts[:, o:o + G].reshape(N_CHIP, 4, G // N_CHIP).transpose(1, 0, 2).reshape(1, D)
    o += G
    q_norm_full = parts[:, o:o + QL // N_CHIP].reshape(1, QL)
    pvec = jnp.concatenate([pool_b_full, pool_scale, jnp.zeros((6, D), F32)], axis=0)

    c_pad = jnp.concatenate([c_all, jnp.zeros((8, D), F32)], axis=0)
    mod_loc = mod_fwd(c_pad, ada_w, chip_cols(ada_b, MOD_COLS, 1).reshape(2, 1, MOD_COLS))
    got = gather_devices("gather_mod", mod_loc[:, :8].transpose(1, 0, 2).reshape(8, 2 * MOD_COLS))
    mine = lax.dynamic_index_in_dim(got[0::2].reshape(N_CHIP, 8, 2, MOD_COLS), dev, axis=1, keepdims=False)
    mod = mine.transpose(1, 0, 2).reshape(2, 9, D)

    bf = lambda a: a.astype(BF16)
    w_in_halves = ffn_w_in.reshape(2, 2, 2, D // 2, FSH)
    w_out_halves = ffn_w_out.reshape(2, 2, 2, DFF // 8, D)
    shards = [(w_in_halves, (i, k)) for i in range(2) for k in range(2)]
    shards += [(w_out_halves, (i, k)) for i in range(2) for k in range(2)]
    shards += [(pool_w.reshape(2, 2 * G // N_CHIP, G), ()), (mla_w_in.reshape(2, D // 8, QL + KVL + ROPE), ()),
               (mla_w_uq.reshape(2, QL // 8, N_HEADS * (NOPE + ROPE)), ()), (mla_w_o.reshape(2, D // 8, D), ())]
    full = [None] * len(shards)
    stages = [(0, 4, 8), (1, 5), (2, 6), (9, 10, 11), (3, 7)]
    first, token = cast_into_slots("cast_first", chip_arr, [shards[t] for t in stages[0]])
    slotted = dict(zip(stages[0], first))
    rest = [t for members in stages[1:] for t in members]
    for stage, members in enumerate(stages):
        got_w = gather_weights(f"gather_weights_{stage}", stage, [slotted[t] for t in members])
        for t, a in zip(members, got_w):
            full[t] = a
        if stage == 0:
            slotted.update(zip(rest, cast_into_slots("cast_rest", chip_arr, [shards[t] for t in rest], token)[0]))
    ffn_in, ffn_out, pw, mw, bduv, wo = _unpack_weights(full, bf(mla_w_uk[0]), bf(mla_w_uv[0]), q_norm_full,
                                                        mla_kv_norm)

    place_arr = jnp.stack([chip, ic]).astype(jnp.int32)
    reducer = _GradReducer(core_arr, place_arr, dev.astype(jnp.int32).reshape(1))
    loss_mine, grad_x, vg, pgrad, ngrad = _example_step(
        x[0], loss_target[0], mod, norm_g_full, pvec, ffn_in, ffn_out, pw, mw, bduv, wo, reducer)

    dmod = jnp.stack([jnp.concatenate([vg[i, k][0:3] for k in range(3)]) for i in range(2)])
    dnorm = jnp.stack([jnp.concatenate([vg[i, k][3:5] for k in range(3)]) for i in range(2)])
    small = _pack([dmod, dnorm, pgrad[0], pgrad[1], ngrad[0], ngrad[1, :KVL], loss_mine], SMALL_GRAD)
    got = gather_devices("gather_small_grad", small)
    tot = sum_devices("sum_small_grad", got).reshape(-1)
    n_mod = 2 * 9 * D
    g_ada_b = tot[:n_mod].reshape(ada_b.shape)
    o = n_mod
    g_norm = chip_cols(tot[o:o + 12 * D].reshape(2, 6, D), NG, 2)
    o += 12 * D
    g_pool_b = chip_cols(tot[o:o + D].reshape(1, 4, G), G // N_CHIP, 2)
    o += D
    g_pool_scale = tot[o:o + D].reshape(pool_scale.shape)
    o += D
    g_q_norm = chip_cols(tot[o:o + QL].reshape(1, QL), QL // N_CHIP, 1)
    o += QL
    g_kv_norm = tot[o:o + KVL].reshape(mla_kv_norm.shape)
    loss = tot[o + KVL]
    dmod_all = chip_cols(got.reshape(N_DEV, -1)[:, :n_mod].reshape(N_DEV, 2, 9 * D), MOD_COLS, 2)
    dmod_pad = jnp.concatenate([dmod_all.transpose(1, 0, 2), jnp.zeros((2, 8, MOD_COLS), F32)], axis=1)

    g_ada_w, d_ada_w, nm_ada_w, nv_ada_w = adamw_ada(c_pad, dmod_pad, ada_w, m_ada_w, v_ada_w)
    small_names = ["ada_b", "norm_g", "pool_b", "pool_scale", "mla_q_norm", "mla_kv_norm"]
    small_w = [ada_b, norm_g, pool_b, pool_scale, mla_q_norm, mla_kv_norm]
    small_g = [g_ada_b, g_norm, g_pool_b, g_pool_scale, g_q_norm, g_kv_norm]
    small_m = [m_ada_b, m_norm_g, m_pool_b, m_pool_scale, m_mla_q_norm, m_mla_kv_norm]
    small_v = [v_ada_b, v_norm_g, v_pool_b, v_pool_scale, v_mla_q_norm, v_mla_kv_norm]
    packed = adamw("adamw_small", *[_pack(p, SMALL_W) for p in (small_w, small_g, small_m, small_v)])
    upd = {}
    o = 0
    for name, w in zip(small_names, small_w):
        upd[name] = [p.reshape(-1)[o:o + w.size].reshape(w.shape) for p in packed]
        o += w.size
    upd["ada_w"] = [d_ada_w, nm_ada_w, nv_ada_w]

    reducer.advance(after=(d_ada_w[0, :SUBLANES, :LANES],))
    ffn = [("ffn_w_in", 0, ffn_w_in, m_ffn_w_in, v_ffn_w_in), ("ffn_w_out", 1, ffn_w_out, m_ffn_w_out, v_ffn_w_out)]
    slots = lambda a: a.reshape((4,) + a.shape[2:])
    early = {name: adamw(f"adamw_{name}_early", slots(w), slots(reducer.stacks[o].reshape(w.shape)), slots(m),
                         slots(v), part=(1, 3), copy_grad=True) for name, o, w, m, v in ffn}
    g_mla_in = reducer.stacks[3].reshape(mla_w_in.shape)
    g_uq = reducer.stacks[4].reshape(mla_w_uq.shape)
    g_wo = reducer.stacks[5].reshape(mla_w_o.shape)
    upd["mla_w_in"], upd["mla_w_uq"], upd["mla_w_o"] = adamw_whole(
        "adamw_mla", [(mla_w_in, g_mla_in, m_mla_w_in, v_mla_w_in), (mla_w_uq, g_uq, m_mla_w_uq, v_mla_w_uq),
                      (mla_w_o, g_wo, m_mla_w_o, v_mla_w_o)])

    reducer.advance(after=(early["ffn_w_in"][0][1, :SUBLANES, :LANES], early["ffn_w_out"][0][1, :SUBLANES, :LANES],
                           upd["mla_w_o"][0][0, :SUBLANES, :LANES], upd["mla_w_in"][0][0, :SUBLANES, :LANES]))
    ukv = sum_devices("sum_ukv", reducer.replicated)
    g_uk = ukv[:KVL].reshape(mla_w_uk.shape)
    g_uv = ukv[KVL:].reshape(mla_w_uv.shape)
    upd["mla_w_uk"], upd["mla_w_uv"] = adamw_whole(
        "adamw_ukv", [(mla_w_uk, g_uk, m_mla_w_uk, v_mla_w_uk), (mla_w_uv, g_uv, m_mla_w_uv, v_mla_w_uv)])
    stacks, _ = reducer.finish()
    g_pool_w = stacks[2].reshape(pool_w.shape)
    g_ffn = {}
    for name, o, w, m, v in ffn:
        done = adamw(f"adamw_{name}_last", slots(w), slots(stacks[o].reshape(w.shape)), slots(m), slots(v),
                     part=(0, 1), prev=early[name], copy_grad=True)
        upd[name] = [p.reshape(w.shape) for p in done[:3]]
        g_ffn[name] = done[3].reshape(w.shape)
    g_ffn_in, g_ffn_out = g_ffn["ffn_w_in"], g_ffn["ffn_w_out"]
    upd["pool_w"] = adamw("adamw_pool_w", pool_w, g_pool_w, m_pool_w, v_pool_w)

    order = ["ada_w", "ada_b", "norm_g", "ffn_w_in", "ffn_w_out", "pool_w", "pool_b", "pool_scale", "mla_w_in",
             "mla_q_norm", "mla_kv_norm", "mla_w_uq", "mla_w_uk", "mla_w_uv", "mla_w_o"]
    grad = dict(ada_w=g_ada_w, ada_b=g_ada_b, norm_g=g_norm, ffn_w_in=g_ffn_in, ffn_w_out=g_ffn_out, pool_w=g_pool_w,
                pool_b=g_pool_b, pool_scale=g_pool_scale, mla_w_in=g_mla_in, mla_q_norm=g_q_norm,
                mla_kv_norm=g_kv_norm, mla_w_uq=g_uq, mla_w_uk=g_uk, mla_w_uv=g_uv, mla_w_o=g_wo)
    return (loss, grad_x[None], *[grad[n] for n in order], *[upd[n][0] for n in order],
            *[upd[n][1] for n in order], *[upd[n][2] for n in order])
```

```python
import functools

import jax
import jax.numpy as jnp
from jax import lax
from jax.experimental import pallas as pl
from jax.experimental.pallas import tpu as pltpu
from jax.experimental.pallas import tpu_sc as plsc

F32 = jnp.float32
BF16 = jnp.bfloat16

D = 1024
DFF = 2816
FSH = 1408
N_CHIP = 4
N_DEV = 8
N_HEADS = 16
NOPE = 64
ROPE = 32
VH = 64
QL = 256
KVL = 128
LANES = 128
SUBLANES = 8
QPAD = 256
EPS = 1e-6
ATTN_SCALE = (NOPE + ROPE) ** -0.5
ROPE_THETA = 10000.0
POOL_WINDOWS = (2, 4, 8, 16)
HALO = 8
ATTN_TQ = 1024
ATTN_KC = 512
ROW_TILE = 512
DW_TK = 2048

ADAM_LR, ADAM_B1, ADAM_B2, ADAM_EPS, ADAM_WD, ADAM_STEP = 0.001, 0.9, 0.999, 1e-08, 0.01, 10

VMEM_LIMIT = 60 * 1024 * 1024
MESH = pl.DeviceIdType.MESH

NT = (((1,), (1,)), ((), ()))
TN = (((0,), (0,)), ((), ()))


def _params(*sem):
    return pltpu.CompilerParams(dimension_semantics=sem, vmem_limit_bytes=VMEM_LIMIT)


def _dot(a, b, dims=None):
    if dims is None:
        return jnp.dot(a, b, preferred_element_type=F32)
    return lax.dot_general(a, b, dims, preferred_element_type=F32)


def _rms(x):
    r = lax.rsqrt(jnp.mean(x * x, axis=-1, keepdims=True) + EPS)
    return x * r, r


def _rms_bwd(xhat, r, dxhat):
    return r * (dxhat - xhat * jnp.mean(dxhat * xhat, axis=-1, keepdims=True))


def _as_row(col):
    return jnp.broadcast_to(col, (col.shape[0], LANES)).T[0:1, :]


def _prenorm(x, vec_ref):
    xhat, r = _rms(x)
    h = xhat * vec_ref[0:1, :] * (1.0 + vec_ref[3:4, :]) + vec_ref[2:3, :]
    return h, xhat, r


def _postnorm_bwd(dout, u, vec_ref, weight):
    uhat, r = _rms(u)
    gt = weight * (1.0 + vec_ref[4:5, :])
    dy = dout * gt
    dgate_rows = (weight * dout) * (uhat * vec_ref[1:2, :])
    dgpost_rows = dy * uhat
    du = _rms_bwd(uhat, r, dy * vec_ref[1:2, :])
    return du, dgate_rows, dgpost_rows


def _prenorm_bwd(dh, x, vec_ref, vg_ref):
    xhat, r = _rms(x)
    sc1 = 1.0 + vec_ref[3:4, :]
    g = vec_ref[0:1, :]
    vg_ref[0:1, :] += jnp.sum(dh, axis=0, keepdims=True)
    vg_ref[1:2, :] += jnp.sum(dh * (xhat * g), axis=0, keepdims=True)
    vg_ref[3:4, :] += jnp.sum(dh * sc1 * xhat, axis=0, keepdims=True)
    return _rms_bwd(xhat, r, dh * g * sc1)


def ffn_fwd(x, vec, w_in, w_out, weight, target=None):
    S = x.shape[0]
    tm = min(512, S)
    row = lambda i: (i, 0)
    half = lambda j: [_w3((8, D)), pl.BlockSpec((None, D, FSH), lambda i: (j, 0, 0)),
                      pl.BlockSpec((None, D, FSH), lambda i: (j + 2, 0, 0)),
                      pl.BlockSpec((None, FSH, D), lambda i: (j, 0, 0))]
    a_spec = lambda j: pl.BlockSpec((2, tm, FSH), lambda i: (0, i, j))
    a_shape = jax.ShapeDtypeStruct((2, S, DFF), BF16)

    def hidden(hb, wg_ref, wu_ref, wo_ref, a_ref):
        g = _dot(hb, wg_ref[...])
        up = _dot(hb, wu_ref[...])
        a_ref[0] = g.astype(BF16)
        a_ref[1] = up.astype(BF16)
        act = (g * jax.nn.sigmoid(g)) * up
        return _dot(act.astype(BF16), wo_ref[...])

    def first(x_ref, vec_ref, wg_ref, wu_ref, wo_ref, h_ref, a_ref, u_ref):
        h, _, _ = _prenorm(x_ref[...], vec_ref)
        hb = h.astype(BF16)
        h_ref[...] = hb
        u_ref[...] = hidden(hb, wg_ref, wu_ref, wo_ref, a_ref)

    h, a, u_half = pl.pallas_call(
        first, name="ffn_fwd_first", grid=(S // tm,),
        in_specs=[pl.BlockSpec((tm, D), row)] + half(0),
        out_specs=[pl.BlockSpec((tm, D), row), a_spec(0), pl.BlockSpec((tm, D), row)],
        out_shape=[jax.ShapeDtypeStruct((S, D), BF16), a_shape, jax.ShapeDtypeStruct((S, D), F32)],
        compiler_params=_params("parallel"),
    )(x, vec, w_in, w_in, w_out)

    def second(x_ref, h_ref, uh_ref, vec_ref, wg_ref, wu_ref, wo_ref, a_in, xo_ref, a_ref, u_ref):
        u = uh_ref[...] + hidden(h_ref[...], wg_ref, wu_ref, wo_ref, a_ref)
        u_ref[...] = u
        uhat, _ = _rms(u)
        xo_ref[...] = x_ref[...] + (weight * (1.0 + vec_ref[4:5, :])) * (uhat * vec_ref[1:2, :])

    def second_with_loss(x_ref, h_ref, uh_ref, vec_ref, wg_ref, wu_ref, wo_ref, t_ref, a_in,
                         dy_ref, a_ref, u_ref, loss_ref):
        @pl.when(pl.program_id(0) == 0)
        def _():
            loss_ref[...] = jnp.zeros_like(loss_ref)

        second(x_ref, h_ref, uh_ref, vec_ref, wg_ref, wu_ref, wo_ref, a_in, dy_ref, a_ref, u_ref)
        err = dy_ref[...] - t_ref[...]
        dy_ref[...] = err * (1.0 / D)
        loss_ref[...] += 0.5 * jnp.sum(jnp.mean(err * err, axis=-1, keepdims=True), axis=0, keepdims=True)

    rows3 = [pl.BlockSpec((tm, D), row)] * 3
    if target is None:
        xo, a, u = pl.pallas_call(
            second, name="ffn_fwd_second", grid=(S // tm,),
            in_specs=rows3 + half(1) + [_ANY],
            out_specs=[pl.BlockSpec((tm, D), row), a_spec(1), pl.BlockSpec((tm, D), row)],
            out_shape=[jax.ShapeDtypeStruct((S, D), F32), a_shape, jax.ShapeDtypeStruct((S, D), F32)],
            input_output_aliases={7: 1},
            compiler_params=_params("parallel"),
        )(x, h, u_half, vec, w_in, w_in, w_out, a)
        return xo, a, u, h
    dy, a, u, loss = pl.pallas_call(
        second_with_loss, name="ffn_fwd_last", grid=(S // tm,),
        in_specs=rows3 + half(1) + [pl.BlockSpec((tm, D), row), _ANY],
        out_specs=[pl.BlockSpec((tm, D), row), a_spec(1), pl.BlockSpec((tm, D), row), _w3((1, 1))],
        out_shape=[jax.ShapeDtypeStruct((S, D), F32), a_shape, jax.ShapeDtypeStruct((S, D), F32),
                   jax.ShapeDtypeStruct((1, 1), F32)],
        input_output_aliases={8: 1},
        compiler_params=_params("arbitrary"),
    )(x, h, u_half, vec, w_in, w_in, w_out, target, a)
    return dy, a, u, h, loss


def ffn_bwd(dout, x, u, a, vec, w_in, w_out, weight):
    S = x.shape[0]
    tm = min(512, S)
    row = lambda i: (i, 0)
    half = lambda j: [pl.BlockSpec((2, tm, FSH), lambda i: (0, i, j)), _w3((8, D)),
                      pl.BlockSpec((None, D, FSH), lambda i: (j, 0, 0)),
                      pl.BlockSpec((None, D, FSH), lambda i: (j + 2, 0, 0)),
                      pl.BlockSpec((None, FSH, D), lambda i: (j, 0, 0))]
    half_out = lambda j: [pl.BlockSpec((tm, FSH), lambda i: (i, j)), pl.BlockSpec((2, tm, FSH), lambda i: (0, i, j))]
    half_shape = [jax.ShapeDtypeStruct((S, DFF), BF16), jax.ShapeDtypeStruct((2, S, DFF), BF16)]

    def hidden_bwd(du, a_ref, wg_ref, wu_ref, wo_ref, act_ref, da_ref):
        dact = _dot(du, wo_ref[...], NT)
        g = a_ref[0].astype(F32)
        up = a_ref[1].astype(F32)
        s = jax.nn.sigmoid(g)
        silu = g * s
        act_ref[...] = (silu * up).astype(BF16)
        dg = (dact * up * (s * (1.0 + g * (1.0 - s)))).astype(BF16)
        dup = (dact * silu).astype(BF16)
        da_ref[0] = dg
        da_ref[1] = dup
        return _dot(dg, wg_ref[...], NT) + _dot(dup, wu_ref[...], NT)

    def first(do_ref, u_ref, a_ref, vec_ref, wg_ref, wu_ref, wo_ref, du_ref, dh_ref, act_ref, da_ref, vg_ref):
        @pl.when(pl.program_id(0) == 0)
        def _():
            vg_ref[...] = jnp.zeros_like(vg_ref)

        du, dgate_rows, dgpost_rows = _postnorm_bwd(do_ref[...], u_ref[...], vec_ref, weight)
        vg_ref[2:3, :] += jnp.sum(dgate_rows, axis=0, keepdims=True)
        vg_ref[4:5, :] += jnp.sum(dgpost_rows, axis=0, keepdims=True)
        du = du.astype(BF16)
        du_ref[...] = du
        dh_ref[...] = hidden_bwd(du, a_ref, wg_ref, wu_ref, wo_ref, act_ref, da_ref)

    du, dh, act, da, vg_post = pl.pallas_call(
        first, name="ffn_bwd_first", grid=(S // tm,),
        in_specs=[pl.BlockSpec((tm, D), row), pl.BlockSpec((tm, D), row)] + half(0),
        out_specs=[pl.BlockSpec((tm, D), row), pl.BlockSpec((tm, D), row)] + half_out(0) + [_w3((8, D))],
        out_shape=[jax.ShapeDtypeStruct((S, D), BF16), jax.ShapeDtypeStruct((S, D), F32)] + half_shape
        + [jax.ShapeDtypeStruct((8, D), F32)],
        compiler_params=_params("arbitrary"),
    )(dout, u, a, vec, w_in, w_in, w_out)

    def second(do_ref, x_ref, du_ref, dh_ref, a_ref, vec_ref, wg_ref, wu_ref, wo_ref, act_in, da_in,
               dx_ref, act_ref, da_ref, vg_ref):
        @pl.when(pl.program_id(0) == 0)
        def _():
            vg_ref[...] = jnp.zeros_like(vg_ref)

        dh = dh_ref[...] + hidden_bwd(du_ref[...], a_ref, wg_ref, wu_ref, wo_ref, act_ref, da_ref)
        dx_ref[...] = do_ref[...] + _prenorm_bwd(dh, x_ref[...], vec_ref, vg_ref)

    dx, act, da, vg_pre = pl.pallas_call(
        second, name="ffn_bwd_second", grid=(S // tm,),
        in_specs=[pl.BlockSpec((tm, D), row), pl.BlockSpec((tm, D), row), pl.BlockSpec((tm, D), row),
                  pl.BlockSpec((tm, D), row)] + half(1) + [_ANY, _ANY],
        out_specs=[pl.BlockSpec((tm, D), row)] + half_out(1) + [_w3((8, D))],
        out_shape=[jax.ShapeDtypeStruct((S, D), F32)] + half_shape + [jax.ShapeDtypeStruct((8, D), F32)],
        input_output_aliases={9: 1, 10: 2},
        compiler_params=_params("arbitrary"),
    )(dout, x, du, dh, a, vec, w_in, w_in, w_out, act, da)
    return dx, du, act, da, vg_post + vg_pre


def dw_matmul(name, a, b, a_spec, b_spec, out_shape, out_spec, grid):
    def body(a_ref, b_ref, o_ref):
        @pl.when(pl.program_id(len(grid) - 1) == 0)
        def _():
            o_ref[...] = jnp.zeros_like(o_ref)

        o_ref[...] += _dot(a_ref[...], b_ref[...], TN)

    return pl.pallas_call(
        body, name=name, grid=grid, in_specs=[a_spec, b_spec], out_specs=out_spec,
        out_shape=jax.ShapeDtypeStruct(out_shape, F32),
        compiler_params=_params(*(["parallel"] * (len(grid) - 1) + ["arbitrary"])),
    )(a, b)


def ffn_dw(h, da, act, du):
    S = h.shape[0]
    tk = min(DW_TK, S)
    dw_in = dw_matmul("ffn_dw_in", h, da,
                      pl.BlockSpec((tk, D), lambda n, k: (k, 0)),
                      pl.BlockSpec((None, tk, FSH), lambda n, k: (n // 2, k, n % 2)),
                      (N_CHIP, D, FSH), pl.BlockSpec((None, D, FSH), lambda n, k: (n, 0, 0)),
                      (N_CHIP, S // tk))
    dw_out = dw_matmul("ffn_dw_out", act, du,
                       pl.BlockSpec((tk, FSH), lambda n, k: (k, n)),
                       pl.BlockSpec((tk, D), lambda n, k: (k, 0)),
                       (DFF, D), pl.BlockSpec((FSH, D), lambda n, k: (n, 0)),
                       (2, S // tk))
    return dw_in, dw_out


def _halo_specs(tm, S):
    nb = tm // HALO
    last = S // HALO - 1
    return [pl.BlockSpec((HALO, D), lambda i: (jnp.maximum(i * nb - 1, 0), 0)),
            pl.BlockSpec((tm, D), lambda i: (i, 0)),
            pl.BlockSpec((HALO, D), lambda i: (jnp.minimum((i + 1) * nb, last), 0))]


def _shift_rows(v, k):
    return pltpu.roll(v, k % v.shape[0], 0)


def _window_sum(v, g, forward):
    acc = v + _shift_rows(v, 1 if forward else -1)
    for step in (1, 2, 4)[:g]:
        acc = _shift_rows(acc, step) + _shift_rows(acc, -step)
    return acc


def _pool_count(t, w, S):
    return jnp.maximum(jnp.minimum(t + w // 2, S) - jnp.maximum(t - w // 2, 0), 1).astype(F32)


def pool_fwd(x, vec, pw, pvec):
    S = x.shape[0]
    tm = min(ROW_TILE, S)
    G = D // 4

    def body(xp_ref, x_ref, xn_ref, vec_ref, pw_ref, pv_ref, xo_ref, y_ref, z_ref):
        i = pl.program_id(0)
        xa = jnp.concatenate([xp_ref[...], x_ref[...], xn_ref[...]], axis=0)
        t = i * tm - HALO + lax.broadcasted_iota(jnp.int32, (tm + 2 * HALO, 1), 0)
        h, _, _ = _prenorm(xa, vec_ref)
        h = jnp.where((t >= 0) & (t < S), h, 0.0)
        tmain = t[HALO:HALO + tm]
        for g in range(4):
            hg = h[:, g * G:(g + 1) * G]
            pooled = _window_sum(hg, g, True)[HALO:HALO + tm] / _pool_count(tmain, POOL_WINDOWS[g], S)
            z = (pooled - hg[HALO:HALO + tm]).astype(BF16)
            z_ref[:, g * G:(g + 1) * G] = z
            y_ref[:, g * G:(g + 1) * G] = _dot(z, pw_ref[g]) + pv_ref[0:1, g * G:(g + 1) * G]
        u = y_ref[...] * pv_ref[1:2, :]
        uhat, _ = _rms(u)
        xo_ref[...] = x_ref[...] + (1.0 + vec_ref[4:5, :]) * (uhat * vec_ref[1:2, :])

    row = lambda i: (i, 0)
    full = lambda i: (0, 0)
    return pl.pallas_call(
        body, name="pool_fwd", grid=(S // tm,),
        in_specs=_halo_specs(tm, S) + [pl.BlockSpec((8, D), full), pl.BlockSpec((4, G, G), lambda i: (0, 0, 0)),
                                       pl.BlockSpec((8, D), full)],
        out_specs=[pl.BlockSpec((tm, D), row)] * 3,
        out_shape=[jax.ShapeDtypeStruct((S, D), F32), jax.ShapeDtypeStruct((S, D), F32),
                   jax.ShapeDtypeStruct((S, D), BF16)],
        compiler_params=_params("parallel"),
    )(x, x, x, vec, pw, pvec)


def pool_bwd(dout, x, y, z, vec, pw, pvec):
    S = x.shape[0]
    tm = min(ROW_TILE, S)
    G = D // 4
    R = G // N_CHIP

    def body(dop_ref, do_ref, don_ref, yp_ref, y_ref, yn_ref, x_ref, z_ref, vec_ref, pw_ref, pv_ref,
             dx_ref, vg_ref, pg_ref, dw_ref, dh_ref):
        i = pl.program_id(0)

        @pl.when(i == 0)
        def _():
            vg_ref[...] = jnp.zeros_like(vg_ref)
            pg_ref[...] = jnp.zeros_like(pg_ref)
            dw_ref[...] = jnp.zeros_like(dw_ref)

        doa = jnp.concatenate([dop_ref[...], do_ref[...], don_ref[...]], axis=0)
        ya = jnp.concatenate([yp_ref[...], y_ref[...], yn_ref[...]], axis=0)
        t = i * tm - HALO + lax.broadcasted_iota(jnp.int32, (tm + 2 * HALO, 1), 0)
        inside = (t >= 0) & (t < S)
        main = (t >= i * tm) & (t < (i + 1) * tm)
        du, dgate_rows, dgpost_rows = _postnorm_bwd(doa, ya * pv_ref[1:2, :], vec_ref, 1.0)
        du = jnp.where(inside, du, 0.0)
        vg_ref[2:3, :] += jnp.sum(jnp.where(main, dgate_rows, 0.0), axis=0, keepdims=True)
        vg_ref[4:5, :] += jnp.sum(jnp.where(main, dgpost_rows, 0.0), axis=0, keepdims=True)
        dy = du * pv_ref[1:2, :]
        pg_ref[0:1, :] += jnp.sum(jnp.where(main, dy, 0.0), axis=0, keepdims=True)
        pg_ref[1:2, :] += jnp.sum(jnp.where(main, du * ya, 0.0), axis=0, keepdims=True)
        for g in range(4):
            dyg = dy[:, g * G:(g + 1) * G].astype(BF16)
            dz = _dot(dyg, pw_ref[g], NT)
            e = dz / _pool_count(t, POOL_WINDOWS[g], S)
            dh_ref[:, g * G:(g + 1) * G] = (_window_sum(e, g, False) - dz)[HALO:HALO + tm]
            dwg = _dot(z_ref[:, g * G:(g + 1) * G], dyg[HALO:HALO + tm], TN)
            for q in range(N_CHIP):
                dw_ref[q, g] += dwg[q * R:(q + 1) * R, :]
        dx_ref[...] = do_ref[...] + _prenorm_bwd(dh_ref[...], x_ref[...], vec_ref, vg_ref)

    row = lambda i: (i, 0)
    full = lambda i: (0, 0)
    halo = _halo_specs(tm, S)
    return pl.pallas_call(
        body, name="pool_bwd", grid=(S // tm,),
        in_specs=halo + halo + [pl.BlockSpec((tm, D), row), pl.BlockSpec((tm, D), row), pl.BlockSpec((8, D), full),
                                pl.BlockSpec((4, G, G), lambda i: (0, 0, 0)), pl.BlockSpec((8, D), full)],
        out_specs=[pl.BlockSpec((tm, D), row), pl.BlockSpec((8, D), full), pl.BlockSpec((8, D), full),
                   pl.BlockSpec((N_CHIP, 4, R, G), lambda i: (0, 0, 0, 0))],
        out_shape=[jax.ShapeDtypeStruct((S, D), F32), jax.ShapeDtypeStruct((8, D), F32),
                   jax.ShapeDtypeStruct((8, D), F32), jax.ShapeDtypeStruct((N_CHIP, 4, R, G), F32)],
        scratch_shapes=[pltpu.VMEM((tm, D), F32)],
        compiler_params=_params("arbitrary"),
    )(dout, dout, dout, y, y, y, x, z, vec, pw, pvec)


N_PAIR = N_HEADS // 2
SLOTS = LANES // ROPE
ROPE_ALL = N_HEADS * ROPE
NOPE_ALL = N_HEADS * NOPE
LAT_ALL = N_HEADS * KVL
DLAT = QL + KVL + 2 * LANES
DQ_ALL = NOPE_ALL + 2 * ROPE_ALL


def _w3(shape):
    return pl.BlockSpec(shape, lambda i: (0,) * len(shape))


def _slot_mask(hd, rows):
    lane = lax.broadcasted_iota(jnp.int32, (rows, LANES), 1)
    return (lane // ROPE) == (hd % SLOTS)


MLA_WEIGHTS = ("wq", "wkv", "wkr4", "wkrs4", "qn", "kvn", "wn", "wr", "wrs", "bduk")


def _mla_weight_specs():
    return [_w3((D, QL)), _w3((D, KVL)), _w3((D, LANES)), _w3((D, LANES)), _w3((1, QL)), _w3((1, KVL)),
            _w3((QL, NOPE_ALL)), _w3((QL, ROPE_ALL)), _w3((QL, ROPE_ALL)), _w3((N_PAIR, 2 * NOPE, 2 * KVL))]


def mla_pre(x, vec, mw, tabs):
    S = x.shape[0]
    tm = min(ROW_TILE, S)

    def body(x_ref, vec_ref, cos_ref, sin_ref, wq_ref, wkv_ref, wkr_ref, wkrs_ref, qn_ref, kvn_ref,
             wn_ref, wr_ref, wrs_ref, bduk_ref,
             h_ref, cq_ref, ckv_ref, cqn_ref, qnope_ref, qcat_ref, kcat_ref, vcat_ref):
        h, _, _ = _prenorm(x_ref[...], vec_ref)
        hb = h.astype(BF16)
        h_ref[...] = hb
        cq_raw = _dot(hb, wq_ref[...])
        ckv_raw = _dot(hb, wkv_ref[...])
        cq_ref[...] = cq_raw
        ckv_ref[...] = ckv_raw
        cos, sin = cos_ref[...], sin_ref[...]
        ckv = (_rms(ckv_raw)[0] * kvn_ref[...]).astype(BF16)
        kcat_ref[:, 0:KVL] = ckv
        kcat_ref[:, KVL:] = (_dot(hb, wkr_ref[...]) * cos + _dot(hb, wkrs_ref[...]) * sin).astype(BF16)
        vcat_ref[:, 0:KVL] = ckv
        ones = lax.broadcasted_iota(jnp.int32, (tm, QPAD - KVL), 1) == 0
        vcat_ref[:, KVL:] = jnp.where(ones, 1.0, 0.0).astype(BF16)
        cqb = (_rms(cq_raw)[0] * qn_ref[...]).astype(BF16)
        cqn_ref[...] = cqb
        qn = _dot(cqb, wn_ref[...]).astype(BF16)
        qnope_ref[...] = qn
        cos4, sin4 = jnp.tile(cos, (1, SLOTS)), jnp.tile(sin, (1, SLOTS))
        qr = ((_dot(cqb, wr_ref[...]) * cos4 + _dot(cqb, wrs_ref[...]) * sin4) * ATTN_SCALE).astype(BF16)
        for j in range(N_PAIR):
            ql = (_dot(qn[:, 2 * NOPE * j:2 * NOPE * (j + 1)], bduk_ref[j]) * ATTN_SCALE).astype(BF16)
            for hd in (2 * j, 2 * j + 1):
                qcat_ref[hd, :, 0:KVL] = ql[:, KVL * (hd - 2 * j):KVL * (hd - 2 * j + 1)]
                group = qr[:, LANES * (hd // SLOTS):LANES * (hd // SLOTS + 1)]
                qcat_ref[hd, :, KVL:] = jnp.where(_slot_mask(hd, tm), group, jnp.zeros_like(group))

    row = lambda i: (i, 0)
    hrow = lambda i: (0, i, 0)
    return pl.pallas_call(
        body, name="mla_pre", grid=(S // tm,),
        in_specs=[pl.BlockSpec((tm, D), row), _w3((8, D)), pl.BlockSpec((tm, LANES), row), pl.BlockSpec((tm, LANES), row)]
        + _mla_weight_specs(),
        out_specs=[pl.BlockSpec((tm, D), row), pl.BlockSpec((tm, QL), row), pl.BlockSpec((tm, KVL), row),
                   pl.BlockSpec((tm, QL), row), pl.BlockSpec((tm, NOPE_ALL), row),
                   pl.BlockSpec((N_HEADS, tm, QPAD), hrow), pl.BlockSpec((tm, QPAD), row),
                   pl.BlockSpec((tm, QPAD), row)],
        out_shape=[jax.ShapeDtypeStruct((S, D), BF16), jax.ShapeDtypeStruct((S, QL), F32),
                   jax.ShapeDtypeStruct((S, KVL), F32), jax.ShapeDtypeStruct((S, QL), BF16),
                   jax.ShapeDtypeStruct((S, NOPE_ALL), BF16), jax.ShapeDtypeStruct((N_HEADS, S, QPAD), BF16),
                   jax.ShapeDtypeStruct((S, QPAD), BF16), jax.ShapeDtypeStruct((S, QPAD), BF16)],
        compiler_params=_params("parallel"),
    )(x, vec, tabs[0], tabs[1], *[mw[k] for k in MLA_WEIGHTS])


def attn_fwd(qcat, kcat, vcat):
    S = kcat.shape[0]
    tq = min(ATTN_TQ, S)
    kc = min(ATTN_KC, S)

    def body(q_ref, k_ref, v_ref, o_ref, lse_ref):
        q = q_ref[...]
        m = jnp.full((tq, 1), -jnp.inf, F32)
        ov = jnp.zeros((tq, QPAD), F32)
        for c in range(S // kc):
            s = _dot(q, k_ref[c * kc:(c + 1) * kc, :], NT)
            m_new = jnp.maximum(m, jnp.max(s, axis=-1, keepdims=True))
            p = jnp.exp(s - m_new).astype(BF16)
            ov = ov * jnp.exp(m - m_new) + _dot(p, v_ref[c * kc:(c + 1) * kc, :])
            m = m_new
        l = ov[:, KVL:KVL + 1]
        o_ref[...] = (ov[:, 0:KVL] * (1.0 / l)).astype(BF16)
        lse_ref[...] = _as_row(m + jnp.log(l))

    return pl.pallas_call(
        body, name="attn_fwd", grid=(N_HEADS, S // tq),
        in_specs=[pl.BlockSpec((None, tq, QPAD), lambda h, i: (h, i, 0)),
                  pl.BlockSpec((S, QPAD), lambda h, i: (0, 0)),
                  pl.BlockSpec((S, QPAD), lambda h, i: (0, 0))],
        out_specs=[pl.BlockSpec((tq, KVL), lambda h, i: (i, h)),
                   pl.BlockSpec((None, 1, tq), lambda h, i: (h, 0, i))],
        out_shape=[jax.ShapeDtypeStruct((S, LAT_ALL), BF16), jax.ShapeDtypeStruct((N_HEADS, 1, S), F32)],
        compiler_params=_params("parallel", "parallel"),
    )(qcat, kcat, vcat)


def mla_post(olat, x, vec, bduv, wo):
    S = x.shape[0]
    tm = min(ROW_TILE, S)

    def body(o_ref, x_ref, vec_ref, bduv_ref, wo_ref, xo_ref, u_ref, ocat_ref):
        for j in range(N_PAIR):
            oc = _dot(o_ref[:, 2 * KVL * j:2 * KVL * (j + 1)], bduv_ref[j])
            ocat_ref[:, 2 * VH * j:2 * VH * (j + 1)] = oc.astype(BF16)
        u = _dot(ocat_ref[...], wo_ref[...])
        u_ref[...] = u
        uhat, _ = _rms(u)
        xo_ref[...] = x_ref[...] + (1.0 + vec_ref[4:5, :]) * (uhat * vec_ref[1:2, :])

    row = lambda i: (i, 0)
    return pl.pallas_call(
        body, name="mla_post", grid=(S // tm,),
        in_specs=[pl.BlockSpec((tm, LAT_ALL), row), pl.BlockSpec((tm, D), row), _w3((8, D)),
                  _w3((N_PAIR, 2 * KVL, 2 * VH)), _w3((D, D))],
        out_specs=[pl.BlockSpec((tm, D), row), pl.BlockSpec((tm, D), row), pl.BlockSpec((tm, D), row)],
        out_shape=[jax.ShapeDtypeStruct((S, D), F32), jax.ShapeDtypeStruct((S, D), F32),
                   jax.ShapeDtypeStruct((S, D), BF16)],
        compiler_params=_params("parallel"),
    )(olat, x, vec, bduv, wo)


def mla_post_bwd(dout, u, olat, vec, bduv, wo):
    S = u.shape[0]
    tm = min(ROW_TILE, S)

    def body(do_ref, u_ref, o_ref, vec_ref, bduv_ref, wo_ref, du_ref, docat_ref, dolat_ref, delta_ref, vg_ref):
        @pl.when(pl.program_id(0) == 0)
        def _():
            vg_ref[...] = jnp.zeros_like(vg_ref)

        du, dgate_rows, dgpost_rows = _postnorm_bwd(do_ref[...], u_ref[...], vec_ref, 1.0)
        vg_ref[2:3, :] += jnp.sum(dgate_rows, axis=0, keepdims=True)
        vg_ref[4:5, :] += jnp.sum(dgpost_rows, axis=0, keepdims=True)
        dub = du.astype(BF16)
        du_ref[...] = dub
        docat_ref[...] = _dot(dub, wo_ref[...], NT).astype(BF16)
        for j in range(N_PAIR):
            dol = _dot(docat_ref[:, 2 * VH * j:2 * VH * (j + 1)], bduv_ref[j], NT).astype(BF16)
            dolat_ref[:, 2 * KVL * j:2 * KVL * (j + 1)] = dol
            prod = dol.astype(F32) * o_ref[:, 2 * KVL * j:2 * KVL * (j + 1)].astype(F32)
            delta_ref[2 * j] = _as_row(jnp.sum(prod[:, 0:KVL], axis=-1, keepdims=True))
            delta_ref[2 * j + 1] = _as_row(jnp.sum(prod[:, KVL:], axis=-1, keepdims=True))

    row = lambda i: (i, 0)
    hrow = lambda i: (0, i, 0)
    return pl.pallas_call(
        body, name="mla_post_bwd", grid=(S // tm,),
        in_specs=[pl.BlockSpec((tm, D), row), pl.BlockSpec((tm, D), row), pl.BlockSpec((tm, LAT_ALL), row),
                  _w3((8, D)), _w3((N_PAIR, 2 * KVL, 2 * VH)), _w3((D, D))],
        out_specs=[pl.BlockSpec((tm, D), row), pl.BlockSpec((tm, D), row),
                   pl.BlockSpec((tm, LAT_ALL), row), pl.BlockSpec((N_HEADS, 1, tm), lambda i: (0, 0, i)), _w3((8, D))],
        out_shape=[jax.ShapeDtypeStruct((S, D), BF16), jax.ShapeDtypeStruct((S, D), BF16),
                   jax.ShapeDtypeStruct((S, LAT_ALL), BF16), jax.ShapeDtypeStruct((N_HEADS, 1, S), F32),
                   jax.ShapeDtypeStruct((8, D), F32)],
        compiler_params=_params("arbitrary"),
    )(dout, u, olat, vec, bduv, wo)


def attn_bwd(qcat, kcat, kcat_t, dolat, lse_row, delta_row):
    S = kcat.shape[0]
    tq = min(2 * ATTN_TQ, S)
    kc = min(2 * ATTN_KC, S)

    def body(q_ref, k_ref, kt_ref, do_ref, lse_ref, dl_ref, dq_ref, dk_ref, dv_ref):
        @pl.when((pl.program_id(0) == 0) & (pl.program_id(1) == 0))
        def _():
            dk_ref[...] = jnp.zeros_like(dk_ref)
            dv_ref[...] = jnp.zeros_like(dv_ref)

        q, do = q_ref[...], do_ref[...]
        lse, dl = lse_ref[...], dl_ref[...]
        dqt = jnp.zeros((QPAD, tq), F32)
        for c in range(S // kc):
            rows = slice(c * kc, (c + 1) * kc)
            st = _dot(k_ref[rows, :], q, NT)
            pt = jnp.exp(st - lse)
            dpt = _dot(k_ref[rows, 0:KVL], do, NT)
            dst = (pt * (dpt - dl)).astype(BF16)
            dv_ref[rows, :] += _dot(pt.astype(BF16), do)
            dk_ref[rows, :] += _dot(dst, q)
            dqt = dqt + _dot(kt_ref[:, rows], dst)
        dq_ref[...] = (dqt.T * ATTN_SCALE).astype(BF16)

    return pl.pallas_call(
        body, name="attn_bwd", grid=(N_HEADS, S // tq),
        in_specs=[pl.BlockSpec((None, tq, QPAD), lambda h, i: (h, i, 0)),
                  pl.BlockSpec((S, QPAD), lambda h, i: (0, 0)),
                  pl.BlockSpec((QPAD, S), lambda h, i: (0, 0)),
                  pl.BlockSpec((tq, KVL), lambda h, i: (i, h)),
                  pl.BlockSpec((None, 1, tq), lambda h, i: (h, 0, i)),
                  pl.BlockSpec((None, 1, tq), lambda h, i: (h, 0, i))],
        out_specs=[pl.BlockSpec((None, tq, QPAD), lambda h, i: (h, i, 0)),
                   pl.BlockSpec((S, QPAD), lambda h, i: (0, 0)),
                   pl.BlockSpec((S, KVL), lambda h, i: (0, 0))],
        out_shape=[jax.ShapeDtypeStruct((N_HEADS, S, QPAD), BF16), jax.ShapeDtypeStruct((S, QPAD), F32),
                   jax.ShapeDtypeStruct((S, KVL), F32)],
        compiler_params=_params("arbitrary", "arbitrary"),
    )(qcat, kcat, kcat_t, dolat, lse_row, delta_row)


def mla_pre_bwd(dout, dq, dk, dv, x, cq_raw, ckv_raw, vec, mw, tabs):
    S = x.shape[0]
    tm = min(ROW_TILE, S)

    def body(do_ref, dq_ref, dk_ref, dv_ref, x_ref, cq_ref, ckv_ref, vec_ref, cos_ref, sin_ref,
             wq_ref, wkv_ref, wkr_ref, wkrs_ref, qn_ref, kvn_ref, wn_ref, wr_ref, wrs_ref, bduk_ref,
             dx_ref, dlat_ref, dql_ref, dqcat_ref, vg_ref, ng_ref):
        @pl.when(pl.program_id(0) == 0)
        def _():
            vg_ref[...] = jnp.zeros_like(vg_ref)
            ng_ref[...] = jnp.zeros_like(ng_ref)

        cos, sin = cos_ref[...], sin_ref[...]
        for j in range(N_PAIR):
            dql = jnp.concatenate([dq_ref[2 * j, :, 0:KVL], dq_ref[2 * j + 1, :, 0:KVL]], axis=1)
            dql_ref[:, 2 * KVL * j:2 * KVL * (j + 1)] = dql
            dqcat_ref[:, 2 * NOPE * j:2 * NOPE * (j + 1)] = _dot(dql, bduk_ref[j], NT).astype(BF16)
        groups = []
        for grp in range(N_HEADS // SLOTS):
            acc = jnp.zeros((tm, LANES), F32)
            for hd in range(SLOTS * grp, SLOTS * (grp + 1)):
                acc = acc + jnp.where(_slot_mask(hd, tm), dq_ref[hd, :, KVL:].astype(F32), 0.0)
            groups.append(acc)
        dqr = jnp.concatenate(groups, axis=1)
        qa = (dqr * jnp.tile(cos, (1, SLOTS))).astype(BF16)
        qb = (dqr * jnp.tile(sin, (1, SLOTS))).astype(BF16)
        dqcat_ref[:, NOPE_ALL:NOPE_ALL + ROPE_ALL] = qa
        dqcat_ref[:, NOPE_ALL + ROPE_ALL:] = qb
        dcq = _dot(dqcat_ref[:, 0:NOPE_ALL], wn_ref[...], NT) + _dot(qa, wr_ref[...], NT) + _dot(qb, wrs_ref[...], NT)
        cqh, rq = _rms(cq_ref[...])
        ng_ref[0:1, :] += jnp.sum(dcq * cqh, axis=0, keepdims=True)
        dcq_raw = _rms_bwd(cqh, rq, dcq * qn_ref[...]).astype(BF16)
        dckv = dk_ref[:, 0:KVL] + dv_ref[...]
        ckvh, rk = _rms(ckv_ref[...])
        ng_ref[1:2, 0:KVL] += jnp.sum(dckv * ckvh, axis=0, keepdims=True)
        dckv_raw = _rms_bwd(ckvh, rk, dckv * kvn_ref[...]).astype(BF16)
        dkr = dk_ref[:, KVL:]
        ka = (dkr * cos).astype(BF16)
        kb = (dkr * sin).astype(BF16)
        dlat_ref[:, 0:QL] = dcq_raw
        dlat_ref[:, QL:QL + KVL] = dckv_raw
        dlat_ref[:, QL + KVL:QL + KVL + LANES] = ka
        dlat_ref[:, QL + KVL + LANES:] = kb
        dh = (_dot(dcq_raw, wq_ref[...], NT) + _dot(dckv_raw, wkv_ref[...], NT)
              + _dot(ka, wkr_ref[...], NT) + _dot(kb, wkrs_ref[...], NT))
        dx_ref[...] = do_ref[...] + _prenorm_bwd(dh, x_ref[...], vec_ref, vg_ref)

    row = lambda i: (i, 0)
    hrow = lambda i: (0, i, 0)
    return pl.pallas_call(
        body, name="mla_pre_bwd", grid=(S // tm,),
        in_specs=[pl.BlockSpec((tm, D), row), pl.BlockSpec((N_HEADS, tm, QPAD), hrow), pl.BlockSpec((tm, QPAD), row),
                  pl.BlockSpec((tm, KVL), row), pl.BlockSpec((tm, D), row), pl.BlockSpec((tm, QL), row),
                  pl.BlockSpec((tm, KVL), row), _w3((8, D)), pl.BlockSpec((tm, LANES), row), pl.BlockSpec((tm, LANES), row)]
        + _mla_weight_specs(),
        out_specs=[pl.BlockSpec((tm, D), row), pl.BlockSpec((tm, DLAT), row), pl.BlockSpec((tm, LAT_ALL), row),
                   pl.BlockSpec((tm, DQ_ALL), row), _w3((8, D)), _w3((8, QL))],
        out_shape=[jax.ShapeDtypeStruct((S, D), F32), jax.ShapeDtypeStruct((S, DLAT), BF16),
                   jax.ShapeDtypeStruct((S, LAT_ALL), BF16), jax.ShapeDtypeStruct((S, DQ_ALL), BF16),
                   jax.ShapeDtypeStruct((8, D), F32), jax.ShapeDtypeStruct((8, QL), F32)],
        compiler_params=_params("arbitrary"),
    )(dout, dq, dk, dv, x, cq_raw, ckv_raw, vec, tabs[0], tabs[1], *[mw[k] for k in MLA_WEIGHTS])


def mla_dw(h, dlat, cqn, dqcat, dql, qnope, olat, docat, ocat, du):
    S = h.shape[0]
    tk = min(DW_TK, S)
    nk = S // tk
    flat = lambda w: pl.BlockSpec((tk, w), lambda k: (k, 0))
    cols = lambda w: pl.BlockSpec((tk, w), lambda n, k: (k, n))
    pair_o = pl.BlockSpec((None, 2 * KVL, 2 * NOPE), lambda n, k: (n, 0, 0))
    g = {}
    g["in"] = dw_matmul("mla_dw_in", h, dlat, flat(D), flat(DLAT), (D, DLAT),
                        pl.BlockSpec((D, DLAT), lambda k: (0, 0)), (nk,))
    g["q"] = dw_matmul("mla_dw_q", cqn, dqcat, flat(QL), flat(DQ_ALL), (QL, DQ_ALL),
                       pl.BlockSpec((QL, DQ_ALL), lambda k: (0, 0)), (nk,))
    g["uk"] = dw_matmul("mla_dw_uk", dql, qnope, cols(2 * KVL), cols(2 * NOPE), (N_PAIR, 2 * KVL, 2 * NOPE), pair_o,
                        (N_PAIR, nk))
    g["uv"] = dw_matmul("mla_dw_uv", olat, docat, cols(2 * KVL), cols(2 * VH), (N_PAIR, 2 * KVL, 2 * VH), pair_o,
                        (N_PAIR, nk))
    g["o"] = dw_matmul("mla_dw_o", ocat, du, cols(256), pl.BlockSpec((tk, D), lambda n, k: (k, 0)), (D, D),
                       pl.BlockSpec((256, D), lambda n, k: (n, 0)), (D // 256, nk))
    return g


MOD_COLS = 9 * D // N_CHIP


def mod_fwd(c_pad, ada_w, ada_b_loc):
    tn = MOD_COLS // 3

    def body(c_ref, w_ref, b_ref, o_ref):
        c = c_ref[...]
        sc = (c * jax.nn.sigmoid(c)).astype(BF16)
        o_ref[...] = _dot(sc, w_ref[...].astype(BF16)) + b_ref[...]

    return pl.pallas_call(
        body, name="mod_fwd", grid=(2, 3),
        in_specs=[pl.BlockSpec((16, D), lambda i, n: (0, 0)), pl.BlockSpec((None, D, tn), lambda i, n: (i, 0, n)),
                  pl.BlockSpec((None, 1, tn), lambda i, n: (i, 0, n))],
        out_specs=pl.BlockSpec((None, 16, tn), lambda i, n: (i, 0, n)),
        out_shape=jax.ShapeDtypeStruct((2, 16, MOD_COLS), F32),
        compiler_params=_params("parallel", "parallel"),
    )(c_pad, ada_w, ada_b_loc)


def _adamw_math(w, g, m, v):
    m = ADAM_B1 * m + (1.0 - ADAM_B1) * g
    v = ADAM_B2 * v + (1.0 - ADAM_B2) * (g * g)
    m_hat = m / (1.0 - ADAM_B1 ** ADAM_STEP)
    v_hat = v / (1.0 - ADAM_B2 ** ADAM_STEP)
    delta = -ADAM_LR * (m_hat / (jnp.sqrt(v_hat) + ADAM_EPS) + ADAM_WD * w)
    return delta, m, v


def adamw_whole(name, items):
    n = len(items)

    def body(*refs):
        for t in range(n):
            w_ref, g_ref, m_ref, v_ref = refs[4 * t:4 * t + 4]
            d_ref, mo_ref, vo_ref = refs[4 * n + 3 * t:4 * n + 3 * t + 3]
            d_ref[...], mo_ref[...], vo_ref[...] = _adamw_math(w_ref[...], g_ref[...], m_ref[...], v_ref[...])

    whole = lambda shape: pl.BlockSpec(shape, lambda i: (0,) * len(shape))
    outs = pl.pallas_call(
        body, name=name, grid=(1,),
        in_specs=[whole(a.shape) for item in items for a in item],
        out_specs=[whole(item[0].shape) for item in items for _ in range(3)],
        out_shape=[jax.ShapeDtypeStruct(item[0].shape, F32) for item in items for _ in range(3)],
        compiler_params=_params("arbitrary"),
    )(*[a for item in items for a in item])
    return [list(outs[3 * t:3 * t + 3]) for t in range(n)]


def adamw(name, w, g, m, v, part=None, prev=None, copy_grad=False):
    shape = w.shape
    if part is None and w.size * 4 <= (1 << 20):
        return adamw_whole(name, [(w, g, m, v)])[0]
    cols = shape[-1]
    rows = w.size // cols
    per_entry = rows // shape[0] if part is not None else rows
    tr = per_entry
    budget_rows = (2 << 20) // (cols * 4)
    for cand in range(min(per_entry, budget_rows) // 8 * 8, 0, -8):
        if per_entry % cand == 0:
            tr = cand
            break
    first, count = part if part is not None else (0, 1)
    tiles = per_entry // tr

    n_out = 4 if copy_grad else 3

    def body(w_ref, g_ref, m_ref, v_ref, *rest):
        outs = rest[-n_out:]
        outs[0][...], outs[1][...], outs[2][...] = _adamw_math(w_ref[...], g_ref[...], m_ref[...], v_ref[...])
        if copy_grad:
            outs[3][...] = g_ref[...]

    spec = pl.BlockSpec((tr, cols), lambda i: (i + first * tiles, 0))
    operands = [a.reshape(rows, cols) for a in (w, g, m, v)]
    aliases = {}
    if prev is not None:
        operands += [p.reshape(rows, cols) for p in prev]
        aliases = {4 + t: t for t in range(n_out)}
    outs = pl.pallas_call(
        body, name=name, grid=(count * tiles,), in_specs=[spec] * 4 + [_ANY] * (len(operands) - 4),
        out_specs=[spec] * n_out, out_shape=[jax.ShapeDtypeStruct((rows, cols), F32)] * n_out,
        input_output_aliases=aliases, compiler_params=_params("parallel"),
    )(*operands)
    return [o.reshape(shape) for o in outs]


def adamw_ada(c_pad, dmod, w, m, v):
    tr = 256

    def body(c_ref, dm_ref, w_ref, m_ref, v_ref, g_ref, d_ref, mo_ref, vo_ref):
        c = c_ref[...]
        sc = (c * jax.nn.sigmoid(c)).astype(BF16)
        g = _dot(sc, dm_ref[...].astype(BF16), TN)
        g_ref[...] = g
        d_ref[...], mo_ref[...], vo_ref[...] = _adamw_math(w_ref[...], g, m_ref[...], v_ref[...])

    wspec = pl.BlockSpec((None, tr, MOD_COLS), lambda i, r: (i, r, 0))
    return pl.pallas_call(
        body, name="adamw_ada", grid=(2, D // tr),
        in_specs=[pl.BlockSpec((16, tr), lambda i, r: (0, r)),
                  pl.BlockSpec((None, 16, MOD_COLS), lambda i, r: (i, 0, 0)), wspec, wspec, wspec],
        out_specs=[wspec] * 4,
        out_shape=[jax.ShapeDtypeStruct((2, D, MOD_COLS), F32)] * 4,
        compiler_params=_params("parallel", "parallel"),
    )(c_pad, dmod, w, m, v)


def sum_devices(name, a):
    _, R, C = a.shape
    tr = R
    for cand in (64, 32, 16, 8):
        if R % cand == 0:
            tr = cand
            break

    def body(a_ref, o_ref):
        acc = a_ref[0]
        for dev in range(1, N_DEV):
            acc = acc + a_ref[dev]
        o_ref[...] = acc

    return pl.pallas_call(
        body, name=name, grid=(R // tr,),
        in_specs=[pl.BlockSpec((N_DEV, tr, C), lambda i: (0, i, 0))],
        out_specs=pl.BlockSpec((tr, C), lambda i: (i, 0)),
        out_shape=jax.ShapeDtypeStruct((R, C), F32),
        compiler_params=_params("parallel"),
    )(a)


def _place():
    return lax.axis_index("x"), lax.axis_index("y"), lax.axis_index("c")


def _other_chips(x, y):
    return [(1 - x, y), (x, 1 - y), (1 - x, 1 - y)]


def gather_devices(name, a):
    m_per, n = a.shape

    def body(x_ref, out_ref, send_sems, recv_sems, local_sem):
        x, y, c = _place()
        me, sibling = (x, y, c), (x, y, 1 - c)
        chips = _other_chips(x, y)

        def rows(px, py, pc):
            return out_ref.at[pl.ds((4 * px + 2 * py + pc) * m_per, m_per), :]

        def copy(k, block, to, src=None):
            return pltpu.make_async_remote_copy(
                src_ref=rows(*block) if src is None else src, dst_ref=rows(*block),
                send_sem=send_sems.at[k], recv_sem=recv_sems.at[k], device_id=to, device_id_type=MESH)

        mine = pltpu.make_async_copy(x_ref, rows(*me), local_sem)
        mine.start()
        first = [copy(0, me, sibling, src=x_ref)]
        first += [copy(1 + j, me, (*chip, c), src=x_ref) for j, chip in enumerate(chips)]
        for cp in first:
            cp.start()
        passed = [copy(4 + j, (*chip, c), sibling) for j, chip in enumerate(chips)]
        for j, chip in enumerate(chips):
            copy(1 + j, (*chip, c), me).wait_recv()
            passed[j].start()
        copy(0, sibling, me).wait_recv()
        for j, chip in enumerate(chips):
            copy(4 + j, (*chip, 1 - c), me).wait_recv()
        for cp in first + passed:
            cp.wait_send()
        mine.wait()

    out = pl.pallas_call(
        body, name=name,
        out_shape=jax.ShapeDtypeStruct((N_DEV * m_per, n), a.dtype),
        in_specs=[pl.BlockSpec(memory_space=pltpu.VMEM)],
        out_specs=pl.BlockSpec(memory_space=pltpu.VMEM),
        scratch_shapes=[pltpu.SemaphoreType.DMA((7,)), pltpu.SemaphoreType.DMA((7,)), pltpu.SemaphoreType.DMA],
        compiler_params=pltpu.CompilerParams(vmem_limit_bytes=VMEM_LIMIT),
    )(a)
    return out.reshape(N_DEV, m_per, n)


_ANY = pl.BlockSpec(memory_space=pl.ANY)


def _hbm_ref(a):
    return jax.new_ref(a, memory_space=pltpu.MemorySpace.HBM)


def _hbm_empty(shape, dtype):
    return jax.empty_ref(jax.ShapeDtypeStruct(shape, dtype), memory_space=pltpu.MemorySpace.HBM)


ID_PAIR, ID_CHIPS, ID_SHARE, ID_UKV = 8, 9, 10, 11


def _sequencer(name, collective_id, n_sem, peers_of, program):
    sems = pltpu.SemaphoreType.DMA((n_sem,))

    @pl.kernel(mesh=plsc.ScalarSubcoreMesh(axis_name="seq", num_cores=1), name=name, scratch_types=[sems, sems],
               compiler_params=pltpu.CompilerParams(collective_id=collective_id))
    def launch(send_sem, recv_sem):
        x, y, c = _place()
        peers = peers_of(x, y, c)
        barrier = pltpu.get_barrier_semaphore()
        for peer in peers:
            pl.semaphore_signal(barrier, inc=1, device_id=peer, device_id_type=MESH)
        pl.semaphore_wait(barrier, len(peers))
        program(x, y, c, send_sem, recv_sem)

    launch()


def gather_weights(name, stage, arrays):
    n = len(arrays)
    refs = [_hbm_ref(a) for a in arrays]

    def program(x, y, c, send_sem, recv_sem):
        me = 2 * x + y
        chips = _other_chips(x, y)

        def ici(t, r, half):
            cx, cy = chips[r]
            mine = refs[t].at[me, half]
            return pltpu.make_async_remote_copy(
                src_ref=mine, dst_ref=mine, send_sem=send_sem.at[3 * t + r], recv_sem=recv_sem.at[3 * t + r],
                device_id=(cx, cy, c), device_id_type=MESH)

        def d2d(t, r, half):
            cx, cy = chips[r]
            there = refs[t].at[2 * cx + cy, half]
            k = 3 * n + 3 * t + r
            return pltpu.make_async_remote_copy(
                src_ref=there, dst_ref=there, send_sem=send_sem.at[k], recv_sem=recv_sem.at[k],
                device_id=(x, y, 1 - c), device_id_type=MESH)

        for t in range(n):
            for r in range(3):
                ici(t, r, c).start()
        for t in range(n):
            for r in range(3):
                ici(t, r, c).wait_recv()
                d2d(t, r, c).start()
        for t in range(n):
            for r in range(3):
                d2d(t, r, 1 - c).wait_recv()
        for t in range(n):
            for r in range(3):
                ici(t, r, c).wait_send()
                d2d(t, r, c).wait_send()

    _sequencer(name, stage, 6 * n, lambda x, y, c: [(x, y, 1 - c)] + [(cx, cy, c) for cx, cy in _other_chips(x, y)],
               program)
    return [r[...] for r in refs]


def cast_into_slots(name, chip, shards, after=None):
    steps = 2
    n = len(shards)

    def body(chip_ref, *refs):
        for src, dst in zip(refs[:n], refs[-n - 1:-1]):
            dst[...] = src[...].astype(BF16)
        refs[-1][...] = jnp.zeros_like(refs[-1])

    token_spec = pl.BlockSpec((SUBLANES, LANES), lambda h, i, chip_ref: (0, 0))

    def spec_in(a, prefix):
        R, C = a.shape[-2:]
        return pl.BlockSpec((None,) * (len(prefix) + 1) + (R // steps, C), lambda h, i, chip_ref: prefix + (h, i, 0))

    def spec_out(a):
        R, C = a.shape[-2:]
        return pl.BlockSpec((None, None, R // steps, C), lambda h, i, chip_ref: (chip_ref[0], h, i, 0))

    outs = pl.pallas_call(
        body, name=name,
        grid_spec=pltpu.PrefetchScalarGridSpec(
            num_scalar_prefetch=1, grid=(2, steps),
            in_specs=[spec_in(a, p) for a, p in shards] + ([token_spec] if after is not None else []),
            out_specs=[spec_out(a) for a, _ in shards] + [token_spec]),
        out_shape=[jax.ShapeDtypeStruct((N_CHIP, 2) + a.shape[-2:], BF16) for a, _ in shards]
        + [jax.ShapeDtypeStruct((SUBLANES, LANES), F32)],
        compiler_params=_params("arbitrary", "arbitrary"),
    )(chip, *[a for a, _ in shards], *([after] if after is not None else []))
    return outs[:-1], outs[-1]


def reduce_pair(name, grads):
    n = len(grads)
    src = [_hbm_ref(g) for g in grads]
    dst = [_hbm_empty((N_CHIP,) + g.shape[2:], g.dtype) for g in grads]

    def program(x, y, c, send_sem, recv_sem):
        cps = [pltpu.make_async_remote_copy(
            src_ref=src[t].at[:, 1 - c], dst_ref=dst[t], send_sem=send_sem.at[t], recv_sem=recv_sem.at[t],
            device_id=(x, y, 1 - c), device_id_type=MESH) for t in range(n)]
        for cp in cps:
            cp.start()
        for cp in cps:
            cp.wait()

    _sequencer(name, ID_PAIR, n, lambda x, y, c: [(x, y, 1 - c)], program)
    return [r[...] for r in src], [r[...] for r in dst]


def pair_add(name, core, gs, gots):
    n = len(gs)

    def body(core_ref, *refs):
        for t in range(n):
            refs[2 * n + t][...] = (refs[2 * t][...] + refs[2 * t + 1][...]).astype(BF16)

    in_specs, out_specs, out_shape = [], [], []
    for g in gs:
        _, _, R, C = g.shape
        in_specs += [pl.BlockSpec((None, None, R, C), lambda q, core_ref: (q, core_ref[0], 0, 0)),
                     pl.BlockSpec((None, R, C), lambda q, core_ref: (q, 0, 0))]
        out_specs.append(pl.BlockSpec((None, R, C), lambda q, core_ref: (q, 0, 0)))
        out_shape.append(jax.ShapeDtypeStruct((N_CHIP, R, C), BF16))
    return pl.pallas_call(
        body, name=name,
        grid_spec=pltpu.PrefetchScalarGridSpec(num_scalar_prefetch=1, grid=(N_CHIP,), in_specs=in_specs,
                                               out_specs=out_specs),
        out_shape=out_shape, compiler_params=_params("parallel"),
    )(core, *[a for pair in zip(gs, gots) for a in pair])


def reduce_chips(name, sums):
    n = len(sums)
    src = [_hbm_ref(s) for s in sums]
    dst = [_hbm_empty((3,) + s.shape[1:], s.dtype) for s in sums]

    def program(x, y, c, send_sem, recv_sem):
        cps = []
        for t in range(n):
            for r, (cx, cy) in enumerate(_other_chips(x, y)):
                cps.append(pltpu.make_async_remote_copy(
                    src_ref=src[t].at[2 * cx + cy], dst_ref=dst[t].at[r],
                    send_sem=send_sem.at[3 * t + r], recv_sem=recv_sem.at[3 * t + r],
                    device_id=(cx, cy, c), device_id_type=MESH))
        for cp in cps:
            cp.start()
        for cp in cps:
            cp.wait()

    _sequencer(name, ID_CHIPS, 3 * n, lambda x, y, c: [(cx, cy, c) for cx, cy in _other_chips(x, y)], program)
    return [r[...] for r in src], [r[...] for r in dst]


def chip_add(name, place, items, after=None):
    n = len(items)

    def body(place_ref, *refs):
        for t in range(n):
            s_ref, got_ref, o_ref = refs[2 * t], refs[2 * t + 1], refs[len(refs) - n + t]
            o_ref[...] = ((s_ref[...].astype(F32) + got_ref[0].astype(F32)) + got_ref[1].astype(F32)) + got_ref[2].astype(F32)

    in_specs, args, out_specs, out_shape, aliases = [], [place], [], [], {}
    for s, got, k, n_slots, _ in items:
        _, R, C = s.shape
        in_specs += [pl.BlockSpec((None, R, C), lambda i, place_ref: (place_ref[0], 0, 0)),
                     pl.BlockSpec((3, R, C), lambda i, place_ref: (0, 0, 0))]
        args += [s, got]
        out_specs.append(pl.BlockSpec((None, None, R, C), lambda i, place_ref, k=k: (k, place_ref[1], 0, 0)))
        out_shape.append(jax.ShapeDtypeStruct((n_slots, 2, R, C), F32))
    for t, (*_, prev) in enumerate(items):
        if prev is not None:
            aliases[len(args)] = t
            in_specs.append(_ANY)
            args.append(prev)
    for piece in after or ():
        in_specs.append(pl.BlockSpec((SUBLANES, LANES), lambda i, place_ref: (0, 0)))
        args.append(piece)
    return pl.pallas_call(
        body, name=name,
        grid_spec=pltpu.PrefetchScalarGridSpec(num_scalar_prefetch=1, grid=(1,), in_specs=in_specs,
                                               out_specs=out_specs),
        out_shape=out_shape, input_output_aliases=aliases, compiler_params=_params("arbitrary"),
    )(*args)


def share_halves(name, stacks, slots):
    n = len(stacks)
    dst = [_hbm_ref(s) for s in stacks]

    def program(x, y, c, send_sem, recv_sem):
        cps = [pltpu.make_async_remote_copy(
            src_ref=dst[t].at[slots[t], c], dst_ref=dst[t].at[slots[t], c],
            send_sem=send_sem.at[t], recv_sem=recv_sem.at[t],
            device_id=(x, y, 1 - c), device_id_type=MESH) for t in range(n)]
        for cp in cps:
            cp.start()
        for cp in cps:
            cp.wait()

    _sequencer(name, ID_SHARE, n, lambda x, y, c: [(x, y, 1 - c)], program)
    return [r[...] for r in dst]


def gather_blocks(name, slotted):
    out = _hbm_ref(slotted)

    def program(x, y, c, send_sem, recv_sem):
        sibling = (x, y, 1 - c)
        chips = _other_chips(x, y)

        def copy(k, px, py, pc, to):
            block = out.at[4 * px + 2 * py + pc]
            return pltpu.make_async_remote_copy(src_ref=block, dst_ref=block, send_sem=send_sem.at[k],
                                                recv_sem=recv_sem.at[k], device_id=to, device_id_type=MESH)

        first = [copy(0, x, y, c, sibling)] + [copy(1 + j, x, y, c, (cx, cy, c)) for j, (cx, cy) in enumerate(chips)]
        for cp in first:
            cp.start()
        passed = [copy(4 + j, cx, cy, c, sibling) for j, (cx, cy) in enumerate(chips)]
        for j, (cx, cy) in enumerate(chips):
            copy(1 + j, cx, cy, c, (x, y, c)).wait_recv()
            passed[j].start()
        copy(0, x, y, 1 - c, (x, y, c)).wait_recv()
        for j, (cx, cy) in enumerate(chips):
            copy(4 + j, cx, cy, 1 - c, (x, y, c)).wait_recv()
        for cp in first + passed:
            cp.wait_send()

    _sequencer(name, ID_UKV, 7, lambda x, y, c: [(x, y, 1 - c)] + [(cx, cy, c) for cx, cy in _other_chips(x, y)],
               program)
    return out[...]


def place_block(name, dev, a):
    M, N = a.shape
    tr = min(M, 64)

    def body(dev_ref, a_ref, o_ref):
        o_ref[...] = a_ref[...]

    return pl.pallas_call(
        body, name=name,
        grid_spec=pltpu.PrefetchScalarGridSpec(
            num_scalar_prefetch=1, grid=(M // tr,),
            in_specs=[pl.BlockSpec((tr, N), lambda i, dev_ref: (i, 0))],
            out_specs=pl.BlockSpec((None, tr, N), lambda i, dev_ref: (dev_ref[0], i, 0))),
        out_shape=jax.ShapeDtypeStruct((N_DEV, M, N), a.dtype),
        compiler_params=_params("parallel"),
    )(dev, a)


def _swap_rope(a):
    return jnp.concatenate([a[..., ROPE // 2:], a[..., :ROPE // 2]], axis=-1)


def _rope_tables(S):
    inv = 1.0 / (ROPE_THETA ** (jnp.arange(0, ROPE, 2, dtype=F32) / ROPE))
    ang = jnp.arange(S, dtype=F32)[:, None] * inv[None, :]
    cos, sin = jnp.cos(ang), jnp.sin(ang)
    return (jnp.tile(jnp.concatenate([cos, cos], axis=1), (1, SLOTS)),
            jnp.tile(jnp.concatenate([-sin, sin], axis=1), (1, SLOTS)))


def _vec(norm_g, mod, i, k):
    rows = [norm_g[i, 2 * k], norm_g[i, 2 * k + 1], mod[i, 3 * k], mod[i, 3 * k + 1], mod[i, 3 * k + 2]]
    return jnp.concatenate([jnp.stack(rows), jnp.zeros((3, D), F32)], axis=0)


def _unpack_weights(full, w_uk, w_uv, q_norm, kv_norm):
    G = D // 4
    ffn_in = [[full[2 * i + k].reshape(N_CHIP, D, FSH) for k in range(2)] for i in range(2)]
    ffn_out = [[full[4 + 2 * i + k].reshape(2, FSH, D) for k in range(2)] for i in range(2)]
    pw = full[8].reshape(N_CHIP, 4, G // N_CHIP, G).transpose(1, 0, 2, 3).reshape(4, G, G)
    w_in = full[9].reshape(D, QL + KVL + ROPE)
    w_uq = full[10].reshape(QL, N_HEADS, NOPE + ROPE)
    wkr = w_in[:, QL + KVL:]
    wr = w_uq[:, :, NOPE:]
    eye2 = jnp.eye(2, dtype=BF16)
    uk_t = jnp.transpose(w_uk, (1, 2, 0)).reshape(N_PAIR, 2, NOPE, KVL)
    bduk = jnp.einsum("janc,ab->janbc", uk_t, eye2).reshape(N_PAIR, 2 * NOPE, 2 * KVL)
    uv = jnp.transpose(w_uv, (1, 0, 2)).reshape(N_PAIR, 2, KVL, VH)
    bduv = jnp.einsum("jacn,ab->jacbn", uv, eye2).reshape(N_PAIR, 2 * KVL, 2 * VH)
    mw = dict(wq=w_in[:, :QL], wkv=w_in[:, QL:QL + KVL], wkr4=jnp.tile(wkr, (1, SLOTS)),
              wkrs4=jnp.tile(_swap_rope(wkr), (1, SLOTS)), qn=q_norm, kvn=kv_norm,
              wn=w_uq[:, :, :NOPE].reshape(QL, NOPE_ALL), wr=wr.reshape(QL, ROPE_ALL),
              wrs=_swap_rope(wr).reshape(QL, ROPE_ALL), bduk=bduk)
    return ffn_in, ffn_out, pw, mw, bduv, full[11].reshape(D, D)


def _example_step(x, target, mod, norm_g, pvec, ffn_in, ffn_out, pw, mw, bduv, wo, reducer):
    S = x.shape[0]
    tabs = _rope_tables(S)
    vec = [[_vec(norm_g, mod, i, k) for k in range(3)] for i in range(2)]
    saved = {}
    for i in range(2):
        xin = x
        x, a, u, h = ffn_fwd(xin, vec[i][0], ffn_in[i][0], ffn_out[i][0], 0.5)
        saved[i, 0] = (xin, a, u, h)
        xin = x
        if i == 0:
            x, y, z = pool_fwd(xin, vec[i][1], pw, pvec)
            saved[i, 1] = (xin, y, z)
        else:
            h_m, cq_raw, ckv_raw, cqn, qnope, qcat, kcat, vcat = mla_pre(xin, vec[i][1], mw, tabs)
            olat, lse = attn_fwd(qcat, kcat, vcat)
            x, u_m, ocat = mla_post(olat, xin, vec[i][1], bduv, wo)
            saved[i, 1] = (xin, h_m, cq_raw, ckv_raw, cqn, qnope, qcat, kcat, olat, lse, u_m, ocat)
        xin = x
        if i == 0:
            x, a, u, h = ffn_fwd(xin, vec[i][2], ffn_in[i][1], ffn_out[i][1], 0.5)
        else:
            dx, a, u, h, loss = ffn_fwd(xin, vec[i][2], ffn_in[i][1], ffn_out[i][1], 0.5, target)
        saved[i, 2] = (xin, a, u, h)

    vg = {}
    G = D // 4

    def ffn_grads(i, k, dw_in, dw_out):
        return [(0, 2 * i + k, 4, dw_in.reshape(N_CHIP, 2, D // 2, FSH)),
                (1, 2 * i + k, 4, dw_out.reshape(N_CHIP, 2, DFF // 8, D))]

    piece = lambda t: t[:SUBLANES, :LANES]
    for i in (1, 0):
        xin, a, u, h = saved[i, 2]
        dx, du, act, da, vg[i, 2] = ffn_bwd(dx, xin, u, a, vec[i][2], ffn_in[i][1], ffn_out[i][1], 0.5)
        reducer.advance(after=(piece(dx),))
        reducer.add(f"f{i}1", ffn_grads(i, 1, *ffn_dw(h, da, act, du)))
        if i == 0:
            xin, y, z = saved[i, 1]
            dx, vg[i, 1], pgrad, g_pool = pool_bwd(dx, xin, y, z, vec[i][1], pw, pvec)
        else:
            xin, h_m, cq_raw, ckv_raw, cqn, qnope, qcat, kcat, olat, lse, u_m, ocat = saved[i, 1]
            du, docat, dolat, delta, vg_post = mla_post_bwd(dx, u_m, olat, vec[i][1], bduv, wo)
            reducer.advance()
            dq, dk, dv = attn_bwd(qcat, kcat, kcat.T, dolat, lse, delta)
            reducer.advance(after=(piece(dk),))
            dx, dlat, dql, dqcat, vg_pre, ngrad = mla_pre_bwd(
                dx, dq, dk, dv, xin, cq_raw, ckv_raw, vec[i][1], mw, tabs)
            vg[i, 1] = vg_post + vg_pre
            g = mla_dw(h_m, dlat, cqn, dqcat, dql, qnope, olat, docat, ocat, du)
            slots = lambda a: a.reshape(D, SLOTS, ROPE).sum(axis=1)
            g_kr = slots(g["in"][:, QL + KVL:QL + KVL + LANES]) + _swap_rope(slots(g["in"][:, QL + KVL + LANES:]))
            g_in = jnp.concatenate([g["in"][:, :QL + KVL], g_kr], axis=1)
            g_r = g["q"][:, NOPE_ALL:NOPE_ALL + ROPE_ALL].reshape(QL, N_HEADS, ROPE)
            g_rs = g["q"][:, NOPE_ALL + ROPE_ALL:].reshape(QL, N_HEADS, ROPE)
            g_uq = jnp.concatenate([g["q"][:, :NOPE_ALL].reshape(QL, N_HEADS, NOPE), g_r + _swap_rope(g_rs)], axis=-1)

            def heads(pairs):
                blk = pairs.reshape(N_PAIR, 2, KVL, 2, NOPE)
                per_head = jnp.stack([blk[:, 0, :, 0, :], blk[:, 1, :, 1, :]], axis=1).reshape(N_HEADS, KVL, NOPE)
                return jnp.transpose(per_head, (1, 0, 2)).reshape(KVL, N_HEADS * NOPE)

            reducer.add("mla", [(3, 0, 1, g_in.reshape(N_CHIP, 2, D // 8, QL + KVL + ROPE)),
                                (4, 0, 1, g_uq.reshape(N_CHIP, 2, QL // 8, N_HEADS * (NOPE + ROPE))),
                                (5, 0, 1, g["o"].reshape(N_CHIP, 2, D // 8, D))])
            reducer.add_replicated(jnp.concatenate([heads(g["uk"]), heads(g["uv"])], axis=0))
        xin, a, u, h = saved[i, 0]
        dx, du, act, da, vg[i, 0] = ffn_bwd(dx, xin, u, a, vec[i][0], ffn_in[i][0], ffn_out[i][0], 0.5)
        reducer.advance(after=(piece(dx),))
        grads = ffn_grads(i, 0, *ffn_dw(h, da, act, du))
        if i == 0:
            grads.append((2, 0, 1, g_pool.reshape(N_CHIP, 2, 2 * G // N_CHIP, G)))
        reducer.add(f"f{i}0", grads)
    return loss, dx, vg, pgrad, ngrad


class _GradReducer:
    def __init__(self, core, place, dev):
        self.core, self.place, self.dev = core, place, dev
        self.stacks = {}
        self.live = []
        self.replicated = None

    def add(self, tag, items):
        gen = self._run(tag, items)
        next(gen)
        self.live.append(gen)

    def add_replicated(self, block):
        self.replicated = gather_blocks("gather_ukv", place_block("place_ukv", self.dev, block))

    def advance(self, after=None):
        self.after = after
        live = []
        for gen in self.live:
            try:
                next(gen)
                live.append(gen)
            except StopIteration:
                pass
        self.live = live

    def finish(self):
        while self.live:
            self.advance()
        return self.stacks, self.replicated

    def _run(self, tag, items):
        grads, from_pair = reduce_pair(f"reduce_pair_{tag}", [g for *_, g in items])
        yield
        sums = pair_add(f"pair_add_{tag}", self.core, grads, from_pair)
        sums, from_chips = reduce_chips(f"reduce_chips_{tag}", sums)
        yield
        stacks = chip_add(f"chip_add_{tag}", self.place,
                          [(s, p, k, n_slots, self.stacks.get(o)) for (o, k, n_slots, _), s, p
                           in zip(items, sums, from_chips)], self.after)
        for (o, *_), stack in zip(items, stacks):
            self.stacks[o] = stack
        shared = share_halves(f"share_halves_{tag}", [self.stacks[o] for o, *_ in items], [k for _, k, *_ in items])
        for (o, *_), v in zip(items, shared):
            self.stacks[o] = v


SMALL_IN = 8 * 640
SMALL_GRAD = 8 * 4224
SMALL_W = 8 * 2944


def _pack(parts, total):
    flat = jnp.concatenate([p.reshape(-1) for p in parts])
    return jnp.concatenate([flat, jnp.zeros((total - flat.shape[0],), F32)]).reshape(8, total // 8)


def kernel(x, c, ada_w, ada_b, norm_g, ffn_w_in, ffn_w_out, pool_w, pool_b, pool_scale, mla_w_in, mla_q_norm, mla_kv_norm, mla_w_uq, mla_w_uk, mla_w_uv, mla_w_o, loss_target, m_ada_w, m_ada_b, m_norm_g, m_ffn_w_in, m_ffn_w_out, m_pool_w, m_pool_b, m_pool_scale, m_mla_w_in, m_mla_q_norm, m_mla_kv_norm, m_mla_w_uq, m_mla_w_uk, m_mla_w_uv, m_mla_w_o, v_ada_w, v_ada_b, v_norm_g, v_ffn_w_in, v_ffn_w_out, v_pool_w, v_pool_b, v_pool_scale, v_mla_w_in, v_mla_q_norm, v_mla_kv_norm, v_mla_w_uq, v_mla_w_uk, v_mla_w_uv, v_mla_w_o):
    ix, iy, ic = _place()
    chip = 2 * ix + iy
    dev = 2 * chip + ic
    core_arr = ic.astype(jnp.int32).reshape(1)
    chip_arr = chip.astype(jnp.int32).reshape(1)
    S = x.shape[1]
    G = D // 4
    NG = D // N_CHIP

    def chip_cols(a, width, axis):
        return lax.dynamic_slice_in_dim(a, chip * width, width, axis)

    got = gather_devices("gather_small_in", _pack([c, norm_g, pool_b, mla_q_norm], SMALL_IN)).reshape(N_DEV, SMALL_IN)
    c_all = got[:, :D]
    parts = got[0::2]
    o = D
    norm_g_full = parts[:, o:o + 12 * NG].reshape(N_CHIP, 2, 6, NG).transpose(1, 2, 0, 3).reshape(2, 6, D)
    o += 12 * NG
    pool_b_full = parts[:, o:o + G].reshape(N_CHIP, 4, G // N_CHIP).transpose(1, 0, 2).reshape(1, D)
    o += G
    q_norm_full = parts[:, o:o + QL // N_CHIP].reshape(1, QL)
    pvec = jnp.concatenate([pool_b_full, pool_scale, jnp.zeros((6, D), F32)], axis=0)

    c_pad = jnp.concatenate([c_all, jnp.zeros((8, D), F32)], axis=0)
    mod_loc = mod_fwd(c_pad, ada_w, chip_cols(ada_b, MOD_COLS, 1).reshape(2, 1, MOD_COLS))
    got = gather_devices("gather_mod", mod_loc[:, :8].transpose(1, 0, 2).reshape(8, 2 * MOD_COLS))
    mine = lax.dynamic_index_in_dim(got[0::2].reshape(N_CHIP, 8, 2, MOD_COLS), dev, axis=1, keepdims=False)
    mod = mine.transpose(1, 0, 2).reshape(2, 9, D)

    bf = lambda a: a.astype(BF16)
    w_in_halves = ffn_w_in.reshape(2, 2, 2, D // 2, FSH)
    w_out_halves = ffn_w_out.reshape(2, 2, 2, DFF // 8, D)
    shards = [(w_in_halves, (i, k)) for i in range(2) for k in range(2)]
    shards += [(w_out_halves, (i, k)) for i in range(2) for k in range(2)]
    shards += [(pool_w.reshape(2, 2 * G // N_CHIP, G), ()), (mla_w_in.reshape(2, D // 8, QL + KVL + ROPE), ()),
               (mla_w_uq.reshape(2, QL // 8, N_HEADS * (NOPE + ROPE)), ()), (mla_w_o.reshape(2, D // 8, D), ())]
    full = [None] * len(shards)
    stages = [(0, 4, 8), (1, 5), (2, 6), (9, 10, 11), (3, 7)]
    first, token = cast_into_slots("cast_first", chip_arr, [shards[t] for t in stages[0]])
    slotted = dict(zip(stages[0], first))
    rest = [t for members in stages[1:] for t in members]
    for stage, members in enumerate(stages):
        got_w = gather_weights(f"gather_weights_{stage}", stage, [slotted[t] for t in members])
        for t, a in zip(members, got_w):
            full[t] = a
        if stage == 0:
            slotted.update(zip(rest, cast_into_slots("cast_rest", chip_arr, [shards[t] for t in rest], token)[0]))
    ffn_in, ffn_out, pw, mw, bduv, wo = _unpack_weights(full, bf(mla_w_uk[0]), bf(mla_w_uv[0]), q_norm_full,
                                                        mla_kv_norm)

    place_arr = jnp.stack([chip, ic]).astype(jnp.int32)
    reducer = _GradReducer(core_arr, place_arr, dev.astype(jnp.int32).reshape(1))
    loss_mine, grad_x, vg, pgrad, ngrad = _example_step(
        x[0], loss_target[0], mod, norm_g_full, pvec, ffn_in, ffn_out, pw, mw, bduv, wo, reducer)

    dmod = jnp.stack([jnp.concatenate([vg[i, k][0:3] for k in range(3)]) for i in range(2)])
    dnorm = jnp.stack([jnp.concatenate([vg[i, k][3:5] for k in range(3)]) for i in range(2)])
    small = _pack([dmod, dnorm, pgrad[0], pgrad[1], ngrad[0], ngrad[1, :KVL], loss_mine], SMALL_GRAD)
    got = gather_devices("gather_small_grad", small)
    tot = sum_devices("sum_small_grad", got).reshape(-1)
    n_mod = 2 * 9 * D
    g_ada_b = tot[:n_mod].reshape(ada_b.shape)
    o = n_mod
    g_norm = chip_cols(tot[o:o + 12 * D].reshape(2, 6, D), NG, 2)
    o += 12 * D
    g_pool_b = chip_cols(tot[o:o + D].reshape(1, 4, G), G // N_CHIP, 2)
    o += D
    g_pool_scale = tot[o:o + D].reshape(pool_scale.shape)
    o += D
    g_q_norm = chip_cols(tot[o:o + QL].reshape(1, QL), QL // N_CHIP, 1)
    o += QL
    g_kv_norm = tot[o:o + KVL].reshape(mla_kv_norm.shape)
    loss = tot[o + KVL]
    dmod_all = chip_cols(got.reshape(N_DEV, -1)[:, :n_mod].reshape(N_DEV, 2, 9 * D), MOD_COLS, 2)
    dmod_pad = jnp.concatenate([dmod_all.transpose(1, 0, 2), jnp.zeros((2, 8, MOD_COLS), F32)], axis=1)

    g_ada_w, d_ada_w, nm_ada_w, nv_ada_w = adamw_ada(c_pad, dmod_pad, ada_w, m_ada_w, v_ada_w)
    small_names = ["ada_b", "norm_g", "pool_b", "pool_scale", "mla_q_norm", "mla_kv_norm"]
    small_w = [ada_b, norm_g, pool_b, pool_scale, mla_q_norm, mla_kv_norm]
    small_g = [g_ada_b, g_norm, g_pool_b, g_pool_scale, g_q_norm, g_kv_norm]
    small_m = [m_ada_b, m_norm_g, m_pool_b, m_pool_scale, m_mla_q_norm, m_mla_kv_norm]
    small_v = [v_ada_b, v_norm_g, v_pool_b, v_pool_scale, v_mla_q_norm, v_mla_kv_norm]
    packed = adamw("adamw_small", *[_pack(p, SMALL_W) for p in (small_w, small_g, small_m, small_v)])
    upd = {}
    o = 0
    for name, w in zip(small_names, small_w):
        upd[name] = [p.reshape(-1)[o:o + w.size].reshape(w.shape) for p in packed]
        o += w.size
    upd["ada_w"] = [d_ada_w, nm_ada_w, nv_ada_w]

    reducer.advance(after=(d_ada_w[0, :SUBLANES, :LANES],))
    ffn = [("ffn_w_in", 0, ffn_w_in, m_ffn_w_in, v_ffn_w_in), ("ffn_w_out", 1, ffn_w_out, m_ffn_w_out, v_ffn_w_out)]
    slots = lambda a: a.reshape((4,) + a.shape[2:])
    early = {name: adamw(f"adamw_{name}_early", slots(w), slots(reducer.stacks[o].reshape(w.shape)), slots(m),
                         slots(v), part=(1, 3), copy_grad=True) for name, o, w, m, v in ffn}
    g_mla_in = reducer.stacks[3].reshape(mla_w_in.shape)
    g_uq = reducer.stacks[4].reshape(mla_w_uq.shape)
    g_wo = reducer.stacks[5].reshape(mla_w_o.shape)
    upd["mla_w_in"], upd["mla_w_uq"], upd["mla_w_o"] = adamw_whole(
        "adamw_mla", [(mla_w_in, g_mla_in, m_mla_w_in, v_mla_w_in), (mla_w_uq, g_uq, m_mla_w_uq, v_mla_w_uq),
                      (mla_w_o, g_wo, m_mla_w_o, v_mla_w_o)])

    reducer.advance(after=(early["ffn_w_in"][0][1, :SUBLANES, :LANES], early["ffn_w_out"][0][1, :SUBLANES, :LANES],
                           upd["mla_w_o"][0][0, :SUBLANES, :LANES], upd["mla_w_in"][0][0, :SUBLANES, :LANES]))
    ukv = sum_devices("sum_ukv", reducer.replicated)
    g_uk = ukv[:KVL].reshape(mla_w_uk.shape)
    g_uv = ukv[KVL:].reshape(mla_w_uv.shape)
    upd["mla_w_uk"], upd["mla_w_uv"] = adamw_whole(
        "adamw_ukv", [(mla_w_uk, g_uk, m_mla_w_uk, v_mla_w_uk), (mla_w_uv, g_uv, m_mla_w_uv, v_mla_w_uv)])
    stacks, _ = reducer.finish()
    g_pool_w = stacks[2].reshape(pool_w.shape)
    g_ffn = {}
    for name, o, w, m, v in ffn:
        done = adamw(f"adamw_{name}_last", slots(w), slots(stacks[o].reshape(w.shape)), slots(m), slots(v),
                     part=(0, 1), prev=early[name], copy_grad=True)
        upd[name] = [p.reshape(w.shape) for p in done[:3]]
        g_ffn[name] = done[3].reshape(w.shape)
    g_ffn_in, g_ffn_out = g_ffn["ffn_w_in"], g_ffn["ffn_w_out"]
    upd["pool_w"] = adamw("adamw_pool_w", pool_w, g_pool_w, m_pool_w, v_pool_w)

    order = ["ada_w", "ada_b", "norm_g", "ffn_w_in", "ffn_w_out", "pool_w", "pool_b", "pool_scale", "mla_w_in",
             "mla_q_norm", "mla_kv_norm", "mla_w_uq", "mla_w_uk", "mla_w_uv", "mla_w_o"]
    grad = dict(ada_w=g_ada_w, ada_b=g_ada_b, norm_g=g_norm, ffn_w_in=g_ffn_in, ffn_w_out=g_ffn_out, pool_w=g_pool_w,
                pool_b=g_pool_b, pool_scale=g_pool_scale, mla_w_in=g_mla_in, mla_q_norm=g_q_norm,
                mla_kv_norm=g_kv_norm, mla_w_uq=g_uq, mla_w_uk=g_uk, mla_w_uv=g_uv, mla_w_o=g_wo)
    return (loss, grad_x[None], *[grad[n] for n in order], *[upd[n][0] for n in order],
            *[upd[n][1] for n in order], *[upd[n][2] for n in order])
```
